```python
import math
import jax, jax.numpy as jnp
from jax import lax
import numpy as np

D_MODEL = 1024
BATCH = 8
SEQ = 4096
DEPTH = 1

HEAD_DIM = 64
DIL_GROUPS = ((128, 1), (512, 4), (2048, 16))
DIL_HEADS = 4
DIL_WIDTH = len(DIL_GROUPS) * DIL_HEADS * HEAD_DIM
DIL_OUT = DIL_HEADS * HEAD_DIM
SWA_Q_HEADS = 16
SWA_KV_HEADS = 2
SWA_REP = SWA_Q_HEADS // SWA_KV_HEADS
SWA_WINDOW = 128
SWA_Q_WIDTH = SWA_Q_HEADS * HEAD_DIM
SWA_KV_WIDTH = SWA_KV_HEADS * HEAD_DIM
N_ALIBI_HEADS = SWA_Q_HEADS + len(DIL_GROUPS) * DIL_HEADS
ATTN_BLOCK = 128
SPLIT_SIZES = (DIL_WIDTH, DIL_WIDTH, DIL_WIDTH, SWA_Q_WIDTH, SWA_KV_WIDTH, SWA_KV_WIDTH, D_MODEL, D_MODEL)
SPLIT_POINTS = tuple(sum(SPLIT_SIZES[:i + 1]) for i in range(len(SPLIT_SIZES) - 1))
IN_COLS = sum(SPLIT_SIZES)
N_EXPERTS = 32
TOP_K = 4
D_FF = 1024
SWIGLU_LIMIT = 7.0
SWIGLU_ALPHA = 1.702
MOE_BLOCK = 128
LN_EPS = 1e-5
DEEPNORM_ALPHA = (2 * DEPTH) ** 0.25
DEEPNORM_BETA = (8 * DEPTH) ** -0.25
NEG_INF = -1e30

kernel_name = "hybrid_dilated_swa_sink_moe_deepnorm"


def layer_norm(x, g, b):
    xf = x.astype(jnp.float32)
    mu = jnp.mean(xf, axis=-1, keepdims=True)
    var = jnp.mean(jnp.square(xf - mu), axis=-1, keepdims=True)
    y = (xf - mu) * lax.rsqrt(var + LN_EPS) * g.astype(jnp.float32) + b.astype(jnp.float32)
    return y.astype(x.dtype)


def alibi_slopes():
    h = jnp.arange(1, N_ALIBI_HEADS + 1, dtype=jnp.float32)
    return jnp.exp2(-8.0 * h / N_ALIBI_HEADS)


def banded_attention(q, k, v, slopes, max_diff, dist_scale, sinks):
    n, g, r, L, hd = q.shape
    nb = -(-L // ATTN_BLOCK)
    pad = nb * ATTN_BLOCK - L
    q = jnp.pad(q, ((0, 0), (0, 0), (0, 0), (0, pad), (0, 0)))
    k = jnp.pad(k, ((0, 0), (0, 0), (ATTN_BLOCK, pad), (0, 0)))
    v = jnp.pad(v, ((0, 0), (0, 0), (ATTN_BLOCK, pad), (0, 0)))
    qb = q.reshape(n, g, r, nb, ATTN_BLOCK, hd)
    kb = k.reshape(n, g, nb + 1, ATTN_BLOCK, hd)
    vb = v.reshape(n, g, nb + 1, ATTN_BLOCK, hd)
    kw = jnp.concatenate([kb[:, :, :-1], kb[:, :, 1:]], axis=3)
    vw = jnp.concatenate([vb[:, :, :-1], vb[:, :, 1:]], axis=3)
    s = jnp.einsum('ngrbqd,ngbkd->ngrbqk', qb, kw, preferred_element_type=jnp.float32) * (hd ** -0.5)
    qi = jnp.arange(ATTN_BLOCK)[:, None]
    kj = jnp.arange(2 * ATTN_BLOCK)[None, :]
    diff = qi - kj + ATTN_BLOCK
    kpos = jnp.arange(nb)[:, None, None] * ATTN_BLOCK - ATTN_BLOCK + kj[None]
    valid = (diff >= 0) & (diff <= max_diff) & (kpos >= 0)
    bias = -slopes.astype(jnp.float32)[:, :, None, None, None] * (dist_scale * diff).astype(jnp.float32)
    s = jnp.where(valid, s + bias, NEG_INF)
    m = jnp.max(s, axis=-1)
    if sinks is not None:
        sink = sinks.astype(jnp.float32)[:, :, None, None]
        m = jnp.maximum(m, sink)
    p = jnp.exp(s - m[..., None])
    denom = jnp.sum(p, axis=-1)
    if sinks is not None:
        denom = denom + jnp.exp(sink - m)
    o = jnp.einsum('ngrbqk,ngbkd->ngrbqd', p.astype(v.dtype), vw, preferred_element_type=jnp.float32)
    o = o / denom[..., None]
    lse = m + jnp.log(denom)
    o = o.reshape(n, g, r, nb * ATTN_BLOCK, hd)[:, :, :, :L]
    lse = lse.reshape(n, g, r, nb * ATTN_BLOCK)[:, :, :, :L]
    return o.astype(q.dtype), lse


def dilated_attention(q, k, v, slopes_a):
    b, S, _ = q.shape
    q = q.reshape(b, S, len(DIL_GROUPS), DIL_HEADS, HEAD_DIM)
    k = k.reshape(b, S, len(DIL_GROUPS), DIL_HEADS, HEAD_DIM)
    v = v.reshape(b, S, len(DIL_GROUPS), DIL_HEADS, HEAD_DIM)
    outs, lses = [], []
    for gi, (window, dil) in enumerate(DIL_GROUPS):
        L = S // dil

        def to_sub(t):
            t = t[:, :, gi].reshape(b, L, dil, DIL_HEADS, HEAD_DIM)
            return t.transpose(0, 2, 3, 1, 4).reshape(b * dil, DIL_HEADS, L, HEAD_DIM)

        qs, ks, vs = to_sub(q), to_sub(k), to_sub(v)
        o, lse = banded_attention(qs[:, :, None], ks, vs, slopes_a[gi][:, None],
                                  max_diff=window // dil, dist_scale=dil, sinks=None)
        o = o[:, :, 0].reshape(b, dil, DIL_HEADS, L, HEAD_DIM).transpose(0, 3, 1, 2, 4).reshape(b, S, DIL_HEADS, HEAD_DIM)
        lse = lse[:, :, 0].reshape(b, dil, DIL_HEADS, L).transpose(0, 3, 1, 2).reshape(b, S, DIL_HEADS)
        outs.append(o)
        lses.append(lse)
    w = jax.nn.softmax(jnp.stack(lses, axis=0), axis=0)
    o = jnp.einsum('gbsh,gbshd->bshd', w, jnp.stack(outs, axis=0).astype(jnp.float32))
    return o.reshape(b, S, DIL_OUT).astype(q.dtype)


def swa_sink_attention(q, k, v, slopes_b, sinks):
    b, S, _ = q.shape
    q = q.reshape(b, S, SWA_KV_HEADS, SWA_REP, HEAD_DIM).transpose(0, 2, 3, 1, 4)
    k = k.reshape(b, S, SWA_KV_HEADS, HEAD_DIM).transpose(0, 2, 1, 3)
    v = v.reshape(b, S, SWA_KV_HEADS, HEAD_DIM).transpose(0, 2, 1, 3)
    o, _ = banded_attention(q, k, v, slopes_b, max_diff=SWA_WINDOW - 1, dist_scale=1, sinks=sinks)
    return o.transpose(0, 3, 1, 2, 4).reshape(b, S, SWA_Q_WIDTH)


def token_mixer(x, w_in, sinks, w_proj_a, w_proj_b, w_out, slopes):
    h = jnp.einsum('bsd,dc->bsc', x, w_in)
    a_q, a_k, a_v, b_q, b_k, b_v, g_a, g_b = jnp.split(h, SPLIT_POINTS, axis=-1)
    slopes_b = slopes[:SWA_Q_HEADS].reshape(SWA_KV_HEADS, SWA_REP)
    slopes_a = slopes[SWA_Q_HEADS:].reshape(len(DIL_GROUPS), DIL_HEADS)
    out_a = dilated_attention(a_q, a_k, a_v, slopes_a)
    out_b = swa_sink_attention(b_q, b_k, b_v, slopes_b, sinks.reshape(SWA_KV_HEADS, SWA_REP))
    merged = (jax.nn.sigmoid(g_a) * jnp.einsum('bsc,cd->bsd', out_a, w_proj_a)
              + jax.nn.sigmoid(g_b) * jnp.einsum('bsc,cd->bsd', out_b, w_proj_b))
    return jnp.einsum('bsd,de->bse', merged, w_out)


def moe_ffn(y, router_w, router_b, w_gate, b_gate, w_up, b_up, w_down, b_down):
    b, S, D = y.shape
    n_tok = b * S
    yf = y.reshape(n_tok, D)
    logits = (yf @ router_w + router_b).astype(jnp.float32)
    top_val, top_idx = lax.top_k(logits, TOP_K)
    gates = jax.nn.softmax(top_val, axis=-1)
    n_assign = n_tok * TOP_K
    flat_e = top_idx.reshape(-1)
    order = jnp.argsort(flat_e)
    sorted_e = flat_e[order]
    tok = order // TOP_K
    counts = jnp.zeros((N_EXPERTS,), jnp.int32).at[flat_e].add(1)
    padded = (counts + MOE_BLOCK - 1) // MOE_BLOCK * MOE_BLOCK
    p_end = jnp.cumsum(padded)
    p_start = p_end - padded
    u_start = jnp.cumsum(counts) - counts
    dest = p_start[sorted_e] + jnp.arange(n_assign, dtype=jnp.int32) - u_start[sorted_e]
    n_blocks = -(-n_assign // MOE_BLOCK) + N_EXPERTS
    rows = n_blocks * MOE_BLOCK
    xs = jnp.zeros((rows, D), y.dtype).at[dest].set(yf[tok])
    block_e = jnp.minimum(jnp.searchsorted(p_end, jnp.arange(n_blocks, dtype=jnp.int32) * MOE_BLOCK, side='right'),
                          N_EXPERTS - 1).astype(jnp.int32)

    def expert_block(args):
        xb, e = args
        gt = xb @ w_gate[e] + b_gate[e]
        up = xb @ w_up[e] + b_up[e]
        gt = jnp.minimum(gt, SWIGLU_LIMIT)
        up = jnp.clip(up, -SWIGLU_LIMIT, SWIGLU_LIMIT)
        hdn = gt * jax.nn.sigmoid(SWIGLU_ALPHA * gt) * (up + 1.0)
        return hdn @ w_down[e] + b_down[e]

    ys = lax.map(expert_block, (xs.reshape(n_blocks, MOE_BLOCK, D), block_e)).reshape(rows, D)
    contrib = ys[dest] * gates.reshape(-1)[order][:, None].astype(ys.dtype)
    out = jax.ops.segment_sum(contrib, tok, num_segments=n_tok)
    return out.reshape(b, S, D)


def setup_inputs(seed: int = 0) -> dict:
    key = jax.random.key(seed)
    ks = jax.random.split(key, 20)
    f32 = jnp.float32
    beta = DEEPNORM_BETA
    col_scale = jnp.concatenate([
        jnp.ones((2 * DIL_WIDTH,), f32), jnp.full((DIL_WIDTH,), beta, f32),
        jnp.ones((SWA_Q_WIDTH + SWA_KV_WIDTH,), f32), jnp.full((SWA_KV_WIDTH,), beta, f32),
        jnp.ones((2 * D_MODEL,), f32)])
    nrm = lambda k, shape: jax.random.normal(k, shape, f32)
    return {
        "x": nrm(ks[0], (BATCH, SEQ, D_MODEL)),
        "w_in": nrm(ks[1], (DEPTH, D_MODEL, IN_COLS)) * (D_MODEL ** -0.5) * col_scale,
        "sinks": 0.5 * nrm(ks[2], (DEPTH, SWA_Q_HEADS)),
        "w_proj_a": nrm(ks[3], (DEPTH, DIL_OUT, D_MODEL)) * (DIL_OUT ** -0.5),
        "w_proj_b": nrm(ks[4], (DEPTH, SWA_Q_WIDTH, D_MODEL)) * (SWA_Q_WIDTH ** -0.5),
        "w_out": nrm(ks[5], (DEPTH, D_MODEL, D_MODEL)) * (D_MODEL ** -0.5) * beta,
        "ln1_g": 1.0 + 0.05 * nrm(ks[6], (DEPTH, D_MODEL)),
        "ln1_b": 0.02 * nrm(ks[7], (DEPTH, D_MODEL)),
        "router_w": nrm(ks[8], (DEPTH, D_MODEL, N_EXPERTS)) * (D_MODEL ** -0.5),
        "router_b": 0.01 * nrm(ks[9], (DEPTH, N_EXPERTS)),
        "w_gate": nrm(ks[10], (DEPTH, N_EXPERTS, D_MODEL, D_FF)) * (D_MODEL ** -0.5),
        "b_gate": 0.02 * nrm(ks[11], (DEPTH, N_EXPERTS, D_FF)),
        "w_up": nrm(ks[12], (DEPTH, N_EXPERTS, D_MODEL, D_FF)) * (D_MODEL ** -0.5),
        "b_up": 0.02 * nrm(ks[13], (DEPTH, N_EXPERTS, D_FF)),
        "w_down": nrm(ks[14], (DEPTH, N_EXPERTS, D_FF, D_MODEL)) * (D_FF ** -0.5) * beta,
        "b_down": 0.02 * nrm(ks[15], (DEPTH, N_EXPERTS, D_MODEL)),
        "ln2_g": 1.0 + 0.05 * nrm(ks[16], (DEPTH, D_MODEL)),
        "ln2_b": 0.02 * nrm(ks[17], (DEPTH, D_MODEL)),
    }


def reference(x, w_in, sinks, w_proj_a, w_proj_b, w_out, ln1_g, ln1_b, router_w, router_b,
              w_gate, b_gate, w_up, b_up, w_down, b_down, ln2_g, ln2_b):
    slopes = alibi_slopes()
    for l in range(DEPTH):
        mix = token_mixer(x, w_in[l], sinks[l], w_proj_a[l], w_proj_b[l], w_out[l], slopes)
        x = layer_norm(DEEPNORM_ALPHA * x + mix, ln1_g[l], ln1_b[l])
        ffn = moe_ffn(x, router_w[l], router_b[l], w_gate[l], b_gate[l], w_up[l], b_up[l], w_down[l], b_down[l])
        x = layer_norm(DEEPNORM_ALPHA * x + ffn, ln2_g[l], ln2_b[l])
    return x
```

```python
import functools

import jax
import jax.numpy as jnp
from jax import lax
from jax.experimental import pallas as pl
from jax.experimental.pallas import tpu as pltpu

F32 = jnp.float32
BF16 = jnp.bfloat16
I32 = jnp.int32

HEAD_DIM = 64
DIL_GROUPS = ((128, 1), (512, 4), (2048, 16))
DIL_HEADS = 4
N_DIL = len(DIL_GROUPS)
DIL_OUT = DIL_HEADS * HEAD_DIM
DIL_WIDTH = N_DIL * DIL_OUT
SWA_Q_HEADS = 16
SWA_KV_HEADS = 2
SWA_REP = SWA_Q_HEADS // SWA_KV_HEADS
SWA_WINDOW = 128
SWA_Q_WIDTH = SWA_Q_HEADS * HEAD_DIM
SWA_KV_WIDTH = SWA_KV_HEADS * HEAD_DIM
N_ALIBI_HEADS = SWA_Q_HEADS + N_DIL * DIL_HEADS
ATTN_BLOCK = 128
N_EXPERTS = 32
TOP_K = 4
SWIGLU_LIMIT = 7.0
SWIGLU_ALPHA = 1.702
LN_EPS = 1e-5
DEPTH = 1
DEEPNORM_ALPHA = (2 * DEPTH) ** 0.25
NEG_INF = -1e30

LANES = 128
VMEM_LIMIT_BYTES = 56 * 1024 * 1024

A_QKV_W = 3 * DIL_WIDTH
B_Q_OFF = A_QKV_W
B_KV_OFF = B_Q_OFF + SWA_Q_WIDTH
GATE_OFF = B_KV_OFF + 2 * SWA_KV_WIDTH


def _cparams(*sem):
    return pltpu.CompilerParams(dimension_semantics=sem, vmem_limit_bytes=VMEM_LIMIT_BYTES)


def _in_proj_kernel(x_ref, w_ref, ha_ref, hq_ref, hkv_ref, g_ref, *, d_model):
    xb = x_ref[...].astype(BF16)
    segments = ((ha_ref, 0, A_QKV_W), (hq_ref, B_Q_OFF, SWA_Q_WIDTH),
                (hkv_ref, B_KV_OFF, 2 * SWA_KV_WIDTH), (g_ref, GATE_OFF, 2 * d_model))
    for out_ref, col0, width in segments:
        for c in range(0, width, 512):
            w = min(512, width - c)
            r = jnp.dot(xb, w_ref[:, col0 + c:col0 + c + w], preferred_element_type=F32)
            out_ref[:, c:c + w] = r.astype(out_ref.dtype)


def _in_proj(x2, w_in_bf, tm=512):
    n, d = x2.shape
    cols = w_in_bf.shape[1]
    row = lambda i: (i, 0)
    return pl.pallas_call(
        functools.partial(_in_proj_kernel, d_model=d),
        grid=(n // tm,),
        in_specs=[pl.BlockSpec((tm, d), row),
                  pl.BlockSpec((d, cols), lambda i: (0, 0), pipeline_mode=pl.Buffered(1))],
        out_specs=[pl.BlockSpec((tm, A_QKV_W), row), pl.BlockSpec((tm, SWA_Q_WIDTH), row),
                   pl.BlockSpec((tm, 2 * SWA_KV_WIDTH), row), pl.BlockSpec((tm, 2 * d), row)],
        out_shape=[jax.ShapeDtypeStruct((n, A_QKV_W), BF16), jax.ShapeDtypeStruct((n, SWA_Q_WIDTH), BF16),
                   jax.ShapeDtypeStruct((n, 2 * SWA_KV_WIDTH), BF16), jax.ShapeDtypeStruct((n, 2 * d), BF16)],
        compiler_params=_cparams("parallel"),
        name="in_proj",
    )(x2, w_in_bf)


def _band_tables(max_diff):
    qi = lax.broadcasted_iota(I32, (ATTN_BLOCK, ATTN_BLOCK), 0)
    kj = lax.broadcasted_iota(I32, (ATTN_BLOCK, ATTN_BLOCK), 1)
    diff_prev = qi - kj + ATTN_BLOCK
    diff_cur = qi - kj
    valid_prev = diff_prev <= max_diff
    valid_cur = (diff_cur >= 0) & (diff_cur <= max_diff)
    return diff_prev.astype(F32), diff_cur.astype(F32), valid_prev, valid_cur


def _attend(q, k_prev, k_cur, v_prev, v_cur, tables, prev_ok, slope_scaled, sink):
    diff_prev, diff_cur, valid_prev, valid_cur = tables
    nt = (((1,), (1,)), ((), ()))
    s_prev = lax.dot_general(q, k_prev, nt, preferred_element_type=F32)
    s_cur = lax.dot_general(q, k_cur, nt, preferred_element_type=F32)
    s_prev = jnp.where(valid_prev & prev_ok, s_prev - slope_scaled * diff_prev, NEG_INF)
    s_cur = jnp.where(valid_cur, s_cur - slope_scaled * diff_cur, NEG_INF)
    m = jnp.max(jnp.maximum(s_prev, s_cur), axis=-1, keepdims=True)
    if sink is not None:
        m = jnp.maximum(m, sink)
    p_prev = jnp.exp(s_prev - m)
    p_cur = jnp.exp(s_cur - m)
    denom = jnp.sum(p_prev + p_cur, axis=-1, keepdims=True)
    if sink is not None:
        denom = denom + jnp.exp(sink - m)
    o = (jnp.dot(p_prev.astype(BF16), v_prev, preferred_element_type=F32)
         + jnp.dot(p_cur.astype(BF16), v_cur, preferred_element_type=F32))
    return o / denom, m + jnp.log(denom)


def _dil_attn_kernel(slope_ref, q_ref, kc_ref, kp_ref, vc_ref, vp_ref, o_ref, lse_ref, *, tq, max_diff, dist_scale):
    first = pl.program_id(2) == 0
    tables = _band_tables(max_diff)
    for i in range(tq // ATTN_BLOCK):
        rows = slice(i * ATTN_BLOCK, (i + 1) * ATTN_BLOCK)
        prev_rows = slice((i - 1) * ATTN_BLOCK, i * ATTN_BLOCK)
        prev_ok = jnp.logical_not(first) if i == 0 else True
        for h in range(DIL_HEADS):
            cols = slice(h * HEAD_DIM, (h + 1) * HEAD_DIM)
            q = q_ref[rows, cols] * 0.125
            k_prev = kp_ref[:, cols] if i == 0 else kc_ref[prev_rows, cols]
            v_prev = vp_ref[:, cols] if i == 0 else vc_ref[prev_rows, cols]
            o, lse = _attend(q, k_prev, kc_ref[rows, cols], v_prev, vc_ref[rows, cols], tables,
                             prev_ok, slope_ref[h] * dist_scale, None)
            o_ref[rows, cols] = o.astype(o_ref.dtype)
            lse_ref[rows, cols] = jnp.broadcast_to(lse, (ATTN_BLOCK, HEAD_DIM))


def _dil_attention(ha, slopes_g, g, bsz, seq):
    window, dil = DIL_GROUPS[g]
    sub_len = seq // dil
    tq = min(512, sub_len)
    nqb = tq // ATTN_BLOCK
    ha3 = ha.reshape(bsz, sub_len, dil * A_QKV_W)
    ncol = A_QKV_W // DIL_OUT
    cur = lambda off: (lambda b, r, m: (b, m, r * ncol + off + g))
    prev = lambda off: (lambda b, r, m: (b, jnp.maximum(m * nqb - 1, 0), r * ncol + off + g))
    blk = lambda rows, imap: pl.BlockSpec((None, rows, DIL_OUT), imap)
    out_map = lambda b, r, m: (b, m, r)
    o, lse = pl.pallas_call(
        functools.partial(_dil_attn_kernel, tq=tq, max_diff=window // dil, dist_scale=float(dil)),
        grid=(bsz, dil, sub_len // tq),
        in_specs=[pl.BlockSpec(memory_space=pltpu.SMEM),
                  blk(tq, cur(0)), blk(tq, cur(3)), blk(ATTN_BLOCK, prev(3)),
                  blk(tq, cur(6)), blk(ATTN_BLOCK, prev(6))],
        out_specs=[blk(tq, out_map), blk(tq, out_map)],
        out_shape=[jax.ShapeDtypeStruct((bsz, sub_len, dil * DIL_OUT), BF16),
                   jax.ShapeDtypeStruct((bsz, sub_len, dil * DIL_OUT), F32)],
        compiler_params=_cparams("parallel", "parallel", "arbitrary"),
        name=f"dil_attn_g{g}",
    )(slopes_g, ha3, ha3, ha3, ha3, ha3)
    return o.reshape(bsz * seq, DIL_OUT), lse.reshape(bsz * seq, DIL_OUT)


def _swa_attn_kernel(slope_ref, sink_ref, q_ref, kvc_ref, kvp_ref, o_ref, *, tq):
    first = pl.program_id(1) == 0
    tables = _band_tables(SWA_WINDOW - 1)
    for i in range(tq // ATTN_BLOCK):
        rows = slice(i * ATTN_BLOCK, (i + 1) * ATTN_BLOCK)
        prev_rows = slice((i - 1) * ATTN_BLOCK, i * ATTN_BLOCK)
        prev_ok = jnp.logical_not(first) if i == 0 else True
        for h in range(SWA_Q_HEADS):
            kvh = h // SWA_REP
            kcols = slice(kvh * HEAD_DIM, (kvh + 1) * HEAD_DIM)
            vcols = slice(SWA_KV_WIDTH + kvh * HEAD_DIM, SWA_KV_WIDTH + (kvh + 1) * HEAD_DIM)
            cols = slice(h * HEAD_DIM, (h + 1) * HEAD_DIM)
            q = q_ref[rows, cols] * 0.125
            k_prev = kvp_ref[:, kcols] if i == 0 else kvc_ref[prev_rows, kcols]
            v_prev = kvp_ref[:, vcols] if i == 0 else kvc_ref[prev_rows, vcols]
            o, _ = _attend(q, k_prev, kvc_ref[rows, kcols], v_prev, kvc_ref[rows, vcols], tables,
                           prev_ok, slope_ref[h], sink_ref[h])
            o_ref[rows, cols] = o.astype(o_ref.dtype)


def _swa_attention(hq, hkv, slopes_b, sinks, bsz, seq, tq=512):
    nqb = tq // ATTN_BLOCK
    hq3 = hq.reshape(bsz, seq, SWA_Q_WIDTH)
    hkv3 = hkv.reshape(bsz, seq, 2 * SWA_KV_WIDTH)
    smem = pl.BlockSpec(memory_space=pltpu.SMEM)
    out = pl.pallas_call(
        functools.partial(_swa_attn_kernel, tq=tq),
        grid=(bsz, seq // tq),
        in_specs=[smem, smem,
                  pl.BlockSpec((None, tq, SWA_Q_WIDTH), lambda b, m: (b, m, 0)),
                  pl.BlockSpec((None, tq, 2 * SWA_KV_WIDTH), lambda b, m: (b, m, 0)),
                  pl.BlockSpec((None, ATTN_BLOCK, 2 * SWA_KV_WIDTH),
                               lambda b, m: (b, jnp.maximum(m * nqb - 1, 0), 0))],
        out_specs=pl.BlockSpec((None, tq, SWA_Q_WIDTH), lambda b, m: (b, m, 0)),
        out_shape=jax.ShapeDtypeStruct((bsz, seq, SWA_Q_WIDTH), BF16),
        compiler_params=_cparams("parallel", "arbitrary"),
        name="swa_attn",
    )(slopes_b, sinks, hq3, hkv3, hkv3)
    return out.reshape(bsz * seq, SWA_Q_WIDTH)


def _layer_norm(z, g, b):
    mu = jnp.mean(z, axis=-1, keepdims=True)
    zc = z - mu
    var = jnp.mean(zc * zc, axis=-1, keepdims=True)
    return zc * lax.rsqrt(var + LN_EPS) * g + b


def _mix_out_kernel(x_ref, o0_ref, o1_ref, o2_ref, l0_ref, l1_ref, l2_ref, ob_ref, g_ref,
                    wpa_ref, wpb_ref, wo_ref, lng_ref, lnb_ref, rw_ref, rb_ref,
                    y_ref, idx_ref, gate_ref, *, d_model):
    l0, l1, l2 = l0_ref[...], l1_ref[...], l2_ref[...]
    lm = jnp.maximum(jnp.maximum(l0, l1), l2)
    e0, e1, e2 = jnp.exp(l0 - lm), jnp.exp(l1 - lm), jnp.exp(l2 - lm)
    esum = e0 + e1 + e2
    out_a = (e0 / esum) * o0_ref[...].astype(F32) + (e1 / esum) * o1_ref[...].astype(F32) \
        + (e2 / esum) * o2_ref[...].astype(F32)
    pa = jnp.dot(out_a.astype(BF16), wpa_ref[...], preferred_element_type=F32)
    pb = jnp.dot(ob_ref[...], wpb_ref[...], preferred_element_type=F32)
    ga = jax.nn.sigmoid(g_ref[:, :d_model].astype(F32))
    gb = jax.nn.sigmoid(g_ref[:, d_model:].astype(F32))
    merged = ga * pa + gb * pb
    mix = jnp.dot(merged.astype(BF16), wo_ref[...], preferred_element_type=F32)
    y = _layer_norm(DEEPNORM_ALPHA * x_ref[...] + mix, lng_ref[...], lnb_ref[...])
    y_ref[...] = y

    logits = jnp.dot(y.astype(BF16), rw_ref[...], preferred_element_type=F32) + rb_ref[...]
    lane = lax.broadcasted_iota(I32, logits.shape, 1)
    logits = jnp.where(lane < N_EXPERTS, logits, -jnp.inf)
    idx_out = jnp.zeros(logits.shape, I32)
    val_out = jnp.full(logits.shape, -jnp.inf, F32)
    for k in range(TOP_K):
        top = jnp.max(logits, axis=-1, keepdims=True)
        top_idx = jnp.min(jnp.where(logits == top, lane, LANES), axis=-1, keepdims=True)
        idx_out = jnp.where(lane == k, top_idx, idx_out)
        val_out = jnp.where(lane == k, top, val_out)
        logits = jnp.where(lane == top_idx, -jnp.inf, logits)
    ev = jnp.exp(val_out - jnp.max(val_out, axis=-1, keepdims=True))
    gates = ev / jnp.sum(ev, axis=-1, keepdims=True)
    idx_ref[...] = idx_out[:, :TOP_K]
    gate_ref[...] = gates[:, :TOP_K]


def _mix_out(x2, o_g, lse_g, out_b, gates_h, wpa, wpb, wo, ln_g, ln_b, rw_pad, rb_pad, tm=256):
    n, d = x2.shape
    row = lambda i: (i, 0)
    const = lambda i: (0, 0)
    rb = lambda w: pl.BlockSpec((tm, w), row)
    full = lambda a: pl.BlockSpec(a.shape, const)
    return pl.pallas_call(
        functools.partial(_mix_out_kernel, d_model=d),
        grid=(n // tm,),
        in_specs=[rb(d), rb(DIL_OUT), rb(DIL_OUT), rb(DIL_OUT), rb(DIL_OUT), rb(DIL_OUT), rb(DIL_OUT),
                  rb(SWA_Q_WIDTH), rb(2 * d),
                  full(wpa), full(wpb), full(wo), full(ln_g), full(ln_b), full(rw_pad), full(rb_pad)],
        out_specs=[rb(d), rb(TOP_K), rb(TOP_K)],
        out_shape=[jax.ShapeDtypeStruct((n, d), F32),
                   jax.ShapeDtypeStruct((n, TOP_K), I32), jax.ShapeDtypeStruct((n, TOP_K), F32)],
        compiler_params=_cparams("parallel"),
        name="mix_out",
    )(x2, *o_g, *lse_g, out_b, gates_h, wpa, wpb, wo, ln_g, ln_b, rw_pad, rb_pad)


def _token_mixer_ln1(x2, w_in, sinks, w_proj_a, w_proj_b, w_out, ln1_g, ln1_b, router_w, router_b, bsz, seq):
    n, d = x2.shape
    heads = jnp.arange(1, N_ALIBI_HEADS + 1, dtype=F32)
    slopes = jnp.exp2(-8.0 * heads / N_ALIBI_HEADS)
    ha, hq, hkv, gates_h = _in_proj(x2, w_in.astype(BF16))
    o_g, lse_g = [], []
    for g in range(N_DIL):
        sl = slopes[SWA_Q_HEADS + g * DIL_HEADS:SWA_Q_HEADS + (g + 1) * DIL_HEADS]
        o, lse = _dil_attention(ha, sl, g, bsz, seq)
        o_g.append(o)
        lse_g.append(lse)
    out_b = _swa_attention(hq, hkv, slopes[:SWA_Q_HEADS], sinks.astype(F32), bsz, seq)
    rw_pad = jnp.zeros((d, LANES), BF16).at[:, :N_EXPERTS].set(router_w.astype(BF16))
    rb_pad = jnp.zeros((1, LANES), F32).at[0, :N_EXPERTS].set(router_b.astype(F32))
    return _mix_out(x2, o_g, lse_g, out_b, gates_h, w_proj_a.astype(BF16), w_proj_b.astype(BF16),
                    w_out.astype(BF16), ln1_g.reshape(1, d).astype(F32), ln1_b.reshape(1, d).astype(F32),
                    rw_pad, rb_pad)


def _route_kernel(idx_ref, dest_ref, cnt_ref, carry_ref, pstart_ref, *, expert_block):
    phase, i = pl.program_id(0), pl.program_id(1)
    idx = idx_ref[...]
    t = idx.shape[0]
    lane = lax.broadcasted_iota(I32, (t, LANES), 1)
    onehot = [lane == idx[:, k:k + 1] for k in range(TOP_K)]
    multi = sum(oh.astype(F32) for oh in onehot)
    tile_cnt = jnp.sum(multi, axis=0, keepdims=True).astype(I32)

    @pl.when((phase == 0) & (i == 0))
    def _():
        carry_ref[...] = jnp.zeros_like(carry_ref)

    @pl.when((phase == 1) & (i == 0))
    def _():
        counts = carry_ref[...]
        padded = (counts + (expert_block - 1)) & (-expert_block)
        lane8 = lax.broadcasted_iota(I32, counts.shape, 1)
        incl = padded
        for sh in (1, 2, 4, 8, 16):
            incl = incl + jnp.where(lane8 >= sh, pltpu.roll(incl, sh, axis=1), 0)
        pstart_ref[...] = incl - padded
        cnt_ref[...] = counts
        carry_ref[...] = jnp.zeros_like(carry_ref)

    @pl.when(phase == 1)
    def _():
        r = lax.broadcasted_iota(I32, (t, t), 0)
        c = lax.broadcasted_iota(I32, (t, t), 1)
        earlier = (c < r).astype(BF16)
        cum = jnp.dot(earlier, multi.astype(BF16), preferred_element_type=F32).astype(I32)
        base = cum + carry_ref[0:1, :] + pstart_ref[0:1, :]
        dest = jnp.zeros((t, LANES), I32)
        for k in range(TOP_K):
            d_k = jnp.sum(jnp.where(onehot[k], base, 0), axis=-1, keepdims=True)
            dest = jnp.where(lane == k, d_k, dest)
        dest_ref[...] = dest[:, :TOP_K]

    carry_ref[...] = carry_ref[...] + tile_cnt


def _route(idx, expert_block, tile=512):
    n = idx.shape[0]
    tile = min(tile, n)
    return pl.pallas_call(
        functools.partial(_route_kernel, expert_block=expert_block),
        grid=(2, n // tile),
        in_specs=[pl.BlockSpec((tile, TOP_K), lambda p, i: (i, 0))],
        out_specs=[pl.BlockSpec((tile, TOP_K), lambda p, i: (i * p, 0)),
                   pl.BlockSpec((8, LANES), lambda p, i: (0, 0))],
        out_shape=[jax.ShapeDtypeStruct((n, TOP_K), I32), jax.ShapeDtypeStruct((8, LANES), I32)],
        scratch_shapes=[pltpu.VMEM((8, LANES), I32), pltpu.VMEM((8, LANES), I32)],
        compiler_params=_cparams("arbitrary", "arbitrary"),
        name="route",
    )(idx)


def _row_copy(src_hbm, src_row, dst, dst_row, sem):
    return pltpu.make_async_copy(src_hbm.at[pl.ds(src_row, 1)], dst.at[pl.ds(dst_row, 1)], sem)


def _dispatch_kernel(dest_ref, y_hbm, xs_in_hbm, xs_hbm, sem, *, tile):
    del xs_in_hbm
    base = pl.program_id(0) * tile

    def issue(t, carry):
        for k in range(TOP_K):
            _row_copy(y_hbm, base + t, xs_hbm, dest_ref[t * TOP_K + k], sem).start()
        return carry

    lax.fori_loop(0, tile, issue, 0)

    def drain(t, carry):
        _row_copy(y_hbm, 0, xs_hbm, 0, sem).wait()
        return carry

    lax.fori_loop(0, tile * TOP_K, drain, 0)


def _dispatch(dest_flat, y, rows, tile=512):
    n, d = y.shape
    tile = min(tile, n)
    any_spec = pl.BlockSpec(memory_space=pl.ANY)
    return pl.pallas_call(
        functools.partial(_dispatch_kernel, tile=tile),
        grid=(n // tile,),
        in_specs=[pl.BlockSpec((tile * TOP_K,), lambda i: (i,), memory_space=pltpu.SMEM), any_spec, any_spec],
        out_specs=any_spec,
        out_shape=jax.ShapeDtypeStruct((rows, d), y.dtype),
        scratch_shapes=[pltpu.SemaphoreType.DMA(())],
        input_output_aliases={2: 0},
        compiler_params=_cparams("arbitrary"),
        name="dispatch",
    )(dest_flat, y, jnp.zeros((rows, d), y.dtype))


def _expert_kernel(be_ref, nu_ref, x_ref, wg_ref, bg_ref, wu_ref, bu_ref, wd_ref, bd_ref, o_ref):
    del be_ref
    j = pl.program_id(0)

    @pl.when(j < nu_ref[0])
    def _():
        xb = x_ref[...].astype(BF16)
        gt = jnp.dot(xb, wg_ref[...], preferred_element_type=F32) + bg_ref[...]
        up = jnp.dot(xb, wu_ref[...], preferred_element_type=F32) + bu_ref[...]
        gt = jnp.minimum(gt, SWIGLU_LIMIT)
        up = jnp.clip(up, -SWIGLU_LIMIT, SWIGLU_LIMIT)
        hdn = gt * jax.nn.sigmoid(SWIGLU_ALPHA * gt) * (up + 1.0)
        o_ref[...] = jnp.dot(hdn.astype(BF16), wd_ref[...], preferred_element_type=F32) + bd_ref[...]

    @pl.when(j >= nu_ref[0])
    def _():
        o_ref[...] = jnp.zeros_like(o_ref)


def _experts(block_e, n_used, xs, wg, bg, wu, bu, wd, bd, expert_block):
    rows, d = xs.shape
    e, _, f = wg.shape
    xmap = lambda j, be, nu: (jnp.minimum(j, nu[0] - 1), 0)
    wmap = lambda j, be, nu: (be[j], 0, 0)
    grid_spec = pltpu.PrefetchScalarGridSpec(
        num_scalar_prefetch=2,
        grid=(rows // expert_block,),
        in_specs=[pl.BlockSpec((expert_block, d), xmap),
                  pl.BlockSpec((None, d, f), wmap), pl.BlockSpec((None, 1, f), wmap),
                  pl.BlockSpec((None, d, f), wmap), pl.BlockSpec((None, 1, f), wmap),
                  pl.BlockSpec((None, f, d), wmap), pl.BlockSpec((None, 1, d), wmap)],
        out_specs=pl.BlockSpec((expert_block, d), lambda j, be, nu: (j, 0)),
    )
    return pl.pallas_call(
        _expert_kernel,
        grid_spec=grid_spec,
        out_shape=jax.ShapeDtypeStruct((rows, d), F32),
        compiler_params=_cparams("arbitrary"),
        name="experts",
    )(block_e, n_used, xs, wg, bg.reshape(e, 1, f), wu, bu.reshape(e, 1, f), wd, bd.reshape(e, 1, d))


def _combine_kernel(dest_ref, gate_ref, y_ref, lng_ref, lnb_ref, ys_hbm, o_ref, buf_ref, sem, *, tile):
    def issue(t, carry):
        for k in range(TOP_K):
            _row_copy(ys_hbm, dest_ref[t * TOP_K + k], buf_ref.at[k], t, sem).start()
        return carry

    lax.fori_loop(0, tile, issue, 0)

    def drain(t, carry):
        _row_copy(ys_hbm, 0, buf_ref.at[0], 0, sem).wait()
        return carry

    lax.fori_loop(0, tile * TOP_K, drain, 0)

    gates = gate_ref[...]
    ffn = gates[:, 0:1] * buf_ref[0]
    for k in range(1, TOP_K):
        ffn = ffn + gates[:, k:k + 1] * buf_ref[k]
    o_ref[...] = _layer_norm(DEEPNORM_ALPHA * y_ref[...] + ffn, lng_ref[...], lnb_ref[...])


def _combine(dest_flat, gates, y, ln_g, ln_b, ys, tile=256):
    n, d = y.shape
    tile = min(tile, n)
    row = lambda i: (i, 0)
    const = lambda i: (0, 0)
    return pl.pallas_call(
        functools.partial(_combine_kernel, tile=tile),
        grid=(n // tile,),
        in_specs=[pl.BlockSpec((tile * TOP_K,), lambda i: (i,), memory_space=pltpu.SMEM),
                  pl.BlockSpec((tile, TOP_K), row), pl.BlockSpec((tile, d), row),
                  pl.BlockSpec((1, d), const), pl.BlockSpec((1, d), const),
                  pl.BlockSpec(memory_space=pl.ANY)],
        out_specs=pl.BlockSpec((tile, d), row),
        out_shape=jax.ShapeDtypeStruct((n, d), F32),
        scratch_shapes=[pltpu.VMEM((TOP_K, tile, d), F32), pltpu.SemaphoreType.DMA(())],
        compiler_params=_cparams("arbitrary"),
        name="combine",
    )(dest_flat, gates, y, ln_g, ln_b, ys)


MOE_ROW_BLOCK = 256


def _moe_ln2(y, idx, gates, w_gate, b_gate, w_up, b_up, w_down, b_down, ln2_g, ln2_b):
    n, d = y.shape
    n_blocks = n * TOP_K // MOE_ROW_BLOCK + N_EXPERTS
    rows = n_blocks * MOE_ROW_BLOCK
    dest, counts = _route(idx, MOE_ROW_BLOCK)
    cnt = counts[0, :N_EXPERTS]
    padded = (cnt + MOE_ROW_BLOCK - 1) // MOE_ROW_BLOCK * MOE_ROW_BLOCK
    p_end = jnp.cumsum(padded)
    block_e = jnp.minimum(jnp.searchsorted(p_end, jnp.arange(n_blocks, dtype=I32) * MOE_ROW_BLOCK, side='right'),
                          N_EXPERTS - 1).astype(I32)
    n_used = (p_end[-1:] // MOE_ROW_BLOCK).astype(I32)
    dest_flat = dest.reshape(-1)
    xs = _dispatch(dest_flat, y, rows)
    ys = _experts(block_e, n_used, xs, w_gate.astype(BF16), b_gate.astype(F32), w_up.astype(BF16),
                  b_up.astype(F32), w_down.astype(BF16), b_down.astype(F32), MOE_ROW_BLOCK)
    return _combine(dest_flat, gates, y, ln2_g.reshape(1, d).astype(F32), ln2_b.reshape(1, d).astype(F32), ys)


def kernel(x, w_in, sinks, w_proj_a, w_proj_b, w_out, ln1_g, ln1_b, router_w, router_b,
           w_gate, b_gate, w_up, b_up, w_down, b_down, ln2_g, ln2_b):
    bsz, seq, d = x.shape
    h = x.reshape(bsz * seq, d)
    for l in range(w_in.shape[0]):
        y, idx, gates = _token_mixer_ln1(h, w_in[l], sinks[l], w_proj_a[l], w_proj_b[l], w_out[l],
                                         ln1_g[l], ln1_b[l], router_w[l], router_b[l], bsz, seq)
        h = _moe_ln2(y, idx, gates, w_gate[l], b_gate[l], w_up[l], b_up[l], w_down[l], b_down[l],
                     ln2_g[l], ln2_b[l])
    return h.reshape(bsz, seq, d)
```

```python
import functools

import jax
import jax.numpy as jnp
from jax import lax
from jax.experimental import pallas as pl
from jax.experimental.pallas import tpu as pltpu

F32 = jnp.float32
BF16 = jnp.bfloat16
I32 = jnp.int32

HEAD_DIM = 64
DIL_GROUPS = ((128, 1), (512, 4), (2048, 16))
DIL_HEADS = 4
N_DIL = len(DIL_GROUPS)
DIL_OUT = DIL_HEADS * HEAD_DIM
DIL_WIDTH = N_DIL * DIL_OUT
SWA_Q_HEADS = 16
SWA_KV_HEADS = 2
SWA_REP = SWA_Q_HEADS // SWA_KV_HEADS
SWA_WINDOW = 128
SWA_Q_WIDTH = SWA_Q_HEADS * HEAD_DIM
SWA_KV_WIDTH = SWA_KV_HEADS * HEAD_DIM
N_ALIBI_HEADS = SWA_Q_HEADS + N_DIL * DIL_HEADS
ATTN_BLOCK = 128
N_EXPERTS = 32
TOP_K = 4
SWIGLU_LIMIT = 7.0
SWIGLU_ALPHA = 1.702
LN_EPS = 1e-5
DEPTH = 1
DEEPNORM_ALPHA = (2 * DEPTH) ** 0.25
NEG_INF = -1e30

LANES = 128
VMEM_LIMIT_BYTES = 56 * 1024 * 1024

A_QKV_W = 3 * DIL_WIDTH
B_Q_OFF = A_QKV_W
B_KV_OFF = B_Q_OFF + SWA_Q_WIDTH
GATE_OFF = B_KV_OFF + 2 * SWA_KV_WIDTH


def _cparams(*sem):
    return pltpu.CompilerParams(dimension_semantics=sem, vmem_limit_bytes=VMEM_LIMIT_BYTES)


def _in_proj_kernel(x_ref, w_ref, a0_ref, a1_ref, a2_ref, hq_ref, hkv_ref, g_ref, *scratch, d_model, tm):
    xb = x_ref[...].astype(BF16)
    segments = ((hq_ref, B_Q_OFF, SWA_Q_WIDTH), (hkv_ref, B_KV_OFF, 2 * SWA_KV_WIDTH),
                (g_ref, GATE_OFF, 2 * d_model))
    for out_ref, col0, width in segments:
        for c in range(0, width, 512):
            w = min(512, width - c)
            r = jnp.dot(xb, w_ref[:, col0 + c:col0 + c + w], preferred_element_type=F32)
            out_ref[:, c:c + w] = r.astype(out_ref.dtype)
    for g, a_ref in enumerate((a0_ref, a1_ref, a2_ref)):
        dil = DIL_GROUPS[g][1]
        per = tm // dil
        for part in range(3):
            col0 = part * DIL_WIDTH + g * DIL_OUT
            res = jnp.dot(xb, w_ref[:, col0:col0 + DIL_OUT], preferred_element_type=F32)
            if dil == 1:
                a_ref[0, :, part * DIL_OUT:(part + 1) * DIL_OUT] = res.astype(a_ref.dtype)
                continue
            for half in range(DIL_OUT // LANES):
                stage = scratch[part * (DIL_OUT // LANES) + half]
                stage[...] = res[:, half * LANES:(half + 1) * LANES]
                c0 = part * DIL_OUT + half * LANES
                for r in range(dil):
                    a_ref[r, :, c0:c0 + LANES] = stage[pl.ds(r, per, stride=dil), :].astype(a_ref.dtype)


def _in_proj(x2, w_in_bf, bsz, seq, tm=512):
    n, d = x2.shape
    cols = w_in_bf.shape[1]
    tiles = seq // tm
    row = lambda i: (i, 0)
    dil_spec = lambda dil: pl.BlockSpec((None, dil, tm // dil, 3 * DIL_OUT), lambda i: (i // tiles, 0, i % tiles, 0))
    dil_shape = lambda dil: jax.ShapeDtypeStruct((bsz, dil, seq // dil, 3 * DIL_OUT), BF16)
    dils = [dil for _, dil in DIL_GROUPS]
    return pl.pallas_call(
        functools.partial(_in_proj_kernel, d_model=d, tm=tm),
        grid=(n // tm,),
        in_specs=[pl.BlockSpec((tm, d), row),
                  pl.BlockSpec((d, cols), lambda i: (0, 0), pipeline_mode=pl.Buffered(1))],
        out_specs=[dil_spec(dil) for dil in dils]
        + [pl.BlockSpec((tm, SWA_Q_WIDTH), row), pl.BlockSpec((tm, 2 * SWA_KV_WIDTH), row),
           pl.BlockSpec((tm, 2 * d), row)],
        out_shape=[dil_shape(dil) for dil in dils]
        + [jax.ShapeDtypeStruct((n, SWA_Q_WIDTH), BF16), jax.ShapeDtypeStruct((n, 2 * SWA_KV_WIDTH), BF16),
           jax.ShapeDtypeStruct((n, 2 * d), BF16)],
        scratch_shapes=[pltpu.VMEM((tm, LANES), F32)] * (3 * DIL_OUT // LANES),
        compiler_params=_cparams("parallel"),
        name="in_proj",
    )(x2, w_in_bf)


def _band_tables(max_diff):
    qi = lax.broadcasted_iota(I32, (ATTN_BLOCK, ATTN_BLOCK), 0)
    kj = lax.broadcasted_iota(I32, (ATTN_BLOCK, ATTN_BLOCK), 1)
    diff_prev = qi - kj + ATTN_BLOCK
    diff_cur = qi - kj
    valid_prev = diff_prev <= max_diff
    valid_cur = (diff_cur >= 0) & (diff_cur <= max_diff)
    return diff_prev.astype(F32), diff_cur.astype(F32), valid_prev, valid_cur


def _attend(q, k_prev, k_cur, v_prev, v_cur, tables, prev_ok, slope_scaled, sink):
    diff_prev, diff_cur, valid_prev, valid_cur = tables
    nt = (((1,), (1,)), ((), ()))
    s_prev = lax.dot_general(q, k_prev, nt, preferred_element_type=F32)
    s_cur = lax.dot_general(q, k_cur, nt, preferred_element_type=F32)
    s_prev = jnp.where(valid_prev & prev_ok, s_prev - slope_scaled * diff_prev, NEG_INF)
    s_cur = jnp.where(valid_cur, s_cur - slope_scaled * diff_cur, NEG_INF)
    m = jnp.max(jnp.maximum(s_prev, s_cur), axis=-1, keepdims=True)
    if sink is not None:
        m = jnp.maximum(m, sink)
    p_prev = jnp.exp(s_prev - m)
    p_cur = jnp.exp(s_cur - m)
    denom = jnp.sum(p_prev + p_cur, axis=-1, keepdims=True)
    if sink is not None:
        denom = denom + jnp.exp(sink - m)
    o = (jnp.dot(p_prev.astype(BF16), v_prev, preferred_element_type=F32)
         + jnp.dot(p_cur.astype(BF16), v_cur, preferred_element_type=F32))
    return o / denom, m + jnp.log(denom)


def _dil_attn_kernel(slope_ref, q_ref, kc_ref, kp_ref, vc_ref, vp_ref, o_ref, lse_ref, *, tq, max_diff, dist_scale):
    first = pl.program_id(2) == 0
    tables = _band_tables(max_diff)
    for i in range(tq // ATTN_BLOCK):
        rows = slice(i * ATTN_BLOCK, (i + 1) * ATTN_BLOCK)
        prev_rows = slice((i - 1) * ATTN_BLOCK, i * ATTN_BLOCK)
        prev_ok = jnp.logical_not(first) if i == 0 else True
        for h in range(DIL_HEADS):
            cols = slice(h * HEAD_DIM, (h + 1) * HEAD_DIM)
            q = q_ref[rows, cols] * 0.125
            k_prev = kp_ref[:, cols] if i == 0 else kc_ref[prev_rows, cols]
            v_prev = vp_ref[:, cols] if i == 0 else vc_ref[prev_rows, cols]
            o, lse = _attend(q, k_prev, kc_ref[rows, cols], v_prev, vc_ref[rows, cols], tables,
                             prev_ok, slope_ref[h] * dist_scale, None)
            o_ref[rows, cols] = o.astype(o_ref.dtype)
            lse_ref[rows, cols] = jnp.broadcast_to(lse, (ATTN_BLOCK, HEAD_DIM))


def _dil_attention(a_g, slopes_g, g):
    window, dil = DIL_GROUPS[g]
    bsz, _, sub_len, _ = a_g.shape
    tq = min(512, sub_len)
    nqb = tq // ATTN_BLOCK
    cur = lambda part: (lambda b, r, m: (b, r, m, part))
    prev = lambda part: (lambda b, r, m: (b, r, jnp.maximum(m * nqb - 1, 0), part))
    blk = lambda rows, imap: pl.BlockSpec((None, None, rows, DIL_OUT), imap)
    return pl.pallas_call(
        functools.partial(_dil_attn_kernel, tq=tq, max_diff=window // dil, dist_scale=float(dil)),
        grid=(bsz, dil, sub_len // tq),
        in_specs=[pl.BlockSpec(memory_space=pltpu.SMEM),
                  blk(tq, cur(0)), blk(tq, cur(1)), blk(ATTN_BLOCK, prev(1)),
                  blk(tq, cur(2)), blk(ATTN_BLOCK, prev(2))],
        out_specs=[blk(tq, cur(0)), blk(tq, cur(0))],
        out_shape=[jax.ShapeDtypeStruct((bsz, dil, sub_len, DIL_OUT), BF16),
                   jax.ShapeDtypeStruct((bsz, dil, sub_len, DIL_OUT), F32)],
        compiler_params=_cparams("parallel", "parallel", "arbitrary"),
        name=f"dil_attn_g{g}",
    )(slopes_g, a_g, a_g, a_g, a_g, a_g)


def _swa_attn_kernel(slope_ref, sink_ref, q_ref, kvc_ref, kvp_ref, o_ref, *, tq):
    first = pl.program_id(1) == 0
    tables = _band_tables(SWA_WINDOW - 1)
    for i in range(tq // ATTN_BLOCK):
        rows = slice(i * ATTN_BLOCK, (i + 1) * ATTN_BLOCK)
        prev_rows = slice((i - 1) * ATTN_BLOCK, i * ATTN_BLOCK)
        prev_ok = jnp.logical_not(first) if i == 0 else True
        for h in range(SWA_Q_HEADS):
            kvh = h // SWA_REP
            kcols = slice(kvh * HEAD_DIM, (kvh + 1) * HEAD_DIM)
            vcols = slice(SWA_KV_WIDTH + kvh * HEAD_DIM, SWA_KV_WIDTH + (kvh + 1) * HEAD_DIM)
            cols = slice(h * HEAD_DIM, (h + 1) * HEAD_DIM)
            q = q_ref[rows, cols] * 0.125
            k_prev = kvp_ref[:, kcols] if i == 0 else kvc_ref[prev_rows, kcols]
            v_prev = kvp_ref[:, vcols] if i == 0 else kvc_ref[prev_rows, vcols]
            o, _ = _attend(q, k_prev, kvc_ref[rows, kcols], v_prev, kvc_ref[rows, vcols], tables,
                           prev_ok, slope_ref[h], sink_ref[h])
            o_ref[rows, cols] = o.astype(o_ref.dtype)


def _swa_attention(hq, hkv, slopes_b, sinks, bsz, seq, tq=512):
    nqb = tq // ATTN_BLOCK
    hq3 = hq.reshape(bsz, seq, SWA_Q_WIDTH)
    hkv3 = hkv.reshape(bsz, seq, 2 * SWA_KV_WIDTH)
    smem = pl.BlockSpec(memory_space=pltpu.SMEM)
    out = pl.pallas_call(
        functools.partial(_swa_attn_kernel, tq=tq),
        grid=(bsz, seq // tq),
        in_specs=[smem, smem,
                  pl.BlockSpec((None, tq, SWA_Q_WIDTH), lambda b, m: (b, m, 0)),
                  pl.BlockSpec((None, tq, 2 * SWA_KV_WIDTH), lambda b, m: (b, m, 0)),
                  pl.BlockSpec((None, ATTN_BLOCK, 2 * SWA_KV_WIDTH),
                               lambda b, m: (b, jnp.maximum(m * nqb - 1, 0), 0))],
        out_specs=pl.BlockSpec((None, tq, SWA_Q_WIDTH), lambda b, m: (b, m, 0)),
        out_shape=jax.ShapeDtypeStruct((bsz, seq, SWA_Q_WIDTH), BF16),
        compiler_params=_cparams("parallel", "arbitrary"),
        name="swa_attn",
    )(slopes_b, sinks, hq3, hkv3, hkv3)
    return out.reshape(bsz * seq, SWA_Q_WIDTH)


def _layer_norm(z, g, b):
    mu = jnp.mean(z, axis=-1, keepdims=True)
    zc = z - mu
    var = jnp.mean(zc * zc, axis=-1, keepdims=True)
    return zc * lax.rsqrt(var + LN_EPS) * g + b


def _mix_out_kernel(x_ref, o0_ref, o1_ref, o2_ref, l0_ref, l1_ref, l2_ref, ob_ref, g_ref,
                    wpa_ref, wpb_ref, wo_ref, lng_ref, lnb_ref, rw_ref, rb_ref,
                    y_ref, idx_ref, gate_ref, *scratch, d_model):
    def natural(ref, stages):
        dil, per, _ = ref.shape
        if dil == 1:
            return ref[0].astype(F32)
        for half, stage in enumerate(stages):
            for r in range(dil):
                stage[pl.ds(r, per, stride=dil), :] = ref[r, :, half * LANES:(half + 1) * LANES].astype(F32)
        return jnp.concatenate([stage[...] for stage in stages], axis=1)

    o0, o1, o2 = natural(o0_ref, None), natural(o1_ref, scratch[0:2]), natural(o2_ref, scratch[2:4])
    l0, l1, l2 = natural(l0_ref, None), natural(l1_ref, scratch[4:6]), natural(l2_ref, scratch[6:8])
    lm = jnp.maximum(jnp.maximum(l0, l1), l2)
    e0, e1, e2 = jnp.exp(l0 - lm), jnp.exp(l1 - lm), jnp.exp(l2 - lm)
    esum = e0 + e1 + e2
    out_a = (e0 / esum) * o0 + (e1 / esum) * o1 + (e2 / esum) * o2
    pa = jnp.dot(out_a.astype(BF16), wpa_ref[...], preferred_element_type=F32)
    pb = jnp.dot(ob_ref[...], wpb_ref[...], preferred_element_type=F32)
    ga = jax.nn.sigmoid(g_ref[:, :d_model].astype(F32))
    gb = jax.nn.sigmoid(g_ref[:, d_model:].astype(F32))
    merged = ga * pa + gb * pb
    mix = jnp.dot(merged.astype(BF16), wo_ref[...], preferred_element_type=F32)
    y = _layer_norm(DEEPNORM_ALPHA * x_ref[...] + mix, lng_ref[...], lnb_ref[...])
    y_ref[...] = y

    logits = jnp.dot(y.astype(BF16), rw_ref[...], preferred_element_type=F32) + rb_ref[...]
    lane = lax.broadcasted_iota(I32, logits.shape, 1)
    logits = jnp.where(lane < N_EXPERTS, logits, -jnp.inf)
    idx_out = jnp.zeros(logits.shape, I32)
    val_out = jnp.full(logits.shape, -jnp.inf, F32)
    for k in range(TOP_K):
        top = jnp.max(logits, axis=-1, keepdims=True)
        top_idx = jnp.min(jnp.where(logits == top, lane, LANES), axis=-1, keepdims=True)
        idx_out = jnp.where(lane == k, top_idx, idx_out)
        val_out = jnp.where(lane == k, top, val_out)
        logits = jnp.where(lane == top_idx, -jnp.inf, logits)
    ev = jnp.exp(val_out - jnp.max(val_out, axis=-1, keepdims=True))
    gates = ev / jnp.sum(ev, axis=-1, keepdims=True)
    idx_ref[...] = idx_out[:, :TOP_K]
    gate_ref[...] = gates[:, :TOP_K]


def _mix_out(x2, o_g, lse_g, out_b, gates_h, wpa, wpb, wo, ln_g, ln_b, rw_pad, rb_pad, seq, tm=256):
    n, d = x2.shape
    tiles = seq // tm
    row = lambda i: (i, 0)
    const = lambda i: (0, 0)
    rb = lambda w: pl.BlockSpec((tm, w), row)
    full = lambda a: pl.BlockSpec(a.shape, const)
    dil_specs = [pl.BlockSpec((None, dil, tm // dil, DIL_OUT), lambda i: (i // tiles, 0, i % tiles, 0))
                 for _, dil in DIL_GROUPS]
    return pl.pallas_call(
        functools.partial(_mix_out_kernel, d_model=d),
        grid=(n // tm,),
        in_specs=[rb(d)] + dil_specs + dil_specs + [rb(SWA_Q_WIDTH), rb(2 * d),
                  full(wpa), full(wpb), full(wo), full(ln_g), full(ln_b), full(rw_pad), full(rb_pad)],
        out_specs=[rb(d), rb(TOP_K), rb(TOP_K)],
        out_shape=[jax.ShapeDtypeStruct((n, d), F32),
                   jax.ShapeDtypeStruct((n, TOP_K), I32), jax.ShapeDtypeStruct((n, TOP_K), F32)],
        scratch_shapes=[pltpu.VMEM((tm, LANES), F32)] * (4 * DIL_OUT // LANES),
        compiler_params=_cparams("parallel"),
        name="mix_out",
    )(x2, *o_g, *lse_g, out_b, gates_h, wpa, wpb, wo, ln_g, ln_b, rw_pad, rb_pad)


def _token_mixer_ln1(x2, w_in, sinks, w_proj_a, w_proj_b, w_out, ln1_g, ln1_b, router_w, router_b, bsz, seq):
    n, d = x2.shape
    heads = jnp.arange(1, N_ALIBI_HEADS + 1, dtype=F32)
    slopes = jnp.exp2(-8.0 * heads / N_ALIBI_HEADS)
    *a_g, hq, hkv, gates_h = _in_proj(x2, w_in.astype(BF16), bsz, seq)
    o_g, lse_g = [], []
    for g in range(N_DIL):
        sl = slopes[SWA_Q_HEADS + g * DIL_HEADS:SWA_Q_HEADS + (g + 1) * DIL_HEADS]
        o, lse = _dil_attention(a_g[g], sl, g)
        o_g.append(o)
        lse_g.append(lse)
    out_b = _swa_attention(hq, hkv, slopes[:SWA_Q_HEADS], sinks.astype(F32), bsz, seq)
    rw_pad = jnp.zeros((d, LANES), BF16).at[:, :N_EXPERTS].set(router_w.astype(BF16))
    rb_pad = jnp.zeros((1, LANES), F32).at[0, :N_EXPERTS].set(router_b.astype(F32))
    return _mix_out(x2, o_g, lse_g, out_b, gates_h, w_proj_a.astype(BF16), w_proj_b.astype(BF16),
                    w_out.astype(BF16), ln1_g.reshape(1, d).astype(F32), ln1_b.reshape(1, d).astype(F32),
                    rw_pad, rb_pad, seq)


def _route_kernel(idx_ref, dest_ref, cnt_ref, carry_ref, pstart_ref, *, expert_block):
    phase, i = pl.program_id(0), pl.program_id(1)
    idx = idx_ref[...]
    t = idx.shape[0]
    lane = lax.broadcasted_iota(I32, (t, LANES), 1)
    onehot = [lane == idx[:, k:k + 1] for k in range(TOP_K)]
    multi = sum(oh.astype(F32) for oh in onehot)
    tile_cnt = jnp.sum(multi, axis=0, keepdims=True).astype(I32)

    @pl.when((phase == 0) & (i == 0))
    def _():
        carry_ref[...] = jnp.zeros_like(carry_ref)

    @pl.when((phase == 1) & (i == 0))
    def _():
        counts = carry_ref[...]
        padded = (counts + (expert_block - 1)) & (-expert_block)
        lane8 = lax.broadcasted_iota(I32, counts.shape, 1)
        incl = padded
        for sh in (1, 2, 4, 8, 16):
            incl = incl + jnp.where(lane8 >= sh, pltpu.roll(incl, sh, axis=1), 0)
        pstart_ref[...] = incl - padded
        cnt_ref[...] = counts
        carry_ref[...] = jnp.zeros_like(carry_ref)

    @pl.when(phase == 1)
    def _():
        r = lax.broadcasted_iota(I32, (t, t), 0)
        c = lax.broadcasted_iota(I32, (t, t), 1)
        earlier = (c < r).astype(BF16)
        cum = jnp.dot(earlier, multi.astype(BF16), preferred_element_type=F32).astype(I32)
        base = cum + carry_ref[0:1, :] + pstart_ref[0:1, :]
        dest = jnp.zeros((t, LANES), I32)
        for k in range(TOP_K):
            d_k = jnp.sum(jnp.where(onehot[k], base, 0), axis=-1, keepdims=True)
            dest = jnp.where(lane == k, d_k, dest)
        dest_ref[...] = dest[:, :TOP_K]

    carry_ref[...] = carry_ref[...] + tile_cnt


def _route(idx, expert_block, tile=512):
    n = idx.shape[0]
    tile = min(tile, n)
    return pl.pallas_call(
        functools.partial(_route_kernel, expert_block=expert_block),
        grid=(2, n // tile),
        in_specs=[pl.BlockSpec((tile, TOP_K), lambda p, i: (i, 0))],
        out_specs=[pl.BlockSpec((tile, TOP_K), lambda p, i: (i * p, 0)),
                   pl.BlockSpec((8, LANES), lambda p, i: (0, 0))],
        out_shape=[jax.ShapeDtypeStruct((n, TOP_K), I32), jax.ShapeDtypeStruct((8, LANES), I32)],
        scratch_shapes=[pltpu.VMEM((8, LANES), I32), pltpu.VMEM((8, LANES), I32)],
        compiler_params=_cparams("arbitrary", "arbitrary"),
        name="route",
    )(idx)


def _row_copy(src_hbm, src_row, dst, dst_row, sem):
    return pltpu.make_async_copy(src_hbm.at[pl.ds(src_row, 1)], dst.at[pl.ds(dst_row, 1)], sem)


def _dispatch_kernel(dest_ref, y_ref, xs_in_hbm, xs_hbm, sem, *, tile):
    del xs_in_hbm

    def issue(t, carry):
        for k in range(TOP_K):
            _row_copy(y_ref, t, xs_hbm, dest_ref[t * TOP_K + k], sem).start()
        return carry

    lax.fori_loop(0, tile, issue, 0)

    def drain(t, carry):
        _row_copy(y_ref, 0, xs_hbm, 0, sem).wait()
        return carry

    lax.fori_loop(0, tile * TOP_K, drain, 0)


def _dispatch(dest_flat, y, rows, tile=512):
    n, d = y.shape
    tile = min(tile, n)
    any_spec = pl.BlockSpec(memory_space=pl.ANY)
    return pl.pallas_call(
        functools.partial(_dispatch_kernel, tile=tile),
        grid=(n // tile,),
        in_specs=[pl.BlockSpec((tile * TOP_K,), lambda i: (i,), memory_space=pltpu.SMEM),
                  pl.BlockSpec((tile, d), lambda i: (i, 0)), any_spec],
        out_specs=any_spec,
        out_shape=jax.ShapeDtypeStruct((rows, d), y.dtype),
        scratch_shapes=[pltpu.SemaphoreType.DMA(())],
        input_output_aliases={2: 0},
        compiler_params=_cparams("arbitrary"),
        name="dispatch",
    )(dest_flat, y, jnp.zeros((rows, d), y.dtype))


def _expert_kernel(be_ref, nu_ref, x_ref, wg_ref, bg_ref, wu_ref, bu_ref, wd_ref, bd_ref, o_ref):
    del be_ref
    j = pl.program_id(0)

    @pl.when(j < nu_ref[0])
    def _():
        xb = x_ref[...].astype(BF16)
        gt = jnp.dot(xb, wg_ref[...], preferred_element_type=F32) + bg_ref[...]
        up = jnp.dot(xb, wu_ref[...], preferred_element_type=F32) + bu_ref[...]
        gt = jnp.minimum(gt, SWIGLU_LIMIT)
        up = jnp.clip(up, -SWIGLU_LIMIT, SWIGLU_LIMIT)
        hdn = gt * jax.nn.sigmoid(SWIGLU_ALPHA * gt) * (up + 1.0)
        o_ref[...] = jnp.dot(hdn.astype(BF16), wd_ref[...], preferred_element_type=F32) + bd_ref[...]

    @pl.when(j >= nu_ref[0])
    def _():
        o_ref[...] = jnp.zeros_like(o_ref)


def _experts(block_e, n_used, xs, wg, bg, wu, bu, wd, bd, expert_block):
    rows, d = xs.shape
    e, _, f = wg.shape
    xmap = lambda j, be, nu: (jnp.minimum(j, nu[0] - 1), 0)
    wmap = lambda j, be, nu: (be[j], 0, 0)
    grid_spec = pltpu.PrefetchScalarGridSpec(
        num_scalar_prefetch=2,
        grid=(rows // expert_block,),
        in_specs=[pl.BlockSpec((expert_block, d), xmap),
                  pl.BlockSpec((None, d, f), wmap), pl.BlockSpec((None, 1, f), wmap),
                  pl.BlockSpec((None, d, f), wmap), pl.BlockSpec((None, 1, f), wmap),
                  pl.BlockSpec((None, f, d), wmap), pl.BlockSpec((None, 1, d), wmap)],
        out_specs=pl.BlockSpec((expert_block, d), lambda j, be, nu: (j, 0)),
    )
    return pl.pallas_call(
        _expert_kernel,
        grid_spec=grid_spec,
        out_shape=jax.ShapeDtypeStruct((rows, d), F32),
        compiler_params=_cparams("arbitrary"),
        name="experts",
    )(block_e, n_used, xs, wg, bg.reshape(e, 1, f), wu, bu.reshape(e, 1, f), wd, bd.reshape(e, 1, d))


def _combine_kernel(dest_ref, gate_ref, y_ref, lng_ref, lnb_ref, ys_hbm, o_ref, buf_ref, sem, *, tile):
    def issue(t, carry):
        for k in range(TOP_K):
            _row_copy(ys_hbm, dest_ref[t * TOP_K + k], buf_ref.at[k], t, sem).start()
        return carry

    lax.fori_loop(0, tile, issue, 0)

    def drain(t, carry):
        _row_copy(ys_hbm, 0, buf_ref.at[0], 0, sem).wait()
        return carry

    lax.fori_loop(0, tile * TOP_K, drain, 0)

    gates = gate_ref[...]
    ffn = gates[:, 0:1] * buf_ref[0]
    for k in range(1, TOP_K):
        ffn = ffn + gates[:, k:k + 1] * buf_ref[k]
    o_ref[...] = _layer_norm(DEEPNORM_ALPHA * y_ref[...] + ffn, lng_ref[...], lnb_ref[...])


def _combine(dest_flat, gates, y, ln_g, ln_b, ys, tile=256):
    n, d = y.shape
    tile = min(tile, n)
    row = lambda i: (i, 0)
    const = lambda i: (0, 0)
    return pl.pallas_call(
        functools.partial(_combine_kernel, tile=tile),
        grid=(n // tile,),
        in_specs=[pl.BlockSpec((tile * TOP_K,), lambda i: (i,), memory_space=pltpu.SMEM),
                  pl.BlockSpec((tile, TOP_K), row), pl.BlockSpec((tile, d), row),
                  pl.BlockSpec((1, d), const), pl.BlockSpec((1, d), const),
                  pl.BlockSpec(memory_space=pl.ANY)],
        out_specs=pl.BlockSpec((tile, d), row),
        out_shape=jax.ShapeDtypeStruct((n, d), F32),
        scratch_shapes=[pltpu.VMEM((TOP_K, tile, d), F32), pltpu.SemaphoreType.DMA(())],
        compiler_params=_cparams("arbitrary"),
        name="combine",
    )(dest_flat, gates, y, ln_g, ln_b, ys)


MOE_ROW_BLOCK = 256


def _moe_ln2(y, idx, gates, w_gate, b_gate, w_up, b_up, w_down, b_down, ln2_g, ln2_b):
    n, d = y.shape
    n_blocks = n * TOP_K // MOE_ROW_BLOCK + N_EXPERTS
    rows = n_blocks * MOE_ROW_BLOCK
    dest, counts = _route(idx, MOE_ROW_BLOCK)
    cnt = counts[0, :N_EXPERTS]
    padded = (cnt + MOE_ROW_BLOCK - 1) // MOE_ROW_BLOCK * MOE_ROW_BLOCK
    p_end = jnp.cumsum(padded)
    block_row0 = jnp.arange(n_blocks, dtype=I32) * MOE_ROW_BLOCK
    block_e = jnp.minimum(jnp.sum(p_end[None, :] <= block_row0[:, None], axis=1), N_EXPERTS - 1).astype(I32)
    n_used = (p_end[-1:] // MOE_ROW_BLOCK).astype(I32)
    dest_flat = dest.reshape(-1)
    xs = _dispatch(dest_flat, y, rows)
    ys = _experts(block_e, n_used, xs, w_gate.astype(BF16), b_gate.astype(F32), w_up.astype(BF16),
                  b_up.astype(F32), w_down.astype(BF16), b_down.astype(F32), MOE_ROW_BLOCK)
    return _combine(dest_flat, gates, y, ln2_g.reshape(1, d).astype(F32), ln2_b.reshape(1, d).astype(F32), ys)


def kernel(x, w_in, sinks, w_proj_a, w_proj_b, w_out, ln1_g, ln1_b, router_w, router_b,
           w_gate, b_gate, w_up, b_up, w_down, b_down, ln2_g, ln2_b):
    bsz, seq, d = x.shape
    h = x.reshape(bsz * seq, d)
    for l in range(w_in.shape[0]):
        y, idx, gates = _token_mixer_ln1(h, w_in[l], sinks[l], w_proj_a[l], w_proj_b[l], w_out[l],
                                         ln1_g[l], ln1_b[l], router_w[l], router_b[l], bsz, seq)
        h = _moe_ln2(y, idx, gates, w_gate[l], b_gate[l], w_up[l], b_up[l], w_down[l], b_down[l],
                     ln2_g[l], ln2_b[l])
    return h.reshape(bsz, seq, d)
```

```python
import functools

import jax
import jax.numpy as jnp
from jax import lax
from jax.experimental import pallas as pl
from jax.experimental.pallas import tpu as pltpu
from jax.experimental.pallas import tpu_sc as plsc

F32 = jnp.float32
BF16 = jnp.bfloat16
I32 = jnp.int32

HEAD_DIM = 64
DIL_GROUPS = ((128, 1), (512, 4), (2048, 16))
DIL_HEADS = 4
N_DIL = len(DIL_GROUPS)
DIL_OUT = DIL_HEADS * HEAD_DIM
DIL_WIDTH = N_DIL * DIL_OUT
SWA_Q_HEADS = 16
SWA_KV_HEADS = 2
SWA_REP = SWA_Q_HEADS // SWA_KV_HEADS
SWA_WINDOW = 128
SWA_Q_WIDTH = SWA_Q_HEADS * HEAD_DIM
SWA_KV_WIDTH = SWA_KV_HEADS * HEAD_DIM
N_ALIBI_HEADS = SWA_Q_HEADS + N_DIL * DIL_HEADS
ATTN_BLOCK = 128
N_EXPERTS = 32
TOP_K = 4
SWIGLU_LIMIT = 7.0
SWIGLU_ALPHA = 1.702
LN_EPS = 1e-5
DEPTH = 1
DEEPNORM_ALPHA = (2 * DEPTH) ** 0.25
NEG_INF = -1e30

LANES = 128
VMEM_LIMIT_BYTES = 56 * 1024 * 1024

A_QKV_W = 3 * DIL_WIDTH
B_Q_OFF = A_QKV_W
B_KV_OFF = B_Q_OFF + SWA_Q_WIDTH
GATE_OFF = B_KV_OFF + 2 * SWA_KV_WIDTH


def _cparams(*sem):
    return pltpu.CompilerParams(dimension_semantics=sem, vmem_limit_bytes=VMEM_LIMIT_BYTES)


def _in_proj_kernel(x_ref, w_ref, a0_ref, a1_ref, a2_ref, hq_ref, hkv_ref, g_ref, *scratch, d_model, tm):
    xb = x_ref[...].astype(BF16)
    segments = ((hq_ref, B_Q_OFF, SWA_Q_WIDTH), (hkv_ref, B_KV_OFF, 2 * SWA_KV_WIDTH),
                (g_ref, GATE_OFF, 2 * d_model))
    for out_ref, col0, width in segments:
        for c in range(0, width, 512):
            w = min(512, width - c)
            r = jnp.dot(xb, w_ref[:, col0 + c:col0 + c + w], preferred_element_type=F32)
            out_ref[:, c:c + w] = r.astype(out_ref.dtype)
    for g, a_ref in enumerate((a0_ref, a1_ref, a2_ref)):
        dil = DIL_GROUPS[g][1]
        per = tm // dil
        for part in range(3):
            col0 = part * DIL_WIDTH + g * DIL_OUT
            res = jnp.dot(xb, w_ref[:, col0:col0 + DIL_OUT], preferred_element_type=F32)
            if dil == 1:
                a_ref[0, :, part * DIL_OUT:(part + 1) * DIL_OUT] = res.astype(a_ref.dtype)
                continue
            for half in range(DIL_OUT // LANES):
                stage = scratch[part * (DIL_OUT // LANES) + half]
                stage[...] = res[:, half * LANES:(half + 1) * LANES]
                c0 = part * DIL_OUT + half * LANES
                for r in range(dil):
                    a_ref[r, :, c0:c0 + LANES] = stage[pl.ds(r, per, stride=dil), :].astype(a_ref.dtype)


def _in_proj(x2, w_in_bf, bsz, seq, tm=512):
    n, d = x2.shape
    cols = w_in_bf.shape[1]
    tiles = seq // tm
    row = lambda i: (i, 0)
    dil_spec = lambda dil: pl.BlockSpec((None, dil, tm // dil, 3 * DIL_OUT), lambda i: (i // tiles, 0, i % tiles, 0))
    dil_shape = lambda dil: jax.ShapeDtypeStruct((bsz, dil, seq // dil, 3 * DIL_OUT), BF16)
    dils = [dil for _, dil in DIL_GROUPS]
    return pl.pallas_call(
        functools.partial(_in_proj_kernel, d_model=d, tm=tm),
        grid=(n // tm,),
        in_specs=[pl.BlockSpec((tm, d), row),
                  pl.BlockSpec((d, cols), lambda i: (0, 0), pipeline_mode=pl.Buffered(1))],
        out_specs=[dil_spec(dil) for dil in dils]
        + [pl.BlockSpec((tm, SWA_Q_WIDTH), row), pl.BlockSpec((tm, 2 * SWA_KV_WIDTH), row),
           pl.BlockSpec((tm, 2 * d), row)],
        out_shape=[dil_shape(dil) for dil in dils]
        + [jax.ShapeDtypeStruct((n, SWA_Q_WIDTH), BF16), jax.ShapeDtypeStruct((n, 2 * SWA_KV_WIDTH), BF16),
           jax.ShapeDtypeStruct((n, 2 * d), BF16)],
        scratch_shapes=[pltpu.VMEM((tm, LANES), F32)] * (3 * DIL_OUT // LANES),
        compiler_params=_cparams("parallel"),
        name="in_proj",
    )(x2, w_in_bf)


def _band_tables(max_diff):
    qi = lax.broadcasted_iota(I32, (ATTN_BLOCK, ATTN_BLOCK), 0)
    kj = lax.broadcasted_iota(I32, (ATTN_BLOCK, ATTN_BLOCK), 1)
    diff_prev = qi - kj + ATTN_BLOCK
    diff_cur = qi - kj
    valid_prev = diff_prev <= max_diff
    valid_cur = (diff_cur >= 0) & (diff_cur <= max_diff)
    return diff_prev.astype(F32), diff_cur.astype(F32), valid_prev, valid_cur


def _attend(q, k_prev, k_cur, v_prev, v_cur, tables, prev_ok, slope_scaled, sink):
    diff_prev, diff_cur, valid_prev, valid_cur = tables
    nt = (((1,), (1,)), ((), ()))
    s_prev = lax.dot_general(q, k_prev, nt, preferred_element_type=F32)
    s_cur = lax.dot_general(q, k_cur, nt, preferred_element_type=F32)
    s_prev = jnp.where(valid_prev & prev_ok, s_prev - slope_scaled * diff_prev, NEG_INF)
    s_cur = jnp.where(valid_cur, s_cur - slope_scaled * diff_cur, NEG_INF)
    m = jnp.max(jnp.maximum(s_prev, s_cur), axis=-1, keepdims=True)
    if sink is not None:
        m = jnp.maximum(m, sink)
    p_prev = jnp.exp(s_prev - m)
    p_cur = jnp.exp(s_cur - m)
    denom = jnp.sum(p_prev + p_cur, axis=-1, keepdims=True)
    if sink is not None:
        denom = denom + jnp.exp(sink - m)
    o = (jnp.dot(p_prev.astype(BF16), v_prev, preferred_element_type=F32)
         + jnp.dot(p_cur.astype(BF16), v_cur, preferred_element_type=F32))
    return o / denom, m + jnp.log(denom)


def _dil_attn_kernel(slope_ref, q_ref, kc_ref, kp_ref, vc_ref, vp_ref, o_ref, lse_ref, *, tq, max_diff, dist_scale):
    first = pl.program_id(2) == 0
    tables = _band_tables(max_diff)
    for i in range(tq // ATTN_BLOCK):
        rows = slice(i * ATTN_BLOCK, (i + 1) * ATTN_BLOCK)
        prev_rows = slice((i - 1) * ATTN_BLOCK, i * ATTN_BLOCK)
        prev_ok = jnp.logical_not(first) if i == 0 else True
        for h in range(DIL_HEADS):
            cols = slice(h * HEAD_DIM, (h + 1) * HEAD_DIM)
            q = q_ref[rows, cols] * 0.125
            k_prev = kp_ref[:, cols] if i == 0 else kc_ref[prev_rows, cols]
            v_prev = vp_ref[:, cols] if i == 0 else vc_ref[prev_rows, cols]
            o, lse = _attend(q, k_prev, kc_ref[rows, cols], v_prev, vc_ref[rows, cols], tables,
                             prev_ok, slope_ref[h] * dist_scale, None)
            o_ref[rows, cols] = o.astype(o_ref.dtype)
            lse_ref[rows, cols] = jnp.broadcast_to(lse, (ATTN_BLOCK, HEAD_DIM))


def _dil_attention(a_g, slopes_g, g):
    window, dil = DIL_GROUPS[g]
    bsz, _, sub_len, _ = a_g.shape
    tq = min(512, sub_len)
    nqb = tq // ATTN_BLOCK
    cur = lambda part: (lambda b, r, m: (b, r, m, part))
    prev = lambda part: (lambda b, r, m: (b, r, jnp.maximum(m * nqb - 1, 0), part))
    blk = lambda rows, imap: pl.BlockSpec((None, None, rows, DIL_OUT), imap)
    return pl.pallas_call(
        functools.partial(_dil_attn_kernel, tq=tq, max_diff=window // dil, dist_scale=float(dil)),
        grid=(bsz, dil, sub_len // tq),
        in_specs=[pl.BlockSpec(memory_space=pltpu.SMEM),
                  blk(tq, cur(0)), blk(tq, cur(1)), blk(ATTN_BLOCK, prev(1)),
                  blk(tq, cur(2)), blk(ATTN_BLOCK, prev(2))],
        out_specs=[blk(tq, cur(0)), blk(tq, cur(0))],
        out_shape=[jax.ShapeDtypeStruct((bsz, dil, sub_len, DIL_OUT), BF16),
                   jax.ShapeDtypeStruct((bsz, dil, sub_len, DIL_OUT), F32)],
        compiler_params=_cparams("parallel", "parallel", "arbitrary"),
        name=f"dil_attn_g{g}",
    )(slopes_g, a_g, a_g, a_g, a_g, a_g)


def _swa_attn_kernel(slope_ref, sink_ref, q_ref, kvc_ref, kvp_ref, o_ref, *, tq):
    first = pl.program_id(1) == 0
    tables = _band_tables(SWA_WINDOW - 1)
    for i in range(tq // ATTN_BLOCK):
        rows = slice(i * ATTN_BLOCK, (i + 1) * ATTN_BLOCK)
        prev_rows = slice((i - 1) * ATTN_BLOCK, i * ATTN_BLOCK)
        prev_ok = jnp.logical_not(first) if i == 0 else True
        for h in range(SWA_Q_HEADS):
            kvh = h // SWA_REP
            kcols = slice(kvh * HEAD_DIM, (kvh + 1) * HEAD_DIM)
            vcols = slice(SWA_KV_WIDTH + kvh * HEAD_DIM, SWA_KV_WIDTH + (kvh + 1) * HEAD_DIM)
            cols = slice(h * HEAD_DIM, (h + 1) * HEAD_DIM)
            q = q_ref[rows, cols] * 0.125
            k_prev = kvp_ref[:, kcols] if i == 0 else kvc_ref[prev_rows, kcols]
            v_prev = kvp_ref[:, vcols] if i == 0 else kvc_ref[prev_rows, vcols]
            o, _ = _attend(q, k_prev, kvc_ref[rows, kcols], v_prev, kvc_ref[rows, vcols], tables,
                           prev_ok, slope_ref[h], sink_ref[h])
            o_ref[rows, cols] = o.astype(o_ref.dtype)


def _swa_attention(hq, hkv, slopes_b, sinks, bsz, seq, tq=512):
    nqb = tq // ATTN_BLOCK
    hq3 = hq.reshape(bsz, seq, SWA_Q_WIDTH)
    hkv3 = hkv.reshape(bsz, seq, 2 * SWA_KV_WIDTH)
    smem = pl.BlockSpec(memory_space=pltpu.SMEM)
    out = pl.pallas_call(
        functools.partial(_swa_attn_kernel, tq=tq),
        grid=(bsz, seq // tq),
        in_specs=[smem, smem,
                  pl.BlockSpec((None, tq, SWA_Q_WIDTH), lambda b, m: (b, m, 0)),
                  pl.BlockSpec((None, tq, 2 * SWA_KV_WIDTH), lambda b, m: (b, m, 0)),
                  pl.BlockSpec((None, ATTN_BLOCK, 2 * SWA_KV_WIDTH),
                               lambda b, m: (b, jnp.maximum(m * nqb - 1, 0), 0))],
        out_specs=pl.BlockSpec((None, tq, SWA_Q_WIDTH), lambda b, m: (b, m, 0)),
        out_shape=jax.ShapeDtypeStruct((bsz, seq, SWA_Q_WIDTH), BF16),
        compiler_params=_cparams("parallel", "arbitrary"),
        name="swa_attn",
    )(slopes_b, sinks, hq3, hkv3, hkv3)
    return out.reshape(bsz * seq, SWA_Q_WIDTH)


def _layer_norm(z, g, b):
    mu = jnp.mean(z, axis=-1, keepdims=True)
    zc = z - mu
    var = jnp.mean(zc * zc, axis=-1, keepdims=True)
    return zc * lax.rsqrt(var + LN_EPS) * g + b


def _mix_out_kernel(x_ref, o0_ref, o1_ref, o2_ref, l0_ref, l1_ref, l2_ref, ob_ref, g_ref,
                    wpa_ref, wpb_ref, wo_ref, lng_ref, lnb_ref, rw_ref, rb_ref,
                    y_ref, idx_ref, gate_ref, *scratch, d_model):
    def natural(ref, stages):
        dil, per, _ = ref.shape
        if dil == 1:
            return ref[0].astype(F32)
        for half, stage in enumerate(stages):
            for r in range(dil):
                stage[pl.ds(r, per, stride=dil), :] = ref[r, :, half * LANES:(half + 1) * LANES].astype(F32)
        return jnp.concatenate([stage[...] for stage in stages], axis=1)

    o0, o1, o2 = natural(o0_ref, None), natural(o1_ref, scratch[0:2]), natural(o2_ref, scratch[2:4])
    l0, l1, l2 = natural(l0_ref, None), natural(l1_ref, scratch[4:6]), natural(l2_ref, scratch[6:8])
    lm = jnp.maximum(jnp.maximum(l0, l1), l2)
    e0, e1, e2 = jnp.exp(l0 - lm), jnp.exp(l1 - lm), jnp.exp(l2 - lm)
    esum = e0 + e1 + e2
    out_a = (e0 / esum) * o0 + (e1 / esum) * o1 + (e2 / esum) * o2
    pa = jnp.dot(out_a.astype(BF16), wpa_ref[...], preferred_element_type=F32)
    pb = jnp.dot(ob_ref[...], wpb_ref[...], preferred_element_type=F32)
    ga = jax.nn.sigmoid(g_ref[:, :d_model].astype(F32))
    gb = jax.nn.sigmoid(g_ref[:, d_model:].astype(F32))
    merged = ga * pa + gb * pb
    mix = jnp.dot(merged.astype(BF16), wo_ref[...], preferred_element_type=F32)
    y = _layer_norm(DEEPNORM_ALPHA * x_ref[...] + mix, lng_ref[...], lnb_ref[...])
    y_ref[...] = y

    logits = jnp.dot(y.astype(BF16), rw_ref[...], preferred_element_type=F32) + rb_ref[...]
    lane = lax.broadcasted_iota(I32, logits.shape, 1)
    logits = jnp.where(lane < N_EXPERTS, logits, -jnp.inf)
    idx_out = jnp.zeros(logits.shape, I32)
    val_out = jnp.full(logits.shape, -jnp.inf, F32)
    for k in range(TOP_K):
        top = jnp.max(logits, axis=-1, keepdims=True)
        top_idx = jnp.min(jnp.where(logits == top, lane, LANES), axis=-1, keepdims=True)
        idx_out = jnp.where(lane == k, top_idx, idx_out)
        val_out = jnp.where(lane == k, top, val_out)
        logits = jnp.where(lane == top_idx, -jnp.inf, logits)
    ev = jnp.exp(val_out - jnp.max(val_out, axis=-1, keepdims=True))
    gates = ev / jnp.sum(ev, axis=-1, keepdims=True)
    idx_ref[...] = idx_out[:, :TOP_K]
    gate_ref[...] = gates[:, :TOP_K]


def _mix_out(x2, o_g, lse_g, out_b, gates_h, wpa, wpb, wo, ln_g, ln_b, rw_pad, rb_pad, seq, tm=256):
    n, d = x2.shape
    tiles = seq // tm
    row = lambda i: (i, 0)
    const = lambda i: (0, 0)
    rb = lambda w: pl.BlockSpec((tm, w), row)
    full = lambda a: pl.BlockSpec(a.shape, const)
    dil_specs = [pl.BlockSpec((None, dil, tm // dil, DIL_OUT), lambda i: (i // tiles, 0, i % tiles, 0))
                 for _, dil in DIL_GROUPS]
    return pl.pallas_call(
        functools.partial(_mix_out_kernel, d_model=d),
        grid=(n // tm,),
        in_specs=[rb(d)] + dil_specs + dil_specs + [rb(SWA_Q_WIDTH), rb(2 * d),
                  full(wpa), full(wpb), full(wo), full(ln_g), full(ln_b), full(rw_pad), full(rb_pad)],
        out_specs=[rb(d), rb(TOP_K), rb(TOP_K)],
        out_shape=[jax.ShapeDtypeStruct((n, d), F32),
                   jax.ShapeDtypeStruct((n, TOP_K), I32), jax.ShapeDtypeStruct((n, TOP_K), F32)],
        scratch_shapes=[pltpu.VMEM((tm, LANES), F32)] * (4 * DIL_OUT // LANES),
        compiler_params=_cparams("parallel"),
        name="mix_out",
    )(x2, *o_g, *lse_g, out_b, gates_h, wpa, wpb, wo, ln_g, ln_b, rw_pad, rb_pad)


def _token_mixer_ln1(x2, w_in, sinks, w_proj_a, w_proj_b, w_out, ln1_g, ln1_b, router_w, router_b, bsz, seq):
    n, d = x2.shape
    heads = jnp.arange(1, N_ALIBI_HEADS + 1, dtype=F32)
    slopes = jnp.exp2(-8.0 * heads / N_ALIBI_HEADS)
    *a_g, hq, hkv, gates_h = _in_proj(x2, w_in.astype(BF16), bsz, seq)
    o_g, lse_g = [], []
    for g in range(N_DIL):
        sl = slopes[SWA_Q_HEADS + g * DIL_HEADS:SWA_Q_HEADS + (g + 1) * DIL_HEADS]
        o, lse = _dil_attention(a_g[g], sl, g)
        o_g.append(o)
        lse_g.append(lse)
    out_b = _swa_attention(hq, hkv, slopes[:SWA_Q_HEADS], sinks.astype(F32), bsz, seq)
    rw_pad = jnp.zeros((d, LANES), BF16).at[:, :N_EXPERTS].set(router_w.astype(BF16))
    rb_pad = jnp.zeros((1, LANES), F32).at[0, :N_EXPERTS].set(router_b.astype(F32))
    return _mix_out(x2, o_g, lse_g, out_b, gates_h, w_proj_a.astype(BF16), w_proj_b.astype(BF16),
                    w_out.astype(BF16), ln1_g.reshape(1, d).astype(F32), ln1_b.reshape(1, d).astype(F32),
                    rw_pad, rb_pad, seq)


def _route_kernel(idx_ref, dest_ref, cnt_ref, carry_ref, pstart_ref, *, expert_block):
    phase, i = pl.program_id(0), pl.program_id(1)
    idx = idx_ref[...]
    t = idx.shape[0]
    lane = lax.broadcasted_iota(I32, (t, LANES), 1)
    onehot = [lane == idx[:, k:k + 1] for k in range(TOP_K)]
    multi = sum(oh.astype(F32) for oh in onehot)
    tile_cnt = jnp.sum(multi, axis=0, keepdims=True).astype(I32)

    @pl.when((phase == 0) & (i == 0))
    def _():
        carry_ref[...] = jnp.zeros_like(carry_ref)

    @pl.when((phase == 1) & (i == 0))
    def _():
        counts = carry_ref[...]
        padded = (counts + (expert_block - 1)) & (-expert_block)
        lane8 = lax.broadcasted_iota(I32, counts.shape, 1)
        incl = padded
        for sh in (1, 2, 4, 8, 16):
            incl = incl + jnp.where(lane8 >= sh, pltpu.roll(incl, sh, axis=1), 0)
        pstart_ref[...] = incl - padded
        cnt_ref[...] = counts
        carry_ref[...] = jnp.zeros_like(carry_ref)

    @pl.when(phase == 1)
    def _():
        r = lax.broadcasted_iota(I32, (t, t), 0)
        c = lax.broadcasted_iota(I32, (t, t), 1)
        earlier = (c < r).astype(BF16)
        cum = jnp.dot(earlier, multi.astype(BF16), preferred_element_type=F32).astype(I32)
        base = cum + carry_ref[0:1, :] + pstart_ref[0:1, :]
        dest = jnp.zeros((t, LANES), I32)
        for k in range(TOP_K):
            d_k = jnp.sum(jnp.where(onehot[k], base, 0), axis=-1, keepdims=True)
            dest = jnp.where(lane == k, d_k, dest)
        dest_ref[...] = dest[:, :TOP_K]

    carry_ref[...] = carry_ref[...] + tile_cnt


def _route(idx, expert_block, tile=512):
    n = idx.shape[0]
    tile = min(tile, n)
    return pl.pallas_call(
        functools.partial(_route_kernel, expert_block=expert_block),
        grid=(2, n // tile),
        in_specs=[pl.BlockSpec((tile, TOP_K), lambda p, i: (i, 0))],
        out_specs=[pl.BlockSpec((tile, TOP_K), lambda p, i: (i * p, 0)),
                   pl.BlockSpec((8, LANES), lambda p, i: (0, 0))],
        out_shape=[jax.ShapeDtypeStruct((n, TOP_K), I32), jax.ShapeDtypeStruct((8, LANES), I32)],
        scratch_shapes=[pltpu.VMEM((8, LANES), I32), pltpu.VMEM((8, LANES), I32)],
        compiler_params=_cparams("arbitrary", "arbitrary"),
        name="route",
    )(idx)


def _row_copy(src_hbm, src_row, dst, dst_row, sem):
    return pltpu.make_async_copy(src_hbm.at[pl.ds(src_row, 1)], dst.at[pl.ds(dst_row, 1)], sem)


def _dispatch_kernel(dest_ref, y_ref, xs_in_hbm, xs_hbm, sem, *, tile):
    del xs_in_hbm

    def issue(t, carry):
        for k in range(TOP_K):
            _row_copy(y_ref, t, xs_hbm, dest_ref[t * TOP_K + k], sem).start()
        return carry

    lax.fori_loop(0, tile, issue, 0)

    def drain(t, carry):
        _row_copy(y_ref, 0, xs_hbm, 0, sem).wait()
        return carry

    lax.fori_loop(0, tile * TOP_K, drain, 0)


def _dispatch(dest_flat, y, rows, tile=512):
    n, d = y.shape
    tile = min(tile, n)
    any_spec = pl.BlockSpec(memory_space=pl.ANY)
    return pl.pallas_call(
        functools.partial(_dispatch_kernel, tile=tile),
        grid=(n // tile,),
        in_specs=[pl.BlockSpec((tile * TOP_K,), lambda i: (i,), memory_space=pltpu.SMEM),
                  pl.BlockSpec((tile, d), lambda i: (i, 0)), any_spec],
        out_specs=any_spec,
        out_shape=jax.ShapeDtypeStruct((rows, d), y.dtype),
        scratch_shapes=[pltpu.SemaphoreType.DMA(())],
        input_output_aliases={2: 0},
        compiler_params=_cparams("arbitrary"),
        name="dispatch",
    )(dest_flat, y, jnp.zeros((rows, d), y.dtype))


def _sc_dispatch(dest_t, y, rows, window=32):
    n, d = y.shape
    info = plsc.get_sparse_core_info()
    n_cores = info.num_cores
    per_worker = n // (n_cores * info.num_subcores)
    mesh = plsc.VectorSubcoreMesh(core_axis_name="c", subcore_axis_name="s")

    @functools.partial(pl.kernel, mesh=mesh, out_type=jax.ShapeDtypeStruct((rows, d), y.dtype),
                       scratch_types=[pltpu.VMEM((TOP_K, window), I32), pltpu.VMEM((window, d), y.dtype)],
                       name="sc_dispatch")
    def body(dest_hbm, y_hbm, xs_hbm, idx_v, rows_v):
        base = (lax.axis_index("s") * n_cores + lax.axis_index("c")) * per_worker

        @pl.loop(0, per_worker // window)
        def _(c):
            t0 = base + c * window
            for k in range(TOP_K):
                pltpu.sync_copy(dest_hbm.at[k, pl.ds(t0, window)], idx_v.at[k])
            pltpu.sync_copy(y_hbm.at[pl.ds(t0, window)], rows_v)
            for k in range(TOP_K):
                pltpu.sync_copy(rows_v, xs_hbm.at[idx_v.at[k]])

    return body(dest_t, y)


def _expert_kernel(be_ref, nu_ref, x_ref, wg_ref, bg_ref, wu_ref, bu_ref, wd_ref, bd_ref, o_ref):
    del be_ref
    j = pl.program_id(0)

    @pl.when(j < nu_ref[0])
    def _():
        xb = x_ref[...].astype(BF16)
        gt = jnp.dot(xb, wg_ref[...], preferred_element_type=F32) + bg_ref[...]
        up = jnp.dot(xb, wu_ref[...], preferred_element_type=F32) + bu_ref[...]
        gt = jnp.minimum(gt, SWIGLU_LIMIT)
        up = jnp.clip(up, -SWIGLU_LIMIT, SWIGLU_LIMIT)
        hdn = gt * jax.nn.sigmoid(SWIGLU_ALPHA * gt) * (up + 1.0)
        o_ref[...] = jnp.dot(hdn.astype(BF16), wd_ref[...], preferred_element_type=F32) + bd_ref[...]

    @pl.when(j >= nu_ref[0])
    def _():
        o_ref[...] = jnp.zeros_like(o_ref)


def _experts(block_e, n_used, xs, wg, bg, wu, bu, wd, bd, expert_block):
    rows, d = xs.shape
    e, _, f = wg.shape
    xmap = lambda j, be, nu: (jnp.minimum(j, nu[0] - 1), 0)
    wmap = lambda j, be, nu: (be[j], 0, 0)
    grid_spec = pltpu.PrefetchScalarGridSpec(
        num_scalar_prefetch=2,
        grid=(rows // expert_block,),
        in_specs=[pl.BlockSpec((expert_block, d), xmap),
                  pl.BlockSpec((None, d, f), wmap), pl.BlockSpec((None, 1, f), wmap),
                  pl.BlockSpec((None, d, f), wmap), pl.BlockSpec((None, 1, f), wmap),
                  pl.BlockSpec((None, f, d), wmap), pl.BlockSpec((None, 1, d), wmap)],
        out_specs=pl.BlockSpec((expert_block, d), lambda j, be, nu: (j, 0)),
    )
    return pl.pallas_call(
        _expert_kernel,
        grid_spec=grid_spec,
        out_shape=jax.ShapeDtypeStruct((rows, d), F32),
        compiler_params=_cparams("arbitrary"),
        name="experts",
    )(block_e, n_used, xs, wg, bg.reshape(e, 1, f), wu, bu.reshape(e, 1, f), wd, bd.reshape(e, 1, d))


def _combine_kernel(dest_ref, gate_ref, y_ref, lng_ref, lnb_ref, ys_hbm, o_ref, buf_ref, sem, *, tile):
    def issue(t, carry):
        for k in range(TOP_K):
            _row_copy(ys_hbm, dest_ref[t * TOP_K + k], buf_ref.at[k], t, sem).start()
        return carry

    lax.fori_loop(0, tile, issue, 0)

    def drain(t, carry):
        _row_copy(ys_hbm, 0, buf_ref.at[0], 0, sem).wait()
        return carry

    lax.fori_loop(0, tile * TOP_K, drain, 0)

    gates = gate_ref[...]
    ffn = gates[:, 0:1] * buf_ref[0]
    for k in range(1, TOP_K):
        ffn = ffn + gates[:, k:k + 1] * buf_ref[k]
    o_ref[...] = _layer_norm(DEEPNORM_ALPHA * y_ref[...] + ffn, lng_ref[...], lnb_ref[...])


def _combine(dest_flat, gates, y, ln_g, ln_b, ys, tile=256):
    n, d = y.shape
    tile = min(tile, n)
    row = lambda i: (i, 0)
    const = lambda i: (0, 0)
    return pl.pallas_call(
        functools.partial(_combine_kernel, tile=tile),
        grid=(n // tile,),
        in_specs=[pl.BlockSpec((tile * TOP_K,), lambda i: (i,), memory_space=pltpu.SMEM),
                  pl.BlockSpec((tile, TOP_K), row), pl.BlockSpec((tile, d), row),
                  pl.BlockSpec((1, d), const), pl.BlockSpec((1, d), const),
                  pl.BlockSpec(memory_space=pl.ANY)],
        out_specs=pl.BlockSpec((tile, d), row),
        out_shape=jax.ShapeDtypeStruct((n, d), F32),
        scratch_shapes=[pltpu.VMEM((TOP_K, tile, d), F32), pltpu.SemaphoreType.DMA(())],
        compiler_params=_cparams("arbitrary"),
        name="combine",
    )(dest_flat, gates, y, ln_g, ln_b, ys)


MOE_ROW_BLOCK = 256


def _moe_ln2(y, idx, gates, w_gate, b_gate, w_up, b_up, w_down, b_down, ln2_g, ln2_b):
    n, d = y.shape
    n_blocks = n * TOP_K // MOE_ROW_BLOCK + N_EXPERTS
    rows = n_blocks * MOE_ROW_BLOCK
    dest, counts = _route(idx, MOE_ROW_BLOCK)
    cnt = counts[0, :N_EXPERTS]
    padded = (cnt + MOE_ROW_BLOCK - 1) // MOE_ROW_BLOCK * MOE_ROW_BLOCK
    p_end = jnp.cumsum(padded)
    block_row0 = jnp.arange(n_blocks, dtype=I32) * MOE_ROW_BLOCK
    block_e = jnp.minimum(jnp.sum(p_end[None, :] <= block_row0[:, None], axis=1), N_EXPERTS - 1).astype(I32)
    n_used = (p_end[-1:] // MOE_ROW_BLOCK).astype(I32)
    dest_flat = dest.reshape(-1)
    xs = _sc_dispatch(dest.T, y, rows)
    ys = _experts(block_e, n_used, xs, w_gate.astype(BF16), b_gate.astype(F32), w_up.astype(BF16),
                  b_up.astype(F32), w_down.astype(BF16), b_down.astype(F32), MOE_ROW_BLOCK)
    return _combine(dest_flat, gates, y, ln2_g.reshape(1, d).astype(F32), ln2_b.reshape(1, d).astype(F32), ys)


def kernel(x, w_in, sinks, w_proj_a, w_proj_b, w_out, ln1_g, ln1_b, router_w, router_b,
           w_gate, b_gate, w_up, b_up, w_down, b_down, ln2_g, ln2_b):
    bsz, seq, d = x.shape
    h = x.reshape(bsz * seq, d)
    for l in range(w_in.shape[0]):
        y, idx, gates = _token_mixer_ln1(h, w_in[l], sinks[l], w_proj_a[l], w_proj_b[l], w_out[l],
                                         ln1_g[l], ln1_b[l], router_w[l], router_b[l], bsz, seq)
        h = _moe_ln2(y, idx, gates, w_gate[l], b_gate[l], w_up[l], b_up[l], w_down[l], b_down[l],
                     ln2_g[l], ln2_b[l])
    return h.reshape(bsz, seq, d)
```

```python
import functools

import jax
import jax.numpy as jnp
from jax import lax
from jax.experimental import pallas as pl
from jax.experimental.pallas import tpu as pltpu
from jax.experimental.pallas import tpu_sc as plsc

F32 = jnp.float32
BF16 = jnp.bfloat16
I32 = jnp.int32

HEAD_DIM = 64
DIL_GROUPS = ((128, 1), (512, 4), (2048, 16))
DIL_HEADS = 4
N_DIL = len(DIL_GROUPS)
DIL_OUT = DIL_HEADS * HEAD_DIM
DIL_WIDTH = N_DIL * DIL_OUT
SWA_Q_HEADS = 16
SWA_KV_HEADS = 2
SWA_REP = SWA_Q_HEADS // SWA_KV_HEADS
SWA_WINDOW = 128
SWA_Q_WIDTH = SWA_Q_HEADS * HEAD_DIM
SWA_KV_WIDTH = SWA_KV_HEADS * HEAD_DIM
N_ALIBI_HEADS = SWA_Q_HEADS + N_DIL * DIL_HEADS
ATTN_BLOCK = 128
N_EXPERTS = 32
TOP_K = 4
SWIGLU_LIMIT = 7.0
SWIGLU_ALPHA = 1.702
LN_EPS = 1e-5
DEPTH = 1
DEEPNORM_ALPHA = (2 * DEPTH) ** 0.25
NEG_INF = -1e30

LANES = 128
VMEM_LIMIT_BYTES = 56 * 1024 * 1024

A_QKV_W = 3 * DIL_WIDTH
B_Q_OFF = A_QKV_W
B_KV_OFF = B_Q_OFF + SWA_Q_WIDTH
GATE_OFF = B_KV_OFF + 2 * SWA_KV_WIDTH


def _cparams(*sem):
    return pltpu.CompilerParams(dimension_semantics=sem, vmem_limit_bytes=VMEM_LIMIT_BYTES)


def _in_proj_kernel(x_ref, w_ref, a0_ref, a1_ref, a2_ref, hq_ref, hkv_ref, g_ref, *scratch, d_model, tm):
    xb = x_ref[...].astype(BF16)
    segments = ((hq_ref, B_Q_OFF, SWA_Q_WIDTH), (hkv_ref, B_KV_OFF, 2 * SWA_KV_WIDTH),
                (g_ref, GATE_OFF, 2 * d_model))
    for out_ref, col0, width in segments:
        for c in range(0, width, 512):
            w = min(512, width - c)
            r = jnp.dot(xb, w_ref[:, col0 + c:col0 + c + w], preferred_element_type=F32)
            out_ref[:, c:c + w] = r.astype(out_ref.dtype)
    for g, a_ref in enumerate((a0_ref, a1_ref, a2_ref)):
        dil = DIL_GROUPS[g][1]
        per = tm // dil
        for part in range(3):
            col0 = part * DIL_WIDTH + g * DIL_OUT
            res = jnp.dot(xb, w_ref[:, col0:col0 + DIL_OUT], preferred_element_type=F32)
            if dil == 1:
                a_ref[0, :, part * DIL_OUT:(part + 1) * DIL_OUT] = res.astype(a_ref.dtype)
                continue
            for half in range(DIL_OUT // LANES):
                stage = scratch[part * (DIL_OUT // LANES) + half]
                stage[...] = res[:, half * LANES:(half + 1) * LANES]
                c0 = part * DIL_OUT + half * LANES
                for r in range(dil):
                    a_ref[r, :, c0:c0 + LANES] = stage[pl.ds(r, per, stride=dil), :].astype(a_ref.dtype)


def _in_proj(x2, w_in_bf, bsz, seq, tm=512):
    n, d = x2.shape
    cols = w_in_bf.shape[1]
    tiles = seq // tm
    row = lambda i: (i, 0)
    dil_spec = lambda dil: pl.BlockSpec((None, dil, tm // dil, 3 * DIL_OUT), lambda i: (i // tiles, 0, i % tiles, 0))
    dil_shape = lambda dil: jax.ShapeDtypeStruct((bsz, dil, seq // dil, 3 * DIL_OUT), BF16)
    dils = [dil for _, dil in DIL_GROUPS]
    return pl.pallas_call(
        functools.partial(_in_proj_kernel, d_model=d, tm=tm),
        grid=(n // tm,),
        in_specs=[pl.BlockSpec((tm, d), row),
                  pl.BlockSpec((d, cols), lambda i: (0, 0), pipeline_mode=pl.Buffered(1))],
        out_specs=[dil_spec(dil) for dil in dils]
        + [pl.BlockSpec((tm, SWA_Q_WIDTH), row), pl.BlockSpec((tm, 2 * SWA_KV_WIDTH), row),
           pl.BlockSpec((tm, 2 * d), row)],
        out_shape=[dil_shape(dil) for dil in dils]
        + [jax.ShapeDtypeStruct((n, SWA_Q_WIDTH), BF16), jax.ShapeDtypeStruct((n, 2 * SWA_KV_WIDTH), BF16),
           jax.ShapeDtypeStruct((n, 2 * d), BF16)],
        scratch_shapes=[pltpu.VMEM((tm, LANES), F32)] * (3 * DIL_OUT // LANES),
        compiler_params=_cparams("parallel"),
        name="in_proj",
    )(x2, w_in_bf)


def _band_tables(max_diff):
    qi = lax.broadcasted_iota(I32, (ATTN_BLOCK, ATTN_BLOCK), 0)
    kj = lax.broadcasted_iota(I32, (ATTN_BLOCK, ATTN_BLOCK), 1)
    diff_prev = qi - kj + ATTN_BLOCK
    diff_cur = qi - kj
    valid_prev = diff_prev <= max_diff
    valid_cur = (diff_cur >= 0) & (diff_cur <= max_diff)
    return diff_prev.astype(F32), diff_cur.astype(F32), valid_prev, valid_cur


def _attend(q, k_prev, k_cur, v_prev, v_cur, tables, prev_ok, slope_scaled, sink):
    diff_prev, diff_cur, valid_prev, valid_cur = tables
    nt = (((1,), (1,)), ((), ()))
    s_prev = lax.dot_general(q, k_prev, nt, preferred_element_type=F32)
    s_cur = lax.dot_general(q, k_cur, nt, preferred_element_type=F32)
    s_prev = jnp.where(valid_prev & prev_ok, s_prev - slope_scaled * diff_prev, NEG_INF)
    s_cur = jnp.where(valid_cur, s_cur - slope_scaled * diff_cur, NEG_INF)
    m = jnp.max(jnp.maximum(s_prev, s_cur), axis=-1, keepdims=True)
    if sink is not None:
        m = jnp.maximum(m, sink)
    p_prev = jnp.exp(s_prev - m)
    p_cur = jnp.exp(s_cur - m)
    denom = jnp.sum(p_prev + p_cur, axis=-1, keepdims=True)
    if sink is not None:
        denom = denom + jnp.exp(sink - m)
    o = (jnp.dot(p_prev.astype(BF16), v_prev, preferred_element_type=F32)
         + jnp.dot(p_cur.astype(BF16), v_cur, preferred_element_type=F32))
    return o / denom, m + jnp.log(denom)


def _dil_attn_kernel(slope_ref, q_ref, kc_ref, kp_ref, vc_ref, vp_ref, o_ref, lse_ref, *, tq, max_diff, dist_scale):
    first = pl.program_id(2) == 0
    tables = _band_tables(max_diff)
    for i in range(tq // ATTN_BLOCK):
        rows = slice(i * ATTN_BLOCK, (i + 1) * ATTN_BLOCK)
        prev_rows = slice((i - 1) * ATTN_BLOCK, i * ATTN_BLOCK)
        prev_ok = jnp.logical_not(first) if i == 0 else True
        for h in range(DIL_HEADS):
            cols = slice(h * HEAD_DIM, (h + 1) * HEAD_DIM)
            q = q_ref[rows, cols] * 0.125
            k_prev = kp_ref[:, cols] if i == 0 else kc_ref[prev_rows, cols]
            v_prev = vp_ref[:, cols] if i == 0 else vc_ref[prev_rows, cols]
            o, lse = _attend(q, k_prev, kc_ref[rows, cols], v_prev, vc_ref[rows, cols], tables,
                             prev_ok, slope_ref[h] * dist_scale, None)
            o_ref[rows, cols] = o.astype(o_ref.dtype)
            lse_ref[rows, cols] = jnp.broadcast_to(lse, (ATTN_BLOCK, HEAD_DIM))


def _dil_attention(a_g, slopes_g, g):
    window, dil = DIL_GROUPS[g]
    bsz, _, sub_len, _ = a_g.shape
    tq = min(512, sub_len)
    nqb = tq // ATTN_BLOCK
    cur = lambda part: (lambda b, r, m: (b, r, m, part))
    prev = lambda part: (lambda b, r, m: (b, r, jnp.maximum(m * nqb - 1, 0), part))
    blk = lambda rows, imap: pl.BlockSpec((None, None, rows, DIL_OUT), imap)
    return pl.pallas_call(
        functools.partial(_dil_attn_kernel, tq=tq, max_diff=window // dil, dist_scale=float(dil)),
        grid=(bsz, dil, sub_len // tq),
        in_specs=[pl.BlockSpec(memory_space=pltpu.SMEM),
                  blk(tq, cur(0)), blk(tq, cur(1)), blk(ATTN_BLOCK, prev(1)),
                  blk(tq, cur(2)), blk(ATTN_BLOCK, prev(2))],
        out_specs=[blk(tq, cur(0)), blk(tq, cur(0))],
        out_shape=[jax.ShapeDtypeStruct((bsz, dil, sub_len, DIL_OUT), BF16),
                   jax.ShapeDtypeStruct((bsz, dil, sub_len, DIL_OUT), F32)],
        compiler_params=_cparams("parallel", "parallel", "arbitrary"),
        name=f"dil_attn_g{g}",
    )(slopes_g, a_g, a_g, a_g, a_g, a_g)


def _swa_attn_kernel(slope_ref, sink_ref, q_ref, kvc_ref, kvp_ref, o_ref, *, tq):
    first = pl.program_id(1) == 0
    tables = _band_tables(SWA_WINDOW - 1)
    for i in range(tq // ATTN_BLOCK):
        rows = slice(i * ATTN_BLOCK, (i + 1) * ATTN_BLOCK)
        prev_rows = slice((i - 1) * ATTN_BLOCK, i * ATTN_BLOCK)
        prev_ok = jnp.logical_not(first) if i == 0 else True
        for h in range(SWA_Q_HEADS):
            kvh = h // SWA_REP
            kcols = slice(kvh * HEAD_DIM, (kvh + 1) * HEAD_DIM)
            vcols = slice(SWA_KV_WIDTH + kvh * HEAD_DIM, SWA_KV_WIDTH + (kvh + 1) * HEAD_DIM)
            cols = slice(h * HEAD_DIM, (h + 1) * HEAD_DIM)
            q = q_ref[rows, cols] * 0.125
            k_prev = kvp_ref[:, kcols] if i == 0 else kvc_ref[prev_rows, kcols]
            v_prev = kvp_ref[:, vcols] if i == 0 else kvc_ref[prev_rows, vcols]
            o, _ = _attend(q, k_prev, kvc_ref[rows, kcols], v_prev, kvc_ref[rows, vcols], tables,
                           prev_ok, slope_ref[h], sink_ref[h])
            o_ref[rows, cols] = o.astype(o_ref.dtype)


def _swa_attention(hq, hkv, slopes_b, sinks, bsz, seq, tq=512):
    nqb = tq // ATTN_BLOCK
    hq3 = hq.reshape(bsz, seq, SWA_Q_WIDTH)
    hkv3 = hkv.reshape(bsz, seq, 2 * SWA_KV_WIDTH)
    smem = pl.BlockSpec(memory_space=pltpu.SMEM)
    out = pl.pallas_call(
        functools.partial(_swa_attn_kernel, tq=tq),
        grid=(bsz, seq // tq),
        in_specs=[smem, smem,
                  pl.BlockSpec((None, tq, SWA_Q_WIDTH), lambda b, m: (b, m, 0)),
                  pl.BlockSpec((None, tq, 2 * SWA_KV_WIDTH), lambda b, m: (b, m, 0)),
                  pl.BlockSpec((None, ATTN_BLOCK, 2 * SWA_KV_WIDTH),
                               lambda b, m: (b, jnp.maximum(m * nqb - 1, 0), 0))],
        out_specs=pl.BlockSpec((None, tq, SWA_Q_WIDTH), lambda b, m: (b, m, 0)),
        out_shape=jax.ShapeDtypeStruct((bsz, seq, SWA_Q_WIDTH), BF16),
        compiler_params=_cparams("parallel", "arbitrary"),
        name="swa_attn",
    )(slopes_b, sinks, hq3, hkv3, hkv3)
    return out.reshape(bsz * seq, SWA_Q_WIDTH)


def _layer_norm(z, g, b):
    mu = jnp.mean(z, axis=-1, keepdims=True)
    zc = z - mu
    var = jnp.mean(zc * zc, axis=-1, keepdims=True)
    return zc * lax.rsqrt(var + LN_EPS) * g + b


def _mix_out_kernel(x_ref, o0_ref, o1_ref, o2_ref, l0_ref, l1_ref, l2_ref, ob_ref, g_ref,
                    wpa_ref, wpb_ref, wo_ref, lng_ref, lnb_ref, rw_ref, rb_ref,
                    y_ref, idx_ref, gate_ref, *scratch, d_model):
    def natural(ref, stages):
        dil, per, _ = ref.shape
        if dil == 1:
            return ref[0].astype(F32)
        for half, stage in enumerate(stages):
            for r in range(dil):
                stage[pl.ds(r, per, stride=dil), :] = ref[r, :, half * LANES:(half + 1) * LANES].astype(F32)
        return jnp.concatenate([stage[...] for stage in stages], axis=1)

    o0, o1, o2 = natural(o0_ref, None), natural(o1_ref, scratch[0:2]), natural(o2_ref, scratch[2:4])
    l0, l1, l2 = natural(l0_ref, None), natural(l1_ref, scratch[4:6]), natural(l2_ref, scratch[6:8])
    lm = jnp.maximum(jnp.maximum(l0, l1), l2)
    e0, e1, e2 = jnp.exp(l0 - lm), jnp.exp(l1 - lm), jnp.exp(l2 - lm)
    esum = e0 + e1 + e2
    out_a = (e0 / esum) * o0 + (e1 / esum) * o1 + (e2 / esum) * o2
    pa = jnp.dot(out_a.astype(BF16), wpa_ref[...], preferred_element_type=F32)
    pb = jnp.dot(ob_ref[...], wpb_ref[...], preferred_element_type=F32)
    ga = jax.nn.sigmoid(g_ref[:, :d_model].astype(F32))
    gb = jax.nn.sigmoid(g_ref[:, d_model:].astype(F32))
    merged = ga * pa + gb * pb
    mix = jnp.dot(merged.astype(BF16), wo_ref[...], preferred_element_type=F32)
    y = _layer_norm(DEEPNORM_ALPHA * x_ref[...] + mix, lng_ref[...], lnb_ref[...])
    y_ref[...] = y

    logits = jnp.dot(y.astype(BF16), rw_ref[...], preferred_element_type=F32) + rb_ref[...]
    lane = lax.broadcasted_iota(I32, logits.shape, 1)
    logits = jnp.where(lane < N_EXPERTS, logits, -jnp.inf)
    idx_out = jnp.zeros(logits.shape, I32)
    val_out = jnp.full(logits.shape, -jnp.inf, F32)
    for k in range(TOP_K):
        top = jnp.max(logits, axis=-1, keepdims=True)
        top_idx = jnp.min(jnp.where(logits == top, lane, LANES), axis=-1, keepdims=True)
        idx_out = jnp.where(lane == k, top_idx, idx_out)
        val_out = jnp.where(lane == k, top, val_out)
        logits = jnp.where(lane == top_idx, -jnp.inf, logits)
    ev = jnp.exp(val_out - jnp.max(val_out, axis=-1, keepdims=True))
    gates = ev / jnp.sum(ev, axis=-1, keepdims=True)
    idx_ref[...] = idx_out[:, :TOP_K]
    gate_ref[...] = gates[:, :TOP_K]


def _mix_out(x2, o_g, lse_g, out_b, gates_h, wpa, wpb, wo, ln_g, ln_b, rw_pad, rb_pad, seq, tm=256):
    n, d = x2.shape
    tiles = seq // tm
    row = lambda i: (i, 0)
    const = lambda i: (0, 0)
    rb = lambda w: pl.BlockSpec((tm, w), row)
    full = lambda a: pl.BlockSpec(a.shape, const)
    dil_specs = [pl.BlockSpec((None, dil, tm // dil, DIL_OUT), lambda i: (i // tiles, 0, i % tiles, 0))
                 for _, dil in DIL_GROUPS]
    return pl.pallas_call(
        functools.partial(_mix_out_kernel, d_model=d),
        grid=(n // tm,),
        in_specs=[rb(d)] + dil_specs + dil_specs + [rb(SWA_Q_WIDTH), rb(2 * d),
                  full(wpa), full(wpb), full(wo), full(ln_g), full(ln_b), full(rw_pad), full(rb_pad)],
        out_specs=[rb(d), rb(TOP_K), rb(TOP_K)],
        out_shape=[jax.ShapeDtypeStruct((n, d), F32),
                   jax.ShapeDtypeStruct((n, TOP_K), I32), jax.ShapeDtypeStruct((n, TOP_K), F32)],
        scratch_shapes=[pltpu.VMEM((tm, LANES), F32)] * (4 * DIL_OUT // LANES),
        compiler_params=_cparams("parallel"),
        name="mix_out",
    )(x2, *o_g, *lse_g, out_b, gates_h, wpa, wpb, wo, ln_g, ln_b, rw_pad, rb_pad)


def _token_mixer_ln1(x2, w_in, sinks, w_proj_a, w_proj_b, w_out, ln1_g, ln1_b, router_w, router_b, bsz, seq):
    n, d = x2.shape
    heads = jnp.arange(1, N_ALIBI_HEADS + 1, dtype=F32)
    slopes = jnp.exp2(-8.0 * heads / N_ALIBI_HEADS)
    *a_g, hq, hkv, gates_h = _in_proj(x2, w_in.astype(BF16), bsz, seq)
    o_g, lse_g = [], []
    for g in range(N_DIL):
        sl = slopes[SWA_Q_HEADS + g * DIL_HEADS:SWA_Q_HEADS + (g + 1) * DIL_HEADS]
        o, lse = _dil_attention(a_g[g], sl, g)
        o_g.append(o)
        lse_g.append(lse)
    out_b = _swa_attention(hq, hkv, slopes[:SWA_Q_HEADS], sinks.astype(F32), bsz, seq)
    rw_pad = jnp.zeros((d, LANES), BF16).at[:, :N_EXPERTS].set(router_w.astype(BF16))
    rb_pad = jnp.zeros((1, LANES), F32).at[0, :N_EXPERTS].set(router_b.astype(F32))
    return _mix_out(x2, o_g, lse_g, out_b, gates_h, w_proj_a.astype(BF16), w_proj_b.astype(BF16),
                    w_out.astype(BF16), ln1_g.reshape(1, d).astype(F32), ln1_b.reshape(1, d).astype(F32),
                    rw_pad, rb_pad, seq)


def _route_kernel(idx_ref, dest_ref, cnt_ref, carry_ref, pstart_ref, *, expert_block):
    phase, i = pl.program_id(0), pl.program_id(1)
    idx = idx_ref[...]
    t = idx.shape[0]
    lane = lax.broadcasted_iota(I32, (t, LANES), 1)
    onehot = [lane == idx[:, k:k + 1] for k in range(TOP_K)]
    multi = sum(oh.astype(F32) for oh in onehot)
    tile_cnt = jnp.sum(multi, axis=0, keepdims=True).astype(I32)

    @pl.when((phase == 0) & (i == 0))
    def _():
        carry_ref[...] = jnp.zeros_like(carry_ref)

    @pl.when((phase == 1) & (i == 0))
    def _():
        counts = carry_ref[...]
        padded = (counts + (expert_block - 1)) & (-expert_block)
        lane8 = lax.broadcasted_iota(I32, counts.shape, 1)
        incl = padded
        for sh in (1, 2, 4, 8, 16):
            incl = incl + jnp.where(lane8 >= sh, pltpu.roll(incl, sh, axis=1), 0)
        pstart_ref[...] = incl - padded
        cnt_ref[...] = counts
        carry_ref[...] = jnp.zeros_like(carry_ref)

    @pl.when(phase == 1)
    def _():
        r = lax.broadcasted_iota(I32, (t, t), 0)
        c = lax.broadcasted_iota(I32, (t, t), 1)
        earlier = (c < r).astype(BF16)
        cum = jnp.dot(earlier, multi.astype(BF16), preferred_element_type=F32).astype(I32)
        base = cum + carry_ref[0:1, :] + pstart_ref[0:1, :]
        dest = jnp.zeros((t, LANES), I32)
        for k in range(TOP_K):
            d_k = jnp.sum(jnp.where(onehot[k], base, 0), axis=-1, keepdims=True)
            dest = jnp.where(lane == k, d_k, dest)
        dest_ref[...] = dest[:, :TOP_K]

    carry_ref[...] = carry_ref[...] + tile_cnt


def _route(idx, expert_block, tile=512):
    n = idx.shape[0]
    tile = min(tile, n)
    return pl.pallas_call(
        functools.partial(_route_kernel, expert_block=expert_block),
        grid=(2, n // tile),
        in_specs=[pl.BlockSpec((tile, TOP_K), lambda p, i: (i, 0))],
        out_specs=[pl.BlockSpec((tile, TOP_K), lambda p, i: (i * p, 0)),
                   pl.BlockSpec((8, LANES), lambda p, i: (0, 0))],
        out_shape=[jax.ShapeDtypeStruct((n, TOP_K), I32), jax.ShapeDtypeStruct((8, LANES), I32)],
        scratch_shapes=[pltpu.VMEM((8, LANES), I32), pltpu.VMEM((8, LANES), I32)],
        compiler_params=_cparams("arbitrary", "arbitrary"),
        name="route",
    )(idx)


def _row_copy(src_hbm, src_row, dst, dst_row, sem):
    return pltpu.make_async_copy(src_hbm.at[pl.ds(src_row, 1)], dst.at[pl.ds(dst_row, 1)], sem)


def _dispatch_kernel(dest_ref, y_ref, xs_in_hbm, xs_hbm, sem, *, tile):
    del xs_in_hbm

    def issue(t, carry):
        for k in range(TOP_K):
            _row_copy(y_ref, t, xs_hbm, dest_ref[t * TOP_K + k], sem).start()
        return carry

    lax.fori_loop(0, tile, issue, 0)

    def drain(t, carry):
        _row_copy(y_ref, 0, xs_hbm, 0, sem).wait()
        return carry

    lax.fori_loop(0, tile * TOP_K, drain, 0)


def _dispatch(dest_flat, y, rows, tile=512):
    n, d = y.shape
    tile = min(tile, n)
    any_spec = pl.BlockSpec(memory_space=pl.ANY)
    return pl.pallas_call(
        functools.partial(_dispatch_kernel, tile=tile),
        grid=(n // tile,),
        in_specs=[pl.BlockSpec((tile * TOP_K,), lambda i: (i,), memory_space=pltpu.SMEM),
                  pl.BlockSpec((tile, d), lambda i: (i, 0)), any_spec],
        out_specs=any_spec,
        out_shape=jax.ShapeDtypeStruct((rows, d), y.dtype),
        scratch_shapes=[pltpu.SemaphoreType.DMA(())],
        input_output_aliases={2: 0},
        compiler_params=_cparams("arbitrary"),
        name="dispatch",
    )(dest_flat, y, jnp.zeros((rows, d), y.dtype))


def _sc_dispatch(dest_t, y, rows, window=32):
    n, d = y.shape
    info = plsc.get_sparse_core_info()
    n_cores = info.num_cores
    per_worker = n // (n_cores * info.num_subcores)
    mesh = plsc.VectorSubcoreMesh(core_axis_name="c", subcore_axis_name="s")

    @functools.partial(pl.kernel, mesh=mesh, out_type=jax.ShapeDtypeStruct((rows, d), y.dtype),
                       scratch_types=[pltpu.VMEM((TOP_K, window), I32), pltpu.VMEM((window, d), y.dtype)],
                       name="sc_dispatch")
    def body(dest_hbm, y_hbm, xs_hbm, idx_v, rows_v):
        base = (lax.axis_index("s") * n_cores + lax.axis_index("c")) * per_worker

        @pl.loop(0, per_worker // window)
        def _(c):
            t0 = base + c * window
            for k in range(TOP_K):
                pltpu.sync_copy(dest_hbm.at[k, pl.ds(t0, window)], idx_v.at[k])
            pltpu.sync_copy(y_hbm.at[pl.ds(t0, window)], rows_v)
            for k in range(TOP_K):
                pltpu.sync_copy(rows_v, xs_hbm.at[idx_v.at[k]])

    return body(dest_t, y)


def _expert_kernel(be_ref, nu_ref, nv_ref, x_ref, wg_ref, bg_ref, wu_ref, bu_ref, wd_ref, bd_ref, o_ref):
    del be_ref
    j = pl.program_id(0)

    @pl.when(j < nu_ref[0])
    def _():
        row = lax.broadcasted_iota(I32, x_ref.shape, 0)
        xb = jnp.where(row < nv_ref[j], x_ref[...], 0.0).astype(BF16)
        gt = jnp.dot(xb, wg_ref[...], preferred_element_type=F32) + bg_ref[...]
        up = jnp.dot(xb, wu_ref[...], preferred_element_type=F32) + bu_ref[...]
        gt = jnp.minimum(gt, SWIGLU_LIMIT)
        up = jnp.clip(up, -SWIGLU_LIMIT, SWIGLU_LIMIT)
        hdn = gt * jax.nn.sigmoid(SWIGLU_ALPHA * gt) * (up + 1.0)
        o_ref[...] = jnp.dot(hdn.astype(BF16), wd_ref[...], preferred_element_type=F32) + bd_ref[...]

    @pl.when(j >= nu_ref[0])
    def _():
        o_ref[...] = jnp.zeros_like(o_ref)


def _experts(block_e, n_used, n_valid, xs, wg, bg, wu, bu, wd, bd, expert_block):
    rows, d = xs.shape
    e, _, f = wg.shape
    xmap = lambda j, be, nu, nv: (jnp.minimum(j, nu[0] - 1), 0)
    wmap = lambda j, be, nu, nv: (be[j], 0, 0)
    grid_spec = pltpu.PrefetchScalarGridSpec(
        num_scalar_prefetch=3,
        grid=(rows // expert_block,),
        in_specs=[pl.BlockSpec((expert_block, d), xmap),
                  pl.BlockSpec((None, d, f), wmap), pl.BlockSpec((None, 1, f), wmap),
                  pl.BlockSpec((None, d, f), wmap), pl.BlockSpec((None, 1, f), wmap),
                  pl.BlockSpec((None, f, d), wmap), pl.BlockSpec((None, 1, d), wmap)],
        out_specs=pl.BlockSpec((expert_block, d), lambda j, be, nu, nv: (j, 0)),
    )
    return pl.pallas_call(
        _expert_kernel,
        grid_spec=grid_spec,
        out_shape=jax.ShapeDtypeStruct((rows, d), F32),
        compiler_params=_cparams("arbitrary"),
        name="experts",
    )(block_e, n_used, n_valid, xs, wg, bg.reshape(e, 1, f), wu, bu.reshape(e, 1, f), wd, bd.reshape(e, 1, d))


def _combine_kernel(dest_ref, gate_ref, y_ref, lng_ref, lnb_ref, ys_hbm, o_ref, buf_ref, sem, *, tile):
    def issue(t, carry):
        for k in range(TOP_K):
            _row_copy(ys_hbm, dest_ref[t * TOP_K + k], buf_ref.at[k], t, sem).start()
        return carry

    lax.fori_loop(0, tile, issue, 0)

    def drain(t, carry):
        _row_copy(ys_hbm, 0, buf_ref.at[0], 0, sem).wait()
        return carry

    lax.fori_loop(0, tile * TOP_K, drain, 0)

    gates = gate_ref[...]
    ffn = gates[:, 0:1] * buf_ref[0]
    for k in range(1, TOP_K):
        ffn = ffn + gates[:, k:k + 1] * buf_ref[k]
    o_ref[...] = _layer_norm(DEEPNORM_ALPHA * y_ref[...] + ffn, lng_ref[...], lnb_ref[...])


def _combine(dest_flat, gates, y, ln_g, ln_b, ys, tile=256):
    n, d = y.shape
    tile = min(tile, n)
    row = lambda i: (i, 0)
    const = lambda i: (0, 0)
    return pl.pallas_call(
        functools.partial(_combine_kernel, tile=tile),
        grid=(n // tile,),
        in_specs=[pl.BlockSpec((tile * TOP_K,), lambda i: (i,), memory_space=pltpu.SMEM),
                  pl.BlockSpec((tile, TOP_K), row), pl.BlockSpec((tile, d), row),
                  pl.BlockSpec((1, d), const), pl.BlockSpec((1, d), const),
                  pl.BlockSpec(memory_space=pl.ANY)],
        out_specs=pl.BlockSpec((tile, d), row),
        out_shape=jax.ShapeDtypeStruct((n, d), F32),
        scratch_shapes=[pltpu.VMEM((TOP_K, tile, d), F32), pltpu.SemaphoreType.DMA(())],
        compiler_params=_cparams("arbitrary"),
        name="combine",
    )(dest_flat, gates, y, ln_g, ln_b, ys)


def _sc_gather(dest_t, ys, n, window=32):
    _, d = ys.shape
    info = plsc.get_sparse_core_info()
    n_cores = info.num_cores
    per_worker = n // (n_cores * info.num_subcores)
    mesh = plsc.VectorSubcoreMesh(core_axis_name="c", subcore_axis_name="s")

    @functools.partial(pl.kernel, mesh=mesh, out_type=jax.ShapeDtypeStruct((TOP_K, n, d), ys.dtype),
                       scratch_types=[pltpu.VMEM((TOP_K, window), I32), pltpu.VMEM((window, d), ys.dtype)],
                       name="sc_gather")
    def body(dest_hbm, ys_hbm, out_hbm, idx_v, rows_v):
        base = (lax.axis_index("s") * n_cores + lax.axis_index("c")) * per_worker

        @pl.loop(0, per_worker // window)
        def _(c):
            t0 = base + c * window
            for k in range(TOP_K):
                pltpu.sync_copy(dest_hbm.at[k, pl.ds(t0, window)], idx_v.at[k])
            for k in range(TOP_K):
                pltpu.sync_copy(ys_hbm.at[idx_v.at[k]], rows_v)
                pltpu.sync_copy(rows_v, out_hbm.at[k, pl.ds(t0, window)])

    return body(dest_t, ys)


def _combine_dense_kernel(gate_ref, y_ref, lng_ref, lnb_ref, g_ref, o_ref):
    gates = gate_ref[...]
    ffn = gates[:, 0:1] * g_ref[0]
    for k in range(1, TOP_K):
        ffn = ffn + gates[:, k:k + 1] * g_ref[k]
    o_ref[...] = _layer_norm(DEEPNORM_ALPHA * y_ref[...] + ffn, lng_ref[...], lnb_ref[...])


def _combine_dense(gates, y, ln_g, ln_b, gathered, tile=512):
    n, d = y.shape
    tile = min(tile, n)
    row = lambda i: (i, 0)
    const = lambda i: (0, 0)
    return pl.pallas_call(
        _combine_dense_kernel,
        grid=(n // tile,),
        in_specs=[pl.BlockSpec((tile, TOP_K), row), pl.BlockSpec((tile, d), row),
                  pl.BlockSpec((1, d), const), pl.BlockSpec((1, d), const),
                  pl.BlockSpec((TOP_K, tile, d), lambda i: (0, i, 0))],
        out_specs=pl.BlockSpec((tile, d), row),
        out_shape=jax.ShapeDtypeStruct((n, d), F32),
        compiler_params=_cparams("parallel"),
        name="combine_dense",
    )(gates, y, ln_g, ln_b, gathered)


MOE_ROW_BLOCK = 256


def _moe_ln2(y, idx, gates, w_gate, b_gate, w_up, b_up, w_down, b_down, ln2_g, ln2_b):
    n, d = y.shape
    n_blocks = n * TOP_K // MOE_ROW_BLOCK + N_EXPERTS
    rows = n_blocks * MOE_ROW_BLOCK
    dest, counts = _route(idx, MOE_ROW_BLOCK)
    cnt = counts[0, :N_EXPERTS]
    padded = (cnt + MOE_ROW_BLOCK - 1) // MOE_ROW_BLOCK * MOE_ROW_BLOCK
    p_end = jnp.cumsum(padded)
    block_row0 = jnp.arange(n_blocks, dtype=I32) * MOE_ROW_BLOCK
    block_e = jnp.minimum(jnp.sum(p_end[None, :] <= block_row0[:, None], axis=1), N_EXPERTS - 1).astype(I32)
    n_used = (p_end[-1:] // MOE_ROW_BLOCK).astype(I32)
    n_valid = jnp.clip(cnt[block_e] - (block_row0 - (p_end - padded)[block_e]), 0, MOE_ROW_BLOCK).astype(I32)
    dest_t = dest.T
    xs = _sc_dispatch(dest_t, y, rows)
    ys = _experts(block_e, n_used, n_valid, xs, w_gate.astype(BF16), b_gate.astype(F32), w_up.astype(BF16),
                  b_up.astype(F32), w_down.astype(BF16), b_down.astype(F32), MOE_ROW_BLOCK)
    gathered = _sc_gather(dest_t, ys, n)
    return _combine_dense(gates, y, ln2_g.reshape(1, d).astype(F32), ln2_b.reshape(1, d).astype(F32), gathered)


def kernel(x, w_in, sinks, w_proj_a, w_proj_b, w_out, ln1_g, ln1_b, router_w, router_b,
           w_gate, b_gate, w_up, b_up, w_down, b_down, ln2_g, ln2_b):
    bsz, seq, d = x.shape
    h = x.reshape(bsz * seq, d)
    for l in range(w_in.shape[0]):
        y, idx, gates = _token_mixer_ln1(h, w_in[l], sinks[l], w_proj_a[l], w_proj_b[l], w_out[l],
                                         ln1_g[l], ln1_b[l], router_w[l], router_b[l], bsz, seq)
        h = _moe_ln2(y, idx, gates, w_gate[l], b_gate[l], w_up[l], b_up[l], w_down[l], b_down[l],
                     ln2_g[l], ln2_b[l])
    return h.reshape(bsz, seq, d)
```

```python
import functools

import jax
import jax.numpy as jnp
from jax import lax
from jax.experimental import pallas as pl
from jax.experimental.pallas import tpu as pltpu
from jax.experimental.pallas import tpu_sc as plsc

F32 = jnp.float32
BF16 = jnp.bfloat16
I32 = jnp.int32

HEAD_DIM = 64
DIL_GROUPS = ((128, 1), (512, 4), (2048, 16))
DIL_HEADS = 4
N_DIL = len(DIL_GROUPS)
DIL_OUT = DIL_HEADS * HEAD_DIM
DIL_WIDTH = N_DIL * DIL_OUT
SWA_Q_HEADS = 16
SWA_KV_HEADS = 2
SWA_REP = SWA_Q_HEADS // SWA_KV_HEADS
SWA_WINDOW = 128
SWA_Q_WIDTH = SWA_Q_HEADS * HEAD_DIM
SWA_KV_WIDTH = SWA_KV_HEADS * HEAD_DIM
N_ALIBI_HEADS = SWA_Q_HEADS + N_DIL * DIL_HEADS
ATTN_BLOCK = 128
N_EXPERTS = 32
TOP_K = 4
SWIGLU_LIMIT = 7.0
SWIGLU_ALPHA = 1.702
LN_EPS = 1e-5
DEPTH = 1
DEEPNORM_ALPHA = (2 * DEPTH) ** 0.25
NEG_INF = -1e30

LANES = 128
VMEM_LIMIT_BYTES = 56 * 1024 * 1024

A_QKV_W = 3 * DIL_WIDTH
B_Q_OFF = A_QKV_W
B_KV_OFF = B_Q_OFF + SWA_Q_WIDTH
GATE_OFF = B_KV_OFF + 2 * SWA_KV_WIDTH


def _cparams(*sem):
    return pltpu.CompilerParams(dimension_semantics=sem, vmem_limit_bytes=VMEM_LIMIT_BYTES)


def _in_proj_kernel(x_ref, w_ref, a0_ref, a1_ref, a2_ref, hq_ref, hkv_ref, g_ref, *scratch, d_model, tm):
    xb = x_ref[...].astype(BF16)
    segments = ((hq_ref, B_Q_OFF, SWA_Q_WIDTH), (hkv_ref, B_KV_OFF, 2 * SWA_KV_WIDTH),
                (g_ref, GATE_OFF, 2 * d_model))
    for out_ref, col0, width in segments:
        for c in range(0, width, 512):
            w = min(512, width - c)
            r = jnp.dot(xb, w_ref[:, col0 + c:col0 + c + w], preferred_element_type=F32)
            out_ref[:, c:c + w] = r.astype(out_ref.dtype)
    for g, a_ref in enumerate((a0_ref, a1_ref, a2_ref)):
        dil = DIL_GROUPS[g][1]
        per = tm // dil
        for part in range(3):
            col0 = part * DIL_WIDTH + g * DIL_OUT
            res = jnp.dot(xb, w_ref[:, col0:col0 + DIL_OUT], preferred_element_type=F32)
            if dil == 1:
                a_ref[0, :, part * DIL_OUT:(part + 1) * DIL_OUT] = res.astype(a_ref.dtype)
                continue
            for half in range(DIL_OUT // LANES):
                stage = scratch[part * (DIL_OUT // LANES) + half]
                stage[...] = res[:, half * LANES:(half + 1) * LANES]
                c0 = part * DIL_OUT + half * LANES
                for r in range(dil):
                    a_ref[r, :, c0:c0 + LANES] = stage[pl.ds(r, per, stride=dil), :].astype(a_ref.dtype)


def _in_proj(x2, w_in_bf, bsz, seq, tm=512):
    n, d = x2.shape
    cols = w_in_bf.shape[1]
    tiles = seq // tm
    row = lambda i: (i, 0)
    dil_spec = lambda dil: pl.BlockSpec((None, dil, tm // dil, 3 * DIL_OUT), lambda i: (i // tiles, 0, i % tiles, 0))
    dil_shape = lambda dil: jax.ShapeDtypeStruct((bsz, dil, seq // dil, 3 * DIL_OUT), BF16)
    dils = [dil for _, dil in DIL_GROUPS]
    return pl.pallas_call(
        functools.partial(_in_proj_kernel, d_model=d, tm=tm),
        grid=(n // tm,),
        in_specs=[pl.BlockSpec((tm, d), row),
                  pl.BlockSpec((d, cols), lambda i: (0, 0), pipeline_mode=pl.Buffered(1))],
        out_specs=[dil_spec(dil) for dil in dils]
        + [pl.BlockSpec((tm, SWA_Q_WIDTH), row), pl.BlockSpec((tm, 2 * SWA_KV_WIDTH), row),
           pl.BlockSpec((tm, 2 * d), row)],
        out_shape=[dil_shape(dil) for dil in dils]
        + [jax.ShapeDtypeStruct((n, SWA_Q_WIDTH), BF16), jax.ShapeDtypeStruct((n, 2 * SWA_KV_WIDTH), BF16),
           jax.ShapeDtypeStruct((n, 2 * d), BF16)],
        scratch_shapes=[pltpu.VMEM((tm, LANES), F32)] * (3 * DIL_OUT // LANES),
        compiler_params=_cparams("parallel"),
        name="in_proj",
    )(x2, w_in_bf)


_NT = (((1,), (1,)), ((), ()))


def _per_head_column(ref, head0, n_heads, scale=1.0):
    head_of_row = lax.broadcasted_iota(I32, (n_heads * ATTN_BLOCK, 1), 0) // ATTN_BLOCK
    col = jnp.zeros((n_heads * ATTN_BLOCK, 1), F32)
    for h in range(n_heads):
        col = jnp.where(head_of_row == h, ref[head0 + h] * scale, col)
    return col


def _store_band_bias(bias_ref, slope_col, max_diff):
    rows = slope_col.shape[0]
    qi = lax.broadcasted_iota(I32, (rows, ATTN_BLOCK), 0) % ATTN_BLOCK
    kj = lax.broadcasted_iota(I32, (rows, ATTN_BLOCK), 1)
    diff_prev = qi - kj + ATTN_BLOCK
    diff_cur = qi - kj
    bias_ref[0] = jnp.where(diff_prev <= max_diff, -slope_col * diff_prev.astype(F32), NEG_INF)
    bias_ref[1] = jnp.where((diff_cur >= 0) & (diff_cur <= max_diff), -slope_col * diff_cur.astype(F32), NEG_INF)


def _band_softmax(s_prev, s_cur, sink_col):
    m = jnp.max(jnp.maximum(s_prev, s_cur), axis=-1, keepdims=True)
    if sink_col is not None:
        m = jnp.maximum(m, sink_col)
    p_prev = jnp.exp(s_prev - m)
    p_cur = jnp.exp(s_cur - m)
    denom = jnp.sum(p_prev + p_cur, axis=-1, keepdims=True)
    if sink_col is not None:
        denom = denom + jnp.exp(sink_col - m)
    return p_prev.astype(BF16), p_cur.astype(BF16), m, denom


def _dil_attn_kernel(slope_ref, q_ref, kc_ref, kp_ref, vc_ref, vp_ref, o_ref, lse_ref, bias_ref,
                     *, tq, max_diff, dist_scale):
    first = pl.program_id(2) == 0
    _store_band_bias(bias_ref, _per_head_column(slope_ref, 0, DIL_HEADS, dist_scale), max_diff)
    head_cols = [slice(h * HEAD_DIM, (h + 1) * HEAD_DIM) for h in range(DIL_HEADS)]
    for i in range(tq // ATTN_BLOCK):
        rows = slice(i * ATTN_BLOCK, (i + 1) * ATTN_BLOCK)
        prev_rows = slice((i - 1) * ATTN_BLOCK, i * ATTN_BLOCK)
        k_prev = [kp_ref[:, c] if i == 0 else kc_ref[prev_rows, c] for c in head_cols]
        v_prev = [vp_ref[:, c] if i == 0 else vc_ref[prev_rows, c] for c in head_cols]
        q = [q_ref[rows, c] * 0.125 for c in head_cols]
        s_prev = jnp.concatenate([lax.dot_general(q[h], k_prev[h], _NT, preferred_element_type=F32)
                                  for h in range(DIL_HEADS)], axis=0) + bias_ref[0]
        s_cur = jnp.concatenate([lax.dot_general(q[h], kc_ref[rows, head_cols[h]], _NT, preferred_element_type=F32)
                                 for h in range(DIL_HEADS)], axis=0) + bias_ref[1]
        if i == 0:
            s_prev = jnp.where(first, NEG_INF, s_prev)
        p_prev, p_cur, m, denom = _band_softmax(s_prev, s_cur, None)
        lse = m + jnp.log(denom)
        for h in range(DIL_HEADS):
            hr = slice(h * ATTN_BLOCK, (h + 1) * ATTN_BLOCK)
            o = (jnp.dot(p_prev[hr], v_prev[h], preferred_element_type=F32)
                 + jnp.dot(p_cur[hr], vc_ref[rows, head_cols[h]], preferred_element_type=F32)) / denom[hr]
            o_ref[rows, head_cols[h]] = o.astype(o_ref.dtype)
            lse_ref[rows, head_cols[h]] = jnp.broadcast_to(lse[hr], (ATTN_BLOCK, HEAD_DIM))


def _dil_attention(a_g, slopes_g, g):
    window, dil = DIL_GROUPS[g]
    bsz, _, sub_len, _ = a_g.shape
    tq = min(512, sub_len)
    nqb = tq // ATTN_BLOCK
    cur = lambda part: (lambda b, r, m: (b, r, m, part))
    prev = lambda part: (lambda b, r, m: (b, r, jnp.maximum(m * nqb - 1, 0), part))
    blk = lambda rows, imap: pl.BlockSpec((None, None, rows, DIL_OUT), imap)
    return pl.pallas_call(
        functools.partial(_dil_attn_kernel, tq=tq, max_diff=window // dil, dist_scale=float(dil)),
        grid=(bsz, dil, sub_len // tq),
        in_specs=[pl.BlockSpec(memory_space=pltpu.SMEM),
                  blk(tq, cur(0)), blk(tq, cur(1)), blk(ATTN_BLOCK, prev(1)),
                  blk(tq, cur(2)), blk(ATTN_BLOCK, prev(2))],
        out_specs=[blk(tq, cur(0)), blk(tq, cur(0))],
        out_shape=[jax.ShapeDtypeStruct((bsz, dil, sub_len, DIL_OUT), BF16),
                   jax.ShapeDtypeStruct((bsz, dil, sub_len, DIL_OUT), F32)],
        scratch_shapes=[pltpu.VMEM((2, DIL_HEADS * ATTN_BLOCK, ATTN_BLOCK), F32)],
        compiler_params=_cparams("parallel", "parallel", "arbitrary"),
        name=f"dil_attn_g{g}",
    )(slopes_g, a_g, a_g, a_g, a_g, a_g)


def _swa_attn_kernel(slope_ref, sink_ref, q_ref, kvc_ref, kvp_ref, o_ref, bias_ref, *, tq):
    first = pl.program_id(1) == 0
    for kvh in range(SWA_KV_HEADS):
        head0 = kvh * SWA_REP
        _store_band_bias(bias_ref, _per_head_column(slope_ref, head0, SWA_REP), SWA_WINDOW - 1)
        sink_col = _per_head_column(sink_ref, head0, SWA_REP)
        kcols = slice(kvh * HEAD_DIM, (kvh + 1) * HEAD_DIM)
        vcols = slice(SWA_KV_WIDTH + kvh * HEAD_DIM, SWA_KV_WIDTH + (kvh + 1) * HEAD_DIM)
        head_cols = [slice((head0 + r) * HEAD_DIM, (head0 + r + 1) * HEAD_DIM) for r in range(SWA_REP)]
        for i in range(tq // ATTN_BLOCK):
            rows = slice(i * ATTN_BLOCK, (i + 1) * ATTN_BLOCK)
            prev_rows = slice((i - 1) * ATTN_BLOCK, i * ATTN_BLOCK)
            k_prev = kvp_ref[:, kcols] if i == 0 else kvc_ref[prev_rows, kcols]
            v_prev = kvp_ref[:, vcols] if i == 0 else kvc_ref[prev_rows, vcols]
            q = jnp.concatenate([q_ref[rows, c] for c in head_cols], axis=0) * 0.125
            s_prev = lax.dot_general(q, k_prev, _NT, preferred_element_type=F32) + bias_ref[0]
            s_cur = lax.dot_general(q, kvc_ref[rows, kcols], _NT, preferred_element_type=F32) + bias_ref[1]
            if i == 0:
                s_prev = jnp.where(first, NEG_INF, s_prev)
            p_prev, p_cur, _, denom = _band_softmax(s_prev, s_cur, sink_col)
            o = (jnp.dot(p_prev, v_prev, preferred_element_type=F32)
                 + jnp.dot(p_cur, kvc_ref[rows, vcols], preferred_element_type=F32)) / denom
            for r in range(SWA_REP):
                o_ref[rows, head_cols[r]] = o[r * ATTN_BLOCK:(r + 1) * ATTN_BLOCK].astype(o_ref.dtype)


def _swa_attention(hq, hkv, slopes_b, sinks, bsz, seq, tq=512):
    nqb = tq // ATTN_BLOCK
    hq3 = hq.reshape(bsz, seq, SWA_Q_WIDTH)
    hkv3 = hkv.reshape(bsz, seq, 2 * SWA_KV_WIDTH)
    smem = pl.BlockSpec(memory_space=pltpu.SMEM)
    out = pl.pallas_call(
        functools.partial(_swa_attn_kernel, tq=tq),
        grid=(bsz, seq // tq),
        in_specs=[smem, smem,
                  pl.BlockSpec((None, tq, SWA_Q_WIDTH), lambda b, m: (b, m, 0)),
                  pl.BlockSpec((None, tq, 2 * SWA_KV_WIDTH), lambda b, m: (b, m, 0)),
                  pl.BlockSpec((None, ATTN_BLOCK, 2 * SWA_KV_WIDTH),
                               lambda b, m: (b, jnp.maximum(m * nqb - 1, 0), 0))],
        out_specs=pl.BlockSpec((None, tq, SWA_Q_WIDTH), lambda b, m: (b, m, 0)),
        out_shape=jax.ShapeDtypeStruct((bsz, seq, SWA_Q_WIDTH), BF16),
        scratch_shapes=[pltpu.VMEM((2, SWA_REP * ATTN_BLOCK, ATTN_BLOCK), F32)],
        compiler_params=_cparams("parallel", "arbitrary"),
        name="swa_attn",
    )(slopes_b, sinks, hq3, hkv3, hkv3)
    return out.reshape(bsz * seq, SWA_Q_WIDTH)


def _layer_norm(z, g, b):
    mu = jnp.mean(z, axis=-1, keepdims=True)
    zc = z - mu
    var = jnp.mean(zc * zc, axis=-1, keepdims=True)
    return zc * lax.rsqrt(var + LN_EPS) * g + b


def _mix_out_kernel(x_ref, o0_ref, o1_ref, o2_ref, l0_ref, l1_ref, l2_ref, ob_ref, g_ref,
                    wpa_ref, wpb_ref, wo_ref, lng_ref, lnb_ref, rw_ref, rb_ref,
                    y_ref, idx_ref, gate_ref, *scratch, d_model):
    def natural(ref, stages):
        dil, per, _ = ref.shape
        if dil == 1:
            return ref[0].astype(F32)
        for half, stage in enumerate(stages):
            for r in range(dil):
                stage[pl.ds(r, per, stride=dil), :] = ref[r, :, half * LANES:(half + 1) * LANES].astype(F32)
        return jnp.concatenate([stage[...] for stage in stages], axis=1)

    o0, o1, o2 = natural(o0_ref, None), natural(o1_ref, scratch[0:2]), natural(o2_ref, scratch[2:4])
    l0, l1, l2 = natural(l0_ref, None), natural(l1_ref, scratch[4:6]), natural(l2_ref, scratch[6:8])
    lm = jnp.maximum(jnp.maximum(l0, l1), l2)
    e0, e1, e2 = jnp.exp(l0 - lm), jnp.exp(l1 - lm), jnp.exp(l2 - lm)
    esum = e0 + e1 + e2
    out_a = (e0 / esum) * o0 + (e1 / esum) * o1 + (e2 / esum) * o2
    pa = jnp.dot(out_a.astype(BF16), wpa_ref[...], preferred_element_type=F32)
    pb = jnp.dot(ob_ref[...], wpb_ref[...], preferred_element_type=F32)
    ga = jax.nn.sigmoid(g_ref[:, :d_model].astype(F32))
    gb = jax.nn.sigmoid(g_ref[:, d_model:].astype(F32))
    merged = ga * pa + gb * pb
    mix = jnp.dot(merged.astype(BF16), wo_ref[...], preferred_element_type=F32)
    y = _layer_norm(DEEPNORM_ALPHA * x_ref[...] + mix, lng_ref[...], lnb_ref[...])
    y_ref[...] = y

    logits = jnp.dot(y.astype(BF16), rw_ref[...], preferred_element_type=F32) + rb_ref[...]
    lane = lax.broadcasted_iota(I32, logits.shape, 1)
    logits = jnp.where(lane < N_EXPERTS, logits, -jnp.inf)
    idx_out = jnp.zeros(logits.shape, I32)
    val_out = jnp.full(logits.shape, -jnp.inf, F32)
    for k in range(TOP_K):
        top = jnp.max(logits, axis=-1, keepdims=True)
        top_idx = jnp.min(jnp.where(logits == top, lane, LANES), axis=-1, keepdims=True)
        idx_out = jnp.where(lane == k, top_idx, idx_out)
        val_out = jnp.where(lane == k, top, val_out)
        logits = jnp.where(lane == top_idx, -jnp.inf, logits)
    ev = jnp.exp(val_out - jnp.max(val_out, axis=-1, keepdims=True))
    gates = ev / jnp.sum(ev, axis=-1, keepdims=True)
    idx_ref[...] = idx_out[:, :TOP_K]
    gate_ref[...] = gates[:, :TOP_K]


def _mix_out(x2, o_g, lse_g, out_b, gates_h, wpa, wpb, wo, ln_g, ln_b, rw_pad, rb_pad, seq, tm=256):
    n, d = x2.shape
    tiles = seq // tm
    row = lambda i: (i, 0)
    const = lambda i: (0, 0)
    rb = lambda w: pl.BlockSpec((tm, w), row)
    full = lambda a: pl.BlockSpec(a.shape, const)
    dil_specs = [pl.BlockSpec((None, dil, tm // dil, DIL_OUT), lambda i: (i // tiles, 0, i % tiles, 0))
                 for _, dil in DIL_GROUPS]
    return pl.pallas_call(
        functools.partial(_mix_out_kernel, d_model=d),
        grid=(n // tm,),
        in_specs=[rb(d)] + dil_specs + dil_specs + [rb(SWA_Q_WIDTH), rb(2 * d),
                  full(wpa), full(wpb), full(wo), full(ln_g), full(ln_b), full(rw_pad), full(rb_pad)],
        out_specs=[rb(d), rb(TOP_K), rb(TOP_K)],
        out_shape=[jax.ShapeDtypeStruct((n, d), F32),
                   jax.ShapeDtypeStruct((n, TOP_K), I32), jax.ShapeDtypeStruct((n, TOP_K), F32)],
        scratch_shapes=[pltpu.VMEM((tm, LANES), F32)] * (4 * DIL_OUT // LANES),
        compiler_params=_cparams("parallel"),
        name="mix_out",
    )(x2, *o_g, *lse_g, out_b, gates_h, wpa, wpb, wo, ln_g, ln_b, rw_pad, rb_pad)


def _token_mixer_ln1(x2, w_in, sinks, w_proj_a, w_proj_b, w_out, ln1_g, ln1_b, router_w, router_b, bsz, seq):
    n, d = x2.shape
    heads = jnp.arange(1, N_ALIBI_HEADS + 1, dtype=F32)
    slopes = jnp.exp2(-8.0 * heads / N_ALIBI_HEADS)
    *a_g, hq, hkv, gates_h = _in_proj(x2, w_in.astype(BF16), bsz, seq)
    o_g, lse_g = [], []
    for g in range(N_DIL):
        sl = slopes[SWA_Q_HEADS + g * DIL_HEADS:SWA_Q_HEADS + (g + 1) * DIL_HEADS]
        o, lse = _dil_attention(a_g[g], sl, g)
        o_g.append(o)
        lse_g.append(lse)
    out_b = _swa_attention(hq, hkv, slopes[:SWA_Q_HEADS], sinks.astype(F32), bsz, seq)
    rw_pad = jnp.zeros((d, LANES), BF16).at[:, :N_EXPERTS].set(router_w.astype(BF16))
    rb_pad = jnp.zeros((1, LANES), F32).at[0, :N_EXPERTS].set(router_b.astype(F32))
    return _mix_out(x2, o_g, lse_g, out_b, gates_h, w_proj_a.astype(BF16), w_proj_b.astype(BF16),
                    w_out.astype(BF16), ln1_g.reshape(1, d).astype(F32), ln1_b.reshape(1, d).astype(F32),
                    rw_pad, rb_pad, seq)


def _route_kernel(idx_ref, dest_ref, cnt_ref, carry_ref, pstart_ref, *, expert_block):
    phase, i = pl.program_id(0), pl.program_id(1)
    idx = idx_ref[...]
    t = idx.shape[0]
    lane = lax.broadcasted_iota(I32, (t, LANES), 1)
    onehot = [lane == idx[:, k:k + 1] for k in range(TOP_K)]
    multi = sum(oh.astype(F32) for oh in onehot)
    tile_cnt = jnp.sum(multi, axis=0, keepdims=True).astype(I32)

    @pl.when((phase == 0) & (i == 0))
    def _():
        carry_ref[...] = jnp.zeros_like(carry_ref)

    @pl.when((phase == 1) & (i == 0))
    def _():
        counts = carry_ref[...]
        padded = (counts + (expert_block - 1)) & (-expert_block)
        lane8 = lax.broadcasted_iota(I32, counts.shape, 1)
        incl = padded
        for sh in (1, 2, 4, 8, 16):
            incl = incl + jnp.where(lane8 >= sh, pltpu.roll(incl, sh, axis=1), 0)
        pstart_ref[...] = incl - padded
        cnt_ref[...] = counts
        carry_ref[...] = jnp.zeros_like(carry_ref)

    @pl.when(phase == 1)
    def _():
        r = lax.broadcasted_iota(I32, (t, t), 0)
        c = lax.broadcasted_iota(I32, (t, t), 1)
        earlier = (c < r).astype(BF16)
        cum = jnp.dot(earlier, multi.astype(BF16), preferred_element_type=F32).astype(I32)
        base = cum + carry_ref[0:1, :] + pstart_ref[0:1, :]
        dest = jnp.zeros((t, LANES), I32)
        for k in range(TOP_K):
            d_k = jnp.sum(jnp.where(onehot[k], base, 0), axis=-1, keepdims=True)
            dest = jnp.where(lane == k, d_k, dest)
        dest_ref[...] = dest[:, :TOP_K]

    carry_ref[...] = carry_ref[...] + tile_cnt


def _route(idx, expert_block, tile=512):
    n = idx.shape[0]
    tile = min(tile, n)
    return pl.pallas_call(
        functools.partial(_route_kernel, expert_block=expert_block),
        grid=(2, n // tile),
        in_specs=[pl.BlockSpec((tile, TOP_K), lambda p, i: (i, 0))],
        out_specs=[pl.BlockSpec((tile, TOP_K), lambda p, i: (i * p, 0)),
                   pl.BlockSpec((8, LANES), lambda p, i: (0, 0))],
        out_shape=[jax.ShapeDtypeStruct((n, TOP_K), I32), jax.ShapeDtypeStruct((8, LANES), I32)],
        scratch_shapes=[pltpu.VMEM((8, LANES), I32), pltpu.VMEM((8, LANES), I32)],
        compiler_params=_cparams("arbitrary", "arbitrary"),
        name="route",
    )(idx)


def _row_copy(src_hbm, src_row, dst, dst_row, sem):
    return pltpu.make_async_copy(src_hbm.at[pl.ds(src_row, 1)], dst.at[pl.ds(dst_row, 1)], sem)


def _dispatch_kernel(dest_ref, y_ref, xs_in_hbm, xs_hbm, sem, *, tile):
    del xs_in_hbm

    def issue(t, carry):
        for k in range(TOP_K):
            _row_copy(y_ref, t, xs_hbm, dest_ref[t * TOP_K + k], sem).start()
        return carry

    lax.fori_loop(0, tile, issue, 0)

    def drain(t, carry):
        _row_copy(y_ref, 0, xs_hbm, 0, sem).wait()
        return carry

    lax.fori_loop(0, tile * TOP_K, drain, 0)


def _dispatch(dest_flat, y, rows, tile=512):
    n, d = y.shape
    tile = min(tile, n)
    any_spec = pl.BlockSpec(memory_space=pl.ANY)
    return pl.pallas_call(
        functools.partial(_dispatch_kernel, tile=tile),
        grid=(n // tile,),
        in_specs=[pl.BlockSpec((tile * TOP_K,), lambda i: (i,), memory_space=pltpu.SMEM),
                  pl.BlockSpec((tile, d), lambda i: (i, 0)), any_spec],
        out_specs=any_spec,
        out_shape=jax.ShapeDtypeStruct((rows, d), y.dtype),
        scratch_shapes=[pltpu.SemaphoreType.DMA(())],
        input_output_aliases={2: 0},
        compiler_params=_cparams("arbitrary"),
        name="dispatch",
    )(dest_flat, y, jnp.zeros((rows, d), y.dtype))


def _sc_dispatch(dest_t, y, rows, window=32):
    n, d = y.shape
    info = plsc.get_sparse_core_info()
    n_cores = info.num_cores
    per_worker = n // (n_cores * info.num_subcores)
    mesh = plsc.VectorSubcoreMesh(core_axis_name="c", subcore_axis_name="s")

    @functools.partial(pl.kernel, mesh=mesh, out_type=jax.ShapeDtypeStruct((rows, d), y.dtype),
                       scratch_types=[pltpu.VMEM((TOP_K, window), I32), pltpu.VMEM((window, d), y.dtype)],
                       name="sc_dispatch")
    def body(dest_hbm, y_hbm, xs_hbm, idx_v, rows_v):
        base = (lax.axis_index("s") * n_cores + lax.axis_index("c")) * per_worker

        @pl.loop(0, per_worker // window)
        def _(c):
            t0 = base + c * window
            for k in range(TOP_K):
                pltpu.sync_copy(dest_hbm.at[k, pl.ds(t0, window)], idx_v.at[k])
            pltpu.sync_copy(y_hbm.at[pl.ds(t0, window)], rows_v)
            for k in range(TOP_K):
                pltpu.sync_copy(rows_v, xs_hbm.at[idx_v.at[k]])

    return body(dest_t, y)


def _expert_kernel(be_ref, nu_ref, nv_ref, x_ref, wg_ref, bg_ref, wu_ref, bu_ref, wd_ref, bd_ref, o_ref):
    del be_ref
    j = pl.program_id(0)

    @pl.when(j < nu_ref[0])
    def _():
        row = lax.broadcasted_iota(I32, x_ref.shape, 0)
        xb = jnp.where(row < nv_ref[j], x_ref[...], 0.0).astype(BF16)
        gt = jnp.dot(xb, wg_ref[...], preferred_element_type=F32) + bg_ref[...]
        up = jnp.dot(xb, wu_ref[...], preferred_element_type=F32) + bu_ref[...]
        gt = jnp.minimum(gt, SWIGLU_LIMIT)
        up = jnp.clip(up, -SWIGLU_LIMIT, SWIGLU_LIMIT)
        hdn = gt * jax.nn.sigmoid(SWIGLU_ALPHA * gt) * (up + 1.0)
        o_ref[...] = jnp.dot(hdn.astype(BF16), wd_ref[...], preferred_element_type=F32) + bd_ref[...]

    @pl.when(j >= nu_ref[0])
    def _():
        o_ref[...] = jnp.zeros_like(o_ref)


def _experts(block_e, n_used, n_valid, xs, wg, bg, wu, bu, wd, bd, expert_block):
    rows, d = xs.shape
    e, _, f = wg.shape
    xmap = lambda j, be, nu, nv: (jnp.minimum(j, nu[0] - 1), 0)
    wmap = lambda j, be, nu, nv: (be[j], 0, 0)
    grid_spec = pltpu.PrefetchScalarGridSpec(
        num_scalar_prefetch=3,
        grid=(rows // expert_block,),
        in_specs=[pl.BlockSpec((expert_block, d), xmap),
                  pl.BlockSpec((None, d, f), wmap), pl.BlockSpec((None, 1, f), wmap),
                  pl.BlockSpec((None, d, f), wmap), pl.BlockSpec((None, 1, f), wmap),
                  pl.BlockSpec((None, f, d), wmap), pl.BlockSpec((None, 1, d), wmap)],
        out_specs=pl.BlockSpec((expert_block, d), lambda j, be, nu, nv: (j, 0)),
    )
    return pl.pallas_call(
        _expert_kernel,
        grid_spec=grid_spec,
        out_shape=jax.ShapeDtypeStruct((rows, d), F32),
        compiler_params=_cparams("arbitrary"),
        name="experts",
    )(block_e, n_used, n_valid, xs, wg, bg.reshape(e, 1, f), wu, bu.reshape(e, 1, f), wd, bd.reshape(e, 1, d))


def _combine_kernel(dest_ref, gate_ref, y_ref, lng_ref, lnb_ref, ys_hbm, o_ref, buf_ref, sem, *, tile):
    def issue(t, carry):
        for k in range(TOP_K):
            _row_copy(ys_hbm, dest_ref[t * TOP_K + k], buf_ref.at[k], t, sem).start()
        return carry

    lax.fori_loop(0, tile, issue, 0)

    def drain(t, carry):
        _row_copy(ys_hbm, 0, buf_ref.at[0], 0, sem).wait()
        return carry

    lax.fori_loop(0, tile * TOP_K, drain, 0)

    gates = gate_ref[...]
    ffn = gates[:, 0:1] * buf_ref[0]
    for k in range(1, TOP_K):
        ffn = ffn + gates[:, k:k + 1] * buf_ref[k]
    o_ref[...] = _layer_norm(DEEPNORM_ALPHA * y_ref[...] + ffn, lng_ref[...], lnb_ref[...])


def _combine(dest_flat, gates, y, ln_g, ln_b, ys, tile=256):
    n, d = y.shape
    tile = min(tile, n)
    row = lambda i: (i, 0)
    const = lambda i: (0, 0)
    return pl.pallas_call(
        functools.partial(_combine_kernel, tile=tile),
        grid=(n // tile,),
        in_specs=[pl.BlockSpec((tile * TOP_K,), lambda i: (i,), memory_space=pltpu.SMEM),
                  pl.BlockSpec((tile, TOP_K), row), pl.BlockSpec((tile, d), row),
                  pl.BlockSpec((1, d), const), pl.BlockSpec((1, d), const),
                  pl.BlockSpec(memory_space=pl.ANY)],
        out_specs=pl.BlockSpec((tile, d), row),
        out_shape=jax.ShapeDtypeStruct((n, d), F32),
        scratch_shapes=[pltpu.VMEM((TOP_K, tile, d), F32), pltpu.SemaphoreType.DMA(())],
        compiler_params=_cparams("arbitrary"),
        name="combine",
    )(dest_flat, gates, y, ln_g, ln_b, ys)


def _sc_gather(dest_t, ys, n, window=32):
    _, d = ys.shape
    info = plsc.get_sparse_core_info()
    n_cores = info.num_cores
    per_worker = n // (n_cores * info.num_subcores)
    mesh = plsc.VectorSubcoreMesh(core_axis_name="c", subcore_axis_name="s")

    @functools.partial(pl.kernel, mesh=mesh, out_type=jax.ShapeDtypeStruct((TOP_K, n, d), ys.dtype),
                       scratch_types=[pltpu.VMEM((TOP_K, window), I32), pltpu.VMEM((window, d), ys.dtype)],
                       name="sc_gather")
    def body(dest_hbm, ys_hbm, out_hbm, idx_v, rows_v):
        base = (lax.axis_index("s") * n_cores + lax.axis_index("c")) * per_worker

        @pl.loop(0, per_worker // window)
        def _(c):
            t0 = base + c * window
            for k in range(TOP_K):
                pltpu.sync_copy(dest_hbm.at[k, pl.ds(t0, window)], idx_v.at[k])
            for k in range(TOP_K):
                pltpu.sync_copy(ys_hbm.at[idx_v.at[k]], rows_v)
                pltpu.sync_copy(rows_v, out_hbm.at[k, pl.ds(t0, window)])

    return body(dest_t, ys)


def _combine_dense_kernel(gate_ref, y_ref, lng_ref, lnb_ref, g_ref, o_ref):
    gates = gate_ref[...]
    ffn = gates[:, 0:1] * g_ref[0]
    for k in range(1, TOP_K):
        ffn = ffn + gates[:, k:k + 1] * g_ref[k]
    o_ref[...] = _layer_norm(DEEPNORM_ALPHA * y_ref[...] + ffn, lng_ref[...], lnb_ref[...])


def _combine_dense(gates, y, ln_g, ln_b, gathered, tile=512):
    n, d = y.shape
    tile = min(tile, n)
    row = lambda i: (i, 0)
    const = lambda i: (0, 0)
    return pl.pallas_call(
        _combine_dense_kernel,
        grid=(n // tile,),
        in_specs=[pl.BlockSpec((tile, TOP_K), row), pl.BlockSpec((tile, d), row),
                  pl.BlockSpec((1, d), const), pl.BlockSpec((1, d), const),
                  pl.BlockSpec((TOP_K, tile, d), lambda i: (0, i, 0))],
        out_specs=pl.BlockSpec((tile, d), row),
        out_shape=jax.ShapeDtypeStruct((n, d), F32),
        compiler_params=_cparams("parallel"),
        name="combine_dense",
    )(gates, y, ln_g, ln_b, gathered)


MOE_ROW_BLOCK = 256


def _moe_ln2(y, idx, gates, w_gate, b_gate, w_up, b_up, w_down, b_down, ln2_g, ln2_b):
    n, d = y.shape
    n_blocks = n * TOP_K // MOE_ROW_BLOCK + N_EXPERTS
    rows = n_blocks * MOE_ROW_BLOCK
    dest, counts = _route(idx, MOE_ROW_BLOCK)
    cnt = counts[0, :N_EXPERTS]
    padded = (cnt + MOE_ROW_BLOCK - 1) // MOE_ROW_BLOCK * MOE_ROW_BLOCK
    p_end = jnp.cumsum(padded)
    block_row0 = jnp.arange(n_blocks, dtype=I32) * MOE_ROW_BLOCK
    block_e = jnp.minimum(jnp.sum(p_end[None, :] <= block_row0[:, None], axis=1), N_EXPERTS - 1).astype(I32)
    n_used = (p_end[-1:] // MOE_ROW_BLOCK).astype(I32)
    n_valid = jnp.clip(cnt[block_e] - (block_row0 - (p_end - padded)[block_e]), 0, MOE_ROW_BLOCK).astype(I32)
    dest_t = dest.T
    xs = _sc_dispatch(dest_t, y, rows)
    ys = _experts(block_e, n_used, n_valid, xs, w_gate.astype(BF16), b_gate.astype(F32), w_up.astype(BF16),
                  b_up.astype(F32), w_down.astype(BF16), b_down.astype(F32), MOE_ROW_BLOCK)
    gathered = _sc_gather(dest_t, ys, n)
    return _combine_dense(gates, y, ln2_g.reshape(1, d).astype(F32), ln2_b.reshape(1, d).astype(F32), gathered)


def kernel(x, w_in, sinks, w_proj_a, w_proj_b, w_out, ln1_g, ln1_b, router_w, router_b,
           w_gate, b_gate, w_up, b_up, w_down, b_down, ln2_g, ln2_b):
    bsz, seq, d = x.shape
    h = x.reshape(bsz * seq, d)
    for l in range(w_in.shape[0]):
        y, idx, gates = _token_mixer_ln1(h, w_in[l], sinks[l], w_proj_a[l], w_proj_b[l], w_out[l],
                                         ln1_g[l], ln1_b[l], router_w[l], router_b[l], bsz, seq)
        h = _moe_ln2(y, idx, gates, w_gate[l], b_gate[l], w_up[l], b_up[l], w_down[l], b_down[l],
                     ln2_g[l], ln2_b[l])
    return h.reshape(bsz, seq, d)
```

```python
import functools

import jax
import jax.numpy as jnp
from jax import lax
from jax.experimental import pallas as pl
from jax.experimental.pallas import tpu as pltpu
from jax.experimental.pallas import tpu_sc as plsc

F32 = jnp.float32
BF16 = jnp.bfloat16
I32 = jnp.int32

HEAD_DIM = 64
DIL_GROUPS = ((128, 1), (512, 4), (2048, 16))
DIL_HEADS = 4
N_DIL = len(DIL_GROUPS)
DIL_OUT = DIL_HEADS * HEAD_DIM
DIL_WIDTH = N_DIL * DIL_OUT
SWA_Q_HEADS = 16
SWA_KV_HEADS = 2
SWA_REP = SWA_Q_HEADS // SWA_KV_HEADS
SWA_WINDOW = 128
SWA_Q_WIDTH = SWA_Q_HEADS * HEAD_DIM
SWA_KV_WIDTH = SWA_KV_HEADS * HEAD_DIM
N_ALIBI_HEADS = SWA_Q_HEADS + N_DIL * DIL_HEADS
ATTN_BLOCK = 128
N_EXPERTS = 32
TOP_K = 4
SWIGLU_LIMIT = 7.0
SWIGLU_ALPHA = 1.702
LN_EPS = 1e-5
DEPTH = 1
DEEPNORM_ALPHA = (2 * DEPTH) ** 0.25
NEG_INF = -1e30

LANES = 128
VMEM_LIMIT_BYTES = 56 * 1024 * 1024

A_QKV_W = 3 * DIL_WIDTH
B_Q_OFF = A_QKV_W
B_KV_OFF = B_Q_OFF + SWA_Q_WIDTH
GATE_OFF = B_KV_OFF + 2 * SWA_KV_WIDTH


def _cparams(*sem):
    return pltpu.CompilerParams(dimension_semantics=sem, vmem_limit_bytes=VMEM_LIMIT_BYTES)


def _in_proj_kernel(x_ref, w_ref, a0_ref, a1_ref, a2_ref, hq_ref, hkv_ref, g_ref, *scratch, d_model, tm):
    xb = x_ref[...].astype(BF16)
    segments = ((hq_ref, B_Q_OFF, SWA_Q_WIDTH), (hkv_ref, B_KV_OFF, 2 * SWA_KV_WIDTH),
                (g_ref, GATE_OFF, 2 * d_model))
    for out_ref, col0, width in segments:
        for c in range(0, width, 512):
            w = min(512, width - c)
            r = jnp.dot(xb, w_ref[:, col0 + c:col0 + c + w], preferred_element_type=F32)
            out_ref[:, c:c + w] = r.astype(out_ref.dtype)
    for g, a_ref in enumerate((a0_ref, a1_ref, a2_ref)):
        dil = DIL_GROUPS[g][1]
        per = tm // dil
        for part in range(3):
            col0 = part * DIL_WIDTH + g * DIL_OUT
            res = jnp.dot(xb, w_ref[:, col0:col0 + DIL_OUT], preferred_element_type=F32)
            if dil == 1:
                a_ref[0, :, part * DIL_OUT:(part + 1) * DIL_OUT] = res.astype(a_ref.dtype)
                continue
            for half in range(DIL_OUT // LANES):
                stage = scratch[part * (DIL_OUT // LANES) + half]
                stage[...] = res[:, half * LANES:(half + 1) * LANES]
                c0 = part * DIL_OUT + half * LANES
                for r in range(dil):
                    a_ref[r, :, c0:c0 + LANES] = stage[pl.ds(r, per, stride=dil), :].astype(a_ref.dtype)


def _in_proj(x2, w_in_bf, bsz, seq, tm=512):
    n, d = x2.shape
    cols = w_in_bf.shape[1]
    tiles = seq // tm
    row = lambda i: (i, 0)
    dil_spec = lambda dil: pl.BlockSpec((None, dil, tm // dil, 3 * DIL_OUT), lambda i: (i // tiles, 0, i % tiles, 0))
    dil_shape = lambda dil: jax.ShapeDtypeStruct((bsz, dil, seq // dil, 3 * DIL_OUT), BF16)
    dils = [dil for _, dil in DIL_GROUPS]
    return pl.pallas_call(
        functools.partial(_in_proj_kernel, d_model=d, tm=tm),
        grid=(n // tm,),
        in_specs=[pl.BlockSpec((tm, d), row),
                  pl.BlockSpec((d, cols), lambda i: (0, 0), pipeline_mode=pl.Buffered(1))],
        out_specs=[dil_spec(dil) for dil in dils]
        + [pl.BlockSpec((tm, SWA_Q_WIDTH), row), pl.BlockSpec((tm, 2 * SWA_KV_WIDTH), row),
           pl.BlockSpec((tm, 2 * d), row)],
        out_shape=[dil_shape(dil) for dil in dils]
        + [jax.ShapeDtypeStruct((n, SWA_Q_WIDTH), BF16), jax.ShapeDtypeStruct((n, 2 * SWA_KV_WIDTH), BF16),
           jax.ShapeDtypeStruct((n, 2 * d), BF16)],
        scratch_shapes=[pltpu.VMEM((tm, LANES), F32)] * (3 * DIL_OUT // LANES),
        compiler_params=_cparams("parallel"),
        name="in_proj",
    )(x2, w_in_bf)


_NT = (((1,), (1,)), ((), ()))


def _per_head_column(ref, head0, n_heads, scale=1.0):
    head_of_row = lax.broadcasted_iota(I32, (n_heads * ATTN_BLOCK, 1), 0) // ATTN_BLOCK
    col = jnp.zeros((n_heads * ATTN_BLOCK, 1), F32)
    for h in range(n_heads):
        col = jnp.where(head_of_row == h, ref[head0 + h] * scale, col)
    return col


def _store_band_bias(bias_ref, slope_col, max_diff):
    rows = slope_col.shape[0]
    qi = lax.broadcasted_iota(I32, (rows, ATTN_BLOCK), 0) % ATTN_BLOCK
    kj = lax.broadcasted_iota(I32, (rows, ATTN_BLOCK), 1)
    diff_prev = qi - kj + ATTN_BLOCK
    diff_cur = qi - kj
    bias_ref[0] = jnp.where(diff_prev <= max_diff, -slope_col * diff_prev.astype(F32), NEG_INF)
    bias_ref[1] = jnp.where((diff_cur >= 0) & (diff_cur <= max_diff), -slope_col * diff_cur.astype(F32), NEG_INF)


def _band_softmax(s_prev, s_cur, sink_col):
    m = jnp.max(jnp.maximum(s_prev, s_cur), axis=-1, keepdims=True)
    if sink_col is not None:
        m = jnp.maximum(m, sink_col)
    p_prev = jnp.exp(s_prev - m)
    p_cur = jnp.exp(s_cur - m)
    denom = jnp.sum(p_prev + p_cur, axis=-1, keepdims=True)
    if sink_col is not None:
        denom = denom + jnp.exp(sink_col - m)
    return p_prev.astype(BF16), p_cur.astype(BF16), m, denom


def _dil_attn_kernel(slope_ref, q_ref, kc_ref, kp_ref, vc_ref, vp_ref, o_ref, lse_ref, bias_ref,
                     *, tq, max_diff, dist_scale):
    first = pl.program_id(2) == 0
    _store_band_bias(bias_ref, _per_head_column(slope_ref, 0, DIL_HEADS, dist_scale), max_diff)
    head_cols = [slice(h * HEAD_DIM, (h + 1) * HEAD_DIM) for h in range(DIL_HEADS)]
    for i in range(tq // ATTN_BLOCK):
        rows = slice(i * ATTN_BLOCK, (i + 1) * ATTN_BLOCK)
        prev_rows = slice((i - 1) * ATTN_BLOCK, i * ATTN_BLOCK)
        k_prev = [kp_ref[:, c] if i == 0 else kc_ref[prev_rows, c] for c in head_cols]
        v_prev = [vp_ref[:, c] if i == 0 else vc_ref[prev_rows, c] for c in head_cols]
        q = [q_ref[rows, c] * 0.125 for c in head_cols]
        s_prev = jnp.concatenate([lax.dot_general(q[h], k_prev[h], _NT, preferred_element_type=F32)
                                  for h in range(DIL_HEADS)], axis=0) + bias_ref[0]
        s_cur = jnp.concatenate([lax.dot_general(q[h], kc_ref[rows, head_cols[h]], _NT, preferred_element_type=F32)
                                 for h in range(DIL_HEADS)], axis=0) + bias_ref[1]
        if i == 0:
            s_prev = jnp.where(first, NEG_INF, s_prev)
        p_prev, p_cur, m, denom = _band_softmax(s_prev, s_cur, None)
        lse = m + jnp.log(denom)
        for h in range(DIL_HEADS):
            hr = slice(h * ATTN_BLOCK, (h + 1) * ATTN_BLOCK)
            o = (jnp.dot(p_prev[hr], v_prev[h], preferred_element_type=F32)
                 + jnp.dot(p_cur[hr], vc_ref[rows, head_cols[h]], preferred_element_type=F32)) / denom[hr]
            o_ref[rows, head_cols[h]] = o.astype(o_ref.dtype)
            lse_ref[rows, head_cols[h]] = jnp.broadcast_to(lse[hr], (ATTN_BLOCK, HEAD_DIM))


def _dil_attention(a_g, slopes_g, g):
    window, dil = DIL_GROUPS[g]
    bsz, _, sub_len, _ = a_g.shape
    tq = min(512, sub_len)
    nqb = tq // ATTN_BLOCK
    cur = lambda part: (lambda b, r, m: (b, r, m, part))
    prev = lambda part: (lambda b, r, m: (b, r, jnp.maximum(m * nqb - 1, 0), part))
    blk = lambda rows, imap: pl.BlockSpec((None, None, rows, DIL_OUT), imap)
    return pl.pallas_call(
        functools.partial(_dil_attn_kernel, tq=tq, max_diff=window // dil, dist_scale=float(dil)),
        grid=(bsz, dil, sub_len // tq),
        in_specs=[pl.BlockSpec(memory_space=pltpu.SMEM),
                  blk(tq, cur(0)), blk(tq, cur(1)), blk(ATTN_BLOCK, prev(1)),
                  blk(tq, cur(2)), blk(ATTN_BLOCK, prev(2))],
        out_specs=[blk(tq, cur(0)), blk(tq, cur(0))],
        out_shape=[jax.ShapeDtypeStruct((bsz, dil, sub_len, DIL_OUT), BF16),
                   jax.ShapeDtypeStruct((bsz, dil, sub_len, DIL_OUT), F32)],
        scratch_shapes=[pltpu.VMEM((2, DIL_HEADS * ATTN_BLOCK, ATTN_BLOCK), F32)],
        compiler_params=_cparams("parallel", "parallel", "arbitrary"),
        name=f"dil_attn_g{g}",
    )(slopes_g, a_g, a_g, a_g, a_g, a_g)


def _swa_attn_kernel(slope_ref, sink_ref, q_ref, kvc_ref, kvp_ref, o_ref, bias_ref, *, tq):
    first = pl.program_id(1) == 0
    for kvh in range(SWA_KV_HEADS):
        head0 = kvh * SWA_REP
        _store_band_bias(bias_ref, _per_head_column(slope_ref, head0, SWA_REP), SWA_WINDOW - 1)
        sink_col = _per_head_column(sink_ref, head0, SWA_REP)
        kcols = slice(kvh * HEAD_DIM, (kvh + 1) * HEAD_DIM)
        vcols = slice(SWA_KV_WIDTH + kvh * HEAD_DIM, SWA_KV_WIDTH + (kvh + 1) * HEAD_DIM)
        head_cols = [slice((head0 + r) * HEAD_DIM, (head0 + r + 1) * HEAD_DIM) for r in range(SWA_REP)]
        for i in range(tq // ATTN_BLOCK):
            rows = slice(i * ATTN_BLOCK, (i + 1) * ATTN_BLOCK)
            prev_rows = slice((i - 1) * ATTN_BLOCK, i * ATTN_BLOCK)
            k_prev = kvp_ref[:, kcols] if i == 0 else kvc_ref[prev_rows, kcols]
            v_prev = kvp_ref[:, vcols] if i == 0 else kvc_ref[prev_rows, vcols]
            q = jnp.concatenate([q_ref[rows, c] for c in head_cols], axis=0) * 0.125
            s_prev = lax.dot_general(q, k_prev, _NT, preferred_element_type=F32) + bias_ref[0]
            s_cur = lax.dot_general(q, kvc_ref[rows, kcols], _NT, preferred_element_type=F32) + bias_ref[1]
            if i == 0:
                s_prev = jnp.where(first, NEG_INF, s_prev)
            p_prev, p_cur, _, denom = _band_softmax(s_prev, s_cur, sink_col)
            o = (jnp.dot(p_prev, v_prev, preferred_element_type=F32)
                 + jnp.dot(p_cur, kvc_ref[rows, vcols], preferred_element_type=F32)) / denom
            for r in range(SWA_REP):
                o_ref[rows, head_cols[r]] = o[r * ATTN_BLOCK:(r + 1) * ATTN_BLOCK].astype(o_ref.dtype)


def _swa_attention(hq, hkv, slopes_b, sinks, bsz, seq, tq=512):
    nqb = tq // ATTN_BLOCK
    hq3 = hq.reshape(bsz, seq, SWA_Q_WIDTH)
    hkv3 = hkv.reshape(bsz, seq, 2 * SWA_KV_WIDTH)
    smem = pl.BlockSpec(memory_space=pltpu.SMEM)
    out = pl.pallas_call(
        functools.partial(_swa_attn_kernel, tq=tq),
        grid=(bsz, seq // tq),
        in_specs=[smem, smem,
                  pl.BlockSpec((None, tq, SWA_Q_WIDTH), lambda b, m: (b, m, 0)),
                  pl.BlockSpec((None, tq, 2 * SWA_KV_WIDTH), lambda b, m: (b, m, 0)),
                  pl.BlockSpec((None, ATTN_BLOCK, 2 * SWA_KV_WIDTH),
                               lambda b, m: (b, jnp.maximum(m * nqb - 1, 0), 0))],
        out_specs=pl.BlockSpec((None, tq, SWA_Q_WIDTH), lambda b, m: (b, m, 0)),
        out_shape=jax.ShapeDtypeStruct((bsz, seq, SWA_Q_WIDTH), BF16),
        scratch_shapes=[pltpu.VMEM((2, SWA_REP * ATTN_BLOCK, ATTN_BLOCK), F32)],
        compiler_params=_cparams("parallel", "arbitrary"),
        name="swa_attn",
    )(slopes_b, sinks, hq3, hkv3, hkv3)
    return out.reshape(bsz * seq, SWA_Q_WIDTH)


def _sigmoid(x):
    return 0.5 * (jnp.tanh(0.5 * x) + 1.0)


def _layer_norm(z, g, b):
    mu = jnp.mean(z, axis=-1, keepdims=True)
    zc = z - mu
    var = jnp.mean(zc * zc, axis=-1, keepdims=True)
    return zc * lax.rsqrt(var + LN_EPS) * g + b


def _mix_out_kernel(x_ref, o0_ref, o1_ref, o2_ref, l0_ref, l1_ref, l2_ref, ob_ref, g_ref,
                    wpa_ref, wpb_ref, wo_ref, lng_ref, lnb_ref, rw_ref, rb_ref,
                    y_ref, idx_ref, gate_ref, *scratch, d_model):
    def natural(ref, stages):
        dil, per, _ = ref.shape
        if dil == 1:
            return ref[0].astype(F32)
        for half, stage in enumerate(stages):
            for r in range(dil):
                stage[pl.ds(r, per, stride=dil), :] = ref[r, :, half * LANES:(half + 1) * LANES].astype(F32)
        return jnp.concatenate([stage[...] for stage in stages], axis=1)

    o0, o1, o2 = natural(o0_ref, None), natural(o1_ref, scratch[0:2]), natural(o2_ref, scratch[2:4])
    l0, l1, l2 = natural(l0_ref, None), natural(l1_ref, scratch[4:6]), natural(l2_ref, scratch[6:8])
    lm = jnp.maximum(jnp.maximum(l0, l1), l2)
    e0, e1, e2 = jnp.exp(l0 - lm), jnp.exp(l1 - lm), jnp.exp(l2 - lm)
    esum = e0 + e1 + e2
    out_a = (e0 / esum) * o0 + (e1 / esum) * o1 + (e2 / esum) * o2
    pa = jnp.dot(out_a.astype(BF16), wpa_ref[...], preferred_element_type=F32)
    pb = jnp.dot(ob_ref[...], wpb_ref[...], preferred_element_type=F32)
    ga = _sigmoid(g_ref[:, :d_model].astype(F32))
    gb = _sigmoid(g_ref[:, d_model:].astype(F32))
    merged = ga * pa + gb * pb
    mix = jnp.dot(merged.astype(BF16), wo_ref[...], preferred_element_type=F32)
    y = _layer_norm(DEEPNORM_ALPHA * x_ref[...] + mix, lng_ref[...], lnb_ref[...])
    y_ref[...] = y

    logits = jnp.dot(y.astype(BF16), rw_ref[...], preferred_element_type=F32) + rb_ref[...]
    lane = lax.broadcasted_iota(I32, logits.shape, 1)
    logits = jnp.where(lane < N_EXPERTS, logits, -jnp.inf)
    idx_out = jnp.zeros(logits.shape, I32)
    val_out = jnp.full(logits.shape, -jnp.inf, F32)
    for k in range(TOP_K):
        top = jnp.max(logits, axis=-1, keepdims=True)
        top_idx = jnp.min(jnp.where(logits == top, lane, LANES), axis=-1, keepdims=True)
        idx_out = jnp.where(lane == k, top_idx, idx_out)
        val_out = jnp.where(lane == k, top, val_out)
        logits = jnp.where(lane == top_idx, -jnp.inf, logits)
    ev = jnp.exp(val_out - jnp.max(val_out, axis=-1, keepdims=True))
    gates = ev / jnp.sum(ev, axis=-1, keepdims=True)
    idx_ref[...] = idx_out[:, :TOP_K]
    gate_ref[...] = gates[:, :TOP_K]


def _mix_out(x2, o_g, lse_g, out_b, gates_h, wpa, wpb, wo, ln_g, ln_b, rw_pad, rb_pad, seq, tm=512):
    n, d = x2.shape
    tiles = seq // tm
    row = lambda i: (i, 0)
    const = lambda i: (0, 0)
    rb = lambda w: pl.BlockSpec((tm, w), row)
    full = lambda a: pl.BlockSpec(a.shape, const)
    dil_specs = [pl.BlockSpec((None, dil, tm // dil, DIL_OUT), lambda i: (i // tiles, 0, i % tiles, 0))
                 for _, dil in DIL_GROUPS]
    return pl.pallas_call(
        functools.partial(_mix_out_kernel, d_model=d),
        grid=(n // tm,),
        in_specs=[rb(d)] + dil_specs + dil_specs + [rb(SWA_Q_WIDTH), rb(2 * d),
                  full(wpa), full(wpb), full(wo), full(ln_g), full(ln_b), full(rw_pad), full(rb_pad)],
        out_specs=[rb(d), rb(TOP_K), rb(TOP_K)],
        out_shape=[jax.ShapeDtypeStruct((n, d), F32),
                   jax.ShapeDtypeStruct((n, TOP_K), I32), jax.ShapeDtypeStruct((n, TOP_K), F32)],
        scratch_shapes=[pltpu.VMEM((tm, LANES), F32)] * (4 * DIL_OUT // LANES),
        compiler_params=_cparams("parallel"),
        name="mix_out",
    )(x2, *o_g, *lse_g, out_b, gates_h, wpa, wpb, wo, ln_g, ln_b, rw_pad, rb_pad)


def _token_mixer_ln1(x2, w_in, sinks, w_proj_a, w_proj_b, w_out, ln1_g, ln1_b, router_w, router_b, bsz, seq):
    n, d = x2.shape
    heads = jnp.arange(1, N_ALIBI_HEADS + 1, dtype=F32)
    slopes = jnp.exp2(-8.0 * heads / N_ALIBI_HEADS)
    *a_g, hq, hkv, gates_h = _in_proj(x2, w_in.astype(BF16), bsz, seq)
    o_g, lse_g = [], []
    for g in range(N_DIL):
        sl = slopes[SWA_Q_HEADS + g * DIL_HEADS:SWA_Q_HEADS + (g + 1) * DIL_HEADS]
        o, lse = _dil_attention(a_g[g], sl, g)
        o_g.append(o)
        lse_g.append(lse)
    out_b = _swa_attention(hq, hkv, slopes[:SWA_Q_HEADS], sinks.astype(F32), bsz, seq)
    rw_pad = jnp.zeros((d, LANES), BF16).at[:, :N_EXPERTS].set(router_w.astype(BF16))
    rb_pad = jnp.zeros((1, LANES), F32).at[0, :N_EXPERTS].set(router_b.astype(F32))
    return _mix_out(x2, o_g, lse_g, out_b, gates_h, w_proj_a.astype(BF16), w_proj_b.astype(BF16),
                    w_out.astype(BF16), ln1_g.reshape(1, d).astype(F32), ln1_b.reshape(1, d).astype(F32),
                    rw_pad, rb_pad, seq)


def _route_kernel(idx_ref, dest_ref, cnt_ref, carry_ref, pstart_ref, *, expert_block):
    phase, i = pl.program_id(0), pl.program_id(1)
    idx = idx_ref[...]
    t = idx.shape[0]
    lane = lax.broadcasted_iota(I32, (t, LANES), 1)
    onehot = [lane == idx[:, k:k + 1] for k in range(TOP_K)]
    multi = sum(oh.astype(F32) for oh in onehot)
    tile_cnt = jnp.sum(multi, axis=0, keepdims=True).astype(I32)

    @pl.when((phase == 0) & (i == 0))
    def _():
        carry_ref[...] = jnp.zeros_like(carry_ref)

    @pl.when((phase == 1) & (i == 0))
    def _():
        counts = carry_ref[...]
        padded = (counts + (expert_block - 1)) & (-expert_block)
        lane8 = lax.broadcasted_iota(I32, counts.shape, 1)
        incl = padded
        for sh in (1, 2, 4, 8, 16):
            incl = incl + jnp.where(lane8 >= sh, pltpu.roll(incl, sh, axis=1), 0)
        pstart_ref[...] = incl - padded
        cnt_ref[...] = counts
        carry_ref[...] = jnp.zeros_like(carry_ref)

    @pl.when(phase == 1)
    def _():
        r = lax.broadcasted_iota(I32, (t, t), 0)
        c = lax.broadcasted_iota(I32, (t, t), 1)
        earlier = (c < r).astype(BF16)
        cum = jnp.dot(earlier, multi.astype(BF16), preferred_element_type=F32).astype(I32)
        base = cum + carry_ref[0:1, :] + pstart_ref[0:1, :]
        dest = jnp.zeros((t, LANES), I32)
        for k in range(TOP_K):
            d_k = jnp.sum(jnp.where(onehot[k], base, 0), axis=-1, keepdims=True)
            dest = jnp.where(lane == k, d_k, dest)
        dest_ref[...] = dest[:, :TOP_K]

    carry_ref[...] = carry_ref[...] + tile_cnt


def _route(idx, expert_block, tile=512):
    n = idx.shape[0]
    tile = min(tile, n)
    return pl.pallas_call(
        functools.partial(_route_kernel, expert_block=expert_block),
        grid=(2, n // tile),
        in_specs=[pl.BlockSpec((tile, TOP_K), lambda p, i: (i, 0))],
        out_specs=[pl.BlockSpec((tile, TOP_K), lambda p, i: (i * p, 0)),
                   pl.BlockSpec((8, LANES), lambda p, i: (0, 0))],
        out_shape=[jax.ShapeDtypeStruct((n, TOP_K), I32), jax.ShapeDtypeStruct((8, LANES), I32)],
        scratch_shapes=[pltpu.VMEM((8, LANES), I32), pltpu.VMEM((8, LANES), I32)],
        compiler_params=_cparams("arbitrary", "arbitrary"),
        name="route",
    )(idx)


def _sc_dispatch(dest_t, y, rows, window=32):
    n, d = y.shape
    info = plsc.get_sparse_core_info()
    n_cores = info.num_cores
    per_worker = n // (n_cores * info.num_subcores)
    mesh = plsc.VectorSubcoreMesh(core_axis_name="c", subcore_axis_name="s")

    @functools.partial(pl.kernel, mesh=mesh, out_type=jax.ShapeDtypeStruct((rows, d), y.dtype),
                       scratch_types=[pltpu.VMEM((TOP_K, window), I32), pltpu.VMEM((window, d), y.dtype)],
                       name="sc_dispatch")
    def body(dest_hbm, y_hbm, xs_hbm, idx_v, rows_v):
        base = (lax.axis_index("s") * n_cores + lax.axis_index("c")) * per_worker

        @pl.loop(0, per_worker // window)
        def _(c):
            t0 = base + c * window
            for k in range(TOP_K):
                pltpu.sync_copy(dest_hbm.at[k, pl.ds(t0, window)], idx_v.at[k])
            pltpu.sync_copy(y_hbm.at[pl.ds(t0, window)], rows_v)
            for k in range(TOP_K):
                pltpu.sync_copy(rows_v, xs_hbm.at[idx_v.at[k]])

    return body(dest_t, y)


def _expert_kernel(be_ref, nu_ref, nv_ref, x_ref, wg_ref, bg_ref, wu_ref, bu_ref, wd_ref, bd_ref, o_ref):
    del be_ref
    j = pl.program_id(0)

    @pl.when(j < nu_ref[0])
    def _():
        row = lax.broadcasted_iota(I32, x_ref.shape, 0)
        xb = jnp.where(row < nv_ref[j], x_ref[...], 0.0).astype(BF16)
        gt = jnp.dot(xb, wg_ref[...], preferred_element_type=F32) + bg_ref[...]
        up = jnp.dot(xb, wu_ref[...], preferred_element_type=F32) + bu_ref[...]
        gt = jnp.minimum(gt, SWIGLU_LIMIT)
        up = jnp.clip(up, -SWIGLU_LIMIT, SWIGLU_LIMIT)
        hdn = gt * _sigmoid(SWIGLU_ALPHA * gt) * (up + 1.0)
        o_ref[...] = jnp.dot(hdn.astype(BF16), wd_ref[...], preferred_element_type=F32) + bd_ref[...]

    @pl.when(j >= nu_ref[0])
    def _():
        o_ref[...] = jnp.zeros_like(o_ref)


def _experts(block_e, n_used, n_valid, xs, wg, bg, wu, bu, wd, bd, expert_block):
    rows, d = xs.shape
    e, _, f = wg.shape
    xmap = lambda j, be, nu, nv: (jnp.minimum(j, nu[0] - 1), 0)
    wmap = lambda j, be, nu, nv: (be[j], 0, 0)
    grid_spec = pltpu.PrefetchScalarGridSpec(
        num_scalar_prefetch=3,
        grid=(rows // expert_block,),
        in_specs=[pl.BlockSpec((expert_block, d), xmap),
                  pl.BlockSpec((None, d, f), wmap), pl.BlockSpec((None, 1, f), wmap),
                  pl.BlockSpec((None, d, f), wmap), pl.BlockSpec((None, 1, f), wmap),
                  pl.BlockSpec((None, f, d), wmap), pl.BlockSpec((None, 1, d), wmap)],
        out_specs=pl.BlockSpec((expert_block, d), lambda j, be, nu, nv: (j, 0)),
    )
    return pl.pallas_call(
        _expert_kernel,
        grid_spec=grid_spec,
        out_shape=jax.ShapeDtypeStruct((rows, d), F32),
        compiler_params=_cparams("arbitrary"),
        name="experts",
    )(block_e, n_used, n_valid, xs, wg, bg.reshape(e, 1, f), wu, bu.reshape(e, 1, f), wd, bd.reshape(e, 1, d))


def _sc_gather(dest_t, ys, n, window=32):
    _, d = ys.shape
    info = plsc.get_sparse_core_info()
    n_cores = info.num_cores
    per_worker = n // (n_cores * info.num_subcores)
    mesh = plsc.VectorSubcoreMesh(core_axis_name="c", subcore_axis_name="s")

    @functools.partial(pl.kernel, mesh=mesh, out_type=jax.ShapeDtypeStruct((TOP_K, n, d), ys.dtype),
                       scratch_types=[pltpu.VMEM((TOP_K, window), I32), pltpu.VMEM((window, d), ys.dtype)],
                       name="sc_gather")
    def body(dest_hbm, ys_hbm, out_hbm, idx_v, rows_v):
        base = (lax.axis_index("s") * n_cores + lax.axis_index("c")) * per_worker

        @pl.loop(0, per_worker // window)
        def _(c):
            t0 = base + c * window
            for k in range(TOP_K):
                pltpu.sync_copy(dest_hbm.at[k, pl.ds(t0, window)], idx_v.at[k])
            for k in range(TOP_K):
                pltpu.sync_copy(ys_hbm.at[idx_v.at[k]], rows_v)
                pltpu.sync_copy(rows_v, out_hbm.at[k, pl.ds(t0, window)])

    return body(dest_t, ys)


def _combine_dense_kernel(gate_ref, y_ref, lng_ref, lnb_ref, g_ref, o_ref):
    gates = gate_ref[...]
    ffn = gates[:, 0:1] * g_ref[0]
    for k in range(1, TOP_K):
        ffn = ffn + gates[:, k:k + 1] * g_ref[k]
    o_ref[...] = _layer_norm(DEEPNORM_ALPHA * y_ref[...] + ffn, lng_ref[...], lnb_ref[...])


def _combine_dense(gates, y, ln_g, ln_b, gathered, tile=512):
    n, d = y.shape
    tile = min(tile, n)
    row = lambda i: (i, 0)
    const = lambda i: (0, 0)
    return pl.pallas_call(
        _combine_dense_kernel,
        grid=(n // tile,),
        in_specs=[pl.BlockSpec((tile, TOP_K), row), pl.BlockSpec((tile, d), row),
                  pl.BlockSpec((1, d), const), pl.BlockSpec((1, d), const),
                  pl.BlockSpec((TOP_K, tile, d), lambda i: (0, i, 0))],
        out_specs=pl.BlockSpec((tile, d), row),
        out_shape=jax.ShapeDtypeStruct((n, d), F32),
        compiler_params=_cparams("parallel"),
        name="combine_dense",
    )(gates, y, ln_g, ln_b, gathered)


MOE_ROW_BLOCK = 512


def _moe_ln2(y, idx, gates, w_gate, b_gate, w_up, b_up, w_down, b_down, ln2_g, ln2_b):
    n, d = y.shape
    n_blocks = n * TOP_K // MOE_ROW_BLOCK + N_EXPERTS
    rows = n_blocks * MOE_ROW_BLOCK
    dest, counts = _route(idx, MOE_ROW_BLOCK)
    cnt = counts[0, :N_EXPERTS]
    padded = (cnt + MOE_ROW_BLOCK - 1) // MOE_ROW_BLOCK * MOE_ROW_BLOCK
    p_end = jnp.cumsum(padded)
    block_row0 = jnp.arange(n_blocks, dtype=I32) * MOE_ROW_BLOCK
    block_e = jnp.minimum(jnp.sum(p_end[None, :] <= block_row0[:, None], axis=1), N_EXPERTS - 1).astype(I32)
    n_used = (p_end[-1:] // MOE_ROW_BLOCK).astype(I32)
    n_valid = jnp.clip(cnt[block_e] - (block_row0 - (p_end - padded)[block_e]), 0, MOE_ROW_BLOCK).astype(I32)
    dest_t = dest.T
    xs = _sc_dispatch(dest_t, y, rows)
    ys = _experts(block_e, n_used, n_valid, xs, w_gate.astype(BF16), b_gate.astype(F32), w_up.astype(BF16),
                  b_up.astype(F32), w_down.astype(BF16), b_down.astype(F32), MOE_ROW_BLOCK)
    gathered = _sc_gather(dest_t, ys, n)
    return _combine_dense(gates, y, ln2_g.reshape(1, d).astype(F32), ln2_b.reshape(1, d).astype(F32), gathered)


def kernel(x, w_in, sinks, w_proj_a, w_proj_b, w_out, ln1_g, ln1_b, router_w, router_b,
           w_gate, b_gate, w_up, b_up, w_down, b_down, ln2_g, ln2_b):
    bsz, seq, d = x.shape
    h = x.reshape(bsz * seq, d)
    for l in range(w_in.shape[0]):
        y, idx, gates = _token_mixer_ln1(h, w_in[l], sinks[l], w_proj_a[l], w_proj_b[l], w_out[l],
                                         ln1_g[l], ln1_b[l], router_w[l], router_b[l], bsz, seq)
        h = _moe_ln2(y, idx, gates, w_gate[l], b_gate[l], w_up[l], b_up[l], w_down[l], b_down[l],
                     ln2_g[l], ln2_b[l])
    return h.reshape(bsz, seq, d)
```

```python
import functools

import jax
import jax.numpy as jnp
from jax import lax
from jax.experimental import pallas as pl
from jax.experimental.pallas import tpu as pltpu
from jax.experimental.pallas import tpu_sc as plsc

F32 = jnp.float32
BF16 = jnp.bfloat16
I32 = jnp.int32

HEAD_DIM = 64
DIL_GROUPS = ((128, 1), (512, 4), (2048, 16))
DIL_HEADS = 4
N_DIL = len(DIL_GROUPS)
DIL_OUT = DIL_HEADS * HEAD_DIM
DIL_WIDTH = N_DIL * DIL_OUT
SWA_Q_HEADS = 16
SWA_KV_HEADS = 2
SWA_REP = SWA_Q_HEADS // SWA_KV_HEADS
SWA_WINDOW = 128
SWA_Q_WIDTH = SWA_Q_HEADS * HEAD_DIM
SWA_KV_WIDTH = SWA_KV_HEADS * HEAD_DIM
N_ALIBI_HEADS = SWA_Q_HEADS + N_DIL * DIL_HEADS
ATTN_BLOCK = 128
N_EXPERTS = 32
TOP_K = 4
SWIGLU_LIMIT = 7.0
SWIGLU_ALPHA = 1.702
LN_EPS = 1e-5
DEPTH = 1
DEEPNORM_ALPHA = (2 * DEPTH) ** 0.25
NEG_INF = -1e30

LANES = 128
VMEM_LIMIT_BYTES = 56 * 1024 * 1024

A_QKV_W = 3 * DIL_WIDTH
B_Q_OFF = A_QKV_W
B_KV_OFF = B_Q_OFF + SWA_Q_WIDTH
GATE_OFF = B_KV_OFF + 2 * SWA_KV_WIDTH


def _cparams(*sem):
    return pltpu.CompilerParams(dimension_semantics=sem, vmem_limit_bytes=VMEM_LIMIT_BYTES)


def _in_proj_kernel(x_ref, w_ref, a0_ref, a1_ref, a2_ref, hq_ref, hkv_ref, g_ref, *scratch, d_model, tm):
    xb = x_ref[...].astype(BF16)
    segments = ((hq_ref, B_Q_OFF, SWA_Q_WIDTH), (hkv_ref, B_KV_OFF, 2 * SWA_KV_WIDTH),
                (g_ref, GATE_OFF, 2 * d_model))
    for out_ref, col0, width in segments:
        for c in range(0, width, 512):
            w = min(512, width - c)
            r = jnp.dot(xb, w_ref[:, col0 + c:col0 + c + w], preferred_element_type=F32)
            out_ref[:, c:c + w] = r.astype(out_ref.dtype)
    for g, a_ref in enumerate((a0_ref, a1_ref, a2_ref)):
        dil = DIL_GROUPS[g][1]
        per = tm // dil
        for part in range(3):
            col0 = part * DIL_WIDTH + g * DIL_OUT
            res = jnp.dot(xb, w_ref[:, col0:col0 + DIL_OUT], preferred_element_type=F32)
            if dil == 1:
                a_ref[0, :, part * DIL_OUT:(part + 1) * DIL_OUT] = res.astype(a_ref.dtype)
                continue
            for half in range(DIL_OUT // LANES):
                stage = scratch[part * (DIL_OUT // LANES) + half]
                stage[...] = res[:, half * LANES:(half + 1) * LANES]
                c0 = part * DIL_OUT + half * LANES
                for r in range(dil):
                    a_ref[r, :, c0:c0 + LANES] = stage[pl.ds(r, per, stride=dil), :].astype(a_ref.dtype)


def _in_proj(x2, w_in_bf, bsz, seq, tm=512):
    n, d = x2.shape
    cols = w_in_bf.shape[1]
    tiles = seq // tm
    row = lambda i: (i, 0)
    dil_spec = lambda dil: pl.BlockSpec((None, dil, tm // dil, 3 * DIL_OUT), lambda i: (i // tiles, 0, i % tiles, 0))
    dil_shape = lambda dil: jax.ShapeDtypeStruct((bsz, dil, seq // dil, 3 * DIL_OUT), BF16)
    dils = [dil for _, dil in DIL_GROUPS]
    return pl.pallas_call(
        functools.partial(_in_proj_kernel, d_model=d, tm=tm),
        grid=(n // tm,),
        in_specs=[pl.BlockSpec((tm, d), row),
                  pl.BlockSpec((d, cols), lambda i: (0, 0), pipeline_mode=pl.Buffered(1))],
        out_specs=[dil_spec(dil) for dil in dils]
        + [pl.BlockSpec((tm, SWA_Q_WIDTH), row), pl.BlockSpec((tm, 2 * SWA_KV_WIDTH), row),
           pl.BlockSpec((tm, 2 * d), row)],
        out_shape=[dil_shape(dil) for dil in dils]
        + [jax.ShapeDtypeStruct((n, SWA_Q_WIDTH), BF16), jax.ShapeDtypeStruct((n, 2 * SWA_KV_WIDTH), BF16),
           jax.ShapeDtypeStruct((n, 2 * d), BF16)],
        scratch_shapes=[pltpu.VMEM((tm, LANES), F32)] * (3 * DIL_OUT // LANES),
        compiler_params=_cparams("parallel"),
        name="in_proj",
    )(x2, w_in_bf)


_NT = (((1,), (1,)), ((), ()))


def _per_head_column(ref, head0, n_heads, scale=1.0):
    head_of_row = lax.broadcasted_iota(I32, (n_heads * ATTN_BLOCK, 1), 0) // ATTN_BLOCK
    col = jnp.zeros((n_heads * ATTN_BLOCK, 1), F32)
    for h in range(n_heads):
        col = jnp.where(head_of_row == h, ref[head0 + h] * scale, col)
    return col


def _store_band_bias(bias_ref, slope_col, max_diff):
    rows = slope_col.shape[0]
    qi = lax.broadcasted_iota(I32, (rows, ATTN_BLOCK), 0) % ATTN_BLOCK
    kj = lax.broadcasted_iota(I32, (rows, ATTN_BLOCK), 1)
    diff_prev = qi - kj + ATTN_BLOCK
    diff_cur = qi - kj
    bias_ref[0] = jnp.where(diff_prev <= max_diff, -slope_col * diff_prev.astype(F32), NEG_INF)
    bias_ref[1] = jnp.where((diff_cur >= 0) & (diff_cur <= max_diff), -slope_col * diff_cur.astype(F32), NEG_INF)


def _band_softmax(s_prev, s_cur, sink_col):
    m = jnp.max(jnp.maximum(s_prev, s_cur), axis=-1, keepdims=True)
    if sink_col is not None:
        m = jnp.maximum(m, sink_col)
    p_prev = jnp.exp(s_prev - m)
    p_cur = jnp.exp(s_cur - m)
    denom = jnp.sum(p_prev + p_cur, axis=-1, keepdims=True)
    if sink_col is not None:
        denom = denom + jnp.exp(sink_col - m)
    return p_prev.astype(BF16), p_cur.astype(BF16), m, denom


def _dil_attn_kernel(slope_ref, q_ref, kc_ref, kp_ref, vc_ref, vp_ref, o_ref, lse_ref, bias_ref,
                     *, tq, max_diff, dist_scale):
    first = pl.program_id(2) == 0
    _store_band_bias(bias_ref, _per_head_column(slope_ref, 0, DIL_HEADS, dist_scale), max_diff)
    head_cols = [slice(h * HEAD_DIM, (h + 1) * HEAD_DIM) for h in range(DIL_HEADS)]
    for i in range(tq // ATTN_BLOCK):
        rows = slice(i * ATTN_BLOCK, (i + 1) * ATTN_BLOCK)
        prev_rows = slice((i - 1) * ATTN_BLOCK, i * ATTN_BLOCK)
        k_prev = [kp_ref[:, c] if i == 0 else kc_ref[prev_rows, c] for c in head_cols]
        v_prev = [vp_ref[:, c] if i == 0 else vc_ref[prev_rows, c] for c in head_cols]
        q = [q_ref[rows, c] * 0.125 for c in head_cols]
        s_prev = jnp.concatenate([lax.dot_general(q[h], k_prev[h], _NT, preferred_element_type=F32)
                                  for h in range(DIL_HEADS)], axis=0) + bias_ref[0]
        s_cur = jnp.concatenate([lax.dot_general(q[h], kc_ref[rows, head_cols[h]], _NT, preferred_element_type=F32)
                                 for h in range(DIL_HEADS)], axis=0) + bias_ref[1]
        if i == 0:
            s_prev = jnp.where(first, NEG_INF, s_prev)
        p_prev, p_cur, m, denom = _band_softmax(s_prev, s_cur, None)
        lse = m + jnp.log(denom)
        for h in range(DIL_HEADS):
            hr = slice(h * ATTN_BLOCK, (h + 1) * ATTN_BLOCK)
            o = (jnp.dot(p_prev[hr], v_prev[h], preferred_element_type=F32)
                 + jnp.dot(p_cur[hr], vc_ref[rows, head_cols[h]], preferred_element_type=F32)) / denom[hr]
            o_ref[rows, head_cols[h]] = o.astype(o_ref.dtype)
            lse_ref[rows, head_cols[h]] = jnp.broadcast_to(lse[hr], (ATTN_BLOCK, HEAD_DIM))


def _dil_attention(a_g, slopes_g, g):
    window, dil = DIL_GROUPS[g]
    bsz, _, sub_len, _ = a_g.shape
    tq = min(512, sub_len)
    nqb = tq // ATTN_BLOCK
    cur = lambda part: (lambda b, r, m: (b, r, m, part))
    prev = lambda part: (lambda b, r, m: (b, r, jnp.maximum(m * nqb - 1, 0), part))
    blk = lambda rows, imap: pl.BlockSpec((None, None, rows, DIL_OUT), imap)
    return pl.pallas_call(
        functools.partial(_dil_attn_kernel, tq=tq, max_diff=window // dil, dist_scale=float(dil)),
        grid=(bsz, dil, sub_len // tq),
        in_specs=[pl.BlockSpec(memory_space=pltpu.SMEM),
                  blk(tq, cur(0)), blk(tq, cur(1)), blk(ATTN_BLOCK, prev(1)),
                  blk(tq, cur(2)), blk(ATTN_BLOCK, prev(2))],
        out_specs=[blk(tq, cur(0)), blk(tq, cur(0))],
        out_shape=[jax.ShapeDtypeStruct((bsz, dil, sub_len, DIL_OUT), BF16),
                   jax.ShapeDtypeStruct((bsz, dil, sub_len, DIL_OUT), F32)],
        scratch_shapes=[pltpu.VMEM((2, DIL_HEADS * ATTN_BLOCK, ATTN_BLOCK), F32)],
        compiler_params=_cparams("parallel", "parallel", "arbitrary"),
        name=f"dil_attn_g{g}",
    )(slopes_g, a_g, a_g, a_g, a_g, a_g)


def _swa_attn_kernel(slope_ref, sink_ref, q_ref, kvc_ref, kvp_ref, o_ref, bias_ref, *, tq):
    first = pl.program_id(1) == 0
    for kvh in range(SWA_KV_HEADS):
        head0 = kvh * SWA_REP
        _store_band_bias(bias_ref, _per_head_column(slope_ref, head0, SWA_REP), SWA_WINDOW - 1)
        sink_col = _per_head_column(sink_ref, head0, SWA_REP)
        kcols = slice(kvh * HEAD_DIM, (kvh + 1) * HEAD_DIM)
        vcols = slice(SWA_KV_WIDTH + kvh * HEAD_DIM, SWA_KV_WIDTH + (kvh + 1) * HEAD_DIM)
        head_cols = [slice((head0 + r) * HEAD_DIM, (head0 + r + 1) * HEAD_DIM) for r in range(SWA_REP)]
        for i in range(tq // ATTN_BLOCK):
            rows = slice(i * ATTN_BLOCK, (i + 1) * ATTN_BLOCK)
            prev_rows = slice((i - 1) * ATTN_BLOCK, i * ATTN_BLOCK)
            k_prev = kvp_ref[:, kcols] if i == 0 else kvc_ref[prev_rows, kcols]
            v_prev = kvp_ref[:, vcols] if i == 0 else kvc_ref[prev_rows, vcols]
            q = jnp.concatenate([q_ref[rows, c] for c in head_cols], axis=0) * 0.125
            s_prev = lax.dot_general(q, k_prev, _NT, preferred_element_type=F32) + bias_ref[0]
            s_cur = lax.dot_general(q, kvc_ref[rows, kcols], _NT, preferred_element_type=F32) + bias_ref[1]
            if i == 0:
                s_prev = jnp.where(first, NEG_INF, s_prev)
            p_prev, p_cur, _, denom = _band_softmax(s_prev, s_cur, sink_col)
            o = (jnp.dot(p_prev, v_prev, preferred_element_type=F32)
                 + jnp.dot(p_cur, kvc_ref[rows, vcols], preferred_element_type=F32)) / denom
            for r in range(SWA_REP):
                o_ref[rows, head_cols[r]] = o[r * ATTN_BLOCK:(r + 1) * ATTN_BLOCK].astype(o_ref.dtype)


def _swa_attention(hq, hkv, slopes_b, sinks, bsz, seq, tq=512):
    nqb = tq // ATTN_BLOCK
    hq3 = hq.reshape(bsz, seq, SWA_Q_WIDTH)
    hkv3 = hkv.reshape(bsz, seq, 2 * SWA_KV_WIDTH)
    smem = pl.BlockSpec(memory_space=pltpu.SMEM)
    out = pl.pallas_call(
        functools.partial(_swa_attn_kernel, tq=tq),
        grid=(bsz, seq // tq),
        in_specs=[smem, smem,
                  pl.BlockSpec((None, tq, SWA_Q_WIDTH), lambda b, m: (b, m, 0)),
                  pl.BlockSpec((None, tq, 2 * SWA_KV_WIDTH), lambda b, m: (b, m, 0)),
                  pl.BlockSpec((None, ATTN_BLOCK, 2 * SWA_KV_WIDTH),
                               lambda b, m: (b, jnp.maximum(m * nqb - 1, 0), 0))],
        out_specs=pl.BlockSpec((None, tq, SWA_Q_WIDTH), lambda b, m: (b, m, 0)),
        out_shape=jax.ShapeDtypeStruct((bsz, seq, SWA_Q_WIDTH), BF16),
        scratch_shapes=[pltpu.VMEM((2, SWA_REP * ATTN_BLOCK, ATTN_BLOCK), F32)],
        compiler_params=_cparams("parallel", "arbitrary"),
        name="swa_attn",
    )(slopes_b, sinks, hq3, hkv3, hkv3)
    return out.reshape(bsz * seq, SWA_Q_WIDTH)


def _sigmoid(x):
    return 0.5 * (jnp.tanh(0.5 * x) + 1.0)


def _layer_norm(z, g, b):
    mu = jnp.mean(z, axis=-1, keepdims=True)
    zc = z - mu
    var = jnp.mean(zc * zc, axis=-1, keepdims=True)
    return zc * lax.rsqrt(var + LN_EPS) * g + b


def _mix_out_kernel(x_ref, o0_ref, o1_ref, o2_ref, l0_ref, l1_ref, l2_ref, ob_ref, g_ref,
                    wpa_ref, wpb_ref, wo_ref, lng_ref, lnb_ref, rw_ref, rb_ref,
                    y_ref, idx_ref, gate_ref, *scratch, d_model):
    def natural(ref, stages):
        dil, per, _ = ref.shape
        if dil == 1:
            return ref[0].astype(F32)
        for half, stage in enumerate(stages):
            for r in range(dil):
                stage[pl.ds(r, per, stride=dil), :] = ref[r, :, half * LANES:(half + 1) * LANES].astype(F32)
        return jnp.concatenate([stage[...] for stage in stages], axis=1)

    o0, o1, o2 = natural(o0_ref, None), natural(o1_ref, scratch[0:2]), natural(o2_ref, scratch[2:4])
    l0, l1, l2 = natural(l0_ref, None), natural(l1_ref, scratch[4:6]), natural(l2_ref, scratch[6:8])
    lm = jnp.maximum(jnp.maximum(l0, l1), l2)
    e0, e1, e2 = jnp.exp(l0 - lm), jnp.exp(l1 - lm), jnp.exp(l2 - lm)
    esum = e0 + e1 + e2
    out_a = (e0 / esum) * o0 + (e1 / esum) * o1 + (e2 / esum) * o2
    pa = jnp.dot(out_a.astype(BF16), wpa_ref[...], preferred_element_type=F32)
    pb = jnp.dot(ob_ref[...], wpb_ref[...], preferred_element_type=F32)
    ga = _sigmoid(g_ref[:, :d_model].astype(F32))
    gb = _sigmoid(g_ref[:, d_model:].astype(F32))
    merged = ga * pa + gb * pb
    mix = jnp.dot(merged.astype(BF16), wo_ref[...], preferred_element_type=F32)
    y = _layer_norm(DEEPNORM_ALPHA * x_ref[...] + mix, lng_ref[...], lnb_ref[...])
    y_ref[...] = y

    logits = jnp.dot(y.astype(BF16), rw_ref[...], preferred_element_type=F32) + rb_ref[...]
    lane = lax.broadcasted_iota(I32, logits.shape, 1)
    logits = jnp.where(lane < N_EXPERTS, logits, -jnp.inf)
    idx_out = jnp.zeros(logits.shape, I32)
    val_out = jnp.full(logits.shape, -jnp.inf, F32)
    for k in range(TOP_K):
        top = jnp.max(logits, axis=-1, keepdims=True)
        top_idx = jnp.min(jnp.where(logits == top, lane, LANES), axis=-1, keepdims=True)
        idx_out = jnp.where(lane == k, top_idx, idx_out)
        val_out = jnp.where(lane == k, top, val_out)
        logits = jnp.where(lane == top_idx, -jnp.inf, logits)
    ev = jnp.exp(val_out - jnp.max(val_out, axis=-1, keepdims=True))
    gates = ev / jnp.sum(ev, axis=-1, keepdims=True)
    idx_ref[...] = idx_out[:, :TOP_K]
    gate_ref[...] = gates[:, :TOP_K]


def _mix_out(x2, o_g, lse_g, out_b, gates_h, wpa, wpb, wo, ln_g, ln_b, rw_pad, rb_pad, seq, tm=512):
    n, d = x2.shape
    tiles = seq // tm
    row = lambda i: (i, 0)
    const = lambda i: (0, 0)
    rb = lambda w: pl.BlockSpec((tm, w), row)
    full = lambda a: pl.BlockSpec(a.shape, const)
    dil_specs = [pl.BlockSpec((None, dil, tm // dil, DIL_OUT), lambda i: (i // tiles, 0, i % tiles, 0))
                 for _, dil in DIL_GROUPS]
    return pl.pallas_call(
        functools.partial(_mix_out_kernel, d_model=d),
        grid=(n // tm,),
        in_specs=[rb(d)] + dil_specs + dil_specs + [rb(SWA_Q_WIDTH), rb(2 * d),
                  full(wpa), full(wpb), full(wo), full(ln_g), full(ln_b), full(rw_pad), full(rb_pad)],
        out_specs=[rb(d), rb(TOP_K), rb(TOP_K)],
        out_shape=[jax.ShapeDtypeStruct((n, d), F32),
                   jax.ShapeDtypeStruct((n, TOP_K), I32), jax.ShapeDtypeStruct((n, TOP_K), F32)],
        scratch_shapes=[pltpu.VMEM((tm, LANES), F32)] * (4 * DIL_OUT // LANES),
        compiler_params=_cparams("parallel"),
        name="mix_out",
    )(x2, *o_g, *lse_g, out_b, gates_h, wpa, wpb, wo, ln_g, ln_b, rw_pad, rb_pad)


def _token_mixer_ln1(x2, w_in, sinks, w_proj_a, w_proj_b, w_out, ln1_g, ln1_b, router_w, router_b, bsz, seq):
    n, d = x2.shape
    heads = jnp.arange(1, N_ALIBI_HEADS + 1, dtype=F32)
    slopes = jnp.exp2(-8.0 * heads / N_ALIBI_HEADS)
    *a_g, hq, hkv, gates_h = _in_proj(x2, w_in.astype(BF16), bsz, seq)
    o_g, lse_g = [], []
    for g in range(N_DIL):
        sl = slopes[SWA_Q_HEADS + g * DIL_HEADS:SWA_Q_HEADS + (g + 1) * DIL_HEADS]
        o, lse = _dil_attention(a_g[g], sl, g)
        o_g.append(o)
        lse_g.append(lse)
    out_b = _swa_attention(hq, hkv, slopes[:SWA_Q_HEADS], sinks.astype(F32), bsz, seq)
    rw_pad = jnp.zeros((d, LANES), BF16).at[:, :N_EXPERTS].set(router_w.astype(BF16))
    rb_pad = jnp.zeros((1, LANES), F32).at[0, :N_EXPERTS].set(router_b.astype(F32))
    return _mix_out(x2, o_g, lse_g, out_b, gates_h, w_proj_a.astype(BF16), w_proj_b.astype(BF16),
                    w_out.astype(BF16), ln1_g.reshape(1, d).astype(F32), ln1_b.reshape(1, d).astype(F32),
                    rw_pad, rb_pad, seq)


def _route_kernel(idx_ref, dest_ref, cnt_ref, carry_ref, pstart_ref, *, expert_block):
    phase, i = pl.program_id(0), pl.program_id(1)
    idx = idx_ref[...]
    t = idx.shape[0]
    lane = lax.broadcasted_iota(I32, (t, LANES), 1)
    onehot = [lane == idx[:, k:k + 1] for k in range(TOP_K)]
    multi = sum(oh.astype(F32) for oh in onehot)
    tile_cnt = jnp.sum(multi, axis=0, keepdims=True).astype(I32)

    @pl.when((phase == 0) & (i == 0))
    def _():
        carry_ref[...] = jnp.zeros_like(carry_ref)

    @pl.when((phase == 1) & (i == 0))
    def _():
        counts = carry_ref[...]
        padded = (counts + (expert_block - 1)) & (-expert_block)
        lane8 = lax.broadcasted_iota(I32, counts.shape, 1)
        incl = padded
        for sh in (1, 2, 4, 8, 16):
            incl = incl + jnp.where(lane8 >= sh, pltpu.roll(incl, sh, axis=1), 0)
        pstart_ref[...] = incl - padded
        cnt_ref[...] = counts
        carry_ref[...] = jnp.zeros_like(carry_ref)

    @pl.when(phase == 1)
    def _():
        r = lax.broadcasted_iota(I32, (t, t), 0)
        c = lax.broadcasted_iota(I32, (t, t), 1)
        earlier = (c < r).astype(BF16)
        cum = jnp.dot(earlier, multi.astype(BF16), preferred_element_type=F32).astype(I32)
        base = cum + carry_ref[0:1, :] + pstart_ref[0:1, :]
        dest = jnp.zeros((t, LANES), I32)
        for k in range(TOP_K):
            d_k = jnp.sum(jnp.where(onehot[k], base, 0), axis=-1, keepdims=True)
            dest = jnp.where(lane == k, d_k, dest)
        dest_ref[...] = dest[:, :TOP_K]

    carry_ref[...] = carry_ref[...] + tile_cnt


def _route(idx, expert_block, tile=512):
    n = idx.shape[0]
    tile = min(tile, n)
    return pl.pallas_call(
        functools.partial(_route_kernel, expert_block=expert_block),
        grid=(2, n // tile),
        in_specs=[pl.BlockSpec((tile, TOP_K), lambda p, i: (i, 0))],
        out_specs=[pl.BlockSpec((tile, TOP_K), lambda p, i: (i * p, 0)),
                   pl.BlockSpec((8, LANES), lambda p, i: (0, 0))],
        out_shape=[jax.ShapeDtypeStruct((n, TOP_K), I32), jax.ShapeDtypeStruct((8, LANES), I32)],
        scratch_shapes=[pltpu.VMEM((8, LANES), I32), pltpu.VMEM((8, LANES), I32)],
        compiler_params=_cparams("arbitrary", "arbitrary"),
        name="route",
    )(idx)


def _sc_dispatch(dest_t, y, rows, window=32):
    n, d = y.shape
    info = plsc.get_sparse_core_info()
    n_cores = info.num_cores
    per_worker = n // (n_cores * info.num_subcores)
    mesh = plsc.VectorSubcoreMesh(core_axis_name="c", subcore_axis_name="s")

    @functools.partial(pl.kernel, mesh=mesh, out_type=jax.ShapeDtypeStruct((rows, d), y.dtype),
                       scratch_types=[pltpu.VMEM((TOP_K, window), I32), pltpu.VMEM((window, d), y.dtype)],
                       name="sc_dispatch")
    def body(dest_hbm, y_hbm, xs_hbm, idx_v, rows_v):
        base = (lax.axis_index("s") * n_cores + lax.axis_index("c")) * per_worker

        @pl.loop(0, per_worker // window)
        def _(c):
            t0 = base + c * window
            for k in range(TOP_K):
                pltpu.sync_copy(dest_hbm.at[k, pl.ds(t0, window)], idx_v.at[k])
            pltpu.sync_copy(y_hbm.at[pl.ds(t0, window)], rows_v)
            for k in range(TOP_K):
                pltpu.sync_copy(rows_v, xs_hbm.at[idx_v.at[k]])

    return body(dest_t, y)


def _expert_kernel(be_ref, nu_ref, nv_ref, x_ref, wg_ref, bg_ref, wu_ref, bu_ref, wd_ref, bd_ref, o_ref):
    del be_ref
    j = pl.program_id(0)

    @pl.when(j < nu_ref[0])
    def _():
        row = lax.broadcasted_iota(I32, x_ref.shape, 0)
        xb = jnp.where(row < nv_ref[j], x_ref[...], 0.0).astype(BF16)
        gt = jnp.dot(xb, wg_ref[...], preferred_element_type=F32) + bg_ref[...]
        up = jnp.dot(xb, wu_ref[...], preferred_element_type=F32) + bu_ref[...]
        gt = jnp.minimum(gt, SWIGLU_LIMIT)
        up = jnp.clip(up, -SWIGLU_LIMIT, SWIGLU_LIMIT)
        hdn = gt * _sigmoid(SWIGLU_ALPHA * gt) * (up + 1.0)
        o_ref[...] = jnp.dot(hdn.astype(BF16), wd_ref[...], preferred_element_type=F32) + bd_ref[...]

    @pl.when(j >= nu_ref[0])
    def _():
        o_ref[...] = jnp.zeros_like(o_ref)


def _experts(block_e, n_used, n_valid, xs, wg, bg, wu, bu, wd, bd, expert_block):
    rows, d = xs.shape
    e, _, f = wg.shape
    xmap = lambda j, be, nu, nv: (jnp.minimum(j, nu[0] - 1), 0)
    wmap = lambda j, be, nu, nv: (be[j], 0, 0)
    grid_spec = pltpu.PrefetchScalarGridSpec(
        num_scalar_prefetch=3,
        grid=(rows // expert_block,),
        in_specs=[pl.BlockSpec((expert_block, d), xmap),
                  pl.BlockSpec((None, d, f), wmap), pl.BlockSpec((None, 1, f), wmap),
                  pl.BlockSpec((None, d, f), wmap), pl.BlockSpec((None, 1, f), wmap),
                  pl.BlockSpec((None, f, d), wmap), pl.BlockSpec((None, 1, d), wmap)],
        out_specs=pl.BlockSpec((expert_block, d), lambda j, be, nu, nv: (j, 0)),
    )
    return pl.pallas_call(
        _expert_kernel,
        grid_spec=grid_spec,
        out_shape=jax.ShapeDtypeStruct((rows, d), F32),
        compiler_params=_cparams("arbitrary"),
        name="experts",
    )(block_e, n_used, n_valid, xs, wg, bg.reshape(e, 1, f), wu, bu.reshape(e, 1, f), wd, bd.reshape(e, 1, d))


def _sc_gather(dest_t, ys, n, window=32):
    _, d = ys.shape
    info = plsc.get_sparse_core_info()
    n_cores = info.num_cores
    per_worker = n // (n_cores * info.num_subcores)
    mesh = plsc.VectorSubcoreMesh(core_axis_name="c", subcore_axis_name="s")

    n_chunks = per_worker // window
    n_buf = 2

    @functools.partial(pl.kernel, mesh=mesh, out_type=jax.ShapeDtypeStruct((TOP_K, n, d), ys.dtype),
                       scratch_types=[pltpu.VMEM((TOP_K, per_worker), I32)]
                       + [pltpu.VMEM((window, d), ys.dtype)] * n_buf + [pltpu.SemaphoreType.DMA] * (2 * n_buf),
                       name="sc_gather")
    def body(dest_hbm, ys_hbm, out_hbm, idx_v, buf0, buf1, gsem0, gsem1, ssem0, ssem1):
        bufs, gsem, ssem = (buf0, buf1), (gsem0, gsem1), (ssem0, ssem1)
        base = (lax.axis_index("s") * n_cores + lax.axis_index("c")) * per_worker
        for k in range(TOP_K):
            pltpu.sync_copy(dest_hbm.at[k, pl.ds(base, per_worker)], idx_v.at[k])

        def gather(c, k):
            return pltpu.make_async_copy(ys_hbm.at[idx_v.at[k, pl.ds(c * window, window)]], bufs[k % n_buf],
                                         gsem[k % n_buf])

        def store(c, k):
            return pltpu.make_async_copy(bufs[k % n_buf], out_hbm.at[k, pl.ds(base + c * window, window)],
                                         ssem[k % n_buf])

        gather(0, 0).start()

        @pl.loop(0, n_chunks)
        def _(c):
            for k in range(TOP_K):
                if k > 0:
                    store(c, k - 1).wait()
                    if k + 1 < TOP_K:
                        gather(c, k + 1).start()
                    else:
                        @pl.when(c + 1 < n_chunks)
                        def _():
                            gather(c + 1, 0).start()
                else:
                    @pl.when(c > 0)
                    def _():
                        store(c - 1, TOP_K - 1).wait()
                    gather(c, 1).start()
                gather(c, k).wait()
                store(c, k).start()

        store(n_chunks - 1, TOP_K - 1).wait()

    return body(dest_t, ys)


def _combine_dense_kernel(gate_ref, y_ref, lng_ref, lnb_ref, g_ref, o_ref):
    gates = gate_ref[...]
    ffn = gates[:, 0:1] * g_ref[0]
    for k in range(1, TOP_K):
        ffn = ffn + gates[:, k:k + 1] * g_ref[k]
    o_ref[...] = _layer_norm(DEEPNORM_ALPHA * y_ref[...] + ffn, lng_ref[...], lnb_ref[...])


def _combine_dense(gates, y, ln_g, ln_b, gathered, tile=512):
    n, d = y.shape
    tile = min(tile, n)
    row = lambda i: (i, 0)
    const = lambda i: (0, 0)
    return pl.pallas_call(
        _combine_dense_kernel,
        grid=(n // tile,),
        in_specs=[pl.BlockSpec((tile, TOP_K), row), pl.BlockSpec((tile, d), row),
                  pl.BlockSpec((1, d), const), pl.BlockSpec((1, d), const),
                  pl.BlockSpec((TOP_K, tile, d), lambda i: (0, i, 0))],
        out_specs=pl.BlockSpec((tile, d), row),
        out_shape=jax.ShapeDtypeStruct((n, d), F32),
        compiler_params=_cparams("parallel"),
        name="combine_dense",
    )(gates, y, ln_g, ln_b, gathered)


MOE_ROW_BLOCK = 512


def _moe_ln2(y, idx, gates, w_gate, b_gate, w_up, b_up, w_down, b_down, ln2_g, ln2_b):
    n, d = y.shape
    n_blocks = n * TOP_K // MOE_ROW_BLOCK + N_EXPERTS
    rows = n_blocks * MOE_ROW_BLOCK
    dest, counts = _route(idx, MOE_ROW_BLOCK)
    cnt = counts[0, :N_EXPERTS]
    padded = (cnt + MOE_ROW_BLOCK - 1) // MOE_ROW_BLOCK * MOE_ROW_BLOCK
    p_end = jnp.cumsum(padded)
    block_row0 = jnp.arange(n_blocks, dtype=I32) * MOE_ROW_BLOCK
    block_e = jnp.minimum(jnp.sum(p_end[None, :] <= block_row0[:, None], axis=1), N_EXPERTS - 1).astype(I32)
    n_used = (p_end[-1:] // MOE_ROW_BLOCK).astype(I32)
    n_valid = jnp.clip(cnt[block_e] - (block_row0 - (p_end - padded)[block_e]), 0, MOE_ROW_BLOCK).astype(I32)
    dest_t = dest.T
    xs = _sc_dispatch(dest_t, y, rows)
    ys = _experts(block_e, n_used, n_valid, xs, w_gate.astype(BF16), b_gate.astype(F32), w_up.astype(BF16),
                  b_up.astype(F32), w_down.astype(BF16), b_down.astype(F32), MOE_ROW_BLOCK)
    gathered = _sc_gather(dest_t, ys, n)
    return _combine_dense(gates, y, ln2_g.reshape(1, d).astype(F32), ln2_b.reshape(1, d).astype(F32), gathered)


def kernel(x, w_in, sinks, w_proj_a, w_proj_b, w_out, ln1_g, ln1_b, router_w, router_b,
           w_gate, b_gate, w_up, b_up, w_down, b_down, ln2_g, ln2_b):
    bsz, seq, d = x.shape
    h = x.reshape(bsz * seq, d)
    for l in range(w_in.shape[0]):
        y, idx, gates = _token_mixer_ln1(h, w_in[l], sinks[l], w_proj_a[l], w_proj_b[l], w_out[l],
                                         ln1_g[l], ln1_b[l], router_w[l], router_b[l], bsz, seq)
        h = _moe_ln2(y, idx, gates, w_gate[l], b_gate[l], w_up[l], b_up[l], w_down[l], b_down[l],
                     ln2_g[l], ln2_b[l])
    return h.reshape(bsz, seq, d)
```

```python
import functools

import jax
import jax.numpy as jnp
from jax import lax
from jax.experimental import pallas as pl
from jax.experimental.pallas import tpu as pltpu
from jax.experimental.pallas import tpu_sc as plsc

F32 = jnp.float32
BF16 = jnp.bfloat16
I32 = jnp.int32

HEAD_DIM = 64
DIL_GROUPS = ((128, 1), (512, 4), (2048, 16))
DIL_HEADS = 4
N_DIL = len(DIL_GROUPS)
DIL_OUT = DIL_HEADS * HEAD_DIM
DIL_WIDTH = N_DIL * DIL_OUT
SWA_Q_HEADS = 16
SWA_KV_HEADS = 2
SWA_REP = SWA_Q_HEADS // SWA_KV_HEADS
SWA_WINDOW = 128
SWA_Q_WIDTH = SWA_Q_HEADS * HEAD_DIM
SWA_KV_WIDTH = SWA_KV_HEADS * HEAD_DIM
N_ALIBI_HEADS = SWA_Q_HEADS + N_DIL * DIL_HEADS
ATTN_BLOCK = 128
N_EXPERTS = 32
TOP_K = 4
SWIGLU_LIMIT = 7.0
SWIGLU_ALPHA = 1.702
LN_EPS = 1e-5
DEPTH = 1
DEEPNORM_ALPHA = (2 * DEPTH) ** 0.25
NEG_INF = -1e30

LANES = 128
VMEM_LIMIT_BYTES = 56 * 1024 * 1024

A_QKV_W = 3 * DIL_WIDTH
B_Q_OFF = A_QKV_W
B_KV_OFF = B_Q_OFF + SWA_Q_WIDTH
GATE_OFF = B_KV_OFF + 2 * SWA_KV_WIDTH


def _cparams(*sem):
    return pltpu.CompilerParams(dimension_semantics=sem, vmem_limit_bytes=VMEM_LIMIT_BYTES)


def _pack_bf16_pairs(x):
    c = x.shape[1] // 2
    lo = lax.bitcast_convert_type(x[:, :c].astype(BF16).astype(F32), jnp.uint32)
    hi = lax.bitcast_convert_type(x[:, c:].astype(BF16).astype(F32), jnp.uint32)
    return (lo >> 16) | hi


def _unpack_bf16_pairs(w):
    lo = lax.bitcast_convert_type(w << 16, F32)
    hi = lax.bitcast_convert_type(w & jnp.uint32(0xFFFF0000), F32)
    return lo, hi


def _in_proj_kernel(x_ref, w_ref, a0_ref, a1_ref, a2_ref, hq_ref, hkv_ref, g_ref, *scratch, d_model, tm):
    xb = x_ref[...].astype(BF16)
    segments = ((hq_ref, B_Q_OFF, SWA_Q_WIDTH), (hkv_ref, B_KV_OFF, 2 * SWA_KV_WIDTH),
                (g_ref, GATE_OFF, 2 * d_model))
    for out_ref, col0, width in segments:
        for c in range(0, width, 512):
            w = min(512, width - c)
            r = jnp.dot(xb, w_ref[:, col0 + c:col0 + c + w], preferred_element_type=F32)
            out_ref[:, c:c + w] = r.astype(out_ref.dtype)
    for g, a_ref in enumerate((a0_ref, a1_ref, a2_ref)):
        dil = DIL_GROUPS[g][1]
        per = tm // dil
        for part in range(3):
            col0 = part * DIL_WIDTH + g * DIL_OUT
            res = jnp.dot(xb, w_ref[:, col0:col0 + DIL_OUT], preferred_element_type=F32)
            if dil == 1:
                a_ref[0, :, part * DIL_OUT:(part + 1) * DIL_OUT] = res.astype(a_ref.dtype)
                continue
            for half in range(DIL_OUT // LANES):
                stage = scratch[part * (DIL_OUT // LANES) + half]
                stage[...] = res[:, half * LANES:(half + 1) * LANES]
                c0 = part * DIL_OUT + half * LANES
                for r in range(dil):
                    a_ref[r, :, c0:c0 + LANES] = stage[pl.ds(r, per, stride=dil), :].astype(a_ref.dtype)


def _in_proj(x2, w_in_bf, bsz, seq, tm=512):
    n, d = x2.shape
    cols = w_in_bf.shape[1]
    tiles = seq // tm
    row = lambda i: (i, 0)
    dil_spec = lambda dil: pl.BlockSpec((None, dil, tm // dil, 3 * DIL_OUT), lambda i: (i // tiles, 0, i % tiles, 0))
    dil_shape = lambda dil: jax.ShapeDtypeStruct((bsz, dil, seq // dil, 3 * DIL_OUT), BF16)
    dils = [dil for _, dil in DIL_GROUPS]
    return pl.pallas_call(
        functools.partial(_in_proj_kernel, d_model=d, tm=tm),
        grid=(n // tm,),
        in_specs=[pl.BlockSpec((tm, d), row),
                  pl.BlockSpec((d, cols), lambda i: (0, 0), pipeline_mode=pl.Buffered(1))],
        out_specs=[dil_spec(dil) for dil in dils]
        + [pl.BlockSpec((tm, SWA_Q_WIDTH), row), pl.BlockSpec((tm, 2 * SWA_KV_WIDTH), row),
           pl.BlockSpec((tm, 2 * d), row)],
        out_shape=[dil_shape(dil) for dil in dils]
        + [jax.ShapeDtypeStruct((n, SWA_Q_WIDTH), BF16), jax.ShapeDtypeStruct((n, 2 * SWA_KV_WIDTH), BF16),
           jax.ShapeDtypeStruct((n, 2 * d), BF16)],
        scratch_shapes=[pltpu.VMEM((tm, LANES), F32)] * (3 * DIL_OUT // LANES),
        compiler_params=_cparams("parallel"),
        name="in_proj",
    )(x2, w_in_bf)


_NT = (((1,), (1,)), ((), ()))


def _per_head_column(ref, head0, n_heads, scale=1.0):
    head_of_row = lax.broadcasted_iota(I32, (n_heads * ATTN_BLOCK, 1), 0) // ATTN_BLOCK
    col = jnp.zeros((n_heads * ATTN_BLOCK, 1), F32)
    for h in range(n_heads):
        col = jnp.where(head_of_row == h, ref[head0 + h] * scale, col)
    return col


def _store_band_bias(bias_ref, slope_col, max_diff):
    rows = slope_col.shape[0]
    qi = lax.broadcasted_iota(I32, (rows, ATTN_BLOCK), 0) % ATTN_BLOCK
    kj = lax.broadcasted_iota(I32, (rows, ATTN_BLOCK), 1)
    diff_prev = qi - kj + ATTN_BLOCK
    diff_cur = qi - kj
    bias_ref[0] = jnp.where(diff_prev <= max_diff, -slope_col * diff_prev.astype(F32), NEG_INF)
    bias_ref[1] = jnp.where((diff_cur >= 0) & (diff_cur <= max_diff), -slope_col * diff_cur.astype(F32), NEG_INF)


def _band_softmax(s_prev, s_cur, sink_col):
    m = jnp.max(jnp.maximum(s_prev, s_cur), axis=-1, keepdims=True)
    if sink_col is not None:
        m = jnp.maximum(m, sink_col)
    p_prev = jnp.exp(s_prev - m)
    p_cur = jnp.exp(s_cur - m)
    denom = jnp.sum(p_prev + p_cur, axis=-1, keepdims=True)
    if sink_col is not None:
        denom = denom + jnp.exp(sink_col - m)
    return p_prev.astype(BF16), p_cur.astype(BF16), m, denom


def _dil_attn_kernel(slope_ref, q_ref, kc_ref, kp_ref, vc_ref, vp_ref, o_ref, lse_ref, bias_ref,
                     *, tq, max_diff, dist_scale):
    first = pl.program_id(2) == 0
    _store_band_bias(bias_ref, _per_head_column(slope_ref, 0, DIL_HEADS, dist_scale), max_diff)
    head_cols = [slice(h * HEAD_DIM, (h + 1) * HEAD_DIM) for h in range(DIL_HEADS)]
    for i in range(tq // ATTN_BLOCK):
        rows = slice(i * ATTN_BLOCK, (i + 1) * ATTN_BLOCK)
        prev_rows = slice((i - 1) * ATTN_BLOCK, i * ATTN_BLOCK)
        k_prev = [kp_ref[:, c] if i == 0 else kc_ref[prev_rows, c] for c in head_cols]
        v_prev = [vp_ref[:, c] if i == 0 else vc_ref[prev_rows, c] for c in head_cols]
        q = [q_ref[rows, c] * 0.125 for c in head_cols]
        s_prev = jnp.concatenate([lax.dot_general(q[h], k_prev[h], _NT, preferred_element_type=F32)
                                  for h in range(DIL_HEADS)], axis=0) + bias_ref[0]
        s_cur = jnp.concatenate([lax.dot_general(q[h], kc_ref[rows, head_cols[h]], _NT, preferred_element_type=F32)
                                 for h in range(DIL_HEADS)], axis=0) + bias_ref[1]
        if i == 0:
            s_prev = jnp.where(first, NEG_INF, s_prev)
        p_prev, p_cur, m, denom = _band_softmax(s_prev, s_cur, None)
        lse = m + jnp.log(denom)
        for h in range(DIL_HEADS):
            hr = slice(h * ATTN_BLOCK, (h + 1) * ATTN_BLOCK)
            o = (jnp.dot(p_prev[hr], v_prev[h], preferred_element_type=F32)
                 + jnp.dot(p_cur[hr], vc_ref[rows, head_cols[h]], preferred_element_type=F32)) / denom[hr]
            o_ref[rows, head_cols[h]] = o.astype(o_ref.dtype)
            lse_ref[rows, head_cols[h]] = jnp.broadcast_to(lse[hr], (ATTN_BLOCK, HEAD_DIM))


def _dil_attention(a_g, slopes_g, g):
    window, dil = DIL_GROUPS[g]
    bsz, _, sub_len, _ = a_g.shape
    tq = min(512, sub_len)
    nqb = tq // ATTN_BLOCK
    cur = lambda part: (lambda b, r, m: (b, r, m, part))
    prev = lambda part: (lambda b, r, m: (b, r, jnp.maximum(m * nqb - 1, 0), part))
    blk = lambda rows, imap: pl.BlockSpec((None, None, rows, DIL_OUT), imap)
    return pl.pallas_call(
        functools.partial(_dil_attn_kernel, tq=tq, max_diff=window // dil, dist_scale=float(dil)),
        grid=(bsz, dil, sub_len // tq),
        in_specs=[pl.BlockSpec(memory_space=pltpu.SMEM),
                  blk(tq, cur(0)), blk(tq, cur(1)), blk(ATTN_BLOCK, prev(1)),
                  blk(tq, cur(2)), blk(ATTN_BLOCK, prev(2))],
        out_specs=[blk(tq, cur(0)), blk(tq, cur(0))],
        out_shape=[jax.ShapeDtypeStruct((bsz, dil, sub_len, DIL_OUT), BF16),
                   jax.ShapeDtypeStruct((bsz, dil, sub_len, DIL_OUT), F32)],
        scratch_shapes=[pltpu.VMEM((2, DIL_HEADS * ATTN_BLOCK, ATTN_BLOCK), F32)],
        compiler_params=_cparams("parallel", "parallel", "arbitrary"),
        name=f"dil_attn_g{g}",
    )(slopes_g, a_g, a_g, a_g, a_g, a_g)


def _swa_attn_kernel(slope_ref, sink_ref, q_ref, kvc_ref, kvp_ref, o_ref, bias_ref, *, tq):
    first = pl.program_id(1) == 0
    for kvh in range(SWA_KV_HEADS):
        head0 = kvh * SWA_REP
        _store_band_bias(bias_ref, _per_head_column(slope_ref, head0, SWA_REP), SWA_WINDOW - 1)
        sink_col = _per_head_column(sink_ref, head0, SWA_REP)
        kcols = slice(kvh * HEAD_DIM, (kvh + 1) * HEAD_DIM)
        vcols = slice(SWA_KV_WIDTH + kvh * HEAD_DIM, SWA_KV_WIDTH + (kvh + 1) * HEAD_DIM)
        head_cols = [slice((head0 + r) * HEAD_DIM, (head0 + r + 1) * HEAD_DIM) for r in range(SWA_REP)]
        for i in range(tq // ATTN_BLOCK):
            rows = slice(i * ATTN_BLOCK, (i + 1) * ATTN_BLOCK)
            prev_rows = slice((i - 1) * ATTN_BLOCK, i * ATTN_BLOCK)
            k_prev = kvp_ref[:, kcols] if i == 0 else kvc_ref[prev_rows, kcols]
            v_prev = kvp_ref[:, vcols] if i == 0 else kvc_ref[prev_rows, vcols]
            q = jnp.concatenate([q_ref[rows, c] for c in head_cols], axis=0) * 0.125
            s_prev = lax.dot_general(q, k_prev, _NT, preferred_element_type=F32) + bias_ref[0]
            s_cur = lax.dot_general(q, kvc_ref[rows, kcols], _NT, preferred_element_type=F32) + bias_ref[1]
            if i == 0:
                s_prev = jnp.where(first, NEG_INF, s_prev)
            p_prev, p_cur, _, denom = _band_softmax(s_prev, s_cur, sink_col)
            o = (jnp.dot(p_prev, v_prev, preferred_element_type=F32)
                 + jnp.dot(p_cur, kvc_ref[rows, vcols], preferred_element_type=F32)) / denom
            for r in range(SWA_REP):
                o_ref[rows, head_cols[r]] = o[r * ATTN_BLOCK:(r + 1) * ATTN_BLOCK].astype(o_ref.dtype)


def _swa_attention(hq, hkv, slopes_b, sinks, bsz, seq, tq=512):
    nqb = tq // ATTN_BLOCK
    hq3 = hq.reshape(bsz, seq, SWA_Q_WIDTH)
    hkv3 = hkv.reshape(bsz, seq, 2 * SWA_KV_WIDTH)
    smem = pl.BlockSpec(memory_space=pltpu.SMEM)
    out = pl.pallas_call(
        functools.partial(_swa_attn_kernel, tq=tq),
        grid=(bsz, seq // tq),
        in_specs=[smem, smem,
                  pl.BlockSpec((None, tq, SWA_Q_WIDTH), lambda b, m: (b, m, 0)),
                  pl.BlockSpec((None, tq, 2 * SWA_KV_WIDTH), lambda b, m: (b, m, 0)),
                  pl.BlockSpec((None, ATTN_BLOCK, 2 * SWA_KV_WIDTH),
                               lambda b, m: (b, jnp.maximum(m * nqb - 1, 0), 0))],
        out_specs=pl.BlockSpec((None, tq, SWA_Q_WIDTH), lambda b, m: (b, m, 0)),
        out_shape=jax.ShapeDtypeStruct((bsz, seq, SWA_Q_WIDTH), BF16),
        scratch_shapes=[pltpu.VMEM((2, SWA_REP * ATTN_BLOCK, ATTN_BLOCK), F32)],
        compiler_params=_cparams("parallel", "arbitrary"),
        name="swa_attn",
    )(slopes_b, sinks, hq3, hkv3, hkv3)
    return out.reshape(bsz * seq, SWA_Q_WIDTH)


def _sigmoid(x):
    return 0.5 * (jnp.tanh(0.5 * x) + 1.0)


def _layer_norm(z, g, b):
    mu = jnp.mean(z, axis=-1, keepdims=True)
    zc = z - mu
    var = jnp.mean(zc * zc, axis=-1, keepdims=True)
    return zc * lax.rsqrt(var + LN_EPS) * g + b


def _mix_out_kernel(x_ref, o0_ref, o1_ref, o2_ref, l0_ref, l1_ref, l2_ref, ob_ref, g_ref,
                    wpa_ref, wpb_ref, wo_ref, lng_ref, lnb_ref, rw_ref, rb_ref,
                    y_ref, ypk_ref, idx_ref, gate_ref, *scratch, d_model):
    def natural(ref, stages):
        dil, per, _ = ref.shape
        if dil == 1:
            return ref[0].astype(F32)
        for half, stage in enumerate(stages):
            for r in range(dil):
                stage[pl.ds(r, per, stride=dil), :] = ref[r, :, half * LANES:(half + 1) * LANES].astype(F32)
        return jnp.concatenate([stage[...] for stage in stages], axis=1)

    o0, o1, o2 = natural(o0_ref, None), natural(o1_ref, scratch[0:2]), natural(o2_ref, scratch[2:4])
    l0, l1, l2 = natural(l0_ref, None), natural(l1_ref, scratch[4:6]), natural(l2_ref, scratch[6:8])
    lm = jnp.maximum(jnp.maximum(l0, l1), l2)
    e0, e1, e2 = jnp.exp(l0 - lm), jnp.exp(l1 - lm), jnp.exp(l2 - lm)
    esum = e0 + e1 + e2
    out_a = (e0 / esum) * o0 + (e1 / esum) * o1 + (e2 / esum) * o2
    pa = jnp.dot(out_a.astype(BF16), wpa_ref[...], preferred_element_type=F32)
    pb = jnp.dot(ob_ref[...], wpb_ref[...], preferred_element_type=F32)
    ga = _sigmoid(g_ref[:, :d_model].astype(F32))
    gb = _sigmoid(g_ref[:, d_model:].astype(F32))
    merged = ga * pa + gb * pb
    mix = jnp.dot(merged.astype(BF16), wo_ref[...], preferred_element_type=F32)
    y = _layer_norm(DEEPNORM_ALPHA * x_ref[...] + mix, lng_ref[...], lnb_ref[...])
    y_ref[...] = y
    ypk_ref[...] = _pack_bf16_pairs(y)

    logits = jnp.dot(y.astype(BF16), rw_ref[...], preferred_element_type=F32) + rb_ref[...]
    lane = lax.broadcasted_iota(I32, logits.shape, 1)
    logits = jnp.where(lane < N_EXPERTS, logits, -jnp.inf)
    idx_out = jnp.zeros(logits.shape, I32)
    val_out = jnp.full(logits.shape, -jnp.inf, F32)
    for k in range(TOP_K):
        top = jnp.max(logits, axis=-1, keepdims=True)
        top_idx = jnp.min(jnp.where(logits == top, lane, LANES), axis=-1, keepdims=True)
        idx_out = jnp.where(lane == k, top_idx, idx_out)
        val_out = jnp.where(lane == k, top, val_out)
        logits = jnp.where(lane == top_idx, -jnp.inf, logits)
    ev = jnp.exp(val_out - jnp.max(val_out, axis=-1, keepdims=True))
    gates = ev / jnp.sum(ev, axis=-1, keepdims=True)
    idx_ref[...] = idx_out[:, :TOP_K]
    gate_ref[...] = gates[:, :TOP_K]


def _mix_out(x2, o_g, lse_g, out_b, gates_h, wpa, wpb, wo, ln_g, ln_b, rw_pad, rb_pad, seq, tm=512):
    n, d = x2.shape
    tiles = seq // tm
    row = lambda i: (i, 0)
    const = lambda i: (0, 0)
    rb = lambda w: pl.BlockSpec((tm, w), row)
    full = lambda a: pl.BlockSpec(a.shape, const)
    dil_specs = [pl.BlockSpec((None, dil, tm // dil, DIL_OUT), lambda i: (i // tiles, 0, i % tiles, 0))
                 for _, dil in DIL_GROUPS]
    return pl.pallas_call(
        functools.partial(_mix_out_kernel, d_model=d),
        grid=(n // tm,),
        in_specs=[rb(d)] + dil_specs + dil_specs + [rb(SWA_Q_WIDTH), rb(2 * d),
                  full(wpa), full(wpb), full(wo), full(ln_g), full(ln_b), full(rw_pad), full(rb_pad)],
        out_specs=[rb(d), rb(d // 2), rb(TOP_K), rb(TOP_K)],
        out_shape=[jax.ShapeDtypeStruct((n, d), F32), jax.ShapeDtypeStruct((n, d // 2), jnp.uint32),
                   jax.ShapeDtypeStruct((n, TOP_K), I32), jax.ShapeDtypeStruct((n, TOP_K), F32)],
        scratch_shapes=[pltpu.VMEM((tm, LANES), F32)] * (4 * DIL_OUT // LANES),
        compiler_params=_cparams("parallel"),
        name="mix_out",
    )(x2, *o_g, *lse_g, out_b, gates_h, wpa, wpb, wo, ln_g, ln_b, rw_pad, rb_pad)


def _token_mixer_ln1(x2, w_in, sinks, w_proj_a, w_proj_b, w_out, ln1_g, ln1_b, router_w, router_b, bsz, seq):
    n, d = x2.shape
    heads = jnp.arange(1, N_ALIBI_HEADS + 1, dtype=F32)
    slopes = jnp.exp2(-8.0 * heads / N_ALIBI_HEADS)
    *a_g, hq, hkv, gates_h = _in_proj(x2, w_in.astype(BF16), bsz, seq)
    o_g, lse_g = [], []
    for g in range(N_DIL):
        sl = slopes[SWA_Q_HEADS + g * DIL_HEADS:SWA_Q_HEADS + (g + 1) * DIL_HEADS]
        o, lse = _dil_attention(a_g[g], sl, g)
        o_g.append(o)
        lse_g.append(lse)
    out_b = _swa_attention(hq, hkv, slopes[:SWA_Q_HEADS], sinks.astype(F32), bsz, seq)
    rw_pad = jnp.zeros((d, LANES), BF16).at[:, :N_EXPERTS].set(router_w.astype(BF16))
    rb_pad = jnp.zeros((1, LANES), F32).at[0, :N_EXPERTS].set(router_b.astype(F32))
    return _mix_out(x2, o_g, lse_g, out_b, gates_h, w_proj_a.astype(BF16), w_proj_b.astype(BF16),
                    w_out.astype(BF16), ln1_g.reshape(1, d).astype(F32), ln1_b.reshape(1, d).astype(F32),
                    rw_pad, rb_pad, seq)


def _route_kernel(idx_ref, dest_ref, cnt_ref, carry_ref, pstart_ref, *, expert_block):
    phase, i = pl.program_id(0), pl.program_id(1)
    idx = idx_ref[...]
    t = idx.shape[0]
    lane = lax.broadcasted_iota(I32, (t, LANES), 1)
    onehot = [lane == idx[:, k:k + 1] for k in range(TOP_K)]
    multi = sum(oh.astype(F32) for oh in onehot)
    tile_cnt = jnp.sum(multi, axis=0, keepdims=True).astype(I32)

    @pl.when((phase == 0) & (i == 0))
    def _():
        carry_ref[...] = jnp.zeros_like(carry_ref)

    @pl.when((phase == 1) & (i == 0))
    def _():
        counts = carry_ref[...]
        padded = (counts + (expert_block - 1)) & (-expert_block)
        lane8 = lax.broadcasted_iota(I32, counts.shape, 1)
        incl = padded
        for sh in (1, 2, 4, 8, 16):
            incl = incl + jnp.where(lane8 >= sh, pltpu.roll(incl, sh, axis=1), 0)
        pstart_ref[...] = incl - padded
        cnt_ref[...] = counts
        carry_ref[...] = jnp.zeros_like(carry_ref)

    @pl.when(phase == 1)
    def _():
        r = lax.broadcasted_iota(I32, (t, t), 0)
        c = lax.broadcasted_iota(I32, (t, t), 1)
        earlier = (c < r).astype(BF16)
        cum = jnp.dot(earlier, multi.astype(BF16), preferred_element_type=F32).astype(I32)
        base = cum + carry_ref[0:1, :] + pstart_ref[0:1, :]
        dest = jnp.zeros((t, LANES), I32)
        for k in range(TOP_K):
            d_k = jnp.sum(jnp.where(onehot[k], base, 0), axis=-1, keepdims=True)
            dest = jnp.where(lane == k, d_k, dest)
        dest_ref[...] = dest[:, :TOP_K]

    carry_ref[...] = carry_ref[...] + tile_cnt


def _route(idx, expert_block, tile=512):
    n = idx.shape[0]
    tile = min(tile, n)
    return pl.pallas_call(
        functools.partial(_route_kernel, expert_block=expert_block),
        grid=(2, n // tile),
        in_specs=[pl.BlockSpec((tile, TOP_K), lambda p, i: (i, 0))],
        out_specs=[pl.BlockSpec((tile, TOP_K), lambda p, i: (i * p, 0)),
                   pl.BlockSpec((8, LANES), lambda p, i: (0, 0))],
        out_shape=[jax.ShapeDtypeStruct((n, TOP_K), I32), jax.ShapeDtypeStruct((8, LANES), I32)],
        scratch_shapes=[pltpu.VMEM((8, LANES), I32), pltpu.VMEM((8, LANES), I32)],
        compiler_params=_cparams("arbitrary", "arbitrary"),
        name="route",
    )(idx)


def _sc_dispatch(dest_t, y, rows, window=64):
    n, d = y.shape
    info = plsc.get_sparse_core_info()
    n_cores = info.num_cores
    per_worker = n // (n_cores * info.num_subcores)
    mesh = plsc.VectorSubcoreMesh(core_axis_name="c", subcore_axis_name="s")

    @functools.partial(pl.kernel, mesh=mesh, out_type=jax.ShapeDtypeStruct((rows, d), y.dtype),
                       scratch_types=[pltpu.VMEM((TOP_K, window), I32), pltpu.VMEM((window, d), y.dtype)],
                       name="sc_dispatch")
    def body(dest_hbm, y_hbm, xs_hbm, idx_v, rows_v):
        base = (lax.axis_index("s") * n_cores + lax.axis_index("c")) * per_worker

        @pl.loop(0, per_worker // window)
        def _(c):
            t0 = base + c * window
            for k in range(TOP_K):
                pltpu.sync_copy(dest_hbm.at[k, pl.ds(t0, window)], idx_v.at[k])
            pltpu.sync_copy(y_hbm.at[pl.ds(t0, window)], rows_v)
            for k in range(TOP_K):
                pltpu.sync_copy(rows_v, xs_hbm.at[idx_v.at[k]])

    return body(dest_t, y)


def _expert_kernel(be_ref, nu_ref, nv_ref, x_ref, wg_ref, bg_ref, wu_ref, bu_ref, wd_ref, bd_ref, o_ref):
    del be_ref
    j = pl.program_id(0)

    @pl.when(j < nu_ref[0])
    def _():
        row = lax.broadcasted_iota(I32, x_ref.shape, 0)
        x_lo, x_hi = _unpack_bf16_pairs(jnp.where(row < nv_ref[j], x_ref[...], jnp.uint32(0)))
        xb = jnp.concatenate([x_lo.astype(BF16), x_hi.astype(BF16)], axis=1)
        gt = jnp.dot(xb, wg_ref[...], preferred_element_type=F32) + bg_ref[...]
        up = jnp.dot(xb, wu_ref[...], preferred_element_type=F32) + bu_ref[...]
        gt = jnp.minimum(gt, SWIGLU_LIMIT)
        up = jnp.clip(up, -SWIGLU_LIMIT, SWIGLU_LIMIT)
        hdn = gt * _sigmoid(SWIGLU_ALPHA * gt) * (up + 1.0)
        o_ref[...] = _pack_bf16_pairs(jnp.dot(hdn.astype(BF16), wd_ref[...], preferred_element_type=F32)
                                      + bd_ref[...])

    @pl.when(j >= nu_ref[0])
    def _():
        o_ref[...] = jnp.zeros_like(o_ref)


def _experts(block_e, n_used, n_valid, xs, wg, bg, wu, bu, wd, bd, expert_block):
    rows, _ = xs.shape
    e, d, f = wg.shape
    xmap = lambda j, be, nu, nv: (jnp.minimum(j, nu[0] - 1), 0)
    wmap = lambda j, be, nu, nv: (be[j], 0, 0)
    grid_spec = pltpu.PrefetchScalarGridSpec(
        num_scalar_prefetch=3,
        grid=(rows // expert_block,),
        in_specs=[pl.BlockSpec((expert_block, d // 2), xmap),
                  pl.BlockSpec((None, d, f), wmap), pl.BlockSpec((None, 1, f), wmap),
                  pl.BlockSpec((None, d, f), wmap), pl.BlockSpec((None, 1, f), wmap),
                  pl.BlockSpec((None, f, d), wmap), pl.BlockSpec((None, 1, d), wmap)],
        out_specs=pl.BlockSpec((expert_block, d // 2), lambda j, be, nu, nv: (j, 0)),
    )
    return pl.pallas_call(
        _expert_kernel,
        grid_spec=grid_spec,
        out_shape=jax.ShapeDtypeStruct((rows, d // 2), jnp.uint32),
        compiler_params=_cparams("arbitrary"),
        name="experts",
    )(block_e, n_used, n_valid, xs, wg, bg.reshape(e, 1, f), wu, bu.reshape(e, 1, f), wd, bd.reshape(e, 1, d))


def _sc_gather(dest_t, ys, n, window=64):
    _, d = ys.shape
    info = plsc.get_sparse_core_info()
    n_cores = info.num_cores
    per_worker = n // (n_cores * info.num_subcores)
    mesh = plsc.VectorSubcoreMesh(core_axis_name="c", subcore_axis_name="s")

    n_chunks = per_worker // window
    n_buf = 2

    @functools.partial(pl.kernel, mesh=mesh, out_type=jax.ShapeDtypeStruct((TOP_K, n, d), ys.dtype),
                       scratch_types=[pltpu.VMEM((TOP_K, per_worker), I32)]
                       + [pltpu.VMEM((window, d), ys.dtype)] * n_buf + [pltpu.SemaphoreType.DMA] * (2 * n_buf),
                       name="sc_gather")
    def body(dest_hbm, ys_hbm, out_hbm, idx_v, buf0, buf1, gsem0, gsem1, ssem0, ssem1):
        bufs, gsem, ssem = (buf0, buf1), (gsem0, gsem1), (ssem0, ssem1)
        base = (lax.axis_index("s") * n_cores + lax.axis_index("c")) * per_worker
        for k in range(TOP_K):
            pltpu.sync_copy(dest_hbm.at[k, pl.ds(base, per_worker)], idx_v.at[k])

        def gather(c, k):
            return pltpu.make_async_copy(ys_hbm.at[idx_v.at[k, pl.ds(c * window, window)]], bufs[k % n_buf],
                                         gsem[k % n_buf])

        def store(c, k):
            return pltpu.make_async_copy(bufs[k % n_buf], out_hbm.at[k, pl.ds(base + c * window, window)],
                                         ssem[k % n_buf])

        gather(0, 0).start()

        @pl.loop(0, n_chunks)
        def _(c):
            for k in range(TOP_K):
                if k > 0:
                    store(c, k - 1).wait()
                    if k + 1 < TOP_K:
                        gather(c, k + 1).start()
                    else:
                        @pl.when(c + 1 < n_chunks)
                        def _():
                            gather(c + 1, 0).start()
                else:
                    @pl.when(c > 0)
                    def _():
                        store(c - 1, TOP_K - 1).wait()
                    gather(c, 1).start()
                gather(c, k).wait()
                store(c, k).start()

        store(n_chunks - 1, TOP_K - 1).wait()

    return body(dest_t, ys)


def _combine_dense_kernel(gate_ref, y_ref, lng_ref, lnb_ref, g_ref, o_ref):
    gates = gate_ref[...]
    ffn_lo = ffn_hi = None
    for k in range(TOP_K):
        lo, hi = _unpack_bf16_pairs(g_ref[k])
        gate = gates[:, k:k + 1]
        ffn_lo = gate * lo if k == 0 else ffn_lo + gate * lo
        ffn_hi = gate * hi if k == 0 else ffn_hi + gate * hi
    ffn = jnp.concatenate([ffn_lo, ffn_hi], axis=1)
    o_ref[...] = _layer_norm(DEEPNORM_ALPHA * y_ref[...] + ffn, lng_ref[...], lnb_ref[...])


def _combine_dense(gates, y, ln_g, ln_b, gathered, tile=512):
    n, d = y.shape
    tile = min(tile, n)
    row = lambda i: (i, 0)
    const = lambda i: (0, 0)
    return pl.pallas_call(
        _combine_dense_kernel,
        grid=(n // tile,),
        in_specs=[pl.BlockSpec((tile, TOP_K), row), pl.BlockSpec((tile, d), row),
                  pl.BlockSpec((1, d), const), pl.BlockSpec((1, d), const),
                  pl.BlockSpec((TOP_K, tile, d // 2), lambda i: (0, i, 0))],
        out_specs=pl.BlockSpec((tile, d), row),
        out_shape=jax.ShapeDtypeStruct((n, d), F32),
        compiler_params=_cparams("parallel"),
        name="combine_dense",
    )(gates, y, ln_g, ln_b, gathered)


MOE_ROW_BLOCK = 512


def _moe_ln2(y, y_packed, idx, gates, w_gate, b_gate, w_up, b_up, w_down, b_down, ln2_g, ln2_b):
    n, d = y.shape
    n_blocks = n * TOP_K // MOE_ROW_BLOCK + N_EXPERTS
    rows = n_blocks * MOE_ROW_BLOCK
    dest, counts = _route(idx, MOE_ROW_BLOCK)
    cnt = counts[0, :N_EXPERTS]
    padded = (cnt + MOE_ROW_BLOCK - 1) // MOE_ROW_BLOCK * MOE_ROW_BLOCK
    p_end = jnp.cumsum(padded)
    block_row0 = jnp.arange(n_blocks, dtype=I32) * MOE_ROW_BLOCK
    block_e = jnp.minimum(jnp.sum(p_end[None, :] <= block_row0[:, None], axis=1), N_EXPERTS - 1).astype(I32)
    n_used = (p_end[-1:] // MOE_ROW_BLOCK).astype(I32)
    n_valid = jnp.clip(cnt[block_e] - (block_row0 - (p_end - padded)[block_e]), 0, MOE_ROW_BLOCK).astype(I32)
    dest_t = dest.T
    xs = _sc_dispatch(dest_t, y_packed, rows)
    ys = _experts(block_e, n_used, n_valid, xs, w_gate.astype(BF16), b_gate.astype(F32), w_up.astype(BF16),
                  b_up.astype(F32), w_down.astype(BF16), b_down.astype(F32), MOE_ROW_BLOCK)
    gathered = _sc_gather(dest_t, ys, n)
    return _combine_dense(gates, y, ln2_g.reshape(1, d).astype(F32), ln2_b.reshape(1, d).astype(F32), gathered)


def kernel(x, w_in, sinks, w_proj_a, w_proj_b, w_out, ln1_g, ln1_b, router_w, router_b,
           w_gate, b_gate, w_up, b_up, w_down, b_down, ln2_g, ln2_b):
    bsz, seq, d = x.shape
    h = x.reshape(bsz * seq, d)
    for l in range(w_in.shape[0]):
        y, y_packed, idx, gates = _token_mixer_ln1(h, w_in[l], sinks[l], w_proj_a[l], w_proj_b[l], w_out[l],
                                                   ln1_g[l], ln1_b[l], router_w[l], router_b[l], bsz, seq)
        h = _moe_ln2(y, y_packed, idx, gates, w_gate[l], b_gate[l], w_up[l], b_up[l], w_down[l], b_down[l],
                     ln2_g[l], ln2_b[l])
    return h.reshape(bsz, seq, d)
```

```python
import functools

import jax
import jax.numpy as jnp
from jax import lax
from jax.experimental import pallas as pl
from jax.experimental.pallas import tpu as pltpu
from jax.experimental.pallas import tpu_sc as plsc

F32 = jnp.float32
BF16 = jnp.bfloat16
I32 = jnp.int32

HEAD_DIM = 64
DIL_GROUPS = ((128, 1), (512, 4), (2048, 16))
DIL_HEADS = 4
N_DIL = len(DIL_GROUPS)
DIL_OUT = DIL_HEADS * HEAD_DIM
DIL_WIDTH = N_DIL * DIL_OUT
SWA_Q_HEADS = 16
SWA_KV_HEADS = 2
SWA_REP = SWA_Q_HEADS // SWA_KV_HEADS
SWA_WINDOW = 128
SWA_Q_WIDTH = SWA_Q_HEADS * HEAD_DIM
SWA_KV_WIDTH = SWA_KV_HEADS * HEAD_DIM
N_ALIBI_HEADS = SWA_Q_HEADS + N_DIL * DIL_HEADS
ATTN_BLOCK = 128
N_EXPERTS = 32
TOP_K = 4
SWIGLU_LIMIT = 7.0
SWIGLU_ALPHA = 1.702
LN_EPS = 1e-5
DEPTH = 1
DEEPNORM_ALPHA = (2 * DEPTH) ** 0.25
NEG_INF = -1e30

LANES = 128
VMEM_LIMIT_BYTES = 56 * 1024 * 1024

A_QKV_W = 3 * DIL_WIDTH
B_Q_OFF = A_QKV_W
B_KV_OFF = B_Q_OFF + SWA_Q_WIDTH
GATE_OFF = B_KV_OFF + 2 * SWA_KV_WIDTH


def _cparams(*sem):
    return pltpu.CompilerParams(dimension_semantics=sem, vmem_limit_bytes=VMEM_LIMIT_BYTES)


def _pack_bf16_pairs(x):
    c = x.shape[1] // 2
    lo = lax.bitcast_convert_type(x[:, :c].astype(BF16).astype(F32), jnp.uint32)
    hi = lax.bitcast_convert_type(x[:, c:].astype(BF16).astype(F32), jnp.uint32)
    return (lo >> 16) | hi


def _unpack_bf16_pairs(w):
    lo = lax.bitcast_convert_type(w << 16, F32)
    hi = lax.bitcast_convert_type(w & jnp.uint32(0xFFFF0000), F32)
    return lo, hi


def _in_proj_kernel(x_ref, w_ref, a0_ref, a1_ref, a2_ref, hq_ref, hkv_ref, g_ref, *scratch, d_model, tm):
    xb = x_ref[...].astype(BF16)
    segments = ((hq_ref, B_Q_OFF, SWA_Q_WIDTH), (hkv_ref, B_KV_OFF, 2 * SWA_KV_WIDTH),
                (g_ref, GATE_OFF, 2 * d_model))
    for out_ref, col0, width in segments:
        for c in range(0, width, 512):
            w = min(512, width - c)
            r = jnp.dot(xb, w_ref[:, col0 + c:col0 + c + w], preferred_element_type=F32)
            out_ref[:, c:c + w] = r.astype(out_ref.dtype)
    for g, a_ref in enumerate((a0_ref, a1_ref, a2_ref)):
        dil = DIL_GROUPS[g][1]
        per = tm // dil
        for part in range(3):
            col0 = part * DIL_WIDTH + g * DIL_OUT
            res = jnp.dot(xb, w_ref[:, col0:col0 + DIL_OUT], preferred_element_type=F32)
            if dil == 1:
                a_ref[0, :, part * DIL_OUT:(part + 1) * DIL_OUT] = res.astype(a_ref.dtype)
                continue
            for half in range(DIL_OUT // LANES):
                stage = scratch[part * (DIL_OUT // LANES) + half]
                stage[...] = res[:, half * LANES:(half + 1) * LANES]
                c0 = part * DIL_OUT + half * LANES
                for r in range(dil):
                    a_ref[r, :, c0:c0 + LANES] = stage[pl.ds(r, per, stride=dil), :].astype(a_ref.dtype)


def _in_proj(x2, w_in_bf, bsz, seq, tm=512):
    n, d = x2.shape
    cols = w_in_bf.shape[1]
    tiles = seq // tm
    row = lambda i: (i, 0)
    dil_spec = lambda dil: pl.BlockSpec((None, dil, tm // dil, 3 * DIL_OUT), lambda i: (i // tiles, 0, i % tiles, 0))
    dil_shape = lambda dil: jax.ShapeDtypeStruct((bsz, dil, seq // dil, 3 * DIL_OUT), BF16)
    dils = [dil for _, dil in DIL_GROUPS]
    return pl.pallas_call(
        functools.partial(_in_proj_kernel, d_model=d, tm=tm),
        grid=(n // tm,),
        in_specs=[pl.BlockSpec((tm, d), row),
                  pl.BlockSpec((d, cols), lambda i: (0, 0), pipeline_mode=pl.Buffered(1))],
        out_specs=[dil_spec(dil) for dil in dils]
        + [pl.BlockSpec((tm, SWA_Q_WIDTH), row), pl.BlockSpec((tm, 2 * SWA_KV_WIDTH), row),
           pl.BlockSpec((tm, 2 * d), row)],
        out_shape=[dil_shape(dil) for dil in dils]
        + [jax.ShapeDtypeStruct((n, SWA_Q_WIDTH), BF16), jax.ShapeDtypeStruct((n, 2 * SWA_KV_WIDTH), BF16),
           jax.ShapeDtypeStruct((n, 2 * d), BF16)],
        scratch_shapes=[pltpu.VMEM((tm, LANES), F32)] * (3 * DIL_OUT // LANES),
        compiler_params=_cparams("parallel"),
        name="in_proj",
    )(x2, w_in_bf)


_NT = (((1,), (1,)), ((), ()))


def _per_head_column(ref, head0, n_heads, scale=1.0):
    head_of_row = lax.broadcasted_iota(I32, (n_heads * ATTN_BLOCK, 1), 0) // ATTN_BLOCK
    col = jnp.zeros((n_heads * ATTN_BLOCK, 1), F32)
    for h in range(n_heads):
        col = jnp.where(head_of_row == h, ref[head0 + h] * scale, col)
    return col


def _store_band_bias(bias_ref, slope_col, max_diff):
    rows = slope_col.shape[0]
    qi = lax.broadcasted_iota(I32, (rows, ATTN_BLOCK), 0) % ATTN_BLOCK
    kj = lax.broadcasted_iota(I32, (rows, ATTN_BLOCK), 1)
    diff_prev = qi - kj + ATTN_BLOCK
    diff_cur = qi - kj
    bias_ref[0] = jnp.where(diff_prev <= max_diff, -slope_col * diff_prev.astype(F32), NEG_INF)
    bias_ref[1] = jnp.where((diff_cur >= 0) & (diff_cur <= max_diff), -slope_col * diff_cur.astype(F32), NEG_INF)


def _band_softmax(s_prev, s_cur, sink_col):
    m = jnp.max(jnp.maximum(s_prev, s_cur), axis=-1, keepdims=True)
    if sink_col is not None:
        m = jnp.maximum(m, sink_col)
    p_prev = jnp.exp(s_prev - m)
    p_cur = jnp.exp(s_cur - m)
    denom = jnp.sum(p_prev + p_cur, axis=-1, keepdims=True)
    if sink_col is not None:
        denom = denom + jnp.exp(sink_col - m)
    return p_prev.astype(BF16), p_cur.astype(BF16), m, denom


def _dil_attn_kernel(slope_ref, q_ref, kc_ref, kp_ref, vc_ref, vp_ref, o_ref, lse_ref, bias_ref,
                     *, tq, max_diff, dist_scale):
    first = pl.program_id(2) == 0
    _store_band_bias(bias_ref, _per_head_column(slope_ref, 0, DIL_HEADS, dist_scale), max_diff)
    head_cols = [slice(h * HEAD_DIM, (h + 1) * HEAD_DIM) for h in range(DIL_HEADS)]
    for i in range(tq // ATTN_BLOCK):
        rows = slice(i * ATTN_BLOCK, (i + 1) * ATTN_BLOCK)
        prev_rows = slice((i - 1) * ATTN_BLOCK, i * ATTN_BLOCK)
        k_prev = [kp_ref[:, c] if i == 0 else kc_ref[prev_rows, c] for c in head_cols]
        v_prev = [vp_ref[:, c] if i == 0 else vc_ref[prev_rows, c] for c in head_cols]
        q = [q_ref[rows, c] * 0.125 for c in head_cols]
        s_prev = jnp.concatenate([lax.dot_general(q[h], k_prev[h], _NT, preferred_element_type=F32)
                                  for h in range(DIL_HEADS)], axis=0) + bias_ref[0]
        s_cur = jnp.concatenate([lax.dot_general(q[h], kc_ref[rows, head_cols[h]], _NT, preferred_element_type=F32)
                                 for h in range(DIL_HEADS)], axis=0) + bias_ref[1]
        if i == 0:
            s_prev = jnp.where(first, NEG_INF, s_prev)
        p_prev, p_cur, m, denom = _band_softmax(s_prev, s_cur, None)
        lse = m + jnp.log(denom)
        for h in range(DIL_HEADS):
            hr = slice(h * ATTN_BLOCK, (h + 1) * ATTN_BLOCK)
            o = (jnp.dot(p_prev[hr], v_prev[h], preferred_element_type=F32)
                 + jnp.dot(p_cur[hr], vc_ref[rows, head_cols[h]], preferred_element_type=F32)) / denom[hr]
            o_ref[rows, head_cols[h]] = o.astype(o_ref.dtype)
            lse_ref[rows, head_cols[h]] = jnp.broadcast_to(lse[hr], (ATTN_BLOCK, HEAD_DIM))


def _dil_attention(a_g, slopes_g, g):
    window, dil = DIL_GROUPS[g]
    bsz, _, sub_len, _ = a_g.shape
    tq = min(512, sub_len)
    nqb = tq // ATTN_BLOCK
    cur = lambda part: (lambda b, r, m: (b, r, m, part))
    prev = lambda part: (lambda b, r, m: (b, r, jnp.maximum(m * nqb - 1, 0), part))
    blk = lambda rows, imap: pl.BlockSpec((None, None, rows, DIL_OUT), imap)
    return pl.pallas_call(
        functools.partial(_dil_attn_kernel, tq=tq, max_diff=window // dil, dist_scale=float(dil)),
        grid=(bsz, dil, sub_len // tq),
        in_specs=[pl.BlockSpec(memory_space=pltpu.SMEM),
                  blk(tq, cur(0)), blk(tq, cur(1)), blk(ATTN_BLOCK, prev(1)),
                  blk(tq, cur(2)), blk(ATTN_BLOCK, prev(2))],
        out_specs=[blk(tq, cur(0)), blk(tq, cur(0))],
        out_shape=[jax.ShapeDtypeStruct((bsz, dil, sub_len, DIL_OUT), BF16),
                   jax.ShapeDtypeStruct((bsz, dil, sub_len, DIL_OUT), F32)],
        scratch_shapes=[pltpu.VMEM((2, DIL_HEADS * ATTN_BLOCK, ATTN_BLOCK), F32)],
        compiler_params=_cparams("parallel", "parallel", "arbitrary"),
        name=f"dil_attn_g{g}",
    )(slopes_g, a_g, a_g, a_g, a_g, a_g)


def _swa_attn_kernel(slope_ref, sink_ref, q_ref, kvc_ref, kvp_ref, o_ref, bias_ref, *, tq):
    first = pl.program_id(1) == 0
    for kvh in range(SWA_KV_HEADS):
        head0 = kvh * SWA_REP
        _store_band_bias(bias_ref, _per_head_column(slope_ref, head0, SWA_REP), SWA_WINDOW - 1)
        sink_col = _per_head_column(sink_ref, head0, SWA_REP)
        kcols = slice(kvh * HEAD_DIM, (kvh + 1) * HEAD_DIM)
        vcols = slice(SWA_KV_WIDTH + kvh * HEAD_DIM, SWA_KV_WIDTH + (kvh + 1) * HEAD_DIM)
        head_cols = [slice((head0 + r) * HEAD_DIM, (head0 + r + 1) * HEAD_DIM) for r in range(SWA_REP)]
        for i in range(tq // ATTN_BLOCK):
            rows = slice(i * ATTN_BLOCK, (i + 1) * ATTN_BLOCK)
            prev_rows = slice((i - 1) * ATTN_BLOCK, i * ATTN_BLOCK)
            k_prev = kvp_ref[:, kcols] if i == 0 else kvc_ref[prev_rows, kcols]
            v_prev = kvp_ref[:, vcols] if i == 0 else kvc_ref[prev_rows, vcols]
            q = jnp.concatenate([q_ref[rows, c] for c in head_cols], axis=0) * 0.125
            s_prev = lax.dot_general(q, k_prev, _NT, preferred_element_type=F32) + bias_ref[0]
            s_cur = lax.dot_general(q, kvc_ref[rows, kcols], _NT, preferred_element_type=F32) + bias_ref[1]
            if i == 0:
                s_prev = jnp.where(first, NEG_INF, s_prev)
            p_prev, p_cur, _, denom = _band_softmax(s_prev, s_cur, sink_col)
            o = (jnp.dot(p_prev, v_prev, preferred_element_type=F32)
                 + jnp.dot(p_cur, kvc_ref[rows, vcols], preferred_element_type=F32)) / denom
            for r in range(SWA_REP):
                o_ref[rows, head_cols[r]] = o[r * ATTN_BLOCK:(r + 1) * ATTN_BLOCK].astype(o_ref.dtype)


def _swa_attention(hq, hkv, slopes_b, sinks, bsz, seq, tq=512):
    nqb = tq // ATTN_BLOCK
    hq3 = hq.reshape(bsz, seq, SWA_Q_WIDTH)
    hkv3 = hkv.reshape(bsz, seq, 2 * SWA_KV_WIDTH)
    smem = pl.BlockSpec(memory_space=pltpu.SMEM)
    out = pl.pallas_call(
        functools.partial(_swa_attn_kernel, tq=tq),
        grid=(bsz, seq // tq),
        in_specs=[smem, smem,
                  pl.BlockSpec((None, tq, SWA_Q_WIDTH), lambda b, m: (b, m, 0)),
                  pl.BlockSpec((None, tq, 2 * SWA_KV_WIDTH), lambda b, m: (b, m, 0)),
                  pl.BlockSpec((None, ATTN_BLOCK, 2 * SWA_KV_WIDTH),
                               lambda b, m: (b, jnp.maximum(m * nqb - 1, 0), 0))],
        out_specs=pl.BlockSpec((None, tq, SWA_Q_WIDTH), lambda b, m: (b, m, 0)),
        out_shape=jax.ShapeDtypeStruct((bsz, seq, SWA_Q_WIDTH), BF16),
        scratch_shapes=[pltpu.VMEM((2, SWA_REP * ATTN_BLOCK, ATTN_BLOCK), F32)],
        compiler_params=_cparams("parallel", "arbitrary"),
        name="swa_attn",
    )(slopes_b, sinks, hq3, hkv3, hkv3)
    return out.reshape(bsz * seq, SWA_Q_WIDTH)


def _sigmoid(x):
    return 0.5 * (jnp.tanh(0.5 * x) + 1.0)


def _layer_norm(z, g, b):
    mu = jnp.mean(z, axis=-1, keepdims=True)
    zc = z - mu
    var = jnp.mean(zc * zc, axis=-1, keepdims=True)
    return zc * lax.rsqrt(var + LN_EPS) * g + b


def _mix_out_kernel(x_ref, o0_ref, o1_ref, o2_ref, l0_ref, l1_ref, l2_ref, ob_ref, g_ref,
                    wpa_ref, wpb_ref, wo_ref, lng_ref, lnb_ref, rw_ref, rb_ref,
                    y_ref, ypk_ref, idx_ref, gate_ref, *scratch, d_model):
    def natural(ref, stages):
        dil, per, _ = ref.shape
        if dil == 1:
            return ref[0].astype(F32)
        for half, stage in enumerate(stages):
            for r in range(dil):
                stage[pl.ds(r, per, stride=dil), :] = ref[r, :, half * LANES:(half + 1) * LANES].astype(F32)
        return jnp.concatenate([stage[...] for stage in stages], axis=1)

    o0, o1, o2 = natural(o0_ref, None), natural(o1_ref, scratch[0:2]), natural(o2_ref, scratch[2:4])
    l0, l1, l2 = natural(l0_ref, None), natural(l1_ref, scratch[4:6]), natural(l2_ref, scratch[6:8])
    lm = jnp.maximum(jnp.maximum(l0, l1), l2)
    e0, e1, e2 = jnp.exp(l0 - lm), jnp.exp(l1 - lm), jnp.exp(l2 - lm)
    esum = e0 + e1 + e2
    out_a = (e0 / esum) * o0 + (e1 / esum) * o1 + (e2 / esum) * o2
    pa = jnp.dot(out_a.astype(BF16), wpa_ref[...], preferred_element_type=F32)
    pb = jnp.dot(ob_ref[...], wpb_ref[...], preferred_element_type=F32)
    ga = _sigmoid(g_ref[:, :d_model].astype(F32))
    gb = _sigmoid(g_ref[:, d_model:].astype(F32))
    merged = ga * pa + gb * pb
    mix = jnp.dot(merged.astype(BF16), wo_ref[...], preferred_element_type=F32)
    y = _layer_norm(DEEPNORM_ALPHA * x_ref[...] + mix, lng_ref[...], lnb_ref[...])
    y_ref[...] = y
    ypk_ref[...] = _pack_bf16_pairs(y)

    logits = jnp.dot(y.astype(BF16), rw_ref[...], preferred_element_type=F32) + rb_ref[...]
    lane = lax.broadcasted_iota(I32, logits.shape, 1)
    logits = jnp.where(lane < N_EXPERTS, logits, -jnp.inf)
    idx_out = jnp.zeros(logits.shape, I32)
    val_out = jnp.full(logits.shape, -jnp.inf, F32)
    for k in range(TOP_K):
        top = jnp.max(logits, axis=-1, keepdims=True)
        top_idx = jnp.min(jnp.where(logits == top, lane, LANES), axis=-1, keepdims=True)
        idx_out = jnp.where(lane == k, top_idx, idx_out)
        val_out = jnp.where(lane == k, top, val_out)
        logits = jnp.where(lane == top_idx, -jnp.inf, logits)
    ev = jnp.exp(val_out - jnp.max(val_out, axis=-1, keepdims=True))
    gates = ev / jnp.sum(ev, axis=-1, keepdims=True)
    idx_ref[...] = idx_out[:, :TOP_K]
    gate_ref[...] = gates[:, :TOP_K]


def _mix_out(x2, o_g, lse_g, out_b, gates_h, wpa, wpb, wo, ln_g, ln_b, rw_pad, rb_pad, seq, tm=512):
    n, d = x2.shape
    tiles = seq // tm
    row = lambda i: (i, 0)
    const = lambda i: (0, 0)
    rb = lambda w: pl.BlockSpec((tm, w), row)
    full = lambda a: pl.BlockSpec(a.shape, const)
    dil_specs = [pl.BlockSpec((None, dil, tm // dil, DIL_OUT), lambda i: (i // tiles, 0, i % tiles, 0))
                 for _, dil in DIL_GROUPS]
    return pl.pallas_call(
        functools.partial(_mix_out_kernel, d_model=d),
        grid=(n // tm,),
        in_specs=[rb(d)] + dil_specs + dil_specs + [rb(SWA_Q_WIDTH), rb(2 * d),
                  full(wpa), full(wpb), full(wo), full(ln_g), full(ln_b), full(rw_pad), full(rb_pad)],
        out_specs=[rb(d), rb(d // 2), rb(TOP_K), rb(TOP_K)],
        out_shape=[jax.ShapeDtypeStruct((n, d), F32), jax.ShapeDtypeStruct((n, d // 2), jnp.uint32),
                   jax.ShapeDtypeStruct((n, TOP_K), I32), jax.ShapeDtypeStruct((n, TOP_K), F32)],
        scratch_shapes=[pltpu.VMEM((tm, LANES), F32)] * (4 * DIL_OUT // LANES),
        compiler_params=_cparams("parallel"),
        name="mix_out",
    )(x2, *o_g, *lse_g, out_b, gates_h, wpa, wpb, wo, ln_g, ln_b, rw_pad, rb_pad)


def _token_mixer_ln1(x2, w_in, sinks, w_proj_a, w_proj_b, w_out, ln1_g, ln1_b, router_w, router_b, bsz, seq):
    n, d = x2.shape
    heads = jnp.arange(1, N_ALIBI_HEADS + 1, dtype=F32)
    slopes = jnp.exp2(-8.0 * heads / N_ALIBI_HEADS)
    *a_g, hq, hkv, gates_h = _in_proj(x2, w_in.astype(BF16), bsz, seq)
    o_g, lse_g = [], []
    for g in range(N_DIL):
        sl = slopes[SWA_Q_HEADS + g * DIL_HEADS:SWA_Q_HEADS + (g + 1) * DIL_HEADS]
        o, lse = _dil_attention(a_g[g], sl, g)
        o_g.append(o)
        lse_g.append(lse)
    out_b = _swa_attention(hq, hkv, slopes[:SWA_Q_HEADS], sinks.astype(F32), bsz, seq)
    rw_pad = jnp.zeros((d, LANES), BF16).at[:, :N_EXPERTS].set(router_w.astype(BF16))
    rb_pad = jnp.zeros((1, LANES), F32).at[0, :N_EXPERTS].set(router_b.astype(F32))
    return _mix_out(x2, o_g, lse_g, out_b, gates_h, w_proj_a.astype(BF16), w_proj_b.astype(BF16),
                    w_out.astype(BF16), ln1_g.reshape(1, d).astype(F32), ln1_b.reshape(1, d).astype(F32),
                    rw_pad, rb_pad, seq)


def _route_kernel(idx_ref, dest_ref, cnt_ref, carry_ref, pstart_ref, *, expert_block):
    phase, i = pl.program_id(0), pl.program_id(1)
    idx = idx_ref[...]
    t = idx.shape[0]
    lane = lax.broadcasted_iota(I32, (t, LANES), 1)
    onehot = [lane == idx[:, k:k + 1] for k in range(TOP_K)]
    multi = sum(oh.astype(F32) for oh in onehot)
    tile_cnt = jnp.sum(multi, axis=0, keepdims=True).astype(I32)

    @pl.when((phase == 0) & (i == 0))
    def _():
        carry_ref[...] = jnp.zeros_like(carry_ref)

    @pl.when((phase == 1) & (i == 0))
    def _():
        counts = carry_ref[...]
        padded = (counts + (expert_block - 1)) & (-expert_block)
        lane8 = lax.broadcasted_iota(I32, counts.shape, 1)
        incl = padded
        for sh in (1, 2, 4, 8, 16):
            incl = incl + jnp.where(lane8 >= sh, pltpu.roll(incl, sh, axis=1), 0)
        pstart_ref[...] = incl - padded
        cnt_ref[...] = counts
        carry_ref[...] = jnp.zeros_like(carry_ref)

    @pl.when(phase == 1)
    def _():
        r = lax.broadcasted_iota(I32, (t, t), 0)
        c = lax.broadcasted_iota(I32, (t, t), 1)
        earlier = (c < r).astype(BF16)
        cum = jnp.dot(earlier, multi.astype(BF16), preferred_element_type=F32).astype(I32)
        base = cum + carry_ref[0:1, :] + pstart_ref[0:1, :]
        dest = jnp.zeros((t, LANES), I32)
        for k in range(TOP_K):
            d_k = jnp.sum(jnp.where(onehot[k], base, 0), axis=-1, keepdims=True)
            dest = jnp.where(lane == k, d_k, dest)
        dest_ref[...] = dest[:, :TOP_K]

    carry_ref[...] = carry_ref[...] + tile_cnt


def _route(idx, expert_block, tile=512):
    n = idx.shape[0]
    tile = min(tile, n)
    return pl.pallas_call(
        functools.partial(_route_kernel, expert_block=expert_block),
        grid=(2, n // tile),
        in_specs=[pl.BlockSpec((tile, TOP_K), lambda p, i: (i, 0))],
        out_specs=[pl.BlockSpec((tile, TOP_K), lambda p, i: (i * p, 0)),
                   pl.BlockSpec((8, LANES), lambda p, i: (0, 0))],
        out_shape=[jax.ShapeDtypeStruct((n, TOP_K), I32), jax.ShapeDtypeStruct((8, LANES), I32)],
        scratch_shapes=[pltpu.VMEM((8, LANES), I32), pltpu.VMEM((8, LANES), I32)],
        compiler_params=_cparams("arbitrary", "arbitrary"),
        name="route",
    )(idx)


def _sc_dispatch(dest_t, y, rows, window=64):
    n, d = y.shape
    info = plsc.get_sparse_core_info()
    n_cores = info.num_cores
    per_worker = n // (n_cores * info.num_subcores)
    mesh = plsc.VectorSubcoreMesh(core_axis_name="c", subcore_axis_name="s")

    @functools.partial(pl.kernel, mesh=mesh, out_type=jax.ShapeDtypeStruct((rows, d), y.dtype),
                       scratch_types=[pltpu.VMEM((TOP_K, window), I32), pltpu.VMEM((window, d), y.dtype)],
                       name="sc_dispatch")
    def body(dest_hbm, y_hbm, xs_hbm, idx_v, rows_v):
        base = (lax.axis_index("s") * n_cores + lax.axis_index("c")) * per_worker

        @pl.loop(0, per_worker // window)
        def _(c):
            t0 = base + c * window
            for k in range(TOP_K):
                pltpu.sync_copy(dest_hbm.at[k, pl.ds(t0, window)], idx_v.at[k])
            pltpu.sync_copy(y_hbm.at[pl.ds(t0, window)], rows_v)
            for k in range(TOP_K):
                pltpu.sync_copy(rows_v, xs_hbm.at[idx_v.at[k]])

    return body(dest_t, y)


def _expert_kernel(be_ref, nu_ref, nv_ref, x_ref, wg_ref, bg_ref, wu_ref, bu_ref, wd_ref, bd_ref, o_ref,
                   wg_bf, wu_bf, wd_bf):
    j = pl.program_id(0)
    used = j < nu_ref[0]
    new_expert = (j == 0) | (be_ref[j] != be_ref[jnp.maximum(j - 1, 0)])

    @pl.when(used & new_expert)
    def _():
        chunk = 128
        for src, dst in ((wg_ref, wg_bf), (wu_ref, wu_bf), (wd_ref, wd_bf)):
            for r in range(0, src.shape[0], chunk):
                dst[r:r + chunk, :] = src[r:r + chunk, :].astype(BF16)

    @pl.when(used)
    def _():
        row = lax.broadcasted_iota(I32, x_ref.shape, 0)
        x_lo, x_hi = _unpack_bf16_pairs(jnp.where(row < nv_ref[j], x_ref[...], jnp.uint32(0)))
        xb = jnp.concatenate([x_lo.astype(BF16), x_hi.astype(BF16)], axis=1)
        gt = jnp.dot(xb, wg_bf[...], preferred_element_type=F32) + bg_ref[...]
        up = jnp.dot(xb, wu_bf[...], preferred_element_type=F32) + bu_ref[...]
        gt = jnp.minimum(gt, SWIGLU_LIMIT)
        up = jnp.clip(up, -SWIGLU_LIMIT, SWIGLU_LIMIT)
        hdn = gt * _sigmoid(SWIGLU_ALPHA * gt) * (up + 1.0)
        o_ref[...] = _pack_bf16_pairs(jnp.dot(hdn.astype(BF16), wd_bf[...], preferred_element_type=F32)
                                      + bd_ref[...])

    @pl.when(jnp.logical_not(used))
    def _():
        o_ref[...] = jnp.zeros_like(o_ref)


def _experts(block_e, n_used, n_valid, xs, wg, bg, wu, bu, wd, bd, expert_block):
    rows, _ = xs.shape
    e, d, f = wg.shape
    xmap = lambda j, be, nu, nv: (jnp.minimum(j, nu[0] - 1), 0)
    wmap = lambda j, be, nu, nv: (be[j], 0, 0)
    grid_spec = pltpu.PrefetchScalarGridSpec(
        num_scalar_prefetch=3,
        grid=(rows // expert_block,),
        in_specs=[pl.BlockSpec((expert_block, d // 2), xmap),
                  pl.BlockSpec((None, d, f), wmap), pl.BlockSpec((None, 1, f), wmap),
                  pl.BlockSpec((None, d, f), wmap), pl.BlockSpec((None, 1, f), wmap),
                  pl.BlockSpec((None, f, d), wmap), pl.BlockSpec((None, 1, d), wmap)],
        out_specs=pl.BlockSpec((expert_block, d // 2), lambda j, be, nu, nv: (j, 0)),
        scratch_shapes=[pltpu.VMEM((d, f), BF16), pltpu.VMEM((d, f), BF16), pltpu.VMEM((f, d), BF16)],
    )
    return pl.pallas_call(
        _expert_kernel,
        grid_spec=grid_spec,
        out_shape=jax.ShapeDtypeStruct((rows, d // 2), jnp.uint32),
        compiler_params=_cparams("arbitrary"),
        name="experts",
    )(block_e, n_used, n_valid, xs, wg, bg.reshape(e, 1, f), wu, bu.reshape(e, 1, f), wd, bd.reshape(e, 1, d))


def _sc_gather(dest_t, ys, n, window=64):
    _, d = ys.shape
    info = plsc.get_sparse_core_info()
    n_cores = info.num_cores
    per_worker = n // (n_cores * info.num_subcores)
    mesh = plsc.VectorSubcoreMesh(core_axis_name="c", subcore_axis_name="s")

    n_chunks = per_worker // window
    n_buf = 2

    @functools.partial(pl.kernel, mesh=mesh, out_type=jax.ShapeDtypeStruct((TOP_K, n, d), ys.dtype),
                       scratch_types=[pltpu.VMEM((TOP_K, per_worker), I32)]
                       + [pltpu.VMEM((window, d), ys.dtype)] * n_buf + [pltpu.SemaphoreType.DMA] * (2 * n_buf),
                       name="sc_gather")
    def body(dest_hbm, ys_hbm, out_hbm, idx_v, buf0, buf1, gsem0, gsem1, ssem0, ssem1):
        bufs, gsem, ssem = (buf0, buf1), (gsem0, gsem1), (ssem0, ssem1)
        base = (lax.axis_index("s") * n_cores + lax.axis_index("c")) * per_worker
        for k in range(TOP_K):
            pltpu.sync_copy(dest_hbm.at[k, pl.ds(base, per_worker)], idx_v.at[k])

        def gather(c, k):
            return pltpu.make_async_copy(ys_hbm.at[idx_v.at[k, pl.ds(c * window, window)]], bufs[k % n_buf],
                                         gsem[k % n_buf])

        def store(c, k):
            return pltpu.make_async_copy(bufs[k % n_buf], out_hbm.at[k, pl.ds(base + c * window, window)],
                                         ssem[k % n_buf])

        gather(0, 0).start()

        @pl.loop(0, n_chunks)
        def _(c):
            for k in range(TOP_K):
                if k > 0:
                    store(c, k - 1).wait()
                    if k + 1 < TOP_K:
                        gather(c, k + 1).start()
                    else:
                        @pl.when(c + 1 < n_chunks)
                        def _():
                            gather(c + 1, 0).start()
                else:
                    @pl.when(c > 0)
                    def _():
                        store(c - 1, TOP_K - 1).wait()
                    gather(c, 1).start()
                gather(c, k).wait()
                store(c, k).start()

        store(n_chunks - 1, TOP_K - 1).wait()

    return body(dest_t, ys)


def _combine_dense_kernel(gate_ref, y_ref, lng_ref, lnb_ref, g_ref, o_ref):
    gates = gate_ref[...]
    ffn_lo = ffn_hi = None
    for k in range(TOP_K):
        lo, hi = _unpack_bf16_pairs(g_ref[k])
        gate = gates[:, k:k + 1]
        ffn_lo = gate * lo if k == 0 else ffn_lo + gate * lo
        ffn_hi = gate * hi if k == 0 else ffn_hi + gate * hi
    ffn = jnp.concatenate([ffn_lo, ffn_hi], axis=1)
    o_ref[...] = _layer_norm(DEEPNORM_ALPHA * y_ref[...] + ffn, lng_ref[...], lnb_ref[...])


def _combine_dense(gates, y, ln_g, ln_b, gathered, tile=512):
    n, d = y.shape
    tile = min(tile, n)
    row = lambda i: (i, 0)
    const = lambda i: (0, 0)
    return pl.pallas_call(
        _combine_dense_kernel,
        grid=(n // tile,),
        in_specs=[pl.BlockSpec((tile, TOP_K), row), pl.BlockSpec((tile, d), row),
                  pl.BlockSpec((1, d), const), pl.BlockSpec((1, d), const),
                  pl.BlockSpec((TOP_K, tile, d // 2), lambda i: (0, i, 0))],
        out_specs=pl.BlockSpec((tile, d), row),
        out_shape=jax.ShapeDtypeStruct((n, d), F32),
        compiler_params=_cparams("parallel"),
        name="combine_dense",
    )(gates, y, ln_g, ln_b, gathered)


MOE_ROW_BLOCK = 512


def _moe_ln2(y, y_packed, idx, gates, w_gate, b_gate, w_up, b_up, w_down, b_down, ln2_g, ln2_b):
    n, d = y.shape
    n_blocks = n * TOP_K // MOE_ROW_BLOCK + N_EXPERTS
    rows = n_blocks * MOE_ROW_BLOCK
    dest, counts = _route(idx, MOE_ROW_BLOCK)
    cnt = counts[0, :N_EXPERTS]
    padded = (cnt + MOE_ROW_BLOCK - 1) // MOE_ROW_BLOCK * MOE_ROW_BLOCK
    p_end = jnp.cumsum(padded)
    block_row0 = jnp.arange(n_blocks, dtype=I32) * MOE_ROW_BLOCK
    block_e = jnp.minimum(jnp.sum(p_end[None, :] <= block_row0[:, None], axis=1), N_EXPERTS - 1).astype(I32)
    n_used = (p_end[-1:] // MOE_ROW_BLOCK).astype(I32)
    n_valid = jnp.clip(cnt[block_e] - (block_row0 - (p_end - padded)[block_e]), 0, MOE_ROW_BLOCK).astype(I32)
    dest_t = dest.T
    xs = _sc_dispatch(dest_t, y_packed, rows)
    ys = _experts(block_e, n_used, n_valid, xs, w_gate, b_gate.astype(F32), w_up, b_up.astype(F32),
                  w_down, b_down.astype(F32), MOE_ROW_BLOCK)
    gathered = _sc_gather(dest_t, ys, n)
    return _combine_dense(gates, y, ln2_g.reshape(1, d).astype(F32), ln2_b.reshape(1, d).astype(F32), gathered)


def kernel(x, w_in, sinks, w_proj_a, w_proj_b, w_out, ln1_g, ln1_b, router_w, router_b,
           w_gate, b_gate, w_up, b_up, w_down, b_down, ln2_g, ln2_b):
    bsz, seq, d = x.shape
    h = x.reshape(bsz * seq, d)
    for l in range(w_in.shape[0]):
        y, y_packed, idx, gates = _token_mixer_ln1(h, w_in[l], sinks[l], w_proj_a[l], w_proj_b[l], w_out[l],
                                                   ln1_g[l], ln1_b[l], router_w[l], router_b[l], bsz, seq)
        h = _moe_ln2(y, y_packed, idx, gates, w_gate[l], b_gate[l], w_up[l], b_up[l], w_down[l], b_down[l],
                     ln2_g[l], ln2_b[l])
    return h.reshape(bsz, seq, d)
```

```python
import functools

import jax
import jax.numpy as jnp
from jax import lax
from jax.experimental import pallas as pl
from jax.experimental.pallas import tpu as pltpu
from jax.experimental.pallas import tpu_sc as plsc

F32 = jnp.float32
BF16 = jnp.bfloat16
I32 = jnp.int32

HEAD_DIM = 64
DIL_GROUPS = ((128, 1), (512, 4), (2048, 16))
DIL_HEADS = 4
N_DIL = len(DIL_GROUPS)
DIL_OUT = DIL_HEADS * HEAD_DIM
DIL_WIDTH = N_DIL * DIL_OUT
SWA_Q_HEADS = 16
SWA_KV_HEADS = 2
SWA_REP = SWA_Q_HEADS // SWA_KV_HEADS
SWA_WINDOW = 128
SWA_Q_WIDTH = SWA_Q_HEADS * HEAD_DIM
SWA_KV_WIDTH = SWA_KV_HEADS * HEAD_DIM
N_ALIBI_HEADS = SWA_Q_HEADS + N_DIL * DIL_HEADS
ATTN_BLOCK = 128
N_EXPERTS = 32
TOP_K = 4
SWIGLU_LIMIT = 7.0
SWIGLU_ALPHA = 1.702
LN_EPS = 1e-5
DEPTH = 1
DEEPNORM_ALPHA = (2 * DEPTH) ** 0.25
NEG_INF = -1e30

LANES = 128
VMEM_LIMIT_BYTES = 56 * 1024 * 1024

A_QKV_W = 3 * DIL_WIDTH
B_Q_OFF = A_QKV_W
B_KV_OFF = B_Q_OFF + SWA_Q_WIDTH
GATE_OFF = B_KV_OFF + 2 * SWA_KV_WIDTH


def _cparams(*sem):
    return pltpu.CompilerParams(dimension_semantics=sem, vmem_limit_bytes=VMEM_LIMIT_BYTES)


def _pack_bf16_pairs(x):
    c = x.shape[1] // 2
    lo = lax.bitcast_convert_type(x[:, :c].astype(BF16).astype(F32), jnp.uint32)
    hi = lax.bitcast_convert_type(x[:, c:].astype(BF16).astype(F32), jnp.uint32)
    return (lo >> 16) | hi


def _unpack_bf16_pairs(w):
    lo = lax.bitcast_convert_type(w << 16, F32)
    hi = lax.bitcast_convert_type(w & jnp.uint32(0xFFFF0000), F32)
    return lo, hi


def _in_proj_kernel(x_ref, w_ref, a0_ref, a1_ref, a2_ref, hq_ref, hkv_ref, g_ref, *scratch, d_model, tm):
    xb = x_ref[...].astype(BF16)
    segments = ((hq_ref, B_Q_OFF, SWA_Q_WIDTH), (hkv_ref, B_KV_OFF, 2 * SWA_KV_WIDTH),
                (g_ref, GATE_OFF, 2 * d_model))
    for out_ref, col0, width in segments:
        for c in range(0, width, 512):
            w = min(512, width - c)
            r = jnp.dot(xb, w_ref[:, col0 + c:col0 + c + w], preferred_element_type=F32)
            out_ref[:, c:c + w] = r.astype(out_ref.dtype)
    for g, a_ref in enumerate((a0_ref, a1_ref, a2_ref)):
        dil = DIL_GROUPS[g][1]
        per = tm // dil
        for part in range(3):
            col0 = part * DIL_WIDTH + g * DIL_OUT
            res = jnp.dot(xb, w_ref[:, col0:col0 + DIL_OUT], preferred_element_type=F32)
            if dil == 1:
                a_ref[0, :, part * DIL_OUT:(part + 1) * DIL_OUT] = res.astype(a_ref.dtype)
                continue
            for half in range(DIL_OUT // LANES):
                stage = scratch[part * (DIL_OUT // LANES) + half]
                stage[...] = res[:, half * LANES:(half + 1) * LANES]
                c0 = part * DIL_OUT + half * LANES
                for r in range(dil):
                    a_ref[r, :, c0:c0 + LANES] = stage[pl.ds(r, per, stride=dil), :].astype(a_ref.dtype)


def _in_proj(x2, w_in_bf, bsz, seq, tm=512):
    n, d = x2.shape
    cols = w_in_bf.shape[1]
    tiles = seq // tm
    row = lambda i: (i, 0)
    dil_spec = lambda dil: pl.BlockSpec((None, dil, tm // dil, 3 * DIL_OUT), lambda i: (i // tiles, 0, i % tiles, 0))
    dil_shape = lambda dil: jax.ShapeDtypeStruct((bsz, dil, seq // dil, 3 * DIL_OUT), BF16)
    dils = [dil for _, dil in DIL_GROUPS]
    return pl.pallas_call(
        functools.partial(_in_proj_kernel, d_model=d, tm=tm),
        grid=(n // tm,),
        in_specs=[pl.BlockSpec((tm, d), row),
                  pl.BlockSpec((d, cols), lambda i: (0, 0), pipeline_mode=pl.Buffered(1))],
        out_specs=[dil_spec(dil) for dil in dils]
        + [pl.BlockSpec((tm, SWA_Q_WIDTH), row), pl.BlockSpec((tm, 2 * SWA_KV_WIDTH), row),
           pl.BlockSpec((tm, 2 * d), row)],
        out_shape=[dil_shape(dil) for dil in dils]
        + [jax.ShapeDtypeStruct((n, SWA_Q_WIDTH), BF16), jax.ShapeDtypeStruct((n, 2 * SWA_KV_WIDTH), BF16),
           jax.ShapeDtypeStruct((n, 2 * d), BF16)],
        scratch_shapes=[pltpu.VMEM((tm, LANES), F32)] * (3 * DIL_OUT // LANES),
        compiler_params=_cparams("parallel"),
        name="in_proj",
    )(x2, w_in_bf)


_NT = (((1,), (1,)), ((), ()))


def _per_head_column(ref, head0, n_heads, scale=1.0):
    head_of_row = lax.broadcasted_iota(I32, (n_heads * ATTN_BLOCK, 1), 0) // ATTN_BLOCK
    col = jnp.zeros((n_heads * ATTN_BLOCK, 1), F32)
    for h in range(n_heads):
        col = jnp.where(head_of_row == h, ref[head0 + h] * scale, col)
    return col


def _store_band_bias(bias_ref, slope_col, max_diff):
    rows = slope_col.shape[0]
    qi = lax.broadcasted_iota(I32, (rows, ATTN_BLOCK), 0) % ATTN_BLOCK
    kj = lax.broadcasted_iota(I32, (rows, ATTN_BLOCK), 1)
    diff_prev = qi - kj + ATTN_BLOCK
    diff_cur = qi - kj
    bias_ref[0] = jnp.where(diff_prev <= max_diff, -slope_col * diff_prev.astype(F32), NEG_INF)
    bias_ref[1] = jnp.where((diff_cur >= 0) & (diff_cur <= max_diff), -slope_col * diff_cur.astype(F32), NEG_INF)


def _band_softmax(s_prev, s_cur, sink_col):
    m = jnp.max(jnp.maximum(s_prev, s_cur), axis=-1, keepdims=True)
    if sink_col is not None:
        m = jnp.maximum(m, sink_col)
    p_prev = jnp.exp(s_prev - m)
    p_cur = jnp.exp(s_cur - m)
    denom = jnp.sum(p_prev + p_cur, axis=-1, keepdims=True)
    if sink_col is not None:
        denom = denom + jnp.exp(sink_col - m)
    return p_prev.astype(BF16), p_cur.astype(BF16), m, denom


def _dil_attn_kernel(slope_ref, q_ref, kc_ref, kp_ref, vc_ref, vp_ref, o_ref, lse_ref, bias_ref,
                     *, tq, max_diff, dist_scale):
    first = pl.program_id(2) == 0
    _store_band_bias(bias_ref, _per_head_column(slope_ref, 0, DIL_HEADS, dist_scale), max_diff)
    head_cols = [slice(h * HEAD_DIM, (h + 1) * HEAD_DIM) for h in range(DIL_HEADS)]
    for i in range(tq // ATTN_BLOCK):
        rows = slice(i * ATTN_BLOCK, (i + 1) * ATTN_BLOCK)
        prev_rows = slice((i - 1) * ATTN_BLOCK, i * ATTN_BLOCK)
        k_prev = [kp_ref[:, c] if i == 0 else kc_ref[prev_rows, c] for c in head_cols]
        v_prev = [vp_ref[:, c] if i == 0 else vc_ref[prev_rows, c] for c in head_cols]
        q = [q_ref[rows, c] * 0.125 for c in head_cols]
        s_prev = jnp.concatenate([lax.dot_general(q[h], k_prev[h], _NT, preferred_element_type=F32)
                                  for h in range(DIL_HEADS)], axis=0) + bias_ref[0]
        s_cur = jnp.concatenate([lax.dot_general(q[h], kc_ref[rows, head_cols[h]], _NT, preferred_element_type=F32)
                                 for h in range(DIL_HEADS)], axis=0) + bias_ref[1]
        if i == 0:
            s_prev = jnp.where(first, NEG_INF, s_prev)
        p_prev, p_cur, m, denom = _band_softmax(s_prev, s_cur, None)
        lse = m + jnp.log(denom)
        for h in range(DIL_HEADS):
            hr = slice(h * ATTN_BLOCK, (h + 1) * ATTN_BLOCK)
            o = (jnp.dot(p_prev[hr], v_prev[h], preferred_element_type=F32)
                 + jnp.dot(p_cur[hr], vc_ref[rows, head_cols[h]], preferred_element_type=F32)) / denom[hr]
            o_ref[rows, head_cols[h]] = o.astype(o_ref.dtype)
            lse_ref[rows, head_cols[h]] = jnp.broadcast_to(lse[hr], (ATTN_BLOCK, HEAD_DIM))


def _dil_attention(a_g, slopes_g, g):
    window, dil = DIL_GROUPS[g]
    bsz, _, sub_len, _ = a_g.shape
    tq = min(512, sub_len)
    nqb = tq // ATTN_BLOCK
    cur = lambda part: (lambda b, r, m: (b, r, m, part))
    prev = lambda part: (lambda b, r, m: (b, r, jnp.maximum(m * nqb - 1, 0), part))
    blk = lambda rows, imap: pl.BlockSpec((None, None, rows, DIL_OUT), imap)
    return pl.pallas_call(
        functools.partial(_dil_attn_kernel, tq=tq, max_diff=window // dil, dist_scale=float(dil)),
        grid=(bsz, dil, sub_len // tq),
        in_specs=[pl.BlockSpec(memory_space=pltpu.SMEM),
                  blk(tq, cur(0)), blk(tq, cur(1)), blk(ATTN_BLOCK, prev(1)),
                  blk(tq, cur(2)), blk(ATTN_BLOCK, prev(2))],
        out_specs=[blk(tq, cur(0)), blk(tq, cur(0))],
        out_shape=[jax.ShapeDtypeStruct((bsz, dil, sub_len, DIL_OUT), BF16),
                   jax.ShapeDtypeStruct((bsz, dil, sub_len, DIL_OUT), F32)],
        scratch_shapes=[pltpu.VMEM((2, DIL_HEADS * ATTN_BLOCK, ATTN_BLOCK), F32)],
        compiler_params=_cparams("parallel", "parallel", "arbitrary"),
        name=f"dil_attn_g{g}",
    )(slopes_g, a_g, a_g, a_g, a_g, a_g)


def _swa_attn_kernel(slope_ref, sink_ref, q_ref, kvc_ref, kvp_ref, o_ref, bias_ref, *, tq):
    first = pl.program_id(1) == 0
    for kvh in range(SWA_KV_HEADS):
        head0 = kvh * SWA_REP
        _store_band_bias(bias_ref, _per_head_column(slope_ref, head0, SWA_REP), SWA_WINDOW - 1)
        sink_col = _per_head_column(sink_ref, head0, SWA_REP)
        kcols = slice(kvh * HEAD_DIM, (kvh + 1) * HEAD_DIM)
        vcols = slice(SWA_KV_WIDTH + kvh * HEAD_DIM, SWA_KV_WIDTH + (kvh + 1) * HEAD_DIM)
        head_cols = [slice((head0 + r) * HEAD_DIM, (head0 + r + 1) * HEAD_DIM) for r in range(SWA_REP)]
        for i in range(tq // ATTN_BLOCK):
            rows = slice(i * ATTN_BLOCK, (i + 1) * ATTN_BLOCK)
            prev_rows = slice((i - 1) * ATTN_BLOCK, i * ATTN_BLOCK)
            k_prev = kvp_ref[:, kcols] if i == 0 else kvc_ref[prev_rows, kcols]
            v_prev = kvp_ref[:, vcols] if i == 0 else kvc_ref[prev_rows, vcols]
            q = jnp.concatenate([q_ref[rows, c] for c in head_cols], axis=0) * 0.125
            s_prev = lax.dot_general(q, k_prev, _NT, preferred_element_type=F32) + bias_ref[0]
            s_cur = lax.dot_general(q, kvc_ref[rows, kcols], _NT, preferred_element_type=F32) + bias_ref[1]
            if i == 0:
                s_prev = jnp.where(first, NEG_INF, s_prev)
            p_prev, p_cur, _, denom = _band_softmax(s_prev, s_cur, sink_col)
            o = (jnp.dot(p_prev, v_prev, preferred_element_type=F32)
                 + jnp.dot(p_cur, kvc_ref[rows, vcols], preferred_element_type=F32)) / denom
            for r in range(SWA_REP):
                o_ref[rows, head_cols[r]] = o[r * ATTN_BLOCK:(r + 1) * ATTN_BLOCK].astype(o_ref.dtype)


def _swa_attention(hq, hkv, slopes_b, sinks, bsz, seq, tq=512):
    nqb = tq // ATTN_BLOCK
    hq3 = hq.reshape(bsz, seq, SWA_Q_WIDTH)
    hkv3 = hkv.reshape(bsz, seq, 2 * SWA_KV_WIDTH)
    smem = pl.BlockSpec(memory_space=pltpu.SMEM)
    out = pl.pallas_call(
        functools.partial(_swa_attn_kernel, tq=tq),
        grid=(bsz, seq // tq),
        in_specs=[smem, smem,
                  pl.BlockSpec((None, tq, SWA_Q_WIDTH), lambda b, m: (b, m, 0)),
                  pl.BlockSpec((None, tq, 2 * SWA_KV_WIDTH), lambda b, m: (b, m, 0)),
                  pl.BlockSpec((None, ATTN_BLOCK, 2 * SWA_KV_WIDTH),
                               lambda b, m: (b, jnp.maximum(m * nqb - 1, 0), 0))],
        out_specs=pl.BlockSpec((None, tq, SWA_Q_WIDTH), lambda b, m: (b, m, 0)),
        out_shape=jax.ShapeDtypeStruct((bsz, seq, SWA_Q_WIDTH), BF16),
        scratch_shapes=[pltpu.VMEM((2, SWA_REP * ATTN_BLOCK, ATTN_BLOCK), F32)],
        compiler_params=_cparams("parallel", "arbitrary"),
        name="swa_attn",
    )(slopes_b, sinks, hq3, hkv3, hkv3)
    return out.reshape(bsz * seq, SWA_Q_WIDTH)


def _sigmoid(x):
    return 0.5 * (jnp.tanh(0.5 * x) + 1.0)


def _layer_norm(z, g, b):
    mu = jnp.mean(z, axis=-1, keepdims=True)
    zc = z - mu
    var = jnp.mean(zc * zc, axis=-1, keepdims=True)
    return zc * lax.rsqrt(var + LN_EPS) * g + b


def _mix_out_kernel(x_ref, o0_ref, o1_ref, o2_ref, l0_ref, l1_ref, l2_ref, ob_ref, g_ref,
                    wpa_ref, wpb_ref, wo_ref, lng_ref, lnb_ref, rw_ref, rb_ref,
                    y_ref, ypk_ref, idx_ref, gate_ref, *scratch, d_model):
    def natural(ref, stages):
        dil, per, _ = ref.shape
        if dil == 1:
            return ref[0].astype(F32)
        for half, stage in enumerate(stages):
            for r in range(dil):
                stage[pl.ds(r, per, stride=dil), :] = ref[r, :, half * LANES:(half + 1) * LANES].astype(F32)
        return jnp.concatenate([stage[...] for stage in stages], axis=1)

    o0, o1, o2 = natural(o0_ref, None), natural(o1_ref, scratch[0:2]), natural(o2_ref, scratch[2:4])
    l0, l1, l2 = natural(l0_ref, None), natural(l1_ref, scratch[4:6]), natural(l2_ref, scratch[6:8])
    lm = jnp.maximum(jnp.maximum(l0, l1), l2)
    e0, e1, e2 = jnp.exp(l0 - lm), jnp.exp(l1 - lm), jnp.exp(l2 - lm)
    esum = e0 + e1 + e2
    out_a = (e0 / esum) * o0 + (e1 / esum) * o1 + (e2 / esum) * o2
    pa = jnp.dot(out_a.astype(BF16), wpa_ref[...], preferred_element_type=F32)
    pb = jnp.dot(ob_ref[...], wpb_ref[...], preferred_element_type=F32)
    ga = _sigmoid(g_ref[:, :d_model].astype(F32))
    gb = _sigmoid(g_ref[:, d_model:].astype(F32))
    merged = ga * pa + gb * pb
    mix = jnp.dot(merged.astype(BF16), wo_ref[...], preferred_element_type=F32)
    y = _layer_norm(DEEPNORM_ALPHA * x_ref[...] + mix, lng_ref[...], lnb_ref[...])
    y_ref[...] = y
    ypk_ref[...] = _pack_bf16_pairs(y)

    logits = jnp.dot(y.astype(BF16), rw_ref[...], preferred_element_type=F32) + rb_ref[...]
    lane = lax.broadcasted_iota(I32, logits.shape, 1)
    logits = jnp.where(lane < N_EXPERTS, logits, -jnp.inf)
    idx_out = jnp.zeros(logits.shape, I32)
    val_out = jnp.full(logits.shape, -jnp.inf, F32)
    for k in range(TOP_K):
        top = jnp.max(logits, axis=-1, keepdims=True)
        top_idx = jnp.min(jnp.where(logits == top, lane, LANES), axis=-1, keepdims=True)
        idx_out = jnp.where(lane == k, top_idx, idx_out)
        val_out = jnp.where(lane == k, top, val_out)
        logits = jnp.where(lane == top_idx, -jnp.inf, logits)
    ev = jnp.exp(val_out - jnp.max(val_out, axis=-1, keepdims=True))
    gates = ev / jnp.sum(ev, axis=-1, keepdims=True)
    idx_ref[...] = idx_out[:, :TOP_K]
    gate_ref[...] = gates[:, :TOP_K]


def _mix_out(x2, o_g, lse_g, out_b, gates_h, wpa, wpb, wo, ln_g, ln_b, rw_pad, rb_pad, seq, tm=512):
    n, d = x2.shape
    tiles = seq // tm
    row = lambda i: (i, 0)
    const = lambda i: (0, 0)
    rb = lambda w: pl.BlockSpec((tm, w), row)
    full = lambda a: pl.BlockSpec(a.shape, const)
    dil_specs = [pl.BlockSpec((None, dil, tm // dil, DIL_OUT), lambda i: (i // tiles, 0, i % tiles, 0))
                 for _, dil in DIL_GROUPS]
    return pl.pallas_call(
        functools.partial(_mix_out_kernel, d_model=d),
        grid=(n // tm,),
        in_specs=[rb(d)] + dil_specs + dil_specs + [rb(SWA_Q_WIDTH), rb(2 * d),
                  full(wpa), full(wpb), full(wo), full(ln_g), full(ln_b), full(rw_pad), full(rb_pad)],
        out_specs=[rb(d), rb(d // 2), rb(TOP_K), rb(TOP_K)],
        out_shape=[jax.ShapeDtypeStruct((n, d), F32), jax.ShapeDtypeStruct((n, d // 2), jnp.uint32),
                   jax.ShapeDtypeStruct((n, TOP_K), I32), jax.ShapeDtypeStruct((n, TOP_K), F32)],
        scratch_shapes=[pltpu.VMEM((tm, LANES), F32)] * (4 * DIL_OUT // LANES),
        compiler_params=_cparams("parallel"),
        name="mix_out",
    )(x2, *o_g, *lse_g, out_b, gates_h, wpa, wpb, wo, ln_g, ln_b, rw_pad, rb_pad)


def _token_mixer_ln1(x2, w_in, sinks, w_proj_a, w_proj_b, w_out, ln1_g, ln1_b, router_w, router_b, bsz, seq):
    n, d = x2.shape
    heads = jnp.arange(1, N_ALIBI_HEADS + 1, dtype=F32)
    slopes = jnp.exp2(-8.0 * heads / N_ALIBI_HEADS)
    *a_g, hq, hkv, gates_h = _in_proj(x2, w_in.astype(BF16), bsz, seq)
    o_g, lse_g = [], []
    for g in range(N_DIL):
        sl = slopes[SWA_Q_HEADS + g * DIL_HEADS:SWA_Q_HEADS + (g + 1) * DIL_HEADS]
        o, lse = _dil_attention(a_g[g], sl, g)
        o_g.append(o)
        lse_g.append(lse)
    out_b = _swa_attention(hq, hkv, slopes[:SWA_Q_HEADS], sinks.astype(F32), bsz, seq)
    rw_pad = jnp.zeros((d, LANES), BF16).at[:, :N_EXPERTS].set(router_w.astype(BF16))
    rb_pad = jnp.zeros((1, LANES), F32).at[0, :N_EXPERTS].set(router_b.astype(F32))
    return _mix_out(x2, o_g, lse_g, out_b, gates_h, w_proj_a.astype(BF16), w_proj_b.astype(BF16),
                    w_out.astype(BF16), ln1_g.reshape(1, d).astype(F32), ln1_b.reshape(1, d).astype(F32),
                    rw_pad, rb_pad, seq)


def _route_kernel(idx_ref, dest_ref, cnt_ref, carry_ref, pstart_ref, *, expert_block):
    phase, i = pl.program_id(0), pl.program_id(1)
    idx = idx_ref[...]
    t = idx.shape[0]
    lane = lax.broadcasted_iota(I32, (t, LANES), 1)
    onehot = [lane == idx[:, k:k + 1] for k in range(TOP_K)]
    multi = sum(oh.astype(F32) for oh in onehot)
    tile_cnt = jnp.sum(multi, axis=0, keepdims=True).astype(I32)

    @pl.when((phase == 0) & (i == 0))
    def _():
        carry_ref[...] = jnp.zeros_like(carry_ref)

    @pl.when((phase == 1) & (i == 0))
    def _():
        counts = carry_ref[...]
        padded = (counts + (expert_block - 1)) & (-expert_block)
        lane8 = lax.broadcasted_iota(I32, counts.shape, 1)
        incl = padded
        for sh in (1, 2, 4, 8, 16):
            incl = incl + jnp.where(lane8 >= sh, pltpu.roll(incl, sh, axis=1), 0)
        pstart_ref[...] = incl - padded
        cnt_ref[...] = counts
        carry_ref[...] = jnp.zeros_like(carry_ref)

    @pl.when(phase == 1)
    def _():
        r = lax.broadcasted_iota(I32, (t, t), 0)
        c = lax.broadcasted_iota(I32, (t, t), 1)
        earlier = (c < r).astype(BF16)
        cum = jnp.dot(earlier, multi.astype(BF16), preferred_element_type=F32).astype(I32)
        base = cum + carry_ref[0:1, :] + pstart_ref[0:1, :]
        dest = jnp.zeros((t, LANES), I32)
        for k in range(TOP_K):
            d_k = jnp.sum(jnp.where(onehot[k], base, 0), axis=-1, keepdims=True)
            dest = jnp.where(lane == k, d_k, dest)
        dest_ref[...] = dest[:, :TOP_K]

    carry_ref[...] = carry_ref[...] + tile_cnt


def _route(idx, expert_block, tile=512):
    n = idx.shape[0]
    tile = min(tile, n)
    return pl.pallas_call(
        functools.partial(_route_kernel, expert_block=expert_block),
        grid=(2, n // tile),
        in_specs=[pl.BlockSpec((tile, TOP_K), lambda p, i: (i, 0))],
        out_specs=[pl.BlockSpec((tile, TOP_K), lambda p, i: (i * p, 0)),
                   pl.BlockSpec((8, LANES), lambda p, i: (0, 0))],
        out_shape=[jax.ShapeDtypeStruct((n, TOP_K), I32), jax.ShapeDtypeStruct((8, LANES), I32)],
        scratch_shapes=[pltpu.VMEM((8, LANES), I32), pltpu.VMEM((8, LANES), I32)],
        compiler_params=_cparams("arbitrary", "arbitrary"),
        name="route",
    )(idx)


def _sc_dispatch(dest_t, y, rows, window=64):
    n, d = y.shape
    info = plsc.get_sparse_core_info()
    n_cores = info.num_cores
    per_worker = n // (n_cores * info.num_subcores)
    mesh = plsc.VectorSubcoreMesh(core_axis_name="c", subcore_axis_name="s")

    @functools.partial(pl.kernel, mesh=mesh, out_type=jax.ShapeDtypeStruct((rows, d), y.dtype),
                       scratch_types=[pltpu.VMEM((TOP_K, window), I32), pltpu.VMEM((window, d), y.dtype)],
                       name="sc_dispatch")
    def body(dest_hbm, y_hbm, xs_hbm, idx_v, rows_v):
        base = (lax.axis_index("s") * n_cores + lax.axis_index("c")) * per_worker

        @pl.loop(0, per_worker // window)
        def _(c):
            t0 = base + c * window
            for k in range(TOP_K):
                pltpu.sync_copy(dest_hbm.at[k, pl.ds(t0, window)], idx_v.at[k])
            pltpu.sync_copy(y_hbm.at[pl.ds(t0, window)], rows_v)
            for k in range(TOP_K):
                pltpu.sync_copy(rows_v, xs_hbm.at[idx_v.at[k]])

    return body(dest_t, y)


def _expert_kernel(be_ref, nu_ref, nv_ref, x_ref, wg_ref, bg_ref, wu_ref, bu_ref, wd_ref, bd_ref, o_ref,
                   wg_bf, wu_bf, wd_bf):
    j = pl.program_id(0)
    used = j < nu_ref[0]
    new_expert = (j == 0) | (be_ref[j] != be_ref[jnp.maximum(j - 1, 0)])

    @pl.when(used & new_expert)
    def _():
        chunk = 128
        for src, dst in ((wg_ref, wg_bf), (wu_ref, wu_bf), (wd_ref, wd_bf)):
            for r in range(0, src.shape[0], chunk):
                dst[r:r + chunk, :] = src[r:r + chunk, :].astype(BF16)

    @pl.when(used)
    def _():
        row = lax.broadcasted_iota(I32, x_ref.shape, 0)
        x_lo, x_hi = _unpack_bf16_pairs(jnp.where(row < nv_ref[j], x_ref[...], jnp.uint32(0)))
        xb = jnp.concatenate([x_lo.astype(BF16), x_hi.astype(BF16)], axis=1)
        gt = jnp.dot(xb, wg_bf[...], preferred_element_type=F32) + bg_ref[...]
        up = jnp.dot(xb, wu_bf[...], preferred_element_type=F32) + bu_ref[...]
        gt = jnp.minimum(gt, SWIGLU_LIMIT)
        up = jnp.clip(up, -SWIGLU_LIMIT, SWIGLU_LIMIT)
        hdn = gt * _sigmoid(SWIGLU_ALPHA * gt) * (up + 1.0)
        o_ref[...] = _pack_bf16_pairs(jnp.dot(hdn.astype(BF16), wd_bf[...], preferred_element_type=F32)
                                      + bd_ref[...])

    @pl.when(jnp.logical_not(used))
    def _():
        o_ref[...] = jnp.zeros_like(o_ref)


def _experts(block_e, n_used, n_valid, xs, wg, bg, wu, bu, wd, bd, expert_block):
    rows, _ = xs.shape
    e, d, f = wg.shape
    xmap = lambda j, be, nu, nv: (jnp.minimum(j, nu[0] - 1), 0)
    wmap = lambda j, be, nu, nv: (be[j], 0, 0)
    grid_spec = pltpu.PrefetchScalarGridSpec(
        num_scalar_prefetch=3,
        grid=(rows // expert_block,),
        in_specs=[pl.BlockSpec((expert_block, d // 2), xmap),
                  pl.BlockSpec((None, d, f), wmap), pl.BlockSpec((None, 1, f), wmap),
                  pl.BlockSpec((None, d, f), wmap), pl.BlockSpec((None, 1, f), wmap),
                  pl.BlockSpec((None, f, d), wmap), pl.BlockSpec((None, 1, d), wmap)],
        out_specs=pl.BlockSpec((expert_block, d // 2), lambda j, be, nu, nv: (j, 0)),
        scratch_shapes=[pltpu.VMEM((d, f), BF16), pltpu.VMEM((d, f), BF16), pltpu.VMEM((f, d), BF16)],
    )
    return pl.pallas_call(
        _expert_kernel,
        grid_spec=grid_spec,
        out_shape=jax.ShapeDtypeStruct((rows, d // 2), jnp.uint32),
        compiler_params=_cparams("arbitrary"),
        name="experts",
    )(block_e, n_used, n_valid, xs, wg, bg.reshape(e, 1, f), wu, bu.reshape(e, 1, f), wd, bd.reshape(e, 1, d))


def _sc_gather(dest_t, ys, n, window=64):
    _, d = ys.shape
    info = plsc.get_sparse_core_info()
    n_cores = info.num_cores
    per_worker = n // (n_cores * info.num_subcores)
    mesh = plsc.VectorSubcoreMesh(core_axis_name="c", subcore_axis_name="s")

    n_chunks = per_worker // window
    n_buf = 2

    @functools.partial(pl.kernel, mesh=mesh, out_type=jax.ShapeDtypeStruct((TOP_K, n, d), ys.dtype),
                       scratch_types=[pltpu.VMEM((TOP_K, per_worker), I32)]
                       + [pltpu.VMEM((window, d), ys.dtype)] * n_buf + [pltpu.SemaphoreType.DMA] * (2 * n_buf),
                       name="sc_gather")
    def body(dest_hbm, ys_hbm, out_hbm, idx_v, buf0, buf1, gsem0, gsem1, ssem0, ssem1):
        bufs, gsem, ssem = (buf0, buf1), (gsem0, gsem1), (ssem0, ssem1)
        base = (lax.axis_index("s") * n_cores + lax.axis_index("c")) * per_worker
        for k in range(TOP_K):
            pltpu.sync_copy(dest_hbm.at[k, pl.ds(base, per_worker)], idx_v.at[k])

        def gather(c, k):
            return pltpu.make_async_copy(ys_hbm.at[idx_v.at[k, pl.ds(c * window, window)]], bufs[k % n_buf],
                                         gsem[k % n_buf])

        def store(c, k):
            return pltpu.make_async_copy(bufs[k % n_buf], out_hbm.at[k, pl.ds(base + c * window, window)],
                                         ssem[k % n_buf])

        gather(0, 0).start()

        @pl.loop(0, n_chunks)
        def _(c):
            for k in range(TOP_K):
                if k > 0:
                    store(c, k - 1).wait()
                    if k + 1 < TOP_K:
                        gather(c, k + 1).start()
                    else:
                        @pl.when(c + 1 < n_chunks)
                        def _():
                            gather(c + 1, 0).start()
                else:
                    @pl.when(c > 0)
                    def _():
                        store(c - 1, TOP_K - 1).wait()
                    gather(c, 1).start()
                gather(c, k).wait()
                store(c, k).start()

        store(n_chunks - 1, TOP_K - 1).wait()

    return body(dest_t, ys)


def _combine_dense_kernel(gate_ref, y_ref, lng_ref, lnb_ref, g_ref, o_ref):
    gates = gate_ref[...]
    ffn_lo = ffn_hi = None
    for k in range(TOP_K):
        lo, hi = _unpack_bf16_pairs(g_ref[k])
        gate = gates[:, k:k + 1]
        ffn_lo = gate * lo if k == 0 else ffn_lo + gate * lo
        ffn_hi = gate * hi if k == 0 else ffn_hi + gate * hi
    ffn = jnp.concatenate([ffn_lo, ffn_hi], axis=1)
    o_ref[...] = _layer_norm(DEEPNORM_ALPHA * y_ref[...] + ffn, lng_ref[...], lnb_ref[...])


def _combine_dense(gates, y, ln_g, ln_b, gathered, tile=512):
    n, d = y.shape
    tile = min(tile, n)
    row = lambda i: (i, 0)
    const = lambda i: (0, 0)
    return pl.pallas_call(
        _combine_dense_kernel,
        grid=(n // tile,),
        in_specs=[pl.BlockSpec((tile, TOP_K), row), pl.BlockSpec((tile, d), row),
                  pl.BlockSpec((1, d), const), pl.BlockSpec((1, d), const),
                  pl.BlockSpec((TOP_K, tile, d // 2), lambda i: (0, i, 0))],
        out_specs=pl.BlockSpec((tile, d), row),
        out_shape=jax.ShapeDtypeStruct((n, d), F32),
        compiler_params=_cparams("parallel"),
        name="combine_dense",
    )(gates, y, ln_g, ln_b, gathered)


BATCH_GROUPS = 2
MOE_ROW_BLOCK = 512


def _moe_ln2(y, y_packed, idx, gates, w_gate, b_gate, w_up, b_up, w_down, b_down, ln2_g, ln2_b):
    n, d = y.shape
    n_blocks = n * TOP_K // MOE_ROW_BLOCK + N_EXPERTS
    rows = n_blocks * MOE_ROW_BLOCK
    dest, counts = _route(idx, MOE_ROW_BLOCK)
    cnt = counts[0, :N_EXPERTS]
    padded = (cnt + MOE_ROW_BLOCK - 1) // MOE_ROW_BLOCK * MOE_ROW_BLOCK
    p_end = jnp.cumsum(padded)
    block_row0 = jnp.arange(n_blocks, dtype=I32) * MOE_ROW_BLOCK
    block_e = jnp.minimum(jnp.sum(p_end[None, :] <= block_row0[:, None], axis=1), N_EXPERTS - 1).astype(I32)
    n_used = (p_end[-1:] // MOE_ROW_BLOCK).astype(I32)
    n_valid = jnp.clip(cnt[block_e] - (block_row0 - (p_end - padded)[block_e]), 0, MOE_ROW_BLOCK).astype(I32)
    dest_t = dest.T
    xs = _sc_dispatch(dest_t, y_packed, rows)
    ys = _experts(block_e, n_used, n_valid, xs, w_gate, b_gate.astype(F32), w_up, b_up.astype(F32),
                  w_down, b_down.astype(F32), MOE_ROW_BLOCK)
    gathered = _sc_gather(dest_t, ys, n)
    return _combine_dense(gates, y, ln2_g.reshape(1, d).astype(F32), ln2_b.reshape(1, d).astype(F32), gathered)


def kernel(x, w_in, sinks, w_proj_a, w_proj_b, w_out, ln1_g, ln1_b, router_w, router_b,
           w_gate, b_gate, w_up, b_up, w_down, b_down, ln2_g, ln2_b):
    bsz, seq, d = x.shape
    n_groups = BATCH_GROUPS if bsz % BATCH_GROUPS == 0 else 1
    gb = bsz // n_groups
    outs = []
    for grp in range(n_groups):
        h = x[grp * gb:(grp + 1) * gb].reshape(gb * seq, d)
        for l in range(w_in.shape[0]):
            y, y_packed, idx, gates = _token_mixer_ln1(h, w_in[l], sinks[l], w_proj_a[l], w_proj_b[l], w_out[l],
                                                       ln1_g[l], ln1_b[l], router_w[l], router_b[l], gb, seq)
            h = _moe_ln2(y, y_packed, idx, gates, w_gate[l], b_gate[l], w_up[l], b_up[l], w_down[l], b_down[l],
                         ln2_g[l], ln2_b[l])
        outs.append(h.reshape(gb, seq, d))
    return jnp.concatenate(outs, axis=0) if n_groups > 1 else outs[0]
```

```python
import functools

import jax
import jax.numpy as jnp
from jax import lax
from jax.experimental import pallas as pl
from jax.experimental.pallas import tpu as pltpu
from jax.experimental.pallas import tpu_sc as plsc

F32 = jnp.float32
BF16 = jnp.bfloat16
I32 = jnp.int32

HEAD_DIM = 64
DIL_GROUPS = ((128, 1), (512, 4), (2048, 16))
DIL_HEADS = 4
N_DIL = len(DIL_GROUPS)
DIL_OUT = DIL_HEADS * HEAD_DIM
DIL_WIDTH = N_DIL * DIL_OUT
SWA_Q_HEADS = 16
SWA_KV_HEADS = 2
SWA_REP = SWA_Q_HEADS // SWA_KV_HEADS
SWA_WINDOW = 128
SWA_Q_WIDTH = SWA_Q_HEADS * HEAD_DIM
SWA_KV_WIDTH = SWA_KV_HEADS * HEAD_DIM
N_ALIBI_HEADS = SWA_Q_HEADS + N_DIL * DIL_HEADS
ATTN_BLOCK = 128
N_EXPERTS = 32
TOP_K = 4
SWIGLU_LIMIT = 7.0
SWIGLU_ALPHA = 1.702
LN_EPS = 1e-5
DEPTH = 1
DEEPNORM_ALPHA = (2 * DEPTH) ** 0.25
NEG_INF = -1e30

LANES = 128
VMEM_LIMIT_BYTES = 56 * 1024 * 1024

A_QKV_W = 3 * DIL_WIDTH
B_Q_OFF = A_QKV_W
B_KV_OFF = B_Q_OFF + SWA_Q_WIDTH
GATE_OFF = B_KV_OFF + 2 * SWA_KV_WIDTH


def _cparams(*sem):
    return pltpu.CompilerParams(dimension_semantics=sem, vmem_limit_bytes=VMEM_LIMIT_BYTES)


def _pack_bf16_pairs(x):
    c = x.shape[1] // 2
    lo = lax.bitcast_convert_type(x[:, :c].astype(BF16).astype(F32), jnp.uint32)
    hi = lax.bitcast_convert_type(x[:, c:].astype(BF16).astype(F32), jnp.uint32)
    return (lo >> 16) | hi


def _unpack_bf16_pairs(w):
    lo = lax.bitcast_convert_type(w << 16, F32)
    hi = lax.bitcast_convert_type(w & jnp.uint32(0xFFFF0000), F32)
    return lo, hi


def _in_proj_kernel(x_ref, w_ref, a0_ref, a1_ref, a2_ref, hq_ref, hkv_ref, g_ref, *scratch, d_model, tm):
    xb = x_ref[...].astype(BF16)
    segments = ((hq_ref, B_Q_OFF, SWA_Q_WIDTH), (hkv_ref, B_KV_OFF, 2 * SWA_KV_WIDTH),
                (g_ref, GATE_OFF, 2 * d_model))
    for out_ref, col0, width in segments:
        for c in range(0, width, 512):
            w = min(512, width - c)
            r = jnp.dot(xb, w_ref[:, col0 + c:col0 + c + w], preferred_element_type=F32)
            out_ref[:, c:c + w] = r.astype(out_ref.dtype)
    for g, a_ref in enumerate((a0_ref, a1_ref, a2_ref)):
        dil = DIL_GROUPS[g][1]
        per = tm // dil
        for part in range(3):
            col0 = part * DIL_WIDTH + g * DIL_OUT
            res = jnp.dot(xb, w_ref[:, col0:col0 + DIL_OUT], preferred_element_type=F32)
            if dil == 1:
                a_ref[0, :, part * DIL_OUT:(part + 1) * DIL_OUT] = res.astype(a_ref.dtype)
                continue
            for half in range(DIL_OUT // LANES):
                stage = scratch[part * (DIL_OUT // LANES) + half]
                stage[...] = res[:, half * LANES:(half + 1) * LANES]
                c0 = part * DIL_OUT + half * LANES
                for r in range(dil):
                    a_ref[r, :, c0:c0 + LANES] = stage[pl.ds(r, per, stride=dil), :].astype(a_ref.dtype)


def _in_proj(x2, w_in_bf, bsz, seq, tm=512):
    n, d = x2.shape
    cols = w_in_bf.shape[1]
    tiles = seq // tm
    row = lambda i: (i, 0)
    dil_spec = lambda dil: pl.BlockSpec((None, dil, tm // dil, 3 * DIL_OUT), lambda i: (i // tiles, 0, i % tiles, 0))
    dil_shape = lambda dil: jax.ShapeDtypeStruct((bsz, dil, seq // dil, 3 * DIL_OUT), BF16)
    dils = [dil for _, dil in DIL_GROUPS]
    return pl.pallas_call(
        functools.partial(_in_proj_kernel, d_model=d, tm=tm),
        grid=(n // tm,),
        in_specs=[pl.BlockSpec((tm, d), row),
                  pl.BlockSpec((d, cols), lambda i: (0, 0), pipeline_mode=pl.Buffered(1))],
        out_specs=[dil_spec(dil) for dil in dils]
        + [pl.BlockSpec((tm, SWA_Q_WIDTH), row), pl.BlockSpec((tm, 2 * SWA_KV_WIDTH), row),
           pl.BlockSpec((tm, 2 * d), row)],
        out_shape=[dil_shape(dil) for dil in dils]
        + [jax.ShapeDtypeStruct((n, SWA_Q_WIDTH), BF16), jax.ShapeDtypeStruct((n, 2 * SWA_KV_WIDTH), BF16),
           jax.ShapeDtypeStruct((n, 2 * d), BF16)],
        scratch_shapes=[pltpu.VMEM((tm, LANES), F32)] * (3 * DIL_OUT // LANES),
        compiler_params=_cparams("parallel"),
        name="in_proj",
    )(x2, w_in_bf)


_NT = (((1,), (1,)), ((), ()))


def _per_head_column(ref, head0, n_heads, scale=1.0):
    head_of_row = lax.broadcasted_iota(I32, (n_heads * ATTN_BLOCK, 1), 0) // ATTN_BLOCK
    col = jnp.zeros((n_heads * ATTN_BLOCK, 1), F32)
    for h in range(n_heads):
        col = jnp.where(head_of_row == h, ref[head0 + h] * scale, col)
    return col


def _store_band_bias(bias_ref, slope_col, max_diff):
    rows = slope_col.shape[0]
    qi = lax.broadcasted_iota(I32, (rows, ATTN_BLOCK), 0) % ATTN_BLOCK
    kj = lax.broadcasted_iota(I32, (rows, ATTN_BLOCK), 1)
    diff_prev = qi - kj + ATTN_BLOCK
    diff_cur = qi - kj
    bias_ref[0] = jnp.where(diff_prev <= max_diff, -slope_col * diff_prev.astype(F32), NEG_INF)
    bias_ref[1] = jnp.where((diff_cur >= 0) & (diff_cur <= max_diff), -slope_col * diff_cur.astype(F32), NEG_INF)


def _band_softmax(s_prev, s_cur, sink_col):
    m = jnp.max(jnp.maximum(s_prev, s_cur), axis=-1, keepdims=True)
    if sink_col is not None:
        m = jnp.maximum(m, sink_col)
    p_prev = jnp.exp(s_prev - m)
    p_cur = jnp.exp(s_cur - m)
    denom = jnp.sum(p_prev + p_cur, axis=-1, keepdims=True)
    if sink_col is not None:
        denom = denom + jnp.exp(sink_col - m)
    return p_prev.astype(BF16), p_cur.astype(BF16), m, denom


def _dil_attn_kernel(slope_ref, q_ref, kc_ref, kp_ref, vc_ref, vp_ref, o_ref, lse_ref, bias_ref,
                     *, tq, max_diff, dist_scale):
    first = pl.program_id(2) == 0
    _store_band_bias(bias_ref, _per_head_column(slope_ref, 0, DIL_HEADS, dist_scale), max_diff)
    head_cols = [slice(h * HEAD_DIM, (h + 1) * HEAD_DIM) for h in range(DIL_HEADS)]
    for i in range(tq // ATTN_BLOCK):
        rows = slice(i * ATTN_BLOCK, (i + 1) * ATTN_BLOCK)
        prev_rows = slice((i - 1) * ATTN_BLOCK, i * ATTN_BLOCK)
        k_prev = [kp_ref[:, c] if i == 0 else kc_ref[prev_rows, c] for c in head_cols]
        v_prev = [vp_ref[:, c] if i == 0 else vc_ref[prev_rows, c] for c in head_cols]
        q = [q_ref[rows, c] * 0.125 for c in head_cols]
        s_prev = jnp.concatenate([lax.dot_general(q[h], k_prev[h], _NT, preferred_element_type=F32)
                                  for h in range(DIL_HEADS)], axis=0) + bias_ref[0]
        s_cur = jnp.concatenate([lax.dot_general(q[h], kc_ref[rows, head_cols[h]], _NT, preferred_element_type=F32)
                                 for h in range(DIL_HEADS)], axis=0) + bias_ref[1]
        if i == 0:
            s_prev = jnp.where(first, NEG_INF, s_prev)
        p_prev, p_cur, m, denom = _band_softmax(s_prev, s_cur, None)
        lse = m + jnp.log(denom)
        for h in range(DIL_HEADS):
            hr = slice(h * ATTN_BLOCK, (h + 1) * ATTN_BLOCK)
            o = (jnp.dot(p_prev[hr], v_prev[h], preferred_element_type=F32)
                 + jnp.dot(p_cur[hr], vc_ref[rows, head_cols[h]], preferred_element_type=F32)) / denom[hr]
            o_ref[rows, head_cols[h]] = o.astype(o_ref.dtype)
            lse_ref[rows, head_cols[h]] = jnp.broadcast_to(lse[hr], (ATTN_BLOCK, HEAD_DIM))


def _dil_attention(a_g, slopes_g, g):
    window, dil = DIL_GROUPS[g]
    bsz, _, sub_len, _ = a_g.shape
    tq = min(512, sub_len)
    nqb = tq // ATTN_BLOCK
    cur = lambda part: (lambda b, r, m: (b, r, m, part))
    prev = lambda part: (lambda b, r, m: (b, r, jnp.maximum(m * nqb - 1, 0), part))
    blk = lambda rows, imap: pl.BlockSpec((None, None, rows, DIL_OUT), imap)
    return pl.pallas_call(
        functools.partial(_dil_attn_kernel, tq=tq, max_diff=window // dil, dist_scale=float(dil)),
        grid=(bsz, dil, sub_len // tq),
        in_specs=[pl.BlockSpec(memory_space=pltpu.SMEM),
                  blk(tq, cur(0)), blk(tq, cur(1)), blk(ATTN_BLOCK, prev(1)),
                  blk(tq, cur(2)), blk(ATTN_BLOCK, prev(2))],
        out_specs=[blk(tq, cur(0)), blk(tq, cur(0))],
        out_shape=[jax.ShapeDtypeStruct((bsz, dil, sub_len, DIL_OUT), BF16),
                   jax.ShapeDtypeStruct((bsz, dil, sub_len, DIL_OUT), F32)],
        scratch_shapes=[pltpu.VMEM((2, DIL_HEADS * ATTN_BLOCK, ATTN_BLOCK), F32)],
        compiler_params=_cparams("parallel", "parallel", "arbitrary"),
        name=f"dil_attn_g{g}",
    )(slopes_g, a_g, a_g, a_g, a_g, a_g)


def _swa_attn_kernel(slope_ref, sink_ref, q_ref, kvc_ref, kvp_ref, o_ref, bias_ref, *, tq):
    first = pl.program_id(1) == 0
    for kvh in range(SWA_KV_HEADS):
        head0 = kvh * SWA_REP
        _store_band_bias(bias_ref, _per_head_column(slope_ref, head0, SWA_REP), SWA_WINDOW - 1)
        sink_col = _per_head_column(sink_ref, head0, SWA_REP)
        kcols = slice(kvh * HEAD_DIM, (kvh + 1) * HEAD_DIM)
        vcols = slice(SWA_KV_WIDTH + kvh * HEAD_DIM, SWA_KV_WIDTH + (kvh + 1) * HEAD_DIM)
        head_cols = [slice((head0 + r) * HEAD_DIM, (head0 + r + 1) * HEAD_DIM) for r in range(SWA_REP)]
        for i in range(tq // ATTN_BLOCK):
            rows = slice(i * ATTN_BLOCK, (i + 1) * ATTN_BLOCK)
            prev_rows = slice((i - 1) * ATTN_BLOCK, i * ATTN_BLOCK)
            k_prev = kvp_ref[:, kcols] if i == 0 else kvc_ref[prev_rows, kcols]
            v_prev = kvp_ref[:, vcols] if i == 0 else kvc_ref[prev_rows, vcols]
            q = jnp.concatenate([q_ref[rows, c] for c in head_cols], axis=0) * 0.125
            s_prev = lax.dot_general(q, k_prev, _NT, preferred_element_type=F32) + bias_ref[0]
            s_cur = lax.dot_general(q, kvc_ref[rows, kcols], _NT, preferred_element_type=F32) + bias_ref[1]
            if i == 0:
                s_prev = jnp.where(first, NEG_INF, s_prev)
            p_prev, p_cur, _, denom = _band_softmax(s_prev, s_cur, sink_col)
            o = (jnp.dot(p_prev, v_prev, preferred_element_type=F32)
                 + jnp.dot(p_cur, kvc_ref[rows, vcols], preferred_element_type=F32)) / denom
            for r in range(SWA_REP):
                o_ref[rows, head_cols[r]] = o[r * ATTN_BLOCK:(r + 1) * ATTN_BLOCK].astype(o_ref.dtype)


def _swa_attention(hq, hkv, slopes_b, sinks, bsz, seq, tq=512):
    nqb = tq // ATTN_BLOCK
    hq3 = hq.reshape(bsz, seq, SWA_Q_WIDTH)
    hkv3 = hkv.reshape(bsz, seq, 2 * SWA_KV_WIDTH)
    smem = pl.BlockSpec(memory_space=pltpu.SMEM)
    out = pl.pallas_call(
        functools.partial(_swa_attn_kernel, tq=tq),
        grid=(bsz, seq // tq),
        in_specs=[smem, smem,
                  pl.BlockSpec((None, tq, SWA_Q_WIDTH), lambda b, m: (b, m, 0)),
                  pl.BlockSpec((None, tq, 2 * SWA_KV_WIDTH), lambda b, m: (b, m, 0)),
                  pl.BlockSpec((None, ATTN_BLOCK, 2 * SWA_KV_WIDTH),
                               lambda b, m: (b, jnp.maximum(m * nqb - 1, 0), 0))],
        out_specs=pl.BlockSpec((None, tq, SWA_Q_WIDTH), lambda b, m: (b, m, 0)),
        out_shape=jax.ShapeDtypeStruct((bsz, seq, SWA_Q_WIDTH), BF16),
        scratch_shapes=[pltpu.VMEM((2, SWA_REP * ATTN_BLOCK, ATTN_BLOCK), F32)],
        compiler_params=_cparams("parallel", "arbitrary"),
        name="swa_attn",
    )(slopes_b, sinks, hq3, hkv3, hkv3)
    return out.reshape(bsz * seq, SWA_Q_WIDTH)


def _sigmoid(x):
    return 0.5 * (jnp.tanh(0.5 * x) + 1.0)


def _layer_norm(z, g, b):
    mu = jnp.mean(z, axis=-1, keepdims=True)
    zc = z - mu
    var = jnp.mean(zc * zc, axis=-1, keepdims=True)
    return zc * lax.rsqrt(var + LN_EPS) * g + b


def _mix_out_kernel(x_ref, o0_ref, o1_ref, o2_ref, l0_ref, l1_ref, l2_ref, ob_ref, g_ref,
                    wpa_ref, wpb_ref, wo_ref, lng_ref, lnb_ref, rw_ref, rb_ref,
                    y_ref, ypk_ref, idx_ref, gate_ref, *scratch, d_model):
    def natural(ref, stages):
        dil, per, _ = ref.shape
        if dil == 1:
            return ref[0].astype(F32)
        for half, stage in enumerate(stages):
            for r in range(dil):
                stage[pl.ds(r, per, stride=dil), :] = ref[r, :, half * LANES:(half + 1) * LANES].astype(F32)
        return jnp.concatenate([stage[...] for stage in stages], axis=1)

    o0, o1, o2 = natural(o0_ref, None), natural(o1_ref, scratch[0:2]), natural(o2_ref, scratch[2:4])
    l0, l1, l2 = natural(l0_ref, None), natural(l1_ref, scratch[4:6]), natural(l2_ref, scratch[6:8])
    lm = jnp.maximum(jnp.maximum(l0, l1), l2)
    e0, e1, e2 = jnp.exp(l0 - lm), jnp.exp(l1 - lm), jnp.exp(l2 - lm)
    esum = e0 + e1 + e2
    out_a = (e0 / esum) * o0 + (e1 / esum) * o1 + (e2 / esum) * o2
    pa = jnp.dot(out_a.astype(BF16), wpa_ref[...], preferred_element_type=F32)
    pb = jnp.dot(ob_ref[...], wpb_ref[...], preferred_element_type=F32)
    ga = _sigmoid(g_ref[:, :d_model].astype(F32))
    gb = _sigmoid(g_ref[:, d_model:].astype(F32))
    merged = ga * pa + gb * pb
    mix = jnp.dot(merged.astype(BF16), wo_ref[...], preferred_element_type=F32)
    y = _layer_norm(DEEPNORM_ALPHA * x_ref[...] + mix, lng_ref[...], lnb_ref[...])
    y_ref[...] = y
    ypk_ref[...] = _pack_bf16_pairs(y)

    logits = jnp.dot(y.astype(BF16), rw_ref[...], preferred_element_type=F32) + rb_ref[...]
    lane = lax.broadcasted_iota(I32, logits.shape, 1)
    logits = jnp.where(lane < N_EXPERTS, logits, -jnp.inf)
    idx_out = jnp.zeros(logits.shape, I32)
    val_out = jnp.full(logits.shape, -jnp.inf, F32)
    for k in range(TOP_K):
        top = jnp.max(logits, axis=-1, keepdims=True)
        top_idx = jnp.min(jnp.where(logits == top, lane, LANES), axis=-1, keepdims=True)
        idx_out = jnp.where(lane == k, top_idx, idx_out)
        val_out = jnp.where(lane == k, top, val_out)
        logits = jnp.where(lane == top_idx, -jnp.inf, logits)
    ev = jnp.exp(val_out - jnp.max(val_out, axis=-1, keepdims=True))
    gates = ev / jnp.sum(ev, axis=-1, keepdims=True)
    idx_ref[...] = idx_out[:, :TOP_K]
    gate_ref[...] = gates[:, :TOP_K]


def _mix_out(x2, o_g, lse_g, out_b, gates_h, wpa, wpb, wo, ln_g, ln_b, rw_pad, rb_pad, seq, tm=512):
    n, d = x2.shape
    tiles = seq // tm
    row = lambda i: (i, 0)
    const = lambda i: (0, 0)
    rb = lambda w: pl.BlockSpec((tm, w), row)
    full = lambda a: pl.BlockSpec(a.shape, const)
    dil_specs = [pl.BlockSpec((None, dil, tm // dil, DIL_OUT), lambda i: (i // tiles, 0, i % tiles, 0))
                 for _, dil in DIL_GROUPS]
    return pl.pallas_call(
        functools.partial(_mix_out_kernel, d_model=d),
        grid=(n // tm,),
        in_specs=[rb(d)] + dil_specs + dil_specs + [rb(SWA_Q_WIDTH), rb(2 * d),
                  full(wpa), full(wpb), full(wo), full(ln_g), full(ln_b), full(rw_pad), full(rb_pad)],
        out_specs=[rb(d), rb(d // 2), rb(TOP_K), rb(TOP_K)],
        out_shape=[jax.ShapeDtypeStruct((n, d), F32), jax.ShapeDtypeStruct((n, d // 2), jnp.uint32),
                   jax.ShapeDtypeStruct((n, TOP_K), I32), jax.ShapeDtypeStruct((n, TOP_K), F32)],
        scratch_shapes=[pltpu.VMEM((tm, LANES), F32)] * (4 * DIL_OUT // LANES),
        compiler_params=_cparams("parallel"),
        name="mix_out",
    )(x2, *o_g, *lse_g, out_b, gates_h, wpa, wpb, wo, ln_g, ln_b, rw_pad, rb_pad)


def _token_mixer_ln1(x2, w_in, sinks, w_proj_a, w_proj_b, w_out, ln1_g, ln1_b, router_w, router_b, bsz, seq):
    n, d = x2.shape
    heads = jnp.arange(1, N_ALIBI_HEADS + 1, dtype=F32)
    slopes = jnp.exp2(-8.0 * heads / N_ALIBI_HEADS)
    *a_g, hq, hkv, gates_h = _in_proj(x2, w_in.astype(BF16), bsz, seq)
    o_g, lse_g = [], []
    for g in range(N_DIL):
        sl = slopes[SWA_Q_HEADS + g * DIL_HEADS:SWA_Q_HEADS + (g + 1) * DIL_HEADS]
        o, lse = _dil_attention(a_g[g], sl, g)
        o_g.append(o)
        lse_g.append(lse)
    out_b = _swa_attention(hq, hkv, slopes[:SWA_Q_HEADS], sinks.astype(F32), bsz, seq)
    rw_pad = jnp.zeros((d, LANES), BF16).at[:, :N_EXPERTS].set(router_w.astype(BF16))
    rb_pad = jnp.zeros((1, LANES), F32).at[0, :N_EXPERTS].set(router_b.astype(F32))
    return _mix_out(x2, o_g, lse_g, out_b, gates_h, w_proj_a.astype(BF16), w_proj_b.astype(BF16),
                    w_out.astype(BF16), ln1_g.reshape(1, d).astype(F32), ln1_b.reshape(1, d).astype(F32),
                    rw_pad, rb_pad, seq)


def _route_kernel(idx_ref, dest_ref, cnt_ref, carry_ref, pstart_ref, *, expert_block):
    phase, i = pl.program_id(0), pl.program_id(1)
    idx = idx_ref[...]
    t = idx.shape[0]
    lane = lax.broadcasted_iota(I32, (t, LANES), 1)
    onehot = [lane == idx[:, k:k + 1] for k in range(TOP_K)]
    multi = sum(oh.astype(F32) for oh in onehot)
    tile_cnt = jnp.sum(multi, axis=0, keepdims=True).astype(I32)

    @pl.when((phase == 0) & (i == 0))
    def _():
        carry_ref[...] = jnp.zeros_like(carry_ref)

    @pl.when((phase == 1) & (i == 0))
    def _():
        counts = carry_ref[...]
        padded = (counts + (expert_block - 1)) & (-expert_block)
        lane8 = lax.broadcasted_iota(I32, counts.shape, 1)
        incl = padded
        for sh in (1, 2, 4, 8, 16):
            incl = incl + jnp.where(lane8 >= sh, pltpu.roll(incl, sh, axis=1), 0)
        pstart_ref[...] = incl - padded
        cnt_ref[...] = counts
        carry_ref[...] = jnp.zeros_like(carry_ref)

    @pl.when(phase == 1)
    def _():
        r = lax.broadcasted_iota(I32, (t, t), 0)
        c = lax.broadcasted_iota(I32, (t, t), 1)
        earlier = (c < r).astype(BF16)
        cum = jnp.dot(earlier, multi.astype(BF16), preferred_element_type=F32).astype(I32)
        base = cum + carry_ref[0:1, :] + pstart_ref[0:1, :]
        dest = jnp.zeros((t, LANES), I32)
        for k in range(TOP_K):
            d_k = jnp.sum(jnp.where(onehot[k], base, 0), axis=-1, keepdims=True)
            dest = jnp.where(lane == k, d_k, dest)
        dest_ref[...] = dest[:, :TOP_K]

    carry_ref[...] = carry_ref[...] + tile_cnt


def _route(idx, expert_block, tile=512):
    n = idx.shape[0]
    tile = min(tile, n)
    return pl.pallas_call(
        functools.partial(_route_kernel, expert_block=expert_block),
        grid=(2, n // tile),
        in_specs=[pl.BlockSpec((tile, TOP_K), lambda p, i: (i, 0))],
        out_specs=[pl.BlockSpec((tile, TOP_K), lambda p, i: (i * p, 0)),
                   pl.BlockSpec((8, LANES), lambda p, i: (0, 0))],
        out_shape=[jax.ShapeDtypeStruct((n, TOP_K), I32), jax.ShapeDtypeStruct((8, LANES), I32)],
        scratch_shapes=[pltpu.VMEM((8, LANES), I32), pltpu.VMEM((8, LANES), I32)],
        compiler_params=_cparams("arbitrary", "arbitrary"),
        name="route",
    )(idx)


def _sc_dispatch(dest_t, y, rows, window=64):
    n, d = y.shape
    info = plsc.get_sparse_core_info()
    n_cores = info.num_cores
    per_worker = n // (n_cores * info.num_subcores)
    mesh = plsc.VectorSubcoreMesh(core_axis_name="c", subcore_axis_name="s")

    @functools.partial(pl.kernel, mesh=mesh, out_type=jax.ShapeDtypeStruct((rows, d), y.dtype),
                       scratch_types=[pltpu.VMEM((TOP_K, window), I32), pltpu.VMEM((window, d), y.dtype)],
                       name="sc_dispatch")
    def body(dest_hbm, y_hbm, xs_hbm, idx_v, rows_v):
        base = (lax.axis_index("s") * n_cores + lax.axis_index("c")) * per_worker

        @pl.loop(0, per_worker // window)
        def _(c):
            t0 = base + c * window
            for k in range(TOP_K):
                pltpu.sync_copy(dest_hbm.at[k, pl.ds(t0, window)], idx_v.at[k])
            pltpu.sync_copy(y_hbm.at[pl.ds(t0, window)], rows_v)
            for k in range(TOP_K):
                pltpu.sync_copy(rows_v, xs_hbm.at[idx_v.at[k]])

    return body(dest_t, y)


def _expert_kernel(be_ref, nu_ref, nv_ref, x_ref, wg_ref, bg_ref, wu_ref, bu_ref, wd_ref, bd_ref, o_ref,
                   wg_bf, wu_bf, wd_bf):
    j = pl.program_id(0)
    used = j < nu_ref[0]
    new_expert = (j == 0) | (be_ref[j] != be_ref[jnp.maximum(j - 1, 0)])

    @pl.when(used & new_expert)
    def _():
        chunk = 128
        for src, dst in ((wg_ref, wg_bf), (wu_ref, wu_bf), (wd_ref, wd_bf)):
            for r in range(0, src.shape[0], chunk):
                dst[r:r + chunk, :] = src[r:r + chunk, :].astype(BF16)

    @pl.when(used)
    def _():
        row = lax.broadcasted_iota(I32, x_ref.shape, 0)
        x_lo, x_hi = _unpack_bf16_pairs(jnp.where(row < nv_ref[j], x_ref[...], jnp.uint32(0)))
        xb = jnp.concatenate([x_lo.astype(BF16), x_hi.astype(BF16)], axis=1)
        gt = jnp.dot(xb, wg_bf[...], preferred_element_type=F32) + bg_ref[...]
        up = jnp.dot(xb, wu_bf[...], preferred_element_type=F32) + bu_ref[...]
        gt = jnp.minimum(gt, SWIGLU_LIMIT)
        up = jnp.clip(up, -SWIGLU_LIMIT, SWIGLU_LIMIT)
        hdn = gt * _sigmoid(SWIGLU_ALPHA * gt) * (up + 1.0)
        o_ref[...] = _pack_bf16_pairs(jnp.dot(hdn.astype(BF16), wd_bf[...], preferred_element_type=F32)
                                      + bd_ref[...])

    @pl.when(jnp.logical_not(used))
    def _():
        o_ref[...] = jnp.zeros_like(o_ref)


def _experts(block_e, n_used, n_valid, xs, wg, bg, wu, bu, wd, bd, expert_block):
    rows, _ = xs.shape
    e, d, f = wg.shape
    xmap = lambda j, be, nu, nv: (jnp.minimum(j, nu[0] - 1), 0)
    wmap = lambda j, be, nu, nv: (be[j], 0, 0)
    grid_spec = pltpu.PrefetchScalarGridSpec(
        num_scalar_prefetch=3,
        grid=(rows // expert_block,),
        in_specs=[pl.BlockSpec((expert_block, d // 2), xmap),
                  pl.BlockSpec((None, d, f), wmap), pl.BlockSpec((None, 1, f), wmap),
                  pl.BlockSpec((None, d, f), wmap), pl.BlockSpec((None, 1, f), wmap),
                  pl.BlockSpec((None, f, d), wmap), pl.BlockSpec((None, 1, d), wmap)],
        out_specs=pl.BlockSpec((expert_block, d // 2), lambda j, be, nu, nv: (j, 0)),
        scratch_shapes=[pltpu.VMEM((d, f), BF16), pltpu.VMEM((d, f), BF16), pltpu.VMEM((f, d), BF16)],
    )
    return pl.pallas_call(
        _expert_kernel,
        grid_spec=grid_spec,
        out_shape=jax.ShapeDtypeStruct((rows, d // 2), jnp.uint32),
        compiler_params=_cparams("arbitrary"),
        name="experts",
    )(block_e, n_used, n_valid, xs, wg, bg.reshape(e, 1, f), wu, bu.reshape(e, 1, f), wd, bd.reshape(e, 1, d))


def _sc_gather(dest_t, ys, n, window=64):
    _, d = ys.shape
    info = plsc.get_sparse_core_info()
    n_cores = info.num_cores
    per_worker = n // (n_cores * info.num_subcores)
    mesh = plsc.VectorSubcoreMesh(core_axis_name="c", subcore_axis_name="s")

    n_chunks = per_worker // window
    n_buf = 2

    @functools.partial(pl.kernel, mesh=mesh, out_type=jax.ShapeDtypeStruct((TOP_K, n, d), ys.dtype),
                       scratch_types=[pltpu.VMEM((TOP_K, per_worker), I32)]
                       + [pltpu.VMEM((window, d), ys.dtype)] * n_buf + [pltpu.SemaphoreType.DMA] * (2 * n_buf),
                       name="sc_gather")
    def body(dest_hbm, ys_hbm, out_hbm, idx_v, buf0, buf1, gsem0, gsem1, ssem0, ssem1):
        bufs, gsem, ssem = (buf0, buf1), (gsem0, gsem1), (ssem0, ssem1)
        base = (lax.axis_index("s") * n_cores + lax.axis_index("c")) * per_worker
        for k in range(TOP_K):
            pltpu.sync_copy(dest_hbm.at[k, pl.ds(base, per_worker)], idx_v.at[k])

        def gather(c, k):
            return pltpu.make_async_copy(ys_hbm.at[idx_v.at[k, pl.ds(c * window, window)]], bufs[k % n_buf],
                                         gsem[k % n_buf])

        def store(c, k):
            return pltpu.make_async_copy(bufs[k % n_buf], out_hbm.at[k, pl.ds(base + c * window, window)],
                                         ssem[k % n_buf])

        gather(0, 0).start()

        @pl.loop(0, n_chunks)
        def _(c):
            for k in range(TOP_K):
                if k > 0:
                    store(c, k - 1).wait()
                    if k + 1 < TOP_K:
                        gather(c, k + 1).start()
                    else:
                        @pl.when(c + 1 < n_chunks)
                        def _():
                            gather(c + 1, 0).start()
                else:
                    @pl.when(c > 0)
                    def _():
                        store(c - 1, TOP_K - 1).wait()
                    gather(c, 1).start()
                gather(c, k).wait()
                store(c, k).start()

        store(n_chunks - 1, TOP_K - 1).wait()

    return body(dest_t, ys)


def _combine_dense_kernel(gate_ref, y_ref, lng_ref, lnb_ref, g_ref, *rest):
    o_ref = rest[-1]
    gates = gate_ref[...]
    ffn_lo = ffn_hi = None
    for k in range(TOP_K):
        lo, hi = _unpack_bf16_pairs(g_ref[k])
        gate = gates[:, k:k + 1]
        ffn_lo = gate * lo if k == 0 else ffn_lo + gate * lo
        ffn_hi = gate * hi if k == 0 else ffn_hi + gate * hi
    ffn = jnp.concatenate([ffn_lo, ffn_hi], axis=1)
    o_ref[...] = _layer_norm(DEEPNORM_ALPHA * y_ref[...] + ffn, lng_ref[...], lnb_ref[...])


def _combine_dense(gates, y, ln_g, ln_b, gathered, prev_out, chunk, tile=512):
    n, d = y.shape
    tc = gathered.shape[1]
    tile = min(tile, tc)
    tile0 = chunk * (tc // tile)
    row = lambda i: (tile0 + i, 0)
    const = lambda i: (0, 0)
    carried = [] if prev_out is None else [prev_out]
    return pl.pallas_call(
        _combine_dense_kernel,
        grid=(tc // tile,),
        in_specs=[pl.BlockSpec((tile, TOP_K), row), pl.BlockSpec((tile, d), row),
                  pl.BlockSpec((1, d), const), pl.BlockSpec((1, d), const),
                  pl.BlockSpec((TOP_K, tile, d // 2), lambda i: (0, i, 0))]
        + [pl.BlockSpec(memory_space=pl.ANY)] * len(carried),
        out_specs=pl.BlockSpec((tile, d), row),
        out_shape=jax.ShapeDtypeStruct((n, d), F32),
        input_output_aliases={5: 0} if carried else {},
        compiler_params=_cparams("parallel"),
        name="combine_dense",
    )(gates, y, ln_g, ln_b, gathered, *carried)


COMBINE_CHUNKS = 4
SC_MIN_TOKENS = 32 * 64
MOE_ROW_BLOCK = 512


def _moe_ln2(y, y_packed, idx, gates, w_gate, b_gate, w_up, b_up, w_down, b_down, ln2_g, ln2_b):
    n, d = y.shape
    n_blocks = n * TOP_K // MOE_ROW_BLOCK + N_EXPERTS
    rows = n_blocks * MOE_ROW_BLOCK
    dest, counts = _route(idx, MOE_ROW_BLOCK)
    cnt = counts[0, :N_EXPERTS]
    padded = (cnt + MOE_ROW_BLOCK - 1) // MOE_ROW_BLOCK * MOE_ROW_BLOCK
    p_end = jnp.cumsum(padded)
    block_row0 = jnp.arange(n_blocks, dtype=I32) * MOE_ROW_BLOCK
    block_e = jnp.minimum(jnp.sum(p_end[None, :] <= block_row0[:, None], axis=1), N_EXPERTS - 1).astype(I32)
    n_used = (p_end[-1:] // MOE_ROW_BLOCK).astype(I32)
    n_valid = jnp.clip(cnt[block_e] - (block_row0 - (p_end - padded)[block_e]), 0, MOE_ROW_BLOCK).astype(I32)
    dest_t = dest.T
    xs = _sc_dispatch(dest_t, y_packed, rows)
    ys = _experts(block_e, n_used, n_valid, xs, w_gate, b_gate.astype(F32), w_up, b_up.astype(F32),
                  w_down, b_down.astype(F32), MOE_ROW_BLOCK)
    n_chunks = COMBINE_CHUNKS if n % (COMBINE_CHUNKS * SC_MIN_TOKENS) == 0 else 1
    tc = n // n_chunks
    ln_g, ln_b = ln2_g.reshape(1, d).astype(F32), ln2_b.reshape(1, d).astype(F32)
    out = None
    for c in range(n_chunks):
        gathered = _sc_gather(dest_t[:, c * tc:(c + 1) * tc], ys, tc)
        out = _combine_dense(gates, y, ln_g, ln_b, gathered, out, c)
    return out


def kernel(x, w_in, sinks, w_proj_a, w_proj_b, w_out, ln1_g, ln1_b, router_w, router_b,
           w_gate, b_gate, w_up, b_up, w_down, b_down, ln2_g, ln2_b):
    bsz, seq, d = x.shape
    h = x.reshape(bsz * seq, d)
    for l in range(w_in.shape[0]):
        y, y_packed, idx, gates = _token_mixer_ln1(h, w_in[l], sinks[l], w_proj_a[l], w_proj_b[l], w_out[l],
                                                   ln1_g[l], ln1_b[l], router_w[l], router_b[l], bsz, seq)
        h = _moe_ln2(y, y_packed, idx, gates, w_gate[l], b_gate[l], w_up[l], b_up[l], w_down[l], b_down[l],
                     ln2_g[l], ln2_b[l])
    return h.reshape(bsz, seq, d)
```

```python
import functools

import jax
import jax.numpy as jnp
from jax import lax
from jax.experimental import pallas as pl
from jax.experimental.pallas import tpu as pltpu
from jax.experimental.pallas import tpu_sc as plsc

F32 = jnp.float32
BF16 = jnp.bfloat16
I32 = jnp.int32

HEAD_DIM = 64
DIL_GROUPS = ((128, 1), (512, 4), (2048, 16))
DIL_HEADS = 4
N_DIL = len(DIL_GROUPS)
DIL_OUT = DIL_HEADS * HEAD_DIM
DIL_WIDTH = N_DIL * DIL_OUT
SWA_Q_HEADS = 16
SWA_KV_HEADS = 2
SWA_REP = SWA_Q_HEADS // SWA_KV_HEADS
SWA_WINDOW = 128
SWA_Q_WIDTH = SWA_Q_HEADS * HEAD_DIM
SWA_KV_WIDTH = SWA_KV_HEADS * HEAD_DIM
N_ALIBI_HEADS = SWA_Q_HEADS + N_DIL * DIL_HEADS
ATTN_BLOCK = 128
N_EXPERTS = 32
TOP_K = 4
SWIGLU_LIMIT = 7.0
SWIGLU_ALPHA = 1.702
LN_EPS = 1e-5
DEPTH = 1
DEEPNORM_ALPHA = (2 * DEPTH) ** 0.25
NEG_INF = -1e30

LANES = 128
VMEM_LIMIT_BYTES = 56 * 1024 * 1024

A_QKV_W = 3 * DIL_WIDTH
B_Q_OFF = A_QKV_W
B_KV_OFF = B_Q_OFF + SWA_Q_WIDTH
GATE_OFF = B_KV_OFF + 2 * SWA_KV_WIDTH


def _cparams(*sem):
    return pltpu.CompilerParams(dimension_semantics=sem, vmem_limit_bytes=VMEM_LIMIT_BYTES)


def _pack_bf16_pairs(x):
    c = x.shape[1] // 2
    lo = lax.bitcast_convert_type(x[:, :c].astype(BF16).astype(F32), jnp.uint32)
    hi = lax.bitcast_convert_type(x[:, c:].astype(BF16).astype(F32), jnp.uint32)
    return (lo >> 16) | hi


def _unpack_bf16_pairs(w):
    lo = lax.bitcast_convert_type(w << 16, F32)
    hi = lax.bitcast_convert_type(w & jnp.uint32(0xFFFF0000), F32)
    return lo, hi


def _in_proj_kernel(x_ref, w_ref, a0_ref, a1_ref, a2_ref, hq_ref, hkv_ref, g_ref, *scratch, d_model, tm):
    xb = x_ref[...].astype(BF16)
    segments = ((hq_ref, B_Q_OFF, SWA_Q_WIDTH), (hkv_ref, B_KV_OFF, 2 * SWA_KV_WIDTH),
                (g_ref, GATE_OFF, 2 * d_model))
    for out_ref, col0, width in segments:
        for c in range(0, width, 512):
            w = min(512, width - c)
            r = jnp.dot(xb, w_ref[:, col0 + c:col0 + c + w], preferred_element_type=F32)
            out_ref[:, c:c + w] = r.astype(out_ref.dtype)
    for g, a_ref in enumerate((a0_ref, a1_ref, a2_ref)):
        dil = DIL_GROUPS[g][1]
        per = tm // dil
        for part in range(3):
            col0 = part * DIL_WIDTH + g * DIL_OUT
            res = jnp.dot(xb, w_ref[:, col0:col0 + DIL_OUT], preferred_element_type=F32)
            if dil == 1:
                a_ref[0, :, part * DIL_OUT:(part + 1) * DIL_OUT] = res.astype(a_ref.dtype)
                continue
            for half in range(DIL_OUT // LANES):
                stage = scratch[part * (DIL_OUT // LANES) + half]
                stage[...] = res[:, half * LANES:(half + 1) * LANES]
                c0 = part * DIL_OUT + half * LANES
                for r in range(dil):
                    a_ref[r, :, c0:c0 + LANES] = stage[pl.ds(r, per, stride=dil), :].astype(a_ref.dtype)


def _in_proj(x2, w_in_bf, bsz, seq, tm=512):
    n, d = x2.shape
    cols = w_in_bf.shape[1]
    tiles = seq // tm
    row = lambda i: (i, 0)
    dil_spec = lambda dil: pl.BlockSpec((None, dil, tm // dil, 3 * DIL_OUT), lambda i: (i // tiles, 0, i % tiles, 0))
    dil_shape = lambda dil: jax.ShapeDtypeStruct((bsz, dil, seq // dil, 3 * DIL_OUT), BF16)
    dils = [dil for _, dil in DIL_GROUPS]
    return pl.pallas_call(
        functools.partial(_in_proj_kernel, d_model=d, tm=tm),
        grid=(n // tm,),
        in_specs=[pl.BlockSpec((tm, d), row),
                  pl.BlockSpec((d, cols), lambda i: (0, 0), pipeline_mode=pl.Buffered(1))],
        out_specs=[dil_spec(dil) for dil in dils]
        + [pl.BlockSpec((tm, SWA_Q_WIDTH), row), pl.BlockSpec((tm, 2 * SWA_KV_WIDTH), row),
           pl.BlockSpec((tm, 2 * d), row)],
        out_shape=[dil_shape(dil) for dil in dils]
        + [jax.ShapeDtypeStruct((n, SWA_Q_WIDTH), BF16), jax.ShapeDtypeStruct((n, 2 * SWA_KV_WIDTH), BF16),
           jax.ShapeDtypeStruct((n, 2 * d), BF16)],
        scratch_shapes=[pltpu.VMEM((tm, LANES), F32)] * (3 * DIL_OUT // LANES),
        compiler_params=_cparams("parallel"),
        name="in_proj",
    )(x2, w_in_bf)


_NT = (((1,), (1,)), ((), ()))


def _per_head_column(ref, head0, n_heads, scale=1.0):
    head_of_row = lax.broadcasted_iota(I32, (n_heads * ATTN_BLOCK, 1), 0) // ATTN_BLOCK
    col = jnp.zeros((n_heads * ATTN_BLOCK, 1), F32)
    for h in range(n_heads):
        col = jnp.where(head_of_row == h, ref[head0 + h] * scale, col)
    return col


def _store_band_bias(bias_ref, slope_col, max_diff):
    rows = slope_col.shape[0]
    qi = lax.broadcasted_iota(I32, (rows, ATTN_BLOCK), 0) % ATTN_BLOCK
    kj = lax.broadcasted_iota(I32, (rows, ATTN_BLOCK), 1)
    diff_prev = qi - kj + ATTN_BLOCK
    diff_cur = qi - kj
    bias_ref[0] = jnp.where(diff_prev <= max_diff, -slope_col * diff_prev.astype(F32), NEG_INF)
    bias_ref[1] = jnp.where((diff_cur >= 0) & (diff_cur <= max_diff), -slope_col * diff_cur.astype(F32), NEG_INF)


def _band_probs(s_prev, s_cur, sink_col):
    m = jnp.max(jnp.maximum(s_prev, s_cur), axis=-1, keepdims=True)
    if sink_col is not None:
        m = jnp.maximum(m, sink_col)
    return jnp.exp(s_prev - m).astype(BF16), jnp.exp(s_cur - m).astype(BF16), m


def _pair_masks():
    low = lax.broadcasted_iota(I32, (ATTN_BLOCK, LANES), 1) < HEAD_DIM
    scale = HEAD_DIM ** -0.5
    return low, jnp.where(low, scale, 0.0).astype(BF16), jnp.where(low, 0.0, scale).astype(BF16)


def _values_and_ones(v):
    return jnp.concatenate([v, jnp.ones((v.shape[0], LANES), v.dtype)], axis=1)


def _dil_attn_kernel(slope_ref, q_ref, kc_ref, kp_ref, vc_ref, vp_ref, o_ref, lse_ref, bias_ref,
                     *, tq, max_diff, dist_scale):
    first = pl.program_id(2) == 0
    _store_band_bias(bias_ref, _per_head_column(slope_ref, 0, DIL_HEADS, dist_scale), max_diff)
    low, keep_low, keep_high = _pair_masks()
    pair_cols = [slice(p * LANES, (p + 1) * LANES) for p in range(DIL_HEADS // 2)]
    for i in range(tq // ATTN_BLOCK):
        rows = slice(i * ATTN_BLOCK, (i + 1) * ATTN_BLOCK)
        prev_rows = slice((i - 1) * ATTN_BLOCK, i * ATTN_BLOCK)
        s_prev, s_cur = [], []
        for c in pair_cols:
            k_prev = kp_ref[:, c] if i == 0 else kc_ref[prev_rows, c]
            for keep in (keep_low, keep_high):
                q = q_ref[rows, c] * keep
                s_prev.append(lax.dot_general(q, k_prev, _NT, preferred_element_type=F32))
                s_cur.append(lax.dot_general(q, kc_ref[rows, c], _NT, preferred_element_type=F32))
        s_prev = jnp.concatenate(s_prev, axis=0) + bias_ref[0]
        s_cur = jnp.concatenate(s_cur, axis=0) + bias_ref[1]
        if i == 0:
            s_prev = jnp.where(first, NEG_INF, s_prev)
        p_prev, p_cur, m = _band_probs(s_prev, s_cur, None)
        for p, c in enumerate(pair_cols):
            v_prev = _values_and_ones(vp_ref[:, c] if i == 0 else vc_ref[prev_rows, c])
            v_cur = _values_and_ones(vc_ref[rows, c])
            od, mh = [], []
            for h in (2 * p, 2 * p + 1):
                hr = slice(h * ATTN_BLOCK, (h + 1) * ATTN_BLOCK)
                od.append(jnp.dot(p_prev[hr], v_prev, preferred_element_type=F32)
                          + jnp.dot(p_cur[hr], v_cur, preferred_element_type=F32))
                mh.append(m[hr])
            denom = jnp.where(low, od[0][:, LANES:], od[1][:, LANES:])
            o_ref[rows, c] = (jnp.where(low, od[0][:, :LANES], od[1][:, :LANES]) / denom).astype(o_ref.dtype)
            lse_ref[rows, c] = jnp.where(low, mh[0], mh[1]) + jnp.log(denom)


def _dil_attention(a_g, slopes_g, g):
    window, dil = DIL_GROUPS[g]
    bsz, _, sub_len, _ = a_g.shape
    tq = min(512, sub_len)
    nqb = tq // ATTN_BLOCK
    cur = lambda part: (lambda b, r, m: (b, r, m, part))
    prev = lambda part: (lambda b, r, m: (b, r, jnp.maximum(m * nqb - 1, 0), part))
    blk = lambda rows, imap: pl.BlockSpec((None, None, rows, DIL_OUT), imap)
    return pl.pallas_call(
        functools.partial(_dil_attn_kernel, tq=tq, max_diff=window // dil, dist_scale=float(dil)),
        grid=(bsz, dil, sub_len // tq),
        in_specs=[pl.BlockSpec(memory_space=pltpu.SMEM),
                  blk(tq, cur(0)), blk(tq, cur(1)), blk(ATTN_BLOCK, prev(1)),
                  blk(tq, cur(2)), blk(ATTN_BLOCK, prev(2))],
        out_specs=[blk(tq, cur(0)), blk(tq, cur(0))],
        out_shape=[jax.ShapeDtypeStruct((bsz, dil, sub_len, DIL_OUT), BF16),
                   jax.ShapeDtypeStruct((bsz, dil, sub_len, DIL_OUT), F32)],
        scratch_shapes=[pltpu.VMEM((2, DIL_HEADS * ATTN_BLOCK, ATTN_BLOCK), F32)],
        compiler_params=_cparams("parallel", "parallel", "arbitrary"),
        name=f"dil_attn_g{g}",
    )(slopes_g, a_g, a_g, a_g, a_g, a_g)


def _swa_attn_kernel(slope_ref, sink_ref, q_ref, kvc_ref, kvp_ref, o_ref, bias_ref, *, tq):
    first = pl.program_id(1) == 0
    low, keep_low, keep_high = _pair_masks()

    def both_halves(ref, lane0):
        part = ref[:, lane0:lane0 + HEAD_DIM]
        return jnp.concatenate([part, part], axis=1)

    for kvh in range(SWA_KV_HEADS):
        head0 = kvh * SWA_REP
        _store_band_bias(bias_ref, _per_head_column(slope_ref, head0, SWA_REP), SWA_WINDOW - 1)
        sink_col = _per_head_column(sink_ref, head0, SWA_REP)
        k_cur_all, k_first = both_halves(kvc_ref, kvh * HEAD_DIM), both_halves(kvp_ref, kvh * HEAD_DIM)
        v_lane0 = SWA_KV_WIDTH + kvh * HEAD_DIM
        v_cur_all, v_first = both_halves(kvc_ref, v_lane0), both_halves(kvp_ref, v_lane0)
        pair_cols = [slice((head0 + 2 * p) * HEAD_DIM, (head0 + 2 * p + 2) * HEAD_DIM) for p in range(SWA_REP // 2)]
        for i in range(tq // ATTN_BLOCK):
            rows = slice(i * ATTN_BLOCK, (i + 1) * ATTN_BLOCK)
            prev_rows = slice((i - 1) * ATTN_BLOCK, i * ATTN_BLOCK)
            k_prev = k_first if i == 0 else k_cur_all[prev_rows]
            v_prev = v_first if i == 0 else v_cur_all[prev_rows]
            q = jnp.concatenate([q_ref[rows, c] * keep for c in pair_cols for keep in (keep_low, keep_high)], axis=0)
            s_prev = lax.dot_general(q, k_prev, _NT, preferred_element_type=F32) + bias_ref[0]
            s_cur = lax.dot_general(q, k_cur_all[rows], _NT, preferred_element_type=F32) + bias_ref[1]
            if i == 0:
                s_prev = jnp.where(first, NEG_INF, s_prev)
            p_prev, p_cur, m = _band_probs(s_prev, s_cur, sink_col)
            od = (jnp.dot(p_prev, _values_and_ones(v_prev), preferred_element_type=F32)
                  + jnp.dot(p_cur, _values_and_ones(v_cur_all[rows]), preferred_element_type=F32))
            o = od[:, :LANES] / (od[:, LANES:] + jnp.exp(sink_col - m))
            for p, c in enumerate(pair_cols):
                even = o[2 * p * ATTN_BLOCK:(2 * p + 1) * ATTN_BLOCK]
                odd = o[(2 * p + 1) * ATTN_BLOCK:(2 * p + 2) * ATTN_BLOCK]
                o_ref[rows, c] = jnp.where(low, even, odd).astype(o_ref.dtype)


def _swa_attention(hq, hkv, slopes_b, sinks, bsz, seq, tq=512):
    nqb = tq // ATTN_BLOCK
    hq3 = hq.reshape(bsz, seq, SWA_Q_WIDTH)
    hkv3 = hkv.reshape(bsz, seq, 2 * SWA_KV_WIDTH)
    smem = pl.BlockSpec(memory_space=pltpu.SMEM)
    out = pl.pallas_call(
        functools.partial(_swa_attn_kernel, tq=tq),
        grid=(bsz, seq // tq),
        in_specs=[smem, smem,
                  pl.BlockSpec((None, tq, SWA_Q_WIDTH), lambda b, m: (b, m, 0)),
                  pl.BlockSpec((None, tq, 2 * SWA_KV_WIDTH), lambda b, m: (b, m, 0)),
                  pl.BlockSpec((None, ATTN_BLOCK, 2 * SWA_KV_WIDTH),
                               lambda b, m: (b, jnp.maximum(m * nqb - 1, 0), 0))],
        out_specs=pl.BlockSpec((None, tq, SWA_Q_WIDTH), lambda b, m: (b, m, 0)),
        out_shape=jax.ShapeDtypeStruct((bsz, seq, SWA_Q_WIDTH), BF16),
        scratch_shapes=[pltpu.VMEM((2, SWA_REP * ATTN_BLOCK, ATTN_BLOCK), F32)],
        compiler_params=_cparams("parallel", "arbitrary"),
        name="swa_attn",
    )(slopes_b, sinks, hq3, hkv3, hkv3)
    return out.reshape(bsz * seq, SWA_Q_WIDTH)


def _sigmoid(x):
    return 0.5 * (jnp.tanh(0.5 * x) + 1.0)


def _layer_norm(z, g, b):
    mu = jnp.mean(z, axis=-1, keepdims=True)
    zc = z - mu
    var = jnp.mean(zc * zc, axis=-1, keepdims=True)
    return zc * lax.rsqrt(var + LN_EPS) * g + b


def _mix_out_kernel(x_ref, o0_ref, o1_ref, o2_ref, l0_ref, l1_ref, l2_ref, ob_ref, g_ref,
                    wpa_ref, wpb_ref, wo_ref, lng_ref, lnb_ref, rw_ref, rb_ref,
                    y_ref, ypk_ref, idx_ref, gate_ref, *scratch, d_model):
    def natural(ref, stages):
        dil, per, _ = ref.shape
        if dil == 1:
            return ref[0].astype(F32)
        for half, stage in enumerate(stages):
            for r in range(dil):
                stage[pl.ds(r, per, stride=dil), :] = ref[r, :, half * LANES:(half + 1) * LANES].astype(F32)
        return jnp.concatenate([stage[...] for stage in stages], axis=1)

    o0, o1, o2 = natural(o0_ref, None), natural(o1_ref, scratch[0:2]), natural(o2_ref, scratch[2:4])
    l0, l1, l2 = natural(l0_ref, None), natural(l1_ref, scratch[4:6]), natural(l2_ref, scratch[6:8])
    lm = jnp.maximum(jnp.maximum(l0, l1), l2)
    e0, e1, e2 = jnp.exp(l0 - lm), jnp.exp(l1 - lm), jnp.exp(l2 - lm)
    esum = e0 + e1 + e2
    out_a = (e0 / esum) * o0 + (e1 / esum) * o1 + (e2 / esum) * o2
    pa = jnp.dot(out_a.astype(BF16), wpa_ref[...], preferred_element_type=F32)
    pb = jnp.dot(ob_ref[...], wpb_ref[...], preferred_element_type=F32)
    ga = _sigmoid(g_ref[:, :d_model].astype(F32))
    gb = _sigmoid(g_ref[:, d_model:].astype(F32))
    merged = ga * pa + gb * pb
    mix = jnp.dot(merged.astype(BF16), wo_ref[...], preferred_element_type=F32)
    y = _layer_norm(DEEPNORM_ALPHA * x_ref[...] + mix, lng_ref[...], lnb_ref[...])
    y_ref[...] = y
    ypk_ref[...] = _pack_bf16_pairs(y)

    logits = jnp.dot(y.astype(BF16), rw_ref[...], preferred_element_type=F32) + rb_ref[...]
    lane = lax.broadcasted_iota(I32, logits.shape, 1)
    logits = jnp.where(lane < N_EXPERTS, logits, -jnp.inf)
    idx_out = jnp.zeros(logits.shape, I32)
    val_out = jnp.full(logits.shape, -jnp.inf, F32)
    for k in range(TOP_K):
        top = jnp.max(logits, axis=-1, keepdims=True)
        top_idx = jnp.min(jnp.where(logits == top, lane, LANES), axis=-1, keepdims=True)
        idx_out = jnp.where(lane == k, top_idx, idx_out)
        val_out = jnp.where(lane == k, top, val_out)
        logits = jnp.where(lane == top_idx, -jnp.inf, logits)
    ev = jnp.exp(val_out - jnp.max(val_out, axis=-1, keepdims=True))
    gates = ev / jnp.sum(ev, axis=-1, keepdims=True)
    idx_ref[...] = idx_out[:, :TOP_K]
    gate_ref[...] = gates[:, :TOP_K]


def _mix_out(x2, o_g, lse_g, out_b, gates_h, wpa, wpb, wo, ln_g, ln_b, rw_pad, rb_pad, seq, tm=512):
    n, d = x2.shape
    tiles = seq // tm
    row = lambda i: (i, 0)
    const = lambda i: (0, 0)
    rb = lambda w: pl.BlockSpec((tm, w), row)
    full = lambda a: pl.BlockSpec(a.shape, const)
    dil_specs = [pl.BlockSpec((None, dil, tm // dil, DIL_OUT), lambda i: (i // tiles, 0, i % tiles, 0))
                 for _, dil in DIL_GROUPS]
    return pl.pallas_call(
        functools.partial(_mix_out_kernel, d_model=d),
        grid=(n // tm,),
        in_specs=[rb(d)] + dil_specs + dil_specs + [rb(SWA_Q_WIDTH), rb(2 * d),
                  full(wpa), full(wpb), full(wo), full(ln_g), full(ln_b), full(rw_pad), full(rb_pad)],
        out_specs=[rb(d), rb(d // 2), rb(TOP_K), rb(TOP_K)],
        out_shape=[jax.ShapeDtypeStruct((n, d), F32), jax.ShapeDtypeStruct((n, d // 2), jnp.uint32),
                   jax.ShapeDtypeStruct((n, TOP_K), I32), jax.ShapeDtypeStruct((n, TOP_K), F32)],
        scratch_shapes=[pltpu.VMEM((tm, LANES), F32)] * (4 * DIL_OUT // LANES),
        compiler_params=_cparams("parallel"),
        name="mix_out",
    )(x2, *o_g, *lse_g, out_b, gates_h, wpa, wpb, wo, ln_g, ln_b, rw_pad, rb_pad)


def _token_mixer_ln1(x2, w_in, sinks, w_proj_a, w_proj_b, w_out, ln1_g, ln1_b, router_w, router_b, bsz, seq):
    n, d = x2.shape
    heads = jnp.arange(1, N_ALIBI_HEADS + 1, dtype=F32)
    slopes = jnp.exp2(-8.0 * heads / N_ALIBI_HEADS)
    *a_g, hq, hkv, gates_h = _in_proj(x2, w_in.astype(BF16), bsz, seq)
    o_g, lse_g = [], []
    for g in range(N_DIL):
        sl = slopes[SWA_Q_HEADS + g * DIL_HEADS:SWA_Q_HEADS + (g + 1) * DIL_HEADS]
        o, lse = _dil_attention(a_g[g], sl, g)
        o_g.append(o)
        lse_g.append(lse)
    out_b = _swa_attention(hq, hkv, slopes[:SWA_Q_HEADS], sinks.astype(F32), bsz, seq)
    rw_pad = jnp.zeros((d, LANES), BF16).at[:, :N_EXPERTS].set(router_w.astype(BF16))
    rb_pad = jnp.zeros((1, LANES), F32).at[0, :N_EXPERTS].set(router_b.astype(F32))
    return _mix_out(x2, o_g, lse_g, out_b, gates_h, w_proj_a.astype(BF16), w_proj_b.astype(BF16),
                    w_out.astype(BF16), ln1_g.reshape(1, d).astype(F32), ln1_b.reshape(1, d).astype(F32),
                    rw_pad, rb_pad, seq)


def _route_kernel(idx_ref, dest_ref, cnt_ref, carry_ref, pstart_ref, *, expert_block):
    phase, i = pl.program_id(0), pl.program_id(1)
    idx = idx_ref[...]
    t = idx.shape[0]
    lane = lax.broadcasted_iota(I32, (t, LANES), 1)
    onehot = [lane == idx[:, k:k + 1] for k in range(TOP_K)]
    multi = sum(oh.astype(F32) for oh in onehot)
    tile_cnt = jnp.sum(multi, axis=0, keepdims=True).astype(I32)

    @pl.when((phase == 0) & (i == 0))
    def _():
        carry_ref[...] = jnp.zeros_like(carry_ref)

    @pl.when((phase == 1) & (i == 0))
    def _():
        counts = carry_ref[...]
        padded = (counts + (expert_block - 1)) & (-expert_block)
        lane8 = lax.broadcasted_iota(I32, counts.shape, 1)
        incl = padded
        for sh in (1, 2, 4, 8, 16):
            incl = incl + jnp.where(lane8 >= sh, pltpu.roll(incl, sh, axis=1), 0)
        pstart_ref[...] = incl - padded
        cnt_ref[...] = counts
        carry_ref[...] = jnp.zeros_like(carry_ref)

    @pl.when(phase == 1)
    def _():
        r = lax.broadcasted_iota(I32, (t, t), 0)
        c = lax.broadcasted_iota(I32, (t, t), 1)
        earlier = (c < r).astype(BF16)
        cum = jnp.dot(earlier, multi.astype(BF16), preferred_element_type=F32).astype(I32)
        base = cum + carry_ref[0:1, :] + pstart_ref[0:1, :]
        dest = jnp.zeros((t, LANES), I32)
        for k in range(TOP_K):
            d_k = jnp.sum(jnp.where(onehot[k], base, 0), axis=-1, keepdims=True)
            dest = jnp.where(lane == k, d_k, dest)
        dest_ref[...] = dest[:, :TOP_K]

    carry_ref[...] = carry_ref[...] + tile_cnt


def _route(idx, expert_block, tile=512):
    n = idx.shape[0]
    tile = min(tile, n)
    return pl.pallas_call(
        functools.partial(_route_kernel, expert_block=expert_block),
        grid=(2, n // tile),
        in_specs=[pl.BlockSpec((tile, TOP_K), lambda p, i: (i, 0))],
        out_specs=[pl.BlockSpec((tile, TOP_K), lambda p, i: (i * p, 0)),
                   pl.BlockSpec((8, LANES), lambda p, i: (0, 0))],
        out_shape=[jax.ShapeDtypeStruct((n, TOP_K), I32), jax.ShapeDtypeStruct((8, LANES), I32)],
        scratch_shapes=[pltpu.VMEM((8, LANES), I32), pltpu.VMEM((8, LANES), I32)],
        compiler_params=_cparams("arbitrary", "arbitrary"),
        name="route",
    )(idx)


def _sc_dispatch(dest_t, y, rows, window=64):
    n, d = y.shape
    info = plsc.get_sparse_core_info()
    n_cores = info.num_cores
    per_worker = n // (n_cores * info.num_subcores)
    mesh = plsc.VectorSubcoreMesh(core_axis_name="c", subcore_axis_name="s")

    @functools.partial(pl.kernel, mesh=mesh, out_type=jax.ShapeDtypeStruct((rows, d), y.dtype),
                       scratch_types=[pltpu.VMEM((TOP_K, window), I32), pltpu.VMEM((window, d), y.dtype)],
                       name="sc_dispatch")
    def body(dest_hbm, y_hbm, xs_hbm, idx_v, rows_v):
        base = (lax.axis_index("s") * n_cores + lax.axis_index("c")) * per_worker

        @pl.loop(0, per_worker // window)
        def _(c):
            t0 = base + c * window
            for k in range(TOP_K):
                pltpu.sync_copy(dest_hbm.at[k, pl.ds(t0, window)], idx_v.at[k])
            pltpu.sync_copy(y_hbm.at[pl.ds(t0, window)], rows_v)
            for k in range(TOP_K):
                pltpu.sync_copy(rows_v, xs_hbm.at[idx_v.at[k]])

    return body(dest_t, y)


def _expert_kernel(be_ref, nu_ref, nv_ref, x_ref, wg_ref, bg_ref, wu_ref, bu_ref, wd_ref, bd_ref, o_ref,
                   wg_bf, wu_bf, wd_bf):
    j = pl.program_id(0)
    used = j < nu_ref[0]
    new_expert = (j == 0) | (be_ref[j] != be_ref[jnp.maximum(j - 1, 0)])

    @pl.when(used & new_expert)
    def _():
        chunk = 128
        for src, dst in ((wg_ref, wg_bf), (wu_ref, wu_bf), (wd_ref, wd_bf)):
            for r in range(0, src.shape[0], chunk):
                dst[r:r + chunk, :] = src[r:r + chunk, :].astype(BF16)

    @pl.when(used)
    def _():
        row = lax.broadcasted_iota(I32, x_ref.shape, 0)
        x_lo, x_hi = _unpack_bf16_pairs(jnp.where(row < nv_ref[j], x_ref[...], jnp.uint32(0)))
        xb = jnp.concatenate([x_lo.astype(BF16), x_hi.astype(BF16)], axis=1)
        gt = jnp.dot(xb, wg_bf[...], preferred_element_type=F32) + bg_ref[...]
        up = jnp.dot(xb, wu_bf[...], preferred_element_type=F32) + bu_ref[...]
        gt = jnp.minimum(gt, SWIGLU_LIMIT)
        up = jnp.clip(up, -SWIGLU_LIMIT, SWIGLU_LIMIT)
        hdn = gt * _sigmoid(SWIGLU_ALPHA * gt) * (up + 1.0)
        o_ref[...] = _pack_bf16_pairs(jnp.dot(hdn.astype(BF16), wd_bf[...], preferred_element_type=F32)
                                      + bd_ref[...])

    @pl.when(jnp.logical_not(used))
    def _():
        o_ref[...] = jnp.zeros_like(o_ref)


def _experts(block_e, n_used, n_valid, xs, wg, bg, wu, bu, wd, bd, expert_block):
    rows, _ = xs.shape
    e, d, f = wg.shape
    xmap = lambda j, be, nu, nv: (jnp.minimum(j, nu[0] - 1), 0)
    wmap = lambda j, be, nu, nv: (be[j], 0, 0)
    grid_spec = pltpu.PrefetchScalarGridSpec(
        num_scalar_prefetch=3,
        grid=(rows // expert_block,),
        in_specs=[pl.BlockSpec((expert_block, d // 2), xmap),
                  pl.BlockSpec((None, d, f), wmap), pl.BlockSpec((None, 1, f), wmap),
                  pl.BlockSpec((None, d, f), wmap), pl.BlockSpec((None, 1, f), wmap),
                  pl.BlockSpec((None, f, d), wmap), pl.BlockSpec((None, 1, d), wmap)],
        out_specs=pl.BlockSpec((expert_block, d // 2), lambda j, be, nu, nv: (j, 0)),
        scratch_shapes=[pltpu.VMEM((d, f), BF16), pltpu.VMEM((d, f), BF16), pltpu.VMEM((f, d), BF16)],
    )
    return pl.pallas_call(
        _expert_kernel,
        grid_spec=grid_spec,
        out_shape=jax.ShapeDtypeStruct((rows, d // 2), jnp.uint32),
        compiler_params=_cparams("arbitrary"),
        name="experts",
    )(block_e, n_used, n_valid, xs, wg, bg.reshape(e, 1, f), wu, bu.reshape(e, 1, f), wd, bd.reshape(e, 1, d))


def _sc_gather(dest_t, ys, n, window=64):
    _, d = ys.shape
    info = plsc.get_sparse_core_info()
    n_cores = info.num_cores
    per_worker = n // (n_cores * info.num_subcores)
    mesh = plsc.VectorSubcoreMesh(core_axis_name="c", subcore_axis_name="s")

    n_chunks = per_worker // window
    n_buf = 2

    @functools.partial(pl.kernel, mesh=mesh, out_type=jax.ShapeDtypeStruct((TOP_K, n, d), ys.dtype),
                       scratch_types=[pltpu.VMEM((TOP_K, per_worker), I32)]
                       + [pltpu.VMEM((window, d), ys.dtype)] * n_buf + [pltpu.SemaphoreType.DMA] * (2 * n_buf),
                       name="sc_gather")
    def body(dest_hbm, ys_hbm, out_hbm, idx_v, buf0, buf1, gsem0, gsem1, ssem0, ssem1):
        bufs, gsem, ssem = (buf0, buf1), (gsem0, gsem1), (ssem0, ssem1)
        base = (lax.axis_index("s") * n_cores + lax.axis_index("c")) * per_worker
        for k in range(TOP_K):
            pltpu.sync_copy(dest_hbm.at[k, pl.ds(base, per_worker)], idx_v.at[k])

        def gather(c, k):
            return pltpu.make_async_copy(ys_hbm.at[idx_v.at[k, pl.ds(c * window, window)]], bufs[k % n_buf],
                                         gsem[k % n_buf])

        def store(c, k):
            return pltpu.make_async_copy(bufs[k % n_buf], out_hbm.at[k, pl.ds(base + c * window, window)],
                                         ssem[k % n_buf])

        gather(0, 0).start()

        @pl.loop(0, n_chunks)
        def _(c):
            for k in range(TOP_K):
                if k > 0:
                    store(c, k - 1).wait()
                    if k + 1 < TOP_K:
                        gather(c, k + 1).start()
                    else:
                        @pl.when(c + 1 < n_chunks)
                        def _():
                            gather(c + 1, 0).start()
                else:
                    @pl.when(c > 0)
                    def _():
                        store(c - 1, TOP_K - 1).wait()
                    gather(c, 1).start()
                gather(c, k).wait()
                store(c, k).start()

        store(n_chunks - 1, TOP_K - 1).wait()

    return body(dest_t, ys)


def _combine_dense_kernel(gate_ref, y_ref, lng_ref, lnb_ref, g_ref, o_ref):
    gates = gate_ref[...]
    ffn_lo = ffn_hi = None
    for k in range(TOP_K):
        lo, hi = _unpack_bf16_pairs(g_ref[k])
        gate = gates[:, k:k + 1]
        ffn_lo = gate * lo if k == 0 else ffn_lo + gate * lo
        ffn_hi = gate * hi if k == 0 else ffn_hi + gate * hi
    ffn = jnp.concatenate([ffn_lo, ffn_hi], axis=1)
    o_ref[...] = _layer_norm(DEEPNORM_ALPHA * y_ref[...] + ffn, lng_ref[...], lnb_ref[...])


def _combine_dense(gates, y, ln_g, ln_b, gathered, tile=512):
    n, d = y.shape
    tile = min(tile, n)
    row = lambda i: (i, 0)
    const = lambda i: (0, 0)
    return pl.pallas_call(
        _combine_dense_kernel,
        grid=(n // tile,),
        in_specs=[pl.BlockSpec((tile, TOP_K), row), pl.BlockSpec((tile, d), row),
                  pl.BlockSpec((1, d), const), pl.BlockSpec((1, d), const),
                  pl.BlockSpec((TOP_K, tile, d // 2), lambda i: (0, i, 0))],
        out_specs=pl.BlockSpec((tile, d), row),
        out_shape=jax.ShapeDtypeStruct((n, d), F32),
        compiler_params=_cparams("parallel"),
        name="combine_dense",
    )(gates, y, ln_g, ln_b, gathered)


MOE_ROW_BLOCK = 512


def _moe_ln2(y, y_packed, idx, gates, w_gate, b_gate, w_up, b_up, w_down, b_down, ln2_g, ln2_b):
    n, d = y.shape
    n_blocks = n * TOP_K // MOE_ROW_BLOCK + N_EXPERTS
    rows = n_blocks * MOE_ROW_BLOCK
    dest, counts = _route(idx, MOE_ROW_BLOCK)
    cnt = counts[0, :N_EXPERTS]
    padded = (cnt + MOE_ROW_BLOCK - 1) // MOE_ROW_BLOCK * MOE_ROW_BLOCK
    p_end = jnp.cumsum(padded)
    block_row0 = jnp.arange(n_blocks, dtype=I32) * MOE_ROW_BLOCK
    block_e = jnp.minimum(jnp.sum(p_end[None, :] <= block_row0[:, None], axis=1), N_EXPERTS - 1).astype(I32)
    n_used = (p_end[-1:] // MOE_ROW_BLOCK).astype(I32)
    n_valid = jnp.clip(cnt[block_e] - (block_row0 - (p_end - padded)[block_e]), 0, MOE_ROW_BLOCK).astype(I32)
    dest_t = dest.T
    xs = _sc_dispatch(dest_t, y_packed, rows)
    ys = _experts(block_e, n_used, n_valid, xs, w_gate, b_gate.astype(F32), w_up, b_up.astype(F32),
                  w_down, b_down.astype(F32), MOE_ROW_BLOCK)
    gathered = _sc_gather(dest_t, ys, n)
    return _combine_dense(gates, y, ln2_g.reshape(1, d).astype(F32), ln2_b.reshape(1, d).astype(F32), gathered)


def kernel(x, w_in, sinks, w_proj_a, w_proj_b, w_out, ln1_g, ln1_b, router_w, router_b,
           w_gate, b_gate, w_up, b_up, w_down, b_down, ln2_g, ln2_b):
    bsz, seq, d = x.shape
    h = x.reshape(bsz * seq, d)
    for l in range(w_in.shape[0]):
        y, y_packed, idx, gates = _token_mixer_ln1(h, w_in[l], sinks[l], w_proj_a[l], w_proj_b[l], w_out[l],
                                                   ln1_g[l], ln1_b[l], router_w[l], router_b[l], bsz, seq)
        h = _moe_ln2(y, y_packed, idx, gates, w_gate[l], b_gate[l], w_up[l], b_up[l], w_down[l], b_down[l],
                     ln2_g[l], ln2_b[l])
    return h.reshape(bsz, seq, d)
```

```python
import functools

import jax
import jax.numpy as jnp
from jax import lax
from jax.experimental import pallas as pl
from jax.experimental.pallas import tpu as pltpu
from jax.experimental.pallas import tpu_sc as plsc

F32 = jnp.float32
BF16 = jnp.bfloat16
I32 = jnp.int32

HEAD_DIM = 64
DIL_GROUPS = ((128, 1), (512, 4), (2048, 16))
DIL_HEADS = 4
N_DIL = len(DIL_GROUPS)
DIL_OUT = DIL_HEADS * HEAD_DIM
DIL_WIDTH = N_DIL * DIL_OUT
SWA_Q_HEADS = 16
SWA_KV_HEADS = 2
SWA_REP = SWA_Q_HEADS // SWA_KV_HEADS
SWA_WINDOW = 128
SWA_Q_WIDTH = SWA_Q_HEADS * HEAD_DIM
SWA_KV_WIDTH = SWA_KV_HEADS * HEAD_DIM
N_ALIBI_HEADS = SWA_Q_HEADS + N_DIL * DIL_HEADS
ATTN_BLOCK = 128
N_EXPERTS = 32
TOP_K = 4
SWIGLU_LIMIT = 7.0
SWIGLU_ALPHA = 1.702
LN_EPS = 1e-5
DEPTH = 1
DEEPNORM_ALPHA = (2 * DEPTH) ** 0.25
NEG_INF = -1e30

LANES = 128
VMEM_LIMIT_BYTES = 56 * 1024 * 1024

A_QKV_W = 3 * DIL_WIDTH
B_Q_OFF = A_QKV_W
B_KV_OFF = B_Q_OFF + SWA_Q_WIDTH
GATE_OFF = B_KV_OFF + 2 * SWA_KV_WIDTH


def _cparams(*sem):
    return pltpu.CompilerParams(dimension_semantics=sem, vmem_limit_bytes=VMEM_LIMIT_BYTES)


def _pack_bf16_pairs(x):
    c = x.shape[1] // 2
    lo = lax.bitcast_convert_type(x[:, :c].astype(BF16).astype(F32), jnp.uint32)
    hi = lax.bitcast_convert_type(x[:, c:].astype(BF16).astype(F32), jnp.uint32)
    return (lo >> 16) | hi


def _unpack_bf16_pairs(w):
    lo = lax.bitcast_convert_type(w << 16, F32)
    hi = lax.bitcast_convert_type(w & jnp.uint32(0xFFFF0000), F32)
    return lo, hi


def _in_proj_kernel(x_ref, w_ref, a0_ref, a1_ref, a2_ref, hq_ref, hkv_ref, g_ref, *scratch, d_model, tm):
    xb = x_ref[...].astype(BF16)
    segments = ((hq_ref, B_Q_OFF, SWA_Q_WIDTH), (hkv_ref, B_KV_OFF, 2 * SWA_KV_WIDTH),
                (g_ref, GATE_OFF, 2 * d_model))
    for out_ref, col0, width in segments:
        for c in range(0, width, 512):
            w = min(512, width - c)
            r = jnp.dot(xb, w_ref[:, col0 + c:col0 + c + w], preferred_element_type=F32)
            out_ref[:, c:c + w] = r.astype(out_ref.dtype)
    for g, a_ref in enumerate((a0_ref, a1_ref, a2_ref)):
        dil = DIL_GROUPS[g][1]
        per = tm // dil
        for part in range(3):
            col0 = part * DIL_WIDTH + g * DIL_OUT
            res = jnp.dot(xb, w_ref[:, col0:col0 + DIL_OUT], preferred_element_type=F32)
            if dil == 1:
                a_ref[0, :, part * DIL_OUT:(part + 1) * DIL_OUT] = res.astype(a_ref.dtype)
                continue
            for half in range(DIL_OUT // LANES):
                stage = scratch[part * (DIL_OUT // LANES) + half]
                stage[...] = res[:, half * LANES:(half + 1) * LANES]
                c0 = part * DIL_OUT + half * LANES
                for r in range(dil):
                    a_ref[r, :, c0:c0 + LANES] = stage[pl.ds(r, per, stride=dil), :].astype(a_ref.dtype)


def _in_proj(x2, w_in_bf, bsz, seq, tm=512):
    n, d = x2.shape
    cols = w_in_bf.shape[1]
    tiles = seq // tm
    row = lambda i: (i, 0)
    dil_spec = lambda dil: pl.BlockSpec((None, dil, tm // dil, 3 * DIL_OUT), lambda i: (i // tiles, 0, i % tiles, 0))
    dil_shape = lambda dil: jax.ShapeDtypeStruct((bsz, dil, seq // dil, 3 * DIL_OUT), BF16)
    dils = [dil for _, dil in DIL_GROUPS]
    return pl.pallas_call(
        functools.partial(_in_proj_kernel, d_model=d, tm=tm),
        grid=(n // tm,),
        in_specs=[pl.BlockSpec((tm, d), row),
                  pl.BlockSpec((d, cols), lambda i: (0, 0), pipeline_mode=pl.Buffered(1))],
        out_specs=[dil_spec(dil) for dil in dils]
        + [pl.BlockSpec((tm, SWA_Q_WIDTH), row), pl.BlockSpec((tm, 2 * SWA_KV_WIDTH), row),
           pl.BlockSpec((tm, 2 * d), row)],
        out_shape=[dil_shape(dil) for dil in dils]
        + [jax.ShapeDtypeStruct((n, SWA_Q_WIDTH), BF16), jax.ShapeDtypeStruct((n, 2 * SWA_KV_WIDTH), BF16),
           jax.ShapeDtypeStruct((n, 2 * d), BF16)],
        scratch_shapes=[pltpu.VMEM((tm, LANES), F32)] * (3 * DIL_OUT // LANES),
        compiler_params=_cparams("parallel"),
        name="in_proj",
    )(x2, w_in_bf)


_NT = (((1,), (1,)), ((), ()))


def _per_head_column(ref, head0, n_heads, scale=1.0):
    head_of_row = lax.broadcasted_iota(I32, (n_heads * ATTN_BLOCK, 1), 0) // ATTN_BLOCK
    col = jnp.zeros((n_heads * ATTN_BLOCK, 1), F32)
    for h in range(n_heads):
        col = jnp.where(head_of_row == h, ref[head0 + h] * scale, col)
    return col


def _store_band_bias(bias_ref, slope_col, max_diff):
    rows = slope_col.shape[0]
    qi = lax.broadcasted_iota(I32, (rows, ATTN_BLOCK), 0) % ATTN_BLOCK
    kj = lax.broadcasted_iota(I32, (rows, ATTN_BLOCK), 1)
    diff_prev = qi - kj + ATTN_BLOCK
    diff_cur = qi - kj
    bias_ref[0] = jnp.where(diff_prev <= max_diff, -slope_col * diff_prev.astype(F32), NEG_INF)
    bias_ref[1] = jnp.where((diff_cur >= 0) & (diff_cur <= max_diff), -slope_col * diff_cur.astype(F32), NEG_INF)


def _band_probs(s_prev, s_cur, sink_col):
    m = jnp.max(jnp.maximum(s_prev, s_cur), axis=-1, keepdims=True)
    if sink_col is not None:
        m = jnp.maximum(m, sink_col)
    return jnp.exp(s_prev - m).astype(BF16), jnp.exp(s_cur - m).astype(BF16), m


def _pair_masks():
    low = lax.broadcasted_iota(I32, (ATTN_BLOCK, LANES), 1) < HEAD_DIM
    scale = HEAD_DIM ** -0.5
    return low, jnp.where(low, scale, 0.0).astype(BF16), jnp.where(low, 0.0, scale).astype(BF16)


def _values_and_ones(v):
    return jnp.concatenate([v, jnp.ones((v.shape[0], LANES), v.dtype)], axis=1)


def _dil_attn_kernel(slope_ref, q_ref, kc_ref, kp_ref, vc_ref, vp_ref, o_ref, lse_ref, bias_ref,
                     *, tq, max_diff, dist_scale):
    first = pl.program_id(2) == 0
    _store_band_bias(bias_ref, _per_head_column(slope_ref, 0, DIL_HEADS, dist_scale), max_diff)
    low, keep_low, keep_high = _pair_masks()
    pair_cols = [slice(p * LANES, (p + 1) * LANES) for p in range(DIL_HEADS // 2)]
    for i in range(tq // ATTN_BLOCK):
        rows = slice(i * ATTN_BLOCK, (i + 1) * ATTN_BLOCK)
        prev_rows = slice((i - 1) * ATTN_BLOCK, i * ATTN_BLOCK)
        s_prev, s_cur = [], []
        for c in pair_cols:
            k_prev = kp_ref[:, c] if i == 0 else kc_ref[prev_rows, c]
            for keep in (keep_low, keep_high):
                q = q_ref[rows, c] * keep
                s_prev.append(lax.dot_general(q, k_prev, _NT, preferred_element_type=F32))
                s_cur.append(lax.dot_general(q, kc_ref[rows, c], _NT, preferred_element_type=F32))
        s_prev = jnp.concatenate(s_prev, axis=0) + bias_ref[0]
        s_cur = jnp.concatenate(s_cur, axis=0) + bias_ref[1]
        if i == 0:
            s_prev = jnp.where(first, NEG_INF, s_prev)
        p_prev, p_cur, m = _band_probs(s_prev, s_cur, None)
        for p, c in enumerate(pair_cols):
            v_prev = _values_and_ones(vp_ref[:, c] if i == 0 else vc_ref[prev_rows, c])
            v_cur = _values_and_ones(vc_ref[rows, c])
            od, mh = [], []
            for h in (2 * p, 2 * p + 1):
                hr = slice(h * ATTN_BLOCK, (h + 1) * ATTN_BLOCK)
                od.append(jnp.dot(p_prev[hr], v_prev, preferred_element_type=F32)
                          + jnp.dot(p_cur[hr], v_cur, preferred_element_type=F32))
                mh.append(m[hr])
            denom = jnp.where(low, od[0][:, LANES:], od[1][:, LANES:])
            o_ref[rows, c] = (jnp.where(low, od[0][:, :LANES], od[1][:, :LANES]) / denom).astype(o_ref.dtype)
            lse_ref[rows, c] = jnp.where(low, mh[0], mh[1]) + jnp.log(denom)


def _dil_attention(a_g, slopes_g, g):
    window, dil = DIL_GROUPS[g]
    bsz, _, sub_len, _ = a_g.shape
    tq = min(512, sub_len)
    nqb = tq // ATTN_BLOCK
    cur = lambda part: (lambda b, r, m: (b, r, m, part))
    prev = lambda part: (lambda b, r, m: (b, r, jnp.maximum(m * nqb - 1, 0), part))
    blk = lambda rows, imap: pl.BlockSpec((None, None, rows, DIL_OUT), imap)
    return pl.pallas_call(
        functools.partial(_dil_attn_kernel, tq=tq, max_diff=window // dil, dist_scale=float(dil)),
        grid=(bsz, dil, sub_len // tq),
        in_specs=[pl.BlockSpec(memory_space=pltpu.SMEM),
                  blk(tq, cur(0)), blk(tq, cur(1)), blk(ATTN_BLOCK, prev(1)),
                  blk(tq, cur(2)), blk(ATTN_BLOCK, prev(2))],
        out_specs=[blk(tq, cur(0)), blk(tq, cur(0))],
        out_shape=[jax.ShapeDtypeStruct((bsz, dil, sub_len, DIL_OUT), BF16),
                   jax.ShapeDtypeStruct((bsz, dil, sub_len, DIL_OUT), F32)],
        scratch_shapes=[pltpu.VMEM((2, DIL_HEADS * ATTN_BLOCK, ATTN_BLOCK), F32)],
        compiler_params=_cparams("parallel", "parallel", "arbitrary"),
        name=f"dil_attn_g{g}",
    )(slopes_g, a_g, a_g, a_g, a_g, a_g)


def _swa_attn_kernel(slope_ref, sink_ref, q_ref, kvc_ref, kvp_ref, o_ref, bias_ref, *, tq):
    first = pl.program_id(1) == 0
    low, keep_low, keep_high = _pair_masks()

    def both_halves(ref, lane0):
        part = ref[:, lane0:lane0 + HEAD_DIM]
        return jnp.concatenate([part, part], axis=1)

    tile = (SWA_REP, ATTN_BLOCK, ATTN_BLOCK)
    stacked = (SWA_REP * ATTN_BLOCK, ATTN_BLOCK)
    qi = lax.broadcasted_iota(I32, (ATTN_BLOCK, ATTN_BLOCK), 0)
    kj = lax.broadcasted_iota(I32, (ATTN_BLOCK, ATTN_BLOCK), 1)
    from_prev = kj > qi
    diff = jnp.where(from_prev, qi - kj + ATTN_BLOCK, qi - kj).astype(F32)
    for kvh in range(SWA_KV_HEADS):
        head0 = kvh * SWA_REP
        for r in range(SWA_REP):
            bias_ref[r] = -slope_ref[head0 + r] * diff
        sink_col = _per_head_column(sink_ref, head0, SWA_REP)
        k_cur_all, k_first = both_halves(kvc_ref, kvh * HEAD_DIM), both_halves(kvp_ref, kvh * HEAD_DIM)
        v_lane0 = SWA_KV_WIDTH + kvh * HEAD_DIM
        v_cur_all, v_first = both_halves(kvc_ref, v_lane0), both_halves(kvp_ref, v_lane0)
        pair_cols = [slice((head0 + 2 * p) * HEAD_DIM, (head0 + 2 * p + 2) * HEAD_DIM) for p in range(SWA_REP // 2)]
        for i in range(tq // ATTN_BLOCK):
            rows = slice(i * ATTN_BLOCK, (i + 1) * ATTN_BLOCK)
            if i == 0:
                k_win = jnp.concatenate([k_first, k_cur_all[rows]], axis=0)
                v_win = jnp.concatenate([v_first, v_cur_all[rows]], axis=0)
            else:
                win = slice((i - 1) * ATTN_BLOCK, (i + 1) * ATTN_BLOCK)
                k_win, v_win = k_cur_all[win], v_cur_all[win]
            q = jnp.concatenate([q_ref[rows, c] * keep for c in pair_cols for keep in (keep_low, keep_high)], axis=0)
            s2 = lax.dot_general(q, k_win, _NT, preferred_element_type=F32)
            s = jnp.where(from_prev, s2[:, :ATTN_BLOCK].reshape(tile), s2[:, ATTN_BLOCK:].reshape(tile)) \
                + bias_ref[...]
            if i == 0:
                s = jnp.where(from_prev & first, NEG_INF, s)
            s = s.reshape(stacked)
            m = jnp.maximum(jnp.max(s, axis=-1, keepdims=True), sink_col)
            p = jnp.exp(s - m).reshape(tile)
            p2 = jnp.concatenate([jnp.where(from_prev, p, 0.0).reshape(stacked).astype(BF16),
                                  jnp.where(from_prev, 0.0, p).reshape(stacked).astype(BF16)], axis=1)
            od = jnp.dot(p2, _values_and_ones(v_win), preferred_element_type=F32)
            o = od[:, :LANES] / (od[:, LANES:] + jnp.exp(sink_col - m))
            for p, c in enumerate(pair_cols):
                even = o[2 * p * ATTN_BLOCK:(2 * p + 1) * ATTN_BLOCK]
                odd = o[(2 * p + 1) * ATTN_BLOCK:(2 * p + 2) * ATTN_BLOCK]
                o_ref[rows, c] = jnp.where(low, even, odd).astype(o_ref.dtype)


def _swa_attention(hq, hkv, slopes_b, sinks, bsz, seq, tq=512):
    nqb = tq // ATTN_BLOCK
    hq3 = hq.reshape(bsz, seq, SWA_Q_WIDTH)
    hkv3 = hkv.reshape(bsz, seq, 2 * SWA_KV_WIDTH)
    smem = pl.BlockSpec(memory_space=pltpu.SMEM)
    out = pl.pallas_call(
        functools.partial(_swa_attn_kernel, tq=tq),
        grid=(bsz, seq // tq),
        in_specs=[smem, smem,
                  pl.BlockSpec((None, tq, SWA_Q_WIDTH), lambda b, m: (b, m, 0)),
                  pl.BlockSpec((None, tq, 2 * SWA_KV_WIDTH), lambda b, m: (b, m, 0)),
                  pl.BlockSpec((None, ATTN_BLOCK, 2 * SWA_KV_WIDTH),
                               lambda b, m: (b, jnp.maximum(m * nqb - 1, 0), 0))],
        out_specs=pl.BlockSpec((None, tq, SWA_Q_WIDTH), lambda b, m: (b, m, 0)),
        out_shape=jax.ShapeDtypeStruct((bsz, seq, SWA_Q_WIDTH), BF16),
        scratch_shapes=[pltpu.VMEM((SWA_REP, ATTN_BLOCK, ATTN_BLOCK), F32)],
        compiler_params=_cparams("parallel", "arbitrary"),
        name="swa_attn",
    )(slopes_b, sinks, hq3, hkv3, hkv3)
    return out.reshape(bsz * seq, SWA_Q_WIDTH)


def _sigmoid(x):
    return 0.5 * (jnp.tanh(0.5 * x) + 1.0)


def _layer_norm(z, g, b):
    mu = jnp.mean(z, axis=-1, keepdims=True)
    zc = z - mu
    var = jnp.mean(zc * zc, axis=-1, keepdims=True)
    return zc * lax.rsqrt(var + LN_EPS) * g + b


def _mix_out_kernel(x_ref, o0_ref, o1_ref, o2_ref, l0_ref, l1_ref, l2_ref, ob_ref, g_ref,
                    wpa_ref, wpb_ref, wo_ref, lng_ref, lnb_ref, rw_ref, rb_ref,
                    y_ref, ypk_ref, idx_ref, gate_ref, *scratch, d_model):
    def natural(ref, stages):
        dil, per, _ = ref.shape
        if dil == 1:
            return ref[0].astype(F32)
        for half, stage in enumerate(stages):
            for r in range(dil):
                stage[pl.ds(r, per, stride=dil), :] = ref[r, :, half * LANES:(half + 1) * LANES].astype(F32)
        return jnp.concatenate([stage[...] for stage in stages], axis=1)

    o0, o1, o2 = natural(o0_ref, None), natural(o1_ref, scratch[0:2]), natural(o2_ref, scratch[2:4])
    l0, l1, l2 = natural(l0_ref, None), natural(l1_ref, scratch[4:6]), natural(l2_ref, scratch[6:8])
    lm = jnp.maximum(jnp.maximum(l0, l1), l2)
    e0, e1, e2 = jnp.exp(l0 - lm), jnp.exp(l1 - lm), jnp.exp(l2 - lm)
    esum = e0 + e1 + e2
    out_a = (e0 / esum) * o0 + (e1 / esum) * o1 + (e2 / esum) * o2
    pa = jnp.dot(out_a.astype(BF16), wpa_ref[...], preferred_element_type=F32)
    pb = jnp.dot(ob_ref[...], wpb_ref[...], preferred_element_type=F32)
    ga = _sigmoid(g_ref[:, :d_model].astype(F32))
    gb = _sigmoid(g_ref[:, d_model:].astype(F32))
    merged = ga * pa + gb * pb
    mix = jnp.dot(merged.astype(BF16), wo_ref[...], preferred_element_type=F32)
    y = _layer_norm(DEEPNORM_ALPHA * x_ref[...] + mix, lng_ref[...], lnb_ref[...])
    y_ref[...] = y
    ypk_ref[...] = _pack_bf16_pairs(y)

    logits = jnp.dot(y.astype(BF16), rw_ref[...], preferred_element_type=F32) + rb_ref[...]
    lane = lax.broadcasted_iota(I32, logits.shape, 1)
    logits = jnp.where(lane < N_EXPERTS, logits, -jnp.inf)
    idx_out = jnp.zeros(logits.shape, I32)
    val_out = jnp.full(logits.shape, -jnp.inf, F32)
    for k in range(TOP_K):
        top = jnp.max(logits, axis=-1, keepdims=True)
        top_idx = jnp.min(jnp.where(logits == top, lane, LANES), axis=-1, keepdims=True)
        idx_out = jnp.where(lane == k, top_idx, idx_out)
        val_out = jnp.where(lane == k, top, val_out)
        logits = jnp.where(lane == top_idx, -jnp.inf, logits)
    ev = jnp.exp(val_out - jnp.max(val_out, axis=-1, keepdims=True))
    gates = ev / jnp.sum(ev, axis=-1, keepdims=True)
    idx_ref[...] = idx_out[:, :TOP_K]
    gate_ref[...] = gates[:, :TOP_K]


def _mix_out(x2, o_g, lse_g, out_b, gates_h, wpa, wpb, wo, ln_g, ln_b, rw_pad, rb_pad, seq, tm=512):
    n, d = x2.shape
    tiles = seq // tm
    row = lambda i: (i, 0)
    const = lambda i: (0, 0)
    rb = lambda w: pl.BlockSpec((tm, w), row)
    full = lambda a: pl.BlockSpec(a.shape, const)
    dil_specs = [pl.BlockSpec((None, dil, tm // dil, DIL_OUT), lambda i: (i // tiles, 0, i % tiles, 0))
                 for _, dil in DIL_GROUPS]
    return pl.pallas_call(
        functools.partial(_mix_out_kernel, d_model=d),
        grid=(n // tm,),
        in_specs=[rb(d)] + dil_specs + dil_specs + [rb(SWA_Q_WIDTH), rb(2 * d),
                  full(wpa), full(wpb), full(wo), full(ln_g), full(ln_b), full(rw_pad), full(rb_pad)],
        out_specs=[rb(d), rb(d // 2), rb(TOP_K), rb(TOP_K)],
        out_shape=[jax.ShapeDtypeStruct((n, d), F32), jax.ShapeDtypeStruct((n, d // 2), jnp.uint32),
                   jax.ShapeDtypeStruct((n, TOP_K), I32), jax.ShapeDtypeStruct((n, TOP_K), F32)],
        scratch_shapes=[pltpu.VMEM((tm, LANES), F32)] * (4 * DIL_OUT // LANES),
        compiler_params=_cparams("parallel"),
        name="mix_out",
    )(x2, *o_g, *lse_g, out_b, gates_h, wpa, wpb, wo, ln_g, ln_b, rw_pad, rb_pad)


def _token_mixer_ln1(x2, w_in, sinks, w_proj_a, w_proj_b, w_out, ln1_g, ln1_b, router_w, router_b, bsz, seq):
    n, d = x2.shape
    heads = jnp.arange(1, N_ALIBI_HEADS + 1, dtype=F32)
    slopes = jnp.exp2(-8.0 * heads / N_ALIBI_HEADS)
    *a_g, hq, hkv, gates_h = _in_proj(x2, w_in.astype(BF16), bsz, seq)
    o_g, lse_g = [], []
    for g in range(N_DIL):
        sl = slopes[SWA_Q_HEADS + g * DIL_HEADS:SWA_Q_HEADS + (g + 1) * DIL_HEADS]
        o, lse = _dil_attention(a_g[g], sl, g)
        o_g.append(o)
        lse_g.append(lse)
    out_b = _swa_attention(hq, hkv, slopes[:SWA_Q_HEADS], sinks.astype(F32), bsz, seq)
    rw_pad = jnp.zeros((d, LANES), BF16).at[:, :N_EXPERTS].set(router_w.astype(BF16))
    rb_pad = jnp.zeros((1, LANES), F32).at[0, :N_EXPERTS].set(router_b.astype(F32))
    return _mix_out(x2, o_g, lse_g, out_b, gates_h, w_proj_a.astype(BF16), w_proj_b.astype(BF16),
                    w_out.astype(BF16), ln1_g.reshape(1, d).astype(F32), ln1_b.reshape(1, d).astype(F32),
                    rw_pad, rb_pad, seq)


def _route_kernel(idx_ref, dest_ref, cnt_ref, carry_ref, pstart_ref, *, expert_block):
    phase, i = pl.program_id(0), pl.program_id(1)
    idx = idx_ref[...]
    t = idx.shape[0]
    lane = lax.broadcasted_iota(I32, (t, LANES), 1)
    onehot = [lane == idx[:, k:k + 1] for k in range(TOP_K)]
    multi = sum(oh.astype(F32) for oh in onehot)
    tile_cnt = jnp.sum(multi, axis=0, keepdims=True).astype(I32)

    @pl.when((phase == 0) & (i == 0))
    def _():
        carry_ref[...] = jnp.zeros_like(carry_ref)

    @pl.when((phase == 1) & (i == 0))
    def _():
        counts = carry_ref[...]
        padded = (counts + (expert_block - 1)) & (-expert_block)
        lane8 = lax.broadcasted_iota(I32, counts.shape, 1)
        incl = padded
        for sh in (1, 2, 4, 8, 16):
            incl = incl + jnp.where(lane8 >= sh, pltpu.roll(incl, sh, axis=1), 0)
        pstart_ref[...] = incl - padded
        cnt_ref[...] = counts
        carry_ref[...] = jnp.zeros_like(carry_ref)

    @pl.when(phase == 1)
    def _():
        r = lax.broadcasted_iota(I32, (t, t), 0)
        c = lax.broadcasted_iota(I32, (t, t), 1)
        earlier = (c < r).astype(BF16)
        cum = jnp.dot(earlier, multi.astype(BF16), preferred_element_type=F32).astype(I32)
        base = cum + carry_ref[0:1, :] + pstart_ref[0:1, :]
        dest = jnp.zeros((t, LANES), I32)
        for k in range(TOP_K):
            d_k = jnp.sum(jnp.where(onehot[k], base, 0), axis=-1, keepdims=True)
            dest = jnp.where(lane == k, d_k, dest)
        dest_ref[...] = dest[:, :TOP_K]

    carry_ref[...] = carry_ref[...] + tile_cnt


def _route(idx, expert_block, tile=512):
    n = idx.shape[0]
    tile = min(tile, n)
    return pl.pallas_call(
        functools.partial(_route_kernel, expert_block=expert_block),
        grid=(2, n // tile),
        in_specs=[pl.BlockSpec((tile, TOP_K), lambda p, i: (i, 0))],
        out_specs=[pl.BlockSpec((tile, TOP_K), lambda p, i: (i * p, 0)),
                   pl.BlockSpec((8, LANES), lambda p, i: (0, 0))],
        out_shape=[jax.ShapeDtypeStruct((n, TOP_K), I32), jax.ShapeDtypeStruct((8, LANES), I32)],
        scratch_shapes=[pltpu.VMEM((8, LANES), I32), pltpu.VMEM((8, LANES), I32)],
        compiler_params=_cparams("arbitrary", "arbitrary"),
        name="route",
    )(idx)


def _sc_dispatch(dest_t, y, rows, window=64):
    n, d = y.shape
    info = plsc.get_sparse_core_info()
    n_cores = info.num_cores
    per_worker = n // (n_cores * info.num_subcores)
    mesh = plsc.VectorSubcoreMesh(core_axis_name="c", subcore_axis_name="s")

    @functools.partial(pl.kernel, mesh=mesh, out_type=jax.ShapeDtypeStruct((rows, d), y.dtype),
                       scratch_types=[pltpu.VMEM((TOP_K, window), I32), pltpu.VMEM((window, d), y.dtype)],
                       name="sc_dispatch")
    def body(dest_hbm, y_hbm, xs_hbm, idx_v, rows_v):
        base = (lax.axis_index("s") * n_cores + lax.axis_index("c")) * per_worker

        @pl.loop(0, per_worker // window)
        def _(c):
            t0 = base + c * window
            for k in range(TOP_K):
                pltpu.sync_copy(dest_hbm.at[k, pl.ds(t0, window)], idx_v.at[k])
            pltpu.sync_copy(y_hbm.at[pl.ds(t0, window)], rows_v)
            for k in range(TOP_K):
                pltpu.sync_copy(rows_v, xs_hbm.at[idx_v.at[k]])

    return body(dest_t, y)


def _expert_kernel(be_ref, nu_ref, nv_ref, x_ref, wg_ref, bg_ref, wu_ref, bu_ref, wd_ref, bd_ref, o_ref,
                   wg_bf, wu_bf, wd_bf):
    j = pl.program_id(0)
    used = j < nu_ref[0]
    new_expert = (j == 0) | (be_ref[j] != be_ref[jnp.maximum(j - 1, 0)])

    @pl.when(used & new_expert)
    def _():
        chunk = 128
        for src, dst in ((wg_ref, wg_bf), (wu_ref, wu_bf), (wd_ref, wd_bf)):
            for r in range(0, src.shape[0], chunk):
                dst[r:r + chunk, :] = src[r:r + chunk, :].astype(BF16)

    @pl.when(used)
    def _():
        row = lax.broadcasted_iota(I32, x_ref.shape, 0)
        x_lo, x_hi = _unpack_bf16_pairs(jnp.where(row < nv_ref[j], x_ref[...], jnp.uint32(0)))
        xb = jnp.concatenate([x_lo.astype(BF16), x_hi.astype(BF16)], axis=1)
        gt = jnp.dot(xb, wg_bf[...], preferred_element_type=F32) + bg_ref[...]
        up = jnp.dot(xb, wu_bf[...], preferred_element_type=F32) + bu_ref[...]
        gt = jnp.minimum(gt, SWIGLU_LIMIT)
        up = jnp.clip(up, -SWIGLU_LIMIT, SWIGLU_LIMIT)
        hdn = gt * _sigmoid(SWIGLU_ALPHA * gt) * (up + 1.0)
        o_ref[...] = _pack_bf16_pairs(jnp.dot(hdn.astype(BF16), wd_bf[...], preferred_element_type=F32)
                                      + bd_ref[...])

    @pl.when(jnp.logical_not(used))
    def _():
        o_ref[...] = jnp.zeros_like(o_ref)


def _experts(block_e, n_used, n_valid, xs, wg, bg, wu, bu, wd, bd, expert_block):
    rows, _ = xs.shape
    e, d, f = wg.shape
    xmap = lambda j, be, nu, nv: (jnp.minimum(j, nu[0] - 1), 0)
    wmap = lambda j, be, nu, nv: (be[j], 0, 0)
    grid_spec = pltpu.PrefetchScalarGridSpec(
        num_scalar_prefetch=3,
        grid=(rows // expert_block,),
        in_specs=[pl.BlockSpec((expert_block, d // 2), xmap),
                  pl.BlockSpec((None, d, f), wmap), pl.BlockSpec((None, 1, f), wmap),
                  pl.BlockSpec((None, d, f), wmap), pl.BlockSpec((None, 1, f), wmap),
                  pl.BlockSpec((None, f, d), wmap), pl.BlockSpec((None, 1, d), wmap)],
        out_specs=pl.BlockSpec((expert_block, d // 2), lambda j, be, nu, nv: (j, 0)),
        scratch_shapes=[pltpu.VMEM((d, f), BF16), pltpu.VMEM((d, f), BF16), pltpu.VMEM((f, d), BF16)],
    )
    return pl.pallas_call(
        _expert_kernel,
        grid_spec=grid_spec,
        out_shape=jax.ShapeDtypeStruct((rows, d // 2), jnp.uint32),
        compiler_params=_cparams("arbitrary"),
        name="experts",
    )(block_e, n_used, n_valid, xs, wg, bg.reshape(e, 1, f), wu, bu.reshape(e, 1, f), wd, bd.reshape(e, 1, d))


def _sc_gather(dest_t, ys, n, window=64):
    _, d = ys.shape
    info = plsc.get_sparse_core_info()
    n_cores = info.num_cores
    per_worker = n // (n_cores * info.num_subcores)
    mesh = plsc.VectorSubcoreMesh(core_axis_name="c", subcore_axis_name="s")

    n_chunks = per_worker // window
    n_buf = 2

    @functools.partial(pl.kernel, mesh=mesh, out_type=jax.ShapeDtypeStruct((TOP_K, n, d), ys.dtype),
                       scratch_types=[pltpu.VMEM((TOP_K, per_worker), I32)]
                       + [pltpu.VMEM((window, d), ys.dtype)] * n_buf + [pltpu.SemaphoreType.DMA] * (2 * n_buf),
                       name="sc_gather")
    def body(dest_hbm, ys_hbm, out_hbm, idx_v, buf0, buf1, gsem0, gsem1, ssem0, ssem1):
        bufs, gsem, ssem = (buf0, buf1), (gsem0, gsem1), (ssem0, ssem1)
        base = (lax.axis_index("s") * n_cores + lax.axis_index("c")) * per_worker
        for k in range(TOP_K):
            pltpu.sync_copy(dest_hbm.at[k, pl.ds(base, per_worker)], idx_v.at[k])

        def gather(c, k):
            return pltpu.make_async_copy(ys_hbm.at[idx_v.at[k, pl.ds(c * window, window)]], bufs[k % n_buf],
                                         gsem[k % n_buf])

        def store(c, k):
            return pltpu.make_async_copy(bufs[k % n_buf], out_hbm.at[k, pl.ds(base + c * window, window)],
                                         ssem[k % n_buf])

        gather(0, 0).start()

        @pl.loop(0, n_chunks)
        def _(c):
            for k in range(TOP_K):
                if k > 0:
                    store(c, k - 1).wait()
                    if k + 1 < TOP_K:
                        gather(c, k + 1).start()
                    else:
                        @pl.when(c + 1 < n_chunks)
                        def _():
                            gather(c + 1, 0).start()
                else:
                    @pl.when(c > 0)
                    def _():
                        store(c - 1, TOP_K - 1).wait()
                    gather(c, 1).start()
                gather(c, k).wait()
                store(c, k).start()

        store(n_chunks - 1, TOP_K - 1).wait()

    return body(dest_t, ys)


def _combine_dense_kernel(gate_ref, y_ref, lng_ref, lnb_ref, g_ref, o_ref):
    gates = gate_ref[...]
    ffn_lo = ffn_hi = None
    for k in range(TOP_K):
        lo, hi = _unpack_bf16_pairs(g_ref[k])
        gate = gates[:, k:k + 1]
        ffn_lo = gate * lo if k == 0 else ffn_lo + gate * lo
        ffn_hi = gate * hi if k == 0 else ffn_hi + gate * hi
    ffn = jnp.concatenate([ffn_lo, ffn_hi], axis=1)
    o_ref[...] = _layer_norm(DEEPNORM_ALPHA * y_ref[...] + ffn, lng_ref[...], lnb_ref[...])


def _combine_dense(gates, y, ln_g, ln_b, gathered, tile=512):
    n, d = y.shape
    tile = min(tile, n)
    row = lambda i: (i, 0)
    const = lambda i: (0, 0)
    return pl.pallas_call(
        _combine_dense_kernel,
        grid=(n // tile,),
        in_specs=[pl.BlockSpec((tile, TOP_K), row), pl.BlockSpec((tile, d), row),
                  pl.BlockSpec((1, d), const), pl.BlockSpec((1, d), const),
                  pl.BlockSpec((TOP_K, tile, d // 2), lambda i: (0, i, 0))],
        out_specs=pl.BlockSpec((tile, d), row),
        out_shape=jax.ShapeDtypeStruct((n, d), F32),
        compiler_params=_cparams("parallel"),
        name="combine_dense",
    )(gates, y, ln_g, ln_b, gathered)


MOE_ROW_BLOCK = 512


def _moe_ln2(y, y_packed, idx, gates, w_gate, b_gate, w_up, b_up, w_down, b_down, ln2_g, ln2_b):
    n, d = y.shape
    n_blocks = n * TOP_K // MOE_ROW_BLOCK + N_EXPERTS
    rows = n_blocks * MOE_ROW_BLOCK
    dest, counts = _route(idx, MOE_ROW_BLOCK)
    cnt = counts[0, :N_EXPERTS]
    padded = (cnt + MOE_ROW_BLOCK - 1) // MOE_ROW_BLOCK * MOE_ROW_BLOCK
    p_end = jnp.cumsum(padded)
    block_row0 = jnp.arange(n_blocks, dtype=I32) * MOE_ROW_BLOCK
    block_e = jnp.minimum(jnp.sum(p_end[None, :] <= block_row0[:, None], axis=1), N_EXPERTS - 1).astype(I32)
    n_used = (p_end[-1:] // MOE_ROW_BLOCK).astype(I32)
    n_valid = jnp.clip(cnt[block_e] - (block_row0 - (p_end - padded)[block_e]), 0, MOE_ROW_BLOCK).astype(I32)
    dest_t = dest.T
    xs = _sc_dispatch(dest_t, y_packed, rows)
    ys = _experts(block_e, n_used, n_valid, xs, w_gate, b_gate.astype(F32), w_up, b_up.astype(F32),
                  w_down, b_down.astype(F32), MOE_ROW_BLOCK)
    gathered = _sc_gather(dest_t, ys, n)
    return _combine_dense(gates, y, ln2_g.reshape(1, d).astype(F32), ln2_b.reshape(1, d).astype(F32), gathered)


def kernel(x, w_in, sinks, w_proj_a, w_proj_b, w_out, ln1_g, ln1_b, router_w, router_b,
           w_gate, b_gate, w_up, b_up, w_down, b_down, ln2_g, ln2_b):
    bsz, seq, d = x.shape
    h = x.reshape(bsz * seq, d)
    for l in range(w_in.shape[0]):
        y, y_packed, idx, gates = _token_mixer_ln1(h, w_in[l], sinks[l], w_proj_a[l], w_proj_b[l], w_out[l],
                                                   ln1_g[l], ln1_b[l], router_w[l], router_b[l], bsz, seq)
        h = _moe_ln2(y, y_packed, idx, gates, w_gate[l], b_gate[l], w_up[l], b_up[l], w_down[l], b_down[l],
                     ln2_g[l], ln2_b[l])
    return h.reshape(bsz, seq, d)
```

```python
import functools

import jax
import jax.numpy as jnp
from jax import lax
from jax.experimental import pallas as pl
from jax.experimental.pallas import tpu as pltpu
from jax.experimental.pallas import tpu_sc as plsc

F32 = jnp.float32
BF16 = jnp.bfloat16
I32 = jnp.int32

HEAD_DIM = 64
DIL_GROUPS = ((128, 1), (512, 4), (2048, 16))
DIL_HEADS = 4
N_DIL = len(DIL_GROUPS)
DIL_OUT = DIL_HEADS * HEAD_DIM
DIL_WIDTH = N_DIL * DIL_OUT
SWA_Q_HEADS = 16
SWA_KV_HEADS = 2
SWA_REP = SWA_Q_HEADS // SWA_KV_HEADS
SWA_WINDOW = 128
SWA_Q_WIDTH = SWA_Q_HEADS * HEAD_DIM
SWA_KV_WIDTH = SWA_KV_HEADS * HEAD_DIM
N_ALIBI_HEADS = SWA_Q_HEADS + N_DIL * DIL_HEADS
ATTN_BLOCK = 128
N_EXPERTS = 32
TOP_K = 4
SWIGLU_LIMIT = 7.0
SWIGLU_ALPHA = 1.702
LN_EPS = 1e-5
DEPTH = 1
DEEPNORM_ALPHA = (2 * DEPTH) ** 0.25
NEG_INF = -1e30

LANES = 128
VMEM_LIMIT_BYTES = 56 * 1024 * 1024

A_QKV_W = 3 * DIL_WIDTH
B_Q_OFF = A_QKV_W
B_KV_OFF = B_Q_OFF + SWA_Q_WIDTH
GATE_OFF = B_KV_OFF + 2 * SWA_KV_WIDTH


def _cparams(*sem):
    return pltpu.CompilerParams(dimension_semantics=sem, vmem_limit_bytes=VMEM_LIMIT_BYTES)


def _pack_bf16_pairs(x):
    c = x.shape[1] // 2
    lo = lax.bitcast_convert_type(x[:, :c].astype(BF16).astype(F32), jnp.uint32)
    hi = lax.bitcast_convert_type(x[:, c:].astype(BF16).astype(F32), jnp.uint32)
    return (lo >> 16) | hi


def _unpack_bf16_pairs(w):
    lo = lax.bitcast_convert_type(w << 16, F32)
    hi = lax.bitcast_convert_type(w & jnp.uint32(0xFFFF0000), F32)
    return lo, hi


def _in_proj_kernel(x_ref, w_ref, a0_ref, a1_ref, a2_ref, hq_ref, hkv_ref, g_ref, *scratch, d_model, tm):
    xb = x_ref[...].astype(BF16)
    segments = ((hq_ref, B_Q_OFF, SWA_Q_WIDTH), (hkv_ref, B_KV_OFF, 2 * SWA_KV_WIDTH),
                (g_ref, GATE_OFF, 2 * d_model))
    for out_ref, col0, width in segments:
        for c in range(0, width, 512):
            w = min(512, width - c)
            r = jnp.dot(xb, w_ref[:, col0 + c:col0 + c + w], preferred_element_type=F32)
            out_ref[:, c:c + w] = r.astype(out_ref.dtype)
    for g, a_ref in enumerate((a0_ref, a1_ref, a2_ref)):
        dil = DIL_GROUPS[g][1]
        per = tm // dil
        for part in range(3):
            col0 = part * DIL_WIDTH + g * DIL_OUT
            res = jnp.dot(xb, w_ref[:, col0:col0 + DIL_OUT], preferred_element_type=F32)
            if dil == 1:
                a_ref[0, :, part * DIL_OUT:(part + 1) * DIL_OUT] = res.astype(a_ref.dtype)
                continue
            for half in range(DIL_OUT // LANES):
                stage = scratch[part * (DIL_OUT // LANES) + half]
                stage[...] = res[:, half * LANES:(half + 1) * LANES]
                c0 = part * DIL_OUT + half * LANES
                for r in range(dil):
                    a_ref[r, :, c0:c0 + LANES] = stage[pl.ds(r, per, stride=dil), :].astype(a_ref.dtype)


def _in_proj(x2, w_in_bf, bsz, seq, tm=512):
    n, d = x2.shape
    cols = w_in_bf.shape[1]
    tiles = seq // tm
    row = lambda i: (i, 0)
    dil_spec = lambda dil: pl.BlockSpec((None, dil, tm // dil, 3 * DIL_OUT), lambda i: (i // tiles, 0, i % tiles, 0))
    dil_shape = lambda dil: jax.ShapeDtypeStruct((bsz, dil, seq // dil, 3 * DIL_OUT), BF16)
    dils = [dil for _, dil in DIL_GROUPS]
    return pl.pallas_call(
        functools.partial(_in_proj_kernel, d_model=d, tm=tm),
        grid=(n // tm,),
        in_specs=[pl.BlockSpec((tm, d), row),
                  pl.BlockSpec((d, cols), lambda i: (0, 0), pipeline_mode=pl.Buffered(1))],
        out_specs=[dil_spec(dil) for dil in dils]
        + [pl.BlockSpec((tm, SWA_Q_WIDTH), row), pl.BlockSpec((tm, 2 * SWA_KV_WIDTH), row),
           pl.BlockSpec((tm, 2 * d), row)],
        out_shape=[dil_shape(dil) for dil in dils]
        + [jax.ShapeDtypeStruct((n, SWA_Q_WIDTH), BF16), jax.ShapeDtypeStruct((n, 2 * SWA_KV_WIDTH), BF16),
           jax.ShapeDtypeStruct((n, 2 * d), BF16)],
        scratch_shapes=[pltpu.VMEM((tm, LANES), F32)] * (3 * DIL_OUT // LANES),
        compiler_params=_cparams("parallel"),
        name="in_proj",
    )(x2, w_in_bf)


_NT = (((1,), (1,)), ((), ()))


def _per_head_column(ref, head0, n_heads, scale=1.0):
    head_of_row = lax.broadcasted_iota(I32, (n_heads * ATTN_BLOCK, 1), 0) // ATTN_BLOCK
    col = jnp.zeros((n_heads * ATTN_BLOCK, 1), F32)
    for h in range(n_heads):
        col = jnp.where(head_of_row == h, ref[head0 + h] * scale, col)
    return col


def _store_band_bias(bias_ref, slope_col, max_diff):
    rows = slope_col.shape[0]
    qi = lax.broadcasted_iota(I32, (rows, ATTN_BLOCK), 0) % ATTN_BLOCK
    kj = lax.broadcasted_iota(I32, (rows, ATTN_BLOCK), 1)
    diff_prev = qi - kj + ATTN_BLOCK
    diff_cur = qi - kj
    bias_ref[0] = jnp.where(diff_prev <= max_diff, -slope_col * diff_prev.astype(F32), NEG_INF)
    bias_ref[1] = jnp.where((diff_cur >= 0) & (diff_cur <= max_diff), -slope_col * diff_cur.astype(F32), NEG_INF)


def _band_probs(s_prev, s_cur, sink_col):
    m = jnp.max(jnp.maximum(s_prev, s_cur), axis=-1, keepdims=True)
    if sink_col is not None:
        m = jnp.maximum(m, sink_col)
    return jnp.exp(s_prev - m).astype(BF16), jnp.exp(s_cur - m).astype(BF16), m


def _pair_masks():
    low = lax.broadcasted_iota(I32, (ATTN_BLOCK, LANES), 1) < HEAD_DIM
    scale = HEAD_DIM ** -0.5
    return low, jnp.where(low, scale, 0.0).astype(BF16), jnp.where(low, 0.0, scale).astype(BF16)


def _values_and_ones(v):
    return jnp.concatenate([v, jnp.ones((v.shape[0], LANES), v.dtype)], axis=1)


def _dil_attn_kernel(slope_ref, q_ref, kc_ref, kp_ref, vc_ref, vp_ref, o_ref, lse_ref, bias_ref,
                     *, tq, max_diff, dist_scale):
    first = pl.program_id(2) == 0
    _store_band_bias(bias_ref, _per_head_column(slope_ref, 0, DIL_HEADS, dist_scale), max_diff)
    low, keep_low, keep_high = _pair_masks()
    pair_cols = [slice(p * LANES, (p + 1) * LANES) for p in range(DIL_HEADS // 2)]
    for i in range(tq // ATTN_BLOCK):
        rows = slice(i * ATTN_BLOCK, (i + 1) * ATTN_BLOCK)
        prev_rows = slice((i - 1) * ATTN_BLOCK, i * ATTN_BLOCK)
        s_prev, s_cur = [], []
        for c in pair_cols:
            k_prev = kp_ref[:, c] if i == 0 else kc_ref[prev_rows, c]
            for keep in (keep_low, keep_high):
                q = q_ref[rows, c] * keep
                s_prev.append(lax.dot_general(q, k_prev, _NT, preferred_element_type=F32))
                s_cur.append(lax.dot_general(q, kc_ref[rows, c], _NT, preferred_element_type=F32))
        s_prev = jnp.concatenate(s_prev, axis=0) + bias_ref[0]
        s_cur = jnp.concatenate(s_cur, axis=0) + bias_ref[1]
        if i == 0:
            s_prev = jnp.where(first, NEG_INF, s_prev)
        p_prev, p_cur, m = _band_probs(s_prev, s_cur, None)
        for p, c in enumerate(pair_cols):
            v_prev = _values_and_ones(vp_ref[:, c] if i == 0 else vc_ref[prev_rows, c])
            v_cur = _values_and_ones(vc_ref[rows, c])
            od, mh = [], []
            for h in (2 * p, 2 * p + 1):
                hr = slice(h * ATTN_BLOCK, (h + 1) * ATTN_BLOCK)
                od.append(jnp.dot(p_prev[hr], v_prev, preferred_element_type=F32)
                          + jnp.dot(p_cur[hr], v_cur, preferred_element_type=F32))
                mh.append(m[hr])
            denom = jnp.where(low, od[0][:, LANES:], od[1][:, LANES:])
            o_ref[rows, c] = (jnp.where(low, od[0][:, :LANES], od[1][:, :LANES]) / denom).astype(o_ref.dtype)
            lse_ref[rows, c] = jnp.where(low, mh[0], mh[1]) + jnp.log(denom)


def _dil_attention(a_g, slopes_g, g):
    window, dil = DIL_GROUPS[g]
    bsz, _, sub_len, _ = a_g.shape
    tq = min(512, sub_len)
    nqb = tq // ATTN_BLOCK
    cur = lambda part: (lambda b, r, m: (b, r, m, part))
    prev = lambda part: (lambda b, r, m: (b, r, jnp.maximum(m * nqb - 1, 0), part))
    blk = lambda rows, imap: pl.BlockSpec((None, None, rows, DIL_OUT), imap)
    return pl.pallas_call(
        functools.partial(_dil_attn_kernel, tq=tq, max_diff=window // dil, dist_scale=float(dil)),
        grid=(bsz, dil, sub_len // tq),
        in_specs=[pl.BlockSpec(memory_space=pltpu.SMEM),
                  blk(tq, cur(0)), blk(tq, cur(1)), blk(ATTN_BLOCK, prev(1)),
                  blk(tq, cur(2)), blk(ATTN_BLOCK, prev(2))],
        out_specs=[blk(tq, cur(0)), blk(tq, cur(0))],
        out_shape=[jax.ShapeDtypeStruct((bsz, dil, sub_len, DIL_OUT), BF16),
                   jax.ShapeDtypeStruct((bsz, dil, sub_len, DIL_OUT), F32)],
        scratch_shapes=[pltpu.VMEM((2, DIL_HEADS * ATTN_BLOCK, ATTN_BLOCK), F32)],
        compiler_params=_cparams("parallel", "parallel", "arbitrary"),
        name=f"dil_attn_g{g}",
    )(slopes_g, a_g, a_g, a_g, a_g, a_g)


def _swa_attn_kernel(slope_ref, sink_ref, q_ref, kvc_ref, kvp_ref, o_ref, bias_ref, *, tq):
    first = pl.program_id(1) == 0
    low, keep_low, keep_high = _pair_masks()

    def both_halves(ref, lane0):
        part = ref[:, lane0:lane0 + HEAD_DIM]
        return jnp.concatenate([part, part], axis=1)

    tile = (SWA_REP, ATTN_BLOCK, ATTN_BLOCK)
    stacked = (SWA_REP * ATTN_BLOCK, ATTN_BLOCK)
    qi = lax.broadcasted_iota(I32, (ATTN_BLOCK, ATTN_BLOCK), 0)
    kj = lax.broadcasted_iota(I32, (ATTN_BLOCK, ATTN_BLOCK), 1)
    from_prev = kj > qi
    diff = jnp.where(from_prev, qi - kj + ATTN_BLOCK, qi - kj).astype(F32)
    for kvh in range(SWA_KV_HEADS):
        head0 = kvh * SWA_REP
        for r in range(SWA_REP):
            bias_ref[r] = -slope_ref[head0 + r] * diff
        sink_col = _per_head_column(sink_ref, head0, SWA_REP)
        k_cur_all, k_first = both_halves(kvc_ref, kvh * HEAD_DIM), both_halves(kvp_ref, kvh * HEAD_DIM)
        v_lane0 = SWA_KV_WIDTH + kvh * HEAD_DIM
        v_cur_all, v_first = both_halves(kvc_ref, v_lane0), both_halves(kvp_ref, v_lane0)
        pair_cols = [slice((head0 + 2 * p) * HEAD_DIM, (head0 + 2 * p + 2) * HEAD_DIM) for p in range(SWA_REP // 2)]
        for i in range(tq // ATTN_BLOCK):
            rows = slice(i * ATTN_BLOCK, (i + 1) * ATTN_BLOCK)
            if i == 0:
                k_win = jnp.concatenate([k_first, k_cur_all[rows]], axis=0)
                v_win = jnp.concatenate([v_first, v_cur_all[rows]], axis=0)
            else:
                win = slice((i - 1) * ATTN_BLOCK, (i + 1) * ATTN_BLOCK)
                k_win, v_win = k_cur_all[win], v_cur_all[win]
            q = jnp.concatenate([q_ref[rows, c] * keep for c in pair_cols for keep in (keep_low, keep_high)], axis=0)
            s2 = lax.dot_general(q, k_win, _NT, preferred_element_type=F32)
            s = jnp.where(from_prev, s2[:, :ATTN_BLOCK].reshape(tile), s2[:, ATTN_BLOCK:].reshape(tile)) \
                + bias_ref[...]
            if i == 0:
                s = jnp.where(from_prev & first, NEG_INF, s)
            s = s.reshape(stacked)
            m = jnp.maximum(jnp.max(s, axis=-1, keepdims=True), sink_col)
            p = jnp.exp(s - m).reshape(tile)
            p2 = jnp.concatenate([jnp.where(from_prev, p, 0.0).reshape(stacked).astype(BF16),
                                  jnp.where(from_prev, 0.0, p).reshape(stacked).astype(BF16)], axis=1)
            od = jnp.dot(p2, _values_and_ones(v_win), preferred_element_type=F32)
            o = od[:, :LANES] / (od[:, LANES:] + jnp.exp(sink_col - m))
            for p, c in enumerate(pair_cols):
                even = o[2 * p * ATTN_BLOCK:(2 * p + 1) * ATTN_BLOCK]
                odd = o[(2 * p + 1) * ATTN_BLOCK:(2 * p + 2) * ATTN_BLOCK]
                o_ref[rows, c] = jnp.where(low, even, odd).astype(o_ref.dtype)


def _swa_attention(hq, hkv, slopes_b, sinks, bsz, seq, tq=512):
    nqb = tq // ATTN_BLOCK
    hq3 = hq.reshape(bsz, seq, SWA_Q_WIDTH)
    hkv3 = hkv.reshape(bsz, seq, 2 * SWA_KV_WIDTH)
    smem = pl.BlockSpec(memory_space=pltpu.SMEM)
    out = pl.pallas_call(
        functools.partial(_swa_attn_kernel, tq=tq),
        grid=(bsz, seq // tq),
        in_specs=[smem, smem,
                  pl.BlockSpec((None, tq, SWA_Q_WIDTH), lambda b, m: (b, m, 0)),
                  pl.BlockSpec((None, tq, 2 * SWA_KV_WIDTH), lambda b, m: (b, m, 0)),
                  pl.BlockSpec((None, ATTN_BLOCK, 2 * SWA_KV_WIDTH),
                               lambda b, m: (b, jnp.maximum(m * nqb - 1, 0), 0))],
        out_specs=pl.BlockSpec((None, tq, SWA_Q_WIDTH), lambda b, m: (b, m, 0)),
        out_shape=jax.ShapeDtypeStruct((bsz, seq, SWA_Q_WIDTH), BF16),
        scratch_shapes=[pltpu.VMEM((SWA_REP, ATTN_BLOCK, ATTN_BLOCK), F32)],
        compiler_params=_cparams("parallel", "arbitrary"),
        name="swa_attn",
    )(slopes_b, sinks, hq3, hkv3, hkv3)
    return out.reshape(bsz * seq, SWA_Q_WIDTH)


def _sigmoid(x):
    return 0.5 * (jnp.tanh(0.5 * x) + 1.0)


def _layer_norm(z, g, b):
    mu = jnp.mean(z, axis=-1, keepdims=True)
    zc = z - mu
    var = jnp.mean(zc * zc, axis=-1, keepdims=True)
    return zc * lax.rsqrt(var + LN_EPS) * g + b


def _mix_out_kernel(x_ref, o0_ref, o1_ref, o2_ref, l0_ref, l1_ref, l2_ref, ob_ref, g_ref,
                    wpa_ref, wpb_ref, wo_ref, lng_ref, lnb_ref, rw_ref, rb_ref,
                    y_ref, ypk_ref, idx_ref, gate_ref, *scratch, d_model):
    def natural(ref, stages):
        dil, per, _ = ref.shape
        if dil == 1:
            return ref[0].astype(F32)
        for half, stage in enumerate(stages):
            for r in range(dil):
                stage[pl.ds(r, per, stride=dil), :] = ref[r, :, half * LANES:(half + 1) * LANES].astype(F32)
        return jnp.concatenate([stage[...] for stage in stages], axis=1)

    o0, o1, o2 = natural(o0_ref, None), natural(o1_ref, scratch[0:2]), natural(o2_ref, scratch[2:4])
    l0, l1, l2 = natural(l0_ref, None), natural(l1_ref, scratch[4:6]), natural(l2_ref, scratch[6:8])
    lm = jnp.maximum(jnp.maximum(l0, l1), l2)
    e0, e1, e2 = jnp.exp(l0 - lm), jnp.exp(l1 - lm), jnp.exp(l2 - lm)
    esum = e0 + e1 + e2
    out_a = (e0 / esum) * o0 + (e1 / esum) * o1 + (e2 / esum) * o2
    pa = jnp.dot(out_a.astype(BF16), wpa_ref[...], preferred_element_type=F32)
    pb = jnp.dot(ob_ref[...], wpb_ref[...], preferred_element_type=F32)
    ga = _sigmoid(g_ref[:, :d_model].astype(F32))
    gb = _sigmoid(g_ref[:, d_model:].astype(F32))
    merged = ga * pa + gb * pb
    mix = jnp.dot(merged.astype(BF16), wo_ref[...], preferred_element_type=F32)
    y = _layer_norm(DEEPNORM_ALPHA * x_ref[...] + mix, lng_ref[...], lnb_ref[...])
    y_ref[...] = y
    ypk_ref[...] = _pack_bf16_pairs(y)

    logits = jnp.dot(y.astype(BF16), rw_ref[...], preferred_element_type=F32) + rb_ref[...]
    lane = lax.broadcasted_iota(I32, logits.shape, 1)
    logits = jnp.where(lane < N_EXPERTS, logits, -jnp.inf)
    idx_out = jnp.zeros(logits.shape, I32)
    val_out = jnp.full(logits.shape, -jnp.inf, F32)
    for k in range(TOP_K):
        top = jnp.max(logits, axis=-1, keepdims=True)
        top_idx = jnp.min(jnp.where(logits == top, lane, LANES), axis=-1, keepdims=True)
        idx_out = jnp.where(lane == k, top_idx, idx_out)
        val_out = jnp.where(lane == k, top, val_out)
        logits = jnp.where(lane == top_idx, -jnp.inf, logits)
    ev = jnp.exp(val_out - jnp.max(val_out, axis=-1, keepdims=True))
    gates = ev / jnp.sum(ev, axis=-1, keepdims=True)
    idx_ref[...] = idx_out[:, :TOP_K]
    gate_ref[...] = gates[:, :TOP_K]


def _mix_out(x2, o_g, lse_g, out_b, gates_h, wpa, wpb, wo, ln_g, ln_b, rw_pad, rb_pad, seq, tm=512):
    n, d = x2.shape
    tiles = seq // tm
    row = lambda i: (i, 0)
    const = lambda i: (0, 0)
    rb = lambda w: pl.BlockSpec((tm, w), row)
    full = lambda a: pl.BlockSpec(a.shape, const)
    dil_specs = [pl.BlockSpec((None, dil, tm // dil, DIL_OUT), lambda i: (i // tiles, 0, i % tiles, 0))
                 for _, dil in DIL_GROUPS]
    return pl.pallas_call(
        functools.partial(_mix_out_kernel, d_model=d),
        grid=(n // tm,),
        in_specs=[rb(d)] + dil_specs + dil_specs + [rb(SWA_Q_WIDTH), rb(2 * d),
                  full(wpa), full(wpb), full(wo), full(ln_g), full(ln_b), full(rw_pad), full(rb_pad)],
        out_specs=[rb(d), rb(d // 2), rb(TOP_K), rb(TOP_K)],
        out_shape=[jax.ShapeDtypeStruct((n, d), F32), jax.ShapeDtypeStruct((n, d // 2), jnp.uint32),
                   jax.ShapeDtypeStruct((n, TOP_K), I32), jax.ShapeDtypeStruct((n, TOP_K), F32)],
        scratch_shapes=[pltpu.VMEM((tm, LANES), F32)] * (4 * DIL_OUT // LANES),
        compiler_params=_cparams("parallel"),
        name="mix_out",
    )(x2, *o_g, *lse_g, out_b, gates_h, wpa, wpb, wo, ln_g, ln_b, rw_pad, rb_pad)


def _token_mixer_ln1(x2, w_in, sinks, w_proj_a, w_proj_b, w_out, ln1_g, ln1_b, router_w, router_b, bsz, seq):
    n, d = x2.shape
    heads = jnp.arange(1, N_ALIBI_HEADS + 1, dtype=F32)
    slopes = jnp.exp2(-8.0 * heads / N_ALIBI_HEADS)
    *a_g, hq, hkv, gates_h = _in_proj(x2, w_in.astype(BF16), bsz, seq)
    o_g, lse_g = [], []
    for g in range(N_DIL):
        sl = slopes[SWA_Q_HEADS + g * DIL_HEADS:SWA_Q_HEADS + (g + 1) * DIL_HEADS]
        o, lse = _dil_attention(a_g[g], sl, g)
        o_g.append(o)
        lse_g.append(lse)
    out_b = _swa_attention(hq, hkv, slopes[:SWA_Q_HEADS], sinks.astype(F32), bsz, seq)
    rw_pad = jnp.zeros((d, LANES), BF16).at[:, :N_EXPERTS].set(router_w.astype(BF16))
    rb_pad = jnp.zeros((1, LANES), F32).at[0, :N_EXPERTS].set(router_b.astype(F32))
    return _mix_out(x2, o_g, lse_g, out_b, gates_h, w_proj_a.astype(BF16), w_proj_b.astype(BF16),
                    w_out.astype(BF16), ln1_g.reshape(1, d).astype(F32), ln1_b.reshape(1, d).astype(F32),
                    rw_pad, rb_pad, seq)


def _route_kernel(idx_ref, dest_ref, cnt_ref, carry_ref, pstart_ref, *, expert_block):
    phase, i = pl.program_id(0), pl.program_id(1)
    idx = idx_ref[...]
    t = idx.shape[0]
    lane = lax.broadcasted_iota(I32, (t, LANES), 1)
    onehot = [lane == idx[:, k:k + 1] for k in range(TOP_K)]
    multi = sum(oh.astype(F32) for oh in onehot)
    tile_cnt = jnp.sum(multi, axis=0, keepdims=True).astype(I32)

    @pl.when((phase == 0) & (i == 0))
    def _():
        carry_ref[...] = jnp.zeros_like(carry_ref)

    @pl.when((phase == 1) & (i == 0))
    def _():
        counts = carry_ref[...]
        padded = (counts + (expert_block - 1)) & (-expert_block)
        lane8 = lax.broadcasted_iota(I32, counts.shape, 1)
        incl = padded
        for sh in (1, 2, 4, 8, 16):
            incl = incl + jnp.where(lane8 >= sh, pltpu.roll(incl, sh, axis=1), 0)
        pstart_ref[...] = incl - padded
        cnt_ref[...] = counts
        carry_ref[...] = jnp.zeros_like(carry_ref)

    @pl.when(phase == 1)
    def _():
        r = lax.broadcasted_iota(I32, (t, t), 0)
        c = lax.broadcasted_iota(I32, (t, t), 1)
        earlier = (c < r).astype(BF16)
        cum = jnp.dot(earlier, multi.astype(BF16), preferred_element_type=F32).astype(I32)
        base = cum + carry_ref[0:1, :] + pstart_ref[0:1, :]
        dest = jnp.zeros((t, LANES), I32)
        for k in range(TOP_K):
            d_k = jnp.sum(jnp.where(onehot[k], base, 0), axis=-1, keepdims=True)
            dest = jnp.where(lane == k, d_k, dest)
        dest_ref[...] = dest[:, :TOP_K]

    carry_ref[...] = carry_ref[...] + tile_cnt


def _route(idx, expert_block, tile=512):
    n = idx.shape[0]
    tile = min(tile, n)
    return pl.pallas_call(
        functools.partial(_route_kernel, expert_block=expert_block),
        grid=(2, n // tile),
        in_specs=[pl.BlockSpec((tile, TOP_K), lambda p, i: (i, 0))],
        out_specs=[pl.BlockSpec((tile, TOP_K), lambda p, i: (i * p, 0)),
                   pl.BlockSpec((8, LANES), lambda p, i: (0, 0))],
        out_shape=[jax.ShapeDtypeStruct((n, TOP_K), I32), jax.ShapeDtypeStruct((8, LANES), I32)],
        scratch_shapes=[pltpu.VMEM((8, LANES), I32), pltpu.VMEM((8, LANES), I32)],
        compiler_params=_cparams("arbitrary", "arbitrary"),
        name="route",
    )(idx)


def _sc_dispatch(dest_t, y, rows, window=64):
    n, d = y.shape
    info = plsc.get_sparse_core_info()
    n_cores = info.num_cores
    per_worker = n // (n_cores * info.num_subcores)
    mesh = plsc.VectorSubcoreMesh(core_axis_name="c", subcore_axis_name="s")

    @functools.partial(pl.kernel, mesh=mesh, out_type=jax.ShapeDtypeStruct((rows, d), y.dtype),
                       scratch_types=[pltpu.VMEM((TOP_K, window), I32), pltpu.VMEM((window, d), y.dtype)],
                       name="sc_dispatch")
    def body(dest_hbm, y_hbm, xs_hbm, idx_v, rows_v):
        base = (lax.axis_index("s") * n_cores + lax.axis_index("c")) * per_worker

        @pl.loop(0, per_worker // window)
        def _(c):
            t0 = base + c * window
            for k in range(TOP_K):
                pltpu.sync_copy(dest_hbm.at[k, pl.ds(t0, window)], idx_v.at[k])
            pltpu.sync_copy(y_hbm.at[pl.ds(t0, window)], rows_v)
            for k in range(TOP_K):
                pltpu.sync_copy(rows_v, xs_hbm.at[idx_v.at[k]])

    return body(dest_t, y)


def _expert_kernel(row0_ref, cnt_ref, xs_hbm, wg_ref, bg_ref, wu_ref, bu_ref, wd_ref, bd_ref, ys_hbm,
                   wg_bf, wu_bf, wd_bf, x_buf, y_buf, in_sem, out_sem, *, block):
    e = pl.program_id(0)
    count = cnt_ref[e]
    n_blocks = (count + block - 1) // block
    row0 = row0_ref[e]

    chunk = 128
    for src, dst in ((wg_ref, wg_bf), (wu_ref, wu_bf), (wd_ref, wd_bf)):
        for r in range(0, src.shape[0], chunk):
            dst[r:r + chunk, :] = src[r:r + chunk, :].astype(BF16)

    def block_rows(j):
        return pl.ds(pl.multiple_of(row0 + j * block, block), block)

    def load(j, slot):
        return pltpu.make_async_copy(xs_hbm.at[block_rows(j)], x_buf.at[slot], in_sem.at[slot])

    def store(j, slot):
        return pltpu.make_async_copy(y_buf.at[slot], ys_hbm.at[block_rows(j)], out_sem.at[slot])

    def process(j, slot):
        load(j, slot).wait()

        @pl.when(j + 1 < n_blocks)
        def _():
            load(j + 1, 1 - slot).start()

        @pl.when(j >= 2)
        def _():
            store(j - 2, slot).wait()

        row = lax.broadcasted_iota(I32, x_buf.shape[1:], 0)
        x_lo, x_hi = _unpack_bf16_pairs(jnp.where(row < count - j * block, x_buf[slot], jnp.uint32(0)))
        xb = jnp.concatenate([x_lo.astype(BF16), x_hi.astype(BF16)], axis=1)
        gt = jnp.dot(xb, wg_bf[...], preferred_element_type=F32) + bg_ref[...]
        up = jnp.dot(xb, wu_bf[...], preferred_element_type=F32) + bu_ref[...]
        gt = jnp.minimum(gt, SWIGLU_LIMIT)
        up = jnp.clip(up, -SWIGLU_LIMIT, SWIGLU_LIMIT)
        hdn = gt * _sigmoid(SWIGLU_ALPHA * gt) * (up + 1.0)
        y_buf[slot] = _pack_bf16_pairs(jnp.dot(hdn.astype(BF16), wd_bf[...], preferred_element_type=F32)
                                       + bd_ref[...])
        store(j, slot).start()

    @pl.when(n_blocks > 0)
    def _():
        load(0, 0).start()

    def two_blocks(jj, carry):
        process(2 * jj, 0)

        @pl.when(2 * jj + 1 < n_blocks)
        def _():
            process(2 * jj + 1, 1)

        return carry

    lax.fori_loop(0, (n_blocks + 1) // 2, two_blocks, 0)

    @pl.when(n_blocks > 0)
    def _():
        store(0, 0).wait()

    @pl.when(n_blocks > 1)
    def _():
        store(0, 1).wait()


def _experts(row0, counts, xs, wg, bg, wu, bu, wd, bd, expert_block):
    rows, half = xs.shape
    e, d, f = wg.shape
    wmap = lambda i, r0, cnt: (i, 0, 0)
    any_spec = pl.BlockSpec(memory_space=pl.ANY)
    grid_spec = pltpu.PrefetchScalarGridSpec(
        num_scalar_prefetch=2,
        grid=(e,),
        in_specs=[any_spec,
                  pl.BlockSpec((None, d, f), wmap), pl.BlockSpec((None, 1, f), wmap),
                  pl.BlockSpec((None, d, f), wmap), pl.BlockSpec((None, 1, f), wmap),
                  pl.BlockSpec((None, f, d), wmap), pl.BlockSpec((None, 1, d), wmap)],
        out_specs=any_spec,
        scratch_shapes=[pltpu.VMEM((d, f), BF16), pltpu.VMEM((d, f), BF16), pltpu.VMEM((f, d), BF16),
                        pltpu.VMEM((2, expert_block, half), jnp.uint32),
                        pltpu.VMEM((2, expert_block, half), jnp.uint32),
                        pltpu.SemaphoreType.DMA((2,)), pltpu.SemaphoreType.DMA((2,))],
    )
    return pl.pallas_call(
        functools.partial(_expert_kernel, block=expert_block),
        grid_spec=grid_spec,
        out_shape=jax.ShapeDtypeStruct((rows, half), jnp.uint32),
        compiler_params=_cparams("arbitrary"),
        name="experts",
    )(row0, counts, xs, wg, bg.reshape(e, 1, f), wu, bu.reshape(e, 1, f), wd, bd.reshape(e, 1, d))


def _sc_gather(dest_t, ys, n, window=64):
    _, d = ys.shape
    info = plsc.get_sparse_core_info()
    n_cores = info.num_cores
    per_worker = n // (n_cores * info.num_subcores)
    mesh = plsc.VectorSubcoreMesh(core_axis_name="c", subcore_axis_name="s")

    n_chunks = per_worker // window
    n_buf = 2

    @functools.partial(pl.kernel, mesh=mesh, out_type=jax.ShapeDtypeStruct((TOP_K, n, d), ys.dtype),
                       scratch_types=[pltpu.VMEM((TOP_K, per_worker), I32)]
                       + [pltpu.VMEM((window, d), ys.dtype)] * n_buf + [pltpu.SemaphoreType.DMA] * (2 * n_buf),
                       name="sc_gather")
    def body(dest_hbm, ys_hbm, out_hbm, idx_v, buf0, buf1, gsem0, gsem1, ssem0, ssem1):
        bufs, gsem, ssem = (buf0, buf1), (gsem0, gsem1), (ssem0, ssem1)
        base = (lax.axis_index("s") * n_cores + lax.axis_index("c")) * per_worker
        for k in range(TOP_K):
            pltpu.sync_copy(dest_hbm.at[k, pl.ds(base, per_worker)], idx_v.at[k])

        def gather(c, k):
            return pltpu.make_async_copy(ys_hbm.at[idx_v.at[k, pl.ds(c * window, window)]], bufs[k % n_buf],
                                         gsem[k % n_buf])

        def store(c, k):
            return pltpu.make_async_copy(bufs[k % n_buf], out_hbm.at[k, pl.ds(base + c * window, window)],
                                         ssem[k % n_buf])

        gather(0, 0).start()

        @pl.loop(0, n_chunks)
        def _(c):
            for k in range(TOP_K):
                if k > 0:
                    store(c, k - 1).wait()
                    if k + 1 < TOP_K:
                        gather(c, k + 1).start()
                    else:
                        @pl.when(c + 1 < n_chunks)
                        def _():
                            gather(c + 1, 0).start()
                else:
                    @pl.when(c > 0)
                    def _():
                        store(c - 1, TOP_K - 1).wait()
                    gather(c, 1).start()
                gather(c, k).wait()
                store(c, k).start()

        store(n_chunks - 1, TOP_K - 1).wait()

    return body(dest_t, ys)


def _combine_dense_kernel(gate_ref, y_ref, lng_ref, lnb_ref, g_ref, o_ref):
    gates = gate_ref[...]
    ffn_lo = ffn_hi = None
    for k in range(TOP_K):
        lo, hi = _unpack_bf16_pairs(g_ref[k])
        gate = gates[:, k:k + 1]
        ffn_lo = gate * lo if k == 0 else ffn_lo + gate * lo
        ffn_hi = gate * hi if k == 0 else ffn_hi + gate * hi
    ffn = jnp.concatenate([ffn_lo, ffn_hi], axis=1)
    o_ref[...] = _layer_norm(DEEPNORM_ALPHA * y_ref[...] + ffn, lng_ref[...], lnb_ref[...])


def _combine_dense(gates, y, ln_g, ln_b, gathered, tile=512):
    n, d = y.shape
    tile = min(tile, n)
    row = lambda i: (i, 0)
    const = lambda i: (0, 0)
    return pl.pallas_call(
        _combine_dense_kernel,
        grid=(n // tile,),
        in_specs=[pl.BlockSpec((tile, TOP_K), row), pl.BlockSpec((tile, d), row),
                  pl.BlockSpec((1, d), const), pl.BlockSpec((1, d), const),
                  pl.BlockSpec((TOP_K, tile, d // 2), lambda i: (0, i, 0))],
        out_specs=pl.BlockSpec((tile, d), row),
        out_shape=jax.ShapeDtypeStruct((n, d), F32),
        compiler_params=_cparams("parallel"),
        name="combine_dense",
    )(gates, y, ln_g, ln_b, gathered)


MOE_ROW_BLOCK = 512


def _moe_ln2(y, y_packed, idx, gates, w_gate, b_gate, w_up, b_up, w_down, b_down, ln2_g, ln2_b):
    n, d = y.shape
    n_blocks = n * TOP_K // MOE_ROW_BLOCK + N_EXPERTS
    rows = n_blocks * MOE_ROW_BLOCK
    dest, counts = _route(idx, MOE_ROW_BLOCK)
    cnt = counts[0, :N_EXPERTS]
    padded = (cnt + MOE_ROW_BLOCK - 1) // MOE_ROW_BLOCK * MOE_ROW_BLOCK
    row0 = (jnp.cumsum(padded) - padded).astype(I32)
    dest_t = dest.T
    xs = _sc_dispatch(dest_t, y_packed, rows)
    ys = _experts(row0, cnt, xs, w_gate, b_gate.astype(F32), w_up, b_up.astype(F32),
                  w_down, b_down.astype(F32), MOE_ROW_BLOCK)
    gathered = _sc_gather(dest_t, ys, n)
    return _combine_dense(gates, y, ln2_g.reshape(1, d).astype(F32), ln2_b.reshape(1, d).astype(F32), gathered)


def kernel(x, w_in, sinks, w_proj_a, w_proj_b, w_out, ln1_g, ln1_b, router_w, router_b,
           w_gate, b_gate, w_up, b_up, w_down, b_down, ln2_g, ln2_b):
    bsz, seq, d = x.shape
    h = x.reshape(bsz * seq, d)
    for l in range(w_in.shape[0]):
        y, y_packed, idx, gates = _token_mixer_ln1(h, w_in[l], sinks[l], w_proj_a[l], w_proj_b[l], w_out[l],
                                                   ln1_g[l], ln1_b[l], router_w[l], router_b[l], bsz, seq)
        h = _moe_ln2(y, y_packed, idx, gates, w_gate[l], b_gate[l], w_up[l], b_up[l], w_down[l], b_down[l],
                     ln2_g[l], ln2_b[l])
    return h.reshape(bsz, seq, d)
```

```python
import functools

import jax
import jax.numpy as jnp
from jax import lax
from jax.experimental import pallas as pl
from jax.experimental.pallas import tpu as pltpu
from jax.experimental.pallas import tpu_sc as plsc

F32 = jnp.float32
BF16 = jnp.bfloat16
I32 = jnp.int32

HEAD_DIM = 64
DIL_GROUPS = ((128, 1), (512, 4), (2048, 16))
DIL_HEADS = 4
N_DIL = len(DIL_GROUPS)
DIL_OUT = DIL_HEADS * HEAD_DIM
DIL_WIDTH = N_DIL * DIL_OUT
SWA_Q_HEADS = 16
SWA_KV_HEADS = 2
SWA_REP = SWA_Q_HEADS // SWA_KV_HEADS
SWA_WINDOW = 128
SWA_Q_WIDTH = SWA_Q_HEADS * HEAD_DIM
SWA_KV_WIDTH = SWA_KV_HEADS * HEAD_DIM
N_ALIBI_HEADS = SWA_Q_HEADS + N_DIL * DIL_HEADS
ATTN_BLOCK = 128
N_EXPERTS = 32
TOP_K = 4
SWIGLU_LIMIT = 7.0
SWIGLU_ALPHA = 1.702
LN_EPS = 1e-5
DEPTH = 1
DEEPNORM_ALPHA = (2 * DEPTH) ** 0.25
NEG_INF = -1e30

LANES = 128
VMEM_LIMIT_BYTES = 56 * 1024 * 1024

A_QKV_W = 3 * DIL_WIDTH
B_Q_OFF = A_QKV_W
B_KV_OFF = B_Q_OFF + SWA_Q_WIDTH
GATE_OFF = B_KV_OFF + 2 * SWA_KV_WIDTH


def _cparams(*sem):
    return pltpu.CompilerParams(dimension_semantics=sem, vmem_limit_bytes=VMEM_LIMIT_BYTES)


def _pack_bf16_pairs(x):
    c = x.shape[1] // 2
    lo = lax.bitcast_convert_type(x[:, :c].astype(BF16).astype(F32), jnp.uint32)
    hi = lax.bitcast_convert_type(x[:, c:].astype(BF16).astype(F32), jnp.uint32)
    return (lo >> 16) | hi


def _unpack_bf16_pairs(w):
    lo = lax.bitcast_convert_type(w << 16, F32)
    hi = lax.bitcast_convert_type(w & jnp.uint32(0xFFFF0000), F32)
    return lo, hi


def _in_proj_kernel(x_ref, w_ref, a0_ref, a1_ref, a2_ref, hq_ref, hkv_ref, g_ref, *scratch, d_model, tm):
    xb = x_ref[...].astype(BF16)
    segments = ((hq_ref, B_Q_OFF, SWA_Q_WIDTH), (hkv_ref, B_KV_OFF, 2 * SWA_KV_WIDTH),
                (g_ref, GATE_OFF, 2 * d_model))
    for out_ref, col0, width in segments:
        for c in range(0, width, 512):
            w = min(512, width - c)
            r = jnp.dot(xb, w_ref[:, col0 + c:col0 + c + w], preferred_element_type=F32)
            out_ref[:, c:c + w] = r.astype(out_ref.dtype)
    for g, a_ref in enumerate((a0_ref, a1_ref, a2_ref)):
        dil = DIL_GROUPS[g][1]
        per = tm // dil
        for part in range(3):
            col0 = part * DIL_WIDTH + g * DIL_OUT
            res = jnp.dot(xb, w_ref[:, col0:col0 + DIL_OUT], preferred_element_type=F32)
            if dil == 1:
                a_ref[0, :, part * DIL_OUT:(part + 1) * DIL_OUT] = res.astype(a_ref.dtype)
                continue
            for half in range(DIL_OUT // LANES):
                stage = scratch[part * (DIL_OUT // LANES) + half]
                stage[...] = res[:, half * LANES:(half + 1) * LANES]
                c0 = part * DIL_OUT + half * LANES
                for r in range(dil):
                    a_ref[r, :, c0:c0 + LANES] = stage[pl.ds(r, per, stride=dil), :].astype(a_ref.dtype)


def _in_proj(x2, w_in_bf, bsz, seq, tm=512):
    n, d = x2.shape
    cols = w_in_bf.shape[1]
    tiles = seq // tm
    row = lambda i: (i, 0)
    dil_spec = lambda dil: pl.BlockSpec((None, dil, tm // dil, 3 * DIL_OUT), lambda i: (i // tiles, 0, i % tiles, 0))
    dil_shape = lambda dil: jax.ShapeDtypeStruct((bsz, dil, seq // dil, 3 * DIL_OUT), BF16)
    dils = [dil for _, dil in DIL_GROUPS]
    return pl.pallas_call(
        functools.partial(_in_proj_kernel, d_model=d, tm=tm),
        grid=(n // tm,),
        in_specs=[pl.BlockSpec((tm, d), row),
                  pl.BlockSpec((d, cols), lambda i: (0, 0), pipeline_mode=pl.Buffered(1))],
        out_specs=[dil_spec(dil) for dil in dils]
        + [pl.BlockSpec((tm, SWA_Q_WIDTH), row), pl.BlockSpec((tm, 2 * SWA_KV_WIDTH), row),
           pl.BlockSpec((tm, 2 * d), row)],
        out_shape=[dil_shape(dil) for dil in dils]
        + [jax.ShapeDtypeStruct((n, SWA_Q_WIDTH), BF16), jax.ShapeDtypeStruct((n, 2 * SWA_KV_WIDTH), BF16),
           jax.ShapeDtypeStruct((n, 2 * d), BF16)],
        scratch_shapes=[pltpu.VMEM((tm, LANES), F32)] * (3 * DIL_OUT // LANES),
        compiler_params=_cparams("parallel"),
        name="in_proj",
    )(x2, w_in_bf)


_NT = (((1,), (1,)), ((), ()))


def _per_head_column(ref, head0, n_heads, scale=1.0):
    head_of_row = lax.broadcasted_iota(I32, (n_heads * ATTN_BLOCK, 1), 0) // ATTN_BLOCK
    col = jnp.zeros((n_heads * ATTN_BLOCK, 1), F32)
    for h in range(n_heads):
        col = jnp.where(head_of_row == h, ref[head0 + h] * scale, col)
    return col


def _store_band_bias(bias_ref, slope_col, max_diff):
    rows = slope_col.shape[0]
    qi = lax.broadcasted_iota(I32, (rows, ATTN_BLOCK), 0) % ATTN_BLOCK
    kj = lax.broadcasted_iota(I32, (rows, ATTN_BLOCK), 1)
    diff_prev = qi - kj + ATTN_BLOCK
    diff_cur = qi - kj
    bias_ref[0] = jnp.where(diff_prev <= max_diff, -slope_col * diff_prev.astype(F32), NEG_INF)
    bias_ref[1] = jnp.where((diff_cur >= 0) & (diff_cur <= max_diff), -slope_col * diff_cur.astype(F32), NEG_INF)


def _band_probs(s_prev, s_cur, sink_col):
    m = jnp.max(jnp.maximum(s_prev, s_cur), axis=-1, keepdims=True)
    if sink_col is not None:
        m = jnp.maximum(m, sink_col)
    return jnp.exp(s_prev - m).astype(BF16), jnp.exp(s_cur - m).astype(BF16), m


def _pair_masks():
    low = lax.broadcasted_iota(I32, (ATTN_BLOCK, LANES), 1) < HEAD_DIM
    scale = HEAD_DIM ** -0.5
    return low, jnp.where(low, scale, 0.0).astype(BF16), jnp.where(low, 0.0, scale).astype(BF16)


def _values_and_ones(v):
    return jnp.concatenate([v, jnp.ones((v.shape[0], LANES), v.dtype)], axis=1)


def _dil_attn_kernel(slope_ref, q_ref, kc_ref, kp_ref, vc_ref, vp_ref, o_ref, lse_ref, bias_ref,
                     *, tq, max_diff, dist_scale):
    first = pl.program_id(2) == 0
    _store_band_bias(bias_ref, _per_head_column(slope_ref, 0, DIL_HEADS, dist_scale), max_diff)
    low, keep_low, keep_high = _pair_masks()
    pair_cols = [slice(p * LANES, (p + 1) * LANES) for p in range(DIL_HEADS // 2)]
    for res, i in [(res, i) for res in range(q_ref.shape[0]) for i in range(tq // ATTN_BLOCK)]:
        rows = slice(i * ATTN_BLOCK, (i + 1) * ATTN_BLOCK)
        prev_rows = slice((i - 1) * ATTN_BLOCK, i * ATTN_BLOCK)
        s_prev, s_cur = [], []
        for c in pair_cols:
            k_prev = kp_ref[res, :, c] if i == 0 else kc_ref[res, prev_rows, c]
            for keep in (keep_low, keep_high):
                q = q_ref[res, rows, c] * keep
                s_prev.append(lax.dot_general(q, k_prev, _NT, preferred_element_type=F32))
                s_cur.append(lax.dot_general(q, kc_ref[res, rows, c], _NT, preferred_element_type=F32))
        s_prev = jnp.concatenate(s_prev, axis=0) + bias_ref[0]
        s_cur = jnp.concatenate(s_cur, axis=0) + bias_ref[1]
        if i == 0:
            s_prev = jnp.where(first, NEG_INF, s_prev)
        p_prev, p_cur, m = _band_probs(s_prev, s_cur, None)
        for p, c in enumerate(pair_cols):
            v_prev = _values_and_ones(vp_ref[res, :, c] if i == 0 else vc_ref[res, prev_rows, c])
            v_cur = _values_and_ones(vc_ref[res, rows, c])
            od, mh = [], []
            for h in (2 * p, 2 * p + 1):
                hr = slice(h * ATTN_BLOCK, (h + 1) * ATTN_BLOCK)
                od.append(jnp.dot(p_prev[hr], v_prev, preferred_element_type=F32)
                          + jnp.dot(p_cur[hr], v_cur, preferred_element_type=F32))
                mh.append(m[hr])
            denom = jnp.where(low, od[0][:, LANES:], od[1][:, LANES:])
            o_ref[res, rows, c] = (jnp.where(low, od[0][:, :LANES], od[1][:, :LANES]) / denom).astype(o_ref.dtype)
            lse_ref[res, rows, c] = jnp.where(low, mh[0], mh[1]) + jnp.log(denom)


def _dil_attention(a_g, slopes_g, g):
    window, dil = DIL_GROUPS[g]
    bsz, _, sub_len, _ = a_g.shape
    rows_per_step = 512
    tq = min(rows_per_step, sub_len)
    res_per_step = rows_per_step // tq
    nqb = tq // ATTN_BLOCK
    cur = lambda part: (lambda b, r, m: (b, r, m, part))
    prev = lambda part: (lambda b, r, m: (b, r, jnp.maximum(m * nqb - 1, 0), part))
    blk = lambda rows, imap: pl.BlockSpec((None, res_per_step, rows, DIL_OUT), imap)
    return pl.pallas_call(
        functools.partial(_dil_attn_kernel, tq=tq, max_diff=window // dil, dist_scale=float(dil)),
        grid=(bsz, dil // res_per_step, sub_len // tq),
        in_specs=[pl.BlockSpec(memory_space=pltpu.SMEM),
                  blk(tq, cur(0)), blk(tq, cur(1)), blk(ATTN_BLOCK, prev(1)),
                  blk(tq, cur(2)), blk(ATTN_BLOCK, prev(2))],
        out_specs=[blk(tq, cur(0)), blk(tq, cur(0))],
        out_shape=[jax.ShapeDtypeStruct((bsz, dil, sub_len, DIL_OUT), BF16),
                   jax.ShapeDtypeStruct((bsz, dil, sub_len, DIL_OUT), F32)],
        scratch_shapes=[pltpu.VMEM((2, DIL_HEADS * ATTN_BLOCK, ATTN_BLOCK), F32)],
        compiler_params=_cparams("parallel", "parallel", "arbitrary"),
        name=f"dil_attn_g{g}",
    )(slopes_g, a_g, a_g, a_g, a_g, a_g)


def _swa_attn_kernel(slope_ref, sink_ref, q_ref, kvc_ref, kvp_ref, o_ref, bias_ref, *, tq):
    first = pl.program_id(1) == 0
    low, keep_low, keep_high = _pair_masks()

    def both_halves(ref, lane0):
        part = ref[:, lane0:lane0 + HEAD_DIM]
        return jnp.concatenate([part, part], axis=1)

    tile = (SWA_REP, ATTN_BLOCK, ATTN_BLOCK)
    stacked = (SWA_REP * ATTN_BLOCK, ATTN_BLOCK)
    qi = lax.broadcasted_iota(I32, (ATTN_BLOCK, ATTN_BLOCK), 0)
    kj = lax.broadcasted_iota(I32, (ATTN_BLOCK, ATTN_BLOCK), 1)
    from_prev = kj > qi
    diff = jnp.where(from_prev, qi - kj + ATTN_BLOCK, qi - kj).astype(F32)
    for kvh in range(SWA_KV_HEADS):
        head0 = kvh * SWA_REP
        for r in range(SWA_REP):
            bias_ref[r] = -slope_ref[head0 + r] * diff
        sink_col = _per_head_column(sink_ref, head0, SWA_REP)
        k_cur_all, k_first = both_halves(kvc_ref, kvh * HEAD_DIM), both_halves(kvp_ref, kvh * HEAD_DIM)
        v_lane0 = SWA_KV_WIDTH + kvh * HEAD_DIM
        v_cur_all, v_first = both_halves(kvc_ref, v_lane0), both_halves(kvp_ref, v_lane0)
        pair_cols = [slice((head0 + 2 * p) * HEAD_DIM, (head0 + 2 * p + 2) * HEAD_DIM) for p in range(SWA_REP // 2)]
        for i in range(tq // ATTN_BLOCK):
            rows = slice(i * ATTN_BLOCK, (i + 1) * ATTN_BLOCK)
            if i == 0:
                k_win = jnp.concatenate([k_first, k_cur_all[rows]], axis=0)
                v_win = jnp.concatenate([v_first, v_cur_all[rows]], axis=0)
            else:
                win = slice((i - 1) * ATTN_BLOCK, (i + 1) * ATTN_BLOCK)
                k_win, v_win = k_cur_all[win], v_cur_all[win]
            q = jnp.concatenate([q_ref[rows, c] * keep for c in pair_cols for keep in (keep_low, keep_high)], axis=0)
            s2 = lax.dot_general(q, k_win, _NT, preferred_element_type=F32)
            s = jnp.where(from_prev, s2[:, :ATTN_BLOCK].reshape(tile), s2[:, ATTN_BLOCK:].reshape(tile)) \
                + bias_ref[...]
            if i == 0:
                s = jnp.where(from_prev & first, NEG_INF, s)
            s = s.reshape(stacked)
            m = jnp.maximum(jnp.max(s, axis=-1, keepdims=True), sink_col)
            p = jnp.exp(s - m).reshape(tile)
            p2 = jnp.concatenate([jnp.where(from_prev, p, 0.0).reshape(stacked).astype(BF16),
                                  jnp.where(from_prev, 0.0, p).reshape(stacked).astype(BF16)], axis=1)
            od = jnp.dot(p2, _values_and_ones(v_win), preferred_element_type=F32)
            o = od[:, :LANES] / (od[:, LANES:] + jnp.exp(sink_col - m))
            for p, c in enumerate(pair_cols):
                even = o[2 * p * ATTN_BLOCK:(2 * p + 1) * ATTN_BLOCK]
                odd = o[(2 * p + 1) * ATTN_BLOCK:(2 * p + 2) * ATTN_BLOCK]
                o_ref[rows, c] = jnp.where(low, even, odd).astype(o_ref.dtype)


def _swa_attention(hq, hkv, slopes_b, sinks, bsz, seq, tq=512):
    nqb = tq // ATTN_BLOCK
    hq3 = hq.reshape(bsz, seq, SWA_Q_WIDTH)
    hkv3 = hkv.reshape(bsz, seq, 2 * SWA_KV_WIDTH)
    smem = pl.BlockSpec(memory_space=pltpu.SMEM)
    out = pl.pallas_call(
        functools.partial(_swa_attn_kernel, tq=tq),
        grid=(bsz, seq // tq),
        in_specs=[smem, smem,
                  pl.BlockSpec((None, tq, SWA_Q_WIDTH), lambda b, m: (b, m, 0)),
                  pl.BlockSpec((None, tq, 2 * SWA_KV_WIDTH), lambda b, m: (b, m, 0)),
                  pl.BlockSpec((None, ATTN_BLOCK, 2 * SWA_KV_WIDTH),
                               lambda b, m: (b, jnp.maximum(m * nqb - 1, 0), 0))],
        out_specs=pl.BlockSpec((None, tq, SWA_Q_WIDTH), lambda b, m: (b, m, 0)),
        out_shape=jax.ShapeDtypeStruct((bsz, seq, SWA_Q_WIDTH), BF16),
        scratch_shapes=[pltpu.VMEM((SWA_REP, ATTN_BLOCK, ATTN_BLOCK), F32)],
        compiler_params=_cparams("parallel", "arbitrary"),
        name="swa_attn",
    )(slopes_b, sinks, hq3, hkv3, hkv3)
    return out.reshape(bsz * seq, SWA_Q_WIDTH)


def _sigmoid(x):
    return 0.5 * (jnp.tanh(0.5 * x) + 1.0)


def _layer_norm(z, g, b):
    mu = jnp.mean(z, axis=-1, keepdims=True)
    zc = z - mu
    var = jnp.mean(zc * zc, axis=-1, keepdims=True)
    return zc * lax.rsqrt(var + LN_EPS) * g + b


MIX_OUT_CHAINS = 1


def _mix_out_kernel(x_ref, o0_ref, o1_ref, o2_ref, l0_ref, l1_ref, l2_ref, ob_ref, g_ref,
                    wpa_ref, wpb_ref, wo_ref, lng_ref, lnb_ref, rw_ref, rb_ref,
                    y_ref, ypk_ref, idx_ref, gate_ref, *scratch, d_model):
    def natural(ref, stages):
        dil, per, _ = ref.shape
        if dil == 1:
            return ref[0].astype(F32)
        for half, stage in enumerate(stages):
            for r in range(dil):
                stage[pl.ds(r, per, stride=dil), :] = ref[r, :, half * LANES:(half + 1) * LANES].astype(F32)
        return jnp.concatenate([stage[...] for stage in stages], axis=1)

    o_nat = natural(o0_ref, None), natural(o1_ref, scratch[0:2]), natural(o2_ref, scratch[2:4])
    l_nat = natural(l0_ref, None), natural(l1_ref, scratch[4:6]), natural(l2_ref, scratch[6:8])

    def chain(rows):
        (o0, o1, o2), (l0, l1, l2) = [a[rows] for a in o_nat], [a[rows] for a in l_nat]
        lm = jnp.maximum(jnp.maximum(l0, l1), l2)
        e0, e1, e2 = jnp.exp(l0 - lm), jnp.exp(l1 - lm), jnp.exp(l2 - lm)
        out_a = (e0 * o0 + e1 * o1 + e2 * o2) / (e0 + e1 + e2)
        pa = jnp.dot(out_a.astype(BF16), wpa_ref[...], preferred_element_type=F32)
        pb = jnp.dot(ob_ref[rows, :], wpb_ref[...], preferred_element_type=F32)
        ga = _sigmoid(g_ref[rows, :d_model].astype(F32))
        gb = _sigmoid(g_ref[rows, d_model:].astype(F32))
        merged = ga * pa + gb * pb
        mix = jnp.dot(merged.astype(BF16), wo_ref[...], preferred_element_type=F32)
        y = _layer_norm(DEEPNORM_ALPHA * x_ref[rows, :] + mix, lng_ref[...], lnb_ref[...])
        y_ref[rows, :] = y
        ypk_ref[rows, :] = _pack_bf16_pairs(y)

        logits = jnp.dot(y.astype(BF16), rw_ref[...], preferred_element_type=F32) + rb_ref[...]
        lane = lax.broadcasted_iota(I32, logits.shape, 1)
        logits = jnp.where(lane < N_EXPERTS, logits, -jnp.inf)
        idx_out = jnp.zeros(logits.shape, I32)
        val_out = jnp.full(logits.shape, -jnp.inf, F32)
        for k in range(TOP_K):
            top = jnp.max(logits, axis=-1, keepdims=True)
            top_idx = jnp.min(jnp.where(logits == top, lane, LANES), axis=-1, keepdims=True)
            idx_out = jnp.where(lane == k, top_idx, idx_out)
            val_out = jnp.where(lane == k, top, val_out)
            logits = jnp.where(lane == top_idx, -jnp.inf, logits)
        ev = jnp.exp(val_out - jnp.max(val_out, axis=-1, keepdims=True))
        gates = ev / jnp.sum(ev, axis=-1, keepdims=True)
        idx_ref[rows, :] = idx_out[:, :TOP_K]
        gate_ref[rows, :] = gates[:, :TOP_K]

    tm = x_ref.shape[0]
    for part in range(MIX_OUT_CHAINS):
        chain(slice(part * tm // MIX_OUT_CHAINS, (part + 1) * tm // MIX_OUT_CHAINS))


def _mix_out(x2, o_g, lse_g, out_b, gates_h, wpa, wpb, wo, ln_g, ln_b, rw_pad, rb_pad, seq, tm=512):
    n, d = x2.shape
    tiles = seq // tm
    row = lambda i: (i, 0)
    const = lambda i: (0, 0)
    rb = lambda w: pl.BlockSpec((tm, w), row)
    full = lambda a: pl.BlockSpec(a.shape, const)
    dil_specs = [pl.BlockSpec((None, dil, tm // dil, DIL_OUT), lambda i: (i // tiles, 0, i % tiles, 0))
                 for _, dil in DIL_GROUPS]
    return pl.pallas_call(
        functools.partial(_mix_out_kernel, d_model=d),
        grid=(n // tm,),
        in_specs=[rb(d)] + dil_specs + dil_specs + [rb(SWA_Q_WIDTH), rb(2 * d),
                  full(wpa), full(wpb), full(wo), full(ln_g), full(ln_b), full(rw_pad), full(rb_pad)],
        out_specs=[rb(d), rb(d // 2), rb(TOP_K), rb(TOP_K)],
        out_shape=[jax.ShapeDtypeStruct((n, d), F32), jax.ShapeDtypeStruct((n, d // 2), jnp.uint32),
                   jax.ShapeDtypeStruct((n, TOP_K), I32), jax.ShapeDtypeStruct((n, TOP_K), F32)],
        scratch_shapes=[pltpu.VMEM((tm, LANES), F32)] * (4 * DIL_OUT // LANES),
        compiler_params=_cparams("parallel"),
        name="mix_out",
    )(x2, *o_g, *lse_g, out_b, gates_h, wpa, wpb, wo, ln_g, ln_b, rw_pad, rb_pad)


def _token_mixer_ln1(x2, w_in, sinks, w_proj_a, w_proj_b, w_out, ln1_g, ln1_b, router_w, router_b, bsz, seq):
    n, d = x2.shape
    heads = jnp.arange(1, N_ALIBI_HEADS + 1, dtype=F32)
    slopes = jnp.exp2(-8.0 * heads / N_ALIBI_HEADS)
    *a_g, hq, hkv, gates_h = _in_proj(x2, w_in.astype(BF16), bsz, seq)
    o_g, lse_g = [], []
    for g in range(N_DIL):
        sl = slopes[SWA_Q_HEADS + g * DIL_HEADS:SWA_Q_HEADS + (g + 1) * DIL_HEADS]
        o, lse = _dil_attention(a_g[g], sl, g)
        o_g.append(o)
        lse_g.append(lse)
    out_b = _swa_attention(hq, hkv, slopes[:SWA_Q_HEADS], sinks.astype(F32), bsz, seq)
    rw_pad = jnp.zeros((d, LANES), BF16).at[:, :N_EXPERTS].set(router_w.astype(BF16))
    rb_pad = jnp.zeros((1, LANES), F32).at[0, :N_EXPERTS].set(router_b.astype(F32))
    return _mix_out(x2, o_g, lse_g, out_b, gates_h, w_proj_a.astype(BF16), w_proj_b.astype(BF16),
                    w_out.astype(BF16), ln1_g.reshape(1, d).astype(F32), ln1_b.reshape(1, d).astype(F32),
                    rw_pad, rb_pad, seq)


def _route_kernel(idx_ref, dest_ref, cnt_ref, carry_ref, pstart_ref, *, expert_block):
    phase, i = pl.program_id(0), pl.program_id(1)
    idx = idx_ref[...]
    t = idx.shape[0]
    lane = lax.broadcasted_iota(I32, (t, LANES), 1)
    onehot = [lane == idx[:, k:k + 1] for k in range(TOP_K)]
    multi = sum(oh.astype(F32) for oh in onehot)
    tile_cnt = jnp.sum(multi, axis=0, keepdims=True).astype(I32)

    @pl.when((phase == 0) & (i == 0))
    def _():
        carry_ref[...] = jnp.zeros_like(carry_ref)

    @pl.when((phase == 1) & (i == 0))
    def _():
        counts = carry_ref[...]
        padded = (counts + (expert_block - 1)) & (-expert_block)
        lane8 = lax.broadcasted_iota(I32, counts.shape, 1)
        incl = padded
        for sh in (1, 2, 4, 8, 16):
            incl = incl + jnp.where(lane8 >= sh, pltpu.roll(incl, sh, axis=1), 0)
        pstart_ref[...] = incl - padded
        cnt_ref[...] = counts
        carry_ref[...] = jnp.zeros_like(carry_ref)

    @pl.when(phase == 1)
    def _():
        r = lax.broadcasted_iota(I32, (t, t), 0)
        c = lax.broadcasted_iota(I32, (t, t), 1)
        earlier = (c < r).astype(BF16)
        cum = jnp.dot(earlier, multi.astype(BF16), preferred_element_type=F32).astype(I32)
        base = cum + carry_ref[0:1, :] + pstart_ref[0:1, :]
        dest = jnp.zeros((t, LANES), I32)
        for k in range(TOP_K):
            d_k = jnp.sum(jnp.where(onehot[k], base, 0), axis=-1, keepdims=True)
            dest = jnp.where(lane == k, d_k, dest)
        dest_ref[...] = dest[:, :TOP_K]

    carry_ref[...] = carry_ref[...] + tile_cnt


def _route(idx, expert_block, tile=512):
    n = idx.shape[0]
    tile = min(tile, n)
    return pl.pallas_call(
        functools.partial(_route_kernel, expert_block=expert_block),
        grid=(2, n // tile),
        in_specs=[pl.BlockSpec((tile, TOP_K), lambda p, i: (i, 0))],
        out_specs=[pl.BlockSpec((tile, TOP_K), lambda p, i: (i * p, 0)),
                   pl.BlockSpec((8, LANES), lambda p, i: (0, 0))],
        out_shape=[jax.ShapeDtypeStruct((n, TOP_K), I32), jax.ShapeDtypeStruct((8, LANES), I32)],
        scratch_shapes=[pltpu.VMEM((8, LANES), I32), pltpu.VMEM((8, LANES), I32)],
        compiler_params=_cparams("arbitrary", "arbitrary"),
        name="route",
    )(idx)


def _sc_dispatch(dest_t, y, rows, window=64):
    n, d = y.shape
    info = plsc.get_sparse_core_info()
    n_cores = info.num_cores
    per_worker = n // (n_cores * info.num_subcores)
    mesh = plsc.VectorSubcoreMesh(core_axis_name="c", subcore_axis_name="s")

    @functools.partial(pl.kernel, mesh=mesh, out_type=jax.ShapeDtypeStruct((rows, d), y.dtype),
                       scratch_types=[pltpu.VMEM((TOP_K, window), I32), pltpu.VMEM((window, d), y.dtype)],
                       name="sc_dispatch")
    def body(dest_hbm, y_hbm, xs_hbm, idx_v, rows_v):
        base = (lax.axis_index("s") * n_cores + lax.axis_index("c")) * per_worker

        @pl.loop(0, per_worker // window)
        def _(c):
            t0 = base + c * window
            for k in range(TOP_K):
                pltpu.sync_copy(dest_hbm.at[k, pl.ds(t0, window)], idx_v.at[k])
            pltpu.sync_copy(y_hbm.at[pl.ds(t0, window)], rows_v)
            for k in range(TOP_K):
                pltpu.sync_copy(rows_v, xs_hbm.at[idx_v.at[k]])

    return body(dest_t, y)


def _expert_kernel(be_ref, nu_ref, nv_ref, x_ref, wg_ref, bg_ref, wu_ref, bu_ref, wd_ref, bd_ref, o_ref,
                   wg_bf, wu_bf, wd_bf):
    j = pl.program_id(0)
    used = j < nu_ref[0]
    new_expert = (j == 0) | (be_ref[j] != be_ref[jnp.maximum(j - 1, 0)])

    @pl.when(used & new_expert)
    def _():
        chunk = 128
        for src, dst in ((wg_ref, wg_bf), (wu_ref, wu_bf), (wd_ref, wd_bf)):
            for r in range(0, src.shape[0], chunk):
                dst[r:r + chunk, :] = src[r:r + chunk, :].astype(BF16)

    @pl.when(used)
    def _():
        row = lax.broadcasted_iota(I32, x_ref.shape, 0)
        x_lo, x_hi = _unpack_bf16_pairs(jnp.where(row < nv_ref[j], x_ref[...], jnp.uint32(0)))
        xb = jnp.concatenate([x_lo.astype(BF16), x_hi.astype(BF16)], axis=1)
        gt = jnp.dot(xb, wg_bf[...], preferred_element_type=F32) + bg_ref[...]
        up = jnp.dot(xb, wu_bf[...], preferred_element_type=F32) + bu_ref[...]
        gt = jnp.minimum(gt, SWIGLU_LIMIT)
        up = jnp.clip(up, -SWIGLU_LIMIT, SWIGLU_LIMIT)
        hdn = gt * _sigmoid(SWIGLU_ALPHA * gt) * (up + 1.0)
        o_ref[...] = _pack_bf16_pairs(jnp.dot(hdn.astype(BF16), wd_bf[...], preferred_element_type=F32)
                                      + bd_ref[...])

    @pl.when(jnp.logical_not(used))
    def _():
        o_ref[...] = jnp.zeros_like(o_ref)


def _experts(block_e, n_used, n_valid, xs, wg, bg, wu, bu, wd, bd, expert_block):
    rows, _ = xs.shape
    e, d, f = wg.shape
    xmap = lambda j, be, nu, nv: (jnp.minimum(j, nu[0] - 1), 0)
    wmap = lambda j, be, nu, nv: (be[j], 0, 0)
    grid_spec = pltpu.PrefetchScalarGridSpec(
        num_scalar_prefetch=3,
        grid=(rows // expert_block,),
        in_specs=[pl.BlockSpec((expert_block, d // 2), xmap),
                  pl.BlockSpec((None, d, f), wmap), pl.BlockSpec((None, 1, f), wmap),
                  pl.BlockSpec((None, d, f), wmap), pl.BlockSpec((None, 1, f), wmap),
                  pl.BlockSpec((None, f, d), wmap), pl.BlockSpec((None, 1, d), wmap)],
        out_specs=pl.BlockSpec((expert_block, d // 2), lambda j, be, nu, nv: (j, 0)),
        scratch_shapes=[pltpu.VMEM((d, f), BF16), pltpu.VMEM((d, f), BF16), pltpu.VMEM((f, d), BF16)],
    )
    return pl.pallas_call(
        _expert_kernel,
        grid_spec=grid_spec,
        out_shape=jax.ShapeDtypeStruct((rows, d // 2), jnp.uint32),
        compiler_params=_cparams("arbitrary"),
        name="experts",
    )(block_e, n_used, n_valid, xs, wg, bg.reshape(e, 1, f), wu, bu.reshape(e, 1, f), wd, bd.reshape(e, 1, d))


def _sc_gather(dest_t, ys, n, window=64):
    _, d = ys.shape
    info = plsc.get_sparse_core_info()
    n_cores = info.num_cores
    per_worker = n // (n_cores * info.num_subcores)
    mesh = plsc.VectorSubcoreMesh(core_axis_name="c", subcore_axis_name="s")

    n_chunks = per_worker // window
    n_buf = 2

    @functools.partial(pl.kernel, mesh=mesh, out_type=jax.ShapeDtypeStruct((TOP_K, n, d), ys.dtype),
                       scratch_types=[pltpu.VMEM((TOP_K, per_worker), I32)]
                       + [pltpu.VMEM((window, d), ys.dtype)] * n_buf + [pltpu.SemaphoreType.DMA] * (2 * n_buf),
                       name="sc_gather")
    def body(dest_hbm, ys_hbm, out_hbm, idx_v, buf0, buf1, gsem0, gsem1, ssem0, ssem1):
        bufs, gsem, ssem = (buf0, buf1), (gsem0, gsem1), (ssem0, ssem1)
        base = (lax.axis_index("s") * n_cores + lax.axis_index("c")) * per_worker
        for k in range(TOP_K):
            pltpu.sync_copy(dest_hbm.at[k, pl.ds(base, per_worker)], idx_v.at[k])

        def gather(c, k):
            return pltpu.make_async_copy(ys_hbm.at[idx_v.at[k, pl.ds(c * window, window)]], bufs[k % n_buf],
                                         gsem[k % n_buf])

        def store(c, k):
            return pltpu.make_async_copy(bufs[k % n_buf], out_hbm.at[k, pl.ds(base + c * window, window)],
                                         ssem[k % n_buf])

        gather(0, 0).start()

        @pl.loop(0, n_chunks)
        def _(c):
            for k in range(TOP_K):
                if k > 0:
                    store(c, k - 1).wait()
                    if k + 1 < TOP_K:
                        gather(c, k + 1).start()
                    else:
                        @pl.when(c + 1 < n_chunks)
                        def _():
                            gather(c + 1, 0).start()
                else:
                    @pl.when(c > 0)
                    def _():
                        store(c - 1, TOP_K - 1).wait()
                    gather(c, 1).start()
                gather(c, k).wait()
                store(c, k).start()

        store(n_chunks - 1, TOP_K - 1).wait()

    return body(dest_t, ys)


def _combine_dense_kernel(gate_ref, y_ref, lng_ref, lnb_ref, g_ref, o_ref):
    gates = gate_ref[...]
    ffn_lo = ffn_hi = None
    for k in range(TOP_K):
        lo, hi = _unpack_bf16_pairs(g_ref[k])
        gate = gates[:, k:k + 1]
        ffn_lo = gate * lo if k == 0 else ffn_lo + gate * lo
        ffn_hi = gate * hi if k == 0 else ffn_hi + gate * hi
    ffn = jnp.concatenate([ffn_lo, ffn_hi], axis=1)
    o_ref[...] = _layer_norm(DEEPNORM_ALPHA * y_ref[...] + ffn, lng_ref[...], lnb_ref[...])


def _combine_dense(gates, y, ln_g, ln_b, gathered, tile=512):
    n, d = y.shape
    tile = min(tile, n)
    row = lambda i: (i, 0)
    const = lambda i: (0, 0)
    return pl.pallas_call(
        _combine_dense_kernel,
        grid=(n // tile,),
        in_specs=[pl.BlockSpec((tile, TOP_K), row), pl.BlockSpec((tile, d), row),
                  pl.BlockSpec((1, d), const), pl.BlockSpec((1, d), const),
                  pl.BlockSpec((TOP_K, tile, d // 2), lambda i: (0, i, 0))],
        out_specs=pl.BlockSpec((tile, d), row),
        out_shape=jax.ShapeDtypeStruct((n, d), F32),
        compiler_params=_cparams("parallel"),
        name="combine_dense",
    )(gates, y, ln_g, ln_b, gathered)


MOE_ROW_BLOCK = 512


def _moe_ln2(y, y_packed, idx, gates, w_gate, b_gate, w_up, b_up, w_down, b_down, ln2_g, ln2_b):
    n, d = y.shape
    n_blocks = n * TOP_K // MOE_ROW_BLOCK + N_EXPERTS
    rows = n_blocks * MOE_ROW_BLOCK
    dest, counts = _route(idx, MOE_ROW_BLOCK)
    cnt = counts[0, :N_EXPERTS]
    padded = (cnt + MOE_ROW_BLOCK - 1) // MOE_ROW_BLOCK * MOE_ROW_BLOCK
    p_end = jnp.cumsum(padded)
    block_row0 = jnp.arange(n_blocks, dtype=I32) * MOE_ROW_BLOCK
    block_e = jnp.minimum(jnp.sum(p_end[None, :] <= block_row0[:, None], axis=1), N_EXPERTS - 1).astype(I32)
    n_used = (p_end[-1:] // MOE_ROW_BLOCK).astype(I32)
    n_valid = jnp.clip(cnt[block_e] - (block_row0 - (p_end - padded)[block_e]), 0, MOE_ROW_BLOCK).astype(I32)
    dest_t = dest.T
    xs = _sc_dispatch(dest_t, y_packed, rows)
    ys = _experts(block_e, n_used, n_valid, xs, w_gate, b_gate.astype(F32), w_up, b_up.astype(F32),
                  w_down, b_down.astype(F32), MOE_ROW_BLOCK)
    gathered = _sc_gather(dest_t, ys, n)
    return _combine_dense(gates, y, ln2_g.reshape(1, d).astype(F32), ln2_b.reshape(1, d).astype(F32), gathered)


def kernel(x, w_in, sinks, w_proj_a, w_proj_b, w_out, ln1_g, ln1_b, router_w, router_b,
           w_gate, b_gate, w_up, b_up, w_down, b_down, ln2_g, ln2_b):
    bsz, seq, d = x.shape
    h = x.reshape(bsz * seq, d)
    for l in range(w_in.shape[0]):
        y, y_packed, idx, gates = _token_mixer_ln1(h, w_in[l], sinks[l], w_proj_a[l], w_proj_b[l], w_out[l],
                                                   ln1_g[l], ln1_b[l], router_w[l], router_b[l], bsz, seq)
        h = _moe_ln2(y, y_packed, idx, gates, w_gate[l], b_gate[l], w_up[l], b_up[l], w_down[l], b_down[l],
                     ln2_g[l], ln2_b[l])
    return h.reshape(bsz, seq, d)
```

```python
import functools

import jax
import jax.numpy as jnp
from jax import lax
from jax.experimental import pallas as pl
from jax.experimental.pallas import tpu as pltpu
from jax.experimental.pallas import tpu_sc as plsc

F32 = jnp.float32
BF16 = jnp.bfloat16
I32 = jnp.int32

HEAD_DIM = 64
DIL_GROUPS = ((128, 1), (512, 4), (2048, 16))
DIL_HEADS = 4
N_DIL = len(DIL_GROUPS)
DIL_OUT = DIL_HEADS * HEAD_DIM
DIL_WIDTH = N_DIL * DIL_OUT
SWA_Q_HEADS = 16
SWA_KV_HEADS = 2
SWA_REP = SWA_Q_HEADS // SWA_KV_HEADS
SWA_WINDOW = 128
SWA_Q_WIDTH = SWA_Q_HEADS * HEAD_DIM
SWA_KV_WIDTH = SWA_KV_HEADS * HEAD_DIM
N_ALIBI_HEADS = SWA_Q_HEADS + N_DIL * DIL_HEADS
ATTN_BLOCK = 128
N_EXPERTS = 32
TOP_K = 4
SWIGLU_LIMIT = 7.0
SWIGLU_ALPHA = 1.702
LN_EPS = 1e-5
DEPTH = 1
DEEPNORM_ALPHA = (2 * DEPTH) ** 0.25
NEG_INF = -1e30

LANES = 128
SUBLANES = 8
VMEM_LIMIT_BYTES = 56 * 1024 * 1024

A_QKV_W = 3 * DIL_WIDTH
B_Q_OFF = A_QKV_W
B_KV_OFF = B_Q_OFF + SWA_Q_WIDTH
GATE_OFF = B_KV_OFF + 2 * SWA_KV_WIDTH


def _cparams(*sem):
    return pltpu.CompilerParams(dimension_semantics=sem, vmem_limit_bytes=VMEM_LIMIT_BYTES)


def _pack_bf16_pairs(x):
    c = x.shape[1] // 2
    lo = lax.bitcast_convert_type(x[:, :c].astype(BF16).astype(F32), jnp.uint32)
    hi = lax.bitcast_convert_type(x[:, c:].astype(BF16).astype(F32), jnp.uint32)
    return (lo >> 16) | hi


def _unpack_bf16_pairs(w):
    lo = lax.bitcast_convert_type(w << 16, F32)
    hi = lax.bitcast_convert_type(w & jnp.uint32(0xFFFF0000), F32)
    return lo, hi


def _in_proj_kernel(x_ref, w_ref, a0_ref, a1_ref, a2_ref, hq_ref, hkv_ref, g_ref, *scratch, d_model, tm):
    xb = x_ref[...].astype(BF16)
    segments = ((hq_ref, B_Q_OFF, SWA_Q_WIDTH), (hkv_ref, B_KV_OFF, 2 * SWA_KV_WIDTH),
                (g_ref, GATE_OFF, 2 * d_model))
    for out_ref, col0, width in segments:
        for c in range(0, width, 512):
            w = min(512, width - c)
            r = jnp.dot(xb, w_ref[:, col0 + c:col0 + c + w], preferred_element_type=F32)
            out_ref[:, c:c + w] = r.astype(out_ref.dtype)
    for g, a_ref in enumerate((a0_ref, a1_ref, a2_ref)):
        dil = DIL_GROUPS[g][1]
        per = tm // dil
        for part in range(3):
            col0 = part * DIL_WIDTH + g * DIL_OUT
            res = jnp.dot(xb, w_ref[:, col0:col0 + DIL_OUT], preferred_element_type=F32)
            if dil == 1:
                a_ref[0, :, part * DIL_OUT:(part + 1) * DIL_OUT] = res.astype(a_ref.dtype)
                continue
            for half in range(DIL_OUT // LANES):
                stage = scratch[part * (DIL_OUT // LANES) + half]
                stage[...] = res[:, half * LANES:(half + 1) * LANES]
                c0 = part * DIL_OUT + half * LANES
                for r in range(dil):
                    a_ref[r, :, c0:c0 + LANES] = stage[pl.ds(r, per, stride=dil), :].astype(a_ref.dtype)


def _in_proj(x2, w_in_bf, bsz, seq, tm=512):
    n, d = x2.shape
    cols = w_in_bf.shape[1]
    tiles = seq // tm
    row = lambda i: (i, 0)
    dil_spec = lambda dil: pl.BlockSpec((None, dil, tm // dil, 3 * DIL_OUT), lambda i: (i // tiles, 0, i % tiles, 0))
    dil_shape = lambda dil: jax.ShapeDtypeStruct((bsz, dil, seq // dil, 3 * DIL_OUT), BF16)
    dils = [dil for _, dil in DIL_GROUPS]
    return pl.pallas_call(
        functools.partial(_in_proj_kernel, d_model=d, tm=tm),
        grid=(n // tm,),
        in_specs=[pl.BlockSpec((tm, d), row),
                  pl.BlockSpec((d, cols), lambda i: (0, 0), pipeline_mode=pl.Buffered(1))],
        out_specs=[dil_spec(dil) for dil in dils]
        + [pl.BlockSpec((tm, SWA_Q_WIDTH), row), pl.BlockSpec((tm, 2 * SWA_KV_WIDTH), row),
           pl.BlockSpec((tm, 2 * d), row)],
        out_shape=[dil_shape(dil) for dil in dils]
        + [jax.ShapeDtypeStruct((n, SWA_Q_WIDTH), BF16), jax.ShapeDtypeStruct((n, 2 * SWA_KV_WIDTH), BF16),
           jax.ShapeDtypeStruct((n, 2 * d), BF16)],
        scratch_shapes=[pltpu.VMEM((tm, LANES), F32)] * (3 * DIL_OUT // LANES),
        compiler_params=_cparams("parallel"),
        name="in_proj",
    )(x2, w_in_bf)


_NT = (((1,), (1,)), ((), ()))


def _per_head_column(ref, head0, n_heads, scale=1.0):
    head_of_row = lax.broadcasted_iota(I32, (n_heads * ATTN_BLOCK, 1), 0) // ATTN_BLOCK
    col = jnp.zeros((n_heads * ATTN_BLOCK, 1), F32)
    for h in range(n_heads):
        col = jnp.where(head_of_row == h, ref[head0 + h] * scale, col)
    return col


def _store_band_bias(bias_ref, slope_col, max_diff):
    rows = slope_col.shape[0]
    qi = lax.broadcasted_iota(I32, (rows, ATTN_BLOCK), 0) % ATTN_BLOCK
    kj = lax.broadcasted_iota(I32, (rows, ATTN_BLOCK), 1)
    diff_prev = qi - kj + ATTN_BLOCK
    diff_cur = qi - kj
    bias_ref[0] = jnp.where(diff_prev <= max_diff, -slope_col * diff_prev.astype(F32), NEG_INF)
    bias_ref[1] = jnp.where((diff_cur >= 0) & (diff_cur <= max_diff), -slope_col * diff_cur.astype(F32), NEG_INF)


def _band_probs(s_prev, s_cur, sink_col):
    m = jnp.max(jnp.maximum(s_prev, s_cur), axis=-1, keepdims=True)
    if sink_col is not None:
        m = jnp.maximum(m, sink_col)
    return jnp.exp(s_prev - m).astype(BF16), jnp.exp(s_cur - m).astype(BF16), m


def _pair_masks():
    low = lax.broadcasted_iota(I32, (ATTN_BLOCK, LANES), 1) < HEAD_DIM
    scale = HEAD_DIM ** -0.5
    return low, jnp.where(low, scale, 0.0).astype(BF16), jnp.where(low, 0.0, scale).astype(BF16)


def _values_and_ones(v):
    return jnp.concatenate([v, jnp.ones((v.shape[0], LANES), v.dtype)], axis=1)


def _dil_attn_kernel(slope_ref, q_ref, kc_ref, kp_ref, vc_ref, vp_ref, o_ref, lse_ref, bias_ref,
                     *, tq, max_diff, dist_scale):
    first = pl.program_id(2) == 0
    _store_band_bias(bias_ref, _per_head_column(slope_ref, 0, DIL_HEADS, dist_scale), max_diff)
    low, keep_low, keep_high = _pair_masks()
    pair_cols = [slice(p * LANES, (p + 1) * LANES) for p in range(DIL_HEADS // 2)]
    for res, i in [(res, i) for res in range(q_ref.shape[0]) for i in range(tq // ATTN_BLOCK)]:
        rows = slice(i * ATTN_BLOCK, (i + 1) * ATTN_BLOCK)
        prev_rows = slice((i - 1) * ATTN_BLOCK, i * ATTN_BLOCK)
        s_prev, s_cur = [], []
        for c in pair_cols:
            k_prev = kp_ref[res, :, c] if i == 0 else kc_ref[res, prev_rows, c]
            for keep in (keep_low, keep_high):
                q = q_ref[res, rows, c] * keep
                s_prev.append(lax.dot_general(q, k_prev, _NT, preferred_element_type=F32))
                s_cur.append(lax.dot_general(q, kc_ref[res, rows, c], _NT, preferred_element_type=F32))
        s_prev = jnp.concatenate(s_prev, axis=0) + bias_ref[0]
        s_cur = jnp.concatenate(s_cur, axis=0) + bias_ref[1]
        if i == 0:
            s_prev = jnp.where(first, NEG_INF, s_prev)
        p_prev, p_cur, m = _band_probs(s_prev, s_cur, None)
        for p, c in enumerate(pair_cols):
            v_prev = _values_and_ones(vp_ref[res, :, c] if i == 0 else vc_ref[res, prev_rows, c])
            v_cur = _values_and_ones(vc_ref[res, rows, c])
            od, mh = [], []
            for h in (2 * p, 2 * p + 1):
                hr = slice(h * ATTN_BLOCK, (h + 1) * ATTN_BLOCK)
                od.append(jnp.dot(p_prev[hr], v_prev, preferred_element_type=F32)
                          + jnp.dot(p_cur[hr], v_cur, preferred_element_type=F32))
                mh.append(m[hr])
            denom = jnp.where(low, od[0][:, LANES:], od[1][:, LANES:])
            o_ref[res, rows, c] = (jnp.where(low, od[0][:, :LANES], od[1][:, :LANES]) / denom).astype(o_ref.dtype)
            lse_ref[res, rows, c] = jnp.where(low, mh[0], mh[1]) + jnp.log(denom)


def _dil_attention(a_g, slopes_g, g):
    window, dil = DIL_GROUPS[g]
    bsz, _, sub_len, _ = a_g.shape
    rows_per_step = 512
    tq = min(rows_per_step, sub_len)
    res_per_step = rows_per_step // tq
    nqb = tq // ATTN_BLOCK
    cur = lambda part: (lambda b, r, m: (b, r, m, part))
    prev = lambda part: (lambda b, r, m: (b, r, jnp.maximum(m * nqb - 1, 0), part))
    blk = lambda rows, imap: pl.BlockSpec((None, res_per_step, rows, DIL_OUT), imap)
    return pl.pallas_call(
        functools.partial(_dil_attn_kernel, tq=tq, max_diff=window // dil, dist_scale=float(dil)),
        grid=(bsz, dil // res_per_step, sub_len // tq),
        in_specs=[pl.BlockSpec(memory_space=pltpu.SMEM),
                  blk(tq, cur(0)), blk(tq, cur(1)), blk(ATTN_BLOCK, prev(1)),
                  blk(tq, cur(2)), blk(ATTN_BLOCK, prev(2))],
        out_specs=[blk(tq, cur(0)), blk(tq, cur(0))],
        out_shape=[jax.ShapeDtypeStruct((bsz, dil, sub_len, DIL_OUT), BF16),
                   jax.ShapeDtypeStruct((bsz, dil, sub_len, DIL_OUT), F32)],
        scratch_shapes=[pltpu.VMEM((2, DIL_HEADS * ATTN_BLOCK, ATTN_BLOCK), F32)],
        compiler_params=_cparams("parallel", "parallel", "arbitrary"),
        name=f"dil_attn_g{g}",
    )(slopes_g, a_g, a_g, a_g, a_g, a_g)


def _swa_attn_kernel(slope_ref, sink_ref, q_ref, kvc_ref, kvp_ref, o_ref, bias_ref, *, tq):
    first = pl.program_id(1) == 0
    low, keep_low, keep_high = _pair_masks()

    def both_halves(ref, lane0):
        part = ref[:, lane0:lane0 + HEAD_DIM]
        return jnp.concatenate([part, part], axis=1)

    tile = (SWA_REP, ATTN_BLOCK, ATTN_BLOCK)
    stacked = (SWA_REP * ATTN_BLOCK, ATTN_BLOCK)
    qi = lax.broadcasted_iota(I32, (ATTN_BLOCK, ATTN_BLOCK), 0)
    kj = lax.broadcasted_iota(I32, (ATTN_BLOCK, ATTN_BLOCK), 1)
    from_prev = kj > qi
    diff = jnp.where(from_prev, qi - kj + ATTN_BLOCK, qi - kj).astype(F32)
    for kvh in range(SWA_KV_HEADS):
        head0 = kvh * SWA_REP
        for r in range(SWA_REP):
            bias_ref[r] = -slope_ref[head0 + r] * diff
        sink_col = _per_head_column(sink_ref, head0, SWA_REP)
        k_cur_all, k_first = both_halves(kvc_ref, kvh * HEAD_DIM), both_halves(kvp_ref, kvh * HEAD_DIM)
        v_lane0 = SWA_KV_WIDTH + kvh * HEAD_DIM
        v_cur_all, v_first = both_halves(kvc_ref, v_lane0), both_halves(kvp_ref, v_lane0)
        pair_cols = [slice((head0 + 2 * p) * HEAD_DIM, (head0 + 2 * p + 2) * HEAD_DIM) for p in range(SWA_REP // 2)]
        for i in range(tq // ATTN_BLOCK):
            rows = slice(i * ATTN_BLOCK, (i + 1) * ATTN_BLOCK)
            if i == 0:
                k_win = jnp.concatenate([k_first, k_cur_all[rows]], axis=0)
                v_win = jnp.concatenate([v_first, v_cur_all[rows]], axis=0)
            else:
                win = slice((i - 1) * ATTN_BLOCK, (i + 1) * ATTN_BLOCK)
                k_win, v_win = k_cur_all[win], v_cur_all[win]
            q = jnp.concatenate([q_ref[rows, c] * keep for c in pair_cols for keep in (keep_low, keep_high)], axis=0)
            s2 = lax.dot_general(q, k_win, _NT, preferred_element_type=F32)
            s = jnp.where(from_prev, s2[:, :ATTN_BLOCK].reshape(tile), s2[:, ATTN_BLOCK:].reshape(tile)) \
                + bias_ref[...]
            if i == 0:
                s = jnp.where(from_prev & first, NEG_INF, s)
            s = s.reshape(stacked)
            m = jnp.maximum(jnp.max(s, axis=-1, keepdims=True), sink_col)
            p = jnp.exp(s - m).reshape(tile)
            p2 = jnp.concatenate([jnp.where(from_prev, p, 0.0).reshape(stacked).astype(BF16),
                                  jnp.where(from_prev, 0.0, p).reshape(stacked).astype(BF16)], axis=1)
            od = jnp.dot(p2, _values_and_ones(v_win), preferred_element_type=F32)
            o = od[:, :LANES] / (od[:, LANES:] + jnp.exp(sink_col - m))
            for p, c in enumerate(pair_cols):
                even = o[2 * p * ATTN_BLOCK:(2 * p + 1) * ATTN_BLOCK]
                odd = o[(2 * p + 1) * ATTN_BLOCK:(2 * p + 2) * ATTN_BLOCK]
                o_ref[rows, c] = jnp.where(low, even, odd).astype(o_ref.dtype)


def _swa_attention(hq, hkv, slopes_b, sinks, bsz, seq, tq=512):
    nqb = tq // ATTN_BLOCK
    hq3 = hq.reshape(bsz, seq, SWA_Q_WIDTH)
    hkv3 = hkv.reshape(bsz, seq, 2 * SWA_KV_WIDTH)
    smem = pl.BlockSpec(memory_space=pltpu.SMEM)
    out = pl.pallas_call(
        functools.partial(_swa_attn_kernel, tq=tq),
        grid=(bsz, seq // tq),
        in_specs=[smem, smem,
                  pl.BlockSpec((None, tq, SWA_Q_WIDTH), lambda b, m: (b, m, 0)),
                  pl.BlockSpec((None, tq, 2 * SWA_KV_WIDTH), lambda b, m: (b, m, 0)),
                  pl.BlockSpec((None, ATTN_BLOCK, 2 * SWA_KV_WIDTH),
                               lambda b, m: (b, jnp.maximum(m * nqb - 1, 0), 0))],
        out_specs=pl.BlockSpec((None, tq, SWA_Q_WIDTH), lambda b, m: (b, m, 0)),
        out_shape=jax.ShapeDtypeStruct((bsz, seq, SWA_Q_WIDTH), BF16),
        scratch_shapes=[pltpu.VMEM((SWA_REP, ATTN_BLOCK, ATTN_BLOCK), F32)],
        compiler_params=_cparams("parallel", "arbitrary"),
        name="swa_attn",
    )(slopes_b, sinks, hq3, hkv3, hkv3)
    return out.reshape(bsz * seq, SWA_Q_WIDTH)


def _sigmoid(x):
    return 0.5 * (jnp.tanh(0.5 * x) + 1.0)


def _layer_norm(z, g, b):
    mu = jnp.mean(z, axis=-1, keepdims=True)
    zc = z - mu
    var = jnp.mean(zc * zc, axis=-1, keepdims=True)
    return zc * lax.rsqrt(var + LN_EPS) * g + b


MIX_OUT_CHAINS = 1


def _mix_out_kernel(x_ref, o0_ref, o1_ref, o2_ref, l0_ref, l1_ref, l2_ref, ob_ref, g_ref,
                    wpa_ref, wpb_ref, wo_ref, lng_ref, lnb_ref, rw_ref, rb_ref,
                    y_ref, ypk_ref, idx_ref, gate_ref, *scratch, d_model):
    def natural(ref, stages):
        dil, per, _ = ref.shape
        if dil == 1:
            return ref[0].astype(F32)
        for half, stage in enumerate(stages):
            for r in range(dil):
                stage[pl.ds(r, per, stride=dil), :] = ref[r, :, half * LANES:(half + 1) * LANES].astype(F32)
        return jnp.concatenate([stage[...] for stage in stages], axis=1)

    o_nat = natural(o0_ref, None), natural(o1_ref, scratch[0:2]), natural(o2_ref, scratch[2:4])
    l_nat = natural(l0_ref, None), natural(l1_ref, scratch[4:6]), natural(l2_ref, scratch[6:8])

    def chain(rows):
        (o0, o1, o2), (l0, l1, l2) = [a[rows] for a in o_nat], [a[rows] for a in l_nat]
        lm = jnp.maximum(jnp.maximum(l0, l1), l2)
        e0, e1, e2 = jnp.exp(l0 - lm), jnp.exp(l1 - lm), jnp.exp(l2 - lm)
        out_a = (e0 * o0 + e1 * o1 + e2 * o2) / (e0 + e1 + e2)
        pa = jnp.dot(out_a.astype(BF16), wpa_ref[...], preferred_element_type=F32)
        pb = jnp.dot(ob_ref[rows, :], wpb_ref[...], preferred_element_type=F32)
        ga = _sigmoid(g_ref[rows, :d_model].astype(F32))
        gb = _sigmoid(g_ref[rows, d_model:].astype(F32))
        merged = ga * pa + gb * pb
        mix = jnp.dot(merged.astype(BF16), wo_ref[...], preferred_element_type=F32)
        y = _layer_norm(DEEPNORM_ALPHA * x_ref[rows, :] + mix, lng_ref[...], lnb_ref[...])
        y_ref[rows, :] = y
        ypk_ref[rows, :] = _pack_bf16_pairs(y)

        logits = lax.dot_general(rw_ref[...], y.astype(BF16), _NT, preferred_element_type=F32) + rb_ref[...]
        expert = lax.broadcasted_iota(I32, logits.shape, 0)
        slot = lax.broadcasted_iota(I32, (SUBLANES, logits.shape[1]), 0)
        idx_out = jnp.zeros(slot.shape, I32)
        val_out = jnp.full(slot.shape, -jnp.inf, F32)
        for k in range(TOP_K):
            top = jnp.max(logits, axis=0, keepdims=True)
            top_idx = jnp.min(jnp.where(logits == top, expert, N_EXPERTS), axis=0, keepdims=True)
            idx_out = jnp.where(slot == k, top_idx, idx_out)
            val_out = jnp.where(slot == k, top, val_out)
            logits = jnp.where(expert == top_idx, -jnp.inf, logits)
        ev = jnp.exp(val_out - jnp.max(val_out, axis=0, keepdims=True))
        idx_ref[:, rows] = idx_out
        gate_ref[:, rows] = ev / jnp.sum(ev, axis=0, keepdims=True)

    tm = x_ref.shape[0]
    for part in range(MIX_OUT_CHAINS):
        chain(slice(part * tm // MIX_OUT_CHAINS, (part + 1) * tm // MIX_OUT_CHAINS))


def _mix_out(x2, o_g, lse_g, out_b, gates_h, wpa, wpb, wo, ln_g, ln_b, rw_t, rb_col, seq, tm=512):
    n, d = x2.shape
    tiles = seq // tm
    row = lambda i: (i, 0)
    const = lambda i: (0, 0)
    rb = lambda w: pl.BlockSpec((tm, w), row)
    full = lambda a: pl.BlockSpec(a.shape, const)
    dil_specs = [pl.BlockSpec((None, dil, tm // dil, DIL_OUT), lambda i: (i // tiles, 0, i % tiles, 0))
                 for _, dil in DIL_GROUPS]
    return pl.pallas_call(
        functools.partial(_mix_out_kernel, d_model=d),
        grid=(n // tm,),
        in_specs=[rb(d)] + dil_specs + dil_specs + [rb(SWA_Q_WIDTH), rb(2 * d),
                  full(wpa), full(wpb), full(wo), full(ln_g), full(ln_b), full(rw_t), full(rb_col)],
        out_specs=[rb(d), rb(d // 2), pl.BlockSpec((SUBLANES, tm), lambda i: (0, i)),
                   pl.BlockSpec((SUBLANES, tm), lambda i: (0, i))],
        out_shape=[jax.ShapeDtypeStruct((n, d), F32), jax.ShapeDtypeStruct((n, d // 2), jnp.uint32),
                   jax.ShapeDtypeStruct((SUBLANES, n), I32), jax.ShapeDtypeStruct((SUBLANES, n), F32)],
        scratch_shapes=[pltpu.VMEM((tm, LANES), F32)] * (4 * DIL_OUT // LANES),
        compiler_params=_cparams("parallel"),
        name="mix_out",
    )(x2, *o_g, *lse_g, out_b, gates_h, wpa, wpb, wo, ln_g, ln_b, rw_t, rb_col)


def _token_mixer_ln1(x2, w_in, sinks, w_proj_a, w_proj_b, w_out, ln1_g, ln1_b, router_w, router_b, bsz, seq):
    n, d = x2.shape
    heads = jnp.arange(1, N_ALIBI_HEADS + 1, dtype=F32)
    slopes = jnp.exp2(-8.0 * heads / N_ALIBI_HEADS)
    *a_g, hq, hkv, gates_h = _in_proj(x2, w_in.astype(BF16), bsz, seq)
    o_g, lse_g = [], []
    for g in range(N_DIL):
        sl = slopes[SWA_Q_HEADS + g * DIL_HEADS:SWA_Q_HEADS + (g + 1) * DIL_HEADS]
        o, lse = _dil_attention(a_g[g], sl, g)
        o_g.append(o)
        lse_g.append(lse)
    out_b = _swa_attention(hq, hkv, slopes[:SWA_Q_HEADS], sinks.astype(F32), bsz, seq)
    return _mix_out(x2, o_g, lse_g, out_b, gates_h, w_proj_a.astype(BF16), w_proj_b.astype(BF16),
                    w_out.astype(BF16), ln1_g.reshape(1, d).astype(F32), ln1_b.reshape(1, d).astype(F32),
                    router_w.T.astype(BF16), router_b.reshape(N_EXPERTS, 1).astype(F32), seq)


def _route_kernel(idx_ref, dest_ref, cnt_ref, carry_ref, pstart_ref, *, expert_block):
    phase, i = pl.program_id(0), pl.program_id(1)
    idx = idx_ref[...]
    t = idx.shape[1]
    expert = lax.broadcasted_iota(I32, (N_EXPERTS, t), 0)
    onehot = [expert == idx[k:k + 1, :] for k in range(TOP_K)]
    multi = sum(oh.astype(F32) for oh in onehot)
    tile_cnt = jnp.sum(multi, axis=1, keepdims=True).astype(I32)

    @pl.when((phase == 0) & (i == 0))
    def _():
        carry_ref[...] = jnp.zeros_like(carry_ref)

    @pl.when((phase == 1) & (i == 0))
    def _():
        counts = carry_ref[...]
        blocks = (counts + (expert_block - 1)) // expert_block
        er = lax.broadcasted_iota(I32, (N_EXPERTS, N_EXPERTS), 0)
        ec = lax.broadcasted_iota(I32, (N_EXPERTS, N_EXPERTS), 1)
        below = (ec < er).astype(BF16)
        prefix = lambda v: jnp.dot(below, v.astype(F32).astype(BF16), preferred_element_type=F32).astype(I32)
        pstart_ref[...] = (prefix(blocks >> 6) * 64 + prefix(blocks & 63)) * expert_block
        cnt_ref[...] = counts
        carry_ref[...] = jnp.zeros_like(carry_ref)

    @pl.when(phase == 1)
    def _():
        r = lax.broadcasted_iota(I32, (t, t), 0)
        c = lax.broadcasted_iota(I32, (t, t), 1)
        earlier = (r < c).astype(BF16)
        cum = jnp.dot(multi.astype(BF16), earlier, preferred_element_type=F32).astype(I32)
        base = cum + carry_ref[:, 0:1] + pstart_ref[:, 0:1]
        slot = lax.broadcasted_iota(I32, (SUBLANES, t), 0)
        dest = jnp.zeros((SUBLANES, t), I32)
        for k in range(TOP_K):
            d_k = jnp.sum(jnp.where(onehot[k], base, 0), axis=0, keepdims=True)
            dest = jnp.where(slot == k, d_k, dest)
        dest_ref[...] = dest

    carry_ref[...] = carry_ref[...] + tile_cnt


def _route(idx_t, expert_block, tile=512):
    n = idx_t.shape[1]
    tile = min(tile, n)
    table = jax.ShapeDtypeStruct((N_EXPERTS, LANES), I32)
    return pl.pallas_call(
        functools.partial(_route_kernel, expert_block=expert_block),
        grid=(2, n // tile),
        in_specs=[pl.BlockSpec((SUBLANES, tile), lambda p, i: (0, i))],
        out_specs=[pl.BlockSpec((SUBLANES, tile), lambda p, i: (0, i * p)),
                   pl.BlockSpec(table.shape, lambda p, i: (0, 0))],
        out_shape=[jax.ShapeDtypeStruct((SUBLANES, n), I32), table],
        scratch_shapes=[pltpu.VMEM(table.shape, I32), pltpu.VMEM(table.shape, I32)],
        compiler_params=_cparams("arbitrary", "arbitrary"),
        name="route",
    )(idx_t)


def _sc_dispatch(dest_t, y, rows, window=64):
    n, d = y.shape
    info = plsc.get_sparse_core_info()
    n_cores = info.num_cores
    per_worker = n // (n_cores * info.num_subcores)
    mesh = plsc.VectorSubcoreMesh(core_axis_name="c", subcore_axis_name="s")

    @functools.partial(pl.kernel, mesh=mesh, out_type=jax.ShapeDtypeStruct((rows, d), y.dtype),
                       scratch_types=[pltpu.VMEM((TOP_K, window), I32), pltpu.VMEM((window, d), y.dtype)],
                       name="sc_dispatch")
    def body(dest_hbm, y_hbm, xs_hbm, idx_v, rows_v):
        base = (lax.axis_index("s") * n_cores + lax.axis_index("c")) * per_worker

        @pl.loop(0, per_worker // window)
        def _(c):
            t0 = base + c * window
            for k in range(TOP_K):
                pltpu.sync_copy(dest_hbm.at[k, pl.ds(t0, window)], idx_v.at[k])
            pltpu.sync_copy(y_hbm.at[pl.ds(t0, window)], rows_v)
            for k in range(TOP_K):
                pltpu.sync_copy(rows_v, xs_hbm.at[idx_v.at[k]])

    return body(dest_t, y)


def _expert_kernel(be_ref, nu_ref, nv_ref, x_ref, wg_ref, bg_ref, wu_ref, bu_ref, wd_ref, bd_ref, o_ref,
                   wg_bf, wu_bf, wd_bf):
    j = pl.program_id(0)
    used = j < nu_ref[0]
    new_expert = (j == 0) | (be_ref[j] != be_ref[jnp.maximum(j - 1, 0)])

    @pl.when(used & new_expert)
    def _():
        chunk = 128
        for src, dst in ((wg_ref, wg_bf), (wu_ref, wu_bf), (wd_ref, wd_bf)):
            for r in range(0, src.shape[0], chunk):
                dst[r:r + chunk, :] = src[r:r + chunk, :].astype(BF16)

    @pl.when(used)
    def _():
        row = lax.broadcasted_iota(I32, x_ref.shape, 0)
        x_lo, x_hi = _unpack_bf16_pairs(jnp.where(row < nv_ref[j], x_ref[...], jnp.uint32(0)))
        xb = jnp.concatenate([x_lo.astype(BF16), x_hi.astype(BF16)], axis=1)
        gt = jnp.dot(xb, wg_bf[...], preferred_element_type=F32) + bg_ref[...]
        up = jnp.dot(xb, wu_bf[...], preferred_element_type=F32) + bu_ref[...]
        gt = jnp.minimum(gt, SWIGLU_LIMIT)
        up = jnp.clip(up, -SWIGLU_LIMIT, SWIGLU_LIMIT)
        hdn = gt * _sigmoid(SWIGLU_ALPHA * gt) * (up + 1.0)
        o_ref[...] = _pack_bf16_pairs(jnp.dot(hdn.astype(BF16), wd_bf[...], preferred_element_type=F32)
                                      + bd_ref[...])

    @pl.when(jnp.logical_not(used))
    def _():
        o_ref[...] = jnp.zeros_like(o_ref)


def _experts(block_e, n_used, n_valid, xs, wg, bg, wu, bu, wd, bd, expert_block):
    rows, _ = xs.shape
    e, d, f = wg.shape
    xmap = lambda j, be, nu, nv: (jnp.minimum(j, nu[0] - 1), 0)
    wmap = lambda j, be, nu, nv: (be[j], 0, 0)
    grid_spec = pltpu.PrefetchScalarGridSpec(
        num_scalar_prefetch=3,
        grid=(rows // expert_block,),
        in_specs=[pl.BlockSpec((expert_block, d // 2), xmap),
                  pl.BlockSpec((None, d, f), wmap), pl.BlockSpec((None, 1, f), wmap),
                  pl.BlockSpec((None, d, f), wmap), pl.BlockSpec((None, 1, f), wmap),
                  pl.BlockSpec((None, f, d), wmap), pl.BlockSpec((None, 1, d), wmap)],
        out_specs=pl.BlockSpec((expert_block, d // 2), lambda j, be, nu, nv: (j, 0)),
        scratch_shapes=[pltpu.VMEM((d, f), BF16), pltpu.VMEM((d, f), BF16), pltpu.VMEM((f, d), BF16)],
    )
    return pl.pallas_call(
        _expert_kernel,
        grid_spec=grid_spec,
        out_shape=jax.ShapeDtypeStruct((rows, d // 2), jnp.uint32),
        compiler_params=_cparams("arbitrary"),
        name="experts",
    )(block_e, n_used, n_valid, xs, wg, bg.reshape(e, 1, f), wu, bu.reshape(e, 1, f), wd, bd.reshape(e, 1, d))


def _sc_gather(dest_t, ys, n, window=64):
    _, d = ys.shape
    info = plsc.get_sparse_core_info()
    n_cores = info.num_cores
    per_worker = n // (n_cores * info.num_subcores)
    mesh = plsc.VectorSubcoreMesh(core_axis_name="c", subcore_axis_name="s")

    n_chunks = per_worker // window
    n_buf = 2

    @functools.partial(pl.kernel, mesh=mesh, out_type=jax.ShapeDtypeStruct((TOP_K, n, d), ys.dtype),
                       scratch_types=[pltpu.VMEM((TOP_K, per_worker), I32)]
                       + [pltpu.VMEM((window, d), ys.dtype)] * n_buf + [pltpu.SemaphoreType.DMA] * (2 * n_buf),
                       name="sc_gather")
    def body(dest_hbm, ys_hbm, out_hbm, idx_v, buf0, buf1, gsem0, gsem1, ssem0, ssem1):
        bufs, gsem, ssem = (buf0, buf1), (gsem0, gsem1), (ssem0, ssem1)
        base = (lax.axis_index("s") * n_cores + lax.axis_index("c")) * per_worker
        for k in range(TOP_K):
            pltpu.sync_copy(dest_hbm.at[k, pl.ds(base, per_worker)], idx_v.at[k])

        def gather(c, k):
            return pltpu.make_async_copy(ys_hbm.at[idx_v.at[k, pl.ds(c * window, window)]], bufs[k % n_buf],
                                         gsem[k % n_buf])

        def store(c, k):
            return pltpu.make_async_copy(bufs[k % n_buf], out_hbm.at[k, pl.ds(base + c * window, window)],
                                         ssem[k % n_buf])

        gather(0, 0).start()

        @pl.loop(0, n_chunks)
        def _(c):
            for k in range(TOP_K):
                if k > 0:
                    store(c, k - 1).wait()
                    if k + 1 < TOP_K:
                        gather(c, k + 1).start()
                    else:
                        @pl.when(c + 1 < n_chunks)
                        def _():
                            gather(c + 1, 0).start()
                else:
                    @pl.when(c > 0)
                    def _():
                        store(c - 1, TOP_K - 1).wait()
                    gather(c, 1).start()
                gather(c, k).wait()
                store(c, k).start()

        store(n_chunks - 1, TOP_K - 1).wait()

    return body(dest_t, ys)


def _combine_dense_kernel(gate_ref, y_ref, lng_ref, lnb_ref, g_ref, o_ref):
    gates = gate_ref[...].T
    ffn_lo = ffn_hi = None
    for k in range(TOP_K):
        lo, hi = _unpack_bf16_pairs(g_ref[k])
        gate = gates[:, k:k + 1]
        ffn_lo = gate * lo if k == 0 else ffn_lo + gate * lo
        ffn_hi = gate * hi if k == 0 else ffn_hi + gate * hi
    ffn = jnp.concatenate([ffn_lo, ffn_hi], axis=1)
    o_ref[...] = _layer_norm(DEEPNORM_ALPHA * y_ref[...] + ffn, lng_ref[...], lnb_ref[...])


def _combine_dense(gates, y, ln_g, ln_b, gathered, tile=512):
    n, d = y.shape
    tile = min(tile, n)
    row = lambda i: (i, 0)
    const = lambda i: (0, 0)
    return pl.pallas_call(
        _combine_dense_kernel,
        grid=(n // tile,),
        in_specs=[pl.BlockSpec((SUBLANES, tile), lambda i: (0, i)), pl.BlockSpec((tile, d), row),
                  pl.BlockSpec((1, d), const), pl.BlockSpec((1, d), const),
                  pl.BlockSpec((TOP_K, tile, d // 2), lambda i: (0, i, 0))],
        out_specs=pl.BlockSpec((tile, d), row),
        out_shape=jax.ShapeDtypeStruct((n, d), F32),
        compiler_params=_cparams("parallel"),
        name="combine_dense",
    )(gates, y, ln_g, ln_b, gathered)


MOE_ROW_BLOCK = 512


def _moe_ln2(y, y_packed, idx, gates, w_gate, b_gate, w_up, b_up, w_down, b_down, ln2_g, ln2_b):
    n, d = y.shape
    n_blocks = n * TOP_K // MOE_ROW_BLOCK + N_EXPERTS
    rows = n_blocks * MOE_ROW_BLOCK
    dest_t, counts = _route(idx, MOE_ROW_BLOCK)
    cnt = counts[:, 0]
    padded = (cnt + MOE_ROW_BLOCK - 1) // MOE_ROW_BLOCK * MOE_ROW_BLOCK
    p_end = jnp.cumsum(padded)
    block_row0 = jnp.arange(n_blocks, dtype=I32) * MOE_ROW_BLOCK
    block_e = jnp.minimum(jnp.sum(p_end[None, :] <= block_row0[:, None], axis=1), N_EXPERTS - 1).astype(I32)
    n_used = (p_end[-1:] // MOE_ROW_BLOCK).astype(I32)
    n_valid = jnp.clip(cnt[block_e] - (block_row0 - (p_end - padded)[block_e]), 0, MOE_ROW_BLOCK).astype(I32)
    xs = _sc_dispatch(dest_t, y_packed, rows)
    ys = _experts(block_e, n_used, n_valid, xs, w_gate, b_gate.astype(F32), w_up, b_up.astype(F32),
                  w_down, b_down.astype(F32), MOE_ROW_BLOCK)
    gathered = _sc_gather(dest_t, ys, n)
    return _combine_dense(gates, y, ln2_g.reshape(1, d).astype(F32), ln2_b.reshape(1, d).astype(F32), gathered)


def kernel(x, w_in, sinks, w_proj_a, w_proj_b, w_out, ln1_g, ln1_b, router_w, router_b,
           w_gate, b_gate, w_up, b_up, w_down, b_down, ln2_g, ln2_b):
    bsz, seq, d = x.shape
    h = x.reshape(bsz * seq, d)
    for l in range(w_in.shape[0]):
        y, y_packed, idx, gates = _token_mixer_ln1(h, w_in[l], sinks[l], w_proj_a[l], w_proj_b[l], w_out[l],
                                                   ln1_g[l], ln1_b[l], router_w[l], router_b[l], bsz, seq)
        h = _moe_ln2(y, y_packed, idx, gates, w_gate[l], b_gate[l], w_up[l], b_up[l], w_down[l], b_down[l],
                     ln2_g[l], ln2_b[l])
    return h.reshape(bsz, seq, d)
```

```python
import functools

import jax
import jax.numpy as jnp
from jax import lax
from jax.experimental import pallas as pl
from jax.experimental.pallas import tpu as pltpu
from jax.experimental.pallas import tpu_sc as plsc

F32 = jnp.float32
BF16 = jnp.bfloat16
I32 = jnp.int32

HEAD_DIM = 64
DIL_GROUPS = ((128, 1), (512, 4), (2048, 16))
DIL_HEADS = 4
N_DIL = len(DIL_GROUPS)
DIL_OUT = DIL_HEADS * HEAD_DIM
DIL_WIDTH = N_DIL * DIL_OUT
SWA_Q_HEADS = 16
SWA_KV_HEADS = 2
SWA_REP = SWA_Q_HEADS // SWA_KV_HEADS
SWA_WINDOW = 128
SWA_Q_WIDTH = SWA_Q_HEADS * HEAD_DIM
SWA_KV_WIDTH = SWA_KV_HEADS * HEAD_DIM
N_ALIBI_HEADS = SWA_Q_HEADS + N_DIL * DIL_HEADS
ATTN_BLOCK = 128
N_EXPERTS = 32
TOP_K = 4
SWIGLU_LIMIT = 7.0
SWIGLU_ALPHA = 1.702
LN_EPS = 1e-5
DEPTH = 1
DEEPNORM_ALPHA = (2 * DEPTH) ** 0.25
NEG_INF = -1e30

LANES = 128
SUBLANES = 8
VMEM_LIMIT_BYTES = 56 * 1024 * 1024

A_QKV_W = 3 * DIL_WIDTH
B_Q_OFF = A_QKV_W
B_KV_OFF = B_Q_OFF + SWA_Q_WIDTH
GATE_OFF = B_KV_OFF + 2 * SWA_KV_WIDTH


def _cparams(*sem):
    return pltpu.CompilerParams(dimension_semantics=sem, vmem_limit_bytes=VMEM_LIMIT_BYTES)


def _pack_bf16_pairs(x):
    c = x.shape[1] // 2
    lo = lax.bitcast_convert_type(x[:, :c].astype(BF16).astype(F32), jnp.uint32)
    hi = lax.bitcast_convert_type(x[:, c:].astype(BF16).astype(F32), jnp.uint32)
    return (lo >> 16) | hi


def _unpack_bf16_pairs(w):
    lo = lax.bitcast_convert_type(w << 16, F32)
    hi = lax.bitcast_convert_type(w & jnp.uint32(0xFFFF0000), F32)
    return lo, hi


def _in_proj_kernel(x_ref, w_ref, a0_ref, a1_ref, a2_ref, hq_ref, hkv_ref, g_ref, *scratch, d_model, tm):
    xb = x_ref[...].astype(BF16)
    segments = ((hq_ref, B_Q_OFF, SWA_Q_WIDTH), (hkv_ref, B_KV_OFF, 2 * SWA_KV_WIDTH),
                (g_ref, GATE_OFF, 2 * d_model))
    for out_ref, col0, width in segments:
        for c in range(0, width, 512):
            w = min(512, width - c)
            r = jnp.dot(xb, w_ref[:, col0 + c:col0 + c + w], preferred_element_type=F32)
            out_ref[:, c:c + w] = r.astype(out_ref.dtype)
    for g, a_ref in enumerate((a0_ref, a1_ref, a2_ref)):
        dil = DIL_GROUPS[g][1]
        per = tm // dil
        for part in range(3):
            col0 = part * DIL_WIDTH + g * DIL_OUT
            res = jnp.dot(xb, w_ref[:, col0:col0 + DIL_OUT], preferred_element_type=F32)
            if dil == 1:
                a_ref[0, :, part * DIL_OUT:(part + 1) * DIL_OUT] = res.astype(a_ref.dtype)
                continue
            for half in range(DIL_OUT // LANES):
                stage = scratch[part * (DIL_OUT // LANES) + half]
                stage[...] = res[:, half * LANES:(half + 1) * LANES]
                c0 = part * DIL_OUT + half * LANES
                for r in range(dil):
                    a_ref[r, :, c0:c0 + LANES] = stage[pl.ds(r, per, stride=dil), :].astype(a_ref.dtype)


def _in_proj(x2, w_in_bf, bsz, seq, tm=512):
    n, d = x2.shape
    cols = w_in_bf.shape[1]
    tiles = seq // tm
    row = lambda i: (i, 0)
    dil_spec = lambda dil: pl.BlockSpec((None, dil, tm // dil, 3 * DIL_OUT), lambda i: (i // tiles, 0, i % tiles, 0))
    dil_shape = lambda dil: jax.ShapeDtypeStruct((bsz, dil, seq // dil, 3 * DIL_OUT), BF16)
    dils = [dil for _, dil in DIL_GROUPS]
    return pl.pallas_call(
        functools.partial(_in_proj_kernel, d_model=d, tm=tm),
        grid=(n // tm,),
        in_specs=[pl.BlockSpec((tm, d), row),
                  pl.BlockSpec((d, cols), lambda i: (0, 0), pipeline_mode=pl.Buffered(1))],
        out_specs=[dil_spec(dil) for dil in dils]
        + [pl.BlockSpec((tm, SWA_Q_WIDTH), row), pl.BlockSpec((tm, 2 * SWA_KV_WIDTH), row),
           pl.BlockSpec((tm, 2 * d), row)],
        out_shape=[dil_shape(dil) for dil in dils]
        + [jax.ShapeDtypeStruct((n, SWA_Q_WIDTH), BF16), jax.ShapeDtypeStruct((n, 2 * SWA_KV_WIDTH), BF16),
           jax.ShapeDtypeStruct((n, 2 * d), BF16)],
        scratch_shapes=[pltpu.VMEM((tm, LANES), F32)] * (3 * DIL_OUT // LANES),
        compiler_params=_cparams("parallel"),
        name="in_proj",
    )(x2, w_in_bf)


_NT = (((1,), (1,)), ((), ()))


def _per_head_column(ref, head0, n_heads, scale=1.0):
    head_of_row = lax.broadcasted_iota(I32, (n_heads * ATTN_BLOCK, 1), 0) // ATTN_BLOCK
    col = jnp.zeros((n_heads * ATTN_BLOCK, 1), F32)
    for h in range(n_heads):
        col = jnp.where(head_of_row == h, ref[head0 + h] * scale, col)
    return col


def _store_band_bias(bias_ref, slope_col, max_diff):
    rows = slope_col.shape[0]
    qi = lax.broadcasted_iota(I32, (rows, ATTN_BLOCK), 0) % ATTN_BLOCK
    kj = lax.broadcasted_iota(I32, (rows, ATTN_BLOCK), 1)
    diff_prev = qi - kj + ATTN_BLOCK
    diff_cur = qi - kj
    bias_ref[0] = jnp.where(diff_prev <= max_diff, -slope_col * diff_prev.astype(F32), NEG_INF)
    bias_ref[1] = jnp.where((diff_cur >= 0) & (diff_cur <= max_diff), -slope_col * diff_cur.astype(F32), NEG_INF)


def _band_probs(s_prev, s_cur, sink_col):
    m = jnp.max(jnp.maximum(s_prev, s_cur), axis=-1, keepdims=True)
    if sink_col is not None:
        m = jnp.maximum(m, sink_col)
    return jnp.exp(s_prev - m).astype(BF16), jnp.exp(s_cur - m).astype(BF16), m


def _pair_masks():
    low = lax.broadcasted_iota(I32, (ATTN_BLOCK, LANES), 1) < HEAD_DIM
    scale = HEAD_DIM ** -0.5
    return low, jnp.where(low, scale, 0.0).astype(BF16), jnp.where(low, 0.0, scale).astype(BF16)


def _values_and_ones(v):
    return jnp.concatenate([v, jnp.ones((v.shape[0], LANES), v.dtype)], axis=1)


def _dil_attn_kernel(slope_ref, q_ref, kc_ref, kp_ref, vc_ref, vp_ref, o_ref, lse_ref, bias_ref,
                     *, tq, max_diff, dist_scale):
    first = pl.program_id(2) == 0
    _store_band_bias(bias_ref, _per_head_column(slope_ref, 0, DIL_HEADS, dist_scale), max_diff)
    low, keep_low, keep_high = _pair_masks()
    pair_cols = [slice(p * LANES, (p + 1) * LANES) for p in range(DIL_HEADS // 2)]
    for res, i in [(res, i) for res in range(q_ref.shape[0]) for i in range(tq // ATTN_BLOCK)]:
        rows = slice(i * ATTN_BLOCK, (i + 1) * ATTN_BLOCK)
        prev_rows = slice((i - 1) * ATTN_BLOCK, i * ATTN_BLOCK)
        s_prev, s_cur = [], []
        for c in pair_cols:
            k_prev = kp_ref[res, :, c] if i == 0 else kc_ref[res, prev_rows, c]
            for keep in (keep_low, keep_high):
                q = q_ref[res, rows, c] * keep
                s_prev.append(lax.dot_general(q, k_prev, _NT, preferred_element_type=F32))
                s_cur.append(lax.dot_general(q, kc_ref[res, rows, c], _NT, preferred_element_type=F32))
        s_prev = jnp.concatenate(s_prev, axis=0) + bias_ref[0]
        s_cur = jnp.concatenate(s_cur, axis=0) + bias_ref[1]
        if i == 0:
            s_prev = jnp.where(first, NEG_INF, s_prev)
        p_prev, p_cur, m = _band_probs(s_prev, s_cur, None)
        for p, c in enumerate(pair_cols):
            v_prev = _values_and_ones(vp_ref[res, :, c] if i == 0 else vc_ref[res, prev_rows, c])
            v_cur = _values_and_ones(vc_ref[res, rows, c])
            od, mh = [], []
            for h in (2 * p, 2 * p + 1):
                hr = slice(h * ATTN_BLOCK, (h + 1) * ATTN_BLOCK)
                od.append(jnp.dot(p_prev[hr], v_prev, preferred_element_type=F32)
                          + jnp.dot(p_cur[hr], v_cur, preferred_element_type=F32))
                mh.append(m[hr])
            denom = jnp.where(low, od[0][:, LANES:], od[1][:, LANES:])
            o_ref[res, rows, c] = (jnp.where(low, od[0][:, :LANES], od[1][:, :LANES]) / denom).astype(o_ref.dtype)
            lse_ref[res, rows, c] = jnp.where(low, mh[0], mh[1]) + jnp.log(denom)


def _dil_attention(a_g, slopes_g, g):
    window, dil = DIL_GROUPS[g]
    bsz, _, sub_len, _ = a_g.shape
    rows_per_step = 512
    tq = min(rows_per_step, sub_len)
    res_per_step = rows_per_step // tq
    nqb = tq // ATTN_BLOCK
    cur = lambda part: (lambda b, r, m: (b, r, m, part))
    prev = lambda part: (lambda b, r, m: (b, r, jnp.maximum(m * nqb - 1, 0), part))
    blk = lambda rows, imap: pl.BlockSpec((None, res_per_step, rows, DIL_OUT), imap)
    return pl.pallas_call(
        functools.partial(_dil_attn_kernel, tq=tq, max_diff=window // dil, dist_scale=float(dil)),
        grid=(bsz, dil // res_per_step, sub_len // tq),
        in_specs=[pl.BlockSpec(memory_space=pltpu.SMEM),
                  blk(tq, cur(0)), blk(tq, cur(1)), blk(ATTN_BLOCK, prev(1)),
                  blk(tq, cur(2)), blk(ATTN_BLOCK, prev(2))],
        out_specs=[blk(tq, cur(0)), blk(tq, cur(0))],
        out_shape=[jax.ShapeDtypeStruct((bsz, dil, sub_len, DIL_OUT), BF16),
                   jax.ShapeDtypeStruct((bsz, dil, sub_len, DIL_OUT), F32)],
        scratch_shapes=[pltpu.VMEM((2, DIL_HEADS * ATTN_BLOCK, ATTN_BLOCK), F32)],
        compiler_params=_cparams("parallel", "parallel", "arbitrary"),
        name=f"dil_attn_g{g}",
    )(slopes_g, a_g, a_g, a_g, a_g, a_g)


def _swa_attn_kernel(slope_ref, sink_ref, q_ref, kvc_ref, kvp_ref, o_ref, bias_ref, *, tq):
    first = pl.program_id(1) == 0
    low, keep_low, keep_high = _pair_masks()

    def both_halves(ref, lane0):
        part = ref[:, lane0:lane0 + HEAD_DIM]
        return jnp.concatenate([part, part], axis=1)

    tile = (SWA_REP, ATTN_BLOCK, ATTN_BLOCK)
    stacked = (SWA_REP * ATTN_BLOCK, ATTN_BLOCK)
    qi = lax.broadcasted_iota(I32, (ATTN_BLOCK, ATTN_BLOCK), 0)
    kj = lax.broadcasted_iota(I32, (ATTN_BLOCK, ATTN_BLOCK), 1)
    from_prev = kj > qi
    diff = jnp.where(from_prev, qi - kj + ATTN_BLOCK, qi - kj).astype(F32)
    for kvh in range(SWA_KV_HEADS):
        head0 = kvh * SWA_REP
        for r in range(SWA_REP):
            bias_ref[r] = -slope_ref[head0 + r] * diff
        sink_col = _per_head_column(sink_ref, head0, SWA_REP)
        k_cur_all, k_first = both_halves(kvc_ref, kvh * HEAD_DIM), both_halves(kvp_ref, kvh * HEAD_DIM)
        v_lane0 = SWA_KV_WIDTH + kvh * HEAD_DIM
        v_cur_all, v_first = both_halves(kvc_ref, v_lane0), both_halves(kvp_ref, v_lane0)
        pair_cols = [slice((head0 + 2 * p) * HEAD_DIM, (head0 + 2 * p + 2) * HEAD_DIM) for p in range(SWA_REP // 2)]
        for i in range(tq // ATTN_BLOCK):
            rows = slice(i * ATTN_BLOCK, (i + 1) * ATTN_BLOCK)
            if i == 0:
                k_win = jnp.concatenate([k_first, k_cur_all[rows]], axis=0)
                v_win = jnp.concatenate([v_first, v_cur_all[rows]], axis=0)
            else:
                win = slice((i - 1) * ATTN_BLOCK, (i + 1) * ATTN_BLOCK)
                k_win, v_win = k_cur_all[win], v_cur_all[win]
            q = jnp.concatenate([q_ref[rows, c] * keep for c in pair_cols for keep in (keep_low, keep_high)], axis=0)
            s2 = lax.dot_general(q, k_win, _NT, preferred_element_type=F32)
            s = jnp.where(from_prev, s2[:, :ATTN_BLOCK].reshape(tile), s2[:, ATTN_BLOCK:].reshape(tile)) \
                + bias_ref[...]
            if i == 0:
                s = jnp.where(from_prev & first, NEG_INF, s)
            s = s.reshape(stacked)
            m = jnp.maximum(jnp.max(s, axis=-1, keepdims=True), sink_col)
            p = jnp.exp(s - m).reshape(tile)
            p2 = jnp.concatenate([jnp.where(from_prev, p, 0.0).reshape(stacked).astype(BF16),
                                  jnp.where(from_prev, 0.0, p).reshape(stacked).astype(BF16)], axis=1)
            od = jnp.dot(p2, _values_and_ones(v_win), preferred_element_type=F32)
            o = od[:, :LANES] / (od[:, LANES:] + jnp.exp(sink_col - m))
            for p, c in enumerate(pair_cols):
                even = o[2 * p * ATTN_BLOCK:(2 * p + 1) * ATTN_BLOCK]
                odd = o[(2 * p + 1) * ATTN_BLOCK:(2 * p + 2) * ATTN_BLOCK]
                o_ref[rows, c] = jnp.where(low, even, odd).astype(o_ref.dtype)


def _swa_attention(hq, hkv, slopes_b, sinks, bsz, seq, tq=512):
    nqb = tq // ATTN_BLOCK
    hq3 = hq.reshape(bsz, seq, SWA_Q_WIDTH)
    hkv3 = hkv.reshape(bsz, seq, 2 * SWA_KV_WIDTH)
    smem = pl.BlockSpec(memory_space=pltpu.SMEM)
    out = pl.pallas_call(
        functools.partial(_swa_attn_kernel, tq=tq),
        grid=(bsz, seq // tq),
        in_specs=[smem, smem,
                  pl.BlockSpec((None, tq, SWA_Q_WIDTH), lambda b, m: (b, m, 0)),
                  pl.BlockSpec((None, tq, 2 * SWA_KV_WIDTH), lambda b, m: (b, m, 0)),
                  pl.BlockSpec((None, ATTN_BLOCK, 2 * SWA_KV_WIDTH),
                               lambda b, m: (b, jnp.maximum(m * nqb - 1, 0), 0))],
        out_specs=pl.BlockSpec((None, tq, SWA_Q_WIDTH), lambda b, m: (b, m, 0)),
        out_shape=jax.ShapeDtypeStruct((bsz, seq, SWA_Q_WIDTH), BF16),
        scratch_shapes=[pltpu.VMEM((SWA_REP, ATTN_BLOCK, ATTN_BLOCK), F32)],
        compiler_params=_cparams("parallel", "arbitrary"),
        name="swa_attn",
    )(slopes_b, sinks, hq3, hkv3, hkv3)
    return out.reshape(bsz * seq, SWA_Q_WIDTH)


def _sigmoid(x):
    return 0.5 * (jnp.tanh(0.5 * x) + 1.0)


def _layer_norm(z, g, b):
    mu = jnp.mean(z, axis=-1, keepdims=True)
    zc = z - mu
    var = jnp.mean(zc * zc, axis=-1, keepdims=True)
    return zc * lax.rsqrt(var + LN_EPS) * g + b


MIX_OUT_CHAINS = 1


def _mix_out_kernel(x_ref, o0_ref, o1_ref, o2_ref, l0_ref, l1_ref, l2_ref, ob_ref, g_ref,
                    wpa_ref, wpb_ref, wo_ref, lng_ref, lnb_ref, rw_ref, rb_ref,
                    y_ref, ypk_ref, idx_ref, gate_ref, *scratch, d_model):
    def natural(ref, stages):
        dil, per, _ = ref.shape
        if dil == 1:
            return ref[0].astype(F32)
        for half, stage in enumerate(stages):
            for r in range(dil):
                stage[pl.ds(r, per, stride=dil), :] = ref[r, :, half * LANES:(half + 1) * LANES].astype(F32)
        return jnp.concatenate([stage[...] for stage in stages], axis=1)

    o_nat = natural(o0_ref, None), natural(o1_ref, scratch[0:2]), natural(o2_ref, scratch[2:4])
    l_nat = natural(l0_ref, None), natural(l1_ref, scratch[4:6]), natural(l2_ref, scratch[6:8])

    def chain(rows):
        (o0, o1, o2), (l0, l1, l2) = [a[rows] for a in o_nat], [a[rows] for a in l_nat]
        lm = jnp.maximum(jnp.maximum(l0, l1), l2)
        e0, e1, e2 = jnp.exp(l0 - lm), jnp.exp(l1 - lm), jnp.exp(l2 - lm)
        out_a = (e0 * o0 + e1 * o1 + e2 * o2) / (e0 + e1 + e2)
        pa = jnp.dot(out_a.astype(BF16), wpa_ref[...], preferred_element_type=F32)
        pb = jnp.dot(ob_ref[rows, :], wpb_ref[...], preferred_element_type=F32)
        ga = _sigmoid(g_ref[rows, :d_model].astype(F32))
        gb = _sigmoid(g_ref[rows, d_model:].astype(F32))
        merged = ga * pa + gb * pb
        mix = jnp.dot(merged.astype(BF16), wo_ref[...], preferred_element_type=F32)
        y = _layer_norm(DEEPNORM_ALPHA * x_ref[rows, :] + mix, lng_ref[...], lnb_ref[...])
        y_ref[rows, :] = y
        ypk_ref[rows, :] = _pack_bf16_pairs(y)

        logits = lax.dot_general(rw_ref[...], y.astype(BF16), _NT, preferred_element_type=F32) + rb_ref[...]
        expert = lax.broadcasted_iota(I32, logits.shape, 0)
        slot = lax.broadcasted_iota(I32, (SUBLANES, logits.shape[1]), 0)
        idx_out = jnp.zeros(slot.shape, I32)
        val_out = jnp.full(slot.shape, -jnp.inf, F32)
        for k in range(TOP_K):
            top = jnp.max(logits, axis=0, keepdims=True)
            top_idx = jnp.min(jnp.where(logits == top, expert, N_EXPERTS), axis=0, keepdims=True)
            idx_out = jnp.where(slot == k, top_idx, idx_out)
            val_out = jnp.where(slot == k, top, val_out)
            logits = jnp.where(expert == top_idx, -jnp.inf, logits)
        ev = jnp.exp(val_out - jnp.max(val_out, axis=0, keepdims=True))
        idx_ref[:, rows] = idx_out
        gate_ref[:, rows] = ev / jnp.sum(ev, axis=0, keepdims=True)

    tm = x_ref.shape[0]
    for part in range(MIX_OUT_CHAINS):
        chain(slice(part * tm // MIX_OUT_CHAINS, (part + 1) * tm // MIX_OUT_CHAINS))


def _mix_out(x2, o_g, lse_g, out_b, gates_h, wpa, wpb, wo, ln_g, ln_b, rw_t, rb_col, seq, tm=512):
    n, d = x2.shape
    tiles = seq // tm
    row = lambda i: (i, 0)
    const = lambda i: (0, 0)
    rb = lambda w: pl.BlockSpec((tm, w), row)
    full = lambda a: pl.BlockSpec(a.shape, const)
    dil_specs = [pl.BlockSpec((None, dil, tm // dil, DIL_OUT), lambda i: (i // tiles, 0, i % tiles, 0))
                 for _, dil in DIL_GROUPS]
    return pl.pallas_call(
        functools.partial(_mix_out_kernel, d_model=d),
        grid=(n // tm,),
        in_specs=[rb(d)] + dil_specs + dil_specs + [rb(SWA_Q_WIDTH), rb(2 * d),
                  full(wpa), full(wpb), full(wo), full(ln_g), full(ln_b), full(rw_t), full(rb_col)],
        out_specs=[rb(d), rb(d // 2), pl.BlockSpec((SUBLANES, tm), lambda i: (0, i)),
                   pl.BlockSpec((SUBLANES, tm), lambda i: (0, i))],
        out_shape=[jax.ShapeDtypeStruct((n, d), F32), jax.ShapeDtypeStruct((n, d // 2), jnp.uint32),
                   jax.ShapeDtypeStruct((SUBLANES, n), I32), jax.ShapeDtypeStruct((SUBLANES, n), F32)],
        scratch_shapes=[pltpu.VMEM((tm, LANES), F32)] * (4 * DIL_OUT // LANES),
        compiler_params=_cparams("parallel"),
        name="mix_out",
    )(x2, *o_g, *lse_g, out_b, gates_h, wpa, wpb, wo, ln_g, ln_b, rw_t, rb_col)


def _token_mixer_ln1(x2, w_in, sinks, w_proj_a, w_proj_b, w_out, ln1_g, ln1_b, router_w, router_b, bsz, seq):
    n, d = x2.shape
    heads = jnp.arange(1, N_ALIBI_HEADS + 1, dtype=F32)
    slopes = jnp.exp2(-8.0 * heads / N_ALIBI_HEADS)
    *a_g, hq, hkv, gates_h = _in_proj(x2, w_in.astype(BF16), bsz, seq)
    o_g, lse_g = [], []
    for g in range(N_DIL):
        sl = slopes[SWA_Q_HEADS + g * DIL_HEADS:SWA_Q_HEADS + (g + 1) * DIL_HEADS]
        o, lse = _dil_attention(a_g[g], sl, g)
        o_g.append(o)
        lse_g.append(lse)
    out_b = _swa_attention(hq, hkv, slopes[:SWA_Q_HEADS], sinks.astype(F32), bsz, seq)
    return _mix_out(x2, o_g, lse_g, out_b, gates_h, w_proj_a.astype(BF16), w_proj_b.astype(BF16),
                    w_out.astype(BF16), ln1_g.reshape(1, d).astype(F32), ln1_b.reshape(1, d).astype(F32),
                    router_w.T.astype(BF16), router_b.reshape(N_EXPERTS, 1).astype(F32), seq)


def _route_kernel(idx_ref, dest_ref, tbl_ref, carry_ref, pstart_ref, *, expert_block):
    phase, i = pl.program_id(0), pl.program_id(1)
    idx = idx_ref[...]
    t = idx.shape[1]
    expert = lax.broadcasted_iota(I32, (N_EXPERTS, t), 0)
    onehot = [expert == idx[k:k + 1, :] for k in range(TOP_K)]
    multi = sum(oh.astype(F32) for oh in onehot)
    tile_cnt = jnp.sum(multi, axis=1, keepdims=True).astype(I32)

    @pl.when((phase == 0) & (i == 0))
    def _():
        carry_ref[...] = jnp.zeros_like(carry_ref)

    @pl.when((phase == 1) & (i == 0))
    def _():
        counts = carry_ref[...]
        blocks = (counts + (expert_block - 1)) // expert_block
        er = lax.broadcasted_iota(I32, (N_EXPERTS, N_EXPERTS), 0)
        ec = lax.broadcasted_iota(I32, (N_EXPERTS, N_EXPERTS), 1)
        below = (ec < er).astype(BF16)
        prefix = lambda v: jnp.dot(below, v.astype(F32).astype(BF16), preferred_element_type=F32).astype(I32)
        pstart = (prefix(blocks >> 6) * 64 + prefix(blocks & 63)) * expert_block
        pstart_ref[...] = pstart
        carry_ref[...] = jnp.zeros_like(carry_ref)
        n_lanes = tbl_ref.shape[1]
        count_col, start_col, blocks_col = counts[:, 0:1], pstart[:, 0:1], blocks[:, 0:1]
        block_row0 = lax.broadcasted_iota(I32, (1, n_lanes), 1) * expert_block
        ends_before = (start_col + blocks_col * expert_block <= block_row0).astype(I32)
        block_e = jnp.minimum(jnp.sum(ends_before, axis=0, keepdims=True), N_EXPERTS - 1)
        mine = lax.broadcasted_iota(I32, (N_EXPERTS, n_lanes), 0) == block_e
        count_b = jnp.sum(jnp.where(mine, count_col, 0), axis=0, keepdims=True)
        start_b = jnp.sum(jnp.where(mine, start_col, 0), axis=0, keepdims=True)
        n_valid = jnp.clip(count_b - (block_row0 - start_b), 0, expert_block)
        n_used = jnp.sum(blocks_col, axis=0, keepdims=True)
        row = lax.broadcasted_iota(I32, (SUBLANES, n_lanes), 0)
        tbl_ref[...] = jnp.where(row == 0, block_e, jnp.where(row == 1, n_valid, n_used))

    @pl.when(phase == 1)
    def _():
        r = lax.broadcasted_iota(I32, (t, t), 0)
        c = lax.broadcasted_iota(I32, (t, t), 1)
        earlier = (r < c).astype(BF16)
        cum = jnp.dot(multi.astype(BF16), earlier, preferred_element_type=F32).astype(I32)
        base = cum + carry_ref[:, 0:1] + pstart_ref[:, 0:1]
        slot = lax.broadcasted_iota(I32, (SUBLANES, t), 0)
        dest = jnp.zeros((SUBLANES, t), I32)
        for k in range(TOP_K):
            d_k = jnp.sum(jnp.where(onehot[k], base, 0), axis=0, keepdims=True)
            dest = jnp.where(slot == k, d_k, dest)
        dest_ref[...] = dest

    carry_ref[...] = carry_ref[...] + tile_cnt


def _route(idx_t, expert_block, n_blocks, tile=512):
    n = idx_t.shape[1]
    tile = min(tile, n)
    per_expert = (N_EXPERTS, LANES)
    table = jax.ShapeDtypeStruct((SUBLANES, pl.cdiv(n_blocks, LANES) * LANES), I32)
    return pl.pallas_call(
        functools.partial(_route_kernel, expert_block=expert_block),
        grid=(2, n // tile),
        in_specs=[pl.BlockSpec((SUBLANES, tile), lambda p, i: (0, i))],
        out_specs=[pl.BlockSpec((SUBLANES, tile), lambda p, i: (0, i * p)),
                   pl.BlockSpec(table.shape, lambda p, i: (0, 0))],
        out_shape=[jax.ShapeDtypeStruct((SUBLANES, n), I32), table],
        scratch_shapes=[pltpu.VMEM(per_expert, I32), pltpu.VMEM(per_expert, I32)],
        compiler_params=_cparams("arbitrary", "arbitrary"),
        name="route",
    )(idx_t)


def _sc_dispatch(dest_t, y, rows, window=64):
    n, d = y.shape
    info = plsc.get_sparse_core_info()
    n_cores = info.num_cores
    per_worker = n // (n_cores * info.num_subcores)
    mesh = plsc.VectorSubcoreMesh(core_axis_name="c", subcore_axis_name="s")

    @functools.partial(pl.kernel, mesh=mesh, out_type=jax.ShapeDtypeStruct((rows, d), y.dtype),
                       scratch_types=[pltpu.VMEM((TOP_K, window), I32), pltpu.VMEM((window, d), y.dtype)],
                       name="sc_dispatch")
    def body(dest_hbm, y_hbm, xs_hbm, idx_v, rows_v):
        base = (lax.axis_index("s") * n_cores + lax.axis_index("c")) * per_worker

        @pl.loop(0, per_worker // window)
        def _(c):
            t0 = base + c * window
            for k in range(TOP_K):
                pltpu.sync_copy(dest_hbm.at[k, pl.ds(t0, window)], idx_v.at[k])
            pltpu.sync_copy(y_hbm.at[pl.ds(t0, window)], rows_v)
            for k in range(TOP_K):
                pltpu.sync_copy(rows_v, xs_hbm.at[idx_v.at[k]])

    return body(dest_t, y)


def _expert_kernel(be_ref, nu_ref, nv_ref, x_ref, wg_ref, bg_ref, wu_ref, bu_ref, wd_ref, bd_ref, o_ref,
                   wg_bf, wu_bf, wd_bf):
    j = pl.program_id(0)
    used = j < nu_ref[0]
    new_expert = (j == 0) | (be_ref[j] != be_ref[jnp.maximum(j - 1, 0)])

    @pl.when(used & new_expert)
    def _():
        chunk = 128
        for src, dst in ((wg_ref, wg_bf), (wu_ref, wu_bf), (wd_ref, wd_bf)):
            for r in range(0, src.shape[0], chunk):
                dst[r:r + chunk, :] = src[r:r + chunk, :].astype(BF16)

    @pl.when(used)
    def _():
        row = lax.broadcasted_iota(I32, x_ref.shape, 0)
        x_lo, x_hi = _unpack_bf16_pairs(jnp.where(row < nv_ref[j], x_ref[...], jnp.uint32(0)))
        xb = jnp.concatenate([x_lo.astype(BF16), x_hi.astype(BF16)], axis=1)
        gt = jnp.dot(xb, wg_bf[...], preferred_element_type=F32) + bg_ref[...]
        up = jnp.dot(xb, wu_bf[...], preferred_element_type=F32) + bu_ref[...]
        gt = jnp.minimum(gt, SWIGLU_LIMIT)
        up = jnp.clip(up, -SWIGLU_LIMIT, SWIGLU_LIMIT)
        hdn = gt * _sigmoid(SWIGLU_ALPHA * gt) * (up + 1.0)
        o_ref[...] = _pack_bf16_pairs(jnp.dot(hdn.astype(BF16), wd_bf[...], preferred_element_type=F32)
                                      + bd_ref[...])

    @pl.when(jnp.logical_not(used))
    def _():
        o_ref[...] = jnp.zeros_like(o_ref)


def _experts(block_e, n_used, n_valid, xs, wg, bg, wu, bu, wd, bd, expert_block):
    rows, _ = xs.shape
    e, d, f = wg.shape
    xmap = lambda j, be, nu, nv: (jnp.minimum(j, nu[0] - 1), 0)
    wmap = lambda j, be, nu, nv: (be[j], 0, 0)
    grid_spec = pltpu.PrefetchScalarGridSpec(
        num_scalar_prefetch=3,
        grid=(rows // expert_block,),
        in_specs=[pl.BlockSpec((expert_block, d // 2), xmap),
                  pl.BlockSpec((None, d, f), wmap), pl.BlockSpec((None, 1, f), wmap),
                  pl.BlockSpec((None, d, f), wmap), pl.BlockSpec((None, 1, f), wmap),
                  pl.BlockSpec((None, f, d), wmap), pl.BlockSpec((None, 1, d), wmap)],
        out_specs=pl.BlockSpec((expert_block, d // 2), lambda j, be, nu, nv: (j, 0)),
        scratch_shapes=[pltpu.VMEM((d, f), BF16), pltpu.VMEM((d, f), BF16), pltpu.VMEM((f, d), BF16)],
    )
    return pl.pallas_call(
        _expert_kernel,
        grid_spec=grid_spec,
        out_shape=jax.ShapeDtypeStruct((rows, d // 2), jnp.uint32),
        compiler_params=_cparams("arbitrary"),
        name="experts",
    )(block_e, n_used, n_valid, xs, wg, bg.reshape(e, 1, f), wu, bu.reshape(e, 1, f), wd, bd.reshape(e, 1, d))


def _sc_gather(dest_t, ys, n, window=64):
    _, d = ys.shape
    info = plsc.get_sparse_core_info()
    n_cores = info.num_cores
    per_worker = n // (n_cores * info.num_subcores)
    mesh = plsc.VectorSubcoreMesh(core_axis_name="c", subcore_axis_name="s")

    n_chunks = per_worker // window
    n_buf = 2

    @functools.partial(pl.kernel, mesh=mesh, out_type=jax.ShapeDtypeStruct((TOP_K, n, d), ys.dtype),
                       scratch_types=[pltpu.VMEM((TOP_K, per_worker), I32)]
                       + [pltpu.VMEM((window, d), ys.dtype)] * n_buf + [pltpu.SemaphoreType.DMA] * (2 * n_buf),
                       name="sc_gather")
    def body(dest_hbm, ys_hbm, out_hbm, idx_v, buf0, buf1, gsem0, gsem1, ssem0, ssem1):
        bufs, gsem, ssem = (buf0, buf1), (gsem0, gsem1), (ssem0, ssem1)
        base = (lax.axis_index("s") * n_cores + lax.axis_index("c")) * per_worker
        for k in range(TOP_K):
            pltpu.sync_copy(dest_hbm.at[k, pl.ds(base, per_worker)], idx_v.at[k])

        def gather(c, k):
            return pltpu.make_async_copy(ys_hbm.at[idx_v.at[k, pl.ds(c * window, window)]], bufs[k % n_buf],
                                         gsem[k % n_buf])

        def store(c, k):
            return pltpu.make_async_copy(bufs[k % n_buf], out_hbm.at[k, pl.ds(base + c * window, window)],
                                         ssem[k % n_buf])

        gather(0, 0).start()

        @pl.loop(0, n_chunks)
        def _(c):
            for k in range(TOP_K):
                if k > 0:
                    store(c, k - 1).wait()
                    if k + 1 < TOP_K:
                        gather(c, k + 1).start()
                    else:
                        @pl.when(c + 1 < n_chunks)
                        def _():
                            gather(c + 1, 0).start()
                else:
                    @pl.when(c > 0)
                    def _():
                        store(c - 1, TOP_K - 1).wait()
                    gather(c, 1).start()
                gather(c, k).wait()
                store(c, k).start()

        store(n_chunks - 1, TOP_K - 1).wait()

    return body(dest_t, ys)


def _combine_dense_kernel(gate_ref, y_ref, lng_ref, lnb_ref, g_ref, o_ref):
    gates = gate_ref[...].T
    ffn_lo = ffn_hi = None
    for k in range(TOP_K):
        lo, hi = _unpack_bf16_pairs(g_ref[k])
        gate = gates[:, k:k + 1]
        ffn_lo = gate * lo if k == 0 else ffn_lo + gate * lo
        ffn_hi = gate * hi if k == 0 else ffn_hi + gate * hi
    ffn = jnp.concatenate([ffn_lo, ffn_hi], axis=1)
    o_ref[...] = _layer_norm(DEEPNORM_ALPHA * y_ref[...] + ffn, lng_ref[...], lnb_ref[...])


def _combine_dense(gates, y, ln_g, ln_b, gathered, tile=512):
    n, d = y.shape
    tile = min(tile, n)
    row = lambda i: (i, 0)
    const = lambda i: (0, 0)
    return pl.pallas_call(
        _combine_dense_kernel,
        grid=(n // tile,),
        in_specs=[pl.BlockSpec((SUBLANES, tile), lambda i: (0, i)), pl.BlockSpec((tile, d), row),
                  pl.BlockSpec((1, d), const), pl.BlockSpec((1, d), const),
                  pl.BlockSpec((TOP_K, tile, d // 2), lambda i: (0, i, 0))],
        out_specs=pl.BlockSpec((tile, d), row),
        out_shape=jax.ShapeDtypeStruct((n, d), F32),
        compiler_params=_cparams("parallel"),
        name="combine_dense",
    )(gates, y, ln_g, ln_b, gathered)


MOE_ROW_BLOCK = 512


def _moe_ln2(y, y_packed, idx, gates, w_gate, b_gate, w_up, b_up, w_down, b_down, ln2_g, ln2_b):
    n, d = y.shape
    n_blocks = n * TOP_K // MOE_ROW_BLOCK + N_EXPERTS
    rows = n_blocks * MOE_ROW_BLOCK
    dest_t, block_table = _route(idx, MOE_ROW_BLOCK, n_blocks)
    block_e, n_valid, n_used = block_table[0, :n_blocks], block_table[1, :n_blocks], block_table[2, :1]
    xs = _sc_dispatch(dest_t, y_packed, rows)
    ys = _experts(block_e, n_used, n_valid, xs, w_gate, b_gate.astype(F32), w_up, b_up.astype(F32),
                  w_down, b_down.astype(F32), MOE_ROW_BLOCK)
    gathered = _sc_gather(dest_t, ys, n)
    return _combine_dense(gates, y, ln2_g.reshape(1, d).astype(F32), ln2_b.reshape(1, d).astype(F32), gathered)


def kernel(x, w_in, sinks, w_proj_a, w_proj_b, w_out, ln1_g, ln1_b, router_w, router_b,
           w_gate, b_gate, w_up, b_up, w_down, b_down, ln2_g, ln2_b):
    bsz, seq, d = x.shape
    h = x.reshape(bsz * seq, d)
    for l in range(w_in.shape[0]):
        y, y_packed, idx, gates = _token_mixer_ln1(h, w_in[l], sinks[l], w_proj_a[l], w_proj_b[l], w_out[l],
                                                   ln1_g[l], ln1_b[l], router_w[l], router_b[l], bsz, seq)
        h = _moe_ln2(y, y_packed, idx, gates, w_gate[l], b_gate[l], w_up[l], b_up[l], w_down[l], b_down[l],
                     ln2_g[l], ln2_b[l])
    return h.reshape(bsz, seq, d)
```

```python
import functools

import jax
import jax.numpy as jnp
from jax import lax
from jax.experimental import pallas as pl
from jax.experimental.pallas import tpu as pltpu
from jax.experimental.pallas import tpu_sc as plsc

F32 = jnp.float32
BF16 = jnp.bfloat16
I32 = jnp.int32

HEAD_DIM = 64
DIL_GROUPS = ((128, 1), (512, 4), (2048, 16))
DIL_HEADS = 4
N_DIL = len(DIL_GROUPS)
DIL_OUT = DIL_HEADS * HEAD_DIM
DIL_WIDTH = N_DIL * DIL_OUT
SWA_Q_HEADS = 16
SWA_KV_HEADS = 2
SWA_REP = SWA_Q_HEADS // SWA_KV_HEADS
SWA_WINDOW = 128
SWA_Q_WIDTH = SWA_Q_HEADS * HEAD_DIM
SWA_KV_WIDTH = SWA_KV_HEADS * HEAD_DIM
N_ALIBI_HEADS = SWA_Q_HEADS + N_DIL * DIL_HEADS
ATTN_BLOCK = 128
N_EXPERTS = 32
TOP_K = 4
SWIGLU_LIMIT = 7.0
SWIGLU_ALPHA = 1.702
LN_EPS = 1e-5
DEPTH = 1
DEEPNORM_ALPHA = (2 * DEPTH) ** 0.25
NEG_INF = -1e30

LANES = 128
SUBLANES = 8
VMEM_LIMIT_BYTES = 56 * 1024 * 1024

A_QKV_W = 3 * DIL_WIDTH
B_Q_OFF = A_QKV_W
B_KV_OFF = B_Q_OFF + SWA_Q_WIDTH
GATE_OFF = B_KV_OFF + 2 * SWA_KV_WIDTH


def _cparams(*sem):
    return pltpu.CompilerParams(dimension_semantics=sem, vmem_limit_bytes=VMEM_LIMIT_BYTES)


def _pack_bf16_pairs(x):
    c = x.shape[1] // 2
    lo = lax.bitcast_convert_type(x[:, :c].astype(BF16).astype(F32), jnp.uint32)
    hi = lax.bitcast_convert_type(x[:, c:].astype(BF16).astype(F32), jnp.uint32)
    return (lo >> 16) | hi


def _unpack_bf16_pairs(w):
    lo = lax.bitcast_convert_type(w << 16, F32)
    hi = lax.bitcast_convert_type(w & jnp.uint32(0xFFFF0000), F32)
    return lo, hi


def _in_proj_kernel(x_ref, w_ref, a0_ref, a1_ref, a2_ref, hq_ref, hkv_ref, g_ref, *scratch, d_model, tm):
    xb = x_ref[...].astype(BF16)
    segments = ((hq_ref, B_Q_OFF, SWA_Q_WIDTH), (hkv_ref, B_KV_OFF, 2 * SWA_KV_WIDTH),
                (g_ref, GATE_OFF, 2 * d_model))
    for out_ref, col0, width in segments:
        for c in range(0, width, 512):
            w = min(512, width - c)
            r = jnp.dot(xb, w_ref[:, col0 + c:col0 + c + w], preferred_element_type=F32)
            out_ref[:, c:c + w] = r.astype(out_ref.dtype)
    for g, a_ref in enumerate((a0_ref, a1_ref, a2_ref)):
        dil = DIL_GROUPS[g][1]
        per = tm // dil
        for part in range(3):
            col0 = part * DIL_WIDTH + g * DIL_OUT
            res = jnp.dot(xb, w_ref[:, col0:col0 + DIL_OUT], preferred_element_type=F32)
            if dil == 1:
                a_ref[0, :, part * DIL_OUT:(part + 1) * DIL_OUT] = res.astype(a_ref.dtype)
                continue
            for half in range(DIL_OUT // LANES):
                stage = scratch[part * (DIL_OUT // LANES) + half]
                stage[...] = res[:, half * LANES:(half + 1) * LANES]
                c0 = part * DIL_OUT + half * LANES
                for r in range(dil):
                    a_ref[r, :, c0:c0 + LANES] = stage[pl.ds(r, per, stride=dil), :].astype(a_ref.dtype)


def _in_proj(x2, w_in_bf, bsz, seq, tm=512):
    n, d = x2.shape
    cols = w_in_bf.shape[1]
    tiles = seq // tm
    row = lambda i: (i, 0)
    dil_spec = lambda dil: pl.BlockSpec((None, dil, tm // dil, 3 * DIL_OUT), lambda i: (i // tiles, 0, i % tiles, 0))
    dil_shape = lambda dil: jax.ShapeDtypeStruct((bsz, dil, seq // dil, 3 * DIL_OUT), BF16)
    dils = [dil for _, dil in DIL_GROUPS]
    return pl.pallas_call(
        functools.partial(_in_proj_kernel, d_model=d, tm=tm),
        grid=(n // tm,),
        in_specs=[pl.BlockSpec((tm, d), row),
                  pl.BlockSpec((d, cols), lambda i: (0, 0), pipeline_mode=pl.Buffered(1))],
        out_specs=[dil_spec(dil) for dil in dils]
        + [pl.BlockSpec((tm, SWA_Q_WIDTH), row), pl.BlockSpec((tm, 2 * SWA_KV_WIDTH), row),
           pl.BlockSpec((tm, 2 * d), row)],
        out_shape=[dil_shape(dil) for dil in dils]
        + [jax.ShapeDtypeStruct((n, SWA_Q_WIDTH), BF16), jax.ShapeDtypeStruct((n, 2 * SWA_KV_WIDTH), BF16),
           jax.ShapeDtypeStruct((n, 2 * d), BF16)],
        scratch_shapes=[pltpu.VMEM((tm, LANES), F32)] * (3 * DIL_OUT // LANES),
        compiler_params=_cparams("parallel"),
        name="in_proj",
    )(x2, w_in_bf)


_NT = (((1,), (1,)), ((), ()))


def _per_head_column(ref, head0, n_heads, scale=1.0):
    head_of_row = lax.broadcasted_iota(I32, (n_heads * ATTN_BLOCK, 1), 0) // ATTN_BLOCK
    col = jnp.zeros((n_heads * ATTN_BLOCK, 1), F32)
    for h in range(n_heads):
        col = jnp.where(head_of_row == h, ref[head0 + h] * scale, col)
    return col


def _store_band_bias(bias_ref, slope_col, max_diff):
    rows = slope_col.shape[0]
    qi = lax.broadcasted_iota(I32, (rows, ATTN_BLOCK), 0) % ATTN_BLOCK
    kj = lax.broadcasted_iota(I32, (rows, ATTN_BLOCK), 1)
    diff_prev = qi - kj + ATTN_BLOCK
    diff_cur = qi - kj
    bias_ref[0] = jnp.where(diff_prev <= max_diff, -slope_col * diff_prev.astype(F32), NEG_INF)
    bias_ref[1] = jnp.where((diff_cur >= 0) & (diff_cur <= max_diff), -slope_col * diff_cur.astype(F32), NEG_INF)


def _band_probs(s_prev, s_cur, sink_col):
    m = jnp.max(jnp.maximum(s_prev, s_cur), axis=-1, keepdims=True)
    if sink_col is not None:
        m = jnp.maximum(m, sink_col)
    return jnp.exp(s_prev - m).astype(BF16), jnp.exp(s_cur - m).astype(BF16), m


def _pair_masks():
    low = lax.broadcasted_iota(I32, (ATTN_BLOCK, LANES), 1) < HEAD_DIM
    scale = HEAD_DIM ** -0.5
    return low, jnp.where(low, scale, 0.0).astype(BF16), jnp.where(low, 0.0, scale).astype(BF16)


def _values_and_ones(v):
    return jnp.concatenate([v, jnp.ones((v.shape[0], LANES), v.dtype)], axis=1)


def _dil_attn_kernel(slope_ref, q_ref, kc_ref, kp_ref, vc_ref, vp_ref, o_ref, lse_ref, bias_ref,
                     *, tq, max_diff, dist_scale):
    first = pl.program_id(2) == 0
    _store_band_bias(bias_ref, _per_head_column(slope_ref, 0, DIL_HEADS, dist_scale), max_diff)
    low, keep_low, keep_high = _pair_masks()
    pair_cols = [slice(p * LANES, (p + 1) * LANES) for p in range(DIL_HEADS // 2)]
    for res, i in [(res, i) for res in range(q_ref.shape[0]) for i in range(tq // ATTN_BLOCK)]:
        rows = slice(i * ATTN_BLOCK, (i + 1) * ATTN_BLOCK)
        prev_rows = slice((i - 1) * ATTN_BLOCK, i * ATTN_BLOCK)
        s_prev, s_cur = [], []
        for c in pair_cols:
            k_prev = kp_ref[res, :, c] if i == 0 else kc_ref[res, prev_rows, c]
            for keep in (keep_low, keep_high):
                q = q_ref[res, rows, c] * keep
                s_prev.append(lax.dot_general(q, k_prev, _NT, preferred_element_type=F32))
                s_cur.append(lax.dot_general(q, kc_ref[res, rows, c], _NT, preferred_element_type=F32))
        s_prev = jnp.concatenate(s_prev, axis=0) + bias_ref[0]
        s_cur = jnp.concatenate(s_cur, axis=0) + bias_ref[1]
        if i == 0:
            s_prev = jnp.where(first, NEG_INF, s_prev)
        p_prev, p_cur, m = _band_probs(s_prev, s_cur, None)
        for p, c in enumerate(pair_cols):
            v_prev = _values_and_ones(vp_ref[res, :, c] if i == 0 else vc_ref[res, prev_rows, c])
            v_cur = _values_and_ones(vc_ref[res, rows, c])
            od, mh = [], []
            for h in (2 * p, 2 * p + 1):
                hr = slice(h * ATTN_BLOCK, (h + 1) * ATTN_BLOCK)
                od.append(jnp.dot(p_prev[hr], v_prev, preferred_element_type=F32)
                          + jnp.dot(p_cur[hr], v_cur, preferred_element_type=F32))
                mh.append(m[hr])
            denom = jnp.where(low, od[0][:, LANES:], od[1][:, LANES:])
            o_ref[res, rows, c] = (jnp.where(low, od[0][:, :LANES], od[1][:, :LANES]) / denom).astype(o_ref.dtype)
            lse_ref[res, rows, c] = jnp.where(low, mh[0], mh[1]) + jnp.log(denom)


def _dil_attention(a_g, slopes_g, g):
    window, dil = DIL_GROUPS[g]
    bsz, _, sub_len, _ = a_g.shape
    rows_per_step = 512
    tq = min(rows_per_step, sub_len)
    res_per_step = rows_per_step // tq
    nqb = tq // ATTN_BLOCK
    cur = lambda part: (lambda b, r, m: (b, r, m, part))
    prev = lambda part: (lambda b, r, m: (b, r, jnp.maximum(m * nqb - 1, 0), part))
    blk = lambda rows, imap: pl.BlockSpec((None, res_per_step, rows, DIL_OUT), imap)
    return pl.pallas_call(
        functools.partial(_dil_attn_kernel, tq=tq, max_diff=window // dil, dist_scale=float(dil)),
        grid=(bsz, dil // res_per_step, sub_len // tq),
        in_specs=[pl.BlockSpec(memory_space=pltpu.SMEM),
                  blk(tq, cur(0)), blk(tq, cur(1)), blk(ATTN_BLOCK, prev(1)),
                  blk(tq, cur(2)), blk(ATTN_BLOCK, prev(2))],
        out_specs=[blk(tq, cur(0)), blk(tq, cur(0))],
        out_shape=[jax.ShapeDtypeStruct((bsz, dil, sub_len, DIL_OUT), BF16),
                   jax.ShapeDtypeStruct((bsz, dil, sub_len, DIL_OUT), F32)],
        scratch_shapes=[pltpu.VMEM((2, DIL_HEADS * ATTN_BLOCK, ATTN_BLOCK), F32)],
        compiler_params=_cparams("parallel", "parallel", "arbitrary"),
        name=f"dil_attn_g{g}",
    )(slopes_g, a_g, a_g, a_g, a_g, a_g)


def _swa_attn_kernel(slope_ref, sink_ref, q_ref, kvc_ref, kvp_ref, o_ref, bias_ref, *, tq):
    first = pl.program_id(1) == 0
    low, keep_low, keep_high = _pair_masks()

    def both_halves(ref, lane0):
        part = ref[:, lane0:lane0 + HEAD_DIM]
        return jnp.concatenate([part, part], axis=1)

    tile = (SWA_REP, ATTN_BLOCK, ATTN_BLOCK)
    stacked = (SWA_REP * ATTN_BLOCK, ATTN_BLOCK)
    qi = lax.broadcasted_iota(I32, (ATTN_BLOCK, ATTN_BLOCK), 0)
    kj = lax.broadcasted_iota(I32, (ATTN_BLOCK, ATTN_BLOCK), 1)
    from_prev = kj > qi
    diff = jnp.where(from_prev, qi - kj + ATTN_BLOCK, qi - kj).astype(F32)
    for kvh in range(SWA_KV_HEADS):
        head0 = kvh * SWA_REP
        for r in range(SWA_REP):
            bias_ref[r] = -slope_ref[head0 + r] * diff
        sink_col = _per_head_column(sink_ref, head0, SWA_REP)
        k_cur_all, k_first = both_halves(kvc_ref, kvh * HEAD_DIM), both_halves(kvp_ref, kvh * HEAD_DIM)
        v_lane0 = SWA_KV_WIDTH + kvh * HEAD_DIM
        v_cur_all, v_first = both_halves(kvc_ref, v_lane0), both_halves(kvp_ref, v_lane0)
        pair_cols = [slice((head0 + 2 * p) * HEAD_DIM, (head0 + 2 * p + 2) * HEAD_DIM) for p in range(SWA_REP // 2)]
        for i in range(tq // ATTN_BLOCK):
            rows = slice(i * ATTN_BLOCK, (i + 1) * ATTN_BLOCK)
            if i == 0:
                k_win = jnp.concatenate([k_first, k_cur_all[rows]], axis=0)
                v_win = jnp.concatenate([v_first, v_cur_all[rows]], axis=0)
            else:
                win = slice((i - 1) * ATTN_BLOCK, (i + 1) * ATTN_BLOCK)
                k_win, v_win = k_cur_all[win], v_cur_all[win]
            q = jnp.concatenate([q_ref[rows, c] * keep for c in pair_cols for keep in (keep_low, keep_high)], axis=0)
            s2 = lax.dot_general(q, k_win, _NT, preferred_element_type=F32)
            s = jnp.where(from_prev, s2[:, :ATTN_BLOCK].reshape(tile), s2[:, ATTN_BLOCK:].reshape(tile)) \
                + bias_ref[...]
            if i == 0:
                s = jnp.where(from_prev & first, NEG_INF, s)
            s = s.reshape(stacked)
            m = jnp.maximum(jnp.max(s, axis=-1, keepdims=True), sink_col)
            p = jnp.exp(s - m).reshape(tile)
            p2 = jnp.concatenate([jnp.where(from_prev, p, 0.0).reshape(stacked).astype(BF16),
                                  jnp.where(from_prev, 0.0, p).reshape(stacked).astype(BF16)], axis=1)
            od = jnp.dot(p2, _values_and_ones(v_win), preferred_element_type=F32)
            o = od[:, :LANES] / (od[:, LANES:] + jnp.exp(sink_col - m))
            for p, c in enumerate(pair_cols):
                even = o[2 * p * ATTN_BLOCK:(2 * p + 1) * ATTN_BLOCK]
                odd = o[(2 * p + 1) * ATTN_BLOCK:(2 * p + 2) * ATTN_BLOCK]
                o_ref[rows, c] = jnp.where(low, even, odd).astype(o_ref.dtype)


def _swa_attention(hq, hkv, slopes_b, sinks, bsz, seq, tq=512):
    nqb = tq // ATTN_BLOCK
    hq3 = hq.reshape(bsz, seq, SWA_Q_WIDTH)
    hkv3 = hkv.reshape(bsz, seq, 2 * SWA_KV_WIDTH)
    smem = pl.BlockSpec(memory_space=pltpu.SMEM)
    out = pl.pallas_call(
        functools.partial(_swa_attn_kernel, tq=tq),
        grid=(bsz, seq // tq),
        in_specs=[smem, smem,
                  pl.BlockSpec((None, tq, SWA_Q_WIDTH), lambda b, m: (b, m, 0)),
                  pl.BlockSpec((None, tq, 2 * SWA_KV_WIDTH), lambda b, m: (b, m, 0)),
                  pl.BlockSpec((None, ATTN_BLOCK, 2 * SWA_KV_WIDTH),
                               lambda b, m: (b, jnp.maximum(m * nqb - 1, 0), 0))],
        out_specs=pl.BlockSpec((None, tq, SWA_Q_WIDTH), lambda b, m: (b, m, 0)),
        out_shape=jax.ShapeDtypeStruct((bsz, seq, SWA_Q_WIDTH), BF16),
        scratch_shapes=[pltpu.VMEM((SWA_REP, ATTN_BLOCK, ATTN_BLOCK), F32)],
        compiler_params=_cparams("parallel", "arbitrary"),
        name="swa_attn",
    )(slopes_b, sinks, hq3, hkv3, hkv3)
    return out.reshape(bsz * seq, SWA_Q_WIDTH)


def _sigmoid(x):
    return 0.5 * (jnp.tanh(0.5 * x) + 1.0)


def _layer_norm(z, g, b):
    mu = jnp.mean(z, axis=-1, keepdims=True)
    zc = z - mu
    var = jnp.mean(zc * zc, axis=-1, keepdims=True)
    return zc * lax.rsqrt(var + LN_EPS) * g + b


MIX_OUT_CHAINS = 1


def _mix_out_kernel(x_ref, o0_ref, o1_ref, o2_ref, l0_ref, l1_ref, l2_ref, ob_ref, g_ref,
                    wpa_ref, wpb_ref, wo_ref, lng_ref, lnb_ref, rw_ref, rb_ref,
                    y_ref, ypk_ref, idx_ref, gate_ref, *scratch, d_model):
    def natural(ref, stages):
        dil, per, _ = ref.shape
        if dil == 1:
            return ref[0].astype(F32)
        for half, stage in enumerate(stages):
            for r in range(dil):
                stage[pl.ds(r, per, stride=dil), :] = ref[r, :, half * LANES:(half + 1) * LANES].astype(F32)
        return jnp.concatenate([stage[...] for stage in stages], axis=1)

    o_nat = natural(o0_ref, None), natural(o1_ref, scratch[0:2]), natural(o2_ref, scratch[2:4])
    l_nat = natural(l0_ref, None), natural(l1_ref, scratch[4:6]), natural(l2_ref, scratch[6:8])

    def chain(rows):
        (o0, o1, o2), (l0, l1, l2) = [a[rows] for a in o_nat], [a[rows] for a in l_nat]
        lm = jnp.maximum(jnp.maximum(l0, l1), l2)
        e0, e1, e2 = jnp.exp(l0 - lm), jnp.exp(l1 - lm), jnp.exp(l2 - lm)
        out_a = (e0 * o0 + e1 * o1 + e2 * o2) / (e0 + e1 + e2)
        pa = jnp.dot(out_a.astype(BF16), wpa_ref[...], preferred_element_type=F32)
        pb = jnp.dot(ob_ref[rows, :], wpb_ref[...], preferred_element_type=F32)
        ga = _sigmoid(g_ref[rows, :d_model].astype(F32))
        gb = _sigmoid(g_ref[rows, d_model:].astype(F32))
        merged = ga * pa + gb * pb
        mix = jnp.dot(merged.astype(BF16), wo_ref[...], preferred_element_type=F32)
        y = _layer_norm(DEEPNORM_ALPHA * x_ref[rows, :] + mix, lng_ref[...], lnb_ref[...])
        y_ref[rows, :] = y
        ypk_ref[rows, :] = _pack_bf16_pairs(y)

        logits = lax.dot_general(rw_ref[...], y.astype(BF16), _NT, preferred_element_type=F32) + rb_ref[...]
        expert = lax.broadcasted_iota(I32, logits.shape, 0)
        slot = lax.broadcasted_iota(I32, (SUBLANES, logits.shape[1]), 0)
        idx_out = jnp.zeros(slot.shape, I32)
        val_out = jnp.full(slot.shape, -jnp.inf, F32)
        for k in range(TOP_K):
            top = jnp.max(logits, axis=0, keepdims=True)
            top_idx = jnp.min(jnp.where(logits == top, expert, N_EXPERTS), axis=0, keepdims=True)
            idx_out = jnp.where(slot == k, top_idx, idx_out)
            val_out = jnp.where(slot == k, top, val_out)
            logits = jnp.where(expert == top_idx, -jnp.inf, logits)
        ev = jnp.exp(val_out - jnp.max(val_out, axis=0, keepdims=True))
        idx_ref[:, rows] = idx_out
        gate_ref[:, rows] = ev / jnp.sum(ev, axis=0, keepdims=True)

    tm = x_ref.shape[0]
    for part in range(MIX_OUT_CHAINS):
        chain(slice(part * tm // MIX_OUT_CHAINS, (part + 1) * tm // MIX_OUT_CHAINS))


def _mix_out(x2, o_g, lse_g, out_b, gates_h, wpa, wpb, wo, ln_g, ln_b, rw_t, rb_col, seq, tm=512):
    n, d = x2.shape
    tiles = seq // tm
    row = lambda i: (i, 0)
    const = lambda i: (0, 0)
    rb = lambda w: pl.BlockSpec((tm, w), row)
    full = lambda a: pl.BlockSpec(a.shape, const)
    dil_specs = [pl.BlockSpec((None, dil, tm // dil, DIL_OUT), lambda i: (i // tiles, 0, i % tiles, 0))
                 for _, dil in DIL_GROUPS]
    return pl.pallas_call(
        functools.partial(_mix_out_kernel, d_model=d),
        grid=(n // tm,),
        in_specs=[rb(d)] + dil_specs + dil_specs + [rb(SWA_Q_WIDTH), rb(2 * d),
                  full(wpa), full(wpb), full(wo), full(ln_g), full(ln_b), full(rw_t), full(rb_col)],
        out_specs=[rb(d), rb(d // 2), pl.BlockSpec((SUBLANES, tm), lambda i: (0, i)),
                   pl.BlockSpec((SUBLANES, tm), lambda i: (0, i))],
        out_shape=[jax.ShapeDtypeStruct((n, d), F32), jax.ShapeDtypeStruct((n, d // 2), jnp.uint32),
                   jax.ShapeDtypeStruct((SUBLANES, n), I32), jax.ShapeDtypeStruct((SUBLANES, n), F32)],
        scratch_shapes=[pltpu.VMEM((tm, LANES), F32)] * (4 * DIL_OUT // LANES),
        compiler_params=_cparams("parallel"),
        name="mix_out",
    )(x2, *o_g, *lse_g, out_b, gates_h, wpa, wpb, wo, ln_g, ln_b, rw_t, rb_col)


def _token_mixer_ln1(x2, w_in, sinks, w_proj_a, w_proj_b, w_out, ln1_g, ln1_b, router_w, router_b, bsz, seq):
    n, d = x2.shape
    heads = jnp.arange(1, N_ALIBI_HEADS + 1, dtype=F32)
    slopes = jnp.exp2(-8.0 * heads / N_ALIBI_HEADS)
    *a_g, hq, hkv, gates_h = _in_proj(x2, w_in.astype(BF16), bsz, seq)
    o_g, lse_g = [], []
    for g in range(N_DIL):
        sl = slopes[SWA_Q_HEADS + g * DIL_HEADS:SWA_Q_HEADS + (g + 1) * DIL_HEADS]
        o, lse = _dil_attention(a_g[g], sl, g)
        o_g.append(o)
        lse_g.append(lse)
    out_b = _swa_attention(hq, hkv, slopes[:SWA_Q_HEADS], sinks.astype(F32), bsz, seq)
    return _mix_out(x2, o_g, lse_g, out_b, gates_h, w_proj_a.astype(BF16), w_proj_b.astype(BF16),
                    w_out.astype(BF16), ln1_g.reshape(1, d).astype(F32), ln1_b.reshape(1, d).astype(F32),
                    router_w.T.astype(BF16), router_b.reshape(N_EXPERTS, 1).astype(F32), seq)


def _route_kernel(idx_ref, dest_ref, tbl_ref, carry_ref, pstart_ref, *, expert_block):
    phase, i = pl.program_id(0), pl.program_id(1)
    idx = idx_ref[...]
    t = idx.shape[1]
    expert = lax.broadcasted_iota(I32, (N_EXPERTS, t), 0)
    onehot = [expert == idx[k:k + 1, :] for k in range(TOP_K)]
    multi = sum(oh.astype(F32) for oh in onehot)
    tile_cnt = jnp.sum(multi, axis=1, keepdims=True).astype(I32)

    @pl.when((phase == 0) & (i == 0))
    def _():
        carry_ref[...] = jnp.zeros_like(carry_ref)

    @pl.when((phase == 1) & (i == 0))
    def _():
        counts = carry_ref[...]
        blocks = (counts + (expert_block - 1)) // expert_block
        er = lax.broadcasted_iota(I32, (N_EXPERTS, N_EXPERTS), 0)
        ec = lax.broadcasted_iota(I32, (N_EXPERTS, N_EXPERTS), 1)
        below = (ec < er).astype(BF16)
        prefix = lambda v: jnp.dot(below, v.astype(F32).astype(BF16), preferred_element_type=F32).astype(I32)
        pstart = (prefix(blocks >> 6) * 64 + prefix(blocks & 63)) * expert_block
        pstart_ref[...] = pstart
        carry_ref[...] = jnp.zeros_like(carry_ref)
        n_lanes = tbl_ref.shape[1]
        count_col, start_col, blocks_col = counts[:, 0:1], pstart[:, 0:1], blocks[:, 0:1]
        block_row0 = lax.broadcasted_iota(I32, (1, n_lanes), 1) * expert_block
        ends_before = (start_col + blocks_col * expert_block <= block_row0).astype(I32)
        block_e = jnp.minimum(jnp.sum(ends_before, axis=0, keepdims=True), N_EXPERTS - 1)
        mine = lax.broadcasted_iota(I32, (N_EXPERTS, n_lanes), 0) == block_e
        count_b = jnp.sum(jnp.where(mine, count_col, 0), axis=0, keepdims=True)
        start_b = jnp.sum(jnp.where(mine, start_col, 0), axis=0, keepdims=True)
        n_valid = jnp.clip(count_b - (block_row0 - start_b), 0, expert_block)
        n_used = jnp.sum(blocks_col, axis=0, keepdims=True)
        row = lax.broadcasted_iota(I32, (SUBLANES, n_lanes), 0)
        tbl_ref[...] = jnp.where(row == 0, block_e, jnp.where(row == 1, n_valid, n_used))

    @pl.when(phase == 1)
    def _():
        r = lax.broadcasted_iota(I32, (t, t), 0)
        c = lax.broadcasted_iota(I32, (t, t), 1)
        earlier = (r < c).astype(BF16)
        cum = jnp.dot(multi.astype(BF16), earlier, preferred_element_type=F32).astype(I32)
        base = cum + carry_ref[:, 0:1] + pstart_ref[:, 0:1]
        slot = lax.broadcasted_iota(I32, (SUBLANES, t), 0)
        dest = jnp.zeros((SUBLANES, t), I32)
        for k in range(TOP_K):
            d_k = jnp.sum(jnp.where(onehot[k], base, 0), axis=0, keepdims=True)
            dest = jnp.where(slot == k, d_k, dest)
        dest_ref[...] = dest

    carry_ref[...] = carry_ref[...] + tile_cnt


def _route(idx_t, expert_block, n_blocks, tile=512):
    n = idx_t.shape[1]
    tile = min(tile, n)
    per_expert = (N_EXPERTS, LANES)
    table = jax.ShapeDtypeStruct((SUBLANES, pl.cdiv(n_blocks, LANES) * LANES), I32)
    return pl.pallas_call(
        functools.partial(_route_kernel, expert_block=expert_block),
        grid=(2, n // tile),
        in_specs=[pl.BlockSpec((SUBLANES, tile), lambda p, i: (0, i))],
        out_specs=[pl.BlockSpec((SUBLANES, tile), lambda p, i: (0, i * p)),
                   pl.BlockSpec(table.shape, lambda p, i: (0, 0))],
        out_shape=[jax.ShapeDtypeStruct((SUBLANES, n), I32), table],
        scratch_shapes=[pltpu.VMEM(per_expert, I32), pltpu.VMEM(per_expert, I32)],
        compiler_params=_cparams("arbitrary", "arbitrary"),
        name="route",
    )(idx_t)


def _sc_dispatch(dest_t, y, rows, window=64):
    n, d = y.shape
    info = plsc.get_sparse_core_info()
    n_cores = info.num_cores
    per_worker = n // (n_cores * info.num_subcores)
    window = min(window, per_worker // 2)
    n_chunks = per_worker // window
    mesh = plsc.VectorSubcoreMesh(core_axis_name="c", subcore_axis_name="s")
    dest_chunks = dest_t.reshape(dest_t.shape[0], n // window, window)

    @functools.partial(pl.kernel, mesh=mesh, out_type=jax.ShapeDtypeStruct((rows, d), y.dtype),
                       scratch_types=[pltpu.VMEM((TOP_K, n_chunks, window), I32)]
                       + [pltpu.VMEM((window, d), y.dtype)] * 2 + [pltpu.SemaphoreType.DMA] * 4,
                       name="sc_dispatch")
    def body(dest_hbm, y_hbm, xs_hbm, idx_v, buf0, buf1, lsem0, lsem1, ssem0, ssem1):
        bufs, lsem, ssem = (buf0, buf1), (lsem0, lsem1), (ssem0, ssem1)
        worker = lax.axis_index("s") * n_cores + lax.axis_index("c")
        base = worker * per_worker
        for k in range(TOP_K):
            pltpu.sync_copy(dest_hbm.at[k, pl.ds(worker * n_chunks, n_chunks)], idx_v.at[k])

        def load(c, b):
            return pltpu.make_async_copy(y_hbm.at[pl.ds(base + c * window, window)], bufs[b], lsem[b])

        def scatter(c, k, b):
            return pltpu.make_async_copy(bufs[b], xs_hbm.at[idx_v.at[k, c]], ssem[b])

        load(0, 0).start()

        @pl.loop(0, n_chunks, step=2)
        def _(c0):
            for b in range(2):
                c = c0 + b
                load(c, b).wait()

                @pl.when(c > 0)
                def _():
                    for k in range(TOP_K):
                        scatter(c - 1, k, 1 - b).wait()

                @pl.when(c + 1 < n_chunks)
                def _():
                    load(c + 1, 1 - b).start()

                for k in range(TOP_K):
                    scatter(c, k, b).start()

        for k in range(TOP_K):
            scatter(n_chunks - 1, k, 1).wait()

    return body(dest_chunks, y)


def _expert_kernel(be_ref, nu_ref, nv_ref, x_ref, wg_ref, bg_ref, wu_ref, bu_ref, wd_ref, bd_ref, o_ref,
                   wg_bf, wu_bf, wd_bf):
    j = pl.program_id(0)
    used = j < nu_ref[0]
    new_expert = (j == 0) | (be_ref[j] != be_ref[jnp.maximum(j - 1, 0)])

    @pl.when(used & new_expert)
    def _():
        chunk = 128
        for src, dst in ((wg_ref, wg_bf), (wu_ref, wu_bf), (wd_ref, wd_bf)):
            for r in range(0, src.shape[0], chunk):
                dst[r:r + chunk, :] = src[r:r + chunk, :].astype(BF16)

    @pl.when(used)
    def _():
        row = lax.broadcasted_iota(I32, x_ref.shape, 0)
        x_lo, x_hi = _unpack_bf16_pairs(jnp.where(row < nv_ref[j], x_ref[...], jnp.uint32(0)))
        xb = jnp.concatenate([x_lo.astype(BF16), x_hi.astype(BF16)], axis=1)
        gt = jnp.dot(xb, wg_bf[...], preferred_element_type=F32) + bg_ref[...]
        up = jnp.dot(xb, wu_bf[...], preferred_element_type=F32) + bu_ref[...]
        gt = jnp.minimum(gt, SWIGLU_LIMIT)
        up = jnp.clip(up, -SWIGLU_LIMIT, SWIGLU_LIMIT)
        hdn = gt * _sigmoid(SWIGLU_ALPHA * gt) * (up + 1.0)
        o_ref[...] = _pack_bf16_pairs(jnp.dot(hdn.astype(BF16), wd_bf[...], preferred_element_type=F32)
                                      + bd_ref[...])

    @pl.when(jnp.logical_not(used))
    def _():
        o_ref[...] = jnp.zeros_like(o_ref)


def _experts(block_e, n_used, n_valid, xs, wg, bg, wu, bu, wd, bd, expert_block):
    rows, _ = xs.shape
    e, d, f = wg.shape
    xmap = lambda j, be, nu, nv: (jnp.minimum(j, nu[0] - 1), 0)
    wmap = lambda j, be, nu, nv: (be[j], 0, 0)
    grid_spec = pltpu.PrefetchScalarGridSpec(
        num_scalar_prefetch=3,
        grid=(rows // expert_block,),
        in_specs=[pl.BlockSpec((expert_block, d // 2), xmap),
                  pl.BlockSpec((None, d, f), wmap), pl.BlockSpec((None, 1, f), wmap),
                  pl.BlockSpec((None, d, f), wmap), pl.BlockSpec((None, 1, f), wmap),
                  pl.BlockSpec((None, f, d), wmap), pl.BlockSpec((None, 1, d), wmap)],
        out_specs=pl.BlockSpec((expert_block, d // 2), lambda j, be, nu, nv: (j, 0)),
        scratch_shapes=[pltpu.VMEM((d, f), BF16), pltpu.VMEM((d, f), BF16), pltpu.VMEM((f, d), BF16)],
    )
    return pl.pallas_call(
        _expert_kernel,
        grid_spec=grid_spec,
        out_shape=jax.ShapeDtypeStruct((rows, d // 2), jnp.uint32),
        compiler_params=_cparams("arbitrary"),
        name="experts",
    )(block_e, n_used, n_valid, xs, wg, bg.reshape(e, 1, f), wu, bu.reshape(e, 1, f), wd, bd.reshape(e, 1, d))


def _sc_gather(dest_t, ys, n, window=64):
    _, d = ys.shape
    info = plsc.get_sparse_core_info()
    n_cores = info.num_cores
    per_worker = n // (n_cores * info.num_subcores)
    mesh = plsc.VectorSubcoreMesh(core_axis_name="c", subcore_axis_name="s")

    n_chunks = per_worker // window
    n_buf = 2

    @functools.partial(pl.kernel, mesh=mesh, out_type=jax.ShapeDtypeStruct((TOP_K, n, d), ys.dtype),
                       scratch_types=[pltpu.VMEM((TOP_K, per_worker), I32)]
                       + [pltpu.VMEM((window, d), ys.dtype)] * n_buf + [pltpu.SemaphoreType.DMA] * (2 * n_buf),
                       name="sc_gather")
    def body(dest_hbm, ys_hbm, out_hbm, idx_v, buf0, buf1, gsem0, gsem1, ssem0, ssem1):
        bufs, gsem, ssem = (buf0, buf1), (gsem0, gsem1), (ssem0, ssem1)
        base = (lax.axis_index("s") * n_cores + lax.axis_index("c")) * per_worker
        for k in range(TOP_K):
            pltpu.sync_copy(dest_hbm.at[k, pl.ds(base, per_worker)], idx_v.at[k])

        def gather(c, k):
            return pltpu.make_async_copy(ys_hbm.at[idx_v.at[k, pl.ds(c * window, window)]], bufs[k % n_buf],
                                         gsem[k % n_buf])

        def store(c, k):
            return pltpu.make_async_copy(bufs[k % n_buf], out_hbm.at[k, pl.ds(base + c * window, window)],
                                         ssem[k % n_buf])

        gather(0, 0).start()

        @pl.loop(0, n_chunks)
        def _(c):
            for k in range(TOP_K):
                if k > 0:
                    store(c, k - 1).wait()
                    if k + 1 < TOP_K:
                        gather(c, k + 1).start()
                    else:
                        @pl.when(c + 1 < n_chunks)
                        def _():
                            gather(c + 1, 0).start()
                else:
                    @pl.when(c > 0)
                    def _():
                        store(c - 1, TOP_K - 1).wait()
                    gather(c, 1).start()
                gather(c, k).wait()
                store(c, k).start()

        store(n_chunks - 1, TOP_K - 1).wait()

    return body(dest_t, ys)


def _combine_dense_kernel(gate_ref, y_ref, lng_ref, lnb_ref, g_ref, o_ref):
    gates = gate_ref[...].T
    ffn_lo = ffn_hi = None
    for k in range(TOP_K):
        lo, hi = _unpack_bf16_pairs(g_ref[k])
        gate = gates[:, k:k + 1]
        ffn_lo = gate * lo if k == 0 else ffn_lo + gate * lo
        ffn_hi = gate * hi if k == 0 else ffn_hi + gate * hi
    ffn = jnp.concatenate([ffn_lo, ffn_hi], axis=1)
    o_ref[...] = _layer_norm(DEEPNORM_ALPHA * y_ref[...] + ffn, lng_ref[...], lnb_ref[...])


def _combine_dense(gates, y, ln_g, ln_b, gathered, tile=512):
    n, d = y.shape
    tile = min(tile, n)
    row = lambda i: (i, 0)
    const = lambda i: (0, 0)
    return pl.pallas_call(
        _combine_dense_kernel,
        grid=(n // tile,),
        in_specs=[pl.BlockSpec((SUBLANES, tile), lambda i: (0, i)), pl.BlockSpec((tile, d), row),
                  pl.BlockSpec((1, d), const), pl.BlockSpec((1, d), const),
                  pl.BlockSpec((TOP_K, tile, d // 2), lambda i: (0, i, 0))],
        out_specs=pl.BlockSpec((tile, d), row),
        out_shape=jax.ShapeDtypeStruct((n, d), F32),
        compiler_params=_cparams("parallel"),
        name="combine_dense",
    )(gates, y, ln_g, ln_b, gathered)


MOE_ROW_BLOCK = 512


def _moe_ln2(y, y_packed, idx, gates, w_gate, b_gate, w_up, b_up, w_down, b_down, ln2_g, ln2_b):
    n, d = y.shape
    n_blocks = n * TOP_K // MOE_ROW_BLOCK + N_EXPERTS
    rows = n_blocks * MOE_ROW_BLOCK
    dest_t, block_table = _route(idx, MOE_ROW_BLOCK, n_blocks)
    block_e, n_valid, n_used = block_table[0, :n_blocks], block_table[1, :n_blocks], block_table[2, :1]
    xs = _sc_dispatch(dest_t, y_packed, rows)
    ys = _experts(block_e, n_used, n_valid, xs, w_gate, b_gate.astype(F32), w_up, b_up.astype(F32),
                  w_down, b_down.astype(F32), MOE_ROW_BLOCK)
    gathered = _sc_gather(dest_t, ys, n)
    return _combine_dense(gates, y, ln2_g.reshape(1, d).astype(F32), ln2_b.reshape(1, d).astype(F32), gathered)


def kernel(x, w_in, sinks, w_proj_a, w_proj_b, w_out, ln1_g, ln1_b, router_w, router_b,
           w_gate, b_gate, w_up, b_up, w_down, b_down, ln2_g, ln2_b):
    bsz, seq, d = x.shape
    h = x.reshape(bsz * seq, d)
    for l in range(w_in.shape[0]):
        y, y_packed, idx, gates = _token_mixer_ln1(h, w_in[l], sinks[l], w_proj_a[l], w_proj_b[l], w_out[l],
                                                   ln1_g[l], ln1_b[l], router_w[l], router_b[l], bsz, seq)
        h = _moe_ln2(y, y_packed, idx, gates, w_gate[l], b_gate[l], w_up[l], b_up[l], w_down[l], b_down[l],
                     ln2_g[l], ln2_b[l])
    return h.reshape(bsz, seq, d)
```

```python
import functools

import jax
import jax.numpy as jnp
from jax import lax
from jax.experimental import pallas as pl
from jax.experimental.pallas import tpu as pltpu
from jax.experimental.pallas import tpu_sc as plsc

F32 = jnp.float32
BF16 = jnp.bfloat16
I32 = jnp.int32

HEAD_DIM = 64
DIL_GROUPS = ((128, 1), (512, 4), (2048, 16))
DIL_HEADS = 4
N_DIL = len(DIL_GROUPS)
DIL_OUT = DIL_HEADS * HEAD_DIM
DIL_WIDTH = N_DIL * DIL_OUT
SWA_Q_HEADS = 16
SWA_KV_HEADS = 2
SWA_REP = SWA_Q_HEADS // SWA_KV_HEADS
SWA_WINDOW = 128
SWA_Q_WIDTH = SWA_Q_HEADS * HEAD_DIM
SWA_KV_WIDTH = SWA_KV_HEADS * HEAD_DIM
N_ALIBI_HEADS = SWA_Q_HEADS + N_DIL * DIL_HEADS
ATTN_BLOCK = 128
N_EXPERTS = 32
TOP_K = 4
SWIGLU_LIMIT = 7.0
SWIGLU_ALPHA = 1.702
LN_EPS = 1e-5
DEPTH = 1
DEEPNORM_ALPHA = (2 * DEPTH) ** 0.25
NEG_INF = -1e30

LANES = 128
SUBLANES = 8
VMEM_LIMIT_BYTES = 56 * 1024 * 1024
ROW_TILE = 512
IN_PROJ_TILE = 2 * ROW_TILE
COL_CHUNK = 512

A_QKV_W = 3 * DIL_WIDTH
B_Q_OFF = A_QKV_W
B_KV_OFF = B_Q_OFF + SWA_Q_WIDTH
GATE_OFF = B_KV_OFF + 2 * SWA_KV_WIDTH


def _cparams(*sem):
    return pltpu.CompilerParams(dimension_semantics=sem, vmem_limit_bytes=VMEM_LIMIT_BYTES)


def _pack_bf16_pairs(x):
    c = x.shape[1] // 2
    lo = lax.bitcast_convert_type(x[:, :c].astype(BF16).astype(F32), jnp.uint32)
    hi = lax.bitcast_convert_type(x[:, c:].astype(BF16).astype(F32), jnp.uint32)
    return (lo >> 16) | hi


def _unpack_bf16_pairs(w):
    lo = lax.bitcast_convert_type(w << 16, F32)
    hi = lax.bitcast_convert_type(w & jnp.uint32(0xFFFF0000), F32)
    return lo, hi


def _in_proj_kernel(x_ref, w_ref, a0_ref, a1_ref, a2_ref, hq_ref, hkv_ref, g_ref, *scratch, d_model, tm):
    xb = x_ref[...].astype(BF16)
    segments = ((hq_ref, B_Q_OFF, SWA_Q_WIDTH), (hkv_ref, B_KV_OFF, 2 * SWA_KV_WIDTH),
                (g_ref, GATE_OFF, 2 * d_model))
    for out_ref, col0, width in segments:
        for c in range(0, width, COL_CHUNK):
            w = min(COL_CHUNK, width - c)
            r = jnp.dot(xb, w_ref[:, col0 + c:col0 + c + w], preferred_element_type=F32)
            out_ref[:, c:c + w] = r.astype(out_ref.dtype)
    for g, a_ref in enumerate((a0_ref, a1_ref, a2_ref)):
        dil = DIL_GROUPS[g][1]
        per = tm // dil
        for part in range(3):
            col0 = part * DIL_WIDTH + g * DIL_OUT
            res = jnp.dot(xb, w_ref[:, col0:col0 + DIL_OUT], preferred_element_type=F32)
            if dil == 1:
                a_ref[0, :, part * DIL_OUT:(part + 1) * DIL_OUT] = res.astype(a_ref.dtype)
                continue
            for half in range(DIL_OUT // LANES):
                stage = scratch[part * (DIL_OUT // LANES) + half]
                stage[...] = res[:, half * LANES:(half + 1) * LANES]
                c0 = part * DIL_OUT + half * LANES
                for r in range(dil):
                    a_ref[r, :, c0:c0 + LANES] = stage[pl.ds(r, per, stride=dil), :].astype(a_ref.dtype)


def _in_proj(x2, w_in_bf, bsz, seq, tm=IN_PROJ_TILE):
    n, d = x2.shape
    cols = w_in_bf.shape[1]
    tiles = seq // tm
    row = lambda i: (i, 0)
    dil_spec = lambda dil: pl.BlockSpec((None, dil, tm // dil, 3 * DIL_OUT), lambda i: (i // tiles, 0, i % tiles, 0))
    dil_shape = lambda dil: jax.ShapeDtypeStruct((bsz, dil, seq // dil, 3 * DIL_OUT), BF16)
    dils = [dil for _, dil in DIL_GROUPS]
    return pl.pallas_call(
        functools.partial(_in_proj_kernel, d_model=d, tm=tm),
        grid=(n // tm,),
        in_specs=[pl.BlockSpec((tm, d), row),
                  pl.BlockSpec((d, cols), lambda i: (0, 0), pipeline_mode=pl.Buffered(1))],
        out_specs=[dil_spec(dil) for dil in dils]
        + [pl.BlockSpec((tm, SWA_Q_WIDTH), row), pl.BlockSpec((tm, 2 * SWA_KV_WIDTH), row),
           pl.BlockSpec((tm, 2 * d), row)],
        out_shape=[dil_shape(dil) for dil in dils]
        + [jax.ShapeDtypeStruct((n, SWA_Q_WIDTH), BF16), jax.ShapeDtypeStruct((n, 2 * SWA_KV_WIDTH), BF16),
           jax.ShapeDtypeStruct((n, 2 * d), BF16)],
        scratch_shapes=[pltpu.VMEM((tm, LANES), F32)] * (3 * DIL_OUT // LANES),
        compiler_params=_cparams("parallel"),
        name="in_proj",
    )(x2, w_in_bf)


_NT = (((1,), (1,)), ((), ()))


def _per_head_column(ref, head0, n_heads, scale=1.0):
    head_of_row = lax.broadcasted_iota(I32, (n_heads * ATTN_BLOCK, 1), 0) // ATTN_BLOCK
    col = jnp.zeros((n_heads * ATTN_BLOCK, 1), F32)
    for h in range(n_heads):
        col = jnp.where(head_of_row == h, ref[head0 + h] * scale, col)
    return col


def _store_band_bias(bias_ref, slope_col, max_diff):
    rows = slope_col.shape[0]
    qi = lax.broadcasted_iota(I32, (rows, ATTN_BLOCK), 0) % ATTN_BLOCK
    kj = lax.broadcasted_iota(I32, (rows, ATTN_BLOCK), 1)
    diff_prev = qi - kj + ATTN_BLOCK
    diff_cur = qi - kj
    bias_ref[0] = jnp.where(diff_prev <= max_diff, -slope_col * diff_prev.astype(F32), NEG_INF)
    bias_ref[1] = jnp.where((diff_cur >= 0) & (diff_cur <= max_diff), -slope_col * diff_cur.astype(F32), NEG_INF)


def _band_probs(s_prev, s_cur, sink_col):
    m = jnp.max(jnp.maximum(s_prev, s_cur), axis=-1, keepdims=True)
    if sink_col is not None:
        m = jnp.maximum(m, sink_col)
    return jnp.exp(s_prev - m).astype(BF16), jnp.exp(s_cur - m).astype(BF16), m


def _pair_masks():
    low = lax.broadcasted_iota(I32, (ATTN_BLOCK, LANES), 1) < HEAD_DIM
    scale = HEAD_DIM ** -0.5
    return low, jnp.where(low, scale, 0.0).astype(BF16), jnp.where(low, 0.0, scale).astype(BF16)


def _values_and_ones(v):
    return jnp.concatenate([v, jnp.ones((v.shape[0], LANES), v.dtype)], axis=1)


def _dil_attn_kernel(slope_ref, q_ref, kc_ref, kp_ref, vc_ref, vp_ref, o_ref, lse_ref, bias_ref,
                     *, tq, max_diff, dist_scale):
    first = pl.program_id(2) == 0
    _store_band_bias(bias_ref, _per_head_column(slope_ref, 0, DIL_HEADS, dist_scale), max_diff)
    low, keep_low, keep_high = _pair_masks()
    pair_cols = [slice(p * LANES, (p + 1) * LANES) for p in range(DIL_HEADS // 2)]
    for res, i in [(res, i) for res in range(q_ref.shape[0]) for i in range(tq // ATTN_BLOCK)]:
        rows = slice(i * ATTN_BLOCK, (i + 1) * ATTN_BLOCK)
        window = slice((i - 1) * ATTN_BLOCK, (i + 1) * ATTN_BLOCK)

        def key_window(cur_ref, first_prev_ref, c):
            if i == 0:
                return jnp.concatenate([first_prev_ref[res, :, c], cur_ref[res, rows, c]], axis=0)
            return cur_ref[res, window, c]

        scores = []
        for c in pair_cols:
            k_win = key_window(kc_ref, kp_ref, c)
            for keep in (keep_low, keep_high):
                scores.append(lax.dot_general(q_ref[res, rows, c] * keep, k_win, _NT, preferred_element_type=F32))
        scores = jnp.concatenate(scores, axis=0)
        s_prev = scores[:, :ATTN_BLOCK] + bias_ref[0]
        s_cur = scores[:, ATTN_BLOCK:] + bias_ref[1]
        if i == 0:
            s_prev = jnp.where(first, NEG_INF, s_prev)
        p_prev, p_cur, m = _band_probs(s_prev, s_cur, None)
        probs = jnp.concatenate([p_prev, p_cur], axis=1)
        for p, c in enumerate(pair_cols):
            v_win = _values_and_ones(key_window(vc_ref, vp_ref, c))
            od, mh = [], []
            for h in (2 * p, 2 * p + 1):
                hr = slice(h * ATTN_BLOCK, (h + 1) * ATTN_BLOCK)
                od.append(jnp.dot(probs[hr], v_win, preferred_element_type=F32))
                mh.append(m[hr])
            denom = jnp.where(low, od[0][:, LANES:], od[1][:, LANES:])
            o_ref[res, rows, c] = (jnp.where(low, od[0][:, :LANES], od[1][:, :LANES]) / denom).astype(o_ref.dtype)
            lse_ref[res, rows, c] = jnp.where(low, mh[0], mh[1]) + jnp.log(denom)


def _dil_attention(a_g, slopes_g, g):
    window, dil = DIL_GROUPS[g]
    bsz, _, sub_len, _ = a_g.shape
    tq = min(ROW_TILE, sub_len)
    res_per_step = ROW_TILE // tq
    nqb = tq // ATTN_BLOCK
    cur = lambda part: (lambda b, r, m: (b, r, m, part))
    prev = lambda part: (lambda b, r, m: (b, r, jnp.maximum(m * nqb - 1, 0), part))
    blk = lambda rows, imap: pl.BlockSpec((None, res_per_step, rows, DIL_OUT), imap)
    return pl.pallas_call(
        functools.partial(_dil_attn_kernel, tq=tq, max_diff=window // dil, dist_scale=float(dil)),
        grid=(bsz, dil // res_per_step, sub_len // tq),
        in_specs=[pl.BlockSpec(memory_space=pltpu.SMEM),
                  blk(tq, cur(0)), blk(tq, cur(1)), blk(ATTN_BLOCK, prev(1)),
                  blk(tq, cur(2)), blk(ATTN_BLOCK, prev(2))],
        out_specs=[blk(tq, cur(0)), blk(tq, cur(0))],
        out_shape=[jax.ShapeDtypeStruct((bsz, dil, sub_len, DIL_OUT), BF16),
                   jax.ShapeDtypeStruct((bsz, dil, sub_len, DIL_OUT), F32)],
        scratch_shapes=[pltpu.VMEM((2, DIL_HEADS * ATTN_BLOCK, ATTN_BLOCK), F32)],
        compiler_params=_cparams("parallel", "parallel", "arbitrary"),
        name=f"dil_attn_g{g}",
    )(slopes_g, a_g, a_g, a_g, a_g, a_g)


def _swa_attn_kernel(slope_ref, sink_ref, q_ref, kvc_ref, kvp_ref, o_ref, bias_ref, *, tq):
    first = pl.program_id(1) == 0
    low, keep_low, keep_high = _pair_masks()

    def both_halves(ref, lane0):
        part = ref[:, lane0:lane0 + HEAD_DIM]
        return jnp.concatenate([part, part], axis=1)

    tile = (SWA_REP, ATTN_BLOCK, ATTN_BLOCK)
    stacked = (SWA_REP * ATTN_BLOCK, ATTN_BLOCK)
    qi = lax.broadcasted_iota(I32, (ATTN_BLOCK, ATTN_BLOCK), 0)
    kj = lax.broadcasted_iota(I32, (ATTN_BLOCK, ATTN_BLOCK), 1)
    from_prev = kj > qi
    diff = jnp.where(from_prev, qi - kj + ATTN_BLOCK, qi - kj).astype(F32)
    for kvh in range(SWA_KV_HEADS):
        head0 = kvh * SWA_REP
        for r in range(SWA_REP):
            bias_ref[r] = -slope_ref[head0 + r] * diff
        sink_col = _per_head_column(sink_ref, head0, SWA_REP)
        k_cur_all, k_first = both_halves(kvc_ref, kvh * HEAD_DIM), both_halves(kvp_ref, kvh * HEAD_DIM)
        v_lane0 = SWA_KV_WIDTH + kvh * HEAD_DIM
        v_cur_all, v_first = both_halves(kvc_ref, v_lane0), both_halves(kvp_ref, v_lane0)
        pair_cols = [slice((head0 + 2 * p) * HEAD_DIM, (head0 + 2 * p + 2) * HEAD_DIM) for p in range(SWA_REP // 2)]
        for i in range(tq // ATTN_BLOCK):
            rows = slice(i * ATTN_BLOCK, (i + 1) * ATTN_BLOCK)
            if i == 0:
                k_win = jnp.concatenate([k_first, k_cur_all[rows]], axis=0)
                v_win = jnp.concatenate([v_first, v_cur_all[rows]], axis=0)
            else:
                win = slice((i - 1) * ATTN_BLOCK, (i + 1) * ATTN_BLOCK)
                k_win, v_win = k_cur_all[win], v_cur_all[win]
            q = jnp.concatenate([q_ref[rows, c] * keep for c in pair_cols for keep in (keep_low, keep_high)], axis=0)
            s2 = lax.dot_general(q, k_win, _NT, preferred_element_type=F32)
            s = jnp.where(from_prev, s2[:, :ATTN_BLOCK].reshape(tile), s2[:, ATTN_BLOCK:].reshape(tile)) \
                + bias_ref[...]
            if i == 0:
                s = jnp.where(from_prev & first, NEG_INF, s)
            s = s.reshape(stacked)
            m = jnp.maximum(jnp.max(s, axis=-1, keepdims=True), sink_col)
            p = jnp.exp(s - m).reshape(tile)
            p2 = jnp.concatenate([jnp.where(from_prev, p, 0.0).reshape(stacked).astype(BF16),
                                  jnp.where(from_prev, 0.0, p).reshape(stacked).astype(BF16)], axis=1)
            od = jnp.dot(p2, _values_and_ones(v_win), preferred_element_type=F32)
            o = od[:, :LANES] / (od[:, LANES:] + jnp.exp(sink_col - m))
            for p, c in enumerate(pair_cols):
                even = o[2 * p * ATTN_BLOCK:(2 * p + 1) * ATTN_BLOCK]
                odd = o[(2 * p + 1) * ATTN_BLOCK:(2 * p + 2) * ATTN_BLOCK]
                o_ref[rows, c] = jnp.where(low, even, odd).astype(o_ref.dtype)


def _swa_attention(hq, hkv, slopes_b, sinks, bsz, seq, tq=ROW_TILE):
    nqb = tq // ATTN_BLOCK
    hq3 = hq.reshape(bsz, seq, SWA_Q_WIDTH)
    hkv3 = hkv.reshape(bsz, seq, 2 * SWA_KV_WIDTH)
    smem = pl.BlockSpec(memory_space=pltpu.SMEM)
    out = pl.pallas_call(
        functools.partial(_swa_attn_kernel, tq=tq),
        grid=(bsz, seq // tq),
        in_specs=[smem, smem,
                  pl.BlockSpec((None, tq, SWA_Q_WIDTH), lambda b, m: (b, m, 0)),
                  pl.BlockSpec((None, tq, 2 * SWA_KV_WIDTH), lambda b, m: (b, m, 0)),
                  pl.BlockSpec((None, ATTN_BLOCK, 2 * SWA_KV_WIDTH),
                               lambda b, m: (b, jnp.maximum(m * nqb - 1, 0), 0))],
        out_specs=pl.BlockSpec((None, tq, SWA_Q_WIDTH), lambda b, m: (b, m, 0)),
        out_shape=jax.ShapeDtypeStruct((bsz, seq, SWA_Q_WIDTH), BF16),
        scratch_shapes=[pltpu.VMEM((SWA_REP, ATTN_BLOCK, ATTN_BLOCK), F32)],
        compiler_params=_cparams("parallel", "arbitrary"),
        name="swa_attn",
    )(slopes_b, sinks, hq3, hkv3, hkv3)
    return out.reshape(bsz * seq, SWA_Q_WIDTH)


def _sigmoid(x):
    return 0.5 * (jnp.tanh(0.5 * x) + 1.0)


def _layer_norm(z, g, b):
    mu = jnp.mean(z, axis=-1, keepdims=True)
    zc = z - mu
    var = jnp.mean(zc * zc, axis=-1, keepdims=True)
    return zc * lax.rsqrt(var + LN_EPS) * g + b


def _mix_out_kernel(x_ref, o0_ref, o1_ref, o2_ref, l0_ref, l1_ref, l2_ref, ob_ref, g_ref,
                    wpa_ref, wpb_ref, wo_ref, lng_ref, lnb_ref, rw_ref, rb_ref,
                    y_ref, ypk_ref, idx_ref, gate_ref, *scratch, d_model):
    def natural(ref, stages):
        dil, per, _ = ref.shape
        if dil == 1:
            return ref[0].astype(F32)
        for half, stage in enumerate(stages):
            for r in range(dil):
                stage[pl.ds(r, per, stride=dil), :] = ref[r, :, half * LANES:(half + 1) * LANES].astype(F32)
        return jnp.concatenate([stage[...] for stage in stages], axis=1)

    o_nat = natural(o0_ref, None), natural(o1_ref, scratch[0:2]), natural(o2_ref, scratch[2:4])
    l_nat = natural(l0_ref, None), natural(l1_ref, scratch[4:6]), natural(l2_ref, scratch[6:8])

    def chain(rows):
        (o0, o1, o2), (l0, l1, l2) = [a[rows] for a in o_nat], [a[rows] for a in l_nat]
        lm = jnp.maximum(jnp.maximum(l0, l1), l2)
        e0, e1, e2 = jnp.exp(l0 - lm), jnp.exp(l1 - lm), jnp.exp(l2 - lm)
        out_a = (e0 * o0 + e1 * o1 + e2 * o2) / (e0 + e1 + e2)
        pa = jnp.dot(out_a.astype(BF16), wpa_ref[...], preferred_element_type=F32)
        pb = jnp.dot(ob_ref[rows, :], wpb_ref[...], preferred_element_type=F32)
        ga = _sigmoid(g_ref[rows, :d_model].astype(F32))
        gb = _sigmoid(g_ref[rows, d_model:].astype(F32))
        merged = ga * pa + gb * pb
        mix = jnp.dot(merged.astype(BF16), wo_ref[...], preferred_element_type=F32)
        y = _layer_norm(DEEPNORM_ALPHA * x_ref[rows, :] + mix, lng_ref[...], lnb_ref[...])
        y_ref[rows, :] = y
        ypk_ref[rows, :] = _pack_bf16_pairs(y)

        logits = lax.dot_general(rw_ref[...], y.astype(BF16), _NT, preferred_element_type=F32) + rb_ref[...]
        expert = lax.broadcasted_iota(I32, logits.shape, 0)
        slot = lax.broadcasted_iota(I32, (SUBLANES, logits.shape[1]), 0)
        idx_out = jnp.zeros(slot.shape, I32)
        val_out = jnp.full(slot.shape, -jnp.inf, F32)
        for k in range(TOP_K):
            top = jnp.max(logits, axis=0, keepdims=True)
            top_idx = jnp.min(jnp.where(logits == top, expert, N_EXPERTS), axis=0, keepdims=True)
            idx_out = jnp.where(slot == k, top_idx, idx_out)
            val_out = jnp.where(slot == k, top, val_out)
            logits = jnp.where(expert == top_idx, -jnp.inf, logits)
        ev = jnp.exp(val_out - jnp.max(val_out, axis=0, keepdims=True))
        idx_ref[:, rows] = idx_out
        gate_ref[:, rows] = ev / jnp.sum(ev, axis=0, keepdims=True)

    chain(slice(0, x_ref.shape[0]))


def _mix_out(x2, o_g, lse_g, out_b, gates_h, wpa, wpb, wo, ln_g, ln_b, rw_t, rb_col, seq, tm=ROW_TILE):
    n, d = x2.shape
    tiles = seq // tm
    row = lambda i: (i, 0)
    const = lambda i: (0, 0)
    rb = lambda w: pl.BlockSpec((tm, w), row)
    full = lambda a: pl.BlockSpec(a.shape, const)
    dil_specs = [pl.BlockSpec((None, dil, tm // dil, DIL_OUT), lambda i: (i // tiles, 0, i % tiles, 0))
                 for _, dil in DIL_GROUPS]
    return pl.pallas_call(
        functools.partial(_mix_out_kernel, d_model=d),
        grid=(n // tm,),
        in_specs=[rb(d)] + dil_specs + dil_specs + [rb(SWA_Q_WIDTH), rb(2 * d),
                  full(wpa), full(wpb), full(wo), full(ln_g), full(ln_b), full(rw_t), full(rb_col)],
        out_specs=[rb(d), rb(d // 2), pl.BlockSpec((SUBLANES, tm), lambda i: (0, i)),
                   pl.BlockSpec((SUBLANES, tm), lambda i: (0, i))],
        out_shape=[jax.ShapeDtypeStruct((n, d), F32), jax.ShapeDtypeStruct((n, d // 2), jnp.uint32),
                   jax.ShapeDtypeStruct((SUBLANES, n), I32), jax.ShapeDtypeStruct((SUBLANES, n), F32)],
        scratch_shapes=[pltpu.VMEM((tm, LANES), F32)] * (4 * DIL_OUT // LANES),
        compiler_params=_cparams("parallel"),
        name="mix_out",
    )(x2, *o_g, *lse_g, out_b, gates_h, wpa, wpb, wo, ln_g, ln_b, rw_t, rb_col)


def _token_mixer_ln1(x2, w_in, sinks, w_proj_a, w_proj_b, w_out, ln1_g, ln1_b, router_w, router_b, bsz, seq):
    n, d = x2.shape
    heads = jnp.arange(1, N_ALIBI_HEADS + 1, dtype=F32)
    slopes = jnp.exp2(-8.0 * heads / N_ALIBI_HEADS)
    *a_g, hq, hkv, gates_h = _in_proj(x2, w_in.astype(BF16), bsz, seq)
    o_g, lse_g = [], []
    for g in range(N_DIL):
        sl = slopes[SWA_Q_HEADS + g * DIL_HEADS:SWA_Q_HEADS + (g + 1) * DIL_HEADS]
        o, lse = _dil_attention(a_g[g], sl, g)
        o_g.append(o)
        lse_g.append(lse)
    out_b = _swa_attention(hq, hkv, slopes[:SWA_Q_HEADS], sinks.astype(F32), bsz, seq)
    return _mix_out(x2, o_g, lse_g, out_b, gates_h, w_proj_a.astype(BF16), w_proj_b.astype(BF16),
                    w_out.astype(BF16), ln1_g.reshape(1, d).astype(F32), ln1_b.reshape(1, d).astype(F32),
                    router_w.T.astype(BF16), router_b.reshape(N_EXPERTS, 1).astype(F32), seq)


def _route_kernel(idx_ref, dest_ref, tbl_ref, carry_ref, pstart_ref, *, expert_block):
    phase, i = pl.program_id(0), pl.program_id(1)
    idx = idx_ref[...]
    t = idx.shape[1]
    expert = lax.broadcasted_iota(I32, (N_EXPERTS, t), 0)
    onehot = [expert == idx[k:k + 1, :] for k in range(TOP_K)]
    multi = sum(oh.astype(F32) for oh in onehot)
    tile_cnt = jnp.sum(multi, axis=1, keepdims=True).astype(I32)

    @pl.when((phase == 0) & (i == 0))
    def _():
        carry_ref[...] = jnp.zeros_like(carry_ref)

    @pl.when((phase == 1) & (i == 0))
    def _():
        counts = carry_ref[...]
        blocks = (counts + (expert_block - 1)) // expert_block
        er = lax.broadcasted_iota(I32, (N_EXPERTS, N_EXPERTS), 0)
        ec = lax.broadcasted_iota(I32, (N_EXPERTS, N_EXPERTS), 1)
        below = (ec < er).astype(BF16)
        prefix = lambda v: jnp.dot(below, v.astype(F32).astype(BF16), preferred_element_type=F32).astype(I32)
        pstart = (prefix(blocks >> 6) * 64 + prefix(blocks & 63)) * expert_block
        pstart_ref[...] = pstart
        carry_ref[...] = jnp.zeros_like(carry_ref)
        n_lanes = tbl_ref.shape[1]
        count_col, start_col, blocks_col = counts[:, 0:1], pstart[:, 0:1], blocks[:, 0:1]
        block_row0 = lax.broadcasted_iota(I32, (1, n_lanes), 1) * expert_block
        ends_before = (start_col + blocks_col * expert_block <= block_row0).astype(I32)
        block_e = jnp.minimum(jnp.sum(ends_before, axis=0, keepdims=True), N_EXPERTS - 1)
        mine = lax.broadcasted_iota(I32, (N_EXPERTS, n_lanes), 0) == block_e
        count_b = jnp.sum(jnp.where(mine, count_col, 0), axis=0, keepdims=True)
        start_b = jnp.sum(jnp.where(mine, start_col, 0), axis=0, keepdims=True)
        n_valid = jnp.clip(count_b - (block_row0 - start_b), 0, expert_block)
        n_used = jnp.sum(blocks_col, axis=0, keepdims=True)
        row = lax.broadcasted_iota(I32, (SUBLANES, n_lanes), 0)
        tbl_ref[...] = jnp.where(row == 0, block_e, jnp.where(row == 1, n_valid, n_used))

    @pl.when(phase == 1)
    def _():
        r = lax.broadcasted_iota(I32, (t, t), 0)
        c = lax.broadcasted_iota(I32, (t, t), 1)
        earlier = (r < c).astype(BF16)
        cum = jnp.dot(multi.astype(BF16), earlier, preferred_element_type=F32).astype(I32)
        base = cum + carry_ref[:, 0:1] + pstart_ref[:, 0:1]
        slot = lax.broadcasted_iota(I32, (SUBLANES, t), 0)
        dest = jnp.zeros((SUBLANES, t), I32)
        for k in range(TOP_K):
            d_k = jnp.sum(jnp.where(onehot[k], base, 0), axis=0, keepdims=True)
            dest = jnp.where(slot == k, d_k, dest)
        dest_ref[...] = dest

    carry_ref[...] = carry_ref[...] + tile_cnt


def _route(idx_t, expert_block, n_blocks, tile=ROW_TILE):
    n = idx_t.shape[1]
    tile = min(tile, n)
    per_expert = (N_EXPERTS, LANES)
    table = jax.ShapeDtypeStruct((SUBLANES, pl.cdiv(n_blocks, LANES) * LANES), I32)
    return pl.pallas_call(
        functools.partial(_route_kernel, expert_block=expert_block),
        grid=(2, n // tile),
        in_specs=[pl.BlockSpec((SUBLANES, tile), lambda p, i: (0, i))],
        out_specs=[pl.BlockSpec((SUBLANES, tile), lambda p, i: (0, i * p)),
                   pl.BlockSpec(table.shape, lambda p, i: (0, 0))],
        out_shape=[jax.ShapeDtypeStruct((SUBLANES, n), I32), table],
        scratch_shapes=[pltpu.VMEM(per_expert, I32), pltpu.VMEM(per_expert, I32)],
        compiler_params=_cparams("arbitrary", "arbitrary"),
        name="route",
    )(idx_t)


def _sc_dispatch(dest_t, y, rows, window=64):
    n, d = y.shape
    info = plsc.get_sparse_core_info()
    n_cores = info.num_cores
    per_worker = n // (n_cores * info.num_subcores)
    window = min(window, per_worker // 2)
    n_chunks = per_worker // window
    mesh = plsc.VectorSubcoreMesh(core_axis_name="c", subcore_axis_name="s")
    dest_chunks = dest_t.reshape(dest_t.shape[0], n // window, window)

    @functools.partial(pl.kernel, mesh=mesh, out_type=jax.ShapeDtypeStruct((rows, d), y.dtype),
                       scratch_types=[pltpu.VMEM((TOP_K, n_chunks, window), I32)]
                       + [pltpu.VMEM((window, d), y.dtype)] * 2 + [pltpu.SemaphoreType.DMA] * 4,
                       name="sc_dispatch")
    def body(dest_hbm, y_hbm, xs_hbm, idx_v, buf0, buf1, lsem0, lsem1, ssem0, ssem1):
        bufs, lsem, ssem = (buf0, buf1), (lsem0, lsem1), (ssem0, ssem1)
        worker = lax.axis_index("s") * n_cores + lax.axis_index("c")
        base = worker * per_worker
        for k in range(TOP_K):
            pltpu.sync_copy(dest_hbm.at[k, pl.ds(worker * n_chunks, n_chunks)], idx_v.at[k])

        def load(c, b):
            return pltpu.make_async_copy(y_hbm.at[pl.ds(base + c * window, window)], bufs[b], lsem[b])

        def scatter(c, k, b):
            return pltpu.make_async_copy(bufs[b], xs_hbm.at[idx_v.at[k, c]], ssem[b])

        load(0, 0).start()

        @pl.loop(0, n_chunks, step=2)
        def _(c0):
            for b in range(2):
                c = c0 + b
                load(c, b).wait()

                @pl.when(c > 0)
                def _():
                    for k in range(TOP_K):
                        scatter(c - 1, k, 1 - b).wait()

                @pl.when(c + 1 < n_chunks)
                def _():
                    load(c + 1, 1 - b).start()

                for k in range(TOP_K):
                    scatter(c, k, b).start()

        for k in range(TOP_K):
            scatter(n_chunks - 1, k, 1).wait()

    return body(dest_chunks, y)


def _expert_kernel(be_ref, nu_ref, nv_ref, x_ref, wg_ref, bg_ref, wu_ref, bu_ref, wd_ref, bd_ref, o_ref,
                   wg_bf, wu_bf, wd_bf):
    j = pl.program_id(0)
    used = j < nu_ref[0]
    new_expert = (j == 0) | (be_ref[j] != be_ref[jnp.maximum(j - 1, 0)])

    @pl.when(used & new_expert)
    def _():
        chunk = 128
        for src, dst in ((wg_ref, wg_bf), (wu_ref, wu_bf), (wd_ref, wd_bf)):
            for r in range(0, src.shape[0], chunk):
                dst[r:r + chunk, :] = src[r:r + chunk, :].astype(BF16)

    @pl.when(used)
    def _():
        row = lax.broadcasted_iota(I32, x_ref.shape, 0)
        x_lo, x_hi = _unpack_bf16_pairs(jnp.where(row < nv_ref[j], x_ref[...], jnp.uint32(0)))
        xb = jnp.concatenate([x_lo.astype(BF16), x_hi.astype(BF16)], axis=1)
        gt = jnp.dot(xb, wg_bf[...], preferred_element_type=F32) + bg_ref[...]
        up = jnp.dot(xb, wu_bf[...], preferred_element_type=F32) + bu_ref[...]
        gt = jnp.minimum(gt, SWIGLU_LIMIT)
        up = jnp.clip(up, -SWIGLU_LIMIT, SWIGLU_LIMIT)
        hdn = gt * _sigmoid(SWIGLU_ALPHA * gt) * (up + 1.0)
        o_ref[...] = _pack_bf16_pairs(jnp.dot(hdn.astype(BF16), wd_bf[...], preferred_element_type=F32)
                                      + bd_ref[...])

    @pl.when(jnp.logical_not(used))
    def _():
        o_ref[...] = jnp.zeros_like(o_ref)


def _experts(block_e, n_used, n_valid, xs, wg, bg, wu, bu, wd, bd, expert_block):
    rows, _ = xs.shape
    e, d, f = wg.shape
    xmap = lambda j, be, nu, nv: (jnp.minimum(j, nu[0] - 1), 0)
    wmap = lambda j, be, nu, nv: (be[j], 0, 0)
    grid_spec = pltpu.PrefetchScalarGridSpec(
        num_scalar_prefetch=3,
        grid=(rows // expert_block,),
        in_specs=[pl.BlockSpec((expert_block, d // 2), xmap),
                  pl.BlockSpec((None, d, f), wmap), pl.BlockSpec((None, 1, f), wmap),
                  pl.BlockSpec((None, d, f), wmap), pl.BlockSpec((None, 1, f), wmap),
                  pl.BlockSpec((None, f, d), wmap), pl.BlockSpec((None, 1, d), wmap)],
        out_specs=pl.BlockSpec((expert_block, d // 2), lambda j, be, nu, nv: (j, 0)),
        scratch_shapes=[pltpu.VMEM((d, f), BF16), pltpu.VMEM((d, f), BF16), pltpu.VMEM((f, d), BF16)],
    )
    return pl.pallas_call(
        _expert_kernel,
        grid_spec=grid_spec,
        out_shape=jax.ShapeDtypeStruct((rows, d // 2), jnp.uint32),
        compiler_params=_cparams("arbitrary"),
        name="experts",
    )(block_e, n_used, n_valid, xs, wg, bg.reshape(e, 1, f), wu, bu.reshape(e, 1, f), wd, bd.reshape(e, 1, d))


def _sc_gather(dest_t, ys, n, window=64):
    _, d = ys.shape
    info = plsc.get_sparse_core_info()
    n_cores = info.num_cores
    per_worker = n // (n_cores * info.num_subcores)
    mesh = plsc.VectorSubcoreMesh(core_axis_name="c", subcore_axis_name="s")

    n_chunks = per_worker // window
    n_buf = 2

    @functools.partial(pl.kernel, mesh=mesh, out_type=jax.ShapeDtypeStruct((TOP_K, n, d), ys.dtype),
                       scratch_types=[pltpu.VMEM((TOP_K, per_worker), I32)]
                       + [pltpu.VMEM((window, d), ys.dtype)] * n_buf + [pltpu.SemaphoreType.DMA] * (2 * n_buf),
                       name="sc_gather")
    def body(dest_hbm, ys_hbm, out_hbm, idx_v, buf0, buf1, gsem0, gsem1, ssem0, ssem1):
        bufs, gsem, ssem = (buf0, buf1), (gsem0, gsem1), (ssem0, ssem1)
        base = (lax.axis_index("s") * n_cores + lax.axis_index("c")) * per_worker
        for k in range(TOP_K):
            pltpu.sync_copy(dest_hbm.at[k, pl.ds(base, per_worker)], idx_v.at[k])

        def gather(c, k):
            return pltpu.make_async_copy(ys_hbm.at[idx_v.at[k, pl.ds(c * window, window)]], bufs[k % n_buf],
                                         gsem[k % n_buf])

        def store(c, k):
            return pltpu.make_async_copy(bufs[k % n_buf], out_hbm.at[k, pl.ds(base + c * window, window)],
                                         ssem[k % n_buf])

        gather(0, 0).start()

        @pl.loop(0, n_chunks)
        def _(c):
            for k in range(TOP_K):
                if k > 0:
                    store(c, k - 1).wait()
                    if k + 1 < TOP_K:
                        gather(c, k + 1).start()
                    else:
                        @pl.when(c + 1 < n_chunks)
                        def _():
                            gather(c + 1, 0).start()
                else:
                    @pl.when(c > 0)
                    def _():
                        store(c - 1, TOP_K - 1).wait()
                    gather(c, 1).start()
                gather(c, k).wait()
                store(c, k).start()

        store(n_chunks - 1, TOP_K - 1).wait()

    return body(dest_t, ys)


def _combine_dense_kernel(gate_ref, y_ref, lng_ref, lnb_ref, g_ref, o_ref):
    gates = gate_ref[...].T
    ffn_lo = ffn_hi = None
    for k in range(TOP_K):
        lo, hi = _unpack_bf16_pairs(g_ref[k])
        gate = gates[:, k:k + 1]
        ffn_lo = gate * lo if k == 0 else ffn_lo + gate * lo
        ffn_hi = gate * hi if k == 0 else ffn_hi + gate * hi
    ffn = jnp.concatenate([ffn_lo, ffn_hi], axis=1)
    o_ref[...] = _layer_norm(DEEPNORM_ALPHA * y_ref[...] + ffn, lng_ref[...], lnb_ref[...])


def _combine_dense(gates, y, ln_g, ln_b, gathered, tile=ROW_TILE):
    n, d = y.shape
    tile = min(tile, n)
    row = lambda i: (i, 0)
    const = lambda i: (0, 0)
    return pl.pallas_call(
        _combine_dense_kernel,
        grid=(n // tile,),
        in_specs=[pl.BlockSpec((SUBLANES, tile), lambda i: (0, i)), pl.BlockSpec((tile, d), row),
                  pl.BlockSpec((1, d), const), pl.BlockSpec((1, d), const),
                  pl.BlockSpec((TOP_K, tile, d // 2), lambda i: (0, i, 0))],
        out_specs=pl.BlockSpec((tile, d), row),
        out_shape=jax.ShapeDtypeStruct((n, d), F32),
        compiler_params=_cparams("parallel"),
        name="combine_dense",
    )(gates, y, ln_g, ln_b, gathered)


MOE_ROW_BLOCK = 512


def _moe_ln2(y, y_packed, idx, gates, w_gate, b_gate, w_up, b_up, w_down, b_down, ln2_g, ln2_b):
    n, d = y.shape
    n_blocks = n * TOP_K // MOE_ROW_BLOCK + N_EXPERTS
    rows = n_blocks * MOE_ROW_BLOCK
    dest_t, block_table = _route(idx, MOE_ROW_BLOCK, n_blocks)
    block_e, n_valid, n_used = block_table[0, :n_blocks], block_table[1, :n_blocks], block_table[2, :1]
    xs = _sc_dispatch(dest_t, y_packed, rows)
    ys = _experts(block_e, n_used, n_valid, xs, w_gate, b_gate.astype(F32), w_up, b_up.astype(F32),
                  w_down, b_down.astype(F32), MOE_ROW_BLOCK)
    gathered = _sc_gather(dest_t, ys, n)
    return _combine_dense(gates, y, ln2_g.reshape(1, d).astype(F32), ln2_b.reshape(1, d).astype(F32), gathered)


def kernel(x, w_in, sinks, w_proj_a, w_proj_b, w_out, ln1_g, ln1_b, router_w, router_b,
           w_gate, b_gate, w_up, b_up, w_down, b_down, ln2_g, ln2_b):
    bsz, seq, d = x.shape
    assert seq % (DIL_GROUPS[-1][1] * ATTN_BLOCK) == 0 and seq % IN_PROJ_TILE == 0
    assert w_in.shape[-1] == GATE_OFF + 2 * d and w_proj_b.shape[-2] == SWA_Q_WIDTH
    assert router_w.shape[-1] == N_EXPERTS and (bsz * seq * TOP_K) % MOE_ROW_BLOCK == 0
    h = x.reshape(bsz * seq, d)
    for l in range(w_in.shape[0]):
        y, y_packed, idx, gates = _token_mixer_ln1(h, w_in[l], sinks[l], w_proj_a[l], w_proj_b[l], w_out[l],
                                                   ln1_g[l], ln1_b[l], router_w[l], router_b[l], bsz, seq)
        h = _moe_ln2(y, y_packed, idx, gates, w_gate[l], b_gate[l], w_up[l], b_up[l], w_down[l], b_down[l],
                     ln2_g[l], ln2_b[l])
    return h.reshape(bsz, seq, d)
```

```python
import functools

import jax
import jax.numpy as jnp
from jax import lax
from jax.experimental import pallas as pl
from jax.experimental.pallas import tpu as pltpu
from jax.experimental.pallas import tpu_sc as plsc

F32 = jnp.float32
BF16 = jnp.bfloat16
I32 = jnp.int32

HEAD_DIM = 64
DIL_GROUPS = ((128, 1), (512, 4), (2048, 16))
DIL_HEADS = 4
N_DIL = len(DIL_GROUPS)
DIL_OUT = DIL_HEADS * HEAD_DIM
DIL_WIDTH = N_DIL * DIL_OUT
SWA_Q_HEADS = 16
SWA_KV_HEADS = 2
SWA_REP = SWA_Q_HEADS // SWA_KV_HEADS
SWA_WINDOW = 128
SWA_Q_WIDTH = SWA_Q_HEADS * HEAD_DIM
SWA_KV_WIDTH = SWA_KV_HEADS * HEAD_DIM
N_ALIBI_HEADS = SWA_Q_HEADS + N_DIL * DIL_HEADS
ATTN_BLOCK = 128
N_EXPERTS = 32
TOP_K = 4
SWIGLU_LIMIT = 7.0
SWIGLU_ALPHA = 1.702
LN_EPS = 1e-5
DEPTH = 1
DEEPNORM_ALPHA = (2 * DEPTH) ** 0.25
NEG_INF = -1e30

LANES = 128
SUBLANES = 8
VMEM_LIMIT_BYTES = 56 * 1024 * 1024
ROW_TILE = 512
IN_PROJ_TILE = 2 * ROW_TILE
COL_CHUNK = 512
DIL_ATTN_ROWS = 2 * ROW_TILE

A_QKV_W = 3 * DIL_WIDTH
B_Q_OFF = A_QKV_W
B_KV_OFF = B_Q_OFF + SWA_Q_WIDTH
GATE_OFF = B_KV_OFF + 2 * SWA_KV_WIDTH


def _cparams(*sem):
    return pltpu.CompilerParams(dimension_semantics=sem, vmem_limit_bytes=VMEM_LIMIT_BYTES)


def _pack_bf16_pairs(x):
    c = x.shape[1] // 2
    lo = lax.bitcast_convert_type(x[:, :c].astype(BF16).astype(F32), jnp.uint32)
    hi = lax.bitcast_convert_type(x[:, c:].astype(BF16).astype(F32), jnp.uint32)
    return (lo >> 16) | hi


def _unpack_bf16_pairs(w):
    lo = lax.bitcast_convert_type(w << 16, F32)
    hi = lax.bitcast_convert_type(w & jnp.uint32(0xFFFF0000), F32)
    return lo, hi


def _in_proj_kernel(x_ref, w_ref, a0_ref, a1_ref, a2_ref, hq_ref, hkv_ref, g_ref, *scratch, d_model, tm):
    xb = x_ref[...].astype(BF16)
    segments = ((hq_ref, B_Q_OFF, SWA_Q_WIDTH), (hkv_ref, B_KV_OFF, 2 * SWA_KV_WIDTH),
                (g_ref, GATE_OFF, 2 * d_model))
    for out_ref, col0, width in segments:
        for c in range(0, width, COL_CHUNK):
            w = min(COL_CHUNK, width - c)
            r = jnp.dot(xb, w_ref[:, col0 + c:col0 + c + w], preferred_element_type=F32)
            out_ref[:, c:c + w] = r.astype(out_ref.dtype)
    for g, a_ref in enumerate((a0_ref, a1_ref, a2_ref)):
        dil = DIL_GROUPS[g][1]
        per = tm // dil
        for part in range(3):
            col0 = part * DIL_WIDTH + g * DIL_OUT
            res = jnp.dot(xb, w_ref[:, col0:col0 + DIL_OUT], preferred_element_type=F32)
            if dil == 1:
                a_ref[0, :, part * DIL_OUT:(part + 1) * DIL_OUT] = res.astype(a_ref.dtype)
                continue
            for half in range(DIL_OUT // LANES):
                stage = scratch[part * (DIL_OUT // LANES) + half]
                stage[...] = res[:, half * LANES:(half + 1) * LANES]
                c0 = part * DIL_OUT + half * LANES
                for r in range(dil):
                    a_ref[r, :, c0:c0 + LANES] = stage[pl.ds(r, per, stride=dil), :].astype(a_ref.dtype)


def _in_proj(x2, w_in_bf, bsz, seq, tm=IN_PROJ_TILE):
    n, d = x2.shape
    cols = w_in_bf.shape[1]
    tiles = seq // tm
    row = lambda i: (i, 0)
    dil_spec = lambda dil: pl.BlockSpec((None, dil, tm // dil, 3 * DIL_OUT), lambda i: (i // tiles, 0, i % tiles, 0))
    dil_shape = lambda dil: jax.ShapeDtypeStruct((bsz, dil, seq // dil, 3 * DIL_OUT), BF16)
    dils = [dil for _, dil in DIL_GROUPS]
    return pl.pallas_call(
        functools.partial(_in_proj_kernel, d_model=d, tm=tm),
        grid=(n // tm,),
        in_specs=[pl.BlockSpec((tm, d), row),
                  pl.BlockSpec((d, cols), lambda i: (0, 0), pipeline_mode=pl.Buffered(1))],
        out_specs=[dil_spec(dil) for dil in dils]
        + [pl.BlockSpec((tm, SWA_Q_WIDTH), row), pl.BlockSpec((tm, 2 * SWA_KV_WIDTH), row),
           pl.BlockSpec((tm, 2 * d), row)],
        out_shape=[dil_shape(dil) for dil in dils]
        + [jax.ShapeDtypeStruct((n, SWA_Q_WIDTH), BF16), jax.ShapeDtypeStruct((n, 2 * SWA_KV_WIDTH), BF16),
           jax.ShapeDtypeStruct((n, 2 * d), BF16)],
        scratch_shapes=[pltpu.VMEM((tm, LANES), F32)] * (3 * DIL_OUT // LANES),
        compiler_params=_cparams("parallel"),
        name="in_proj",
    )(x2, w_in_bf)


_NT = (((1,), (1,)), ((), ()))


def _per_head_column(ref, head0, n_heads, scale=1.0):
    head_of_row = lax.broadcasted_iota(I32, (n_heads * ATTN_BLOCK, 1), 0) // ATTN_BLOCK
    col = jnp.zeros((n_heads * ATTN_BLOCK, 1), F32)
    for h in range(n_heads):
        col = jnp.where(head_of_row == h, ref[head0 + h] * scale, col)
    return col


def _store_band_bias(bias_ref, slope_col, max_diff):
    rows = slope_col.shape[0]
    qi = lax.broadcasted_iota(I32, (rows, ATTN_BLOCK), 0) % ATTN_BLOCK
    kj = lax.broadcasted_iota(I32, (rows, ATTN_BLOCK), 1)
    diff_prev = qi - kj + ATTN_BLOCK
    diff_cur = qi - kj
    bias_ref[0] = jnp.where(diff_prev <= max_diff, -slope_col * diff_prev.astype(F32), NEG_INF)
    bias_ref[1] = jnp.where((diff_cur >= 0) & (diff_cur <= max_diff), -slope_col * diff_cur.astype(F32), NEG_INF)


def _band_probs(s_prev, s_cur, sink_col):
    m = jnp.max(jnp.maximum(s_prev, s_cur), axis=-1, keepdims=True)
    if sink_col is not None:
        m = jnp.maximum(m, sink_col)
    return jnp.exp(s_prev - m).astype(BF16), jnp.exp(s_cur - m).astype(BF16), m


def _pair_masks():
    low = lax.broadcasted_iota(I32, (ATTN_BLOCK, LANES), 1) < HEAD_DIM
    scale = HEAD_DIM ** -0.5
    return low, jnp.where(low, scale, 0.0).astype(BF16), jnp.where(low, 0.0, scale).astype(BF16)


def _values_and_ones(v):
    return jnp.concatenate([v, jnp.ones((v.shape[0], LANES), v.dtype)], axis=1)


def _dil_attn_kernel(slope_ref, q_ref, kc_ref, kp_ref, vc_ref, vp_ref, o_ref, lse_ref, bias_ref,
                     *, tq, max_diff, dist_scale):
    first = pl.program_id(2) == 0
    _store_band_bias(bias_ref, _per_head_column(slope_ref, 0, DIL_HEADS, dist_scale), max_diff)
    low, keep_low, keep_high = _pair_masks()
    pair_cols = [slice(p * LANES, (p + 1) * LANES) for p in range(DIL_HEADS // 2)]
    for res, i in [(res, i) for res in range(q_ref.shape[0]) for i in range(tq // ATTN_BLOCK)]:
        rows = slice(i * ATTN_BLOCK, (i + 1) * ATTN_BLOCK)
        window = slice((i - 1) * ATTN_BLOCK, (i + 1) * ATTN_BLOCK)

        def key_window(cur_ref, first_prev_ref, c):
            if i == 0:
                return jnp.concatenate([first_prev_ref[res, :, c], cur_ref[res, rows, c]], axis=0)
            return cur_ref[res, window, c]

        scores = []
        for c in pair_cols:
            k_win = key_window(kc_ref, kp_ref, c)
            for keep in (keep_low, keep_high):
                scores.append(lax.dot_general(q_ref[res, rows, c] * keep, k_win, _NT, preferred_element_type=F32))
        scores = jnp.concatenate(scores, axis=0)
        s_prev = scores[:, :ATTN_BLOCK] + bias_ref[0]
        s_cur = scores[:, ATTN_BLOCK:] + bias_ref[1]
        if i == 0:
            s_prev = jnp.where(first, NEG_INF, s_prev)
        p_prev, p_cur, m = _band_probs(s_prev, s_cur, None)
        probs = jnp.concatenate([p_prev, p_cur], axis=1)
        for p, c in enumerate(pair_cols):
            v_win = _values_and_ones(key_window(vc_ref, vp_ref, c))
            od, mh = [], []
            for h in (2 * p, 2 * p + 1):
                hr = slice(h * ATTN_BLOCK, (h + 1) * ATTN_BLOCK)
                od.append(jnp.dot(probs[hr], v_win, preferred_element_type=F32))
                mh.append(m[hr])
            denom = jnp.where(low, od[0][:, LANES:], od[1][:, LANES:])
            o_ref[res, rows, c] = (jnp.where(low, od[0][:, :LANES], od[1][:, :LANES]) / denom).astype(o_ref.dtype)
            lse_ref[res, rows, c] = jnp.where(low, mh[0], mh[1]) + jnp.log(denom)


def _dil_attention(a_g, slopes_g, g):
    window, dil = DIL_GROUPS[g]
    bsz, _, sub_len, _ = a_g.shape
    tq = min(DIL_ATTN_ROWS, sub_len)
    res_per_step = DIL_ATTN_ROWS // tq
    nqb = tq // ATTN_BLOCK
    cur = lambda part: (lambda b, r, m: (b, r, m, part))
    prev = lambda part: (lambda b, r, m: (b, r, jnp.maximum(m * nqb - 1, 0), part))
    blk = lambda rows, imap: pl.BlockSpec((None, res_per_step, rows, DIL_OUT), imap)
    return pl.pallas_call(
        functools.partial(_dil_attn_kernel, tq=tq, max_diff=window // dil, dist_scale=float(dil)),
        grid=(bsz, dil // res_per_step, sub_len // tq),
        in_specs=[pl.BlockSpec(memory_space=pltpu.SMEM),
                  blk(tq, cur(0)), blk(tq, cur(1)), blk(ATTN_BLOCK, prev(1)),
                  blk(tq, cur(2)), blk(ATTN_BLOCK, prev(2))],
        out_specs=[blk(tq, cur(0)), blk(tq, cur(0))],
        out_shape=[jax.ShapeDtypeStruct((bsz, dil, sub_len, DIL_OUT), BF16),
                   jax.ShapeDtypeStruct((bsz, dil, sub_len, DIL_OUT), F32)],
        scratch_shapes=[pltpu.VMEM((2, DIL_HEADS * ATTN_BLOCK, ATTN_BLOCK), F32)],
        compiler_params=_cparams("parallel", "parallel", "arbitrary"),
        name=f"dil_attn_g{g}",
    )(slopes_g, a_g, a_g, a_g, a_g, a_g)


def _swa_attn_kernel(slope_ref, sink_ref, q_ref, kvc_ref, kvp_ref, o_ref, bias_ref, *, tq):
    first = pl.program_id(1) == 0
    low, keep_low, keep_high = _pair_masks()

    def both_halves(ref, lane0):
        part = ref[:, lane0:lane0 + HEAD_DIM]
        return jnp.concatenate([part, part], axis=1)

    tile = (SWA_REP, ATTN_BLOCK, ATTN_BLOCK)
    stacked = (SWA_REP * ATTN_BLOCK, ATTN_BLOCK)
    qi = lax.broadcasted_iota(I32, (ATTN_BLOCK, ATTN_BLOCK), 0)
    kj = lax.broadcasted_iota(I32, (ATTN_BLOCK, ATTN_BLOCK), 1)
    from_prev = kj > qi
    diff = jnp.where(from_prev, qi - kj + ATTN_BLOCK, qi - kj).astype(F32)
    for kvh in range(SWA_KV_HEADS):
        head0 = kvh * SWA_REP
        for r in range(SWA_REP):
            bias_ref[r] = -slope_ref[head0 + r] * diff
        sink_col = _per_head_column(sink_ref, head0, SWA_REP)
        k_cur_all, k_first = both_halves(kvc_ref, kvh * HEAD_DIM), both_halves(kvp_ref, kvh * HEAD_DIM)
        v_lane0 = SWA_KV_WIDTH + kvh * HEAD_DIM
        v_cur_all, v_first = both_halves(kvc_ref, v_lane0), both_halves(kvp_ref, v_lane0)
        pair_cols = [slice((head0 + 2 * p) * HEAD_DIM, (head0 + 2 * p + 2) * HEAD_DIM) for p in range(SWA_REP // 2)]
        for i in range(tq // ATTN_BLOCK):
            rows = slice(i * ATTN_BLOCK, (i + 1) * ATTN_BLOCK)
            if i == 0:
                k_win = jnp.concatenate([k_first, k_cur_all[rows]], axis=0)
                v_win = jnp.concatenate([v_first, v_cur_all[rows]], axis=0)
            else:
                win = slice((i - 1) * ATTN_BLOCK, (i + 1) * ATTN_BLOCK)
                k_win, v_win = k_cur_all[win], v_cur_all[win]
            q = jnp.concatenate([q_ref[rows, c] * keep for c in pair_cols for keep in (keep_low, keep_high)], axis=0)
            s2 = lax.dot_general(q, k_win, _NT, preferred_element_type=F32)
            s = jnp.where(from_prev, s2[:, :ATTN_BLOCK].reshape(tile), s2[:, ATTN_BLOCK:].reshape(tile)) \
                + bias_ref[...]
            if i == 0:
                s = jnp.where(from_prev & first, NEG_INF, s)
            s = s.reshape(stacked)
            m = jnp.maximum(jnp.max(s, axis=-1, keepdims=True), sink_col)
            p = jnp.exp(s - m).reshape(tile)
            p2 = jnp.concatenate([jnp.where(from_prev, p, 0.0).reshape(stacked).astype(BF16),
                                  jnp.where(from_prev, 0.0, p).reshape(stacked).astype(BF16)], axis=1)
            od = jnp.dot(p2, _values_and_ones(v_win), preferred_element_type=F32)
            o = od[:, :LANES] / (od[:, LANES:] + jnp.exp(sink_col - m))
            for p, c in enumerate(pair_cols):
                even = o[2 * p * ATTN_BLOCK:(2 * p + 1) * ATTN_BLOCK]
                odd = o[(2 * p + 1) * ATTN_BLOCK:(2 * p + 2) * ATTN_BLOCK]
                o_ref[rows, c] = jnp.where(low, even, odd).astype(o_ref.dtype)


def _swa_attention(hq, hkv, slopes_b, sinks, bsz, seq, tq=ROW_TILE):
    nqb = tq // ATTN_BLOCK
    hq3 = hq.reshape(bsz, seq, SWA_Q_WIDTH)
    hkv3 = hkv.reshape(bsz, seq, 2 * SWA_KV_WIDTH)
    smem = pl.BlockSpec(memory_space=pltpu.SMEM)
    out = pl.pallas_call(
        functools.partial(_swa_attn_kernel, tq=tq),
        grid=(bsz, seq // tq),
        in_specs=[smem, smem,
                  pl.BlockSpec((None, tq, SWA_Q_WIDTH), lambda b, m: (b, m, 0)),
                  pl.BlockSpec((None, tq, 2 * SWA_KV_WIDTH), lambda b, m: (b, m, 0)),
                  pl.BlockSpec((None, ATTN_BLOCK, 2 * SWA_KV_WIDTH),
                               lambda b, m: (b, jnp.maximum(m * nqb - 1, 0), 0))],
        out_specs=pl.BlockSpec((None, tq, SWA_Q_WIDTH), lambda b, m: (b, m, 0)),
        out_shape=jax.ShapeDtypeStruct((bsz, seq, SWA_Q_WIDTH), BF16),
        scratch_shapes=[pltpu.VMEM((SWA_REP, ATTN_BLOCK, ATTN_BLOCK), F32)],
        compiler_params=_cparams("parallel", "arbitrary"),
        name="swa_attn",
    )(slopes_b, sinks, hq3, hkv3, hkv3)
    return out.reshape(bsz * seq, SWA_Q_WIDTH)


def _sigmoid(x):
    return 0.5 * (jnp.tanh(0.5 * x) + 1.0)


def _layer_norm(z, g, b):
    mu = jnp.mean(z, axis=-1, keepdims=True)
    zc = z - mu
    var = jnp.mean(zc * zc, axis=-1, keepdims=True)
    return zc * lax.rsqrt(var + LN_EPS) * g + b


def _mix_out_kernel(x_ref, o0_ref, o1_ref, o2_ref, l0_ref, l1_ref, l2_ref, ob_ref, g_ref,
                    wpa_ref, wpb_ref, wo_ref, lng_ref, lnb_ref, rw_ref, rb_ref,
                    y_ref, ypk_ref, idx_ref, gate_ref, *scratch, d_model):
    def natural(ref, stages):
        dil, per, _ = ref.shape
        if dil == 1:
            return ref[0].astype(F32)
        for half, stage in enumerate(stages):
            for r in range(dil):
                stage[pl.ds(r, per, stride=dil), :] = ref[r, :, half * LANES:(half + 1) * LANES].astype(F32)
        return jnp.concatenate([stage[...] for stage in stages], axis=1)

    o_nat = natural(o0_ref, None), natural(o1_ref, scratch[0:2]), natural(o2_ref, scratch[2:4])
    l_nat = natural(l0_ref, None), natural(l1_ref, scratch[4:6]), natural(l2_ref, scratch[6:8])

    def chain(rows):
        (o0, o1, o2), (l0, l1, l2) = [a[rows] for a in o_nat], [a[rows] for a in l_nat]
        lm = jnp.maximum(jnp.maximum(l0, l1), l2)
        e0, e1, e2 = jnp.exp(l0 - lm), jnp.exp(l1 - lm), jnp.exp(l2 - lm)
        out_a = (e0 * o0 + e1 * o1 + e2 * o2) / (e0 + e1 + e2)
        pa = jnp.dot(out_a.astype(BF16), wpa_ref[...], preferred_element_type=F32)
        pb = jnp.dot(ob_ref[rows, :], wpb_ref[...], preferred_element_type=F32)
        ga = _sigmoid(g_ref[rows, :d_model].astype(F32))
        gb = _sigmoid(g_ref[rows, d_model:].astype(F32))
        merged = ga * pa + gb * pb
        mix = jnp.dot(merged.astype(BF16), wo_ref[...], preferred_element_type=F32)
        y = _layer_norm(DEEPNORM_ALPHA * x_ref[rows, :] + mix, lng_ref[...], lnb_ref[...])
        y_ref[rows, :] = y
        ypk_ref[rows, :] = _pack_bf16_pairs(y)

        logits = lax.dot_general(rw_ref[...], y.astype(BF16), _NT, preferred_element_type=F32) + rb_ref[...]
        expert = lax.broadcasted_iota(I32, logits.shape, 0)
        slot = lax.broadcasted_iota(I32, (SUBLANES, logits.shape[1]), 0)
        idx_out = jnp.zeros(slot.shape, I32)
        val_out = jnp.full(slot.shape, -jnp.inf, F32)
        for k in range(TOP_K):
            top = jnp.max(logits, axis=0, keepdims=True)
            top_idx = jnp.min(jnp.where(logits == top, expert, N_EXPERTS), axis=0, keepdims=True)
            idx_out = jnp.where(slot == k, top_idx, idx_out)
            val_out = jnp.where(slot == k, top, val_out)
            logits = jnp.where(expert == top_idx, -jnp.inf, logits)
        ev = jnp.exp(val_out - jnp.max(val_out, axis=0, keepdims=True))
        idx_ref[:, rows] = idx_out
        gate_ref[:, rows] = ev / jnp.sum(ev, axis=0, keepdims=True)

    chain(slice(0, x_ref.shape[0]))


def _mix_out(x2, o_g, lse_g, out_b, gates_h, wpa, wpb, wo, ln_g, ln_b, rw_t, rb_col, seq, tm=ROW_TILE):
    n, d = x2.shape
    tiles = seq // tm
    row = lambda i: (i, 0)
    const = lambda i: (0, 0)
    rb = lambda w: pl.BlockSpec((tm, w), row)
    full = lambda a: pl.BlockSpec(a.shape, const)
    dil_specs = [pl.BlockSpec((None, dil, tm // dil, DIL_OUT), lambda i: (i // tiles, 0, i % tiles, 0))
                 for _, dil in DIL_GROUPS]
    return pl.pallas_call(
        functools.partial(_mix_out_kernel, d_model=d),
        grid=(n // tm,),
        in_specs=[rb(d)] + dil_specs + dil_specs + [rb(SWA_Q_WIDTH), rb(2 * d),
                  full(wpa), full(wpb), full(wo), full(ln_g), full(ln_b), full(rw_t), full(rb_col)],
        out_specs=[rb(d), rb(d // 2), pl.BlockSpec((SUBLANES, tm), lambda i: (0, i)),
                   pl.BlockSpec((SUBLANES, tm), lambda i: (0, i))],
        out_shape=[jax.ShapeDtypeStruct((n, d), F32), jax.ShapeDtypeStruct((n, d // 2), jnp.uint32),
                   jax.ShapeDtypeStruct((SUBLANES, n), I32), jax.ShapeDtypeStruct((SUBLANES, n), F32)],
        scratch_shapes=[pltpu.VMEM((tm, LANES), F32)] * (4 * DIL_OUT // LANES),
        compiler_params=_cparams("parallel"),
        name="mix_out",
    )(x2, *o_g, *lse_g, out_b, gates_h, wpa, wpb, wo, ln_g, ln_b, rw_t, rb_col)


def _token_mixer_ln1(x2, w_in, sinks, w_proj_a, w_proj_b, w_out, ln1_g, ln1_b, router_w, router_b, bsz, seq):
    n, d = x2.shape
    heads = jnp.arange(1, N_ALIBI_HEADS + 1, dtype=F32)
    slopes = jnp.exp2(-8.0 * heads / N_ALIBI_HEADS)
    *a_g, hq, hkv, gates_h = _in_proj(x2, w_in.astype(BF16), bsz, seq)
    o_g, lse_g = [], []
    for g in range(N_DIL):
        sl = slopes[SWA_Q_HEADS + g * DIL_HEADS:SWA_Q_HEADS + (g + 1) * DIL_HEADS]
        o, lse = _dil_attention(a_g[g], sl, g)
        o_g.append(o)
        lse_g.append(lse)
    out_b = _swa_attention(hq, hkv, slopes[:SWA_Q_HEADS], sinks.astype(F32), bsz, seq)
    return _mix_out(x2, o_g, lse_g, out_b, gates_h, w_proj_a.astype(BF16), w_proj_b.astype(BF16),
                    w_out.astype(BF16), ln1_g.reshape(1, d).astype(F32), ln1_b.reshape(1, d).astype(F32),
                    router_w.T.astype(BF16), router_b.reshape(N_EXPERTS, 1).astype(F32), seq)


def _route_kernel(idx_ref, dest_ref, tbl_ref, carry_ref, pstart_ref, *, expert_block):
    phase, i = pl.program_id(0), pl.program_id(1)
    idx = idx_ref[...]
    t = idx.shape[1]
    expert = lax.broadcasted_iota(I32, (N_EXPERTS, t), 0)
    onehot = [expert == idx[k:k + 1, :] for k in range(TOP_K)]
    multi = sum(oh.astype(F32) for oh in onehot)
    tile_cnt = jnp.sum(multi, axis=1, keepdims=True).astype(I32)

    @pl.when((phase == 0) & (i == 0))
    def _():
        carry_ref[...] = jnp.zeros_like(carry_ref)

    @pl.when((phase == 1) & (i == 0))
    def _():
        counts = carry_ref[...]
        blocks = (counts + (expert_block - 1)) // expert_block
        er = lax.broadcasted_iota(I32, (N_EXPERTS, N_EXPERTS), 0)
        ec = lax.broadcasted_iota(I32, (N_EXPERTS, N_EXPERTS), 1)
        below = (ec < er).astype(BF16)
        prefix = lambda v: jnp.dot(below, v.astype(F32).astype(BF16), preferred_element_type=F32).astype(I32)
        pstart = (prefix(blocks >> 6) * 64 + prefix(blocks & 63)) * expert_block
        pstart_ref[...] = pstart
        carry_ref[...] = jnp.zeros_like(carry_ref)
        n_lanes = tbl_ref.shape[1]
        count_col, start_col, blocks_col = counts[:, 0:1], pstart[:, 0:1], blocks[:, 0:1]
        block_row0 = lax.broadcasted_iota(I32, (1, n_lanes), 1) * expert_block
        ends_before = (start_col + blocks_col * expert_block <= block_row0).astype(I32)
        block_e = jnp.minimum(jnp.sum(ends_before, axis=0, keepdims=True), N_EXPERTS - 1)
        mine = lax.broadcasted_iota(I32, (N_EXPERTS, n_lanes), 0) == block_e
        count_b = jnp.sum(jnp.where(mine, count_col, 0), axis=0, keepdims=True)
        start_b = jnp.sum(jnp.where(mine, start_col, 0), axis=0, keepdims=True)
        n_valid = jnp.clip(count_b - (block_row0 - start_b), 0, expert_block)
        n_used = jnp.sum(blocks_col, axis=0, keepdims=True)
        row = lax.broadcasted_iota(I32, (SUBLANES, n_lanes), 0)
        tbl_ref[...] = jnp.where(row == 0, block_e, jnp.where(row == 1, n_valid, n_used))

    @pl.when(phase == 1)
    def _():
        r = lax.broadcasted_iota(I32, (t, t), 0)
        c = lax.broadcasted_iota(I32, (t, t), 1)
        earlier = (r < c).astype(BF16)
        cum = jnp.dot(multi.astype(BF16), earlier, preferred_element_type=F32).astype(I32)
        base = cum + carry_ref[:, 0:1] + pstart_ref[:, 0:1]
        slot = lax.broadcasted_iota(I32, (SUBLANES, t), 0)
        dest = jnp.zeros((SUBLANES, t), I32)
        for k in range(TOP_K):
            d_k = jnp.sum(jnp.where(onehot[k], base, 0), axis=0, keepdims=True)
            dest = jnp.where(slot == k, d_k, dest)
        dest_ref[...] = dest

    carry_ref[...] = carry_ref[...] + tile_cnt


def _route(idx_t, expert_block, n_blocks, tile=2 * ROW_TILE):
    n = idx_t.shape[1]
    tile = min(tile, n)
    per_expert = (N_EXPERTS, LANES)
    table = jax.ShapeDtypeStruct((SUBLANES, pl.cdiv(n_blocks, LANES) * LANES), I32)
    return pl.pallas_call(
        functools.partial(_route_kernel, expert_block=expert_block),
        grid=(2, n // tile),
        in_specs=[pl.BlockSpec((SUBLANES, tile), lambda p, i: (0, i))],
        out_specs=[pl.BlockSpec((SUBLANES, tile), lambda p, i: (0, i * p)),
                   pl.BlockSpec(table.shape, lambda p, i: (0, 0))],
        out_shape=[jax.ShapeDtypeStruct((SUBLANES, n), I32), table],
        scratch_shapes=[pltpu.VMEM(per_expert, I32), pltpu.VMEM(per_expert, I32)],
        compiler_params=_cparams("arbitrary", "arbitrary"),
        name="route",
    )(idx_t)


def _sc_dispatch(dest_t, y, rows, window=64):
    n, d = y.shape
    info = plsc.get_sparse_core_info()
    n_cores = info.num_cores
    per_worker = n // (n_cores * info.num_subcores)
    window = min(window, per_worker // 2)
    n_chunks = per_worker // window
    assert n == n_chunks * window * n_cores * info.num_subcores and n_chunks % 2 == 0
    mesh = plsc.VectorSubcoreMesh(core_axis_name="c", subcore_axis_name="s")
    dest_chunks = dest_t.reshape(dest_t.shape[0], n // window, window)

    @functools.partial(pl.kernel, mesh=mesh, out_type=jax.ShapeDtypeStruct((rows, d), y.dtype),
                       scratch_types=[pltpu.VMEM((TOP_K, n_chunks, window), I32)]
                       + [pltpu.VMEM((window, d), y.dtype)] * 2 + [pltpu.SemaphoreType.DMA] * 4,
                       name="sc_dispatch")
    def body(dest_hbm, y_hbm, xs_hbm, idx_v, buf0, buf1, lsem0, lsem1, ssem0, ssem1):
        bufs, lsem, ssem = (buf0, buf1), (lsem0, lsem1), (ssem0, ssem1)
        worker = lax.axis_index("s") * n_cores + lax.axis_index("c")
        base = worker * per_worker
        for k in range(TOP_K):
            pltpu.sync_copy(dest_hbm.at[k, pl.ds(worker * n_chunks, n_chunks)], idx_v.at[k])

        def load(c, b):
            return pltpu.make_async_copy(y_hbm.at[pl.ds(base + c * window, window)], bufs[b], lsem[b])

        def scatter(c, k, b):
            return pltpu.make_async_copy(bufs[b], xs_hbm.at[idx_v.at[k, c]], ssem[b])

        load(0, 0).start()

        @pl.loop(0, n_chunks, step=2)
        def _(c0):
            for b in range(2):
                c = c0 + b
                load(c, b).wait()

                @pl.when(c > 0)
                def _():
                    for k in range(TOP_K):
                        scatter(c - 1, k, 1 - b).wait()

                @pl.when(c + 1 < n_chunks)
                def _():
                    load(c + 1, 1 - b).start()

                for k in range(TOP_K):
                    scatter(c, k, b).start()

        for k in range(TOP_K):
            scatter(n_chunks - 1, k, 1).wait()

    return body(dest_chunks, y)


def _expert_kernel(be_ref, nu_ref, nv_ref, x_ref, wg_ref, bg_ref, wu_ref, bu_ref, wd_ref, bd_ref, o_ref,
                   wg_bf, wu_bf, wd_bf):
    j = pl.program_id(0)
    used = j < nu_ref[0]
    new_expert = (j == 0) | (be_ref[j] != be_ref[jnp.maximum(j - 1, 0)])

    @pl.when(used & new_expert)
    def _():
        chunk = 128
        for src, dst in ((wg_ref, wg_bf), (wu_ref, wu_bf), (wd_ref, wd_bf)):
            for r in range(0, src.shape[0], chunk):
                dst[r:r + chunk, :] = src[r:r + chunk, :].astype(BF16)

    @pl.when(used)
    def _():
        row = lax.broadcasted_iota(I32, x_ref.shape, 0)
        x_lo, x_hi = _unpack_bf16_pairs(jnp.where(row < nv_ref[j], x_ref[...], jnp.uint32(0)))
        xb = jnp.concatenate([x_lo.astype(BF16), x_hi.astype(BF16)], axis=1)
        gt = jnp.dot(xb, wg_bf[...], preferred_element_type=F32) + bg_ref[...]
        up = jnp.dot(xb, wu_bf[...], preferred_element_type=F32) + bu_ref[...]
        gt = jnp.minimum(gt, SWIGLU_LIMIT)
        up = jnp.clip(up, -SWIGLU_LIMIT, SWIGLU_LIMIT)
        hdn = gt * _sigmoid(SWIGLU_ALPHA * gt) * (up + 1.0)
        o_ref[...] = _pack_bf16_pairs(jnp.dot(hdn.astype(BF16), wd_bf[...], preferred_element_type=F32)
                                      + bd_ref[...])

    @pl.when(jnp.logical_not(used))
    def _():
        o_ref[...] = jnp.zeros_like(o_ref)


def _experts(block_e, n_used, n_valid, xs, wg, bg, wu, bu, wd, bd, expert_block):
    rows, _ = xs.shape
    e, d, f = wg.shape
    xmap = lambda j, be, nu, nv: (jnp.minimum(j, nu[0] - 1), 0)
    wmap = lambda j, be, nu, nv: (be[j], 0, 0)
    grid_spec = pltpu.PrefetchScalarGridSpec(
        num_scalar_prefetch=3,
        grid=(rows // expert_block,),
        in_specs=[pl.BlockSpec((expert_block, d // 2), xmap),
                  pl.BlockSpec((None, d, f), wmap), pl.BlockSpec((None, 1, f), wmap),
                  pl.BlockSpec((None, d, f), wmap), pl.BlockSpec((None, 1, f), wmap),
                  pl.BlockSpec((None, f, d), wmap), pl.BlockSpec((None, 1, d), wmap)],
        out_specs=pl.BlockSpec((expert_block, d // 2), lambda j, be, nu, nv: (j, 0)),
        scratch_shapes=[pltpu.VMEM((d, f), BF16), pltpu.VMEM((d, f), BF16), pltpu.VMEM((f, d), BF16)],
    )
    return pl.pallas_call(
        _expert_kernel,
        grid_spec=grid_spec,
        out_shape=jax.ShapeDtypeStruct((rows, d // 2), jnp.uint32),
        compiler_params=_cparams("arbitrary"),
        name="experts",
    )(block_e, n_used, n_valid, xs, wg, bg.reshape(e, 1, f), wu, bu.reshape(e, 1, f), wd, bd.reshape(e, 1, d))


def _sc_gather(dest_t, ys, n, window=64):
    _, d = ys.shape
    info = plsc.get_sparse_core_info()
    n_cores = info.num_cores
    per_worker = n // (n_cores * info.num_subcores)
    mesh = plsc.VectorSubcoreMesh(core_axis_name="c", subcore_axis_name="s")
    window = min(window, per_worker)
    n_chunks = per_worker // window
    assert n == n_chunks * window * n_cores * info.num_subcores
    n_buf = 2

    @functools.partial(pl.kernel, mesh=mesh, out_type=jax.ShapeDtypeStruct((TOP_K, n, d), ys.dtype),
                       scratch_types=[pltpu.VMEM((TOP_K, per_worker), I32)]
                       + [pltpu.VMEM((window, d), ys.dtype)] * n_buf + [pltpu.SemaphoreType.DMA] * (2 * n_buf),
                       name="sc_gather")
    def body(dest_hbm, ys_hbm, out_hbm, idx_v, buf0, buf1, gsem0, gsem1, ssem0, ssem1):
        bufs, gsem, ssem = (buf0, buf1), (gsem0, gsem1), (ssem0, ssem1)
        base = (lax.axis_index("s") * n_cores + lax.axis_index("c")) * per_worker
        for k in range(TOP_K):
            pltpu.sync_copy(dest_hbm.at[k, pl.ds(base, per_worker)], idx_v.at[k])

        def gather(c, k):
            return pltpu.make_async_copy(ys_hbm.at[idx_v.at[k, pl.ds(c * window, window)]], bufs[k % n_buf],
                                         gsem[k % n_buf])

        def store(c, k):
            return pltpu.make_async_copy(bufs[k % n_buf], out_hbm.at[k, pl.ds(base + c * window, window)],
                                         ssem[k % n_buf])

        gather(0, 0).start()

        @pl.loop(0, n_chunks)
        def _(c):
            for k in range(TOP_K):
                if k > 0:
                    store(c, k - 1).wait()
                    if k + 1 < TOP_K:
                        gather(c, k + 1).start()
                    else:
                        @pl.when(c + 1 < n_chunks)
                        def _():
                            gather(c + 1, 0).start()
                else:
                    @pl.when(c > 0)
                    def _():
                        store(c - 1, TOP_K - 1).wait()
                    gather(c, 1).start()
                gather(c, k).wait()
                store(c, k).start()

        store(n_chunks - 1, TOP_K - 1).wait()

    return body(dest_t, ys)


def _combine_dense_kernel(gate_ref, y_ref, lng_ref, lnb_ref, g_ref, o_ref):
    gates = gate_ref[...].T
    ffn_lo = ffn_hi = None
    for k in range(TOP_K):
        lo, hi = _unpack_bf16_pairs(g_ref[k])
        gate = gates[:, k:k + 1]
        ffn_lo = gate * lo if k == 0 else ffn_lo + gate * lo
        ffn_hi = gate * hi if k == 0 else ffn_hi + gate * hi
    ffn = jnp.concatenate([ffn_lo, ffn_hi], axis=1)
    o_ref[...] = _layer_norm(DEEPNORM_ALPHA * y_ref[...] + ffn, lng_ref[...], lnb_ref[...])


def _combine_dense(gates, y, ln_g, ln_b, gathered, tile=2 * ROW_TILE):
    n, d = y.shape
    tile = min(tile, n)
    row = lambda i: (i, 0)
    const = lambda i: (0, 0)
    return pl.pallas_call(
        _combine_dense_kernel,
        grid=(n // tile,),
        in_specs=[pl.BlockSpec((SUBLANES, tile), lambda i: (0, i)), pl.BlockSpec((tile, d), row),
                  pl.BlockSpec((1, d), const), pl.BlockSpec((1, d), const),
                  pl.BlockSpec((TOP_K, tile, d // 2), lambda i: (0, i, 0))],
        out_specs=pl.BlockSpec((tile, d), row),
        out_shape=jax.ShapeDtypeStruct((n, d), F32),
        compiler_params=_cparams("parallel"),
        name="combine_dense",
    )(gates, y, ln_g, ln_b, gathered)


MOE_ROW_BLOCK = 512


def _moe_ln2(y, y_packed, idx, gates, w_gate, b_gate, w_up, b_up, w_down, b_down, ln2_g, ln2_b):
    n, d = y.shape
    n_blocks = n * TOP_K // MOE_ROW_BLOCK + N_EXPERTS
    rows = n_blocks * MOE_ROW_BLOCK
    dest_t, block_table = _route(idx, MOE_ROW_BLOCK, n_blocks)
    block_e, n_valid, n_used = block_table[0, :n_blocks], block_table[1, :n_blocks], block_table[2, :1]
    xs = _sc_dispatch(dest_t, y_packed, rows)
    ys = _experts(block_e, n_used, n_valid, xs, w_gate, b_gate.astype(F32), w_up, b_up.astype(F32),
                  w_down, b_down.astype(F32), MOE_ROW_BLOCK)
    gathered = _sc_gather(dest_t, ys, n)
    return _combine_dense(gates, y, ln2_g.reshape(1, d).astype(F32), ln2_b.reshape(1, d).astype(F32), gathered)


def kernel(x, w_in, sinks, w_proj_a, w_proj_b, w_out, ln1_g, ln1_b, router_w, router_b,
           w_gate, b_gate, w_up, b_up, w_down, b_down, ln2_g, ln2_b):
    bsz, seq, d = x.shape
    assert seq % (DIL_GROUPS[-1][1] * ATTN_BLOCK) == 0 and seq % IN_PROJ_TILE == 0
    assert w_in.shape[-1] == GATE_OFF + 2 * d and w_proj_b.shape[-2] == SWA_Q_WIDTH
    assert router_w.shape[-1] == N_EXPERTS and (bsz * seq * TOP_K) % MOE_ROW_BLOCK == 0
    h = x.reshape(bsz * seq, d)
    for l in range(w_in.shape[0]):
        y, y_packed, idx, gates = _token_mixer_ln1(h, w_in[l], sinks[l], w_proj_a[l], w_proj_b[l], w_out[l],
                                                   ln1_g[l], ln1_b[l], router_w[l], router_b[l], bsz, seq)
        h = _moe_ln2(y, y_packed, idx, gates, w_gate[l], b_gate[l], w_up[l], b_up[l], w_down[l], b_down[l],
                     ln2_g[l], ln2_b[l])
    return h.reshape(bsz, seq, d)
```

```python
import functools

import jax
import jax.numpy as jnp
from jax import lax
from jax.experimental import pallas as pl
from jax.experimental.pallas import tpu as pltpu
from jax.experimental.pallas import tpu_sc as plsc

F32 = jnp.float32
BF16 = jnp.bfloat16
I32 = jnp.int32

HEAD_DIM = 64
DIL_GROUPS = ((128, 1), (512, 4), (2048, 16))
DIL_HEADS = 4
N_DIL = len(DIL_GROUPS)
DIL_OUT = DIL_HEADS * HEAD_DIM
DIL_WIDTH = N_DIL * DIL_OUT
SWA_Q_HEADS = 16
SWA_KV_HEADS = 2
SWA_REP = SWA_Q_HEADS // SWA_KV_HEADS
SWA_WINDOW = 128
SWA_Q_WIDTH = SWA_Q_HEADS * HEAD_DIM
SWA_KV_WIDTH = SWA_KV_HEADS * HEAD_DIM
N_ALIBI_HEADS = SWA_Q_HEADS + N_DIL * DIL_HEADS
ATTN_BLOCK = 128
N_EXPERTS = 32
TOP_K = 4
SWIGLU_LIMIT = 7.0
SWIGLU_ALPHA = 1.702
LN_EPS = 1e-5
DEPTH = 1
DEEPNORM_ALPHA = (2 * DEPTH) ** 0.25
NEG_INF = -1e30

LANES = 128
SUBLANES = 8
VMEM_LIMIT_BYTES = 56 * 1024 * 1024
ROW_TILE = 512
IN_PROJ_TILE = 2 * ROW_TILE
COL_CHUNK = 512
DIL_ATTN_ROWS = 2 * ROW_TILE

A_QKV_W = 3 * DIL_WIDTH
B_Q_OFF = A_QKV_W
B_KV_OFF = B_Q_OFF + SWA_Q_WIDTH
GATE_OFF = B_KV_OFF + 2 * SWA_KV_WIDTH


def _cparams(*sem):
    return pltpu.CompilerParams(dimension_semantics=sem, vmem_limit_bytes=VMEM_LIMIT_BYTES)


def _pack_bf16_pairs(x):
    c = x.shape[1] // 2
    lo = lax.bitcast_convert_type(x[:, :c].astype(BF16).astype(F32), jnp.uint32)
    hi = lax.bitcast_convert_type(x[:, c:].astype(BF16).astype(F32), jnp.uint32)
    return (lo >> 16) | hi


def _unpack_bf16_pairs(w):
    lo = lax.bitcast_convert_type(w << 16, F32)
    hi = lax.bitcast_convert_type(w & jnp.uint32(0xFFFF0000), F32)
    return lo, hi


def _in_proj_kernel(x_ref, w_ref, a0_ref, a1_ref, a2_ref, hq_ref, hkv_ref, g_ref, *scratch, d_model, tm):
    xb = x_ref[...].astype(BF16)
    segments = ((hq_ref, B_Q_OFF, SWA_Q_WIDTH), (hkv_ref, B_KV_OFF, 2 * SWA_KV_WIDTH),
                (g_ref, GATE_OFF, 2 * d_model))
    for out_ref, col0, width in segments:
        for c in range(0, width, COL_CHUNK):
            w = min(COL_CHUNK, width - c)
            r = jnp.dot(xb, w_ref[:, col0 + c:col0 + c + w], preferred_element_type=F32)
            out_ref[:, c:c + w] = r.astype(out_ref.dtype)
    for g, a_ref in enumerate((a0_ref, a1_ref, a2_ref)):
        dil = DIL_GROUPS[g][1]
        per = tm // dil
        for part in range(3):
            col0 = part * DIL_WIDTH + g * DIL_OUT
            res = jnp.dot(xb, w_ref[:, col0:col0 + DIL_OUT], preferred_element_type=F32)
            if dil == 1:
                a_ref[0, :, part * DIL_OUT:(part + 1) * DIL_OUT] = res.astype(a_ref.dtype)
                continue
            for half in range(DIL_OUT // LANES):
                stage = scratch[part * (DIL_OUT // LANES) + half]
                stage[...] = res[:, half * LANES:(half + 1) * LANES]
                c0 = part * DIL_OUT + half * LANES
                for r in range(dil):
                    a_ref[r, :, c0:c0 + LANES] = stage[pl.ds(r, per, stride=dil), :].astype(a_ref.dtype)


def _in_proj(x2, w_in_bf, bsz, seq, tm=IN_PROJ_TILE):
    n, d = x2.shape
    cols = w_in_bf.shape[1]
    tiles = seq // tm
    row = lambda i: (i, 0)
    dil_spec = lambda dil: pl.BlockSpec((None, dil, tm // dil, 3 * DIL_OUT), lambda i: (i // tiles, 0, i % tiles, 0))
    dil_shape = lambda dil: jax.ShapeDtypeStruct((bsz, dil, seq // dil, 3 * DIL_OUT), BF16)
    dils = [dil for _, dil in DIL_GROUPS]
    return pl.pallas_call(
        functools.partial(_in_proj_kernel, d_model=d, tm=tm),
        grid=(n // tm,),
        in_specs=[pl.BlockSpec((tm, d), row),
                  pl.BlockSpec((d, cols), lambda i: (0, 0), pipeline_mode=pl.Buffered(1))],
        out_specs=[dil_spec(dil) for dil in dils]
        + [pl.BlockSpec((tm, SWA_Q_WIDTH), row), pl.BlockSpec((tm, 2 * SWA_KV_WIDTH), row),
           pl.BlockSpec((tm, 2 * d), row)],
        out_shape=[dil_shape(dil) for dil in dils]
        + [jax.ShapeDtypeStruct((n, SWA_Q_WIDTH), BF16), jax.ShapeDtypeStruct((n, 2 * SWA_KV_WIDTH), BF16),
           jax.ShapeDtypeStruct((n, 2 * d), BF16)],
        scratch_shapes=[pltpu.VMEM((tm, LANES), F32)] * (3 * DIL_OUT // LANES),
        compiler_params=_cparams("parallel"),
        name="in_proj",
    )(x2, w_in_bf)


_NT = (((1,), (1,)), ((), ()))


def _per_head_column(ref, head0, n_heads, scale=1.0):
    head_of_row = lax.broadcasted_iota(I32, (n_heads * ATTN_BLOCK, 1), 0) // ATTN_BLOCK
    col = jnp.zeros((n_heads * ATTN_BLOCK, 1), F32)
    for h in range(n_heads):
        col = jnp.where(head_of_row == h, ref[head0 + h] * scale, col)
    return col


def _store_band_bias(bias_ref, slope_col, max_diff):
    rows = slope_col.shape[0]
    qi = lax.broadcasted_iota(I32, (rows, ATTN_BLOCK), 0) % ATTN_BLOCK
    kj = lax.broadcasted_iota(I32, (rows, ATTN_BLOCK), 1)
    diff_prev = qi - kj + ATTN_BLOCK
    diff_cur = qi - kj
    bias_ref[0] = jnp.where(diff_prev <= max_diff, -slope_col * diff_prev.astype(F32), NEG_INF)
    bias_ref[1] = jnp.where((diff_cur >= 0) & (diff_cur <= max_diff), -slope_col * diff_cur.astype(F32), NEG_INF)


def _band_probs(s_prev, s_cur, sink_col):
    m = jnp.max(jnp.maximum(s_prev, s_cur), axis=-1, keepdims=True)
    if sink_col is not None:
        m = jnp.maximum(m, sink_col)
    return jnp.exp(s_prev - m).astype(BF16), jnp.exp(s_cur - m).astype(BF16), m


def _pair_masks():
    low = lax.broadcasted_iota(I32, (ATTN_BLOCK, LANES), 1) < HEAD_DIM
    scale = HEAD_DIM ** -0.5
    return low, jnp.where(low, scale, 0.0).astype(BF16), jnp.where(low, 0.0, scale).astype(BF16)


def _values_and_ones(v):
    return jnp.concatenate([v, jnp.ones((v.shape[0], LANES), v.dtype)], axis=1)


def _dil_attn_kernel(slope_ref, q_ref, kc_ref, kp_ref, vc_ref, vp_ref, o_ref, lse_ref, bias_ref,
                     *, tq, max_diff, dist_scale):
    first = pl.program_id(2) == 0
    _store_band_bias(bias_ref, _per_head_column(slope_ref, 0, DIL_HEADS, dist_scale), max_diff)
    low, keep_low, keep_high = _pair_masks()
    pair_cols = [slice(p * LANES, (p + 1) * LANES) for p in range(DIL_HEADS // 2)]
    for res, i in [(res, i) for res in range(q_ref.shape[0]) for i in range(tq // ATTN_BLOCK)]:
        rows = slice(i * ATTN_BLOCK, (i + 1) * ATTN_BLOCK)
        window = slice((i - 1) * ATTN_BLOCK, (i + 1) * ATTN_BLOCK)

        def key_window(cur_ref, first_prev_ref, c):
            if i == 0:
                return jnp.concatenate([first_prev_ref[res, :, c], cur_ref[res, rows, c]], axis=0)
            return cur_ref[res, window, c]

        scores = []
        for c in pair_cols:
            k_win = key_window(kc_ref, kp_ref, c)
            for keep in (keep_low, keep_high):
                scores.append(lax.dot_general(q_ref[res, rows, c] * keep, k_win, _NT, preferred_element_type=F32))
        scores = jnp.concatenate(scores, axis=0)
        s_prev = scores[:, :ATTN_BLOCK] + bias_ref[0]
        s_cur = scores[:, ATTN_BLOCK:] + bias_ref[1]
        if i == 0:
            s_prev = jnp.where(first, NEG_INF, s_prev)
        p_prev, p_cur, m = _band_probs(s_prev, s_cur, None)
        probs = jnp.concatenate([p_prev, p_cur], axis=1)
        for p, c in enumerate(pair_cols):
            v_win = _values_and_ones(key_window(vc_ref, vp_ref, c))
            od, mh = [], []
            for h in (2 * p, 2 * p + 1):
                hr = slice(h * ATTN_BLOCK, (h + 1) * ATTN_BLOCK)
                od.append(jnp.dot(probs[hr], v_win, preferred_element_type=F32))
                mh.append(m[hr])
            denom = jnp.where(low, od[0][:, LANES:], od[1][:, LANES:])
            o_ref[res, rows, c] = (jnp.where(low, od[0][:, :LANES], od[1][:, :LANES]) / denom).astype(o_ref.dtype)
            lse_ref[res, rows, c] = jnp.where(low, mh[0], mh[1]) + jnp.log(denom)


def _dil_attention(a_g, slopes_g, g):
    window, dil = DIL_GROUPS[g]
    bsz, _, sub_len, _ = a_g.shape
    tq = min(DIL_ATTN_ROWS, sub_len)
    res_per_step = DIL_ATTN_ROWS // tq
    nqb = tq // ATTN_BLOCK
    cur = lambda part: (lambda b, r, m: (b, r, m, part))
    prev = lambda part: (lambda b, r, m: (b, r, jnp.maximum(m * nqb - 1, 0), part))
    blk = lambda rows, imap: pl.BlockSpec((None, res_per_step, rows, DIL_OUT), imap)
    return pl.pallas_call(
        functools.partial(_dil_attn_kernel, tq=tq, max_diff=window // dil, dist_scale=float(dil)),
        grid=(bsz, dil // res_per_step, sub_len // tq),
        in_specs=[pl.BlockSpec(memory_space=pltpu.SMEM),
                  blk(tq, cur(0)), blk(tq, cur(1)), blk(ATTN_BLOCK, prev(1)),
                  blk(tq, cur(2)), blk(ATTN_BLOCK, prev(2))],
        out_specs=[blk(tq, cur(0)), blk(tq, cur(0))],
        out_shape=[jax.ShapeDtypeStruct((bsz, dil, sub_len, DIL_OUT), BF16),
                   jax.ShapeDtypeStruct((bsz, dil, sub_len, DIL_OUT), F32)],
        scratch_shapes=[pltpu.VMEM((2, DIL_HEADS * ATTN_BLOCK, ATTN_BLOCK), F32)],
        compiler_params=_cparams("parallel", "parallel", "arbitrary"),
        name=f"dil_attn_g{g}",
    )(slopes_g, a_g, a_g, a_g, a_g, a_g)


def _swa_attn_kernel(slope_ref, sink_ref, q_ref, kvc_ref, kvp_ref, o_ref, bias_ref, *, tq):
    first = pl.program_id(1) == 0
    low, keep_low, keep_high = _pair_masks()

    def both_halves(ref, lane0):
        part = ref[:, lane0:lane0 + HEAD_DIM]
        return jnp.concatenate([part, part], axis=1)

    tile = (SWA_REP, ATTN_BLOCK, ATTN_BLOCK)
    stacked = (SWA_REP * ATTN_BLOCK, ATTN_BLOCK)
    qi = lax.broadcasted_iota(I32, (ATTN_BLOCK, ATTN_BLOCK), 0)
    kj = lax.broadcasted_iota(I32, (ATTN_BLOCK, ATTN_BLOCK), 1)
    from_prev = kj > qi
    diff = jnp.where(from_prev, qi - kj + ATTN_BLOCK, qi - kj).astype(F32)
    for kvh in range(SWA_KV_HEADS):
        head0 = kvh * SWA_REP
        for r in range(SWA_REP):
            bias_ref[r] = -slope_ref[head0 + r] * diff
        sink_col = _per_head_column(sink_ref, head0, SWA_REP)
        k_cur_all, k_first = both_halves(kvc_ref, kvh * HEAD_DIM), both_halves(kvp_ref, kvh * HEAD_DIM)
        v_lane0 = SWA_KV_WIDTH + kvh * HEAD_DIM
        v_cur_all, v_first = both_halves(kvc_ref, v_lane0), both_halves(kvp_ref, v_lane0)
        pair_cols = [slice((head0 + 2 * p) * HEAD_DIM, (head0 + 2 * p + 2) * HEAD_DIM) for p in range(SWA_REP // 2)]
        for i in range(tq // ATTN_BLOCK):
            rows = slice(i * ATTN_BLOCK, (i + 1) * ATTN_BLOCK)
            if i == 0:
                k_win = jnp.concatenate([k_first, k_cur_all[rows]], axis=0)
                v_win = jnp.concatenate([v_first, v_cur_all[rows]], axis=0)
            else:
                win = slice((i - 1) * ATTN_BLOCK, (i + 1) * ATTN_BLOCK)
                k_win, v_win = k_cur_all[win], v_cur_all[win]
            q = jnp.concatenate([q_ref[rows, c] * keep for c in pair_cols for keep in (keep_low, keep_high)], axis=0)
            s2 = lax.dot_general(q, k_win, _NT, preferred_element_type=F32)
            s = jnp.where(from_prev, s2[:, :ATTN_BLOCK].reshape(tile), s2[:, ATTN_BLOCK:].reshape(tile)) \
                + bias_ref[...]
            if i == 0:
                s = jnp.where(from_prev & first, NEG_INF, s)
            s = s.reshape(stacked)
            m = jnp.maximum(jnp.max(s, axis=-1, keepdims=True), sink_col)
            p = jnp.exp(s - m).reshape(tile)
            p2 = jnp.concatenate([jnp.where(from_prev, p, 0.0).reshape(stacked).astype(BF16),
                                  jnp.where(from_prev, 0.0, p).reshape(stacked).astype(BF16)], axis=1)
            od = jnp.dot(p2, _values_and_ones(v_win), preferred_element_type=F32)
            o = od[:, :LANES] / (od[:, LANES:] + jnp.exp(sink_col - m))
            for p, c in enumerate(pair_cols):
                even = o[2 * p * ATTN_BLOCK:(2 * p + 1) * ATTN_BLOCK]
                odd = o[(2 * p + 1) * ATTN_BLOCK:(2 * p + 2) * ATTN_BLOCK]
                o_ref[rows, c] = jnp.where(low, even, odd).astype(o_ref.dtype)


def _swa_attention(hq, hkv, slopes_b, sinks, bsz, seq, tq=2 * ROW_TILE):
    nqb = tq // ATTN_BLOCK
    hq3 = hq.reshape(bsz, seq, SWA_Q_WIDTH)
    hkv3 = hkv.reshape(bsz, seq, 2 * SWA_KV_WIDTH)
    smem = pl.BlockSpec(memory_space=pltpu.SMEM)
    out = pl.pallas_call(
        functools.partial(_swa_attn_kernel, tq=tq),
        grid=(bsz, seq // tq),
        in_specs=[smem, smem,
                  pl.BlockSpec((None, tq, SWA_Q_WIDTH), lambda b, m: (b, m, 0)),
                  pl.BlockSpec((None, tq, 2 * SWA_KV_WIDTH), lambda b, m: (b, m, 0)),
                  pl.BlockSpec((None, ATTN_BLOCK, 2 * SWA_KV_WIDTH),
                               lambda b, m: (b, jnp.maximum(m * nqb - 1, 0), 0))],
        out_specs=pl.BlockSpec((None, tq, SWA_Q_WIDTH), lambda b, m: (b, m, 0)),
        out_shape=jax.ShapeDtypeStruct((bsz, seq, SWA_Q_WIDTH), BF16),
        scratch_shapes=[pltpu.VMEM((SWA_REP, ATTN_BLOCK, ATTN_BLOCK), F32)],
        compiler_params=_cparams("parallel", "arbitrary"),
        name="swa_attn",
    )(slopes_b, sinks, hq3, hkv3, hkv3)
    return out.reshape(bsz * seq, SWA_Q_WIDTH)


def _sigmoid(x):
    return 0.5 * (jnp.tanh(0.5 * x) + 1.0)


def _layer_norm(z, g, b):
    mu = jnp.mean(z, axis=-1, keepdims=True)
    zc = z - mu
    var = jnp.mean(zc * zc, axis=-1, keepdims=True)
    return zc * lax.rsqrt(var + LN_EPS) * g + b


def _mix_out_kernel(x_ref, o0_ref, o1_ref, o2_ref, l0_ref, l1_ref, l2_ref, ob_ref, g_ref,
                    wpa_ref, wpb_ref, wo_ref, lng_ref, lnb_ref, rw_ref, rb_ref,
                    y_ref, ypk_ref, idx_ref, gate_ref, *scratch, d_model):
    def natural(ref, stages):
        dil, per, _ = ref.shape
        if dil == 1:
            return ref[0].astype(F32)
        for half, stage in enumerate(stages):
            for r in range(dil):
                stage[pl.ds(r, per, stride=dil), :] = ref[r, :, half * LANES:(half + 1) * LANES].astype(F32)
        return jnp.concatenate([stage[...] for stage in stages], axis=1)

    o_nat = natural(o0_ref, None), natural(o1_ref, scratch[0:2]), natural(o2_ref, scratch[2:4])
    l_nat = natural(l0_ref, None), natural(l1_ref, scratch[4:6]), natural(l2_ref, scratch[6:8])

    def chain(rows):
        (o0, o1, o2), (l0, l1, l2) = [a[rows] for a in o_nat], [a[rows] for a in l_nat]
        lm = jnp.maximum(jnp.maximum(l0, l1), l2)
        e0, e1, e2 = jnp.exp(l0 - lm), jnp.exp(l1 - lm), jnp.exp(l2 - lm)
        out_a = (e0 * o0 + e1 * o1 + e2 * o2) / (e0 + e1 + e2)
        pa = jnp.dot(out_a.astype(BF16), wpa_ref[...], preferred_element_type=F32)
        pb = jnp.dot(ob_ref[rows, :], wpb_ref[...], preferred_element_type=F32)
        ga = _sigmoid(g_ref[rows, :d_model].astype(F32))
        gb = _sigmoid(g_ref[rows, d_model:].astype(F32))
        merged = ga * pa + gb * pb
        mix = jnp.dot(merged.astype(BF16), wo_ref[...], preferred_element_type=F32)
        y = _layer_norm(DEEPNORM_ALPHA * x_ref[rows, :] + mix, lng_ref[...], lnb_ref[...])
        y_ref[rows, :] = y
        ypk_ref[rows, :] = _pack_bf16_pairs(y)

        logits = lax.dot_general(rw_ref[...], y.astype(BF16), _NT, preferred_element_type=F32) + rb_ref[...]
        expert = lax.broadcasted_iota(I32, logits.shape, 0)
        slot = lax.broadcasted_iota(I32, (SUBLANES, logits.shape[1]), 0)
        idx_out = jnp.zeros(slot.shape, I32)
        val_out = jnp.full(slot.shape, -jnp.inf, F32)
        for k in range(TOP_K):
            top = jnp.max(logits, axis=0, keepdims=True)
            top_idx = jnp.min(jnp.where(logits == top, expert, N_EXPERTS), axis=0, keepdims=True)
            idx_out = jnp.where(slot == k, top_idx, idx_out)
            val_out = jnp.where(slot == k, top, val_out)
            logits = jnp.where(expert == top_idx, -jnp.inf, logits)
        ev = jnp.exp(val_out - jnp.max(val_out, axis=0, keepdims=True))
        idx_ref[:, rows] = idx_out
        gate_ref[:, rows] = ev / jnp.sum(ev, axis=0, keepdims=True)

    chain(slice(0, x_ref.shape[0]))


def _mix_out(x2, o_g, lse_g, out_b, gates_h, wpa, wpb, wo, ln_g, ln_b, rw_t, rb_col, seq, tm=ROW_TILE):
    n, d = x2.shape
    tiles = seq // tm
    row = lambda i: (i, 0)
    const = lambda i: (0, 0)
    rb = lambda w: pl.BlockSpec((tm, w), row)
    full = lambda a: pl.BlockSpec(a.shape, const)
    dil_specs = [pl.BlockSpec((None, dil, tm // dil, DIL_OUT), lambda i: (i // tiles, 0, i % tiles, 0))
                 for _, dil in DIL_GROUPS]
    return pl.pallas_call(
        functools.partial(_mix_out_kernel, d_model=d),
        grid=(n // tm,),
        in_specs=[rb(d)] + dil_specs + dil_specs + [rb(SWA_Q_WIDTH), rb(2 * d),
                  full(wpa), full(wpb), full(wo), full(ln_g), full(ln_b), full(rw_t), full(rb_col)],
        out_specs=[rb(d), rb(d // 2), pl.BlockSpec((SUBLANES, tm), lambda i: (0, i)),
                   pl.BlockSpec((SUBLANES, tm), lambda i: (0, i))],
        out_shape=[jax.ShapeDtypeStruct((n, d), F32), jax.ShapeDtypeStruct((n, d // 2), jnp.uint32),
                   jax.ShapeDtypeStruct((SUBLANES, n), I32), jax.ShapeDtypeStruct((SUBLANES, n), F32)],
        scratch_shapes=[pltpu.VMEM((tm, LANES), F32)] * (4 * DIL_OUT // LANES),
        compiler_params=_cparams("parallel"),
        name="mix_out",
    )(x2, *o_g, *lse_g, out_b, gates_h, wpa, wpb, wo, ln_g, ln_b, rw_t, rb_col)


def _token_mixer_ln1(x2, w_in, sinks, w_proj_a, w_proj_b, w_out, ln1_g, ln1_b, router_w, router_b, bsz, seq):
    n, d = x2.shape
    heads = jnp.arange(1, N_ALIBI_HEADS + 1, dtype=F32)
    slopes = jnp.exp2(-8.0 * heads / N_ALIBI_HEADS)
    *a_g, hq, hkv, gates_h = _in_proj(x2, w_in.astype(BF16), bsz, seq)
    o_g, lse_g = [], []
    for g in range(N_DIL):
        sl = slopes[SWA_Q_HEADS + g * DIL_HEADS:SWA_Q_HEADS + (g + 1) * DIL_HEADS]
        o, lse = _dil_attention(a_g[g], sl, g)
        o_g.append(o)
        lse_g.append(lse)
    out_b = _swa_attention(hq, hkv, slopes[:SWA_Q_HEADS], sinks.astype(F32), bsz, seq)
    return _mix_out(x2, o_g, lse_g, out_b, gates_h, w_proj_a.astype(BF16), w_proj_b.astype(BF16),
                    w_out.astype(BF16), ln1_g.reshape(1, d).astype(F32), ln1_b.reshape(1, d).astype(F32),
                    router_w.T.astype(BF16), router_b.reshape(N_EXPERTS, 1).astype(F32), seq)


def _route_kernel(idx_ref, dest_ref, tbl_ref, carry_ref, pstart_ref, *, expert_block):
    phase, i = pl.program_id(0), pl.program_id(1)
    idx = idx_ref[...]
    t = idx.shape[1]
    expert = lax.broadcasted_iota(I32, (N_EXPERTS, t), 0)
    onehot = [expert == idx[k:k + 1, :] for k in range(TOP_K)]
    multi = sum(oh.astype(F32) for oh in onehot)
    tile_cnt = jnp.sum(multi, axis=1, keepdims=True).astype(I32)

    @pl.when((phase == 0) & (i == 0))
    def _():
        carry_ref[...] = jnp.zeros_like(carry_ref)

    @pl.when((phase == 1) & (i == 0))
    def _():
        counts = carry_ref[...]
        blocks = (counts + (expert_block - 1)) // expert_block
        er = lax.broadcasted_iota(I32, (N_EXPERTS, N_EXPERTS), 0)
        ec = lax.broadcasted_iota(I32, (N_EXPERTS, N_EXPERTS), 1)
        below = (ec < er).astype(BF16)
        prefix = lambda v: jnp.dot(below, v.astype(F32).astype(BF16), preferred_element_type=F32).astype(I32)
        pstart = (prefix(blocks >> 6) * 64 + prefix(blocks & 63)) * expert_block
        pstart_ref[...] = pstart
        carry_ref[...] = jnp.zeros_like(carry_ref)
        n_lanes = tbl_ref.shape[1]
        count_col, start_col, blocks_col = counts[:, 0:1], pstart[:, 0:1], blocks[:, 0:1]
        block_row0 = lax.broadcasted_iota(I32, (1, n_lanes), 1) * expert_block
        ends_before = (start_col + blocks_col * expert_block <= block_row0).astype(I32)
        block_e = jnp.minimum(jnp.sum(ends_before, axis=0, keepdims=True), N_EXPERTS - 1)
        mine = lax.broadcasted_iota(I32, (N_EXPERTS, n_lanes), 0) == block_e
        count_b = jnp.sum(jnp.where(mine, count_col, 0), axis=0, keepdims=True)
        start_b = jnp.sum(jnp.where(mine, start_col, 0), axis=0, keepdims=True)
        n_valid = jnp.clip(count_b - (block_row0 - start_b), 0, expert_block)
        n_used = jnp.sum(blocks_col, axis=0, keepdims=True)
        row = lax.broadcasted_iota(I32, (SUBLANES, n_lanes), 0)
        tbl_ref[...] = jnp.where(row == 0, block_e, jnp.where(row == 1, n_valid, n_used))

    @pl.when(phase == 1)
    def _():
        r = lax.broadcasted_iota(I32, (t, t), 0)
        c = lax.broadcasted_iota(I32, (t, t), 1)
        earlier = (r < c).astype(BF16)
        cum = jnp.dot(multi.astype(BF16), earlier, preferred_element_type=F32).astype(I32)
        base = cum + carry_ref[:, 0:1] + pstart_ref[:, 0:1]
        slot = lax.broadcasted_iota(I32, (SUBLANES, t), 0)
        dest = jnp.zeros((SUBLANES, t), I32)
        for k in range(TOP_K):
            d_k = jnp.sum(jnp.where(onehot[k], base, 0), axis=0, keepdims=True)
            dest = jnp.where(slot == k, d_k, dest)
        dest_ref[...] = dest

    carry_ref[...] = carry_ref[...] + tile_cnt


def _route(idx_t, expert_block, n_blocks, tile=2 * ROW_TILE):
    n = idx_t.shape[1]
    tile = min(tile, n)
    per_expert = (N_EXPERTS, LANES)
    table = jax.ShapeDtypeStruct((SUBLANES, pl.cdiv(n_blocks, LANES) * LANES), I32)
    return pl.pallas_call(
        functools.partial(_route_kernel, expert_block=expert_block),
        grid=(2, n // tile),
        in_specs=[pl.BlockSpec((SUBLANES, tile), lambda p, i: (0, i))],
        out_specs=[pl.BlockSpec((SUBLANES, tile), lambda p, i: (0, i * p)),
                   pl.BlockSpec(table.shape, lambda p, i: (0, 0))],
        out_shape=[jax.ShapeDtypeStruct((SUBLANES, n), I32), table],
        scratch_shapes=[pltpu.VMEM(per_expert, I32), pltpu.VMEM(per_expert, I32)],
        compiler_params=_cparams("arbitrary", "arbitrary"),
        name="route",
    )(idx_t)


def _sc_dispatch(dest_t, y, rows, window=64):
    n, d = y.shape
    info = plsc.get_sparse_core_info()
    n_cores = info.num_cores
    per_worker = n // (n_cores * info.num_subcores)
    window = min(window, per_worker // 2)
    n_chunks = per_worker // window
    assert n == n_chunks * window * n_cores * info.num_subcores and n_chunks % 2 == 0
    mesh = plsc.VectorSubcoreMesh(core_axis_name="c", subcore_axis_name="s")
    dest_chunks = dest_t.reshape(dest_t.shape[0], n // window, window)

    @functools.partial(pl.kernel, mesh=mesh, out_type=jax.ShapeDtypeStruct((rows, d), y.dtype),
                       scratch_types=[pltpu.VMEM((TOP_K, n_chunks, window), I32)]
                       + [pltpu.VMEM((window, d), y.dtype)] * 2 + [pltpu.SemaphoreType.DMA] * 4,
                       name="sc_dispatch")
    def body(dest_hbm, y_hbm, xs_hbm, idx_v, buf0, buf1, lsem0, lsem1, ssem0, ssem1):
        bufs, lsem, ssem = (buf0, buf1), (lsem0, lsem1), (ssem0, ssem1)
        worker = lax.axis_index("s") * n_cores + lax.axis_index("c")
        base = worker * per_worker
        for k in range(TOP_K):
            pltpu.sync_copy(dest_hbm.at[k, pl.ds(worker * n_chunks, n_chunks)], idx_v.at[k])

        def load(c, b):
            return pltpu.make_async_copy(y_hbm.at[pl.ds(base + c * window, window)], bufs[b], lsem[b])

        def scatter(c, k, b):
            return pltpu.make_async_copy(bufs[b], xs_hbm.at[idx_v.at[k, c]], ssem[b])

        load(0, 0).start()

        @pl.loop(0, n_chunks, step=2)
        def _(c0):
            for b in range(2):
                c = c0 + b
                load(c, b).wait()

                @pl.when(c > 0)
                def _():
                    for k in range(TOP_K):
                        scatter(c - 1, k, 1 - b).wait()

                @pl.when(c + 1 < n_chunks)
                def _():
                    load(c + 1, 1 - b).start()

                for k in range(TOP_K):
                    scatter(c, k, b).start()

        for k in range(TOP_K):
            scatter(n_chunks - 1, k, 1).wait()

    return body(dest_chunks, y)


def _expert_kernel(be_ref, nu_ref, nv_ref, x_ref, wg_ref, bg_ref, wu_ref, bu_ref, wd_ref, bd_ref, o_ref,
                   wg_bf, wu_bf, wd_bf):
    j = pl.program_id(0)
    used = j < nu_ref[0]
    new_expert = (j == 0) | (be_ref[j] != be_ref[jnp.maximum(j - 1, 0)])

    @pl.when(used & new_expert)
    def _():
        chunk = 128
        for src, dst in ((wg_ref, wg_bf), (wu_ref, wu_bf), (wd_ref, wd_bf)):
            for r in range(0, src.shape[0], chunk):
                dst[r:r + chunk, :] = src[r:r + chunk, :].astype(BF16)

    @pl.when(used)
    def _():
        row = lax.broadcasted_iota(I32, x_ref.shape, 0)
        x_lo, x_hi = _unpack_bf16_pairs(jnp.where(row < nv_ref[j], x_ref[...], jnp.uint32(0)))
        xb = jnp.concatenate([x_lo.astype(BF16), x_hi.astype(BF16)], axis=1)
        gt = jnp.dot(xb, wg_bf[...], preferred_element_type=F32) + bg_ref[...]
        up = jnp.dot(xb, wu_bf[...], preferred_element_type=F32) + bu_ref[...]
        gt = jnp.minimum(gt, SWIGLU_LIMIT)
        up = jnp.clip(up, -SWIGLU_LIMIT, SWIGLU_LIMIT)
        hdn = gt * _sigmoid(SWIGLU_ALPHA * gt) * (up + 1.0)
        o_ref[...] = _pack_bf16_pairs(jnp.dot(hdn.astype(BF16), wd_bf[...], preferred_element_type=F32)
                                      + bd_ref[...])

    @pl.when(jnp.logical_not(used))
    def _():
        o_ref[...] = jnp.zeros_like(o_ref)


def _experts(block_e, n_used, n_valid, xs, wg, bg, wu, bu, wd, bd, expert_block):
    rows, _ = xs.shape
    e, d, f = wg.shape
    xmap = lambda j, be, nu, nv: (jnp.minimum(j, nu[0] - 1), 0)
    wmap = lambda j, be, nu, nv: (be[j], 0, 0)
    grid_spec = pltpu.PrefetchScalarGridSpec(
        num_scalar_prefetch=3,
        grid=(rows // expert_block,),
        in_specs=[pl.BlockSpec((expert_block, d // 2), xmap),
                  pl.BlockSpec((None, d, f), wmap), pl.BlockSpec((None, 1, f), wmap),
                  pl.BlockSpec((None, d, f), wmap), pl.BlockSpec((None, 1, f), wmap),
                  pl.BlockSpec((None, f, d), wmap), pl.BlockSpec((None, 1, d), wmap)],
        out_specs=pl.BlockSpec((expert_block, d // 2), lambda j, be, nu, nv: (j, 0)),
        scratch_shapes=[pltpu.VMEM((d, f), BF16), pltpu.VMEM((d, f), BF16), pltpu.VMEM((f, d), BF16)],
    )
    return pl.pallas_call(
        _expert_kernel,
        grid_spec=grid_spec,
        out_shape=jax.ShapeDtypeStruct((rows, d // 2), jnp.uint32),
        compiler_params=_cparams("arbitrary"),
        name="experts",
    )(block_e, n_used, n_valid, xs, wg, bg.reshape(e, 1, f), wu, bu.reshape(e, 1, f), wd, bd.reshape(e, 1, d))


def _sc_gather(dest_t, ys, n, window=64):
    _, d = ys.shape
    info = plsc.get_sparse_core_info()
    n_cores = info.num_cores
    per_worker = n // (n_cores * info.num_subcores)
    mesh = plsc.VectorSubcoreMesh(core_axis_name="c", subcore_axis_name="s")
    window = min(window, per_worker)
    n_chunks = per_worker // window
    assert n == n_chunks * window * n_cores * info.num_subcores
    n_buf = 2

    @functools.partial(pl.kernel, mesh=mesh, out_type=jax.ShapeDtypeStruct((TOP_K, n, d), ys.dtype),
                       scratch_types=[pltpu.VMEM((TOP_K, per_worker), I32)]
                       + [pltpu.VMEM((window, d), ys.dtype)] * n_buf + [pltpu.SemaphoreType.DMA] * (2 * n_buf),
                       name="sc_gather")
    def body(dest_hbm, ys_hbm, out_hbm, idx_v, buf0, buf1, gsem0, gsem1, ssem0, ssem1):
        bufs, gsem, ssem = (buf0, buf1), (gsem0, gsem1), (ssem0, ssem1)
        base = (lax.axis_index("s") * n_cores + lax.axis_index("c")) * per_worker
        for k in range(TOP_K):
            pltpu.sync_copy(dest_hbm.at[k, pl.ds(base, per_worker)], idx_v.at[k])

        def gather(c, k):
            return pltpu.make_async_copy(ys_hbm.at[idx_v.at[k, pl.ds(c * window, window)]], bufs[k % n_buf],
                                         gsem[k % n_buf])

        def store(c, k):
            return pltpu.make_async_copy(bufs[k % n_buf], out_hbm.at[k, pl.ds(base + c * window, window)],
                                         ssem[k % n_buf])

        gather(0, 0).start()

        @pl.loop(0, n_chunks)
        def _(c):
            for k in range(TOP_K):
                if k > 0:
                    store(c, k - 1).wait()
                    if k + 1 < TOP_K:
                        gather(c, k + 1).start()
                    else:
                        @pl.when(c + 1 < n_chunks)
                        def _():
                            gather(c + 1, 0).start()
                else:
                    @pl.when(c > 0)
                    def _():
                        store(c - 1, TOP_K - 1).wait()
                    gather(c, 1).start()
                gather(c, k).wait()
                store(c, k).start()

        store(n_chunks - 1, TOP_K - 1).wait()

    return body(dest_t, ys)


def _combine_dense_kernel(gate_ref, y_ref, lng_ref, lnb_ref, g_ref, o_ref):
    gates = gate_ref[...].T
    ffn_lo = ffn_hi = None
    for k in range(TOP_K):
        lo, hi = _unpack_bf16_pairs(g_ref[k])
        gate = gates[:, k:k + 1]
        ffn_lo = gate * lo if k == 0 else ffn_lo + gate * lo
        ffn_hi = gate * hi if k == 0 else ffn_hi + gate * hi
    ffn = jnp.concatenate([ffn_lo, ffn_hi], axis=1)
    o_ref[...] = _layer_norm(DEEPNORM_ALPHA * y_ref[...] + ffn, lng_ref[...], lnb_ref[...])


def _combine_dense(gates, y, ln_g, ln_b, gathered, tile=2 * ROW_TILE):
    n, d = y.shape
    tile = min(tile, n)
    row = lambda i: (i, 0)
    const = lambda i: (0, 0)
    return pl.pallas_call(
        _combine_dense_kernel,
        grid=(n // tile,),
        in_specs=[pl.BlockSpec((SUBLANES, tile), lambda i: (0, i)), pl.BlockSpec((tile, d), row),
                  pl.BlockSpec((1, d), const), pl.BlockSpec((1, d), const),
                  pl.BlockSpec((TOP_K, tile, d // 2), lambda i: (0, i, 0))],
        out_specs=pl.BlockSpec((tile, d), row),
        out_shape=jax.ShapeDtypeStruct((n, d), F32),
        compiler_params=_cparams("parallel"),
        name="combine_dense",
    )(gates, y, ln_g, ln_b, gathered)


MOE_ROW_BLOCK = 512


def _moe_ln2(y, y_packed, idx, gates, w_gate, b_gate, w_up, b_up, w_down, b_down, ln2_g, ln2_b):
    n, d = y.shape
    n_blocks = n * TOP_K // MOE_ROW_BLOCK + N_EXPERTS
    rows = n_blocks * MOE_ROW_BLOCK
    dest_t, block_table = _route(idx, MOE_ROW_BLOCK, n_blocks)
    block_e, n_valid, n_used = block_table[0, :n_blocks], block_table[1, :n_blocks], block_table[2, :1]
    xs = _sc_dispatch(dest_t, y_packed, rows)
    ys = _experts(block_e, n_used, n_valid, xs, w_gate, b_gate.astype(F32), w_up, b_up.astype(F32),
                  w_down, b_down.astype(F32), MOE_ROW_BLOCK)
    gathered = _sc_gather(dest_t, ys, n)
    return _combine_dense(gates, y, ln2_g.reshape(1, d).astype(F32), ln2_b.reshape(1, d).astype(F32), gathered)


def kernel(x, w_in, sinks, w_proj_a, w_proj_b, w_out, ln1_g, ln1_b, router_w, router_b,
           w_gate, b_gate, w_up, b_up, w_down, b_down, ln2_g, ln2_b):
    bsz, seq, d = x.shape
    assert seq % (DIL_GROUPS[-1][1] * ATTN_BLOCK) == 0 and seq % IN_PROJ_TILE == 0
    assert w_in.shape[-1] == GATE_OFF + 2 * d and w_proj_b.shape[-2] == SWA_Q_WIDTH
    assert router_w.shape[-1] == N_EXPERTS and (bsz * seq * TOP_K) % MOE_ROW_BLOCK == 0
    h = x.reshape(bsz * seq, d)
    for l in range(w_in.shape[0]):
        y, y_packed, idx, gates = _token_mixer_ln1(h, w_in[l], sinks[l], w_proj_a[l], w_proj_b[l], w_out[l],
                                                   ln1_g[l], ln1_b[l], router_w[l], router_b[l], bsz, seq)
        h = _moe_ln2(y, y_packed, idx, gates, w_gate[l], b_gate[l], w_up[l], b_up[l], w_down[l], b_down[l],
                     ln2_g[l], ln2_b[l])
    return h.reshape(bsz, seq, d)
```

```python
import functools

import jax
import jax.numpy as jnp
from jax import lax
from jax.experimental import pallas as pl
from jax.experimental.pallas import tpu as pltpu
from jax.experimental.pallas import tpu_sc as plsc

F32 = jnp.float32
BF16 = jnp.bfloat16
I32 = jnp.int32

HEAD_DIM = 64
DIL_GROUPS = ((128, 1), (512, 4), (2048, 16))
DIL_HEADS = 4
N_DIL = len(DIL_GROUPS)
DIL_OUT = DIL_HEADS * HEAD_DIM
DIL_WIDTH = N_DIL * DIL_OUT
SWA_Q_HEADS = 16
SWA_KV_HEADS = 2
SWA_REP = SWA_Q_HEADS // SWA_KV_HEADS
SWA_WINDOW = 128
SWA_Q_WIDTH = SWA_Q_HEADS * HEAD_DIM
SWA_KV_WIDTH = SWA_KV_HEADS * HEAD_DIM
N_ALIBI_HEADS = SWA_Q_HEADS + N_DIL * DIL_HEADS
ATTN_BLOCK = 128
N_EXPERTS = 32
TOP_K = 4
SWIGLU_LIMIT = 7.0
SWIGLU_ALPHA = 1.702
LN_EPS = 1e-5
DEPTH = 1
DEEPNORM_ALPHA = (2 * DEPTH) ** 0.25
NEG_INF = -1e30

LANES = 128
SUBLANES = 8
VMEM_LIMIT_BYTES = 56 * 1024 * 1024
ROW_TILE = 512
IN_PROJ_TILE = 2 * ROW_TILE
COL_CHUNK = 512
DIL_ATTN_ROWS = 2 * ROW_TILE

A_QKV_W = 3 * DIL_WIDTH
B_Q_OFF = A_QKV_W
B_KV_OFF = B_Q_OFF + SWA_Q_WIDTH
GATE_OFF = B_KV_OFF + 2 * SWA_KV_WIDTH


def _cparams(*sem):
    return pltpu.CompilerParams(dimension_semantics=sem, vmem_limit_bytes=VMEM_LIMIT_BYTES)


def _pack_bf16_pairs(x):
    c = x.shape[1] // 2
    lo = lax.bitcast_convert_type(x[:, :c].astype(BF16).astype(F32), jnp.uint32)
    hi = lax.bitcast_convert_type(x[:, c:].astype(BF16).astype(F32), jnp.uint32)
    return (lo >> 16) | hi


def _unpack_bf16_pairs(w):
    lo = lax.bitcast_convert_type(w << 16, F32)
    hi = lax.bitcast_convert_type(w & jnp.uint32(0xFFFF0000), F32)
    return lo, hi


def _in_proj_kernel(x_ref, w_ref, a0_ref, a1_ref, a2_ref, hq_ref, hkv_ref, g_ref, *scratch, d_model, tm):
    xb = x_ref[...].astype(BF16)
    segments = ((hq_ref, B_Q_OFF, SWA_Q_WIDTH), (hkv_ref, B_KV_OFF, 2 * SWA_KV_WIDTH),
                (g_ref, GATE_OFF, 2 * d_model))
    for out_ref, col0, width in segments:
        for c in range(0, width, COL_CHUNK):
            w = min(COL_CHUNK, width - c)
            r = jnp.dot(xb, w_ref[:, col0 + c:col0 + c + w], preferred_element_type=F32)
            out_ref[:, c:c + w] = r.astype(out_ref.dtype)
    for g, a_ref in enumerate((a0_ref, a1_ref, a2_ref)):
        dil = DIL_GROUPS[g][1]
        per = tm // dil
        for part in range(3):
            col0 = part * DIL_WIDTH + g * DIL_OUT
            res = jnp.dot(xb, w_ref[:, col0:col0 + DIL_OUT], preferred_element_type=F32)
            if dil == 1:
                a_ref[0, :, part * DIL_OUT:(part + 1) * DIL_OUT] = res.astype(a_ref.dtype)
                continue
            for half in range(DIL_OUT // LANES):
                stage = scratch[part * (DIL_OUT // LANES) + half]
                stage[...] = res[:, half * LANES:(half + 1) * LANES]
                c0 = part * DIL_OUT + half * LANES
                for r in range(dil):
                    a_ref[r, :, c0:c0 + LANES] = stage[pl.ds(r, per, stride=dil), :].astype(a_ref.dtype)


def _in_proj(x2, w_in_bf, bsz, seq, tm=IN_PROJ_TILE):
    n, d = x2.shape
    cols = w_in_bf.shape[1]
    tiles = seq // tm
    row = lambda i: (i, 0)
    dil_spec = lambda dil: pl.BlockSpec((None, dil, tm // dil, 3 * DIL_OUT), lambda i: (i // tiles, 0, i % tiles, 0))
    dil_shape = lambda dil: jax.ShapeDtypeStruct((bsz, dil, seq // dil, 3 * DIL_OUT), BF16)
    dils = [dil for _, dil in DIL_GROUPS]
    return pl.pallas_call(
        functools.partial(_in_proj_kernel, d_model=d, tm=tm),
        grid=(n // tm,),
        in_specs=[pl.BlockSpec((tm, d), row),
                  pl.BlockSpec((d, cols), lambda i: (0, 0), pipeline_mode=pl.Buffered(1))],
        out_specs=[dil_spec(dil) for dil in dils]
        + [pl.BlockSpec((tm, SWA_Q_WIDTH), row), pl.BlockSpec((tm, 2 * SWA_KV_WIDTH), row),
           pl.BlockSpec((tm, 2 * d), row)],
        out_shape=[dil_shape(dil) for dil in dils]
        + [jax.ShapeDtypeStruct((n, SWA_Q_WIDTH), BF16), jax.ShapeDtypeStruct((n, 2 * SWA_KV_WIDTH), BF16),
           jax.ShapeDtypeStruct((n, 2 * d), BF16)],
        scratch_shapes=[pltpu.VMEM((tm, LANES), F32)] * (3 * DIL_OUT // LANES),
        compiler_params=_cparams("parallel"),
        name="in_proj",
    )(x2, w_in_bf)


_NT = (((1,), (1,)), ((), ()))


def _per_head_column(ref, head0, n_heads, scale=1.0):
    head_of_row = lax.broadcasted_iota(I32, (n_heads * ATTN_BLOCK, 1), 0) // ATTN_BLOCK
    col = jnp.zeros((n_heads * ATTN_BLOCK, 1), F32)
    for h in range(n_heads):
        col = jnp.where(head_of_row == h, ref[head0 + h] * scale, col)
    return col


def _store_band_bias(bias_ref, slope_col, max_diff):
    rows = slope_col.shape[0]
    qi = lax.broadcasted_iota(I32, (rows, ATTN_BLOCK), 0) % ATTN_BLOCK
    kj = lax.broadcasted_iota(I32, (rows, ATTN_BLOCK), 1)
    diff_prev = qi - kj + ATTN_BLOCK
    diff_cur = qi - kj
    bias_ref[0] = jnp.where(diff_prev <= max_diff, -slope_col * diff_prev.astype(F32), NEG_INF)
    bias_ref[1] = jnp.where((diff_cur >= 0) & (diff_cur <= max_diff), -slope_col * diff_cur.astype(F32), NEG_INF)


def _band_probs(s_prev, s_cur, sink_col):
    m = jnp.max(jnp.maximum(s_prev, s_cur), axis=-1, keepdims=True)
    if sink_col is not None:
        m = jnp.maximum(m, sink_col)
    return jnp.exp(s_prev - m).astype(BF16), jnp.exp(s_cur - m).astype(BF16), m


def _pair_masks():
    low = lax.broadcasted_iota(I32, (ATTN_BLOCK, LANES), 1) < HEAD_DIM
    scale = HEAD_DIM ** -0.5
    return low, jnp.where(low, scale, 0.0).astype(BF16), jnp.where(low, 0.0, scale).astype(BF16)


def _values_and_ones(v):
    return jnp.concatenate([v, jnp.ones((v.shape[0], LANES), v.dtype)], axis=1)


def _dil_attn_kernel(slope_ref, q_ref, kc_ref, kp_ref, vc_ref, vp_ref, o_ref, lse_ref, bias_ref,
                     *, tq, max_diff, dist_scale):
    first = pl.program_id(2) == 0
    _store_band_bias(bias_ref, _per_head_column(slope_ref, 0, DIL_HEADS, dist_scale), max_diff)
    low, keep_low, keep_high = _pair_masks()
    pair_cols = [slice(p * LANES, (p + 1) * LANES) for p in range(DIL_HEADS // 2)]
    for res, i in [(res, i) for res in range(q_ref.shape[0]) for i in range(tq // ATTN_BLOCK)]:
        rows = slice(i * ATTN_BLOCK, (i + 1) * ATTN_BLOCK)
        window = slice((i - 1) * ATTN_BLOCK, (i + 1) * ATTN_BLOCK)

        def key_window(cur_ref, first_prev_ref, c):
            if i == 0:
                return jnp.concatenate([first_prev_ref[res, :, c], cur_ref[res, rows, c]], axis=0)
            return cur_ref[res, window, c]

        scores = []
        for c in pair_cols:
            k_win = key_window(kc_ref, kp_ref, c)
            for keep in (keep_low, keep_high):
                scores.append(lax.dot_general(q_ref[res, rows, c] * keep, k_win, _NT, preferred_element_type=F32))
        scores = jnp.concatenate(scores, axis=0)
        s_prev = scores[:, :ATTN_BLOCK] + bias_ref[0]
        s_cur = scores[:, ATTN_BLOCK:] + bias_ref[1]
        if i == 0:
            s_prev = jnp.where(first, NEG_INF, s_prev)
        p_prev, p_cur, m = _band_probs(s_prev, s_cur, None)
        probs = jnp.concatenate([p_prev, p_cur], axis=1)
        for p, c in enumerate(pair_cols):
            v_win = _values_and_ones(key_window(vc_ref, vp_ref, c))
            od, mh = [], []
            for h in (2 * p, 2 * p + 1):
                hr = slice(h * ATTN_BLOCK, (h + 1) * ATTN_BLOCK)
                od.append(jnp.dot(probs[hr], v_win, preferred_element_type=F32))
                mh.append(m[hr])
            denom = jnp.where(low, od[0][:, LANES:], od[1][:, LANES:])
            o_ref[res, rows, c] = (jnp.where(low, od[0][:, :LANES], od[1][:, :LANES]) / denom).astype(o_ref.dtype)
            lse_ref[res, rows, c] = jnp.where(low, mh[0], mh[1]) + jnp.log(denom)


def _dil_attention(a_g, slopes_g, g):
    window, dil = DIL_GROUPS[g]
    bsz, _, sub_len, _ = a_g.shape
    tq = min(DIL_ATTN_ROWS, sub_len)
    res_per_step = DIL_ATTN_ROWS // tq
    nqb = tq // ATTN_BLOCK
    cur = lambda part: (lambda b, r, m: (b, r, m, part))
    prev = lambda part: (lambda b, r, m: (b, r, jnp.maximum(m * nqb - 1, 0), part))
    blk = lambda rows, imap: pl.BlockSpec((None, res_per_step, rows, DIL_OUT), imap)
    return pl.pallas_call(
        functools.partial(_dil_attn_kernel, tq=tq, max_diff=window // dil, dist_scale=float(dil)),
        grid=(bsz, dil // res_per_step, sub_len // tq),
        in_specs=[pl.BlockSpec(memory_space=pltpu.SMEM),
                  blk(tq, cur(0)), blk(tq, cur(1)), blk(ATTN_BLOCK, prev(1)),
                  blk(tq, cur(2)), blk(ATTN_BLOCK, prev(2))],
        out_specs=[blk(tq, cur(0)), blk(tq, cur(0))],
        out_shape=[jax.ShapeDtypeStruct((bsz, dil, sub_len, DIL_OUT), BF16),
                   jax.ShapeDtypeStruct((bsz, dil, sub_len, DIL_OUT), F32)],
        scratch_shapes=[pltpu.VMEM((2, DIL_HEADS * ATTN_BLOCK, ATTN_BLOCK), F32)],
        compiler_params=_cparams("parallel", "parallel", "arbitrary"),
        name=f"dil_attn_g{g}",
    )(slopes_g, a_g, a_g, a_g, a_g, a_g)


def _swa_attn_kernel(slope_ref, sink_ref, q_ref, kvc_ref, kvp_ref, o_ref, bias_ref, *, tq):
    first = pl.program_id(1) == 0
    low, keep_low, keep_high = _pair_masks()

    def both_halves(ref, lane0):
        part = ref[:, lane0:lane0 + HEAD_DIM]
        return jnp.concatenate([part, part], axis=1)

    tile = (SWA_REP, ATTN_BLOCK, ATTN_BLOCK)
    stacked = (SWA_REP * ATTN_BLOCK, ATTN_BLOCK)
    qi = lax.broadcasted_iota(I32, (ATTN_BLOCK, ATTN_BLOCK), 0)
    kj = lax.broadcasted_iota(I32, (ATTN_BLOCK, ATTN_BLOCK), 1)
    from_prev = kj > qi
    diff = jnp.where(from_prev, qi - kj + ATTN_BLOCK, qi - kj).astype(F32)
    for kvh in range(SWA_KV_HEADS):
        head0 = kvh * SWA_REP
        for r in range(SWA_REP):
            bias_ref[r] = -slope_ref[head0 + r] * diff
        sink_col = _per_head_column(sink_ref, head0, SWA_REP)
        k_cur_all, k_first = both_halves(kvc_ref, kvh * HEAD_DIM), both_halves(kvp_ref, kvh * HEAD_DIM)
        v_lane0 = SWA_KV_WIDTH + kvh * HEAD_DIM
        v_cur_all, v_first = both_halves(kvc_ref, v_lane0), both_halves(kvp_ref, v_lane0)
        pair_cols = [slice((head0 + 2 * p) * HEAD_DIM, (head0 + 2 * p + 2) * HEAD_DIM) for p in range(SWA_REP // 2)]
        for i in range(tq // ATTN_BLOCK):
            rows = slice(i * ATTN_BLOCK, (i + 1) * ATTN_BLOCK)
            if i == 0:
                k_win = jnp.concatenate([k_first, k_cur_all[rows]], axis=0)
                v_win = jnp.concatenate([v_first, v_cur_all[rows]], axis=0)
            else:
                win = slice((i - 1) * ATTN_BLOCK, (i + 1) * ATTN_BLOCK)
                k_win, v_win = k_cur_all[win], v_cur_all[win]
            q = jnp.concatenate([q_ref[rows, c] * keep for c in pair_cols for keep in (keep_low, keep_high)], axis=0)
            s2 = lax.dot_general(q, k_win, _NT, preferred_element_type=F32)
            s = jnp.where(from_prev, s2[:, :ATTN_BLOCK].reshape(tile), s2[:, ATTN_BLOCK:].reshape(tile)) \
                + bias_ref[...]
            if i == 0:
                s = jnp.where(from_prev & first, NEG_INF, s)
            s = s.reshape(stacked)
            m = jnp.maximum(jnp.max(s, axis=-1, keepdims=True), sink_col)
            p = jnp.exp(s - m).reshape(tile)
            p2 = jnp.concatenate([jnp.where(from_prev, p, 0.0).reshape(stacked).astype(BF16),
                                  jnp.where(from_prev, 0.0, p).reshape(stacked).astype(BF16)], axis=1)
            od = jnp.dot(p2, _values_and_ones(v_win), preferred_element_type=F32)
            o = od[:, :LANES] / (od[:, LANES:] + jnp.exp(sink_col - m))
            for p, c in enumerate(pair_cols):
                even = o[2 * p * ATTN_BLOCK:(2 * p + 1) * ATTN_BLOCK]
                odd = o[(2 * p + 1) * ATTN_BLOCK:(2 * p + 2) * ATTN_BLOCK]
                o_ref[rows, c] = jnp.where(low, even, odd).astype(o_ref.dtype)


def _swa_attention(hq, hkv, slopes_b, sinks, bsz, seq, tq=ROW_TILE):
    nqb = tq // ATTN_BLOCK
    hq3 = hq.reshape(bsz, seq, SWA_Q_WIDTH)
    hkv3 = hkv.reshape(bsz, seq, 2 * SWA_KV_WIDTH)
    smem = pl.BlockSpec(memory_space=pltpu.SMEM)
    out = pl.pallas_call(
        functools.partial(_swa_attn_kernel, tq=tq),
        grid=(bsz, seq // tq),
        in_specs=[smem, smem,
                  pl.BlockSpec((None, tq, SWA_Q_WIDTH), lambda b, m: (b, m, 0)),
                  pl.BlockSpec((None, tq, 2 * SWA_KV_WIDTH), lambda b, m: (b, m, 0)),
                  pl.BlockSpec((None, ATTN_BLOCK, 2 * SWA_KV_WIDTH),
                               lambda b, m: (b, jnp.maximum(m * nqb - 1, 0), 0))],
        out_specs=pl.BlockSpec((None, tq, SWA_Q_WIDTH), lambda b, m: (b, m, 0)),
        out_shape=jax.ShapeDtypeStruct((bsz, seq, SWA_Q_WIDTH), BF16),
        scratch_shapes=[pltpu.VMEM((SWA_REP, ATTN_BLOCK, ATTN_BLOCK), F32)],
        compiler_params=_cparams("parallel", "arbitrary"),
        name="swa_attn",
    )(slopes_b, sinks, hq3, hkv3, hkv3)
    return out.reshape(bsz * seq, SWA_Q_WIDTH)


def _sigmoid(x):
    return 0.5 * (jnp.tanh(0.5 * x) + 1.0)


def _layer_norm(z, g, b):
    mu = jnp.mean(z, axis=-1, keepdims=True)
    zc = z - mu
    var = jnp.mean(zc * zc, axis=-1, keepdims=True)
    return zc * lax.rsqrt(var + LN_EPS) * g + b


def _mix_out_kernel(x_ref, o0_ref, o1_ref, o2_ref, l0_ref, l1_ref, l2_ref, ob_ref, g_ref,
                    wpa_ref, wpb_ref, wo_ref, lng_ref, lnb_ref, rw_ref, rb_ref,
                    y_ref, ypk_ref, idx_ref, gate_ref, *scratch, d_model):
    def natural(ref, stages):
        dil, per, _ = ref.shape
        if dil == 1:
            return ref[0].astype(F32)
        for half, stage in enumerate(stages):
            for r in range(dil):
                stage[pl.ds(r, per, stride=dil), :] = ref[r, :, half * LANES:(half + 1) * LANES].astype(F32)
        return jnp.concatenate([stage[...] for stage in stages], axis=1)

    o_nat = natural(o0_ref, None), natural(o1_ref, scratch[0:2]), natural(o2_ref, scratch[2:4])
    l_nat = natural(l0_ref, None), natural(l1_ref, scratch[4:6]), natural(l2_ref, scratch[6:8])

    def chain(rows):
        (o0, o1, o2), (l0, l1, l2) = [a[rows] for a in o_nat], [a[rows] for a in l_nat]
        lm = jnp.maximum(jnp.maximum(l0, l1), l2)
        e0, e1, e2 = jnp.exp(l0 - lm), jnp.exp(l1 - lm), jnp.exp(l2 - lm)
        out_a = (e0 * o0 + e1 * o1 + e2 * o2) / (e0 + e1 + e2)
        pa = jnp.dot(out_a.astype(BF16), wpa_ref[...], preferred_element_type=F32)
        pb = jnp.dot(ob_ref[rows, :], wpb_ref[...], preferred_element_type=F32)
        ga = _sigmoid(g_ref[rows, :d_model].astype(F32))
        gb = _sigmoid(g_ref[rows, d_model:].astype(F32))
        merged = ga * pa + gb * pb
        mix = jnp.dot(merged.astype(BF16), wo_ref[...], preferred_element_type=F32)
        y = _layer_norm(DEEPNORM_ALPHA * x_ref[rows, :] + mix, lng_ref[...], lnb_ref[...])
        y_ref[rows, :] = y
        ypk_ref[rows, :] = _pack_bf16_pairs(y)

        logits = lax.dot_general(rw_ref[...], y.astype(BF16), _NT, preferred_element_type=F32) + rb_ref[...]
        expert = lax.broadcasted_iota(I32, logits.shape, 0)
        slot = lax.broadcasted_iota(I32, (SUBLANES, logits.shape[1]), 0)
        idx_out = jnp.zeros(slot.shape, I32)
        val_out = jnp.full(slot.shape, -jnp.inf, F32)
        for k in range(TOP_K):
            top = jnp.max(logits, axis=0, keepdims=True)
            top_idx = jnp.min(jnp.where(logits == top, expert, N_EXPERTS), axis=0, keepdims=True)
            idx_out = jnp.where(slot == k, top_idx, idx_out)
            val_out = jnp.where(slot == k, top, val_out)
            logits = jnp.where(expert == top_idx, -jnp.inf, logits)
        ev = jnp.exp(val_out - jnp.max(val_out, axis=0, keepdims=True))
        idx_ref[:, rows] = idx_out
        gate_ref[:, rows] = ev / jnp.sum(ev, axis=0, keepdims=True)

    chain(slice(0, x_ref.shape[0]))


def _mix_out(x2, o_g, lse_g, out_b, gates_h, wpa, wpb, wo, ln_g, ln_b, rw_t, rb_col, seq, tm=ROW_TILE):
    n, d = x2.shape
    tiles = seq // tm
    row = lambda i: (i, 0)
    const = lambda i: (0, 0)
    rb = lambda w: pl.BlockSpec((tm, w), row)
    full = lambda a: pl.BlockSpec(a.shape, const)
    dil_specs = [pl.BlockSpec((None, dil, tm // dil, DIL_OUT), lambda i: (i // tiles, 0, i % tiles, 0))
                 for _, dil in DIL_GROUPS]
    return pl.pallas_call(
        functools.partial(_mix_out_kernel, d_model=d),
        grid=(n // tm,),
        in_specs=[rb(d)] + dil_specs + dil_specs + [rb(SWA_Q_WIDTH), rb(2 * d),
                  full(wpa), full(wpb), full(wo), full(ln_g), full(ln_b), full(rw_t), full(rb_col)],
        out_specs=[rb(d), rb(d // 2), pl.BlockSpec((SUBLANES, tm), lambda i: (0, i)),
                   pl.BlockSpec((SUBLANES, tm), lambda i: (0, i))],
        out_shape=[jax.ShapeDtypeStruct((n, d), F32), jax.ShapeDtypeStruct((n, d // 2), jnp.uint32),
                   jax.ShapeDtypeStruct((SUBLANES, n), I32), jax.ShapeDtypeStruct((SUBLANES, n), F32)],
        scratch_shapes=[pltpu.VMEM((tm, LANES), F32)] * (4 * DIL_OUT // LANES),
        compiler_params=_cparams("parallel"),
        name="mix_out",
    )(x2, *o_g, *lse_g, out_b, gates_h, wpa, wpb, wo, ln_g, ln_b, rw_t, rb_col)


def _token_mixer_ln1(x2, w_in, sinks, w_proj_a, w_proj_b, w_out, ln1_g, ln1_b, router_w, router_b, bsz, seq):
    n, d = x2.shape
    heads = jnp.arange(1, N_ALIBI_HEADS + 1, dtype=F32)
    slopes = jnp.exp2(-8.0 * heads / N_ALIBI_HEADS)
    *a_g, hq, hkv, gates_h = _in_proj(x2, w_in.astype(BF16), bsz, seq)
    o_g, lse_g = [], []
    for g in range(N_DIL):
        sl = slopes[SWA_Q_HEADS + g * DIL_HEADS:SWA_Q_HEADS + (g + 1) * DIL_HEADS]
        o, lse = _dil_attention(a_g[g], sl, g)
        o_g.append(o)
        lse_g.append(lse)
    out_b = _swa_attention(hq, hkv, slopes[:SWA_Q_HEADS], sinks.astype(F32), bsz, seq)
    return _mix_out(x2, o_g, lse_g, out_b, gates_h, w_proj_a.astype(BF16), w_proj_b.astype(BF16),
                    w_out.astype(BF16), ln1_g.reshape(1, d).astype(F32), ln1_b.reshape(1, d).astype(F32),
                    router_w.T.astype(BF16), router_b.reshape(N_EXPERTS, 1).astype(F32), seq)


def _route_kernel(idx_ref, dest_ref, tbl_ref, carry_ref, pstart_ref, *, expert_block):
    phase, i = pl.program_id(0), pl.program_id(1)
    idx = idx_ref[...]
    t = idx.shape[1]
    expert = lax.broadcasted_iota(I32, (N_EXPERTS, t), 0)
    onehot = [expert == idx[k:k + 1, :] for k in range(TOP_K)]
    multi = sum(oh.astype(F32) for oh in onehot)
    tile_cnt = jnp.sum(multi, axis=1, keepdims=True).astype(I32)

    @pl.when((phase == 0) & (i == 0))
    def _():
        carry_ref[...] = jnp.zeros_like(carry_ref)

    @pl.when((phase == 1) & (i == 0))
    def _():
        counts = carry_ref[...]
        blocks = (counts + (expert_block - 1)) // expert_block
        er = lax.broadcasted_iota(I32, (N_EXPERTS, N_EXPERTS), 0)
        ec = lax.broadcasted_iota(I32, (N_EXPERTS, N_EXPERTS), 1)
        below = (ec < er).astype(BF16)
        prefix = lambda v: jnp.dot(below, v.astype(F32).astype(BF16), preferred_element_type=F32).astype(I32)
        pstart = (prefix(blocks >> 6) * 64 + prefix(blocks & 63)) * expert_block
        pstart_ref[...] = pstart
        carry_ref[...] = jnp.zeros_like(carry_ref)
        n_lanes = tbl_ref.shape[1]
        count_col, start_col, blocks_col = counts[:, 0:1], pstart[:, 0:1], blocks[:, 0:1]
        block_row0 = lax.broadcasted_iota(I32, (1, n_lanes), 1) * expert_block
        ends_before = (start_col + blocks_col * expert_block <= block_row0).astype(I32)
        block_e = jnp.minimum(jnp.sum(ends_before, axis=0, keepdims=True), N_EXPERTS - 1)
        mine = lax.broadcasted_iota(I32, (N_EXPERTS, n_lanes), 0) == block_e
        count_b = jnp.sum(jnp.where(mine, count_col, 0), axis=0, keepdims=True)
        start_b = jnp.sum(jnp.where(mine, start_col, 0), axis=0, keepdims=True)
        n_valid = jnp.clip(count_b - (block_row0 - start_b), 0, expert_block)
        n_used = jnp.sum(blocks_col, axis=0, keepdims=True)
        expert_ids = lax.broadcasted_iota(I32, (N_EXPERTS, n_lanes), 0)
        later_used = (expert_ids > block_e) & (blocks_col > 0)
        next_e = jnp.min(jnp.where(later_used, expert_ids, N_EXPERTS), axis=0, keepdims=True)
        next_e = jnp.where(next_e == N_EXPERTS, -1, next_e)
        row = lax.broadcasted_iota(I32, (SUBLANES, n_lanes), 0)
        tbl_ref[...] = jnp.where(row == 0, block_e,
                                 jnp.where(row == 1, n_valid, jnp.where(row == 2, n_used, next_e)))

    @pl.when(phase == 1)
    def _():
        r = lax.broadcasted_iota(I32, (t, t), 0)
        c = lax.broadcasted_iota(I32, (t, t), 1)
        earlier = (r < c).astype(BF16)
        cum = jnp.dot(multi.astype(BF16), earlier, preferred_element_type=F32).astype(I32)
        base = cum + carry_ref[:, 0:1] + pstart_ref[:, 0:1]
        slot = lax.broadcasted_iota(I32, (SUBLANES, t), 0)
        dest = jnp.zeros((SUBLANES, t), I32)
        for k in range(TOP_K):
            d_k = jnp.sum(jnp.where(onehot[k], base, 0), axis=0, keepdims=True)
            dest = jnp.where(slot == k, d_k, dest)
        dest_ref[...] = dest

    carry_ref[...] = carry_ref[...] + tile_cnt


def _route(idx_t, expert_block, n_blocks, tile=2 * ROW_TILE):
    n = idx_t.shape[1]
    tile = min(tile, n)
    per_expert = (N_EXPERTS, LANES)
    table = jax.ShapeDtypeStruct((SUBLANES, pl.cdiv(n_blocks, LANES) * LANES), I32)
    return pl.pallas_call(
        functools.partial(_route_kernel, expert_block=expert_block),
        grid=(2, n // tile),
        in_specs=[pl.BlockSpec((SUBLANES, tile), lambda p, i: (0, i))],
        out_specs=[pl.BlockSpec((SUBLANES, tile), lambda p, i: (0, i * p)),
                   pl.BlockSpec(table.shape, lambda p, i: (0, 0))],
        out_shape=[jax.ShapeDtypeStruct((SUBLANES, n), I32), table],
        scratch_shapes=[pltpu.VMEM(per_expert, I32), pltpu.VMEM(per_expert, I32)],
        compiler_params=_cparams("arbitrary", "arbitrary"),
        name="route",
    )(idx_t)


def _sc_dispatch(dest_t, y, rows, window=64):
    n, d = y.shape
    info = plsc.get_sparse_core_info()
    n_cores = info.num_cores
    per_worker = n // (n_cores * info.num_subcores)
    window = min(window, per_worker // 2)
    n_chunks = per_worker // window
    assert n == n_chunks * window * n_cores * info.num_subcores and n_chunks % 2 == 0
    mesh = plsc.VectorSubcoreMesh(core_axis_name="c", subcore_axis_name="s")
    dest_chunks = dest_t.reshape(dest_t.shape[0], n // window, window)

    @functools.partial(pl.kernel, mesh=mesh, out_type=jax.ShapeDtypeStruct((rows, d), y.dtype),
                       scratch_types=[pltpu.VMEM((TOP_K, n_chunks, window), I32)]
                       + [pltpu.VMEM((window, d), y.dtype)] * 2 + [pltpu.SemaphoreType.DMA] * 4,
                       name="sc_dispatch")
    def body(dest_hbm, y_hbm, xs_hbm, idx_v, buf0, buf1, lsem0, lsem1, ssem0, ssem1):
        bufs, lsem, ssem = (buf0, buf1), (lsem0, lsem1), (ssem0, ssem1)
        worker = lax.axis_index("s") * n_cores + lax.axis_index("c")
        base = worker * per_worker
        for k in range(TOP_K):
            pltpu.sync_copy(dest_hbm.at[k, pl.ds(worker * n_chunks, n_chunks)], idx_v.at[k])

        def load(c, b):
            return pltpu.make_async_copy(y_hbm.at[pl.ds(base + c * window, window)], bufs[b], lsem[b])

        def scatter(c, k, b):
            return pltpu.make_async_copy(bufs[b], xs_hbm.at[idx_v.at[k, c]], ssem[b])

        load(0, 0).start()

        @pl.loop(0, n_chunks, step=2)
        def _(c0):
            for b in range(2):
                c = c0 + b
                load(c, b).wait()

                @pl.when(c > 0)
                def _():
                    for k in range(TOP_K):
                        scatter(c - 1, k, 1 - b).wait()

                @pl.when(c + 1 < n_chunks)
                def _():
                    load(c + 1, 1 - b).start()

                for k in range(TOP_K):
                    scatter(c, k, b).start()

        for k in range(TOP_K):
            scatter(n_chunks - 1, k, 1).wait()

    return body(dest_chunks, y)


def _expert_kernel(be_ref, nu_ref, nv_ref, nx_ref, x_ref, wg_hbm, bg_ref, wu_hbm, bu_ref, wd_hbm, bd_ref, o_ref,
                   wg_bf, wu_bf, wd_bf, wg_st, wu_st, wd_st, w_sem):
    j = pl.program_id(0)
    used = j < nu_ref[0]
    new_expert = (j == 0) | (be_ref[j] != be_ref[jnp.maximum(j - 1, 0)])
    staged = ((wg_hbm, wg_st, wg_bf), (wu_hbm, wu_st, wu_bf), (wd_hbm, wd_st, wd_bf))

    def fetch(e):
        return [pltpu.make_async_copy(hbm.at[e], st, w_sem.at[i]) for i, (hbm, st, _) in enumerate(staged)]

    @pl.when(used & new_expert)
    def _():
        @pl.when(j == 0)
        def _():
            for copy in fetch(be_ref[0]):
                copy.start()

        for copy in fetch(be_ref[j]):
            copy.wait()
        chunk = 128
        for _, st, dst in staged:
            for r in range(0, st.shape[0], chunk):
                dst[r:r + chunk, :] = st[r:r + chunk, :].astype(BF16)

        @pl.when(nx_ref[j] >= 0)
        def _():
            for copy in fetch(nx_ref[j]):
                copy.start()

    @pl.when(used)
    def _():
        row = lax.broadcasted_iota(I32, x_ref.shape, 0)
        x_lo, x_hi = _unpack_bf16_pairs(jnp.where(row < nv_ref[j], x_ref[...], jnp.uint32(0)))
        xb = jnp.concatenate([x_lo.astype(BF16), x_hi.astype(BF16)], axis=1)
        gt = jnp.dot(xb, wg_bf[...], preferred_element_type=F32) + bg_ref[...]
        up = jnp.dot(xb, wu_bf[...], preferred_element_type=F32) + bu_ref[...]
        gt = jnp.minimum(gt, SWIGLU_LIMIT)
        up = jnp.clip(up, -SWIGLU_LIMIT, SWIGLU_LIMIT)
        hdn = gt * _sigmoid(SWIGLU_ALPHA * gt) * (up + 1.0)
        o_ref[...] = _pack_bf16_pairs(jnp.dot(hdn.astype(BF16), wd_bf[...], preferred_element_type=F32)
                                      + bd_ref[...])

    @pl.when(jnp.logical_not(used))
    def _():
        o_ref[...] = jnp.zeros_like(o_ref)


def _experts(block_e, n_used, n_valid, next_e, xs, wg, bg, wu, bu, wd, bd, expert_block):
    rows, _ = xs.shape
    e, d, f = wg.shape
    xmap = lambda j, be, nu, nv, nx: (jnp.minimum(j, nu[0] - 1), 0)
    bmap = lambda j, be, nu, nv, nx: (be[j], 0, 0)
    weights = pl.BlockSpec(memory_space=pl.ANY)
    grid_spec = pltpu.PrefetchScalarGridSpec(
        num_scalar_prefetch=4,
        grid=(rows // expert_block,),
        in_specs=[pl.BlockSpec((expert_block, d // 2), xmap),
                  weights, pl.BlockSpec((None, 1, f), bmap),
                  weights, pl.BlockSpec((None, 1, f), bmap),
                  weights, pl.BlockSpec((None, 1, d), bmap)],
        out_specs=pl.BlockSpec((expert_block, d // 2), lambda j, be, nu, nv, nx: (j, 0)),
        scratch_shapes=[pltpu.VMEM((d, f), BF16), pltpu.VMEM((d, f), BF16), pltpu.VMEM((f, d), BF16),
                        pltpu.VMEM((d, f), wg.dtype), pltpu.VMEM((d, f), wu.dtype), pltpu.VMEM((f, d), wd.dtype),
                        pltpu.SemaphoreType.DMA((3,))],
    )
    return pl.pallas_call(
        _expert_kernel,
        grid_spec=grid_spec,
        out_shape=jax.ShapeDtypeStruct((rows, d // 2), jnp.uint32),
        compiler_params=_cparams("arbitrary"),
        name="experts",
    )(block_e, n_used, n_valid, next_e, xs, wg, bg.reshape(e, 1, f), wu, bu.reshape(e, 1, f), wd,
      bd.reshape(e, 1, d))


def _sc_gather(dest_t, ys, n, window=64):
    _, d = ys.shape
    info = plsc.get_sparse_core_info()
    n_cores = info.num_cores
    per_worker = n // (n_cores * info.num_subcores)
    mesh = plsc.VectorSubcoreMesh(core_axis_name="c", subcore_axis_name="s")
    window = min(window, per_worker)
    n_chunks = per_worker // window
    assert n == n_chunks * window * n_cores * info.num_subcores
    n_buf = 2

    @functools.partial(pl.kernel, mesh=mesh, out_type=jax.ShapeDtypeStruct((TOP_K, n, d), ys.dtype),
                       scratch_types=[pltpu.VMEM((TOP_K, per_worker), I32)]
                       + [pltpu.VMEM((window, d), ys.dtype)] * n_buf + [pltpu.SemaphoreType.DMA] * (2 * n_buf),
                       name="sc_gather")
    def body(dest_hbm, ys_hbm, out_hbm, idx_v, buf0, buf1, gsem0, gsem1, ssem0, ssem1):
        bufs, gsem, ssem = (buf0, buf1), (gsem0, gsem1), (ssem0, ssem1)
        base = (lax.axis_index("s") * n_cores + lax.axis_index("c")) * per_worker
        for k in range(TOP_K):
            pltpu.sync_copy(dest_hbm.at[k, pl.ds(base, per_worker)], idx_v.at[k])

        def gather(c, k):
            return pltpu.make_async_copy(ys_hbm.at[idx_v.at[k, pl.ds(c * window, window)]], bufs[k % n_buf],
                                         gsem[k % n_buf])

        def store(c, k):
            return pltpu.make_async_copy(bufs[k % n_buf], out_hbm.at[k, pl.ds(base + c * window, window)],
                                         ssem[k % n_buf])

        gather(0, 0).start()

        @pl.loop(0, n_chunks)
        def _(c):
            for k in range(TOP_K):
                if k > 0:
                    store(c, k - 1).wait()
                    if k + 1 < TOP_K:
                        gather(c, k + 1).start()
                    else:
                        @pl.when(c + 1 < n_chunks)
                        def _():
                            gather(c + 1, 0).start()
                else:
                    @pl.when(c > 0)
                    def _():
                        store(c - 1, TOP_K - 1).wait()
                    gather(c, 1).start()
                gather(c, k).wait()
                store(c, k).start()

        store(n_chunks - 1, TOP_K - 1).wait()

    return body(dest_t, ys)


def _combine_dense_kernel(gate_ref, y_ref, lng_ref, lnb_ref, g_ref, o_ref):
    gates = gate_ref[...].T
    ffn_lo = ffn_hi = None
    for k in range(TOP_K):
        lo, hi = _unpack_bf16_pairs(g_ref[k])
        gate = gates[:, k:k + 1]
        ffn_lo = gate * lo if k == 0 else ffn_lo + gate * lo
        ffn_hi = gate * hi if k == 0 else ffn_hi + gate * hi
    ffn = jnp.concatenate([ffn_lo, ffn_hi], axis=1)
    o_ref[...] = _layer_norm(DEEPNORM_ALPHA * y_ref[...] + ffn, lng_ref[...], lnb_ref[...])


def _combine_dense(gates, y, ln_g, ln_b, gathered, tile=2 * ROW_TILE):
    n, d = y.shape
    tile = min(tile, n)
    row = lambda i: (i, 0)
    const = lambda i: (0, 0)
    return pl.pallas_call(
        _combine_dense_kernel,
        grid=(n // tile,),
        in_specs=[pl.BlockSpec((SUBLANES, tile), lambda i: (0, i)), pl.BlockSpec((tile, d), row),
                  pl.BlockSpec((1, d), const), pl.BlockSpec((1, d), const),
                  pl.BlockSpec((TOP_K, tile, d // 2), lambda i: (0, i, 0))],
        out_specs=pl.BlockSpec((tile, d), row),
        out_shape=jax.ShapeDtypeStruct((n, d), F32),
        compiler_params=_cparams("parallel"),
        name="combine_dense",
    )(gates, y, ln_g, ln_b, gathered)


MOE_ROW_BLOCK = 512


def _moe_ln2(y, y_packed, idx, gates, w_gate, b_gate, w_up, b_up, w_down, b_down, ln2_g, ln2_b):
    n, d = y.shape
    n_blocks = n * TOP_K // MOE_ROW_BLOCK + N_EXPERTS
    rows = n_blocks * MOE_ROW_BLOCK
    dest_t, block_table = _route(idx, MOE_ROW_BLOCK, n_blocks)
    block_e, n_valid, n_used = block_table[0, :n_blocks], block_table[1, :n_blocks], block_table[2, :1]
    next_e = block_table[3, :n_blocks]
    xs = _sc_dispatch(dest_t, y_packed, rows)
    ys = _experts(block_e, n_used, n_valid, next_e, xs, w_gate, b_gate.astype(F32), w_up, b_up.astype(F32),
                  w_down, b_down.astype(F32), MOE_ROW_BLOCK)
    gathered = _sc_gather(dest_t, ys, n)
    return _combine_dense(gates, y, ln2_g.reshape(1, d).astype(F32), ln2_b.reshape(1, d).astype(F32), gathered)


def kernel(x, w_in, sinks, w_proj_a, w_proj_b, w_out, ln1_g, ln1_b, router_w, router_b,
           w_gate, b_gate, w_up, b_up, w_down, b_down, ln2_g, ln2_b):
    bsz, seq, d = x.shape
    assert seq % (DIL_GROUPS[-1][1] * ATTN_BLOCK) == 0 and seq % IN_PROJ_TILE == 0
    assert w_in.shape[-1] == GATE_OFF + 2 * d and w_proj_b.shape[-2] == SWA_Q_WIDTH
    assert router_w.shape[-1] == N_EXPERTS and (bsz * seq * TOP_K) % MOE_ROW_BLOCK == 0
    h = x.reshape(bsz * seq, d)
    for l in range(w_in.shape[0]):
        y, y_packed, idx, gates = _token_mixer_ln1(h, w_in[l], sinks[l], w_proj_a[l], w_proj_b[l], w_out[l],
                                                   ln1_g[l], ln1_b[l], router_w[l], router_b[l], bsz, seq)
        h = _moe_ln2(y, y_packed, idx, gates, w_gate[l], b_gate[l], w_up[l], b_up[l], w_down[l], b_down[l],
                     ln2_g[l], ln2_b[l])
    return h.reshape(bsz, seq, d)
```

```python
import functools

import jax
import jax.numpy as jnp
from jax import lax
from jax.experimental import pallas as pl
from jax.experimental.pallas import tpu as pltpu
from jax.experimental.pallas import tpu_sc as plsc

F32 = jnp.float32
BF16 = jnp.bfloat16
I32 = jnp.int32

HEAD_DIM = 64
DIL_GROUPS = ((128, 1), (512, 4), (2048, 16))
DIL_HEADS = 4
N_DIL = len(DIL_GROUPS)
DIL_OUT = DIL_HEADS * HEAD_DIM
DIL_WIDTH = N_DIL * DIL_OUT
SWA_Q_HEADS = 16
SWA_KV_HEADS = 2
SWA_REP = SWA_Q_HEADS // SWA_KV_HEADS
SWA_WINDOW = 128
SWA_Q_WIDTH = SWA_Q_HEADS * HEAD_DIM
SWA_KV_WIDTH = SWA_KV_HEADS * HEAD_DIM
N_ALIBI_HEADS = SWA_Q_HEADS + N_DIL * DIL_HEADS
ATTN_BLOCK = 128
N_EXPERTS = 32
TOP_K = 4
SWIGLU_LIMIT = 7.0
SWIGLU_ALPHA = 1.702
LN_EPS = 1e-5
DEPTH = 1
DEEPNORM_ALPHA = (2 * DEPTH) ** 0.25
NEG_INF = -1e30

LANES = 128
SUBLANES = 8
VMEM_LIMIT_BYTES = 56 * 1024 * 1024
ROW_TILE = 512
IN_PROJ_TILE = 2 * ROW_TILE
COL_CHUNK = 512
DIL_ATTN_ROWS = 4 * ROW_TILE

A_QKV_W = 3 * DIL_WIDTH
B_Q_OFF = A_QKV_W
B_KV_OFF = B_Q_OFF + SWA_Q_WIDTH
GATE_OFF = B_KV_OFF + 2 * SWA_KV_WIDTH


def _cparams(*sem):
    return pltpu.CompilerParams(dimension_semantics=sem, vmem_limit_bytes=VMEM_LIMIT_BYTES)


def _pack_bf16_pairs(x):
    c = x.shape[1] // 2
    lo = lax.bitcast_convert_type(x[:, :c].astype(BF16).astype(F32), jnp.uint32)
    hi = lax.bitcast_convert_type(x[:, c:].astype(BF16).astype(F32), jnp.uint32)
    return (lo >> 16) | hi


def _unpack_bf16_pairs(w):
    lo = lax.bitcast_convert_type(w << 16, F32)
    hi = lax.bitcast_convert_type(w & jnp.uint32(0xFFFF0000), F32)
    return lo, hi


def _in_proj_kernel(x_ref, w_ref, a0_ref, a1_ref, a2_ref, hq_ref, hkv_ref, g_ref, *scratch, d_model, tm):
    xb = x_ref[...].astype(BF16)
    segments = ((hq_ref, B_Q_OFF, SWA_Q_WIDTH), (hkv_ref, B_KV_OFF, 2 * SWA_KV_WIDTH),
                (g_ref, GATE_OFF, 2 * d_model))
    for out_ref, col0, width in segments:
        for c in range(0, width, COL_CHUNK):
            w = min(COL_CHUNK, width - c)
            r = jnp.dot(xb, w_ref[:, col0 + c:col0 + c + w], preferred_element_type=F32)
            out_ref[:, c:c + w] = r.astype(out_ref.dtype)
    for g, a_ref in enumerate((a0_ref, a1_ref, a2_ref)):
        dil = DIL_GROUPS[g][1]
        per = tm // dil
        for part in range(3):
            col0 = part * DIL_WIDTH + g * DIL_OUT
            res = jnp.dot(xb, w_ref[:, col0:col0 + DIL_OUT], preferred_element_type=F32)
            if dil == 1:
                a_ref[0, :, part * DIL_OUT:(part + 1) * DIL_OUT] = res.astype(a_ref.dtype)
                continue
            for half in range(DIL_OUT // LANES):
                stage = scratch[part * (DIL_OUT // LANES) + half]
                stage[...] = res[:, half * LANES:(half + 1) * LANES]
                c0 = part * DIL_OUT + half * LANES
                for r in range(dil):
                    a_ref[r, :, c0:c0 + LANES] = stage[pl.ds(r, per, stride=dil), :].astype(a_ref.dtype)


def _in_proj(x2, w_in_bf, bsz, seq, tm=IN_PROJ_TILE):
    n, d = x2.shape
    cols = w_in_bf.shape[1]
    tiles = seq // tm
    row = lambda i: (i, 0)
    dil_spec = lambda dil: pl.BlockSpec((None, dil, tm // dil, 3 * DIL_OUT), lambda i: (i // tiles, 0, i % tiles, 0))
    dil_shape = lambda dil: jax.ShapeDtypeStruct((bsz, dil, seq // dil, 3 * DIL_OUT), BF16)
    dils = [dil for _, dil in DIL_GROUPS]
    return pl.pallas_call(
        functools.partial(_in_proj_kernel, d_model=d, tm=tm),
        grid=(n // tm,),
        in_specs=[pl.BlockSpec((tm, d), row),
                  pl.BlockSpec((d, cols), lambda i: (0, 0), pipeline_mode=pl.Buffered(1))],
        out_specs=[dil_spec(dil) for dil in dils]
        + [pl.BlockSpec((tm, SWA_Q_WIDTH), row), pl.BlockSpec((tm, 2 * SWA_KV_WIDTH), row),
           pl.BlockSpec((tm, 2 * d), row)],
        out_shape=[dil_shape(dil) for dil in dils]
        + [jax.ShapeDtypeStruct((n, SWA_Q_WIDTH), BF16), jax.ShapeDtypeStruct((n, 2 * SWA_KV_WIDTH), BF16),
           jax.ShapeDtypeStruct((n, 2 * d), BF16)],
        scratch_shapes=[pltpu.VMEM((tm, LANES), F32)] * (3 * DIL_OUT // LANES),
        compiler_params=_cparams("parallel"),
        name="in_proj",
    )(x2, w_in_bf)


_NT = (((1,), (1,)), ((), ()))


def _per_head_column(ref, head0, n_heads, scale=1.0):
    head_of_row = lax.broadcasted_iota(I32, (n_heads * ATTN_BLOCK, 1), 0) // ATTN_BLOCK
    col = jnp.zeros((n_heads * ATTN_BLOCK, 1), F32)
    for h in range(n_heads):
        col = jnp.where(head_of_row == h, ref[head0 + h] * scale, col)
    return col


def _store_band_bias(bias_ref, slope_col, max_diff):
    rows = slope_col.shape[0]
    qi = lax.broadcasted_iota(I32, (rows, ATTN_BLOCK), 0) % ATTN_BLOCK
    kj = lax.broadcasted_iota(I32, (rows, ATTN_BLOCK), 1)
    diff_prev = qi - kj + ATTN_BLOCK
    diff_cur = qi - kj
    bias_ref[0] = jnp.where(diff_prev <= max_diff, -slope_col * diff_prev.astype(F32), NEG_INF)
    bias_ref[1] = jnp.where((diff_cur >= 0) & (diff_cur <= max_diff), -slope_col * diff_cur.astype(F32), NEG_INF)


def _band_probs(s_prev, s_cur, sink_col):
    m = jnp.max(jnp.maximum(s_prev, s_cur), axis=-1, keepdims=True)
    if sink_col is not None:
        m = jnp.maximum(m, sink_col)
    return jnp.exp(s_prev - m).astype(BF16), jnp.exp(s_cur - m).astype(BF16), m


def _pair_masks():
    low = lax.broadcasted_iota(I32, (ATTN_BLOCK, LANES), 1) < HEAD_DIM
    scale = HEAD_DIM ** -0.5
    return low, jnp.where(low, scale, 0.0).astype(BF16), jnp.where(low, 0.0, scale).astype(BF16)


def _values_and_ones(v):
    return jnp.concatenate([v, jnp.ones((v.shape[0], LANES), v.dtype)], axis=1)


def _dil_attn_kernel(slope_ref, q_ref, kc_ref, kp_ref, vc_ref, vp_ref, o_ref, lse_ref, bias_ref,
                     *, tq, max_diff, dist_scale):
    first = pl.program_id(2) == 0
    _store_band_bias(bias_ref, _per_head_column(slope_ref, 0, DIL_HEADS, dist_scale), max_diff)
    low, keep_low, keep_high = _pair_masks()
    pair_cols = [slice(p * LANES, (p + 1) * LANES) for p in range(DIL_HEADS // 2)]
    for res, i in [(res, i) for res in range(q_ref.shape[0]) for i in range(tq // ATTN_BLOCK)]:
        rows = slice(i * ATTN_BLOCK, (i + 1) * ATTN_BLOCK)
        window = slice((i - 1) * ATTN_BLOCK, (i + 1) * ATTN_BLOCK)

        def key_window(cur_ref, first_prev_ref, c):
            if i == 0:
                return jnp.concatenate([first_prev_ref[res, :, c], cur_ref[res, rows, c]], axis=0)
            return cur_ref[res, window, c]

        scores = []
        for c in pair_cols:
            k_win = key_window(kc_ref, kp_ref, c)
            for keep in (keep_low, keep_high):
                scores.append(lax.dot_general(q_ref[res, rows, c] * keep, k_win, _NT, preferred_element_type=F32))
        scores = jnp.concatenate(scores, axis=0)
        s_prev = scores[:, :ATTN_BLOCK] + bias_ref[0]
        s_cur = scores[:, ATTN_BLOCK:] + bias_ref[1]
        if i == 0:
            s_prev = jnp.where(first, NEG_INF, s_prev)
        p_prev, p_cur, m = _band_probs(s_prev, s_cur, None)
        probs = jnp.concatenate([p_prev, p_cur], axis=1)
        for p, c in enumerate(pair_cols):
            v_win = _values_and_ones(key_window(vc_ref, vp_ref, c))
            od, mh = [], []
            for h in (2 * p, 2 * p + 1):
                hr = slice(h * ATTN_BLOCK, (h + 1) * ATTN_BLOCK)
                od.append(jnp.dot(probs[hr], v_win, preferred_element_type=F32))
                mh.append(m[hr])
            denom = jnp.where(low, od[0][:, LANES:], od[1][:, LANES:])
            o_ref[res, rows, c] = (jnp.where(low, od[0][:, :LANES], od[1][:, :LANES]) / denom).astype(o_ref.dtype)
            lse_ref[res, rows, c] = jnp.where(low, mh[0], mh[1]) + jnp.log(denom)


def _dil_attention(a_g, slopes_g, g):
    window, dil = DIL_GROUPS[g]
    bsz, _, sub_len, _ = a_g.shape
    tq = min(DIL_ATTN_ROWS, sub_len)
    res_per_step = DIL_ATTN_ROWS // tq
    nqb = tq // ATTN_BLOCK
    cur = lambda part: (lambda b, r, m: (b, r, m, part))
    prev = lambda part: (lambda b, r, m: (b, r, jnp.maximum(m * nqb - 1, 0), part))
    blk = lambda rows, imap: pl.BlockSpec((None, res_per_step, rows, DIL_OUT), imap)
    return pl.pallas_call(
        functools.partial(_dil_attn_kernel, tq=tq, max_diff=window // dil, dist_scale=float(dil)),
        grid=(bsz, dil // res_per_step, sub_len // tq),
        in_specs=[pl.BlockSpec(memory_space=pltpu.SMEM),
                  blk(tq, cur(0)), blk(tq, cur(1)), blk(ATTN_BLOCK, prev(1)),
                  blk(tq, cur(2)), blk(ATTN_BLOCK, prev(2))],
        out_specs=[blk(tq, cur(0)), blk(tq, cur(0))],
        out_shape=[jax.ShapeDtypeStruct((bsz, dil, sub_len, DIL_OUT), BF16),
                   jax.ShapeDtypeStruct((bsz, dil, sub_len, DIL_OUT), F32)],
        scratch_shapes=[pltpu.VMEM((2, DIL_HEADS * ATTN_BLOCK, ATTN_BLOCK), F32)],
        compiler_params=_cparams("parallel", "parallel", "arbitrary"),
        name=f"dil_attn_g{g}",
    )(slopes_g, a_g, a_g, a_g, a_g, a_g)


def _swa_attn_kernel(slope_ref, sink_ref, q_ref, kvc_ref, kvp_ref, o_ref, bias_ref, *, tq):
    first = pl.program_id(1) == 0
    low, keep_low, keep_high = _pair_masks()

    def both_halves(ref, lane0):
        part = ref[:, lane0:lane0 + HEAD_DIM]
        return jnp.concatenate([part, part], axis=1)

    tile = (SWA_REP, ATTN_BLOCK, ATTN_BLOCK)
    stacked = (SWA_REP * ATTN_BLOCK, ATTN_BLOCK)
    qi = lax.broadcasted_iota(I32, (ATTN_BLOCK, ATTN_BLOCK), 0)
    kj = lax.broadcasted_iota(I32, (ATTN_BLOCK, ATTN_BLOCK), 1)
    from_prev = kj > qi
    diff = jnp.where(from_prev, qi - kj + ATTN_BLOCK, qi - kj).astype(F32)
    for kvh in range(SWA_KV_HEADS):
        head0 = kvh * SWA_REP
        for r in range(SWA_REP):
            bias_ref[r] = -slope_ref[head0 + r] * diff
        sink_col = _per_head_column(sink_ref, head0, SWA_REP)
        k_cur_all, k_first = both_halves(kvc_ref, kvh * HEAD_DIM), both_halves(kvp_ref, kvh * HEAD_DIM)
        v_lane0 = SWA_KV_WIDTH + kvh * HEAD_DIM
        v_cur_all, v_first = both_halves(kvc_ref, v_lane0), both_halves(kvp_ref, v_lane0)
        pair_cols = [slice((head0 + 2 * p) * HEAD_DIM, (head0 + 2 * p + 2) * HEAD_DIM) for p in range(SWA_REP // 2)]
        for i in range(tq // ATTN_BLOCK):
            rows = slice(i * ATTN_BLOCK, (i + 1) * ATTN_BLOCK)
            if i == 0:
                k_win = jnp.concatenate([k_first, k_cur_all[rows]], axis=0)
                v_win = jnp.concatenate([v_first, v_cur_all[rows]], axis=0)
            else:
                win = slice((i - 1) * ATTN_BLOCK, (i + 1) * ATTN_BLOCK)
                k_win, v_win = k_cur_all[win], v_cur_all[win]
            q = jnp.concatenate([q_ref[rows, c] * keep for c in pair_cols for keep in (keep_low, keep_high)], axis=0)
            s2 = lax.dot_general(q, k_win, _NT, preferred_element_type=F32)
            s = jnp.where(from_prev, s2[:, :ATTN_BLOCK].reshape(tile), s2[:, ATTN_BLOCK:].reshape(tile)) \
                + bias_ref[...]
            if i == 0:
                s = jnp.where(from_prev & first, NEG_INF, s)
            s = s.reshape(stacked)
            m = jnp.maximum(jnp.max(s, axis=-1, keepdims=True), sink_col)
            p = jnp.exp(s - m).reshape(tile)
            p2 = jnp.concatenate([jnp.where(from_prev, p, 0.0).reshape(stacked).astype(BF16),
                                  jnp.where(from_prev, 0.0, p).reshape(stacked).astype(BF16)], axis=1)
            od = jnp.dot(p2, _values_and_ones(v_win), preferred_element_type=F32)
            o = od[:, :LANES] / (od[:, LANES:] + jnp.exp(sink_col - m))
            for p, c in enumerate(pair_cols):
                even = o[2 * p * ATTN_BLOCK:(2 * p + 1) * ATTN_BLOCK]
                odd = o[(2 * p + 1) * ATTN_BLOCK:(2 * p + 2) * ATTN_BLOCK]
                o_ref[rows, c] = jnp.where(low, even, odd).astype(o_ref.dtype)


def _swa_attention(hq, hkv, slopes_b, sinks, bsz, seq, tq=ROW_TILE):
    nqb = tq // ATTN_BLOCK
    hq3 = hq.reshape(bsz, seq, SWA_Q_WIDTH)
    hkv3 = hkv.reshape(bsz, seq, 2 * SWA_KV_WIDTH)
    smem = pl.BlockSpec(memory_space=pltpu.SMEM)
    out = pl.pallas_call(
        functools.partial(_swa_attn_kernel, tq=tq),
        grid=(bsz, seq // tq),
        in_specs=[smem, smem,
                  pl.BlockSpec((None, tq, SWA_Q_WIDTH), lambda b, m: (b, m, 0)),
                  pl.BlockSpec((None, tq, 2 * SWA_KV_WIDTH), lambda b, m: (b, m, 0)),
                  pl.BlockSpec((None, ATTN_BLOCK, 2 * SWA_KV_WIDTH),
                               lambda b, m: (b, jnp.maximum(m * nqb - 1, 0), 0))],
        out_specs=pl.BlockSpec((None, tq, SWA_Q_WIDTH), lambda b, m: (b, m, 0)),
        out_shape=jax.ShapeDtypeStruct((bsz, seq, SWA_Q_WIDTH), BF16),
        scratch_shapes=[pltpu.VMEM((SWA_REP, ATTN_BLOCK, ATTN_BLOCK), F32)],
        compiler_params=_cparams("parallel", "arbitrary"),
        name="swa_attn",
    )(slopes_b, sinks, hq3, hkv3, hkv3)
    return out.reshape(bsz * seq, SWA_Q_WIDTH)


def _sigmoid(x):
    return 0.5 * (jnp.tanh(0.5 * x) + 1.0)


def _layer_norm(z, g, b):
    mu = jnp.mean(z, axis=-1, keepdims=True)
    zc = z - mu
    var = jnp.mean(zc * zc, axis=-1, keepdims=True)
    return zc * lax.rsqrt(var + LN_EPS) * g + b


def _mix_out_kernel(x_ref, o0_ref, o1_ref, o2_ref, l0_ref, l1_ref, l2_ref, ob_ref, g_ref,
                    wpa_ref, wpb_ref, wo_ref, lng_ref, lnb_ref, rw_ref, rb_ref,
                    y_ref, ypk_ref, idx_ref, gate_ref, *scratch, d_model):
    def natural(ref, stages):
        dil, per, _ = ref.shape
        if dil == 1:
            return ref[0].astype(F32)
        for half, stage in enumerate(stages):
            for r in range(dil):
                stage[pl.ds(r, per, stride=dil), :] = ref[r, :, half * LANES:(half + 1) * LANES].astype(F32)
        return jnp.concatenate([stage[...] for stage in stages], axis=1)

    o_nat = natural(o0_ref, None), natural(o1_ref, scratch[0:2]), natural(o2_ref, scratch[2:4])
    l_nat = natural(l0_ref, None), natural(l1_ref, scratch[4:6]), natural(l2_ref, scratch[6:8])

    def chain(rows):
        (o0, o1, o2), (l0, l1, l2) = [a[rows] for a in o_nat], [a[rows] for a in l_nat]
        lm = jnp.maximum(jnp.maximum(l0, l1), l2)
        e0, e1, e2 = jnp.exp(l0 - lm), jnp.exp(l1 - lm), jnp.exp(l2 - lm)
        out_a = (e0 * o0 + e1 * o1 + e2 * o2) / (e0 + e1 + e2)
        pa = jnp.dot(out_a.astype(BF16), wpa_ref[...], preferred_element_type=F32)
        pb = jnp.dot(ob_ref[rows, :], wpb_ref[...], preferred_element_type=F32)
        ga = _sigmoid(g_ref[rows, :d_model].astype(F32))
        gb = _sigmoid(g_ref[rows, d_model:].astype(F32))
        merged = ga * pa + gb * pb
        mix = jnp.dot(merged.astype(BF16), wo_ref[...], preferred_element_type=F32)
        y = _layer_norm(DEEPNORM_ALPHA * x_ref[rows, :] + mix, lng_ref[...], lnb_ref[...])
        y_ref[rows, :] = y
        ypk_ref[rows, :] = _pack_bf16_pairs(y)

        logits = lax.dot_general(rw_ref[...], y.astype(BF16), _NT, preferred_element_type=F32) + rb_ref[...]
        expert = lax.broadcasted_iota(I32, logits.shape, 0)
        slot = lax.broadcasted_iota(I32, (SUBLANES, logits.shape[1]), 0)
        idx_out = jnp.zeros(slot.shape, I32)
        val_out = jnp.full(slot.shape, -jnp.inf, F32)
        for k in range(TOP_K):
            top = jnp.max(logits, axis=0, keepdims=True)
            top_idx = jnp.min(jnp.where(logits == top, expert, N_EXPERTS), axis=0, keepdims=True)
            idx_out = jnp.where(slot == k, top_idx, idx_out)
            val_out = jnp.where(slot == k, top, val_out)
            logits = jnp.where(expert == top_idx, -jnp.inf, logits)
        ev = jnp.exp(val_out - jnp.max(val_out, axis=0, keepdims=True))
        idx_ref[:, rows] = idx_out
        gate_ref[:, rows] = ev / jnp.sum(ev, axis=0, keepdims=True)

    chain(slice(0, x_ref.shape[0]))


def _mix_out(x2, o_g, lse_g, out_b, gates_h, wpa, wpb, wo, ln_g, ln_b, rw_t, rb_col, seq, tm=ROW_TILE):
    n, d = x2.shape
    tiles = seq // tm
    row = lambda i: (i, 0)
    const = lambda i: (0, 0)
    rb = lambda w: pl.BlockSpec((tm, w), row)
    full = lambda a: pl.BlockSpec(a.shape, const)
    dil_specs = [pl.BlockSpec((None, dil, tm // dil, DIL_OUT), lambda i: (i // tiles, 0, i % tiles, 0))
                 for _, dil in DIL_GROUPS]
    return pl.pallas_call(
        functools.partial(_mix_out_kernel, d_model=d),
        grid=(n // tm,),
        in_specs=[rb(d)] + dil_specs + dil_specs + [rb(SWA_Q_WIDTH), rb(2 * d),
                  full(wpa), full(wpb), full(wo), full(ln_g), full(ln_b), full(rw_t), full(rb_col)],
        out_specs=[rb(d), rb(d // 2), pl.BlockSpec((SUBLANES, tm), lambda i: (0, i)),
                   pl.BlockSpec((SUBLANES, tm), lambda i: (0, i))],
        out_shape=[jax.ShapeDtypeStruct((n, d), F32), jax.ShapeDtypeStruct((n, d // 2), jnp.uint32),
                   jax.ShapeDtypeStruct((SUBLANES, n), I32), jax.ShapeDtypeStruct((SUBLANES, n), F32)],
        scratch_shapes=[pltpu.VMEM((tm, LANES), F32)] * (4 * DIL_OUT // LANES),
        compiler_params=_cparams("parallel"),
        name="mix_out",
    )(x2, *o_g, *lse_g, out_b, gates_h, wpa, wpb, wo, ln_g, ln_b, rw_t, rb_col)


def _token_mixer_ln1(x2, w_in, sinks, w_proj_a, w_proj_b, w_out, ln1_g, ln1_b, router_w, router_b, bsz, seq):
    n, d = x2.shape
    heads = jnp.arange(1, N_ALIBI_HEADS + 1, dtype=F32)
    slopes = jnp.exp2(-8.0 * heads / N_ALIBI_HEADS)
    *a_g, hq, hkv, gates_h = _in_proj(x2, w_in.astype(BF16), bsz, seq)
    o_g, lse_g = [], []
    for g in range(N_DIL):
        sl = slopes[SWA_Q_HEADS + g * DIL_HEADS:SWA_Q_HEADS + (g + 1) * DIL_HEADS]
        o, lse = _dil_attention(a_g[g], sl, g)
        o_g.append(o)
        lse_g.append(lse)
    out_b = _swa_attention(hq, hkv, slopes[:SWA_Q_HEADS], sinks.astype(F32), bsz, seq)
    return _mix_out(x2, o_g, lse_g, out_b, gates_h, w_proj_a.astype(BF16), w_proj_b.astype(BF16),
                    w_out.astype(BF16), ln1_g.reshape(1, d).astype(F32), ln1_b.reshape(1, d).astype(F32),
                    router_w.T.astype(BF16), router_b.reshape(N_EXPERTS, 1).astype(F32), seq)


def _route_kernel(idx_ref, dest_ref, tbl_ref, carry_ref, pstart_ref, *, expert_block):
    phase, i = pl.program_id(0), pl.program_id(1)
    idx = idx_ref[...]
    t = idx.shape[1]
    expert = lax.broadcasted_iota(I32, (N_EXPERTS, t), 0)
    onehot = [expert == idx[k:k + 1, :] for k in range(TOP_K)]
    multi = sum(oh.astype(F32) for oh in onehot)
    tile_cnt = jnp.sum(multi, axis=1, keepdims=True).astype(I32)

    @pl.when((phase == 0) & (i == 0))
    def _():
        carry_ref[...] = jnp.zeros_like(carry_ref)

    @pl.when((phase == 1) & (i == 0))
    def _():
        counts = carry_ref[...]
        blocks = (counts + (expert_block - 1)) // expert_block
        er = lax.broadcasted_iota(I32, (N_EXPERTS, N_EXPERTS), 0)
        ec = lax.broadcasted_iota(I32, (N_EXPERTS, N_EXPERTS), 1)
        below = (ec < er).astype(BF16)
        prefix = lambda v: jnp.dot(below, v.astype(F32).astype(BF16), preferred_element_type=F32).astype(I32)
        pstart = (prefix(blocks >> 6) * 64 + prefix(blocks & 63)) * expert_block
        pstart_ref[...] = pstart
        carry_ref[...] = jnp.zeros_like(carry_ref)
        n_lanes = tbl_ref.shape[1]
        count_col, start_col, blocks_col = counts[:, 0:1], pstart[:, 0:1], blocks[:, 0:1]
        block_row0 = lax.broadcasted_iota(I32, (1, n_lanes), 1) * expert_block
        ends_before = (start_col + blocks_col * expert_block <= block_row0).astype(I32)
        block_e = jnp.minimum(jnp.sum(ends_before, axis=0, keepdims=True), N_EXPERTS - 1)
        mine = lax.broadcasted_iota(I32, (N_EXPERTS, n_lanes), 0) == block_e
        count_b = jnp.sum(jnp.where(mine, count_col, 0), axis=0, keepdims=True)
        start_b = jnp.sum(jnp.where(mine, start_col, 0), axis=0, keepdims=True)
        n_valid = jnp.clip(count_b - (block_row0 - start_b), 0, expert_block)
        n_used = jnp.sum(blocks_col, axis=0, keepdims=True)
        expert_ids = lax.broadcasted_iota(I32, (N_EXPERTS, n_lanes), 0)
        later_used = (expert_ids > block_e) & (blocks_col > 0)
        next_e = jnp.min(jnp.where(later_used, expert_ids, N_EXPERTS), axis=0, keepdims=True)
        next_e = jnp.where(next_e == N_EXPERTS, -1, next_e)
        row = lax.broadcasted_iota(I32, (SUBLANES, n_lanes), 0)
        tbl_ref[...] = jnp.where(row == 0, block_e,
                                 jnp.where(row == 1, n_valid, jnp.where(row == 2, n_used, next_e)))

    @pl.when(phase == 1)
    def _():
        r = lax.broadcasted_iota(I32, (t, t), 0)
        c = lax.broadcasted_iota(I32, (t, t), 1)
        earlier = (r < c).astype(BF16)
        cum = jnp.dot(multi.astype(BF16), earlier, preferred_element_type=F32).astype(I32)
        base = cum + carry_ref[:, 0:1] + pstart_ref[:, 0:1]
        slot = lax.broadcasted_iota(I32, (SUBLANES, t), 0)
        dest = jnp.zeros((SUBLANES, t), I32)
        for k in range(TOP_K):
            d_k = jnp.sum(jnp.where(onehot[k], base, 0), axis=0, keepdims=True)
            dest = jnp.where(slot == k, d_k, dest)
        dest_ref[...] = dest

    carry_ref[...] = carry_ref[...] + tile_cnt


def _route(idx_t, expert_block, n_blocks, tile=2 * ROW_TILE):
    n = idx_t.shape[1]
    tile = min(tile, n)
    per_expert = (N_EXPERTS, LANES)
    table = jax.ShapeDtypeStruct((SUBLANES, pl.cdiv(n_blocks, LANES) * LANES), I32)
    return pl.pallas_call(
        functools.partial(_route_kernel, expert_block=expert_block),
        grid=(2, n // tile),
        in_specs=[pl.BlockSpec((SUBLANES, tile), lambda p, i: (0, i))],
        out_specs=[pl.BlockSpec((SUBLANES, tile), lambda p, i: (0, i * p)),
                   pl.BlockSpec(table.shape, lambda p, i: (0, 0))],
        out_shape=[jax.ShapeDtypeStruct((SUBLANES, n), I32), table],
        scratch_shapes=[pltpu.VMEM(per_expert, I32), pltpu.VMEM(per_expert, I32)],
        compiler_params=_cparams("arbitrary", "arbitrary"),
        name="route",
    )(idx_t)


def _sc_dispatch(dest_t, y, rows, window=64):
    n, d = y.shape
    info = plsc.get_sparse_core_info()
    n_cores = info.num_cores
    per_worker = n // (n_cores * info.num_subcores)
    window = min(window, per_worker // 2)
    n_chunks = per_worker // window
    assert n == n_chunks * window * n_cores * info.num_subcores and n_chunks % 2 == 0
    mesh = plsc.VectorSubcoreMesh(core_axis_name="c", subcore_axis_name="s")
    dest_chunks = dest_t.reshape(dest_t.shape[0], n // window, window)

    @functools.partial(pl.kernel, mesh=mesh, out_type=jax.ShapeDtypeStruct((rows, d), y.dtype),
                       scratch_types=[pltpu.VMEM((TOP_K, n_chunks, window), I32)]
                       + [pltpu.VMEM((window, d), y.dtype)] * 2 + [pltpu.SemaphoreType.DMA] * 4,
                       name="sc_dispatch")
    def body(dest_hbm, y_hbm, xs_hbm, idx_v, buf0, buf1, lsem0, lsem1, ssem0, ssem1):
        bufs, lsem, ssem = (buf0, buf1), (lsem0, lsem1), (ssem0, ssem1)
        worker = lax.axis_index("s") * n_cores + lax.axis_index("c")
        base = worker * per_worker
        for k in range(TOP_K):
            pltpu.sync_copy(dest_hbm.at[k, pl.ds(worker * n_chunks, n_chunks)], idx_v.at[k])

        def load(c, b):
            return pltpu.make_async_copy(y_hbm.at[pl.ds(base + c * window, window)], bufs[b], lsem[b])

        def scatter(c, k, b):
            return pltpu.make_async_copy(bufs[b], xs_hbm.at[idx_v.at[k, c]], ssem[b])

        load(0, 0).start()

        @pl.loop(0, n_chunks, step=2)
        def _(c0):
            for b in range(2):
                c = c0 + b
                load(c, b).wait()

                @pl.when(c > 0)
                def _():
                    for k in range(TOP_K):
                        scatter(c - 1, k, 1 - b).wait()

                @pl.when(c + 1 < n_chunks)
                def _():
                    load(c + 1, 1 - b).start()

                for k in range(TOP_K):
                    scatter(c, k, b).start()

        for k in range(TOP_K):
            scatter(n_chunks - 1, k, 1).wait()

    return body(dest_chunks, y)


def _expert_kernel(be_ref, nu_ref, nv_ref, nx_ref, x_ref, wg_hbm, bg_ref, wu_hbm, bu_ref, wd_hbm, bd_ref, o_ref,
                   wg_bf, wu_bf, wd_bf, wg_st, wu_st, wd_st, w_sem):
    j = pl.program_id(0)
    used = j < nu_ref[0]
    new_expert = (j == 0) | (be_ref[j] != be_ref[jnp.maximum(j - 1, 0)])
    staged = ((wg_hbm, wg_st, wg_bf), (wu_hbm, wu_st, wu_bf), (wd_hbm, wd_st, wd_bf))

    def fetch(e):
        return [pltpu.make_async_copy(hbm.at[e], st, w_sem.at[i]) for i, (hbm, st, _) in enumerate(staged)]

    @pl.when(used & new_expert)
    def _():
        @pl.when(j == 0)
        def _():
            for copy in fetch(be_ref[0]):
                copy.start()

        for copy in fetch(be_ref[j]):
            copy.wait()
        chunk = 128
        for _, st, dst in staged:
            for r in range(0, st.shape[0], chunk):
                dst[r:r + chunk, :] = st[r:r + chunk, :].astype(BF16)

        @pl.when(nx_ref[j] >= 0)
        def _():
            for copy in fetch(nx_ref[j]):
                copy.start()

    @pl.when(used)
    def _():
        row = lax.broadcasted_iota(I32, x_ref.shape, 0)
        x_lo, x_hi = _unpack_bf16_pairs(jnp.where(row < nv_ref[j], x_ref[...], jnp.uint32(0)))
        xb = jnp.concatenate([x_lo.astype(BF16), x_hi.astype(BF16)], axis=1)
        gt = jnp.dot(xb, wg_bf[...], preferred_element_type=F32) + bg_ref[...]
        up = jnp.dot(xb, wu_bf[...], preferred_element_type=F32) + bu_ref[...]
        gt = jnp.minimum(gt, SWIGLU_LIMIT)
        up = jnp.clip(up, -SWIGLU_LIMIT, SWIGLU_LIMIT)
        hdn = gt * _sigmoid(SWIGLU_ALPHA * gt) * (up + 1.0)
        o_ref[...] = _pack_bf16_pairs(jnp.dot(hdn.astype(BF16), wd_bf[...], preferred_element_type=F32)
                                      + bd_ref[...])

    @pl.when(jnp.logical_not(used))
    def _():
        o_ref[...] = jnp.zeros_like(o_ref)


def _experts(block_e, n_used, n_valid, next_e, xs, wg, bg, wu, bu, wd, bd, expert_block):
    rows, _ = xs.shape
    e, d, f = wg.shape
    xmap = lambda j, be, nu, nv, nx: (jnp.minimum(j, nu[0] - 1), 0)
    bmap = lambda j, be, nu, nv, nx: (be[j], 0, 0)
    weights = pl.BlockSpec(memory_space=pl.ANY)
    grid_spec = pltpu.PrefetchScalarGridSpec(
        num_scalar_prefetch=4,
        grid=(rows // expert_block,),
        in_specs=[pl.BlockSpec((expert_block, d // 2), xmap),
                  weights, pl.BlockSpec((None, 1, f), bmap),
                  weights, pl.BlockSpec((None, 1, f), bmap),
                  weights, pl.BlockSpec((None, 1, d), bmap)],
        out_specs=pl.BlockSpec((expert_block, d // 2), lambda j, be, nu, nv, nx: (j, 0)),
        scratch_shapes=[pltpu.VMEM((d, f), BF16), pltpu.VMEM((d, f), BF16), pltpu.VMEM((f, d), BF16),
                        pltpu.VMEM((d, f), wg.dtype), pltpu.VMEM((d, f), wu.dtype), pltpu.VMEM((f, d), wd.dtype),
                        pltpu.SemaphoreType.DMA((3,))],
    )
    return pl.pallas_call(
        _expert_kernel,
        grid_spec=grid_spec,
        out_shape=jax.ShapeDtypeStruct((rows, d // 2), jnp.uint32),
        compiler_params=_cparams("arbitrary"),
        name="experts",
    )(block_e, n_used, n_valid, next_e, xs, wg, bg.reshape(e, 1, f), wu, bu.reshape(e, 1, f), wd,
      bd.reshape(e, 1, d))


def _sc_gather(dest_t, ys, n, window=64):
    _, d = ys.shape
    info = plsc.get_sparse_core_info()
    n_cores = info.num_cores
    per_worker = n // (n_cores * info.num_subcores)
    mesh = plsc.VectorSubcoreMesh(core_axis_name="c", subcore_axis_name="s")
    window = min(window, per_worker)
    n_chunks = per_worker // window
    assert n == n_chunks * window * n_cores * info.num_subcores
    n_buf = 2

    @functools.partial(pl.kernel, mesh=mesh, out_type=jax.ShapeDtypeStruct((TOP_K, n, d), ys.dtype),
                       scratch_types=[pltpu.VMEM((TOP_K, per_worker), I32)]
                       + [pltpu.VMEM((window, d), ys.dtype)] * n_buf + [pltpu.SemaphoreType.DMA] * (2 * n_buf),
                       name="sc_gather")
    def body(dest_hbm, ys_hbm, out_hbm, idx_v, buf0, buf1, gsem0, gsem1, ssem0, ssem1):
        bufs, gsem, ssem = (buf0, buf1), (gsem0, gsem1), (ssem0, ssem1)
        base = (lax.axis_index("s") * n_cores + lax.axis_index("c")) * per_worker
        for k in range(TOP_K):
            pltpu.sync_copy(dest_hbm.at[k, pl.ds(base, per_worker)], idx_v.at[k])

        def gather(c, k):
            return pltpu.make_async_copy(ys_hbm.at[idx_v.at[k, pl.ds(c * window, window)]], bufs[k % n_buf],
                                         gsem[k % n_buf])

        def store(c, k):
            return pltpu.make_async_copy(bufs[k % n_buf], out_hbm.at[k, pl.ds(base + c * window, window)],
                                         ssem[k % n_buf])

        gather(0, 0).start()

        @pl.loop(0, n_chunks)
        def _(c):
            for k in range(TOP_K):
                if k > 0:
                    store(c, k - 1).wait()
                    if k + 1 < TOP_K:
                        gather(c, k + 1).start()
                    else:
                        @pl.when(c + 1 < n_chunks)
                        def _():
                            gather(c + 1, 0).start()
                else:
                    @pl.when(c > 0)
                    def _():
                        store(c - 1, TOP_K - 1).wait()
                    gather(c, 1).start()
                gather(c, k).wait()
                store(c, k).start()

        store(n_chunks - 1, TOP_K - 1).wait()

    return body(dest_t, ys)


def _combine_dense_kernel(gate_ref, y_ref, lng_ref, lnb_ref, g_ref, o_ref):
    gates = gate_ref[...].T
    ffn_lo = ffn_hi = None
    for k in range(TOP_K):
        lo, hi = _unpack_bf16_pairs(g_ref[k])
        gate = gates[:, k:k + 1]
        ffn_lo = gate * lo if k == 0 else ffn_lo + gate * lo
        ffn_hi = gate * hi if k == 0 else ffn_hi + gate * hi
    ffn = jnp.concatenate([ffn_lo, ffn_hi], axis=1)
    o_ref[...] = _layer_norm(DEEPNORM_ALPHA * y_ref[...] + ffn, lng_ref[...], lnb_ref[...])


def _combine_dense(gates, y, ln_g, ln_b, gathered, tile=2 * ROW_TILE):
    n, d = y.shape
    tile = min(tile, n)
    row = lambda i: (i, 0)
    const = lambda i: (0, 0)
    return pl.pallas_call(
        _combine_dense_kernel,
        grid=(n // tile,),
        in_specs=[pl.BlockSpec((SUBLANES, tile), lambda i: (0, i)), pl.BlockSpec((tile, d), row),
                  pl.BlockSpec((1, d), const), pl.BlockSpec((1, d), const),
                  pl.BlockSpec((TOP_K, tile, d // 2), lambda i: (0, i, 0))],
        out_specs=pl.BlockSpec((tile, d), row),
        out_shape=jax.ShapeDtypeStruct((n, d), F32),
        compiler_params=_cparams("parallel"),
        name="combine_dense",
    )(gates, y, ln_g, ln_b, gathered)


MOE_ROW_BLOCK = 512


def _moe_ln2(y, y_packed, idx, gates, w_gate, b_gate, w_up, b_up, w_down, b_down, ln2_g, ln2_b):
    n, d = y.shape
    n_blocks = n * TOP_K // MOE_ROW_BLOCK + N_EXPERTS
    rows = n_blocks * MOE_ROW_BLOCK
    dest_t, block_table = _route(idx, MOE_ROW_BLOCK, n_blocks)
    block_e, n_valid, n_used = block_table[0, :n_blocks], block_table[1, :n_blocks], block_table[2, :1]
    next_e = block_table[3, :n_blocks]
    xs = _sc_dispatch(dest_t, y_packed, rows)
    ys = _experts(block_e, n_used, n_valid, next_e, xs, w_gate, b_gate.astype(F32), w_up, b_up.astype(F32),
                  w_down, b_down.astype(F32), MOE_ROW_BLOCK)
    gathered = _sc_gather(dest_t, ys, n)
    return _combine_dense(gates, y, ln2_g.reshape(1, d).astype(F32), ln2_b.reshape(1, d).astype(F32), gathered)


def kernel(x, w_in, sinks, w_proj_a, w_proj_b, w_out, ln1_g, ln1_b, router_w, router_b,
           w_gate, b_gate, w_up, b_up, w_down, b_down, ln2_g, ln2_b):
    bsz, seq, d = x.shape
    assert seq % (DIL_GROUPS[-1][1] * ATTN_BLOCK) == 0 and seq % IN_PROJ_TILE == 0
    assert w_in.shape[-1] == GATE_OFF + 2 * d and w_proj_b.shape[-2] == SWA_Q_WIDTH
    assert router_w.shape[-1] == N_EXPERTS and (bsz * seq * TOP_K) % MOE_ROW_BLOCK == 0
    h = x.reshape(bsz * seq, d)
    for l in range(w_in.shape[0]):
        y, y_packed, idx, gates = _token_mixer_ln1(h, w_in[l], sinks[l], w_proj_a[l], w_proj_b[l], w_out[l],
                                                   ln1_g[l], ln1_b[l], router_w[l], router_b[l], bsz, seq)
        h = _moe_ln2(y, y_packed, idx, gates, w_gate[l], b_gate[l], w_up[l], b_up[l], w_down[l], b_down[l],
                     ln2_g[l], ln2_b[l])
    return h.reshape(bsz, seq, d)
```

```python
import functools

import jax
import jax.numpy as jnp
from jax import lax
from jax.experimental import pallas as pl
from jax.experimental.pallas import tpu as pltpu
from jax.experimental.pallas import tpu_sc as plsc

F32 = jnp.float32
BF16 = jnp.bfloat16
I32 = jnp.int32

HEAD_DIM = 64
DIL_GROUPS = ((128, 1), (512, 4), (2048, 16))
DIL_HEADS = 4
N_DIL = len(DIL_GROUPS)
DIL_OUT = DIL_HEADS * HEAD_DIM
DIL_WIDTH = N_DIL * DIL_OUT
SWA_Q_HEADS = 16
SWA_KV_HEADS = 2
SWA_REP = SWA_Q_HEADS // SWA_KV_HEADS
SWA_WINDOW = 128
SWA_Q_WIDTH = SWA_Q_HEADS * HEAD_DIM
SWA_KV_WIDTH = SWA_KV_HEADS * HEAD_DIM
N_ALIBI_HEADS = SWA_Q_HEADS + N_DIL * DIL_HEADS
ATTN_BLOCK = 128
N_EXPERTS = 32
TOP_K = 4
SWIGLU_LIMIT = 7.0
SWIGLU_ALPHA = 1.702
LN_EPS = 1e-5
DEPTH = 1
DEEPNORM_ALPHA = (2 * DEPTH) ** 0.25
NEG_INF = -1e30

LANES = 128
SUBLANES = 8
VMEM_LIMIT_BYTES = 56 * 1024 * 1024
ROW_TILE = 512
IN_PROJ_TILE = 2 * ROW_TILE
COL_CHUNK = 512
DIL_ATTN_ROWS = 2 * ROW_TILE

A_QKV_W = 3 * DIL_WIDTH
B_Q_OFF = A_QKV_W
B_KV_OFF = B_Q_OFF + SWA_Q_WIDTH
GATE_OFF = B_KV_OFF + 2 * SWA_KV_WIDTH


def _cparams(*sem):
    return pltpu.CompilerParams(dimension_semantics=sem, vmem_limit_bytes=VMEM_LIMIT_BYTES)


def _pack_bf16_pairs(x):
    c = x.shape[1] // 2
    lo = lax.bitcast_convert_type(x[:, :c].astype(BF16).astype(F32), jnp.uint32)
    hi = lax.bitcast_convert_type(x[:, c:].astype(BF16).astype(F32), jnp.uint32)
    return (lo >> 16) | hi


def _unpack_bf16_pairs(w):
    lo = lax.bitcast_convert_type(w << 16, F32)
    hi = lax.bitcast_convert_type(w & jnp.uint32(0xFFFF0000), F32)
    return lo, hi


def _in_proj_kernel(x_ref, w_ref, a0_ref, a1_ref, a2_ref, hq_ref, hkv_ref, g_ref, *scratch, d_model, tm):
    xb = x_ref[...].astype(BF16)
    segments = ((hq_ref, B_Q_OFF, SWA_Q_WIDTH), (hkv_ref, B_KV_OFF, 2 * SWA_KV_WIDTH),
                (g_ref, GATE_OFF, 2 * d_model))
    for out_ref, col0, width in segments:
        for c in range(0, width, COL_CHUNK):
            w = min(COL_CHUNK, width - c)
            r = jnp.dot(xb, w_ref[:, col0 + c:col0 + c + w], preferred_element_type=F32)
            out_ref[:, c:c + w] = r.astype(out_ref.dtype)
    for g, a_ref in enumerate((a0_ref, a1_ref, a2_ref)):
        dil = DIL_GROUPS[g][1]
        per = tm // dil
        for part in range(3):
            col0 = part * DIL_WIDTH + g * DIL_OUT
            res = jnp.dot(xb, w_ref[:, col0:col0 + DIL_OUT], preferred_element_type=F32)
            if dil == 1:
                a_ref[0, :, part * DIL_OUT:(part + 1) * DIL_OUT] = res.astype(a_ref.dtype)
                continue
            for half in range(DIL_OUT // LANES):
                stage = scratch[part * (DIL_OUT // LANES) + half]
                stage[...] = res[:, half * LANES:(half + 1) * LANES]
                c0 = part * DIL_OUT + half * LANES
                for r in range(dil):
                    a_ref[r, :, c0:c0 + LANES] = stage[pl.ds(r, per, stride=dil), :].astype(a_ref.dtype)


def _in_proj(x2, w_in_bf, bsz, seq, tm=IN_PROJ_TILE):
    n, d = x2.shape
    cols = w_in_bf.shape[1]
    tiles = seq // tm
    row = lambda i: (i, 0)
    dil_spec = lambda dil: pl.BlockSpec((None, dil, tm // dil, 3 * DIL_OUT), lambda i: (i // tiles, 0, i % tiles, 0))
    dil_shape = lambda dil: jax.ShapeDtypeStruct((bsz, dil, seq // dil, 3 * DIL_OUT), BF16)
    dils = [dil for _, dil in DIL_GROUPS]
    return pl.pallas_call(
        functools.partial(_in_proj_kernel, d_model=d, tm=tm),
        grid=(n // tm,),
        in_specs=[pl.BlockSpec((tm, d), row),
                  pl.BlockSpec((d, cols), lambda i: (0, 0), pipeline_mode=pl.Buffered(1))],
        out_specs=[dil_spec(dil) for dil in dils]
        + [pl.BlockSpec((tm, SWA_Q_WIDTH), row), pl.BlockSpec((tm, 2 * SWA_KV_WIDTH), row),
           pl.BlockSpec((tm, 2 * d), row)],
        out_shape=[dil_shape(dil) for dil in dils]
        + [jax.ShapeDtypeStruct((n, SWA_Q_WIDTH), BF16), jax.ShapeDtypeStruct((n, 2 * SWA_KV_WIDTH), BF16),
           jax.ShapeDtypeStruct((n, 2 * d), BF16)],
        scratch_shapes=[pltpu.VMEM((tm, LANES), F32)] * (3 * DIL_OUT // LANES),
        compiler_params=_cparams("parallel"),
        name="in_proj",
    )(x2, w_in_bf)


_NT = (((1,), (1,)), ((), ()))


def _per_head_column(ref, head0, n_heads, scale=1.0):
    head_of_row = lax.broadcasted_iota(I32, (n_heads * ATTN_BLOCK, 1), 0) // ATTN_BLOCK
    col = jnp.zeros((n_heads * ATTN_BLOCK, 1), F32)
    for h in range(n_heads):
        col = jnp.where(head_of_row == h, ref[head0 + h] * scale, col)
    return col


def _store_band_bias(bias_ref, slope_col, max_diff):
    rows = slope_col.shape[0]
    qi = lax.broadcasted_iota(I32, (rows, ATTN_BLOCK), 0) % ATTN_BLOCK
    kj = lax.broadcasted_iota(I32, (rows, ATTN_BLOCK), 1)
    diff_prev = qi - kj + ATTN_BLOCK
    diff_cur = qi - kj
    bias_ref[0] = jnp.where(diff_prev <= max_diff, -slope_col * diff_prev.astype(F32), NEG_INF)
    bias_ref[1] = jnp.where((diff_cur >= 0) & (diff_cur <= max_diff), -slope_col * diff_cur.astype(F32), NEG_INF)


def _band_probs(s_prev, s_cur, sink_col):
    m = jnp.max(jnp.maximum(s_prev, s_cur), axis=-1, keepdims=True)
    if sink_col is not None:
        m = jnp.maximum(m, sink_col)
    return jnp.exp(s_prev - m).astype(BF16), jnp.exp(s_cur - m).astype(BF16), m


def _pair_masks():
    low = lax.broadcasted_iota(I32, (ATTN_BLOCK, LANES), 1) < HEAD_DIM
    scale = HEAD_DIM ** -0.5
    return low, jnp.where(low, scale, 0.0).astype(BF16), jnp.where(low, 0.0, scale).astype(BF16)


def _values_and_ones(v):
    return jnp.concatenate([v, jnp.ones((v.shape[0], LANES), v.dtype)], axis=1)


def _dil_attn_kernel(slope_ref, q_ref, kc_ref, kp_ref, vc_ref, vp_ref, o_ref, lse_ref, bias_ref,
                     *, tq, max_diff, dist_scale):
    first = pl.program_id(2) == 0
    _store_band_bias(bias_ref, _per_head_column(slope_ref, 0, DIL_HEADS, dist_scale), max_diff)
    low, keep_low, keep_high = _pair_masks()
    pair_cols = [slice(p * LANES, (p + 1) * LANES) for p in range(DIL_HEADS // 2)]
    for res, i in [(res, i) for res in range(q_ref.shape[0]) for i in range(tq // ATTN_BLOCK)]:
        rows = slice(i * ATTN_BLOCK, (i + 1) * ATTN_BLOCK)
        window = slice((i - 1) * ATTN_BLOCK, (i + 1) * ATTN_BLOCK)

        def key_window(cur_ref, first_prev_ref, c):
            if i == 0:
                return jnp.concatenate([first_prev_ref[res, :, c], cur_ref[res, rows, c]], axis=0)
            return cur_ref[res, window, c]

        scores = []
        for c in pair_cols:
            k_win = key_window(kc_ref, kp_ref, c)
            for keep in (keep_low, keep_high):
                scores.append(lax.dot_general(q_ref[res, rows, c] * keep, k_win, _NT, preferred_element_type=F32))
        scores = jnp.concatenate(scores, axis=0)
        s_prev = scores[:, :ATTN_BLOCK] + bias_ref[0]
        s_cur = scores[:, ATTN_BLOCK:] + bias_ref[1]
        if i == 0:
            s_prev = jnp.where(first, NEG_INF, s_prev)
        p_prev, p_cur, m = _band_probs(s_prev, s_cur, None)
        probs = jnp.concatenate([p_prev, p_cur], axis=1)
        for p, c in enumerate(pair_cols):
            v_win = _values_and_ones(key_window(vc_ref, vp_ref, c))
            od, mh = [], []
            for h in (2 * p, 2 * p + 1):
                hr = slice(h * ATTN_BLOCK, (h + 1) * ATTN_BLOCK)
                od.append(jnp.dot(probs[hr], v_win, preferred_element_type=F32))
                mh.append(m[hr])
            denom = jnp.where(low, od[0][:, LANES:], od[1][:, LANES:])
            o_ref[res, rows, c] = (jnp.where(low, od[0][:, :LANES], od[1][:, :LANES]) / denom).astype(o_ref.dtype)
            lse_ref[res, rows, c] = jnp.where(low, mh[0], mh[1]) + jnp.log(denom)


def _dil_attention(a_g, slopes_g, g):
    window, dil = DIL_GROUPS[g]
    bsz, _, sub_len, _ = a_g.shape
    tq = min(DIL_ATTN_ROWS, sub_len)
    res_per_step = DIL_ATTN_ROWS // tq
    nqb = tq // ATTN_BLOCK
    cur = lambda part: (lambda b, r, m: (b, r, m, part))
    prev = lambda part: (lambda b, r, m: (b, r, jnp.maximum(m * nqb - 1, 0), part))
    blk = lambda rows, imap: pl.BlockSpec((None, res_per_step, rows, DIL_OUT), imap)
    return pl.pallas_call(
        functools.partial(_dil_attn_kernel, tq=tq, max_diff=window // dil, dist_scale=float(dil)),
        grid=(bsz, dil // res_per_step, sub_len // tq),
        in_specs=[pl.BlockSpec(memory_space=pltpu.SMEM),
                  blk(tq, cur(0)), blk(tq, cur(1)), blk(ATTN_BLOCK, prev(1)),
                  blk(tq, cur(2)), blk(ATTN_BLOCK, prev(2))],
        out_specs=[blk(tq, cur(0)), blk(tq, cur(0))],
        out_shape=[jax.ShapeDtypeStruct((bsz, dil, sub_len, DIL_OUT), BF16),
                   jax.ShapeDtypeStruct((bsz, dil, sub_len, DIL_OUT), F32)],
        scratch_shapes=[pltpu.VMEM((2, DIL_HEADS * ATTN_BLOCK, ATTN_BLOCK), F32)],
        compiler_params=_cparams("parallel", "parallel", "arbitrary"),
        name=f"dil_attn_g{g}",
    )(slopes_g, a_g, a_g, a_g, a_g, a_g)


def _swa_attn_kernel(slope_ref, sink_ref, q_ref, kvc_ref, kvp_ref, o_ref, bias_ref, *, tq):
    first = pl.program_id(1) == 0
    low, keep_low, keep_high = _pair_masks()

    def both_halves(ref, lane0):
        part = ref[:, lane0:lane0 + HEAD_DIM]
        return jnp.concatenate([part, part], axis=1)

    tile = (SWA_REP, ATTN_BLOCK, ATTN_BLOCK)
    stacked = (SWA_REP * ATTN_BLOCK, ATTN_BLOCK)
    qi = lax.broadcasted_iota(I32, (ATTN_BLOCK, ATTN_BLOCK), 0)
    kj = lax.broadcasted_iota(I32, (ATTN_BLOCK, ATTN_BLOCK), 1)
    from_prev = kj > qi
    diff = jnp.where(from_prev, qi - kj + ATTN_BLOCK, qi - kj).astype(F32)
    for kvh in range(SWA_KV_HEADS):
        head0 = kvh * SWA_REP
        for r in range(SWA_REP):
            bias_ref[r] = -slope_ref[head0 + r] * diff
        sink_col = _per_head_column(sink_ref, head0, SWA_REP)
        k_cur_all, k_first = both_halves(kvc_ref, kvh * HEAD_DIM), both_halves(kvp_ref, kvh * HEAD_DIM)
        v_lane0 = SWA_KV_WIDTH + kvh * HEAD_DIM
        v_cur_all, v_first = both_halves(kvc_ref, v_lane0), both_halves(kvp_ref, v_lane0)
        pair_cols = [slice((head0 + 2 * p) * HEAD_DIM, (head0 + 2 * p + 2) * HEAD_DIM) for p in range(SWA_REP // 2)]
        for i in range(tq // ATTN_BLOCK):
            rows = slice(i * ATTN_BLOCK, (i + 1) * ATTN_BLOCK)
            if i == 0:
                k_win = jnp.concatenate([k_first, k_cur_all[rows]], axis=0)
                v_win = jnp.concatenate([v_first, v_cur_all[rows]], axis=0)
            else:
                win = slice((i - 1) * ATTN_BLOCK, (i + 1) * ATTN_BLOCK)
                k_win, v_win = k_cur_all[win], v_cur_all[win]
            q = jnp.concatenate([q_ref[rows, c] * keep for c in pair_cols for keep in (keep_low, keep_high)], axis=0)
            s2 = lax.dot_general(q, k_win, _NT, preferred_element_type=F32)
            s = jnp.where(from_prev, s2[:, :ATTN_BLOCK].reshape(tile), s2[:, ATTN_BLOCK:].reshape(tile)) \
                + bias_ref[...]
            if i == 0:
                s = jnp.where(from_prev & first, NEG_INF, s)
            s = s.reshape(stacked)
            m = jnp.maximum(jnp.max(s, axis=-1, keepdims=True), sink_col)
            p = jnp.exp(s - m).reshape(tile)
            p2 = jnp.concatenate([jnp.where(from_prev, p, 0.0).reshape(stacked).astype(BF16),
                                  jnp.where(from_prev, 0.0, p).reshape(stacked).astype(BF16)], axis=1)
            od = jnp.dot(p2, _values_and_ones(v_win), preferred_element_type=F32)
            o = od[:, :LANES] / (od[:, LANES:] + jnp.exp(sink_col - m))
            for p, c in enumerate(pair_cols):
                even = o[2 * p * ATTN_BLOCK:(2 * p + 1) * ATTN_BLOCK]
                odd = o[(2 * p + 1) * ATTN_BLOCK:(2 * p + 2) * ATTN_BLOCK]
                o_ref[rows, c] = jnp.where(low, even, odd).astype(o_ref.dtype)


def _swa_attention(hq, hkv, slopes_b, sinks, bsz, seq, tq=ROW_TILE):
    nqb = tq // ATTN_BLOCK
    hq3 = hq.reshape(bsz, seq, SWA_Q_WIDTH)
    hkv3 = hkv.reshape(bsz, seq, 2 * SWA_KV_WIDTH)
    smem = pl.BlockSpec(memory_space=pltpu.SMEM)
    out = pl.pallas_call(
        functools.partial(_swa_attn_kernel, tq=tq),
        grid=(bsz, seq // tq),
        in_specs=[smem, smem,
                  pl.BlockSpec((None, tq, SWA_Q_WIDTH), lambda b, m: (b, m, 0)),
                  pl.BlockSpec((None, tq, 2 * SWA_KV_WIDTH), lambda b, m: (b, m, 0)),
                  pl.BlockSpec((None, ATTN_BLOCK, 2 * SWA_KV_WIDTH),
                               lambda b, m: (b, jnp.maximum(m * nqb - 1, 0), 0))],
        out_specs=pl.BlockSpec((None, tq, SWA_Q_WIDTH), lambda b, m: (b, m, 0)),
        out_shape=jax.ShapeDtypeStruct((bsz, seq, SWA_Q_WIDTH), BF16),
        scratch_shapes=[pltpu.VMEM((SWA_REP, ATTN_BLOCK, ATTN_BLOCK), F32)],
        compiler_params=_cparams("parallel", "arbitrary"),
        name="swa_attn",
    )(slopes_b, sinks, hq3, hkv3, hkv3)
    return out.reshape(bsz * seq, SWA_Q_WIDTH)


def _sigmoid(x):
    return 0.5 * (jnp.tanh(0.5 * x) + 1.0)


def _layer_norm(z, g, b):
    mu = jnp.mean(z, axis=-1, keepdims=True)
    zc = z - mu
    var = jnp.mean(zc * zc, axis=-1, keepdims=True)
    return zc * lax.rsqrt(var + LN_EPS) * g + b


def _mix_out_kernel(x_ref, o0_ref, o1_ref, o2_ref, l0_ref, l1_ref, l2_ref, ob_ref, g_ref,
                    wpa_ref, wpb_ref, wo_ref, lng_ref, lnb_ref, rw_ref, rb_ref,
                    y_ref, ypk_ref, idx_ref, gate_ref, *scratch, d_model):
    def natural(ref, stages):
        dil, per, _ = ref.shape
        if dil == 1:
            return ref[0].astype(F32)
        for half, stage in enumerate(stages):
            for r in range(dil):
                stage[pl.ds(r, per, stride=dil), :] = ref[r, :, half * LANES:(half + 1) * LANES].astype(F32)
        return jnp.concatenate([stage[...] for stage in stages], axis=1)

    o_nat = natural(o0_ref, None), natural(o1_ref, scratch[0:2]), natural(o2_ref, scratch[2:4])
    l_nat = natural(l0_ref, None), natural(l1_ref, scratch[4:6]), natural(l2_ref, scratch[6:8])

    def chain(rows):
        (o0, o1, o2), (l0, l1, l2) = [a[rows] for a in o_nat], [a[rows] for a in l_nat]
        lm = jnp.maximum(jnp.maximum(l0, l1), l2)
        e0, e1, e2 = jnp.exp(l0 - lm), jnp.exp(l1 - lm), jnp.exp(l2 - lm)
        out_a = (e0 * o0 + e1 * o1 + e2 * o2) / (e0 + e1 + e2)
        pa = jnp.dot(out_a.astype(BF16), wpa_ref[...], preferred_element_type=F32)
        pb = jnp.dot(ob_ref[rows, :], wpb_ref[...], preferred_element_type=F32)
        ga = _sigmoid(g_ref[rows, :d_model].astype(F32))
        gb = _sigmoid(g_ref[rows, d_model:].astype(F32))
        merged = ga * pa + gb * pb
        mix = jnp.dot(merged.astype(BF16), wo_ref[...], preferred_element_type=F32)
        y = _layer_norm(DEEPNORM_ALPHA * x_ref[rows, :] + mix, lng_ref[...], lnb_ref[...])
        y_ref[rows, :] = y
        ypk_ref[rows, :] = _pack_bf16_pairs(y)

        logits = lax.dot_general(rw_ref[...], y.astype(BF16), _NT, preferred_element_type=F32) + rb_ref[...]
        expert = lax.broadcasted_iota(I32, logits.shape, 0)
        slot = lax.broadcasted_iota(I32, (SUBLANES, logits.shape[1]), 0)
        idx_out = jnp.zeros(slot.shape, I32)
        val_out = jnp.full(slot.shape, -jnp.inf, F32)
        for k in range(TOP_K):
            top = jnp.max(logits, axis=0, keepdims=True)
            top_idx = jnp.min(jnp.where(logits == top, expert, N_EXPERTS), axis=0, keepdims=True)
            idx_out = jnp.where(slot == k, top_idx, idx_out)
            val_out = jnp.where(slot == k, top, val_out)
            logits = jnp.where(expert == top_idx, -jnp.inf, logits)
        ev = jnp.exp(val_out - jnp.max(val_out, axis=0, keepdims=True))
        idx_ref[:, rows] = idx_out
        gate_ref[:, rows] = ev / jnp.sum(ev, axis=0, keepdims=True)

    chain(slice(0, x_ref.shape[0]))


def _mix_out(x2, o_g, lse_g, out_b, gates_h, wpa, wpb, wo, ln_g, ln_b, rw_t, rb_col, seq, tm=ROW_TILE):
    n, d = x2.shape
    tiles = seq // tm
    row = lambda i: (i, 0)
    const = lambda i: (0, 0)
    rb = lambda w: pl.BlockSpec((tm, w), row)
    full = lambda a: pl.BlockSpec(a.shape, const)
    dil_specs = [pl.BlockSpec((None, dil, tm // dil, DIL_OUT), lambda i: (i // tiles, 0, i % tiles, 0))
                 for _, dil in DIL_GROUPS]
    return pl.pallas_call(
        functools.partial(_mix_out_kernel, d_model=d),
        grid=(n // tm,),
        in_specs=[rb(d)] + dil_specs + dil_specs + [rb(SWA_Q_WIDTH), rb(2 * d),
                  full(wpa), full(wpb), full(wo), full(ln_g), full(ln_b), full(rw_t), full(rb_col)],
        out_specs=[rb(d), rb(d // 2), pl.BlockSpec((SUBLANES, tm), lambda i: (0, i)),
                   pl.BlockSpec((SUBLANES, tm), lambda i: (0, i))],
        out_shape=[jax.ShapeDtypeStruct((n, d), F32), jax.ShapeDtypeStruct((n, d // 2), jnp.uint32),
                   jax.ShapeDtypeStruct((SUBLANES, n), I32), jax.ShapeDtypeStruct((SUBLANES, n), F32)],
        scratch_shapes=[pltpu.VMEM((tm, LANES), F32)] * (4 * DIL_OUT // LANES),
        compiler_params=_cparams("parallel"),
        name="mix_out",
    )(x2, *o_g, *lse_g, out_b, gates_h, wpa, wpb, wo, ln_g, ln_b, rw_t, rb_col)


def _token_mixer_ln1(x2, w_in, sinks, w_proj_a, w_proj_b, w_out, ln1_g, ln1_b, router_w, router_b, bsz, seq):
    n, d = x2.shape
    heads = jnp.arange(1, N_ALIBI_HEADS + 1, dtype=F32)
    slopes = jnp.exp2(-8.0 * heads / N_ALIBI_HEADS)
    *a_g, hq, hkv, gates_h = _in_proj(x2, w_in.astype(BF16), bsz, seq)
    o_g, lse_g = [], []
    for g in range(N_DIL):
        sl = slopes[SWA_Q_HEADS + g * DIL_HEADS:SWA_Q_HEADS + (g + 1) * DIL_HEADS]
        o, lse = _dil_attention(a_g[g], sl, g)
        o_g.append(o)
        lse_g.append(lse)
    out_b = _swa_attention(hq, hkv, slopes[:SWA_Q_HEADS], sinks.astype(F32), bsz, seq)
    return _mix_out(x2, o_g, lse_g, out_b, gates_h, w_proj_a.astype(BF16), w_proj_b.astype(BF16),
                    w_out.astype(BF16), ln1_g.reshape(1, d).astype(F32), ln1_b.reshape(1, d).astype(F32),
                    router_w.T.astype(BF16), router_b.reshape(N_EXPERTS, 1).astype(F32), seq)


def _route_kernel(idx_ref, dest_ref, tbl_ref, carry_ref, pstart_ref, *, expert_block):
    phase, i = pl.program_id(0), pl.program_id(1)
    idx = idx_ref[...]
    t = idx.shape[1]
    expert = lax.broadcasted_iota(I32, (N_EXPERTS, t), 0)
    onehot = [expert == idx[k:k + 1, :] for k in range(TOP_K)]
    multi = sum(oh.astype(F32) for oh in onehot)
    tile_cnt = jnp.sum(multi, axis=1, keepdims=True).astype(I32)

    @pl.when((phase == 0) & (i == 0))
    def _():
        carry_ref[...] = jnp.zeros_like(carry_ref)

    @pl.when((phase == 1) & (i == 0))
    def _():
        counts = carry_ref[...]
        blocks = (counts + (expert_block - 1)) // expert_block
        er = lax.broadcasted_iota(I32, (N_EXPERTS, N_EXPERTS), 0)
        ec = lax.broadcasted_iota(I32, (N_EXPERTS, N_EXPERTS), 1)
        below = (ec < er).astype(BF16)
        prefix = lambda v: jnp.dot(below, v.astype(F32).astype(BF16), preferred_element_type=F32).astype(I32)
        pstart = (prefix(blocks >> 6) * 64 + prefix(blocks & 63)) * expert_block
        pstart_ref[...] = pstart
        carry_ref[...] = jnp.zeros_like(carry_ref)
        n_lanes = tbl_ref.shape[1]
        count_col, start_col, blocks_col = counts[:, 0:1], pstart[:, 0:1], blocks[:, 0:1]
        block_row0 = lax.broadcasted_iota(I32, (1, n_lanes), 1) * expert_block
        ends_before = (start_col + blocks_col * expert_block <= block_row0).astype(I32)
        block_e = jnp.minimum(jnp.sum(ends_before, axis=0, keepdims=True), N_EXPERTS - 1)
        mine = lax.broadcasted_iota(I32, (N_EXPERTS, n_lanes), 0) == block_e
        count_b = jnp.sum(jnp.where(mine, count_col, 0), axis=0, keepdims=True)
        start_b = jnp.sum(jnp.where(mine, start_col, 0), axis=0, keepdims=True)
        n_valid = jnp.clip(count_b - (block_row0 - start_b), 0, expert_block)
        n_used = jnp.sum(blocks_col, axis=0, keepdims=True)
        expert_ids = lax.broadcasted_iota(I32, (N_EXPERTS, n_lanes), 0)
        later_used = (expert_ids > block_e) & (blocks_col > 0)
        next_e = jnp.min(jnp.where(later_used, expert_ids, N_EXPERTS), axis=0, keepdims=True)
        next_e = jnp.where(next_e == N_EXPERTS, -1, next_e)
        row = lax.broadcasted_iota(I32, (SUBLANES, n_lanes), 0)
        tbl_ref[...] = jnp.where(row == 0, block_e,
                                 jnp.where(row == 1, n_valid, jnp.where(row == 2, n_used, next_e)))

    @pl.when(phase == 1)
    def _():
        r = lax.broadcasted_iota(I32, (t, t), 0)
        c = lax.broadcasted_iota(I32, (t, t), 1)
        earlier = (r < c).astype(BF16)
        cum = jnp.dot(multi.astype(BF16), earlier, preferred_element_type=F32).astype(I32)
        base = cum + carry_ref[:, 0:1] + pstart_ref[:, 0:1]
        slot = lax.broadcasted_iota(I32, (SUBLANES, t), 0)
        dest = jnp.zeros((SUBLANES, t), I32)
        for k in range(TOP_K):
            d_k = jnp.sum(jnp.where(onehot[k], base, 0), axis=0, keepdims=True)
            dest = jnp.where(slot == k, d_k, dest)
        dest_ref[...] = dest

    carry_ref[...] = carry_ref[...] + tile_cnt


def _route(idx_t, expert_block, n_blocks, tile=2 * ROW_TILE):
    n = idx_t.shape[1]
    tile = min(tile, n)
    per_expert = (N_EXPERTS, LANES)
    table = jax.ShapeDtypeStruct((SUBLANES, pl.cdiv(n_blocks, LANES) * LANES), I32)
    return pl.pallas_call(
        functools.partial(_route_kernel, expert_block=expert_block),
        grid=(2, n // tile),
        in_specs=[pl.BlockSpec((SUBLANES, tile), lambda p, i: (0, i))],
        out_specs=[pl.BlockSpec((SUBLANES, tile), lambda p, i: (0, i * p)),
                   pl.BlockSpec(table.shape, lambda p, i: (0, 0))],
        out_shape=[jax.ShapeDtypeStruct((SUBLANES, n), I32), table],
        scratch_shapes=[pltpu.VMEM(per_expert, I32), pltpu.VMEM(per_expert, I32)],
        compiler_params=_cparams("arbitrary", "arbitrary"),
        name="route",
    )(idx_t)


def _sc_dispatch(dest_t, y, rows, window=64):
    n, d = y.shape
    info = plsc.get_sparse_core_info()
    n_cores = info.num_cores
    per_worker = n // (n_cores * info.num_subcores)
    window = min(window, per_worker // 2)
    n_chunks = per_worker // window
    assert n == n_chunks * window * n_cores * info.num_subcores and n_chunks % 2 == 0
    mesh = plsc.VectorSubcoreMesh(core_axis_name="c", subcore_axis_name="s")
    dest_chunks = dest_t.reshape(dest_t.shape[0], n // window, window)

    @functools.partial(pl.kernel, mesh=mesh, out_type=jax.ShapeDtypeStruct((rows, d), y.dtype),
                       scratch_types=[pltpu.VMEM((TOP_K, n_chunks, window), I32)]
                       + [pltpu.VMEM((window, d), y.dtype)] * 2 + [pltpu.SemaphoreType.DMA] * 4,
                       name="sc_dispatch")
    def body(dest_hbm, y_hbm, xs_hbm, idx_v, buf0, buf1, lsem0, lsem1, ssem0, ssem1):
        bufs, lsem, ssem = (buf0, buf1), (lsem0, lsem1), (ssem0, ssem1)
        worker = lax.axis_index("s") * n_cores + lax.axis_index("c")
        base = worker * per_worker
        for k in range(TOP_K):
            pltpu.sync_copy(dest_hbm.at[k, pl.ds(worker * n_chunks, n_chunks)], idx_v.at[k])

        def load(c, b):
            return pltpu.make_async_copy(y_hbm.at[pl.ds(base + c * window, window)], bufs[b], lsem[b])

        def scatter(c, k, b):
            return pltpu.make_async_copy(bufs[b], xs_hbm.at[idx_v.at[k, c]], ssem[b])

        load(0, 0).start()

        @pl.loop(0, n_chunks, step=2)
        def _(c0):
            for b in range(2):
                c = c0 + b
                load(c, b).wait()

                @pl.when(c > 0)
                def _():
                    for k in range(TOP_K):
                        scatter(c - 1, k, 1 - b).wait()

                @pl.when(c + 1 < n_chunks)
                def _():
                    load(c + 1, 1 - b).start()

                for k in range(TOP_K):
                    scatter(c, k, b).start()

        for k in range(TOP_K):
            scatter(n_chunks - 1, k, 1).wait()

    return body(dest_chunks, y)


def _expert_kernel(be_ref, nu_ref, nv_ref, nx_ref, x_ref, wg_hbm, bg_ref, wu_hbm, bu_ref, wd_hbm, bd_ref, o_ref,
                   wg_bf, wu_bf, wd_bf, wg_st, wu_st, wd_st, w_sem):
    j = pl.program_id(0)
    used = j < nu_ref[0]
    new_expert = (j == 0) | (be_ref[j] != be_ref[jnp.maximum(j - 1, 0)])
    staged = ((wg_hbm, wg_st, wg_bf), (wu_hbm, wu_st, wu_bf), (wd_hbm, wd_st, wd_bf))

    def fetch(e):
        return [pltpu.make_async_copy(hbm.at[e], st, w_sem.at[i]) for i, (hbm, st, _) in enumerate(staged)]

    @pl.when(used & new_expert)
    def _():
        @pl.when(j == 0)
        def _():
            for copy in fetch(be_ref[0]):
                copy.start()

        for copy in fetch(be_ref[j]):
            copy.wait()
        chunk = 128
        for _, st, dst in staged:
            def cast_rows(i, carry, st=st, dst=dst):
                rows = pl.ds(pl.multiple_of(i * chunk, chunk), chunk)
                dst[rows, :] = st[rows, :].astype(BF16)
                return carry

            lax.fori_loop(0, st.shape[0] // chunk, cast_rows, 0)

        @pl.when(nx_ref[j] >= 0)
        def _():
            for copy in fetch(nx_ref[j]):
                copy.start()

    @pl.when(used)
    def _():
        row = lax.broadcasted_iota(I32, x_ref.shape, 0)
        x_lo, x_hi = _unpack_bf16_pairs(jnp.where(row < nv_ref[j], x_ref[...], jnp.uint32(0)))
        xb = jnp.concatenate([x_lo.astype(BF16), x_hi.astype(BF16)], axis=1)
        gt = jnp.dot(xb, wg_bf[...], preferred_element_type=F32) + bg_ref[...]
        up = jnp.dot(xb, wu_bf[...], preferred_element_type=F32) + bu_ref[...]
        gt = jnp.minimum(gt, SWIGLU_LIMIT)
        up = jnp.clip(up, -SWIGLU_LIMIT, SWIGLU_LIMIT)
        hdn = gt * _sigmoid(SWIGLU_ALPHA * gt) * (up + 1.0)
        o_ref[...] = _pack_bf16_pairs(jnp.dot(hdn.astype(BF16), wd_bf[...], preferred_element_type=F32)
                                      + bd_ref[...])

    @pl.when(jnp.logical_not(used))
    def _():
        o_ref[...] = jnp.zeros_like(o_ref)


def _experts(block_e, n_used, n_valid, next_e, xs, wg, bg, wu, bu, wd, bd, expert_block):
    rows, _ = xs.shape
    e, d, f = wg.shape
    xmap = lambda j, be, nu, nv, nx: (jnp.minimum(j, nu[0] - 1), 0)
    bmap = lambda j, be, nu, nv, nx: (be[j], 0, 0)
    weights = pl.BlockSpec(memory_space=pl.ANY)
    grid_spec = pltpu.PrefetchScalarGridSpec(
        num_scalar_prefetch=4,
        grid=(rows // expert_block,),
        in_specs=[pl.BlockSpec((expert_block, d // 2), xmap),
                  weights, pl.BlockSpec((None, 1, f), bmap),
                  weights, pl.BlockSpec((None, 1, f), bmap),
                  weights, pl.BlockSpec((None, 1, d), bmap)],
        out_specs=pl.BlockSpec((expert_block, d // 2), lambda j, be, nu, nv, nx: (j, 0)),
        scratch_shapes=[pltpu.VMEM((d, f), BF16), pltpu.VMEM((d, f), BF16), pltpu.VMEM((f, d), BF16),
                        pltpu.VMEM((d, f), wg.dtype), pltpu.VMEM((d, f), wu.dtype), pltpu.VMEM((f, d), wd.dtype),
                        pltpu.SemaphoreType.DMA((3,))],
    )
    return pl.pallas_call(
        _expert_kernel,
        grid_spec=grid_spec,
        out_shape=jax.ShapeDtypeStruct((rows, d // 2), jnp.uint32),
        compiler_params=_cparams("arbitrary"),
        name="experts",
    )(block_e, n_used, n_valid, next_e, xs, wg, bg.reshape(e, 1, f), wu, bu.reshape(e, 1, f), wd,
      bd.reshape(e, 1, d))


def _sc_gather(dest_t, ys, n, window=64):
    _, d = ys.shape
    info = plsc.get_sparse_core_info()
    n_cores = info.num_cores
    per_worker = n // (n_cores * info.num_subcores)
    mesh = plsc.VectorSubcoreMesh(core_axis_name="c", subcore_axis_name="s")
    window = min(window, per_worker)
    n_chunks = per_worker // window
    assert n == n_chunks * window * n_cores * info.num_subcores
    n_buf = 2

    @functools.partial(pl.kernel, mesh=mesh, out_type=jax.ShapeDtypeStruct((TOP_K, n, d), ys.dtype),
                       scratch_types=[pltpu.VMEM((TOP_K, per_worker), I32)]
                       + [pltpu.VMEM((window, d), ys.dtype)] * n_buf + [pltpu.SemaphoreType.DMA] * (2 * n_buf),
                       name="sc_gather")
    def body(dest_hbm, ys_hbm, out_hbm, idx_v, buf0, buf1, gsem0, gsem1, ssem0, ssem1):
        bufs, gsem, ssem = (buf0, buf1), (gsem0, gsem1), (ssem0, ssem1)
        base = (lax.axis_index("s") * n_cores + lax.axis_index("c")) * per_worker
        for k in range(TOP_K):
            pltpu.sync_copy(dest_hbm.at[k, pl.ds(base, per_worker)], idx_v.at[k])

        def gather(c, k):
            return pltpu.make_async_copy(ys_hbm.at[idx_v.at[k, pl.ds(c * window, window)]], bufs[k % n_buf],
                                         gsem[k % n_buf])

        def store(c, k):
            return pltpu.make_async_copy(bufs[k % n_buf], out_hbm.at[k, pl.ds(base + c * window, window)],
                                         ssem[k % n_buf])

        gather(0, 0).start()

        @pl.loop(0, n_chunks)
        def _(c):
            for k in range(TOP_K):
                if k > 0:
                    store(c, k - 1).wait()
                    if k + 1 < TOP_K:
                        gather(c, k + 1).start()
                    else:
                        @pl.when(c + 1 < n_chunks)
                        def _():
                            gather(c + 1, 0).start()
                else:
                    @pl.when(c > 0)
                    def _():
                        store(c - 1, TOP_K - 1).wait()
                    gather(c, 1).start()
                gather(c, k).wait()
                store(c, k).start()

        store(n_chunks - 1, TOP_K - 1).wait()

    return body(dest_t, ys)


def _combine_dense_kernel(gate_ref, y_ref, lng_ref, lnb_ref, g_ref, o_ref):
    gates = gate_ref[...].T
    ffn_lo = ffn_hi = None
    for k in range(TOP_K):
        lo, hi = _unpack_bf16_pairs(g_ref[k])
        gate = gates[:, k:k + 1]
        ffn_lo = gate * lo if k == 0 else ffn_lo + gate * lo
        ffn_hi = gate * hi if k == 0 else ffn_hi + gate * hi
    ffn = jnp.concatenate([ffn_lo, ffn_hi], axis=1)
    o_ref[...] = _layer_norm(DEEPNORM_ALPHA * y_ref[...] + ffn, lng_ref[...], lnb_ref[...])


def _combine_dense(gates, y, ln_g, ln_b, gathered, tile=2 * ROW_TILE):
    n, d = y.shape
    tile = min(tile, n)
    row = lambda i: (i, 0)
    const = lambda i: (0, 0)
    return pl.pallas_call(
        _combine_dense_kernel,
        grid=(n // tile,),
        in_specs=[pl.BlockSpec((SUBLANES, tile), lambda i: (0, i)), pl.BlockSpec((tile, d), row),
                  pl.BlockSpec((1, d), const), pl.BlockSpec((1, d), const),
                  pl.BlockSpec((TOP_K, tile, d // 2), lambda i: (0, i, 0))],
        out_specs=pl.BlockSpec((tile, d), row),
        out_shape=jax.ShapeDtypeStruct((n, d), F32),
        compiler_params=_cparams("parallel"),
        name="combine_dense",
    )(gates, y, ln_g, ln_b, gathered)


MOE_ROW_BLOCK = 512


def _moe_ln2(y, y_packed, idx, gates, w_gate, b_gate, w_up, b_up, w_down, b_down, ln2_g, ln2_b):
    n, d = y.shape
    n_blocks = n * TOP_K // MOE_ROW_BLOCK + N_EXPERTS
    rows = n_blocks * MOE_ROW_BLOCK
    dest_t, block_table = _route(idx, MOE_ROW_BLOCK, n_blocks)
    block_e, n_valid, n_used = block_table[0, :n_blocks], block_table[1, :n_blocks], block_table[2, :1]
    next_e = block_table[3, :n_blocks]
    xs = _sc_dispatch(dest_t, y_packed, rows)
    ys = _experts(block_e, n_used, n_valid, next_e, xs, w_gate, b_gate.astype(F32), w_up, b_up.astype(F32),
                  w_down, b_down.astype(F32), MOE_ROW_BLOCK)
    gathered = _sc_gather(dest_t, ys, n)
    return _combine_dense(gates, y, ln2_g.reshape(1, d).astype(F32), ln2_b.reshape(1, d).astype(F32), gathered)


def kernel(x, w_in, sinks, w_proj_a, w_proj_b, w_out, ln1_g, ln1_b, router_w, router_b,
           w_gate, b_gate, w_up, b_up, w_down, b_down, ln2_g, ln2_b):
    bsz, seq, d = x.shape
    assert seq % (DIL_GROUPS[-1][1] * ATTN_BLOCK) == 0 and seq % IN_PROJ_TILE == 0
    assert w_in.shape[-1] == GATE_OFF + 2 * d and w_proj_b.shape[-2] == SWA_Q_WIDTH
    assert router_w.shape[-1] == N_EXPERTS and (bsz * seq * TOP_K) % MOE_ROW_BLOCK == 0
    h = x.reshape(bsz * seq, d)
    for l in range(w_in.shape[0]):
        y, y_packed, idx, gates = _token_mixer_ln1(h, w_in[l], sinks[l], w_proj_a[l], w_proj_b[l], w_out[l],
                                                   ln1_g[l], ln1_b[l], router_w[l], router_b[l], bsz, seq)
        h = _moe_ln2(y, y_packed, idx, gates, w_gate[l], b_gate[l], w_up[l], b_up[l], w_down[l], b_down[l],
                     ln2_g[l], ln2_b[l])
    return h.reshape(bsz, seq, d)
```

```python
import functools

import jax
import jax.numpy as jnp
from jax import lax
from jax.experimental import pallas as pl
from jax.experimental.pallas import tpu as pltpu
from jax.experimental.pallas import tpu_sc as plsc

F32 = jnp.float32
BF16 = jnp.bfloat16
I32 = jnp.int32

HEAD_DIM = 64
DIL_GROUPS = ((128, 1), (512, 4), (2048, 16))
DIL_HEADS = 4
N_DIL = len(DIL_GROUPS)
DIL_OUT = DIL_HEADS * HEAD_DIM
DIL_WIDTH = N_DIL * DIL_OUT
SWA_Q_HEADS = 16
SWA_KV_HEADS = 2
SWA_REP = SWA_Q_HEADS // SWA_KV_HEADS
SWA_WINDOW = 128
SWA_Q_WIDTH = SWA_Q_HEADS * HEAD_DIM
SWA_KV_WIDTH = SWA_KV_HEADS * HEAD_DIM
N_ALIBI_HEADS = SWA_Q_HEADS + N_DIL * DIL_HEADS
ATTN_BLOCK = 128
N_EXPERTS = 32
TOP_K = 4
SWIGLU_LIMIT = 7.0
SWIGLU_ALPHA = 1.702
LN_EPS = 1e-5
DEPTH = 1
DEEPNORM_ALPHA = (2 * DEPTH) ** 0.25
NEG_INF = -1e30

LANES = 128
SUBLANES = 8
VMEM_LIMIT_BYTES = 56 * 1024 * 1024
ROW_TILE = 512
IN_PROJ_TILE = 2 * ROW_TILE
COL_CHUNK = 512
DIL_ATTN_ROWS = 4 * ROW_TILE

A_QKV_W = 3 * DIL_WIDTH
B_Q_OFF = A_QKV_W
B_KV_OFF = B_Q_OFF + SWA_Q_WIDTH
GATE_OFF = B_KV_OFF + 2 * SWA_KV_WIDTH


def _cparams(*sem):
    return pltpu.CompilerParams(dimension_semantics=sem, vmem_limit_bytes=VMEM_LIMIT_BYTES)


def _pack_bf16_pairs(x):
    c = x.shape[1] // 2
    lo = lax.bitcast_convert_type(x[:, :c].astype(BF16).astype(F32), jnp.uint32)
    hi = lax.bitcast_convert_type(x[:, c:].astype(BF16).astype(F32), jnp.uint32)
    return (lo >> 16) | hi


def _unpack_bf16_pairs(w):
    lo = lax.bitcast_convert_type(w << 16, F32)
    hi = lax.bitcast_convert_type(w & jnp.uint32(0xFFFF0000), F32)
    return lo, hi


def _in_proj_kernel(x_ref, w_ref, a0_ref, a1_ref, a2_ref, hq_ref, hkv_ref, g_ref, *scratch, d_model, tm):
    xb = x_ref[...].astype(BF16)
    segments = ((hq_ref, B_Q_OFF, SWA_Q_WIDTH), (hkv_ref, B_KV_OFF, 2 * SWA_KV_WIDTH),
                (g_ref, GATE_OFF, 2 * d_model))
    for out_ref, col0, width in segments:
        for c in range(0, width, COL_CHUNK):
            w = min(COL_CHUNK, width - c)
            r = jnp.dot(xb, w_ref[:, col0 + c:col0 + c + w], preferred_element_type=F32)
            out_ref[:, c:c + w] = r.astype(out_ref.dtype)
    for g, a_ref in enumerate((a0_ref, a1_ref, a2_ref)):
        dil = DIL_GROUPS[g][1]
        per = tm // dil
        for part in range(3):
            col0 = part * DIL_WIDTH + g * DIL_OUT
            res = jnp.dot(xb, w_ref[:, col0:col0 + DIL_OUT], preferred_element_type=F32)
            if dil == 1:
                a_ref[0, :, part * DIL_OUT:(part + 1) * DIL_OUT] = res.astype(a_ref.dtype)
                continue
            for half in range(DIL_OUT // LANES):
                stage = scratch[part * (DIL_OUT // LANES) + half]
                stage[...] = res[:, half * LANES:(half + 1) * LANES]
                c0 = part * DIL_OUT + half * LANES
                for r in range(dil):
                    a_ref[r, :, c0:c0 + LANES] = stage[pl.ds(r, per, stride=dil), :].astype(a_ref.dtype)


def _in_proj(x2, w_in_bf, bsz, seq, tm=IN_PROJ_TILE):
    n, d = x2.shape
    cols = w_in_bf.shape[1]
    tiles = seq // tm
    row = lambda i: (i, 0)
    dil_spec = lambda dil: pl.BlockSpec((None, dil, tm // dil, 3 * DIL_OUT), lambda i: (i // tiles, 0, i % tiles, 0))
    dil_shape = lambda dil: jax.ShapeDtypeStruct((bsz, dil, seq // dil, 3 * DIL_OUT), BF16)
    dils = [dil for _, dil in DIL_GROUPS]
    return pl.pallas_call(
        functools.partial(_in_proj_kernel, d_model=d, tm=tm),
        grid=(n // tm,),
        in_specs=[pl.BlockSpec((tm, d), row),
                  pl.BlockSpec((d, cols), lambda i: (0, 0), pipeline_mode=pl.Buffered(1))],
        out_specs=[dil_spec(dil) for dil in dils]
        + [pl.BlockSpec((tm, SWA_Q_WIDTH), row), pl.BlockSpec((tm, 2 * SWA_KV_WIDTH), row),
           pl.BlockSpec((tm, 2 * d), row)],
        out_shape=[dil_shape(dil) for dil in dils]
        + [jax.ShapeDtypeStruct((n, SWA_Q_WIDTH), BF16), jax.ShapeDtypeStruct((n, 2 * SWA_KV_WIDTH), BF16),
           jax.ShapeDtypeStruct((n, 2 * d), BF16)],
        scratch_shapes=[pltpu.VMEM((tm, LANES), F32)] * (3 * DIL_OUT // LANES),
        compiler_params=_cparams("parallel"),
        name="in_proj",
    )(x2, w_in_bf)


_NT = (((1,), (1,)), ((), ()))


def _per_head_column(ref, head0, n_heads, scale=1.0):
    head_of_row = lax.broadcasted_iota(I32, (n_heads * ATTN_BLOCK, 1), 0) // ATTN_BLOCK
    col = jnp.zeros((n_heads * ATTN_BLOCK, 1), F32)
    for h in range(n_heads):
        col = jnp.where(head_of_row == h, ref[head0 + h] * scale, col)
    return col


def _store_band_bias(bias_ref, slope_col, max_diff):
    rows = slope_col.shape[0]
    qi = lax.broadcasted_iota(I32, (rows, ATTN_BLOCK), 0) % ATTN_BLOCK
    kj = lax.broadcasted_iota(I32, (rows, ATTN_BLOCK), 1)
    diff_prev = qi - kj + ATTN_BLOCK
    diff_cur = qi - kj
    bias_ref[0] = jnp.where(diff_prev <= max_diff, -slope_col * diff_prev.astype(F32), NEG_INF)
    bias_ref[1] = jnp.where((diff_cur >= 0) & (diff_cur <= max_diff), -slope_col * diff_cur.astype(F32), NEG_INF)


def _band_probs(s_prev, s_cur, sink_col):
    m = jnp.max(jnp.maximum(s_prev, s_cur), axis=-1, keepdims=True)
    if sink_col is not None:
        m = jnp.maximum(m, sink_col)
    return jnp.exp(s_prev - m).astype(BF16), jnp.exp(s_cur - m).astype(BF16), m


def _pair_masks():
    low = lax.broadcasted_iota(I32, (ATTN_BLOCK, LANES), 1) < HEAD_DIM
    scale = HEAD_DIM ** -0.5
    return low, jnp.where(low, scale, 0.0).astype(BF16), jnp.where(low, 0.0, scale).astype(BF16)


def _values_and_ones(v):
    return jnp.concatenate([v, jnp.ones((v.shape[0], LANES), v.dtype)], axis=1)


def _dil_attn_kernel(slope_ref, q_ref, kc_ref, kp_ref, vc_ref, vp_ref, o_ref, lse_ref, bias_ref,
                     *, tq, max_diff, dist_scale):
    first = pl.program_id(2) == 0
    _store_band_bias(bias_ref, _per_head_column(slope_ref, 0, DIL_HEADS, dist_scale), max_diff)
    low, keep_low, keep_high = _pair_masks()
    pair_cols = [slice(p * LANES, (p + 1) * LANES) for p in range(DIL_HEADS // 2)]
    for res, i in [(res, i) for res in range(q_ref.shape[0]) for i in range(tq // ATTN_BLOCK)]:
        rows = slice(i * ATTN_BLOCK, (i + 1) * ATTN_BLOCK)
        window = slice((i - 1) * ATTN_BLOCK, (i + 1) * ATTN_BLOCK)

        def key_window(cur_ref, first_prev_ref, c):
            if i == 0:
                return jnp.concatenate([first_prev_ref[res, :, c], cur_ref[res, rows, c]], axis=0)
            return cur_ref[res, window, c]

        scores = []
        for c in pair_cols:
            k_win = key_window(kc_ref, kp_ref, c)
            for keep in (keep_low, keep_high):
                scores.append(lax.dot_general(q_ref[res, rows, c] * keep, k_win, _NT, preferred_element_type=F32))
        scores = jnp.concatenate(scores, axis=0)
        s_prev = scores[:, :ATTN_BLOCK] + bias_ref[0]
        s_cur = scores[:, ATTN_BLOCK:] + bias_ref[1]
        if i == 0:
            s_prev = jnp.where(first, NEG_INF, s_prev)
        p_prev, p_cur, m = _band_probs(s_prev, s_cur, None)
        probs = jnp.concatenate([p_prev, p_cur], axis=1)
        for p, c in enumerate(pair_cols):
            v_win = _values_and_ones(key_window(vc_ref, vp_ref, c))
            od, mh = [], []
            for h in (2 * p, 2 * p + 1):
                hr = slice(h * ATTN_BLOCK, (h + 1) * ATTN_BLOCK)
                od.append(jnp.dot(probs[hr], v_win, preferred_element_type=F32))
                mh.append(m[hr])
            denom = jnp.where(low, od[0][:, LANES:], od[1][:, LANES:])
            o_ref[res, rows, c] = (jnp.where(low, od[0][:, :LANES], od[1][:, :LANES]) / denom).astype(o_ref.dtype)
            lse_ref[res, rows, c] = jnp.where(low, mh[0], mh[1]) + jnp.log(denom)


def _dil_attention(a_g, slopes_g, g):
    window, dil = DIL_GROUPS[g]
    bsz, _, sub_len, _ = a_g.shape
    tq = min(DIL_ATTN_ROWS, sub_len)
    res_per_step = DIL_ATTN_ROWS // tq
    nqb = tq // ATTN_BLOCK
    cur = lambda part: (lambda b, r, m: (b, r, m, part))
    prev = lambda part: (lambda b, r, m: (b, r, jnp.maximum(m * nqb - 1, 0), part))
    blk = lambda rows, imap: pl.BlockSpec((None, res_per_step, rows, DIL_OUT), imap)
    return pl.pallas_call(
        functools.partial(_dil_attn_kernel, tq=tq, max_diff=window // dil, dist_scale=float(dil)),
        grid=(bsz, dil // res_per_step, sub_len // tq),
        in_specs=[pl.BlockSpec(memory_space=pltpu.SMEM),
                  blk(tq, cur(0)), blk(tq, cur(1)), blk(ATTN_BLOCK, prev(1)),
                  blk(tq, cur(2)), blk(ATTN_BLOCK, prev(2))],
        out_specs=[blk(tq, cur(0)), blk(tq, cur(0))],
        out_shape=[jax.ShapeDtypeStruct((bsz, dil, sub_len, DIL_OUT), BF16),
                   jax.ShapeDtypeStruct((bsz, dil, sub_len, DIL_OUT), F32)],
        scratch_shapes=[pltpu.VMEM((2, DIL_HEADS * ATTN_BLOCK, ATTN_BLOCK), F32)],
        compiler_params=_cparams("parallel", "parallel", "arbitrary"),
        name=f"dil_attn_g{g}",
    )(slopes_g, a_g, a_g, a_g, a_g, a_g)


def _swa_attn_kernel(slope_ref, sink_ref, q_ref, kvc_ref, kvp_ref, o_ref, bias_ref, *, tq):
    first = pl.program_id(1) == 0
    low, keep_low, keep_high = _pair_masks()

    def both_halves(ref, lane0):
        part = ref[:, lane0:lane0 + HEAD_DIM]
        return jnp.concatenate([part, part], axis=1)

    tile = (SWA_REP, ATTN_BLOCK, ATTN_BLOCK)
    stacked = (SWA_REP * ATTN_BLOCK, ATTN_BLOCK)
    qi = lax.broadcasted_iota(I32, (ATTN_BLOCK, ATTN_BLOCK), 0)
    kj = lax.broadcasted_iota(I32, (ATTN_BLOCK, ATTN_BLOCK), 1)
    from_prev = kj > qi
    diff = jnp.where(from_prev, qi - kj + ATTN_BLOCK, qi - kj).astype(F32)
    for kvh in range(SWA_KV_HEADS):
        head0 = kvh * SWA_REP
        for r in range(SWA_REP):
            bias_ref[r] = -slope_ref[head0 + r] * diff
        sink_col = _per_head_column(sink_ref, head0, SWA_REP)
        k_cur_all, k_first = both_halves(kvc_ref, kvh * HEAD_DIM), both_halves(kvp_ref, kvh * HEAD_DIM)
        v_lane0 = SWA_KV_WIDTH + kvh * HEAD_DIM
        v_cur_all, v_first = both_halves(kvc_ref, v_lane0), both_halves(kvp_ref, v_lane0)
        pair_cols = [slice((head0 + 2 * p) * HEAD_DIM, (head0 + 2 * p + 2) * HEAD_DIM) for p in range(SWA_REP // 2)]
        for i in range(tq // ATTN_BLOCK):
            rows = slice(i * ATTN_BLOCK, (i + 1) * ATTN_BLOCK)
            if i == 0:
                k_win = jnp.concatenate([k_first, k_cur_all[rows]], axis=0)
                v_win = jnp.concatenate([v_first, v_cur_all[rows]], axis=0)
            else:
                win = slice((i - 1) * ATTN_BLOCK, (i + 1) * ATTN_BLOCK)
                k_win, v_win = k_cur_all[win], v_cur_all[win]
            q = jnp.concatenate([q_ref[rows, c] * keep for c in pair_cols for keep in (keep_low, keep_high)], axis=0)
            s2 = lax.dot_general(q, k_win, _NT, preferred_element_type=F32)
            s = jnp.where(from_prev, s2[:, :ATTN_BLOCK].reshape(tile), s2[:, ATTN_BLOCK:].reshape(tile)) \
                + bias_ref[...]
            if i == 0:
                s = jnp.where(from_prev & first, NEG_INF, s)
            s = s.reshape(stacked)
            m = jnp.maximum(jnp.max(s, axis=-1, keepdims=True), sink_col)
            p = jnp.exp(s - m).reshape(tile)
            p2 = jnp.concatenate([jnp.where(from_prev, p, 0.0).reshape(stacked).astype(BF16),
                                  jnp.where(from_prev, 0.0, p).reshape(stacked).astype(BF16)], axis=1)
            od = jnp.dot(p2, _values_and_ones(v_win), preferred_element_type=F32)
            o = od[:, :LANES] / (od[:, LANES:] + jnp.exp(sink_col - m))
            for p, c in enumerate(pair_cols):
                even = o[2 * p * ATTN_BLOCK:(2 * p + 1) * ATTN_BLOCK]
                odd = o[(2 * p + 1) * ATTN_BLOCK:(2 * p + 2) * ATTN_BLOCK]
                o_ref[rows, c] = jnp.where(low, even, odd).astype(o_ref.dtype)


def _swa_attention(hq, hkv, slopes_b, sinks, bsz, seq, tq=2 * ROW_TILE):
    nqb = tq // ATTN_BLOCK
    hq3 = hq.reshape(bsz, seq, SWA_Q_WIDTH)
    hkv3 = hkv.reshape(bsz, seq, 2 * SWA_KV_WIDTH)
    smem = pl.BlockSpec(memory_space=pltpu.SMEM)
    out = pl.pallas_call(
        functools.partial(_swa_attn_kernel, tq=tq),
        grid=(bsz, seq // tq),
        in_specs=[smem, smem,
                  pl.BlockSpec((None, tq, SWA_Q_WIDTH), lambda b, m: (b, m, 0)),
                  pl.BlockSpec((None, tq, 2 * SWA_KV_WIDTH), lambda b, m: (b, m, 0)),
                  pl.BlockSpec((None, ATTN_BLOCK, 2 * SWA_KV_WIDTH),
                               lambda b, m: (b, jnp.maximum(m * nqb - 1, 0), 0))],
        out_specs=pl.BlockSpec((None, tq, SWA_Q_WIDTH), lambda b, m: (b, m, 0)),
        out_shape=jax.ShapeDtypeStruct((bsz, seq, SWA_Q_WIDTH), BF16),
        scratch_shapes=[pltpu.VMEM((SWA_REP, ATTN_BLOCK, ATTN_BLOCK), F32)],
        compiler_params=_cparams("parallel", "arbitrary"),
        name="swa_attn",
    )(slopes_b, sinks, hq3, hkv3, hkv3)
    return out.reshape(bsz * seq, SWA_Q_WIDTH)


def _sigmoid(x):
    return 0.5 * (jnp.tanh(0.5 * x) + 1.0)


def _layer_norm(z, g, b):
    mu = jnp.mean(z, axis=-1, keepdims=True)
    zc = z - mu
    var = jnp.mean(zc * zc, axis=-1, keepdims=True)
    return zc * lax.rsqrt(var + LN_EPS) * g + b


def _mix_out_kernel(x_ref, o0_ref, o1_ref, o2_ref, l0_ref, l1_ref, l2_ref, ob_ref, g_ref,
                    wpa_ref, wpb_ref, wo_ref, lng_ref, lnb_ref, rw_ref, rb_ref,
                    y_ref, ypk_ref, idx_ref, gate_ref, *scratch, d_model):
    def natural(ref, stages):
        dil, per, _ = ref.shape
        if dil == 1:
            return ref[0].astype(F32)
        for half, stage in enumerate(stages):
            for r in range(dil):
                stage[pl.ds(r, per, stride=dil), :] = ref[r, :, half * LANES:(half + 1) * LANES].astype(F32)
        return jnp.concatenate([stage[...] for stage in stages], axis=1)

    o_nat = natural(o0_ref, None), natural(o1_ref, scratch[0:2]), natural(o2_ref, scratch[2:4])
    l_nat = natural(l0_ref, None), natural(l1_ref, scratch[4:6]), natural(l2_ref, scratch[6:8])

    def chain(rows):
        (o0, o1, o2), (l0, l1, l2) = [a[rows] for a in o_nat], [a[rows] for a in l_nat]
        lm = jnp.maximum(jnp.maximum(l0, l1), l2)
        e0, e1, e2 = jnp.exp(l0 - lm), jnp.exp(l1 - lm), jnp.exp(l2 - lm)
        out_a = (e0 * o0 + e1 * o1 + e2 * o2) / (e0 + e1 + e2)
        pa = jnp.dot(out_a.astype(BF16), wpa_ref[...], preferred_element_type=F32)
        pb = jnp.dot(ob_ref[rows, :], wpb_ref[...], preferred_element_type=F32)
        ga = _sigmoid(g_ref[rows, :d_model].astype(F32))
        gb = _sigmoid(g_ref[rows, d_model:].astype(F32))
        merged = ga * pa + gb * pb
        mix = jnp.dot(merged.astype(BF16), wo_ref[...], preferred_element_type=F32)
        y = _layer_norm(DEEPNORM_ALPHA * x_ref[rows, :] + mix, lng_ref[...], lnb_ref[...])
        y_ref[rows, :] = y
        ypk_ref[rows, :] = _pack_bf16_pairs(y)

        logits = lax.dot_general(rw_ref[...], y.astype(BF16), _NT, preferred_element_type=F32) + rb_ref[...]
        expert = lax.broadcasted_iota(I32, logits.shape, 0)
        slot = lax.broadcasted_iota(I32, (SUBLANES, logits.shape[1]), 0)
        idx_out = jnp.zeros(slot.shape, I32)
        val_out = jnp.full(slot.shape, -jnp.inf, F32)
        for k in range(TOP_K):
            top = jnp.max(logits, axis=0, keepdims=True)
            top_idx = jnp.min(jnp.where(logits == top, expert, N_EXPERTS), axis=0, keepdims=True)
            idx_out = jnp.where(slot == k, top_idx, idx_out)
            val_out = jnp.where(slot == k, top, val_out)
            logits = jnp.where(expert == top_idx, -jnp.inf, logits)
        ev = jnp.exp(val_out - jnp.max(val_out, axis=0, keepdims=True))
        idx_ref[:, rows] = idx_out
        gate_ref[:, rows] = ev / jnp.sum(ev, axis=0, keepdims=True)

    chain(slice(0, x_ref.shape[0]))


def _mix_out(x2, o_g, lse_g, out_b, gates_h, wpa, wpb, wo, ln_g, ln_b, rw_t, rb_col, seq, tm=ROW_TILE):
    n, d = x2.shape
    tiles = seq // tm
    row = lambda i: (i, 0)
    const = lambda i: (0, 0)
    rb = lambda w: pl.BlockSpec((tm, w), row)
    full = lambda a: pl.BlockSpec(a.shape, const)
    dil_specs = [pl.BlockSpec((None, dil, tm // dil, DIL_OUT), lambda i: (i // tiles, 0, i % tiles, 0))
                 for _, dil in DIL_GROUPS]
    return pl.pallas_call(
        functools.partial(_mix_out_kernel, d_model=d),
        grid=(n // tm,),
        in_specs=[rb(d)] + dil_specs + dil_specs + [rb(SWA_Q_WIDTH), rb(2 * d),
                  full(wpa), full(wpb), full(wo), full(ln_g), full(ln_b), full(rw_t), full(rb_col)],
        out_specs=[rb(d), rb(d // 2), pl.BlockSpec((SUBLANES, tm), lambda i: (0, i)),
                   pl.BlockSpec((SUBLANES, tm), lambda i: (0, i))],
        out_shape=[jax.ShapeDtypeStruct((n, d), F32), jax.ShapeDtypeStruct((n, d // 2), jnp.uint32),
                   jax.ShapeDtypeStruct((SUBLANES, n), I32), jax.ShapeDtypeStruct((SUBLANES, n), F32)],
        scratch_shapes=[pltpu.VMEM((tm, LANES), F32)] * (4 * DIL_OUT // LANES),
        compiler_params=_cparams("parallel"),
        name="mix_out",
    )(x2, *o_g, *lse_g, out_b, gates_h, wpa, wpb, wo, ln_g, ln_b, rw_t, rb_col)


def _token_mixer_ln1(x2, w_in, sinks, w_proj_a, w_proj_b, w_out, ln1_g, ln1_b, router_w, router_b, bsz, seq):
    n, d = x2.shape
    heads = jnp.arange(1, N_ALIBI_HEADS + 1, dtype=F32)
    slopes = jnp.exp2(-8.0 * heads / N_ALIBI_HEADS)
    *a_g, hq, hkv, gates_h = _in_proj(x2, w_in.astype(BF16), bsz, seq)
    o_g, lse_g = [], []
    for g in range(N_DIL):
        sl = slopes[SWA_Q_HEADS + g * DIL_HEADS:SWA_Q_HEADS + (g + 1) * DIL_HEADS]
        o, lse = _dil_attention(a_g[g], sl, g)
        o_g.append(o)
        lse_g.append(lse)
    out_b = _swa_attention(hq, hkv, slopes[:SWA_Q_HEADS], sinks.astype(F32), bsz, seq)
    return _mix_out(x2, o_g, lse_g, out_b, gates_h, w_proj_a.astype(BF16), w_proj_b.astype(BF16),
                    w_out.astype(BF16), ln1_g.reshape(1, d).astype(F32), ln1_b.reshape(1, d).astype(F32),
                    router_w.T.astype(BF16), router_b.reshape(N_EXPERTS, 1).astype(F32), seq)


def _route_kernel(idx_ref, dest_ref, tbl_ref, carry_ref, pstart_ref, *, expert_block):
    phase, i = pl.program_id(0), pl.program_id(1)
    idx = idx_ref[...]
    t = idx.shape[1]
    expert = lax.broadcasted_iota(I32, (N_EXPERTS, t), 0)
    onehot = [expert == idx[k:k + 1, :] for k in range(TOP_K)]
    multi = sum(oh.astype(F32) for oh in onehot)
    tile_cnt = jnp.sum(multi, axis=1, keepdims=True).astype(I32)

    @pl.when((phase == 0) & (i == 0))
    def _():
        carry_ref[...] = jnp.zeros_like(carry_ref)

    @pl.when((phase == 1) & (i == 0))
    def _():
        counts = carry_ref[...]
        blocks = (counts + (expert_block - 1)) // expert_block
        er = lax.broadcasted_iota(I32, (N_EXPERTS, N_EXPERTS), 0)
        ec = lax.broadcasted_iota(I32, (N_EXPERTS, N_EXPERTS), 1)
        below = (ec < er).astype(BF16)
        prefix = lambda v: jnp.dot(below, v.astype(F32).astype(BF16), preferred_element_type=F32).astype(I32)
        pstart = (prefix(blocks >> 6) * 64 + prefix(blocks & 63)) * expert_block
        pstart_ref[...] = pstart
        carry_ref[...] = jnp.zeros_like(carry_ref)
        n_lanes = tbl_ref.shape[1]
        count_col, start_col, blocks_col = counts[:, 0:1], pstart[:, 0:1], blocks[:, 0:1]
        block_row0 = lax.broadcasted_iota(I32, (1, n_lanes), 1) * expert_block
        ends_before = (start_col + blocks_col * expert_block <= block_row0).astype(I32)
        block_e = jnp.minimum(jnp.sum(ends_before, axis=0, keepdims=True), N_EXPERTS - 1)
        mine = lax.broadcasted_iota(I32, (N_EXPERTS, n_lanes), 0) == block_e
        count_b = jnp.sum(jnp.where(mine, count_col, 0), axis=0, keepdims=True)
        start_b = jnp.sum(jnp.where(mine, start_col, 0), axis=0, keepdims=True)
        n_valid = jnp.clip(count_b - (block_row0 - start_b), 0, expert_block)
        n_used = jnp.sum(blocks_col, axis=0, keepdims=True)
        expert_ids = lax.broadcasted_iota(I32, (N_EXPERTS, n_lanes), 0)
        later_used = (expert_ids > block_e) & (blocks_col > 0)
        next_e = jnp.min(jnp.where(later_used, expert_ids, N_EXPERTS), axis=0, keepdims=True)
        next_e = jnp.where(next_e == N_EXPERTS, -1, next_e)
        row = lax.broadcasted_iota(I32, (SUBLANES, n_lanes), 0)
        tbl_ref[...] = jnp.where(row == 0, block_e,
                                 jnp.where(row == 1, n_valid, jnp.where(row == 2, n_used, next_e)))

    @pl.when(phase == 1)
    def _():
        r = lax.broadcasted_iota(I32, (t, t), 0)
        c = lax.broadcasted_iota(I32, (t, t), 1)
        earlier = (r < c).astype(BF16)
        cum = jnp.dot(multi.astype(BF16), earlier, preferred_element_type=F32).astype(I32)
        base = cum + carry_ref[:, 0:1] + pstart_ref[:, 0:1]
        slot = lax.broadcasted_iota(I32, (SUBLANES, t), 0)
        dest = jnp.zeros((SUBLANES, t), I32)
        for k in range(TOP_K):
            d_k = jnp.sum(jnp.where(onehot[k], base, 0), axis=0, keepdims=True)
            dest = jnp.where(slot == k, d_k, dest)
        dest_ref[...] = dest

    carry_ref[...] = carry_ref[...] + tile_cnt


def _route(idx_t, expert_block, n_blocks, tile=2 * ROW_TILE):
    n = idx_t.shape[1]
    tile = min(tile, n)
    per_expert = (N_EXPERTS, LANES)
    table = jax.ShapeDtypeStruct((SUBLANES, pl.cdiv(n_blocks, LANES) * LANES), I32)
    return pl.pallas_call(
        functools.partial(_route_kernel, expert_block=expert_block),
        grid=(2, n // tile),
        in_specs=[pl.BlockSpec((SUBLANES, tile), lambda p, i: (0, i))],
        out_specs=[pl.BlockSpec((SUBLANES, tile), lambda p, i: (0, i * p)),
                   pl.BlockSpec(table.shape, lambda p, i: (0, 0))],
        out_shape=[jax.ShapeDtypeStruct((SUBLANES, n), I32), table],
        scratch_shapes=[pltpu.VMEM(per_expert, I32), pltpu.VMEM(per_expert, I32)],
        compiler_params=_cparams("arbitrary", "arbitrary"),
        name="route",
    )(idx_t)


def _sc_dispatch(dest_t, y, rows, window=64):
    n, d = y.shape
    info = plsc.get_sparse_core_info()
    n_cores = info.num_cores
    per_worker = n // (n_cores * info.num_subcores)
    window = min(window, per_worker // 2)
    n_chunks = per_worker // window
    assert n == n_chunks * window * n_cores * info.num_subcores and n_chunks % 2 == 0
    mesh = plsc.VectorSubcoreMesh(core_axis_name="c", subcore_axis_name="s")
    dest_chunks = dest_t.reshape(dest_t.shape[0], n // window, window)

    @functools.partial(pl.kernel, mesh=mesh, out_type=jax.ShapeDtypeStruct((rows, d), y.dtype),
                       scratch_types=[pltpu.VMEM((TOP_K, n_chunks, window), I32)]
                       + [pltpu.VMEM((window, d), y.dtype)] * 2 + [pltpu.SemaphoreType.DMA] * 4,
                       name="sc_dispatch")
    def body(dest_hbm, y_hbm, xs_hbm, idx_v, buf0, buf1, lsem0, lsem1, ssem0, ssem1):
        bufs, lsem, ssem = (buf0, buf1), (lsem0, lsem1), (ssem0, ssem1)
        worker = lax.axis_index("s") * n_cores + lax.axis_index("c")
        base = worker * per_worker
        for k in range(TOP_K):
            pltpu.sync_copy(dest_hbm.at[k, pl.ds(worker * n_chunks, n_chunks)], idx_v.at[k])

        def load(c, b):
            return pltpu.make_async_copy(y_hbm.at[pl.ds(base + c * window, window)], bufs[b], lsem[b])

        def scatter(c, k, b):
            return pltpu.make_async_copy(bufs[b], xs_hbm.at[idx_v.at[k, c]], ssem[b])

        load(0, 0).start()

        @pl.loop(0, n_chunks, step=2)
        def _(c0):
            for b in range(2):
                c = c0 + b
                load(c, b).wait()

                @pl.when(c > 0)
                def _():
                    for k in range(TOP_K):
                        scatter(c - 1, k, 1 - b).wait()

                @pl.when(c + 1 < n_chunks)
                def _():
                    load(c + 1, 1 - b).start()

                for k in range(TOP_K):
                    scatter(c, k, b).start()

        for k in range(TOP_K):
            scatter(n_chunks - 1, k, 1).wait()

    return body(dest_chunks, y)


def _expert_kernel(be_ref, nu_ref, nv_ref, nx_ref, x_ref, wg_hbm, bg_ref, wu_hbm, bu_ref, wd_hbm, bd_ref, o_ref,
                   wg_bf, wu_bf, wd_bf, wg_st, wu_st, wd_st, w_sem):
    j = pl.program_id(0)
    used = j < nu_ref[0]
    new_expert = (j == 0) | (be_ref[j] != be_ref[jnp.maximum(j - 1, 0)])
    staged = ((wg_hbm, wg_st, wg_bf), (wu_hbm, wu_st, wu_bf), (wd_hbm, wd_st, wd_bf))

    def fetch(e):
        return [pltpu.make_async_copy(hbm.at[e], st, w_sem.at[i]) for i, (hbm, st, _) in enumerate(staged)]

    @pl.when(used & new_expert)
    def _():
        @pl.when(j == 0)
        def _():
            for copy in fetch(be_ref[0]):
                copy.start()

        for copy in fetch(be_ref[j]):
            copy.wait()
        chunk = 128
        for _, st, dst in staged:
            def cast_rows(i, carry, st=st, dst=dst):
                rows = pl.ds(pl.multiple_of(i * chunk, chunk), chunk)
                dst[rows, :] = st[rows, :].astype(BF16)
                return carry

            lax.fori_loop(0, st.shape[0] // chunk, cast_rows, 0)

        @pl.when(nx_ref[j] >= 0)
        def _():
            for copy in fetch(nx_ref[j]):
                copy.start()

    @pl.when(used)
    def _():
        row = lax.broadcasted_iota(I32, x_ref.shape, 0)
        x_lo, x_hi = _unpack_bf16_pairs(jnp.where(row < nv_ref[j], x_ref[...], jnp.uint32(0)))
        xb = jnp.concatenate([x_lo.astype(BF16), x_hi.astype(BF16)], axis=1)
        gt = jnp.dot(xb, wg_bf[...], preferred_element_type=F32) + bg_ref[...]
        up = jnp.dot(xb, wu_bf[...], preferred_element_type=F32) + bu_ref[...]
        gt = jnp.minimum(gt, SWIGLU_LIMIT)
        up = jnp.clip(up, -SWIGLU_LIMIT, SWIGLU_LIMIT)
        hdn = gt * _sigmoid(SWIGLU_ALPHA * gt) * (up + 1.0)
        o_ref[...] = _pack_bf16_pairs(jnp.dot(hdn.astype(BF16), wd_bf[...], preferred_element_type=F32)
                                      + bd_ref[...])

    @pl.when(jnp.logical_not(used))
    def _():
        o_ref[...] = jnp.zeros_like(o_ref)


def _experts(block_e, n_used, n_valid, next_e, xs, wg, bg, wu, bu, wd, bd, expert_block):
    rows, _ = xs.shape
    e, d, f = wg.shape
    xmap = lambda j, be, nu, nv, nx: (jnp.minimum(j, nu[0] - 1), 0)
    bmap = lambda j, be, nu, nv, nx: (be[j], 0, 0)
    weights = pl.BlockSpec(memory_space=pl.ANY)
    grid_spec = pltpu.PrefetchScalarGridSpec(
        num_scalar_prefetch=4,
        grid=(rows // expert_block,),
        in_specs=[pl.BlockSpec((expert_block, d // 2), xmap),
                  weights, pl.BlockSpec((None, 1, f), bmap),
                  weights, pl.BlockSpec((None, 1, f), bmap),
                  weights, pl.BlockSpec((None, 1, d), bmap)],
        out_specs=pl.BlockSpec((expert_block, d // 2), lambda j, be, nu, nv, nx: (j, 0)),
        scratch_shapes=[pltpu.VMEM((d, f), BF16), pltpu.VMEM((d, f), BF16), pltpu.VMEM((f, d), BF16),
                        pltpu.VMEM((d, f), wg.dtype), pltpu.VMEM((d, f), wu.dtype), pltpu.VMEM((f, d), wd.dtype),
                        pltpu.SemaphoreType.DMA((3,))],
    )
    return pl.pallas_call(
        _expert_kernel,
        grid_spec=grid_spec,
        out_shape=jax.ShapeDtypeStruct((rows, d // 2), jnp.uint32),
        compiler_params=_cparams("arbitrary"),
        name="experts",
    )(block_e, n_used, n_valid, next_e, xs, wg, bg.reshape(e, 1, f), wu, bu.reshape(e, 1, f), wd,
      bd.reshape(e, 1, d))


def _sc_gather(dest_t, ys, n, window=64):
    _, d = ys.shape
    info = plsc.get_sparse_core_info()
    n_cores = info.num_cores
    per_worker = n // (n_cores * info.num_subcores)
    mesh = plsc.VectorSubcoreMesh(core_axis_name="c", subcore_axis_name="s")
    window = min(window, per_worker)
    n_chunks = per_worker // window
    assert n == n_chunks * window * n_cores * info.num_subcores
    n_buf = 2

    @functools.partial(pl.kernel, mesh=mesh, out_type=jax.ShapeDtypeStruct((TOP_K, n, d), ys.dtype),
                       scratch_types=[pltpu.VMEM((TOP_K, per_worker), I32)]
                       + [pltpu.VMEM((window, d), ys.dtype)] * n_buf + [pltpu.SemaphoreType.DMA] * (2 * n_buf),
                       name="sc_gather")
    def body(dest_hbm, ys_hbm, out_hbm, idx_v, buf0, buf1, gsem0, gsem1, ssem0, ssem1):
        bufs, gsem, ssem = (buf0, buf1), (gsem0, gsem1), (ssem0, ssem1)
        base = (lax.axis_index("s") * n_cores + lax.axis_index("c")) * per_worker
        for k in range(TOP_K):
            pltpu.sync_copy(dest_hbm.at[k, pl.ds(base, per_worker)], idx_v.at[k])

        def gather(c, k):
            return pltpu.make_async_copy(ys_hbm.at[idx_v.at[k, pl.ds(c * window, window)]], bufs[k % n_buf],
                                         gsem[k % n_buf])

        def store(c, k):
            return pltpu.make_async_copy(bufs[k % n_buf], out_hbm.at[k, pl.ds(base + c * window, window)],
                                         ssem[k % n_buf])

        gather(0, 0).start()

        @pl.loop(0, n_chunks)
        def _(c):
            for k in range(TOP_K):
                if k > 0:
                    store(c, k - 1).wait()
                    if k + 1 < TOP_K:
                        gather(c, k + 1).start()
                    else:
                        @pl.when(c + 1 < n_chunks)
                        def _():
                            gather(c + 1, 0).start()
                else:
                    @pl.when(c > 0)
                    def _():
                        store(c - 1, TOP_K - 1).wait()
                    gather(c, 1).start()
                gather(c, k).wait()
                store(c, k).start()

        store(n_chunks - 1, TOP_K - 1).wait()

    return body(dest_t, ys)


def _combine_dense_kernel(gate_ref, y_ref, lng_ref, lnb_ref, g_ref, o_ref):
    gates = gate_ref[...].T
    ffn_lo = ffn_hi = None
    for k in range(TOP_K):
        lo, hi = _unpack_bf16_pairs(g_ref[k])
        gate = gates[:, k:k + 1]
        ffn_lo = gate * lo if k == 0 else ffn_lo + gate * lo
        ffn_hi = gate * hi if k == 0 else ffn_hi + gate * hi
    ffn = jnp.concatenate([ffn_lo, ffn_hi], axis=1)
    o_ref[...] = _layer_norm(DEEPNORM_ALPHA * y_ref[...] + ffn, lng_ref[...], lnb_ref[...])


def _combine_dense(gates, y, ln_g, ln_b, gathered, tile=2 * ROW_TILE):
    n, d = y.shape
    tile = min(tile, n)
    row = lambda i: (i, 0)
    const = lambda i: (0, 0)
    return pl.pallas_call(
        _combine_dense_kernel,
        grid=(n // tile,),
        in_specs=[pl.BlockSpec((SUBLANES, tile), lambda i: (0, i)), pl.BlockSpec((tile, d), row),
                  pl.BlockSpec((1, d), const), pl.BlockSpec((1, d), const),
                  pl.BlockSpec((TOP_K, tile, d // 2), lambda i: (0, i, 0))],
        out_specs=pl.BlockSpec((tile, d), row),
        out_shape=jax.ShapeDtypeStruct((n, d), F32),
        compiler_params=_cparams("parallel"),
        name="combine_dense",
    )(gates, y, ln_g, ln_b, gathered)


MOE_ROW_BLOCK = 512


def _moe_ln2(y, y_packed, idx, gates, w_gate, b_gate, w_up, b_up, w_down, b_down, ln2_g, ln2_b):
    n, d = y.shape
    n_blocks = n * TOP_K // MOE_ROW_BLOCK + N_EXPERTS
    rows = n_blocks * MOE_ROW_BLOCK
    dest_t, block_table = _route(idx, MOE_ROW_BLOCK, n_blocks)
    block_e, n_valid, n_used = block_table[0, :n_blocks], block_table[1, :n_blocks], block_table[2, :1]
    next_e = block_table[3, :n_blocks]
    xs = _sc_dispatch(dest_t, y_packed, rows)
    ys = _experts(block_e, n_used, n_valid, next_e, xs, w_gate, b_gate.astype(F32), w_up, b_up.astype(F32),
                  w_down, b_down.astype(F32), MOE_ROW_BLOCK)
    gathered = _sc_gather(dest_t, ys, n)
    return _combine_dense(gates, y, ln2_g.reshape(1, d).astype(F32), ln2_b.reshape(1, d).astype(F32), gathered)


def kernel(x, w_in, sinks, w_proj_a, w_proj_b, w_out, ln1_g, ln1_b, router_w, router_b,
           w_gate, b_gate, w_up, b_up, w_down, b_down, ln2_g, ln2_b):
    bsz, seq, d = x.shape
    assert seq % (DIL_GROUPS[-1][1] * ATTN_BLOCK) == 0 and seq % IN_PROJ_TILE == 0
    assert w_in.shape[-1] == GATE_OFF + 2 * d and w_proj_b.shape[-2] == SWA_Q_WIDTH
    assert router_w.shape[-1] == N_EXPERTS and (bsz * seq * TOP_K) % MOE_ROW_BLOCK == 0
    h = x.reshape(bsz * seq, d)
    for l in range(w_in.shape[0]):
        y, y_packed, idx, gates = _token_mixer_ln1(h, w_in[l], sinks[l], w_proj_a[l], w_proj_b[l], w_out[l],
                                                   ln1_g[l], ln1_b[l], router_w[l], router_b[l], bsz, seq)
        h = _moe_ln2(y, y_packed, idx, gates, w_gate[l], b_gate[l], w_up[l], b_up[l], w_down[l], b_down[l],
                     ln2_g[l], ln2_b[l])
    return h.reshape(bsz, seq, d)
```

```python
import functools

import jax
import jax.numpy as jnp
from jax import lax
from jax.experimental import pallas as pl
from jax.experimental.pallas import tpu as pltpu
from jax.experimental.pallas import tpu_sc as plsc

F32 = jnp.float32
BF16 = jnp.bfloat16
I32 = jnp.int32

HEAD_DIM = 64
DIL_GROUPS = ((128, 1), (512, 4), (2048, 16))
DIL_HEADS = 4
N_DIL = len(DIL_GROUPS)
DIL_OUT = DIL_HEADS * HEAD_DIM
DIL_WIDTH = N_DIL * DIL_OUT
SWA_Q_HEADS = 16
SWA_KV_HEADS = 2
SWA_REP = SWA_Q_HEADS // SWA_KV_HEADS
SWA_WINDOW = 128
SWA_Q_WIDTH = SWA_Q_HEADS * HEAD_DIM
SWA_KV_WIDTH = SWA_KV_HEADS * HEAD_DIM
N_ALIBI_HEADS = SWA_Q_HEADS + N_DIL * DIL_HEADS
ATTN_BLOCK = 128
N_EXPERTS = 32
TOP_K = 4
SWIGLU_LIMIT = 7.0
SWIGLU_ALPHA = 1.702
LN_EPS = 1e-5
DEPTH = 1
DEEPNORM_ALPHA = (2 * DEPTH) ** 0.25
NEG_INF = -1e30

LANES = 128
SUBLANES = 8
VMEM_LIMIT_BYTES = 56 * 1024 * 1024
ROW_TILE = 512
IN_PROJ_TILE = 2 * ROW_TILE
COL_CHUNK = 512
DIL_ATTN_ROWS = 4 * ROW_TILE

A_QKV_W = 3 * DIL_WIDTH
B_Q_OFF = A_QKV_W
B_KV_OFF = B_Q_OFF + SWA_Q_WIDTH
GATE_OFF = B_KV_OFF + 2 * SWA_KV_WIDTH


def _cparams(*sem):
    return pltpu.CompilerParams(dimension_semantics=sem, vmem_limit_bytes=VMEM_LIMIT_BYTES)


def _pack_bf16_pairs(x):
    c = x.shape[1] // 2
    lo = lax.bitcast_convert_type(x[:, :c].astype(BF16).astype(F32), jnp.uint32)
    hi = lax.bitcast_convert_type(x[:, c:].astype(BF16).astype(F32), jnp.uint32)
    return (lo >> 16) | hi


def _unpack_bf16_pairs(w):
    lo = lax.bitcast_convert_type(w << 16, F32)
    hi = lax.bitcast_convert_type(w & jnp.uint32(0xFFFF0000), F32)
    return lo, hi


def _in_proj_kernel(x_ref, w_ref, a0_ref, a1_ref, a2_ref, hq_ref, hkv_ref, g_ref, *scratch, d_model, tm):
    xb = x_ref[...].astype(BF16)
    segments = ((hq_ref, B_Q_OFF, SWA_Q_WIDTH), (hkv_ref, B_KV_OFF, 2 * SWA_KV_WIDTH),
                (g_ref, GATE_OFF, 2 * d_model))
    for out_ref, col0, width in segments:
        for c in range(0, width, COL_CHUNK):
            w = min(COL_CHUNK, width - c)
            r = jnp.dot(xb, w_ref[:, col0 + c:col0 + c + w], preferred_element_type=F32)
            out_ref[:, c:c + w] = r.astype(out_ref.dtype)
    for g, a_ref in enumerate((a0_ref, a1_ref, a2_ref)):
        dil = DIL_GROUPS[g][1]
        per = tm // dil
        for part in range(3):
            col0 = part * DIL_WIDTH + g * DIL_OUT
            res = jnp.dot(xb, w_ref[:, col0:col0 + DIL_OUT], preferred_element_type=F32)
            if dil == 1:
                a_ref[0, :, part * DIL_OUT:(part + 1) * DIL_OUT] = res.astype(a_ref.dtype)
                continue
            for half in range(DIL_OUT // LANES):
                stage = scratch[part * (DIL_OUT // LANES) + half]
                stage[...] = res[:, half * LANES:(half + 1) * LANES]
                c0 = part * DIL_OUT + half * LANES
                for r in range(dil):
                    a_ref[r, :, c0:c0 + LANES] = stage[pl.ds(r, per, stride=dil), :].astype(a_ref.dtype)


def _in_proj(x2, w_in_bf, bsz, seq, tm=IN_PROJ_TILE):
    n, d = x2.shape
    cols = w_in_bf.shape[1]
    tiles = seq // tm
    row = lambda i: (i, 0)
    dil_spec = lambda dil: pl.BlockSpec((None, dil, tm // dil, 3 * DIL_OUT), lambda i: (i // tiles, 0, i % tiles, 0))
    dil_shape = lambda dil: jax.ShapeDtypeStruct((bsz, dil, seq // dil, 3 * DIL_OUT), BF16)
    dils = [dil for _, dil in DIL_GROUPS]
    return pl.pallas_call(
        functools.partial(_in_proj_kernel, d_model=d, tm=tm),
        grid=(n // tm,),
        in_specs=[pl.BlockSpec((tm, d), row),
                  pl.BlockSpec((d, cols), lambda i: (0, 0), pipeline_mode=pl.Buffered(1))],
        out_specs=[dil_spec(dil) for dil in dils]
        + [pl.BlockSpec((tm, SWA_Q_WIDTH), row), pl.BlockSpec((tm, 2 * SWA_KV_WIDTH), row),
           pl.BlockSpec((tm, 2 * d), row)],
        out_shape=[dil_shape(dil) for dil in dils]
        + [jax.ShapeDtypeStruct((n, SWA_Q_WIDTH), BF16), jax.ShapeDtypeStruct((n, 2 * SWA_KV_WIDTH), BF16),
           jax.ShapeDtypeStruct((n, 2 * d), BF16)],
        scratch_shapes=[pltpu.VMEM((tm, LANES), F32)] * (3 * DIL_OUT // LANES),
        compiler_params=_cparams("parallel"),
        name="in_proj",
    )(x2, w_in_bf)


_NT = (((1,), (1,)), ((), ()))


def _per_head_column(ref, head0, n_heads, scale=1.0):
    head_of_row = lax.broadcasted_iota(I32, (n_heads * ATTN_BLOCK, 1), 0) // ATTN_BLOCK
    col = jnp.zeros((n_heads * ATTN_BLOCK, 1), F32)
    for h in range(n_heads):
        col = jnp.where(head_of_row == h, ref[head0 + h] * scale, col)
    return col


def _store_band_bias(bias_ref, slope_col, max_diff):
    rows = slope_col.shape[0]
    qi = lax.broadcasted_iota(I32, (rows, ATTN_BLOCK), 0) % ATTN_BLOCK
    kj = lax.broadcasted_iota(I32, (rows, ATTN_BLOCK), 1)
    diff_prev = qi - kj + ATTN_BLOCK
    diff_cur = qi - kj
    bias_ref[0] = jnp.where(diff_prev <= max_diff, -slope_col * diff_prev.astype(F32), NEG_INF)
    bias_ref[1] = jnp.where((diff_cur >= 0) & (diff_cur <= max_diff), -slope_col * diff_cur.astype(F32), NEG_INF)


def _band_probs(s_prev, s_cur, sink_col):
    m = jnp.max(jnp.maximum(s_prev, s_cur), axis=-1, keepdims=True)
    if sink_col is not None:
        m = jnp.maximum(m, sink_col)
    return jnp.exp(s_prev - m).astype(BF16), jnp.exp(s_cur - m).astype(BF16), m


def _pair_masks():
    low = lax.broadcasted_iota(I32, (ATTN_BLOCK, LANES), 1) < HEAD_DIM
    scale = HEAD_DIM ** -0.5
    return low, jnp.where(low, scale, 0.0).astype(BF16), jnp.where(low, 0.0, scale).astype(BF16)


def _values_and_ones(v):
    return jnp.concatenate([v, jnp.ones((v.shape[0], LANES), v.dtype)], axis=1)


def _dil_attn_kernel(slope_ref, q_ref, kc_ref, kp_ref, vc_ref, vp_ref, o_ref, lse_ref, bias_ref,
                     *, tq, max_diff, dist_scale):
    first = pl.program_id(2) == 0
    _store_band_bias(bias_ref, _per_head_column(slope_ref, 0, DIL_HEADS, dist_scale), max_diff)
    low, keep_low, keep_high = _pair_masks()
    pair_cols = [slice(p * LANES, (p + 1) * LANES) for p in range(DIL_HEADS // 2)]
    for res, i in [(res, i) for res in range(q_ref.shape[0]) for i in range(tq // ATTN_BLOCK)]:
        rows = slice(i * ATTN_BLOCK, (i + 1) * ATTN_BLOCK)
        window = slice((i - 1) * ATTN_BLOCK, (i + 1) * ATTN_BLOCK)

        def key_window(cur_ref, first_prev_ref, c):
            if i == 0:
                return jnp.concatenate([first_prev_ref[res, :, c], cur_ref[res, rows, c]], axis=0)
            return cur_ref[res, window, c]

        scores = []
        for c in pair_cols:
            k_win = key_window(kc_ref, kp_ref, c)
            for keep in (keep_low, keep_high):
                scores.append(lax.dot_general(q_ref[res, rows, c] * keep, k_win, _NT, preferred_element_type=F32))
        scores = jnp.concatenate(scores, axis=0)
        s_prev = scores[:, :ATTN_BLOCK] + bias_ref[0]
        s_cur = scores[:, ATTN_BLOCK:] + bias_ref[1]
        if i == 0:
            s_prev = jnp.where(first, NEG_INF, s_prev)
        p_prev, p_cur, m = _band_probs(s_prev, s_cur, None)
        probs = jnp.concatenate([p_prev, p_cur], axis=1)
        for p, c in enumerate(pair_cols):
            v_win = _values_and_ones(key_window(vc_ref, vp_ref, c))
            od, mh = [], []
            for h in (2 * p, 2 * p + 1):
                hr = slice(h * ATTN_BLOCK, (h + 1) * ATTN_BLOCK)
                od.append(jnp.dot(probs[hr], v_win, preferred_element_type=F32))
                mh.append(m[hr])
            denom = jnp.where(low, od[0][:, LANES:], od[1][:, LANES:])
            o_ref[res, rows, c] = (jnp.where(low, od[0][:, :LANES], od[1][:, :LANES]) / denom).astype(o_ref.dtype)
            lse_ref[res, rows, c] = jnp.where(low, mh[0], mh[1]) + jnp.log(denom)


def _dil_attention(a_g, slopes_g, g):
    window, dil = DIL_GROUPS[g]
    bsz, _, sub_len, _ = a_g.shape
    tq = min(DIL_ATTN_ROWS, sub_len)
    res_per_step = DIL_ATTN_ROWS // tq
    nqb = tq // ATTN_BLOCK
    cur = lambda part: (lambda b, r, m: (b, r, m, part))
    prev = lambda part: (lambda b, r, m: (b, r, jnp.maximum(m * nqb - 1, 0), part))
    blk = lambda rows, imap: pl.BlockSpec((None, res_per_step, rows, DIL_OUT), imap)
    return pl.pallas_call(
        functools.partial(_dil_attn_kernel, tq=tq, max_diff=window // dil, dist_scale=float(dil)),
        grid=(bsz, dil // res_per_step, sub_len // tq),
        in_specs=[pl.BlockSpec(memory_space=pltpu.SMEM),
                  blk(tq, cur(0)), blk(tq, cur(1)), blk(ATTN_BLOCK, prev(1)),
                  blk(tq, cur(2)), blk(ATTN_BLOCK, prev(2))],
        out_specs=[blk(tq, cur(0)), blk(tq, cur(0))],
        out_shape=[jax.ShapeDtypeStruct((bsz, dil, sub_len, DIL_OUT), BF16),
                   jax.ShapeDtypeStruct((bsz, dil, sub_len, DIL_OUT), F32)],
        scratch_shapes=[pltpu.VMEM((2, DIL_HEADS * ATTN_BLOCK, ATTN_BLOCK), F32)],
        compiler_params=_cparams("parallel", "parallel", "arbitrary"),
        name=f"dil_attn_g{g}",
    )(slopes_g, a_g, a_g, a_g, a_g, a_g)


def _swa_attn_kernel(slope_ref, sink_ref, q_ref, kvc_ref, kvp_ref, o_ref, bias_ref, *, tq):
    first = pl.program_id(1) == 0
    low, keep_low, keep_high = _pair_masks()

    def both_halves(ref, lane0):
        part = ref[:, lane0:lane0 + HEAD_DIM]
        return jnp.concatenate([part, part], axis=1)

    tile = (SWA_REP, ATTN_BLOCK, ATTN_BLOCK)
    stacked = (SWA_REP * ATTN_BLOCK, ATTN_BLOCK)
    qi = lax.broadcasted_iota(I32, (ATTN_BLOCK, ATTN_BLOCK), 0)
    kj = lax.broadcasted_iota(I32, (ATTN_BLOCK, ATTN_BLOCK), 1)
    from_prev = kj > qi
    diff = jnp.where(from_prev, qi - kj + ATTN_BLOCK, qi - kj).astype(F32)
    for kvh in range(SWA_KV_HEADS):
        head0 = kvh * SWA_REP
        for r in range(SWA_REP):
            bias_ref[r] = -slope_ref[head0 + r] * diff
        sink_col = _per_head_column(sink_ref, head0, SWA_REP)
        k_cur_all, k_first = both_halves(kvc_ref, kvh * HEAD_DIM), both_halves(kvp_ref, kvh * HEAD_DIM)
        v_lane0 = SWA_KV_WIDTH + kvh * HEAD_DIM
        v_cur_all, v_first = both_halves(kvc_ref, v_lane0), both_halves(kvp_ref, v_lane0)
        pair_cols = [slice((head0 + 2 * p) * HEAD_DIM, (head0 + 2 * p + 2) * HEAD_DIM) for p in range(SWA_REP // 2)]
        for i in range(tq // ATTN_BLOCK):
            rows = slice(i * ATTN_BLOCK, (i + 1) * ATTN_BLOCK)
            if i == 0:
                k_win = jnp.concatenate([k_first, k_cur_all[rows]], axis=0)
                v_win = jnp.concatenate([v_first, v_cur_all[rows]], axis=0)
            else:
                win = slice((i - 1) * ATTN_BLOCK, (i + 1) * ATTN_BLOCK)
                k_win, v_win = k_cur_all[win], v_cur_all[win]
            q = jnp.concatenate([q_ref[rows, c] * keep for c in pair_cols for keep in (keep_low, keep_high)], axis=0)
            s2 = lax.dot_general(q, k_win, _NT, preferred_element_type=F32)
            s = jnp.where(from_prev, s2[:, :ATTN_BLOCK].reshape(tile), s2[:, ATTN_BLOCK:].reshape(tile)) \
                + bias_ref[...]
            if i == 0:
                s = jnp.where(from_prev & first, NEG_INF, s)
            s = s.reshape(stacked)
            m = jnp.maximum(jnp.max(s, axis=-1, keepdims=True), sink_col)
            p = jnp.exp(s - m).reshape(tile)
            p2 = jnp.concatenate([jnp.where(from_prev, p, 0.0).reshape(stacked).astype(BF16),
                                  jnp.where(from_prev, 0.0, p).reshape(stacked).astype(BF16)], axis=1)
            od = jnp.dot(p2, _values_and_ones(v_win), preferred_element_type=F32)
            o = od[:, :LANES] / (od[:, LANES:] + jnp.exp(sink_col - m))
            for p, c in enumerate(pair_cols):
                even = o[2 * p * ATTN_BLOCK:(2 * p + 1) * ATTN_BLOCK]
                odd = o[(2 * p + 1) * ATTN_BLOCK:(2 * p + 2) * ATTN_BLOCK]
                o_ref[rows, c] = jnp.where(low, even, odd).astype(o_ref.dtype)


def _swa_attention(hq, hkv, slopes_b, sinks, bsz, seq, tq=2 * ROW_TILE):
    nqb = tq // ATTN_BLOCK
    hq3 = hq.reshape(bsz, seq, SWA_Q_WIDTH)
    hkv3 = hkv.reshape(bsz, seq, 2 * SWA_KV_WIDTH)
    smem = pl.BlockSpec(memory_space=pltpu.SMEM)
    out = pl.pallas_call(
        functools.partial(_swa_attn_kernel, tq=tq),
        grid=(bsz, seq // tq),
        in_specs=[smem, smem,
                  pl.BlockSpec((None, tq, SWA_Q_WIDTH), lambda b, m: (b, m, 0)),
                  pl.BlockSpec((None, tq, 2 * SWA_KV_WIDTH), lambda b, m: (b, m, 0)),
                  pl.BlockSpec((None, ATTN_BLOCK, 2 * SWA_KV_WIDTH),
                               lambda b, m: (b, jnp.maximum(m * nqb - 1, 0), 0))],
        out_specs=pl.BlockSpec((None, tq, SWA_Q_WIDTH), lambda b, m: (b, m, 0)),
        out_shape=jax.ShapeDtypeStruct((bsz, seq, SWA_Q_WIDTH), BF16),
        scratch_shapes=[pltpu.VMEM((SWA_REP, ATTN_BLOCK, ATTN_BLOCK), F32)],
        compiler_params=_cparams("parallel", "arbitrary"),
        name="swa_attn",
    )(slopes_b, sinks, hq3, hkv3, hkv3)
    return out.reshape(bsz * seq, SWA_Q_WIDTH)


def _sigmoid(x):
    return 0.5 * (jnp.tanh(0.5 * x) + 1.0)


def _layer_norm(z, g, b):
    mu = jnp.mean(z, axis=-1, keepdims=True)
    zc = z - mu
    var = jnp.mean(zc * zc, axis=-1, keepdims=True)
    return zc * lax.rsqrt(var + LN_EPS) * g + b


def _mix_out_kernel(x_ref, o0_ref, o1_ref, o2_ref, l0_ref, l1_ref, l2_ref, ob_ref, g_ref,
                    wpa_ref, wpb_ref, wo_ref, lng_ref, lnb_ref, rw_ref, rb_ref,
                    y_ref, ypk_ref, idx_ref, gate_ref, *scratch, d_model):
    def natural(ref, stages):
        dil, per, _ = ref.shape
        if dil == 1:
            return ref[0].astype(F32)
        for half, stage in enumerate(stages):
            for r in range(dil):
                stage[pl.ds(r, per, stride=dil), :] = ref[r, :, half * LANES:(half + 1) * LANES].astype(F32)
        return jnp.concatenate([stage[...] for stage in stages], axis=1)

    o_nat = natural(o0_ref, None), natural(o1_ref, scratch[0:2]), natural(o2_ref, scratch[2:4])
    l_nat = natural(l0_ref, None), natural(l1_ref, scratch[4:6]), natural(l2_ref, scratch[6:8])

    def chain(rows):
        (o0, o1, o2), (l0, l1, l2) = [a[rows] for a in o_nat], [a[rows] for a in l_nat]
        lm = jnp.maximum(jnp.maximum(l0, l1), l2)
        e0, e1, e2 = jnp.exp(l0 - lm), jnp.exp(l1 - lm), jnp.exp(l2 - lm)
        out_a = (e0 * o0 + e1 * o1 + e2 * o2) / (e0 + e1 + e2)
        pa = jnp.dot(out_a.astype(BF16), wpa_ref[...], preferred_element_type=F32)
        pb = jnp.dot(ob_ref[rows, :], wpb_ref[...], preferred_element_type=F32)
        ga = _sigmoid(g_ref[rows, :d_model].astype(F32))
        gb = _sigmoid(g_ref[rows, d_model:].astype(F32))
        merged = ga * pa + gb * pb
        mix = jnp.dot(merged.astype(BF16), wo_ref[...], preferred_element_type=F32)
        y = _layer_norm(DEEPNORM_ALPHA * x_ref[rows, :] + mix, lng_ref[...], lnb_ref[...])
        y_ref[rows, :] = y
        ypk_ref[rows, :] = _pack_bf16_pairs(y)

        logits = lax.dot_general(rw_ref[...], y.astype(BF16), _NT, preferred_element_type=F32) + rb_ref[...]
        expert = lax.broadcasted_iota(I32, logits.shape, 0)
        slot = lax.broadcasted_iota(I32, (SUBLANES, logits.shape[1]), 0)
        idx_out = jnp.zeros(slot.shape, I32)
        val_out = jnp.full(slot.shape, -jnp.inf, F32)
        for k in range(TOP_K):
            top = jnp.max(logits, axis=0, keepdims=True)
            top_idx = jnp.min(jnp.where(logits == top, expert, N_EXPERTS), axis=0, keepdims=True)
            idx_out = jnp.where(slot == k, top_idx, idx_out)
            val_out = jnp.where(slot == k, top, val_out)
            logits = jnp.where(expert == top_idx, -jnp.inf, logits)
        ev = jnp.exp(val_out - jnp.max(val_out, axis=0, keepdims=True))
        idx_ref[:, rows] = idx_out
        gate_ref[:, rows] = ev / jnp.sum(ev, axis=0, keepdims=True)

    chain(slice(0, x_ref.shape[0]))


def _mix_out(x2, o_g, lse_g, out_b, gates_h, wpa, wpb, wo, ln_g, ln_b, rw_t, rb_col, seq, tm=ROW_TILE):
    n, d = x2.shape
    tiles = seq // tm
    row = lambda i: (i, 0)
    const = lambda i: (0, 0)
    rb = lambda w: pl.BlockSpec((tm, w), row)
    full = lambda a: pl.BlockSpec(a.shape, const)
    dil_specs = [pl.BlockSpec((None, dil, tm // dil, DIL_OUT), lambda i: (i // tiles, 0, i % tiles, 0))
                 for _, dil in DIL_GROUPS]
    return pl.pallas_call(
        functools.partial(_mix_out_kernel, d_model=d),
        grid=(n // tm,),
        in_specs=[rb(d)] + dil_specs + dil_specs + [rb(SWA_Q_WIDTH), rb(2 * d),
                  full(wpa), full(wpb), full(wo), full(ln_g), full(ln_b), full(rw_t), full(rb_col)],
        out_specs=[rb(d), rb(d // 2), pl.BlockSpec((SUBLANES, tm), lambda i: (0, i)),
                   pl.BlockSpec((SUBLANES, tm), lambda i: (0, i))],
        out_shape=[jax.ShapeDtypeStruct((n, d), F32), jax.ShapeDtypeStruct((n, d // 2), jnp.uint32),
                   jax.ShapeDtypeStruct((SUBLANES, n), I32), jax.ShapeDtypeStruct((SUBLANES, n), F32)],
        scratch_shapes=[pltpu.VMEM((tm, LANES), F32)] * (4 * DIL_OUT // LANES),
        compiler_params=_cparams("parallel"),
        name="mix_out",
    )(x2, *o_g, *lse_g, out_b, gates_h, wpa, wpb, wo, ln_g, ln_b, rw_t, rb_col)


def _token_mixer_ln1(x2, w_in, sinks, w_proj_a, w_proj_b, w_out, ln1_g, ln1_b, router_w, router_b, bsz, seq):
    n, d = x2.shape
    heads = jnp.arange(1, N_ALIBI_HEADS + 1, dtype=F32)
    slopes = jnp.exp2(-8.0 * heads / N_ALIBI_HEADS)
    *a_g, hq, hkv, gates_h = _in_proj(x2, w_in.astype(BF16), bsz, seq)
    o_g, lse_g = [], []
    for g in range(N_DIL):
        sl = slopes[SWA_Q_HEADS + g * DIL_HEADS:SWA_Q_HEADS + (g + 1) * DIL_HEADS]
        o, lse = _dil_attention(a_g[g], sl, g)
        o_g.append(o)
        lse_g.append(lse)
    out_b = _swa_attention(hq, hkv, slopes[:SWA_Q_HEADS], sinks.astype(F32), bsz, seq)
    return _mix_out(x2, o_g, lse_g, out_b, gates_h, w_proj_a.astype(BF16), w_proj_b.astype(BF16),
                    w_out.astype(BF16), ln1_g.reshape(1, d).astype(F32), ln1_b.reshape(1, d).astype(F32),
                    router_w.T.astype(BF16), router_b.reshape(N_EXPERTS, 1).astype(F32), seq)


def _route_kernel(idx_ref, dest_ref, tbl_ref, carry_ref, pstart_ref, *, expert_block):
    phase, i = pl.program_id(0), pl.program_id(1)
    idx = idx_ref[...]
    t = idx.shape[1]
    expert = lax.broadcasted_iota(I32, (N_EXPERTS, t), 0)
    onehot = [expert == idx[k:k + 1, :] for k in range(TOP_K)]
    multi = sum(oh.astype(F32) for oh in onehot)
    tile_cnt = jnp.sum(multi, axis=1, keepdims=True).astype(I32)

    @pl.when((phase == 0) & (i == 0))
    def _():
        carry_ref[...] = jnp.zeros_like(carry_ref)

    @pl.when((phase == 1) & (i == 0))
    def _():
        counts = carry_ref[...]
        blocks = (counts + (expert_block - 1)) // expert_block
        er = lax.broadcasted_iota(I32, (N_EXPERTS, N_EXPERTS), 0)
        ec = lax.broadcasted_iota(I32, (N_EXPERTS, N_EXPERTS), 1)
        below = (ec < er).astype(BF16)
        prefix = lambda v: jnp.dot(below, v.astype(F32).astype(BF16), preferred_element_type=F32).astype(I32)
        pstart = (prefix(blocks >> 6) * 64 + prefix(blocks & 63)) * expert_block
        pstart_ref[...] = pstart
        carry_ref[...] = jnp.zeros_like(carry_ref)
        n_lanes = tbl_ref.shape[1]
        count_col, start_col, blocks_col = counts[:, 0:1], pstart[:, 0:1], blocks[:, 0:1]
        block_row0 = lax.broadcasted_iota(I32, (1, n_lanes), 1) * expert_block
        ends_before = (start_col + blocks_col * expert_block <= block_row0).astype(I32)
        block_e = jnp.minimum(jnp.sum(ends_before, axis=0, keepdims=True), N_EXPERTS - 1)
        mine = lax.broadcasted_iota(I32, (N_EXPERTS, n_lanes), 0) == block_e
        count_b = jnp.sum(jnp.where(mine, count_col, 0), axis=0, keepdims=True)
        start_b = jnp.sum(jnp.where(mine, start_col, 0), axis=0, keepdims=True)
        n_valid = jnp.clip(count_b - (block_row0 - start_b), 0, expert_block)
        n_used = jnp.sum(blocks_col, axis=0, keepdims=True)
        expert_ids = lax.broadcasted_iota(I32, (N_EXPERTS, n_lanes), 0)
        later_used = (expert_ids > block_e) & (blocks_col > 0)
        next_e = jnp.min(jnp.where(later_used, expert_ids, N_EXPERTS), axis=0, keepdims=True)
        next_e = jnp.where(next_e == N_EXPERTS, -1, next_e)
        row = lax.broadcasted_iota(I32, (SUBLANES, n_lanes), 0)
        tbl_ref[...] = jnp.where(row == 0, block_e,
                                 jnp.where(row == 1, n_valid, jnp.where(row == 2, n_used, next_e)))

    @pl.when(phase == 1)
    def _():
        r = lax.broadcasted_iota(I32, (t, t), 0)
        c = lax.broadcasted_iota(I32, (t, t), 1)
        earlier = (r < c).astype(BF16)
        cum = jnp.dot(multi.astype(BF16), earlier, preferred_element_type=F32).astype(I32)
        base = cum + carry_ref[:, 0:1] + pstart_ref[:, 0:1]
        slot = lax.broadcasted_iota(I32, (SUBLANES, t), 0)
        dest = jnp.zeros((SUBLANES, t), I32)
        for k in range(TOP_K):
            d_k = jnp.sum(jnp.where(onehot[k], base, 0), axis=0, keepdims=True)
            dest = jnp.where(slot == k, d_k, dest)
        dest_ref[...] = dest

    carry_ref[...] = carry_ref[...] + tile_cnt


def _route(idx_t, expert_block, n_blocks, tile=2 * ROW_TILE):
    n = idx_t.shape[1]
    tile = min(tile, n)
    per_expert = (N_EXPERTS, LANES)
    table = jax.ShapeDtypeStruct((SUBLANES, pl.cdiv(n_blocks, LANES) * LANES), I32)
    return pl.pallas_call(
        functools.partial(_route_kernel, expert_block=expert_block),
        grid=(2, n // tile),
        in_specs=[pl.BlockSpec((SUBLANES, tile), lambda p, i: (0, i))],
        out_specs=[pl.BlockSpec((SUBLANES, tile), lambda p, i: (0, i * p)),
                   pl.BlockSpec(table.shape, lambda p, i: (0, 0))],
        out_shape=[jax.ShapeDtypeStruct((SUBLANES, n), I32), table],
        scratch_shapes=[pltpu.VMEM(per_expert, I32), pltpu.VMEM(per_expert, I32)],
        compiler_params=_cparams("arbitrary", "arbitrary"),
        name="route",
    )(idx_t)


def _sc_dispatch(dest_t, y, rows, window=64):
    n, d = y.shape
    info = plsc.get_sparse_core_info()
    n_cores = info.num_cores
    per_worker = n // (n_cores * info.num_subcores)
    window = min(window, per_worker // 2)
    n_chunks = per_worker // window
    assert n == n_chunks * window * n_cores * info.num_subcores and n_chunks % 2 == 0
    mesh = plsc.VectorSubcoreMesh(core_axis_name="c", subcore_axis_name="s")
    dest_chunks = dest_t.reshape(dest_t.shape[0], n // window, window)

    @functools.partial(pl.kernel, mesh=mesh, out_type=jax.ShapeDtypeStruct((rows, d), y.dtype),
                       scratch_types=[pltpu.VMEM((TOP_K, n_chunks, window), I32)]
                       + [pltpu.VMEM((window, d), y.dtype)] * 2 + [pltpu.SemaphoreType.DMA] * 4,
                       name="sc_dispatch")
    def body(dest_hbm, y_hbm, xs_hbm, idx_v, buf0, buf1, lsem0, lsem1, ssem0, ssem1):
        bufs, lsem, ssem = (buf0, buf1), (lsem0, lsem1), (ssem0, ssem1)
        worker = lax.axis_index("s") * n_cores + lax.axis_index("c")
        base = worker * per_worker
        for k in range(TOP_K):
            pltpu.sync_copy(dest_hbm.at[k, pl.ds(worker * n_chunks, n_chunks)], idx_v.at[k])

        def load(c, b):
            return pltpu.make_async_copy(y_hbm.at[pl.ds(base + c * window, window)], bufs[b], lsem[b])

        def scatter(c, k, b):
            return pltpu.make_async_copy(bufs[b], xs_hbm.at[idx_v.at[k, c]], ssem[b])

        load(0, 0).start()

        @pl.loop(0, n_chunks, step=2)
        def _(c0):
            for b in range(2):
                c = c0 + b
                load(c, b).wait()

                @pl.when(c > 0)
                def _():
                    for k in range(TOP_K):
                        scatter(c - 1, k, 1 - b).wait()

                @pl.when(c + 1 < n_chunks)
                def _():
                    load(c + 1, 1 - b).start()

                for k in range(TOP_K):
                    scatter(c, k, b).start()

        for k in range(TOP_K):
            scatter(n_chunks - 1, k, 1).wait()

    return body(dest_chunks, y)


def _expert_kernel(be_ref, nu_ref, nv_ref, nx_ref, x_ref, wg_hbm, bg_ref, wu_hbm, bu_ref, wd_hbm, bd_ref, o_ref,
                   wg_bf, wu_bf, wd_bf, wg_st, wu_st, wd_st, w_sem):
    j = pl.program_id(0)
    used = j < nu_ref[0]
    new_expert = (j == 0) | (be_ref[j] != be_ref[jnp.maximum(j - 1, 0)])
    staged = ((wg_hbm, wg_st, wg_bf), (wu_hbm, wu_st, wu_bf), (wd_hbm, wd_st, wd_bf))

    def fetch(e):
        return [pltpu.make_async_copy(hbm.at[e], st, w_sem.at[i]) for i, (hbm, st, _) in enumerate(staged)]

    @pl.when(used & new_expert)
    def _():
        @pl.when(j == 0)
        def _():
            for copy in fetch(be_ref[0]):
                copy.start()

        for copy in fetch(be_ref[j]):
            copy.wait()
        chunk = 128
        for _, st, dst in staged:
            def cast_rows(i, carry, st=st, dst=dst):
                rows = pl.ds(pl.multiple_of(i * chunk, chunk), chunk)
                dst[rows, :] = st[rows, :].astype(BF16)
                return carry

            lax.fori_loop(0, st.shape[0] // chunk, cast_rows, 0)

        @pl.when(nx_ref[j] >= 0)
        def _():
            for copy in fetch(nx_ref[j]):
                copy.start(priority=1)

    @pl.when(used)
    def _():
        row = lax.broadcasted_iota(I32, x_ref.shape, 0)
        x_lo, x_hi = _unpack_bf16_pairs(jnp.where(row < nv_ref[j], x_ref[...], jnp.uint32(0)))
        xb = jnp.concatenate([x_lo.astype(BF16), x_hi.astype(BF16)], axis=1)
        gt = jnp.dot(xb, wg_bf[...], preferred_element_type=F32) + bg_ref[...]
        up = jnp.dot(xb, wu_bf[...], preferred_element_type=F32) + bu_ref[...]
        gt = jnp.minimum(gt, SWIGLU_LIMIT)
        up = jnp.clip(up, -SWIGLU_LIMIT, SWIGLU_LIMIT)
        hdn = gt * _sigmoid(SWIGLU_ALPHA * gt) * (up + 1.0)
        o_ref[...] = _pack_bf16_pairs(jnp.dot(hdn.astype(BF16), wd_bf[...], preferred_element_type=F32)
                                      + bd_ref[...])

    @pl.when(jnp.logical_not(used))
    def _():
        o_ref[...] = jnp.zeros_like(o_ref)


def _experts(block_e, n_used, n_valid, next_e, xs, wg, bg, wu, bu, wd, bd, expert_block):
    rows, _ = xs.shape
    e, d, f = wg.shape
    xmap = lambda j, be, nu, nv, nx: (jnp.minimum(j, nu[0] - 1), 0)
    bmap = lambda j, be, nu, nv, nx: (be[j], 0, 0)
    weights = pl.BlockSpec(memory_space=pl.ANY)
    grid_spec = pltpu.PrefetchScalarGridSpec(
        num_scalar_prefetch=4,
        grid=(rows // expert_block,),
        in_specs=[pl.BlockSpec((expert_block, d // 2), xmap),
                  weights, pl.BlockSpec((None, 1, f), bmap),
                  weights, pl.BlockSpec((None, 1, f), bmap),
                  weights, pl.BlockSpec((None, 1, d), bmap)],
        out_specs=pl.BlockSpec((expert_block, d // 2), lambda j, be, nu, nv, nx: (j, 0)),
        scratch_shapes=[pltpu.VMEM((d, f), BF16), pltpu.VMEM((d, f), BF16), pltpu.VMEM((f, d), BF16),
                        pltpu.VMEM((d, f), wg.dtype), pltpu.VMEM((d, f), wu.dtype), pltpu.VMEM((f, d), wd.dtype),
                        pltpu.SemaphoreType.DMA((3,))],
    )
    return pl.pallas_call(
        _expert_kernel,
        grid_spec=grid_spec,
        out_shape=jax.ShapeDtypeStruct((rows, d // 2), jnp.uint32),
        compiler_params=_cparams("arbitrary"),
        name="experts",
    )(block_e, n_used, n_valid, next_e, xs, wg, bg.reshape(e, 1, f), wu, bu.reshape(e, 1, f), wd,
      bd.reshape(e, 1, d))


def _sc_gather(dest_t, ys, n, window=64):
    _, d = ys.shape
    info = plsc.get_sparse_core_info()
    n_cores = info.num_cores
    per_worker = n // (n_cores * info.num_subcores)
    mesh = plsc.VectorSubcoreMesh(core_axis_name="c", subcore_axis_name="s")
    window = min(window, per_worker)
    n_chunks = per_worker // window
    assert n == n_chunks * window * n_cores * info.num_subcores
    n_buf = 2

    @functools.partial(pl.kernel, mesh=mesh, out_type=jax.ShapeDtypeStruct((TOP_K, n, d), ys.dtype),
                       scratch_types=[pltpu.VMEM((TOP_K, per_worker), I32)]
                       + [pltpu.VMEM((window, d), ys.dtype)] * n_buf + [pltpu.SemaphoreType.DMA] * (2 * n_buf),
                       name="sc_gather")
    def body(dest_hbm, ys_hbm, out_hbm, idx_v, buf0, buf1, gsem0, gsem1, ssem0, ssem1):
        bufs, gsem, ssem = (buf0, buf1), (gsem0, gsem1), (ssem0, ssem1)
        base = (lax.axis_index("s") * n_cores + lax.axis_index("c")) * per_worker
        for k in range(TOP_K):
            pltpu.sync_copy(dest_hbm.at[k, pl.ds(base, per_worker)], idx_v.at[k])

        def gather(c, k):
            return pltpu.make_async_copy(ys_hbm.at[idx_v.at[k, pl.ds(c * window, window)]], bufs[k % n_buf],
                                         gsem[k % n_buf])

        def store(c, k):
            return pltpu.make_async_copy(bufs[k % n_buf], out_hbm.at[k, pl.ds(base + c * window, window)],
                                         ssem[k % n_buf])

        gather(0, 0).start()

        @pl.loop(0, n_chunks)
        def _(c):
            for k in range(TOP_K):
                if k > 0:
                    store(c, k - 1).wait()
                    if k + 1 < TOP_K:
                        gather(c, k + 1).start()
                    else:
                        @pl.when(c + 1 < n_chunks)
                        def _():
                            gather(c + 1, 0).start()
                else:
                    @pl.when(c > 0)
                    def _():
                        store(c - 1, TOP_K - 1).wait()
                    gather(c, 1).start()
                gather(c, k).wait()
                store(c, k).start()

        store(n_chunks - 1, TOP_K - 1).wait()

    return body(dest_t, ys)


def _combine_dense_kernel(gate_ref, y_ref, lng_ref, lnb_ref, g_ref, o_ref):
    gates = gate_ref[...].T
    ffn_lo = ffn_hi = None
    for k in range(TOP_K):
        lo, hi = _unpack_bf16_pairs(g_ref[k])
        gate = gates[:, k:k + 1]
        ffn_lo = gate * lo if k == 0 else ffn_lo + gate * lo
        ffn_hi = gate * hi if k == 0 else ffn_hi + gate * hi
    ffn = jnp.concatenate([ffn_lo, ffn_hi], axis=1)
    o_ref[...] = _layer_norm(DEEPNORM_ALPHA * y_ref[...] + ffn, lng_ref[...], lnb_ref[...])


def _combine_dense(gates, y, ln_g, ln_b, gathered, tile=2 * ROW_TILE):
    n, d = y.shape
    tile = min(tile, n)
    row = lambda i: (i, 0)
    const = lambda i: (0, 0)
    return pl.pallas_call(
        _combine_dense_kernel,
        grid=(n // tile,),
        in_specs=[pl.BlockSpec((SUBLANES, tile), lambda i: (0, i)), pl.BlockSpec((tile, d), row),
                  pl.BlockSpec((1, d), const), pl.BlockSpec((1, d), const),
                  pl.BlockSpec((TOP_K, tile, d // 2), lambda i: (0, i, 0))],
        out_specs=pl.BlockSpec((tile, d), row),
        out_shape=jax.ShapeDtypeStruct((n, d), F32),
        compiler_params=_cparams("parallel"),
        name="combine_dense",
    )(gates, y, ln_g, ln_b, gathered)


MOE_ROW_BLOCK = 512


def _moe_ln2(y, y_packed, idx, gates, w_gate, b_gate, w_up, b_up, w_down, b_down, ln2_g, ln2_b):
    n, d = y.shape
    n_blocks = n * TOP_K // MOE_ROW_BLOCK + N_EXPERTS
    rows = n_blocks * MOE_ROW_BLOCK
    dest_t, block_table = _route(idx, MOE_ROW_BLOCK, n_blocks)
    block_e, n_valid, n_used = block_table[0, :n_blocks], block_table[1, :n_blocks], block_table[2, :1]
    next_e = block_table[3, :n_blocks]
    xs = _sc_dispatch(dest_t, y_packed, rows)
    ys = _experts(block_e, n_used, n_valid, next_e, xs, w_gate, b_gate.astype(F32), w_up, b_up.astype(F32),
                  w_down, b_down.astype(F32), MOE_ROW_BLOCK)
    gathered = _sc_gather(dest_t, ys, n)
    return _combine_dense(gates, y, ln2_g.reshape(1, d).astype(F32), ln2_b.reshape(1, d).astype(F32), gathered)


def kernel(x, w_in, sinks, w_proj_a, w_proj_b, w_out, ln1_g, ln1_b, router_w, router_b,
           w_gate, b_gate, w_up, b_up, w_down, b_down, ln2_g, ln2_b):
    bsz, seq, d = x.shape
    assert seq % (DIL_GROUPS[-1][1] * ATTN_BLOCK) == 0 and seq % IN_PROJ_TILE == 0
    assert w_in.shape[-1] == GATE_OFF + 2 * d and w_proj_b.shape[-2] == SWA_Q_WIDTH
    assert router_w.shape[-1] == N_EXPERTS and (bsz * seq * TOP_K) % MOE_ROW_BLOCK == 0
    h = x.reshape(bsz * seq, d)
    for l in range(w_in.shape[0]):
        y, y_packed, idx, gates = _token_mixer_ln1(h, w_in[l], sinks[l], w_proj_a[l], w_proj_b[l], w_out[l],
                                                   ln1_g[l], ln1_b[l], router_w[l], router_b[l], bsz, seq)
        h = _moe_ln2(y, y_packed, idx, gates, w_gate[l], b_gate[l], w_up[l], b_up[l], w_down[l], b_down[l],
                     ln2_g[l], ln2_b[l])
    return h.reshape(bsz, seq, d)
```
